```python
import math
import jax
import jax.numpy as jnp
from jax import lax
import numpy as np

D_MODEL = 1024
BATCH = 32
SEQ = 256
DEPTH = 4
DEC_BATCH = 2
DEC_SEQ = 1024
PAST_LEN = 256

GRID_W = 64
N_MIXERS = 4
N_RET = (DEPTH + 3) // 4
N_RWKV = (DEPTH + 2) // 4
N_DIFF = (DEPTH + 1) // 4
N_NA = DEPTH // 4

RET_HEADS = 4
RET_DK = D_MODEL // RET_HEADS
RET_DV = 2 * RET_DK
RET_QK = RET_HEADS * RET_DK
RET_V = RET_HEADS * RET_DV
RET_CHUNK = 64

RWKV_HD = 64
RWKV_HEADS = D_MODEL // RWKV_HD
RWKV_DECAY_RANK = 64
RWKV_A_RANK = 64

DIFF_HEADS = 8
DIFF_HD = D_MODEL // (2 * DIFF_HEADS)
DIFF_W = DIFF_HEADS * 2 * DIFF_HD

NA_HEADS = 16
NA_HD = D_MODEL // NA_HEADS
NA_WIN_R = 8
NA_WIN_C = 16

ROPE_BASE = 10000.0
Q_BLOCK = 128
EPS = 1e-6
GN_EPS = 1e-5

kernel_name = 'hybrid_ret_rwkv_diff_na_diffusion_step'


def rms_norm(x, w, eps=EPS):
    xf = x.astype(jnp.float32)
    y = xf * lax.rsqrt(jnp.mean(xf * xf, axis=-1, keepdims=True) + eps)
    return (y * w.astype(jnp.float32)).astype(x.dtype)


def head_layer_norm(x, w, eps=GN_EPS):
    xf = x.astype(jnp.float32)
    xc = xf - jnp.mean(xf, axis=-1, keepdims=True)
    return xc * lax.rsqrt(jnp.mean(xc * xc, axis=-1, keepdims=True) + eps) * w.astype(jnp.float32)


def split_heads(t, n_heads):
    b, l, _ = t.shape
    return t.reshape(b, l, n_heads, -1).transpose(0, 2, 1, 3)


def merge_heads(t):
    b, h, l, d = t.shape
    return t.transpose(0, 2, 1, 3).reshape(b, l, h * d)


def ada_modulation(cond, w, b):
    m = jax.nn.silu(cond) @ w + b
    shift, scale, gate = jnp.split(m[:, None, :], 3, axis=-1)
    return shift, scale, gate


def _rope_1d(x, pos):
    d = x.shape[-1]
    inv = ROPE_BASE ** (-jnp.arange(0, d, 2, dtype=jnp.float32) / d)
    ang = pos[:, None] * inv[None, :]
    cos, sin = jnp.cos(ang), jnp.sin(ang)
    x1, x2 = jnp.split(x, 2, axis=-1)
    return jnp.concatenate([x1 * cos - x2 * sin, x1 * sin + x2 * cos], axis=-1)


def axial_rope(x):
    L = x.shape[-2]
    t = jnp.arange(L)
    row = (t // GRID_W).astype(jnp.float32)
    col = (t % GRID_W).astype(jnp.float32)
    xr, xc = jnp.split(x.astype(jnp.float32), 2, axis=-1)
    return jnp.concatenate([_rope_1d(xr, row), _rope_1d(xc, col)], axis=-1).astype(x.dtype)


def centred_shift(x):
    p = jnp.pad(x, ((0, 0), (1, 1), (0, 0)))
    return 0.5 * (p[:, :-2] + p[:, 2:])


def query_blocks(fn, q):
    b, h, l, d = q.shape
    n = l // Q_BLOCK
    qb = q.reshape(b, h, n, Q_BLOCK, d).transpose(2, 0, 1, 3, 4)
    o = lax.map(fn, qb)
    return o.transpose(1, 2, 0, 3, 4).reshape(b, h, l, o.shape[-1])


def dense_attend(q, k, v):
    scale = q.shape[-1] ** -0.5
    def block(qb):
        s = jnp.einsum('bhqd,bhkd->bhqk', qb, k).astype(jnp.float32) * scale
        p = jax.nn.softmax(s, axis=-1).astype(v.dtype)
        return jnp.einsum('bhqk,bhkd->bhqd', p, v)
    return query_blocks(block, q)


def retention_chunkwise(q, k, v, log_g, s0):
    b, h, l, dk = q.shape
    dv = v.shape[-1]
    n = l // RET_CHUNK
    def chunks(t):
        return t.astype(jnp.float32).reshape(b, h, n, RET_CHUNK, t.shape[-1]).transpose(2, 0, 1, 3, 4)
    pos = jnp.arange(RET_CHUNK, dtype=jnp.float32)
    lg = log_g[:, None]
    gap = pos[:, None] - pos[None, :]
    inner_decay = jnp.where(gap >= 0, jnp.exp(lg[:, :, None] * jnp.maximum(gap, 0.0)), 0.0)
    q_decay = jnp.exp(lg * (pos + 1.0))[:, :, None]
    k_decay = jnp.exp(lg * (RET_CHUNK - 1.0 - pos))[:, :, None]
    chunk_decay = jnp.exp(log_g * RET_CHUNK)[:, None, None]
    def step(s, qkv):
        qc, kc, vc = qkv
        att = jnp.einsum('bhqd,bhkd->bhqk', qc, kc) * inner_decay
        o = jnp.einsum('bhqk,bhkv->bhqv', att, vc) + jnp.einsum('bhqd,bhdv->bhqv', qc * q_decay, s)
        s = s * chunk_decay + jnp.einsum('bhkd,bhkv->bhdv', kc * k_decay, vc)
        return s, o
    s, o = lax.scan(step, s0.astype(jnp.float32), (chunks(q), chunks(k), chunks(v)))
    return o.transpose(1, 2, 0, 3, 4).reshape(b, h, l, dv), s


def retention_mixer(h, s0, w_in, decay_logit, gn_w, w_out, latent):
    q, k, v, g = jnp.split(h @ w_in, [RET_QK, 2 * RET_QK, 2 * RET_QK + RET_V], axis=-1)
    q = split_heads(q, RET_HEADS)
    k = split_heads(k, RET_HEADS)
    v = split_heads(v, RET_HEADS)
    if latent:
        q, k = axial_rope(q), axial_rope(k)
    k = k * (RET_DK ** -0.5)
    log_g = jax.nn.log_sigmoid(decay_logit.astype(jnp.float32))
    o_f, s_f = retention_chunkwise(q, k, v, log_g[0], s0[:, 0])
    flip = lambda t: jnp.flip(t, axis=2)
    o_b, s_b = retention_chunkwise(flip(q), flip(k), flip(v), log_g[1], s0[:, 1])
    o = head_layer_norm(o_f + flip(o_b), gn_w.reshape(RET_HEADS, 1, RET_DV))
    o = merge_heads(o).astype(h.dtype) * jax.nn.silu(g)
    return o @ w_out, jnp.stack([s_f, s_b], axis=1)


def rwkv7_scan(r, w, k, v, kk, a, s0):
    def step(s, inp):
        r_t, w_t, k_t, v_t, kk_t, a_t = inp
        sa = jnp.einsum('bhvk,bhk->bhv', s, -kk_t)
        s = (s * w_t[:, :, None, :] + sa[..., None] * (kk_t * a_t)[:, :, None, :]
             + v_t[..., None] * k_t[:, :, None, :])
        return s, jnp.einsum('bhvk,bhk->bhv', s, r_t)
    seq = tuple(jnp.moveaxis(t, 1, 0) for t in (r, w, k, v, kk, a))
    s, y = lax.scan(step, s0.astype(jnp.float32), seq)
    return jnp.moveaxis(y, 0, 1), s


def rwkv7_mixer(h, s0, mu, w_in, w0, wA, wB, a0, aA, aB, k_k, k_a, r_k, gn_w, w_out):
    b, l, d_model = h.shape
    H, hd = RWKV_HEADS, RWKV_HD
    xx = centred_shift(h) - h
    x_r, x_w, x_k, x_v, x_a, x_g = [h + xx * mu[n] for n in range(6)]
    r, k, v, g = jnp.einsum('nbld,dne->nble', jnp.stack([x_r, x_k, x_v, x_g]), w_in.reshape(d_model, 4, d_model))
    to_heads = lambda t: t.astype(jnp.float32).reshape(b, l, H, hd)
    r, k, v = to_heads(r), to_heads(k), to_heads(v)
    kk = k * k_k.astype(jnp.float32).reshape(H, hd)
    kk = kk * lax.rsqrt(jnp.maximum(jnp.sum(kk * kk, axis=-1, keepdims=True), 1e-12))
    ys, bonuses, states = [], [], []
    for dr in range(2):
        wlog = -jax.nn.softplus(-(w0[dr] + jnp.tanh(x_w @ wA[dr]) @ wB[dr])) - 0.5
        decay = to_heads(jnp.exp(-jnp.exp(wlog.astype(jnp.float32))))
        a = to_heads(jax.nn.sigmoid(a0[dr] + (x_a @ aA[dr]) @ aB[dr]))
        kd = k * (1.0 + (a - 1.0) * k_a.astype(jnp.float32).reshape(H, hd))
        seq = (r, decay, kd, v, kk, a)
        if dr == 1:
            seq = tuple(jnp.flip(t, axis=1) for t in seq)
        y_d, s_d = rwkv7_scan(*seq, s0[:, dr])
        if dr == 1:
            y_d = jnp.flip(y_d, axis=1)
        ys.append(y_d)
        bonuses.append(jnp.sum(r * kd * r_k.astype(jnp.float32), axis=-1, keepdims=True) * v)
        states.append(s_d)
    o = head_layer_norm(ys[0] + ys[1], gn_w.reshape(H, hd)) + bonuses[0] + bonuses[1]
    o = o.reshape(b, l, d_model).astype(h.dtype) * jax.nn.silu(g)
    return o @ w_out, jnp.stack(states, axis=1)


def diff_project(h, w_in):
    q, k, v, g = jnp.split(h @ w_in, 4, axis=-1)
    return split_heads(q, DIFF_HEADS), split_heads(k, DIFF_HEADS), split_heads(v, DIFF_HEADS), g


def diff_lambda_value(lam_p, lam_init):
    lp = lam_p.astype(jnp.float32)
    return jnp.exp(jnp.sum(lp[0] * lp[1])) - jnp.exp(jnp.sum(lp[2] * lp[3])) + lam_init


def rope_pair(t):
    t1, t2 = jnp.split(t, 2, axis=-1)
    return jnp.concatenate([axial_rope(t1), axial_rope(t2)], axis=-1)


def diff_attend(q, k, v, lam):
    scale = DIFF_HD ** -0.5
    k1, k2 = jnp.split(k, 2, axis=-1)
    def block(qb):
        q1, q2 = jnp.split(qb, 2, axis=-1)
        p1 = jax.nn.softmax(jnp.einsum('bhqd,bhkd->bhqk', q1, k1).astype(jnp.float32) * scale, axis=-1)
        p2 = jax.nn.softmax(jnp.einsum('bhqd,bhkd->bhqk', q2, k2).astype(jnp.float32) * scale, axis=-1)
        return jnp.einsum('bhqk,bhkv->bhqv', (p1 - lam * p2).astype(v.dtype), v)
    return query_blocks(block, q)


def diff_finish(o, g, gn_w, lam_init, w_out):
    o = rms_norm(o, gn_w.reshape(DIFF_HEADS, 1, 2 * DIFF_HD)) * (1.0 - lam_init)
    return (merge_heads(o) * jax.nn.silu(g)) @ w_out


def na_project(h, w_in):
    q, k, v, g = jnp.split(h @ w_in, 4, axis=-1)
    return split_heads(q, NA_HEADS), split_heads(k, NA_HEADS), split_heads(v, NA_HEADS), g


def na_finish(o, g, w_out):
    return (merge_heads(o) * jax.nn.silu(g)) @ w_out


def neighbourhood_attend(q, k, v, k_ctx, v_ctx, bias_table):
    b, h, l, d = q.shape
    rows = l // GRID_W
    wr = min(NA_WIN_R, rows)
    wc = NA_WIN_C
    scale = d ** -0.5
    r = jnp.arange(rows)
    row_idx = jnp.clip(r - wr // 2, 0, rows - wr)[:, None] + jnp.arange(wr)[None, :]
    c = jnp.arange(GRID_W)
    cs = jnp.clip(c - wc // 2, 0, GRID_W - wc)
    col_ok = (c[None, :] >= cs[:, None]) & (c[None, :] < cs[:, None] + wc)
    row_off = row_idx - r[:, None] + (NA_WIN_R - 1)
    col_off = jnp.clip(c[None, :] - c[:, None], -(wc - 1), wc - 1) + (NA_WIN_C - 1)
    bias = bias_table.astype(jnp.float32)[:, row_off[:, None, :, None], col_off[None, :, None, :]]
    qg = q.reshape(b, h, rows, GRID_W, d)
    kg = k.reshape(b, h, rows, GRID_W, d)[:, :, row_idx]
    vg = v.reshape(b, h, rows, GRID_W, d)[:, :, row_idx]
    s_loc = jnp.einsum('bhrqd,bhrwkd->bhrqwk', qg, kg).astype(jnp.float32) * scale + bias
    s_loc = jnp.where(col_ok[:, None, :], s_loc, -jnp.inf)
    k_ctx = k_ctx.astype(q.dtype)
    v_ctx = v_ctx.astype(v.dtype)
    s_ctx = jnp.einsum('bhrqd,bhcd->bhrqc', qg, k_ctx).astype(jnp.float32) * scale
    n_loc = wr * GRID_W
    p = jax.nn.softmax(jnp.concatenate([s_loc.reshape(b, h, rows, GRID_W, n_loc), s_ctx], axis=-1), axis=-1).astype(v.dtype)
    p_loc = p[..., :n_loc].reshape(b, h, rows, GRID_W, wr, GRID_W)
    p_ctx = p[..., n_loc:]
    o = jnp.einsum('bhrqwk,bhrwkd->bhrqd', p_loc, vg) + jnp.einsum('bhrqc,bhcd->bhrqd', p_ctx, v_ctx)
    return o.reshape(b, h, l, d)


def setup_inputs(seed: int = 0) -> dict:
    key = jax.random.key(seed)
    keys = iter(jax.random.split(key, 48))
    def nrm(shape, scale=1.0):
        return jax.random.normal(next(keys), shape, jnp.float32) * scale
    def gain(shape):
        return 1.0 + nrm(shape, 0.02)
    D = D_MODEL
    ret_gamma_logit = jnp.log(2.0 ** (5.0 + jnp.arange(RET_HEADS, dtype=jnp.float32)) - 1.0)
    rwkv_w0_base = jnp.linspace(-6.0, 1.0, D, dtype=jnp.float32)
    return {
        'x_prompt': nrm((BATCH, SEQ, D)),
        'x_sample': nrm((DEC_BATCH, DEC_SEQ, D)),
        'state_ret': nrm((DEC_BATCH, N_RET, 2, RET_HEADS, RET_DK, RET_DV), 0.1),
        'state_rwkv': nrm((DEC_BATCH, N_RWKV, 2, RWKV_HEADS, RWKV_HD, RWKV_HD), 0.1),
        'cache_diff_k': nrm((DEC_BATCH, N_DIFF, DIFF_HEADS, PAST_LEN, 2 * DIFF_HD)),
        'cache_diff_v': nrm((DEC_BATCH, N_DIFF, DIFF_HEADS, PAST_LEN, 2 * DIFF_HD)),
        'cache_na_k': nrm((DEC_BATCH, N_NA, NA_HEADS, PAST_LEN, NA_HD)),
        'cache_na_v': nrm((DEC_BATCH, N_NA, NA_HEADS, PAST_LEN, NA_HD)),
        'c': nrm((DEC_BATCH, D)),
        'c_ctx': nrm((D,)),
        'norm_w': gain((DEPTH, D)),
        'w_mod': nrm((DEPTH, D, 3 * D), 0.5 * D ** -0.5),
        'b_mod': nrm((DEPTH, 3 * D), 0.1),
        'final_norm_w': gain((D,)),
        'ret_w_in': nrm((N_RET, D, 2 * RET_QK + 2 * RET_V), D ** -0.5),
        'ret_decay': ret_gamma_logit + nrm((N_RET, 2, RET_HEADS), 0.1),
        'ret_gn': gain((N_RET, RET_V)),
        'ret_w_out': nrm((N_RET, RET_V, D), RET_V ** -0.5),
        'rwkv_mu': jax.random.uniform(next(keys), (N_RWKV, 6, D), jnp.float32),
        'rwkv_w_in': nrm((N_RWKV, D, 4 * D), D ** -0.5),
        'rwkv_w0': rwkv_w0_base + nrm((N_RWKV, 2, D), 0.1),
        'rwkv_wA': nrm((N_RWKV, 2, D, RWKV_DECAY_RANK), D ** -0.5),
        'rwkv_wB': nrm((N_RWKV, 2, RWKV_DECAY_RANK, D), 0.5 * RWKV_DECAY_RANK ** -0.5),
        'rwkv_a0': nrm((N_RWKV, 2, D), 0.1),
        'rwkv_aA': nrm((N_RWKV, 2, D, RWKV_A_RANK), D ** -0.5),
        'rwkv_aB': nrm((N_RWKV, 2, RWKV_A_RANK, D), 0.5 * RWKV_A_RANK ** -0.5),
        'rwkv_kk': 0.85 + nrm((N_RWKV, D), 0.02),
        'rwkv_ka': gain((N_RWKV, D)),
        'rwkv_rk': nrm((N_RWKV, RWKV_HEADS, RWKV_HD), 0.1),
        'rwkv_gn': gain((N_RWKV, D)),
        'rwkv_w_out': nrm((N_RWKV, D, D), D ** -0.5),
        'diff_w_in': nrm((N_DIFF, D, 4 * DIFF_W), D ** -0.5),
        'diff_lambda': nrm((N_DIFF, 4, DIFF_HD), 0.1),
        'diff_gn': gain((N_DIFF, DIFF_W)),
        'diff_w_out': nrm((N_DIFF, DIFF_W, D), DIFF_W ** -0.5),
        'na_w_in': nrm((N_NA, D, 4 * D), D ** -0.5),
        'na_bias': nrm((N_NA, NA_HEADS, 2 * NA_WIN_R - 1, 2 * NA_WIN_C - 1), 0.1),
        'na_w_out': nrm((N_NA, D, D), D ** -0.5),
    }


def reference(x_prompt, x_sample, state_ret, state_rwkv, cache_diff_k, cache_diff_v, cache_na_k, cache_na_v,
              c, c_ctx, norm_w, w_mod, b_mod, final_norm_w,
              ret_w_in, ret_decay, ret_gn, ret_w_out,
              rwkv_mu, rwkv_w_in, rwkv_w0, rwkv_wA, rwkv_wB, rwkv_a0, rwkv_aA, rwkv_aB,
              rwkv_kk, rwkv_ka, rwkv_rk, rwkv_gn, rwkv_w_out,
              diff_w_in, diff_lambda, diff_gn, diff_w_out,
              na_w_in, na_bias, na_w_out):
    xp, xs = x_prompt, x_sample
    bp = xp.shape[0]
    new_ret, new_rwkv, new_dk, new_dv, new_nk, new_nv = [], [], [], [], [], []
    for i in range(DEPTH):
        kind, j = i % N_MIXERS, i // N_MIXERS
        sh_p, sc_p, g_p = ada_modulation(c_ctx[None, :], w_mod[i], b_mod[i])
        sh_s, sc_s, g_s = ada_modulation(c, w_mod[i], b_mod[i])
        hp = rms_norm(xp, norm_w[i]) * (1.0 + sc_p) + sh_p
        hs = rms_norm(xs, norm_w[i]) * (1.0 + sc_s) + sh_s
        if kind == 0:
            s0 = jnp.zeros((bp, 2, RET_HEADS, RET_DK, RET_DV), jnp.float32)
            yp, st = retention_mixer(hp, s0, ret_w_in[j], ret_decay[j], ret_gn[j], ret_w_out[j], False)
            ys, _ = retention_mixer(hs, state_ret[:, j], ret_w_in[j], ret_decay[j], ret_gn[j], ret_w_out[j], True)
            new_ret.append(st)
        elif kind == 1:
            rw = (rwkv_mu[j], rwkv_w_in[j], rwkv_w0[j], rwkv_wA[j], rwkv_wB[j], rwkv_a0[j], rwkv_aA[j],
                  rwkv_aB[j], rwkv_kk[j], rwkv_ka[j], rwkv_rk[j], rwkv_gn[j], rwkv_w_out[j])
            s0 = jnp.zeros((bp, 2, RWKV_HEADS, RWKV_HD, RWKV_HD), jnp.float32)
            yp, st = rwkv7_mixer(hp, s0, *rw)
            ys, _ = rwkv7_mixer(hs, state_rwkv[:, j], *rw)
            new_rwkv.append(st)
        elif kind == 2:
            lam_init = 0.8 - 0.6 * math.exp(-0.3 * i)
            lam = diff_lambda_value(diff_lambda[j], lam_init)
            qp, kp, vp, gp = diff_project(hp, diff_w_in[j])
            yp = diff_finish(diff_attend(qp, kp, vp, lam), gp, diff_gn[j], lam_init, diff_w_out[j])
            qs, ks_, vs, gs = diff_project(hs, diff_w_in[j])
            k_all = jnp.concatenate([rope_pair(ks_), cache_diff_k[:, j].astype(ks_.dtype)], axis=2)
            v_all = jnp.concatenate([vs, cache_diff_v[:, j].astype(vs.dtype)], axis=2)
            ys = diff_finish(diff_attend(rope_pair(qs), k_all, v_all, lam), gs, diff_gn[j], lam_init, diff_w_out[j])
            new_dk.append(kp)
            new_dv.append(vp)
        else:
            qp, kp, vp, gp = na_project(hp, na_w_in[j])
            yp = na_finish(dense_attend(qp, kp, vp), gp, na_w_out[j])
            qs, ks_, vs, gs = na_project(hs, na_w_in[j])
            o_s = neighbourhood_attend(qs, ks_, vs, cache_na_k[:, j], cache_na_v[:, j], na_bias[j])
            ys = na_finish(o_s, gs, na_w_out[j])
            new_nk.append(kp)
            new_nv.append(vp)
        xp = xp + g_p * yp
        xs = xs + g_s * ys
    y_prompt = rms_norm(xp, final_norm_w)
    y_sample = rms_norm(xs, final_norm_w)
    return (y_prompt, y_sample, jnp.stack(new_ret, axis=1), jnp.stack(new_rwkv, axis=1),
            jnp.stack(new_dk, axis=1), jnp.stack(new_dv, axis=1),
            jnp.stack(new_nk, axis=1), jnp.stack(new_nv, axis=1))
```

```python
import functools
import math

import jax
import jax.numpy as jnp
from jax import lax
from jax.experimental import pallas as pl
from jax.experimental.pallas import tpu as pltpu

F32 = jnp.float32
BF16 = jnp.bfloat16

D_MODEL = 1024
BATCH = 32
SEQ = 256
DEPTH = 4
N_MIXERS = 4
DEC_BATCH = 2
DEC_SEQ = 1024
PAST_LEN = 256
GRID_W = 64

RET_HEADS = 4
RET_DK = 256
RET_DV = 512
RET_QK = 1024
RET_V = 2048

RWKV_HD = 64
RWKV_HEADS = 16
RWKV_RANK = 64

DIFF_HEADS = 8
DIFF_HD = 64

NA_HEADS = 16
NA_HD = 64
NA_WIN_R = 8
NA_WIN_C = 16

ROPE_BASE = 10000.0
EPS = 1e-6
GN_EPS = 1e-5

N_PROMPT_TOK = BATCH * SEQ
N_SAMPLE_TOK = DEC_BATCH * DEC_SEQ
N_TOK = N_PROMPT_TOK + N_SAMPLE_TOK
N_COND = 8

LANES = 128
VMEM_LIMIT = 56 * 2 ** 20


def _params(*sem):
    return pltpu.CompilerParams(dimension_semantics=sem, vmem_limit_bytes=VMEM_LIMIT)


def _cond_of_tile(i, tm):
    npt = N_PROMPT_TOK // tm
    return jnp.where(i < npt, 0, 1 + (i - npt) // (DEC_SEQ // tm))


def _sigmoid(x):
    return 1.0 / (1.0 + jnp.exp(-x))


def _silu(x):
    return x * _sigmoid(x)


def _dot(a, b):
    return jnp.dot(a, b, preferred_element_type=F32)


def _dot_nt(a, b):
    return lax.dot_general(a, b, (((1,), (1,)), ((), ())), preferred_element_type=F32)


def _dot_tn(a, b):
    return lax.dot_general(a, b, (((0,), (0,)), ((), ())), preferred_element_type=F32)


def _softmax_rows(s):
    m = jnp.max(s, axis=-1, keepdims=True)
    e = jnp.exp(s - m)
    return e / jnp.sum(e, axis=-1, keepdims=True)


def _mod_kernel(c_ref, w_ref, b_ref, o_ref):
    s = _silu(c_ref[...])
    o_ref[0] = jnp.dot(s, w_ref[0], precision=lax.Precision.HIGHEST, preferred_element_type=F32) + b_ref[0]


def _modulation(cond, w_mod, b_mod):
    tn = D_MODEL
    out = pl.pallas_call(
        _mod_kernel,
        grid=(DEPTH, 3 * D_MODEL // tn),
        in_specs=[
            pl.BlockSpec((N_COND, D_MODEL), lambda l, j: (0, 0)),
            pl.BlockSpec((1, D_MODEL, tn), lambda l, j: (l, 0, j)),
            pl.BlockSpec((1, 1, tn), lambda l, j: (l, 0, j)),
        ],
        out_specs=pl.BlockSpec((1, N_COND, tn), lambda l, j: (l, 0, j)),
        out_shape=jax.ShapeDtypeStruct((DEPTH, N_COND, 3 * D_MODEL), F32),
        compiler_params=_params("arbitrary", "arbitrary"),
        name="modulation",
    )(cond, w_mod, b_mod.reshape(DEPTH, 1, 3 * D_MODEL))
    return out.reshape(DEPTH, N_COND, 3, 1, D_MODEL)


def _norm_mod(x, nw, mod_ref):
    ms = jnp.mean(x * x, axis=-1, keepdims=True)
    y = x * lax.rsqrt(ms + EPS) * nw
    return y * (1.0 + mod_ref[0, 1]) + mod_ref[0, 0]


IN_TM = 1024


def _in_proj_kernel(x_ref, nw_ref, mod_ref, w_ref, o_ref, h_ref):
    @pl.when(pl.program_id(1) == 0)
    def _():
        h_ref[...] = _norm_mod(x_ref[...], nw_ref[...], mod_ref).astype(BF16)

    o_ref[...] = _dot(h_ref[...], w_ref[...])


def _in_proj(x, norm_w, mod, w, tn):
    n = w.shape[1]
    return pl.pallas_call(
        _in_proj_kernel,
        grid=(N_TOK // IN_TM, n // tn),
        in_specs=[
            pl.BlockSpec((IN_TM, D_MODEL), lambda i, j: (i, 0)),
            pl.BlockSpec((1, D_MODEL), lambda i, j: (0, 0)),
            pl.BlockSpec((1, 3, 1, D_MODEL), lambda i, j: (_cond_of_tile(i, IN_TM), 0, 0, 0)),
            pl.BlockSpec((D_MODEL, tn), lambda i, j: (0, j)),
        ],
        out_specs=pl.BlockSpec((IN_TM, tn), lambda i, j: (i, j)),
        out_shape=jax.ShapeDtypeStruct((N_TOK, n), F32),
        scratch_shapes=[pltpu.VMEM((IN_TM, D_MODEL), BF16)],
        compiler_params=_params("arbitrary", "arbitrary"),
        name="in_proj",
    )(x, norm_w.reshape(1, D_MODEL), mod, w)


OUT_TM = 256


def _out_proj_kernel(o_ref, g_ref, w_ref, x_ref, mod_ref, fw_ref, y_ref, *, final):
    a = (o_ref[...] * _silu(g_ref[...])).astype(BF16)
    xn = x_ref[...] + mod_ref[0, 2] * _dot(a, w_ref[...])
    if final:
        ms = jnp.mean(xn * xn, axis=-1, keepdims=True)
        xn = xn * lax.rsqrt(ms + EPS) * fw_ref[...]
    y_ref[...] = xn


def _out_proj(o, g_arr, g_blk, w, x, mod, final_w, final):
    k = w.shape[0]
    return pl.pallas_call(
        functools.partial(_out_proj_kernel, final=final),
        grid=(N_TOK // OUT_TM,),
        in_specs=[
            pl.BlockSpec((OUT_TM, k), lambda i: (i, 0)),
            pl.BlockSpec((OUT_TM, k), lambda i: (i, g_blk)),
            pl.BlockSpec((k, D_MODEL), lambda i: (0, 0)),
            pl.BlockSpec((OUT_TM, D_MODEL), lambda i: (i, 0)),
            pl.BlockSpec((1, 3, 1, D_MODEL), lambda i: (_cond_of_tile(i, OUT_TM), 0, 0, 0)),
            pl.BlockSpec((1, D_MODEL), lambda i: (0, 0)),
        ],
        out_specs=pl.BlockSpec((OUT_TM, D_MODEL), lambda i: (i, 0)),
        out_shape=jax.ShapeDtypeStruct((N_TOK, D_MODEL), F32),
        compiler_params=_params("arbitrary"),
        name="out_proj",
    )(o, g_arr, w, x, mod, final_w.reshape(1, D_MODEL))


def _rope_tables(d):
    q = d // 4
    t = jnp.arange(DEC_SEQ)
    row = (t // GRID_W).astype(F32)
    col = (t % GRID_W).astype(F32)
    inv = ROPE_BASE ** (-jnp.arange(0, 2 * q, 2, dtype=F32) / (2 * q))
    ar = row[:, None] * inv[None, :]
    ac = col[:, None] * inv[None, :]
    z = jnp.zeros_like(ar)
    cos = jnp.concatenate([jnp.cos(ar), jnp.cos(ar), jnp.cos(ac), jnp.cos(ac)], axis=-1)
    sin_lo = jnp.concatenate([-jnp.sin(ar), z, -jnp.sin(ac), z], axis=-1)
    sin_hi = jnp.concatenate([z, jnp.sin(ar), z, jnp.sin(ac)], axis=-1)
    return cos, sin_lo, sin_hi


def _rope(x, cos, sin_lo, sin_hi, q):
    w = x.shape[-1]
    x_next = pltpu.roll(x, w - q, axis=1)
    x_prev = pltpu.roll(x, q, axis=1)
    return x * cos + x_next * sin_lo + x_prev * sin_hi


RET_QB = 256


def _ret_kernel(lg_ref, q_ref, k_ref, v_ref, gn_ref, *rest, seq, latent):
    if latent:
        cos_ref, slo_ref, shi_ref, s0_ref, _prev_ref, o_ref = rest
    else:
        o_ref, st_ref = rest
    h = pl.program_id(1)
    lgf = lg_ref[0, h]
    lgb = lg_ref[1, h]
    q = q_ref[...]
    k = k_ref[...]
    if latent:
        q = _rope(q, cos_ref[...], slo_ref[...], shi_ref[...], RET_DK // 4)
        k = _rope(k, cos_ref[...], slo_ref[...], shi_ref[...], RET_DK // 4)
    k = k * (RET_DK ** -0.5)
    kb = k.astype(BF16)
    vb = v_ref[...].astype(BF16)
    gn = gn_ref[...]
    for qi in range(seq // RET_QB):
        qblk = q[qi * RET_QB:(qi + 1) * RET_QB]
        s = _dot_nt(qblk.astype(BF16), kb)
        ii = lax.broadcasted_iota(jnp.int32, (RET_QB, seq), 0) + qi * RET_QB
        jj = lax.broadcasted_iota(jnp.int32, (RET_QB, seq), 1)
        gap = (ii - jj).astype(F32)
        dec = (jnp.where(gap >= 0, jnp.exp(lgf * jnp.maximum(gap, 0.0)), 0.0)
               + jnp.where(gap <= 0, jnp.exp(lgb * jnp.maximum(-gap, 0.0)), 0.0))
        o = _dot((s * dec).astype(BF16), vb)
        if latent:
            pos = (lax.broadcasted_iota(jnp.int32, (RET_QB, 1), 0) + qi * RET_QB).astype(F32)
            qf = qblk * jnp.exp(lgf * (pos + 1.0))
            qr = qblk * jnp.exp(lgb * (seq - pos))
            o = o + _dot(qf.astype(BF16), s0_ref[0, 0, 0, 0].astype(BF16))
            o = o + _dot(qr.astype(BF16), s0_ref[0, 0, 1, 0].astype(BF16))
        oc = o - jnp.mean(o, axis=-1, keepdims=True)
        o = oc * lax.rsqrt(jnp.mean(oc * oc, axis=-1, keepdims=True) + GN_EPS) * gn
        o_ref[qi * RET_QB:(qi + 1) * RET_QB, :] = o
    if not latent:
        pos = lax.broadcasted_iota(jnp.int32, (seq, 1), 0).astype(F32)
        kf = k * jnp.exp(lgf * (seq - 1.0 - pos))
        kr = k * jnp.exp(lgb * pos)
        st_ref[0, 0, 0, 0] = _dot_tn(kf.astype(BF16), vb)
        st_ref[0, 0, 1, 0] = _dot_tn(kr.astype(BF16), vb)


def _retention(p, log_g, gn_w, state_ret, j):
    smem = pl.BlockSpec(memory_space=pltpu.SMEM)
    gn = gn_w.reshape(1, RET_V)
    kq = RET_QK // RET_DK
    o_p, st = pl.pallas_call(
        functools.partial(_ret_kernel, seq=SEQ, latent=False),
        grid=(BATCH, RET_HEADS),
        in_specs=[
            smem,
            pl.BlockSpec((SEQ, RET_DK), lambda b, h: (b, h)),
            pl.BlockSpec((SEQ, RET_DK), lambda b, h: (b, kq + h)),
            pl.BlockSpec((SEQ, RET_DV), lambda b, h: (b, kq + h)),
            pl.BlockSpec((1, RET_DV), lambda b, h: (0, h)),
        ],
        out_specs=[
            pl.BlockSpec((SEQ, RET_DV), lambda b, h: (b, h)),
            pl.BlockSpec((1, 1, 2, 1, RET_DK, RET_DV), lambda b, h: (b, 0, 0, h, 0, 0)),
        ],
        out_shape=[
            jax.ShapeDtypeStruct((N_TOK, RET_V), F32),
            jax.ShapeDtypeStruct((BATCH, 1, 2, RET_HEADS, RET_DK, RET_DV), F32),
        ],
        compiler_params=_params("arbitrary", "arbitrary"),
        name="retention_prompt",
    )(log_g, p, p, p, gn)
    cos, slo, shi = _rope_tables(RET_DK)
    rb = N_PROMPT_TOK // DEC_SEQ
    full = pl.BlockSpec((DEC_SEQ, RET_DK), lambda b, h: (0, 0))
    o = pl.pallas_call(
        functools.partial(_ret_kernel, seq=DEC_SEQ, latent=True),
        grid=(DEC_BATCH, RET_HEADS),
        in_specs=[
            smem,
            pl.BlockSpec((DEC_SEQ, RET_DK), lambda b, h: (rb + b, h)),
            pl.BlockSpec((DEC_SEQ, RET_DK), lambda b, h: (rb + b, kq + h)),
            pl.BlockSpec((DEC_SEQ, RET_DV), lambda b, h: (rb + b, kq + h)),
            pl.BlockSpec((1, RET_DV), lambda b, h: (0, h)),
            full, full, full,
            pl.BlockSpec((1, 1, 2, 1, RET_DK, RET_DV), lambda b, h: (b, j, 0, h, 0, 0)),
            pl.BlockSpec(memory_space=pl.ANY),
        ],
        out_specs=pl.BlockSpec((DEC_SEQ, RET_DV), lambda b, h: (rb + b, h)),
        out_shape=jax.ShapeDtypeStruct((N_TOK, RET_V), F32),
        input_output_aliases={9: 0},
        compiler_params=_params("arbitrary", "arbitrary"),
        name="retention_latent",
    )(log_g, p, p, p, gn, cos, slo, shi, state_ret, o_p)
    return o, st


def _layer_ret(x, p, mod, j):
    i = N_MIXERS * j + 0
    proj = _in_proj(x, p['norm_w'][i], mod, p['ret_w_in'][j].astype(BF16), 1024)
    log_g = jax.nn.log_sigmoid(p['ret_decay'][j].astype(F32))
    o, st = _retention(proj, log_g, p['ret_gn'][j], p['state_ret'], j)
    x = _out_proj(o, proj, (2 * RET_QK + RET_V) // RET_V, p['ret_w_out'][j].astype(BF16), x, mod,
                  p['final_norm_w'], False)
    return x, st


RW_TM = 512
RW_HALO = 8
RW_TB = 16
RW_TT = 512


def _rwkv_prep_kernel(x_ref, xp_ref, xn_ref, nw_ref, mod_ref, mu_ref, wa_ref, aa_ref, wb_ref, ab_ref,
                      w0_ref, a0_ref, xm_ref, dec_ref, a_ref):
    i = pl.program_id(0)
    nw = nw_ref[...]
    h = _norm_mod(x_ref[...], nw, mod_ref)
    h_before = _norm_mod(xp_ref[RW_HALO - 1:RW_HALO, :], nw, mod_ref)
    h_after = _norm_mod(xn_ref[0:1, :], nw, mod_ref)
    seq = jnp.where(i < N_PROMPT_TOK // RW_TM, SEQ, DEC_SEQ)
    row = lax.broadcasted_iota(jnp.int32, (RW_TM, 1), 0)
    t = (row + i * RW_TM) & (seq - 1)
    prev = jnp.where(row == 0, h_before, pltpu.roll(h, 1, axis=0))
    nxt = jnp.where(row == RW_TM - 1, h_after, pltpu.roll(h, RW_TM - 1, axis=0))
    prev = jnp.where(t == 0, 0.0, prev)
    nxt = jnp.where(t == seq - 1, 0.0, nxt)
    xx = 0.5 * (prev + nxt) - h
    for n, m in enumerate((0, 2, 3, 5)):
        xm_ref[n] = (h + xx * mu_ref[m:m + 1, :]).astype(BF16)
    xw = (h + xx * mu_ref[1:2, :]).astype(BF16)
    xa = (h + xx * mu_ref[4:5, :]).astype(BF16)
    lw = jnp.tanh(_dot(xw, wa_ref[...])).astype(BF16)
    la = _dot(xa, aa_ref[...]).astype(BF16)
    for dr in range(2):
        wl = w0_ref[dr:dr + 1, :] + _dot(lw, wb_ref[dr])
        dec_ref[dr] = jnp.exp(-math.exp(-0.5) * _sigmoid(wl))
        a_ref[dr] = _sigmoid(a0_ref[dr:dr + 1, :] + _dot(la, ab_ref[dr]))


def _rwkv_prep(x, norm_w, mod, mu, wa2, aa2, wb_pad, ab_pad, w0, a0):
    nt = N_TOK // RW_TM
    hb = RW_TM // RW_HALO
    last = N_TOK // RW_HALO - 1
    full2 = lambda shape: pl.BlockSpec(shape, lambda i: (0, 0))
    full3 = lambda shape: pl.BlockSpec(shape, lambda i: (0, 0, 0))
    return pl.pallas_call(
        _rwkv_prep_kernel,
        grid=(nt,),
        in_specs=[
            pl.BlockSpec((RW_TM, D_MODEL), lambda i: (i, 0)),
            pl.BlockSpec((RW_HALO, D_MODEL), lambda i: (jnp.maximum(i * hb - 1, 0), 0)),
            pl.BlockSpec((RW_HALO, D_MODEL), lambda i: (jnp.minimum((i + 1) * hb, last), 0)),
            full2((1, D_MODEL)),
            pl.BlockSpec((1, 3, 1, D_MODEL), lambda i: (_cond_of_tile(i, RW_TM), 0, 0, 0)),
            full2((6, D_MODEL)),
            full2((D_MODEL, 2 * RWKV_RANK)),
            full2((D_MODEL, 2 * RWKV_RANK)),
            full3((2, 2 * RWKV_RANK, D_MODEL)),
            full3((2, 2 * RWKV_RANK, D_MODEL)),
            full2((2, D_MODEL)),
            full2((2, D_MODEL)),
        ],
        out_specs=[
            pl.BlockSpec((4, RW_TM, D_MODEL), lambda i: (0, i, 0)),
            pl.BlockSpec((2, RW_TM, D_MODEL), lambda i: (0, i, 0)),
            pl.BlockSpec((2, RW_TM, D_MODEL), lambda i: (0, i, 0)),
        ],
        out_shape=[
            jax.ShapeDtypeStruct((4, N_TOK, D_MODEL), BF16),
            jax.ShapeDtypeStruct((2, N_TOK, D_MODEL), F32),
            jax.ShapeDtypeStruct((2, N_TOK, D_MODEL), F32),
        ],
        compiler_params=_params("arbitrary"),
        name="rwkv_prep",
    )(x, x, x, norm_w.reshape(1, D_MODEL), mod, mu, wa2, aa2, wb_pad, ab_pad, w0, a0)


def _bmm_kernel(a_ref, w_ref, o_ref):
    o_ref[0] = _dot(a_ref[0], w_ref[...])


def _rwkv_rkvg(xm, w):
    tm = 1024
    return pl.pallas_call(
        _bmm_kernel,
        grid=(4, N_TOK // tm),
        in_specs=[
            pl.BlockSpec((1, tm, D_MODEL), lambda n, i: (n, i, 0)),
            pl.BlockSpec((D_MODEL, D_MODEL), lambda n, i: (0, n)),
        ],
        out_specs=pl.BlockSpec((1, tm, D_MODEL), lambda n, i: (n, i, 0)),
        out_shape=jax.ShapeDtypeStruct((4, N_TOK, D_MODEL), F32),
        compiler_params=_params("arbitrary", "arbitrary"),
        name="rwkv_rkvg",
    )(xm, w)


def _rwkv_scan_kernel(*refs, zero_init, n_tb):
    if zero_init:
        r_ref, k_ref, v_ref, w_ref, a_ref, kkp_ref, kap_ref, y_ref, st_ref, s_ref, vec_ref = refs
    else:
        r_ref, k_ref, v_ref, w_ref, a_ref, kkp_ref, kap_ref, s0_ref, y_ref, st_ref, s_ref, vec_ref = refs
    tb = pl.program_id(1)

    @pl.when(tb == 0)
    def _():
        if zero_init:
            s_ref[...] = jnp.zeros(s_ref.shape, F32)
        else:
            s_ref[...] = s0_ref[...]

    kkp = kkp_ref[...]
    kap = kap_ref[...]
    zero = jnp.zeros((RWKV_HD, LANES), F32)

    def step(i, carry):
        k = k_ref[i]
        a = a_ref[i]
        kk = k * kkp
        kk = kk * lax.rsqrt(jnp.maximum(jnp.sum(kk * kk, axis=0, keepdims=True), 1e-12))
        vec_ref[0] = kk
        vec_ref[1] = w_ref[i]
        vec_ref[2] = kk * a
        vec_ref[3] = k * (1.0 + (a - 1.0) * kap)
        vec_ref[4] = r_ref[i]
        v = v_ref[i]

        def row(n, kidx):
            return vec_ref[n, pl.ds(kidx, 1), :]

        def dot_kk(kidx, acc):
            return acc + s_ref[kidx] * row(0, kidx)

        sa = -lax.fori_loop(0, RWKV_HD, dot_kk, zero, unroll=8)

        def update(kidx, y):
            s_new = s_ref[kidx] * row(1, kidx) + sa * row(2, kidx) + v * row(3, kidx)
            s_ref[kidx] = s_new
            return y + s_new * row(4, kidx)

        y_ref[i] = lax.fori_loop(0, RWKV_HD, update, zero, unroll=8)
        return carry

    lax.fori_loop(0, RW_TB, step, 0)

    @pl.when(tb == n_tb - 1)
    def _():
        st_ref[...] = s_ref[...]


def _rwkv_scan(r, k, v, w, a, kkp, kap, s0):
    t, _, nc = r.shape
    n_tb = t // RW_TB
    seq_spec = pl.BlockSpec((RW_TB, RWKV_HD, LANES), lambda g, tb: (tb, 0, g))
    par_spec = pl.BlockSpec((RWKV_HD, LANES), lambda g, tb: (0, g))
    st_spec = pl.BlockSpec((RWKV_HD, RWKV_HD, LANES), lambda g, tb: (0, 0, g))
    args = [r, k, v, w, a, kkp, kap] + ([] if s0 is None else [s0])
    in_specs = [seq_spec] * 5 + [par_spec] * 2 + ([] if s0 is None else [st_spec])
    return pl.pallas_call(
        functools.partial(_rwkv_scan_kernel, zero_init=s0 is None, n_tb=n_tb),
        grid=(nc // LANES, n_tb),
        in_specs=in_specs,
        out_specs=[seq_spec, st_spec],
        out_shape=[
            jax.ShapeDtypeStruct((t, RWKV_HD, nc), F32),
            jax.ShapeDtypeStruct((RWKV_HD, RWKV_HD, nc), F32),
        ],
        scratch_shapes=[
            pltpu.VMEM((RWKV_HD, RWKV_HD, LANES), F32),
            pltpu.VMEM((5, RWKV_HD, LANES), F32),
        ],
        compiler_params=_params("arbitrary", "arbitrary"),
        name="rwkv_scan",
    )(*args)


def _rwkv_finish_kernel(y_ref, r_ref, k_ref, v_ref, a_ref, ka_ref, rk_ref, gn_ref, o_ref):
    shp = (RWKV_HEADS, RWKV_HD, RW_TT)
    par = (RWKV_HEADS, RWKV_HD, 1)
    y = (y_ref[0] + y_ref[1]).reshape(shp)
    yc = y - jnp.mean(y, axis=1, keepdims=True)
    o = yc * lax.rsqrt(jnp.mean(yc * yc, axis=1, keepdims=True) + GN_EPS) * gn_ref[...].reshape(par)
    r = r_ref[...].reshape(shp)
    k = k_ref[...].reshape(shp)
    v = v_ref[...].reshape(shp)
    ka = ka_ref[...].reshape(par)
    rk = rk_ref[...].reshape(par)
    for dr in range(2):
        kd = k * (1.0 + (a_ref[dr].reshape(shp) - 1.0) * ka)
        o = o + jnp.sum(r * kd * rk, axis=1, keepdims=True) * v
    o_ref[...] = o.reshape(D_MODEL, RW_TT)


def _rwkv_finish(y_t, r_t, k_t, v_t, a_t, ka, rk, gn):
    tok = pl.BlockSpec((D_MODEL, RW_TT), lambda i: (0, i))
    tok2 = pl.BlockSpec((2, D_MODEL, RW_TT), lambda i: (0, 0, i))
    par = pl.BlockSpec((D_MODEL, 1), lambda i: (0, 0))
    return pl.pallas_call(
        _rwkv_finish_kernel,
        grid=(N_TOK // RW_TT,),
        in_specs=[tok2, tok, tok, tok, tok2, par, par, par],
        out_specs=tok,
        out_shape=jax.ShapeDtypeStruct((D_MODEL, N_TOK), F32),
        compiler_params=_params("arbitrary"),
        name="rwkv_finish",
    )(y_t, r_t, k_t, v_t, a_t, ka.reshape(D_MODEL, 1), rk.reshape(D_MODEL, 1), gn.reshape(D_MODEL, 1))


def _to_chains(fwd, bwd, b, t):
    f = fwd.reshape(b, t, RWKV_HEADS, RWKV_HD)
    r = bwd.reshape(b, t, RWKV_HEADS, RWKV_HD)[:, ::-1]
    x = jnp.stack([f, r]).transpose(2, 4, 0, 1, 3).reshape(t, RWKV_HD, 2 * b * RWKV_HEADS)
    pad = (-x.shape[-1]) % LANES
    return jnp.pad(x, ((0, 0), (0, 0), (0, pad))) if pad else x


def _param_chains(w, b):
    x = jnp.broadcast_to(w.reshape(1, RWKV_HEADS, RWKV_HD).transpose(2, 0, 1), (RWKV_HD, 2 * b, RWKV_HEADS))
    x = x.reshape(RWKV_HD, 2 * b * RWKV_HEADS)
    pad = (-x.shape[-1]) % LANES
    return jnp.pad(x, ((0, 0), (0, pad))) if pad else x


def _from_chains(y, b, t):
    y = y[:, :, :2 * b * RWKV_HEADS].reshape(t, RWKV_HD, 2, b, RWKV_HEADS)
    y = y.transpose(2, 4, 1, 3, 0)
    y = jnp.stack([y[0], y[1, ..., ::-1]])
    return y.reshape(2, D_MODEL, b * t)


def _layer_rwkv(x, p, mod, j):
    i = N_MIXERS * j + 1
    wa, wb, aa, ab = p['rwkv_wA'][j], p['rwkv_wB'][j], p['rwkv_aA'][j], p['rwkv_aB'][j]
    z = jnp.zeros_like(wb[0])
    wa2 = jnp.concatenate([wa[0], wa[1]], axis=1).astype(BF16)
    aa2 = jnp.concatenate([aa[0], aa[1]], axis=1).astype(BF16)
    wb_pad = jnp.stack([jnp.concatenate([wb[0], z]), jnp.concatenate([z, wb[1]])]).astype(BF16)
    ab_pad = jnp.stack([jnp.concatenate([ab[0], z]), jnp.concatenate([z, ab[1]])]).astype(BF16)
    xm, dec, a = _rwkv_prep(x, p['norm_w'][i], mod, p['rwkv_mu'][j], wa2, aa2, wb_pad, ab_pad,
                            p['rwkv_w0'][j], p['rwkv_a0'][j])
    rkvg = _rwkv_rkvg(xm, p['rwkv_w_in'][j].astype(BF16))
    r, k, v = rkvg[0], rkvg[1], rkvg[2]
    ys = []
    st = None
    for lo, hi, b, t, s0 in ((0, N_PROMPT_TOK, BATCH, SEQ, None),
                             (N_PROMPT_TOK, N_TOK, DEC_BATCH, DEC_SEQ, p['state_rwkv'][:, j])):
        args = [_to_chains(u[lo:hi], u[lo:hi], b, t) for u in (r, k, v)]
        args += [_to_chains(u[0, lo:hi], u[1, lo:hi], b, t) for u in (dec, a)]
        args += [_param_chains(p['rwkv_kk'][j], b), _param_chains(p['rwkv_ka'][j], b)]
        if s0 is not None:
            s0 = s0.transpose(4, 3, 1, 0, 2).reshape(RWKV_HD, RWKV_HD, 2 * b * RWKV_HEADS)
            s0 = jnp.pad(s0, ((0, 0), (0, 0), (0, (-s0.shape[-1]) % LANES)))
        y, s_fin = _rwkv_scan(*args, s0)
        ys.append(_from_chains(y, b, t))
        if s0 is None:
            st = s_fin.reshape(RWKV_HD, RWKV_HD, 2, b, RWKV_HEADS).transpose(3, 2, 4, 1, 0)[:, None]
    y_t = jnp.concatenate(ys, axis=-1)
    o_t = _rwkv_finish(y_t, r.T, k.T, v.T, a.transpose(0, 2, 1), p['rwkv_ka'][j],
                       p['rwkv_rk'][j].reshape(D_MODEL), p['rwkv_gn'][j])
    x = _out_proj(o_t.T, rkvg[3], 0, p['rwkv_w_out'][j].astype(BF16), x, mod, p['final_norm_w'], False)
    return x, st


DIFF_W = 2 * DIFF_HD
ATT_QB = 256


def _first_half_lanes():
    return lax.broadcasted_iota(jnp.int32, (1, LANES), 1) < LANES // 2


def _diff_lambda(lam_ref, lam_init):
    lp = lam_ref[...]
    return (jnp.exp(jnp.sum(lp[0:1] * lp[1:2], keepdims=True))
            - jnp.exp(jnp.sum(lp[2:3] * lp[3:4], keepdims=True)) + lam_init)


def _diff_head(q, key_sets, lam, gn, lam_init):
    first = _first_half_lanes()
    scale = DIFF_HD ** -0.5
    probs = []
    for comp in range(2):
        qm = jnp.where(first if comp == 0 else ~first, q, 0.0).astype(BF16)
        s = [_dot_nt(qm, kb) * scale for kb, _ in key_sets]
        m = functools.reduce(jnp.maximum, [jnp.max(u, axis=-1, keepdims=True) for u in s])
        e = [jnp.exp(u - m) for u in s]
        den = functools.reduce(lambda x, y: x + y, [jnp.sum(u, axis=-1, keepdims=True) for u in e])
        probs.append([u / den for u in e])
    o = None
    for n, (_, vb) in enumerate(key_sets):
        part = _dot((probs[0][n] - lam * probs[1][n]).astype(BF16), vb)
        o = part if o is None else o + part
    o = o * lax.rsqrt(jnp.mean(o * o, axis=-1, keepdims=True) + EPS) * gn
    return o * (1.0 - lam_init)


def _diff_prompt_kernel(lam_ref, q_ref, k_ref, v_ref, gn_ref, o_ref, *, lam_init):
    lam = _diff_lambda(lam_ref, lam_init)
    for h in range(DIFF_HEADS):
        sl = slice(h * DIFF_W, (h + 1) * DIFF_W)
        keys = [(k_ref[:, sl].astype(BF16), v_ref[:, sl].astype(BF16))]
        o_ref[:, sl] = _diff_head(q_ref[:, sl], keys, lam, gn_ref[:, sl], lam_init)


def _diff_latent_kernel(lam_ref, q_ref, k_ref, v_ref, ck_ref, cv_ref, cos_ref, slo_ref, shi_ref, gn_ref,
                        _prev_ref, o_ref, *, lam_init):
    lam = _diff_lambda(lam_ref, lam_init)
    tabs = (cos_ref[...], slo_ref[...], shi_ref[...])
    q = _rope(q_ref[...], *tabs, DIFF_HD // 4)
    k = _rope(k_ref[...], *tabs, DIFF_HD // 4)
    keys = [(k.astype(BF16), v_ref[...].astype(BF16)),
            (ck_ref[0, 0, 0].astype(BF16), cv_ref[0, 0, 0].astype(BF16))]
    gn = gn_ref[...]
    for qi in range(DEC_SEQ // ATT_QB):
        sl = slice(qi * ATT_QB, (qi + 1) * ATT_QB)
        o_ref[sl, :] = _diff_head(q[sl], keys, lam, gn, lam_init)


def _diff_attention(proj, lam_p, gn_w, cache_k, cache_v, j, lam_init):
    gn = gn_w.reshape(1, D_MODEL)
    lam_spec = pl.BlockSpec((4, DIFF_HD), lambda *_: (0, 0))
    o_p = pl.pallas_call(
        functools.partial(_diff_prompt_kernel, lam_init=lam_init),
        grid=(BATCH,),
        in_specs=[
            lam_spec,
            pl.BlockSpec((SEQ, D_MODEL), lambda b: (b, 0)),
            pl.BlockSpec((SEQ, D_MODEL), lambda b: (b, 1)),
            pl.BlockSpec((SEQ, D_MODEL), lambda b: (b, 2)),
            pl.BlockSpec((1, D_MODEL), lambda b: (0, 0)),
        ],
        out_specs=pl.BlockSpec((SEQ, D_MODEL), lambda b: (b, 0)),
        out_shape=jax.ShapeDtypeStruct((N_TOK, D_MODEL), F32),
        compiler_params=_params("arbitrary"),
        name="diff_prompt",
    )(lam_p, proj, proj, proj, gn)
    cos, slo, shi = (jnp.concatenate([u, u], axis=-1) for u in _rope_tables(DIFF_HD))
    rb = N_PROMPT_TOK // DEC_SEQ
    nh = DIFF_HEADS
    tab = pl.BlockSpec((DEC_SEQ, DIFF_W), lambda b, h: (0, 0))
    cache = pl.BlockSpec((1, 1, 1, PAST_LEN, DIFF_W), lambda b, h: (b, j, h, 0, 0))
    return pl.pallas_call(
        functools.partial(_diff_latent_kernel, lam_init=lam_init),
        grid=(DEC_BATCH, nh),
        in_specs=[
            lam_spec,
            pl.BlockSpec((DEC_SEQ, DIFF_W), lambda b, h: (rb + b, h)),
            pl.BlockSpec((DEC_SEQ, DIFF_W), lambda b, h: (rb + b, nh + h)),
            pl.BlockSpec((DEC_SEQ, DIFF_W), lambda b, h: (rb + b, 2 * nh + h)),
            cache, cache, tab, tab, tab,
            pl.BlockSpec((1, DIFF_W), lambda b, h: (0, h)),
            pl.BlockSpec(memory_space=pl.ANY),
        ],
        out_specs=pl.BlockSpec((DEC_SEQ, DIFF_W), lambda b, h: (rb + b, h)),
        out_shape=jax.ShapeDtypeStruct((N_TOK, D_MODEL), F32),
        input_output_aliases={10: 0},
        compiler_params=_params("arbitrary", "arbitrary"),
        name="diff_latent",
    )(lam_p, proj, proj, proj, cache_k, cache_v, cos, slo, shi, gn, o_p)


def _heads_major(proj, col, heads):
    t = proj[:N_PROMPT_TOK, col * D_MODEL:(col + 1) * D_MODEL]
    return t.reshape(BATCH, SEQ, heads, D_MODEL // heads).transpose(0, 2, 1, 3)[:, None]


def _layer_diff(x, p, mod, j, i):
    lam_init = 0.8 - 0.6 * math.exp(-0.3 * i)
    proj = _in_proj(x, p['norm_w'][i], mod, p['diff_w_in'][j].astype(BF16), 1024)
    o = _diff_attention(proj, p['diff_lambda'][j], p['diff_gn'][j], p['cache_diff_k'], p['cache_diff_v'], j, lam_init)
    x = _out_proj(o, proj, 3, p['diff_w_out'][j].astype(BF16), x, mod, p['final_norm_w'], False)
    return x, _heads_major(proj, 1, DIFF_HEADS), _heads_major(proj, 2, DIFF_HEADS)


NA_ROWS = DEC_SEQ // GRID_W
NA_WR = min(NA_WIN_R, NA_ROWS)
NA_LOC = NA_WR * GRID_W


def _na_prompt_kernel(q_ref, k_ref, v_ref, o_ref):
    first = _first_half_lanes()
    scale = NA_HD ** -0.5
    for pr in range(NA_HEADS // 2):
        sl = slice(pr * LANES, (pr + 1) * LANES)
        q = q_ref[:, sl]
        kb = k_ref[:, sl].astype(BF16)
        vb = v_ref[:, sl].astype(BF16)
        outs = []
        for half in range(2):
            qm = jnp.where(first if half == 0 else ~first, q, 0.0).astype(BF16)
            pr_ = _softmax_rows(_dot_nt(qm, kb) * scale).astype(BF16)
            outs.append(_dot(pr_, vb))
        o_ref[:, sl] = jnp.where(first, outs[0], outs[1])


def _na_latent_kernel(q_ref, k_ref, v_ref, kc_ref, vc_ref, bias_ref, _prev_ref, o_ref):
    first = _first_half_lanes()
    scale = NA_HD ** -0.5
    kb = k_ref[...].astype(BF16)
    vb = v_ref[...].astype(BF16)
    kcb = kc_ref[0, 0].astype(BF16)
    vcb = vc_ref[0, 0].astype(BF16)
    qcol = lax.broadcasted_iota(jnp.int32, (GRID_W, NA_LOC), 0)
    kcol = lax.broadcasted_iota(jnp.int32, (GRID_W, NA_LOC), 1) & (GRID_W - 1)
    cstart = jnp.clip(qcol - NA_WIN_C // 2, 0, GRID_W - NA_WIN_C)
    col_ok = (kcol >= cstart) & (kcol < cstart + NA_WIN_C)
    for r in range(NA_ROWS):
        rs = min(max(r - NA_WR // 2, 0), NA_ROWS - NA_WR)
        rows = slice(r * GRID_W, (r + 1) * GRID_W)
        q = q_ref[rows, :]
        kl = kb[rs * GRID_W:(rs + NA_WR) * GRID_W]
        vl = vb[rs * GRID_W:(rs + NA_WR) * GRID_W]
        outs = []
        for half in range(2):
            qm = jnp.where(first if half == 0 else ~first, q, 0.0).astype(BF16)
            s_loc = _dot_nt(qm, kl) * scale + bias_ref[half, r]
            s_loc = jnp.where(col_ok, s_loc, -jnp.inf)
            s_ctx = _dot_nt(qm, kcb) * scale
            m = jnp.maximum(jnp.max(s_loc, axis=-1, keepdims=True), jnp.max(s_ctx, axis=-1, keepdims=True))
            e_loc = jnp.exp(s_loc - m)
            e_ctx = jnp.exp(s_ctx - m)
            den = jnp.sum(e_loc, axis=-1, keepdims=True) + jnp.sum(e_ctx, axis=-1, keepdims=True)
            outs.append(_dot((e_loc / den).astype(BF16), vl) + _dot((e_ctx / den).astype(BF16), vcb))
        o_ref[rows, :] = jnp.where(first, outs[0], outs[1])


def _na_bias_rows(table):
    r = jnp.arange(NA_ROWS)
    row_idx = jnp.clip(r - NA_WR // 2, 0, NA_ROWS - NA_WR)[:, None] + jnp.arange(NA_WR)[None, :]
    row_off = row_idx - r[:, None] + (NA_WIN_R - 1)
    c = jnp.arange(GRID_W)
    col_off = jnp.clip(c[None, :] - c[:, None], -(NA_WIN_C - 1), NA_WIN_C - 1) + (NA_WIN_C - 1)
    bias = table.astype(F32)[:, row_off[:, None, :, None], col_off[None, :, None, :]]
    return bias.reshape(NA_HEADS, NA_ROWS, GRID_W, NA_LOC)


def _pair_heads(cache):
    c = cache.reshape(DEC_BATCH, NA_HEADS // 2, 2, PAST_LEN, NA_HD)
    return c.transpose(0, 1, 3, 2, 4).reshape(DEC_BATCH, NA_HEADS // 2, PAST_LEN, LANES)


def _na_attention(proj, bias_table, cache_k, cache_v):
    o_p = pl.pallas_call(
        _na_prompt_kernel,
        grid=(BATCH,),
        in_specs=[
            pl.BlockSpec((SEQ, D_MODEL), lambda b: (b, 0)),
            pl.BlockSpec((SEQ, D_MODEL), lambda b: (b, 1)),
            pl.BlockSpec((SEQ, D_MODEL), lambda b: (b, 2)),
        ],
        out_specs=pl.BlockSpec((SEQ, D_MODEL), lambda b: (b, 0)),
        out_shape=jax.ShapeDtypeStruct((N_TOK, D_MODEL), F32),
        compiler_params=_params("arbitrary"),
        name="na_prompt",
    )(proj, proj, proj)
    rb = N_PROMPT_TOK // DEC_SEQ
    npair = NA_HEADS // 2
    cache = pl.BlockSpec((1, 1, PAST_LEN, LANES), lambda pr, b: (b, pr, 0, 0))
    return pl.pallas_call(
        _na_latent_kernel,
        grid=(npair, DEC_BATCH),
        in_specs=[
            pl.BlockSpec((DEC_SEQ, LANES), lambda pr, b: (rb + b, pr)),
            pl.BlockSpec((DEC_SEQ, LANES), lambda pr, b: (rb + b, npair + pr)),
            pl.BlockSpec((DEC_SEQ, LANES), lambda pr, b: (rb + b, 2 * npair + pr)),
            cache, cache,
            pl.BlockSpec((2, NA_ROWS, GRID_W, NA_LOC), lambda pr, b: (pr, 0, 0, 0)),
            pl.BlockSpec(memory_space=pl.ANY),
        ],
        out_specs=pl.BlockSpec((DEC_SEQ, LANES), lambda pr, b: (rb + b, pr)),
        out_shape=jax.ShapeDtypeStruct((N_TOK, D_MODEL), F32),
        input_output_aliases={6: 0},
        compiler_params=_params("arbitrary", "arbitrary"),
        name="na_latent",
    )(proj, proj, proj, _pair_heads(cache_k), _pair_heads(cache_v), _na_bias_rows(bias_table), o_p)


def _layer_na(x, p, mod, j, final):
    i = N_MIXERS * j + 3
    proj = _in_proj(x, p['norm_w'][i], mod, p['na_w_in'][j].astype(BF16), 1024)
    o = _na_attention(proj, p['na_bias'][j], p['cache_na_k'][:, j], p['cache_na_v'][:, j])
    x = _out_proj(o, proj, 3, p['na_w_out'][j].astype(BF16), x, mod, p['final_norm_w'], final)
    return x, _heads_major(proj, 1, NA_HEADS), _heads_major(proj, 2, NA_HEADS)


def kernel(x_prompt, x_sample, state_ret, state_rwkv, cache_diff_k, cache_diff_v, cache_na_k, cache_na_v,
           c, c_ctx, norm_w, w_mod, b_mod, final_norm_w,
           ret_w_in, ret_decay, ret_gn, ret_w_out,
           rwkv_mu, rwkv_w_in, rwkv_w0, rwkv_wA, rwkv_wB, rwkv_a0, rwkv_aA, rwkv_aB,
           rwkv_kk, rwkv_ka, rwkv_rk, rwkv_gn, rwkv_w_out,
           diff_w_in, diff_lambda, diff_gn, diff_w_out,
           na_w_in, na_bias, na_w_out):
    p = dict(locals())
    cond = jnp.zeros((N_COND, D_MODEL), F32).at[0].set(c_ctx).at[1:1 + DEC_BATCH].set(c)
    mods = _modulation(cond, w_mod, b_mod)
    x = jnp.concatenate([x_prompt.reshape(N_PROMPT_TOK, D_MODEL), x_sample.reshape(N_SAMPLE_TOK, D_MODEL)])
    new = {n: [] for n in ('ret', 'rwkv', 'dk', 'dv', 'nk', 'nv')}
    for i in range(DEPTH):
        kind, j = i % N_MIXERS, i // N_MIXERS
        if kind == 0:
            x, st = _layer_ret(x, p, mods[i], j)
            new['ret'].append(st)
        elif kind == 1:
            x, st = _layer_rwkv(x, p, mods[i], j)
            new['rwkv'].append(st)
        elif kind == 2:
            x, ck, cv = _layer_diff(x, p, mods[i], j, i)
            new['dk'].append(ck)
            new['dv'].append(cv)
        else:
            x, ck, cv = _layer_na(x, p, mods[i], j, final=(i == DEPTH - 1))
            new['nk'].append(ck)
            new['nv'].append(cv)
    if DEPTH % N_MIXERS:
        raise NotImplementedError("the final norm is fused into the last neighbourhood-attention layer")
    cat = lambda xs: xs[0] if len(xs) == 1 else jnp.concatenate(xs, axis=1)
    return (x[:N_PROMPT_TOK].reshape(BATCH, SEQ, D_MODEL), x[N_PROMPT_TOK:].reshape(DEC_BATCH, DEC_SEQ, D_MODEL),
            cat(new['ret']), cat(new['rwkv']), cat(new['dk']), cat(new['dv']), cat(new['nk']), cat(new['nv']))
```

```python
import functools
import math

import jax
import jax.numpy as jnp
from jax import lax
from jax.experimental import pallas as pl
from jax.experimental.pallas import tpu as pltpu

F32 = jnp.float32
BF16 = jnp.bfloat16

D_MODEL = 1024
BATCH = 32
SEQ = 256
DEPTH = 4
N_MIXERS = 4
DEC_BATCH = 2
DEC_SEQ = 1024
PAST_LEN = 256
GRID_W = 64

RET_HEADS = 4
RET_DK = 256
RET_DV = 512
RET_QK = 1024
RET_V = 2048

RWKV_HD = 64
RWKV_HEADS = 16
RWKV_RANK = 64

DIFF_HEADS = 8
DIFF_HD = 64

NA_HEADS = 16
NA_HD = 64
NA_WIN_R = 8
NA_WIN_C = 16

ROPE_BASE = 10000.0
EPS = 1e-6
GN_EPS = 1e-5

N_PROMPT_TOK = BATCH * SEQ
N_SAMPLE_TOK = DEC_BATCH * DEC_SEQ
N_TOK = N_PROMPT_TOK + N_SAMPLE_TOK
N_COND = 8

LANES = 128
VMEM_LIMIT = 56 * 2 ** 20


def _params(*sem):
    return pltpu.CompilerParams(dimension_semantics=sem, vmem_limit_bytes=VMEM_LIMIT)


def _cond_of_tile(i, tm):
    npt = N_PROMPT_TOK // tm
    return jnp.where(i < npt, 0, 1 + (i - npt) // (DEC_SEQ // tm))


def _sigmoid(x):
    return 1.0 / (1.0 + jnp.exp(-x))


def _silu(x):
    return x * _sigmoid(x)


def _dot(a, b):
    return jnp.dot(a, b, preferred_element_type=F32)


def _dot_nt(a, b):
    return lax.dot_general(a, b, (((1,), (1,)), ((), ())), preferred_element_type=F32)


def _dot_tn(a, b):
    return lax.dot_general(a, b, (((0,), (0,)), ((), ())), preferred_element_type=F32)


def _softmax_rows(s):
    m = jnp.max(s, axis=-1, keepdims=True)
    e = jnp.exp(s - m)
    return e / jnp.sum(e, axis=-1, keepdims=True)


def _mod_kernel(c_ref, w_ref, b_ref, o_ref):
    s = _silu(c_ref[...])
    o_ref[0] = jnp.dot(s, w_ref[0], precision=lax.Precision.HIGHEST, preferred_element_type=F32) + b_ref[0]


def _modulation(cond, w_mod, b_mod):
    tn = D_MODEL
    out = pl.pallas_call(
        _mod_kernel,
        grid=(DEPTH, 3 * D_MODEL // tn),
        in_specs=[
            pl.BlockSpec((N_COND, D_MODEL), lambda l, j: (0, 0)),
            pl.BlockSpec((1, D_MODEL, tn), lambda l, j: (l, 0, j)),
            pl.BlockSpec((1, 1, tn), lambda l, j: (l, 0, j)),
        ],
        out_specs=pl.BlockSpec((1, N_COND, tn), lambda l, j: (l, 0, j)),
        out_shape=jax.ShapeDtypeStruct((DEPTH, N_COND, 3 * D_MODEL), F32),
        compiler_params=_params("arbitrary", "arbitrary"),
        name="modulation",
    )(cond, w_mod, b_mod.reshape(DEPTH, 1, 3 * D_MODEL))
    return out.reshape(DEPTH, N_COND, 3, 1, D_MODEL)


def _norm_mod(x, nw, mod_ref):
    ms = jnp.mean(x * x, axis=-1, keepdims=True)
    y = x * lax.rsqrt(ms + EPS) * nw
    return y * (1.0 + mod_ref[0, 1]) + mod_ref[0, 0]


IN_TM = 1024


def _in_proj_kernel(x_ref, nw_ref, mod_ref, w_ref, o_ref, h_ref):
    @pl.when(pl.program_id(1) == 0)
    def _():
        h_ref[...] = _norm_mod(x_ref[...], nw_ref[...], mod_ref).astype(BF16)

    o_ref[...] = _dot(h_ref[...], w_ref[...])


def _in_proj(x, norm_w, mod, w, tn):
    n = w.shape[1]
    return pl.pallas_call(
        _in_proj_kernel,
        grid=(N_TOK // IN_TM, n // tn),
        in_specs=[
            pl.BlockSpec((IN_TM, D_MODEL), lambda i, j: (i, 0)),
            pl.BlockSpec((1, D_MODEL), lambda i, j: (0, 0)),
            pl.BlockSpec((1, 3, 1, D_MODEL), lambda i, j: (_cond_of_tile(i, IN_TM), 0, 0, 0)),
            pl.BlockSpec((D_MODEL, tn), lambda i, j: (0, j)),
        ],
        out_specs=pl.BlockSpec((IN_TM, tn), lambda i, j: (i, j)),
        out_shape=jax.ShapeDtypeStruct((N_TOK, n), F32),
        scratch_shapes=[pltpu.VMEM((IN_TM, D_MODEL), BF16)],
        compiler_params=_params("arbitrary", "arbitrary"),
        name="in_proj",
    )(x, norm_w.reshape(1, D_MODEL), mod, w)


OUT_TM = 256


def _out_proj_kernel(o_ref, g_ref, w_ref, x_ref, mod_ref, fw_ref, y_ref, *, final):
    a = (o_ref[...] * _silu(g_ref[...])).astype(BF16)
    xn = x_ref[...] + mod_ref[0, 2] * _dot(a, w_ref[...])
    if final:
        ms = jnp.mean(xn * xn, axis=-1, keepdims=True)
        xn = xn * lax.rsqrt(ms + EPS) * fw_ref[...]
    y_ref[...] = xn


def _out_proj(o, g_arr, g_blk, w, x, mod, final_w, final):
    k = w.shape[0]
    return pl.pallas_call(
        functools.partial(_out_proj_kernel, final=final),
        grid=(N_TOK // OUT_TM,),
        in_specs=[
            pl.BlockSpec((OUT_TM, k), lambda i: (i, 0)),
            pl.BlockSpec((OUT_TM, k), lambda i: (i, g_blk)),
            pl.BlockSpec((k, D_MODEL), lambda i: (0, 0)),
            pl.BlockSpec((OUT_TM, D_MODEL), lambda i: (i, 0)),
            pl.BlockSpec((1, 3, 1, D_MODEL), lambda i: (_cond_of_tile(i, OUT_TM), 0, 0, 0)),
            pl.BlockSpec((1, D_MODEL), lambda i: (0, 0)),
        ],
        out_specs=pl.BlockSpec((OUT_TM, D_MODEL), lambda i: (i, 0)),
        out_shape=jax.ShapeDtypeStruct((N_TOK, D_MODEL), F32),
        compiler_params=_params("arbitrary"),
        name="out_proj",
    )(o, g_arr, w, x, mod, final_w.reshape(1, D_MODEL))


def _rope_tables(d):
    q = d // 4
    t = jnp.arange(DEC_SEQ)
    row = (t // GRID_W).astype(F32)
    col = (t % GRID_W).astype(F32)
    inv = ROPE_BASE ** (-jnp.arange(0, 2 * q, 2, dtype=F32) / (2 * q))
    ar = row[:, None] * inv[None, :]
    ac = col[:, None] * inv[None, :]
    z = jnp.zeros_like(ar)
    cos = jnp.concatenate([jnp.cos(ar), jnp.cos(ar), jnp.cos(ac), jnp.cos(ac)], axis=-1)
    sin_lo = jnp.concatenate([-jnp.sin(ar), z, -jnp.sin(ac), z], axis=-1)
    sin_hi = jnp.concatenate([z, jnp.sin(ar), z, jnp.sin(ac)], axis=-1)
    return cos, sin_lo, sin_hi


def _rope(x, cos, sin_lo, sin_hi, q):
    w = x.shape[-1]
    x_next = pltpu.roll(x, w - q, axis=1)
    x_prev = pltpu.roll(x, q, axis=1)
    return x * cos + x_next * sin_lo + x_prev * sin_hi


RET_QB = 256


def _ret_kernel(lg_ref, q_ref, k_ref, v_ref, gn_ref, *rest, seq, latent):
    if latent:
        cos_ref, slo_ref, shi_ref, s0_ref, _prev_ref, o_ref = rest
    else:
        o_ref, st_ref = rest
    h = pl.program_id(1)
    lgf = lg_ref[0, h]
    lgb = lg_ref[1, h]
    q = q_ref[...]
    k = k_ref[...]
    if latent:
        q = _rope(q, cos_ref[...], slo_ref[...], shi_ref[...], RET_DK // 4)
        k = _rope(k, cos_ref[...], slo_ref[...], shi_ref[...], RET_DK // 4)
    k = k * (RET_DK ** -0.5)
    kb = k.astype(BF16)
    vb = v_ref[...].astype(BF16)
    gn = gn_ref[...]
    for qi in range(seq // RET_QB):
        qblk = q[qi * RET_QB:(qi + 1) * RET_QB]
        s = _dot_nt(qblk.astype(BF16), kb)
        ii = lax.broadcasted_iota(jnp.int32, (RET_QB, seq), 0) + qi * RET_QB
        jj = lax.broadcasted_iota(jnp.int32, (RET_QB, seq), 1)
        gap = (ii - jj).astype(F32)
        dec = (jnp.where(gap >= 0, jnp.exp(lgf * jnp.maximum(gap, 0.0)), 0.0)
               + jnp.where(gap <= 0, jnp.exp(lgb * jnp.maximum(-gap, 0.0)), 0.0))
        o = _dot((s * dec).astype(BF16), vb)
        if latent:
            pos = (lax.broadcasted_iota(jnp.int32, (RET_QB, 1), 0) + qi * RET_QB).astype(F32)
            qf = qblk * jnp.exp(lgf * (pos + 1.0))
            qr = qblk * jnp.exp(lgb * (seq - pos))
            o = o + _dot(qf.astype(BF16), s0_ref[0, 0, 0, 0].astype(BF16))
            o = o + _dot(qr.astype(BF16), s0_ref[0, 0, 1, 0].astype(BF16))
        oc = o - jnp.mean(o, axis=-1, keepdims=True)
        o = oc * lax.rsqrt(jnp.mean(oc * oc, axis=-1, keepdims=True) + GN_EPS) * gn
        o_ref[qi * RET_QB:(qi + 1) * RET_QB, :] = o
    if not latent:
        pos = lax.broadcasted_iota(jnp.int32, (seq, 1), 0).astype(F32)
        kf = k * jnp.exp(lgf * (seq - 1.0 - pos))
        kr = k * jnp.exp(lgb * pos)
        st_ref[0, 0, 0, 0] = _dot_tn(kf.astype(BF16), vb)
        st_ref[0, 0, 1, 0] = _dot_tn(kr.astype(BF16), vb)


def _retention(p, log_g, gn_w, state_ret, j):
    smem = pl.BlockSpec(memory_space=pltpu.SMEM)
    gn = gn_w.reshape(1, RET_V)
    kq = RET_QK // RET_DK
    o_p, st = pl.pallas_call(
        functools.partial(_ret_kernel, seq=SEQ, latent=False),
        grid=(BATCH, RET_HEADS),
        in_specs=[
            smem,
            pl.BlockSpec((SEQ, RET_DK), lambda b, h: (b, h)),
            pl.BlockSpec((SEQ, RET_DK), lambda b, h: (b, kq + h)),
            pl.BlockSpec((SEQ, RET_DV), lambda b, h: (b, kq + h)),
            pl.BlockSpec((1, RET_DV), lambda b, h: (0, h)),
        ],
        out_specs=[
            pl.BlockSpec((SEQ, RET_DV), lambda b, h: (b, h)),
            pl.BlockSpec((1, 1, 2, 1, RET_DK, RET_DV), lambda b, h: (b, 0, 0, h, 0, 0)),
        ],
        out_shape=[
            jax.ShapeDtypeStruct((N_TOK, RET_V), F32),
            jax.ShapeDtypeStruct((BATCH, 1, 2, RET_HEADS, RET_DK, RET_DV), F32),
        ],
        compiler_params=_params("arbitrary", "arbitrary"),
        name="retention_prompt",
    )(log_g, p, p, p, gn)
    cos, slo, shi = _rope_tables(RET_DK)
    rb = N_PROMPT_TOK // DEC_SEQ
    full = pl.BlockSpec((DEC_SEQ, RET_DK), lambda b, h: (0, 0))
    o = pl.pallas_call(
        functools.partial(_ret_kernel, seq=DEC_SEQ, latent=True),
        grid=(DEC_BATCH, RET_HEADS),
        in_specs=[
            smem,
            pl.BlockSpec((DEC_SEQ, RET_DK), lambda b, h: (rb + b, h)),
            pl.BlockSpec((DEC_SEQ, RET_DK), lambda b, h: (rb + b, kq + h)),
            pl.BlockSpec((DEC_SEQ, RET_DV), lambda b, h: (rb + b, kq + h)),
            pl.BlockSpec((1, RET_DV), lambda b, h: (0, h)),
            full, full, full,
            pl.BlockSpec((1, 1, 2, 1, RET_DK, RET_DV), lambda b, h: (b, j, 0, h, 0, 0)),
            pl.BlockSpec(memory_space=pl.ANY),
        ],
        out_specs=pl.BlockSpec((DEC_SEQ, RET_DV), lambda b, h: (rb + b, h)),
        out_shape=jax.ShapeDtypeStruct((N_TOK, RET_V), F32),
        input_output_aliases={9: 0},
        compiler_params=_params("arbitrary", "arbitrary"),
        name="retention_latent",
    )(log_g, p, p, p, gn, cos, slo, shi, state_ret, o_p)
    return o, st


def _layer_ret(x, p, mod, j):
    i = N_MIXERS * j + 0
    proj = _in_proj(x, p['norm_w'][i], mod, p['ret_w_in'][j].astype(BF16), 1024)
    log_g = jax.nn.log_sigmoid(p['ret_decay'][j].astype(F32))
    o, st = _retention(proj, log_g, p['ret_gn'][j], p['state_ret'], j)
    x = _out_proj(o, proj, (2 * RET_QK + RET_V) // RET_V, p['ret_w_out'][j].astype(BF16), x, mod,
                  p['final_norm_w'], False)
    return x, st


RW_TM = 512
RW_HALO = 8
RW_TB = 16
RW_TT = 512


def _rwkv_prep_kernel(x_ref, xp_ref, xn_ref, nw_ref, mod_ref, mu_ref, wa_ref, aa_ref, wb_ref, ab_ref,
                      w0_ref, a0_ref, xm_ref, dec_ref, a_ref):
    i = pl.program_id(0)
    nw = nw_ref[...]
    h = _norm_mod(x_ref[...], nw, mod_ref)
    h_before = _norm_mod(xp_ref[RW_HALO - 1:RW_HALO, :], nw, mod_ref)
    h_after = _norm_mod(xn_ref[0:1, :], nw, mod_ref)
    seq = jnp.where(i < N_PROMPT_TOK // RW_TM, SEQ, DEC_SEQ)
    row = lax.broadcasted_iota(jnp.int32, (RW_TM, 1), 0)
    t = (row + i * RW_TM) & (seq - 1)
    prev = jnp.where(row == 0, h_before, pltpu.roll(h, 1, axis=0))
    nxt = jnp.where(row == RW_TM - 1, h_after, pltpu.roll(h, RW_TM - 1, axis=0))
    prev = jnp.where(t == 0, 0.0, prev)
    nxt = jnp.where(t == seq - 1, 0.0, nxt)
    xx = 0.5 * (prev + nxt) - h
    for n, m in enumerate((0, 2, 3, 5)):
        xm_ref[n] = (h + xx * mu_ref[m:m + 1, :]).astype(BF16)
    xw = (h + xx * mu_ref[1:2, :]).astype(BF16)
    xa = (h + xx * mu_ref[4:5, :]).astype(BF16)
    lw = jnp.tanh(_dot(xw, wa_ref[...])).astype(BF16)
    la = _dot(xa, aa_ref[...]).astype(BF16)
    for dr in range(2):
        wl = w0_ref[dr:dr + 1, :] + _dot(lw, wb_ref[dr])
        dec_ref[dr] = jnp.exp(-math.exp(-0.5) * _sigmoid(wl))
        a_ref[dr] = _sigmoid(a0_ref[dr:dr + 1, :] + _dot(la, ab_ref[dr]))


def _rwkv_prep(x, norm_w, mod, mu, wa2, aa2, wb_pad, ab_pad, w0, a0):
    nt = N_TOK // RW_TM
    hb = RW_TM // RW_HALO
    last = N_TOK // RW_HALO - 1
    full2 = lambda shape: pl.BlockSpec(shape, lambda i: (0, 0))
    full3 = lambda shape: pl.BlockSpec(shape, lambda i: (0, 0, 0))
    return pl.pallas_call(
        _rwkv_prep_kernel,
        grid=(nt,),
        in_specs=[
            pl.BlockSpec((RW_TM, D_MODEL), lambda i: (i, 0)),
            pl.BlockSpec((RW_HALO, D_MODEL), lambda i: (jnp.maximum(i * hb - 1, 0), 0)),
            pl.BlockSpec((RW_HALO, D_MODEL), lambda i: (jnp.minimum((i + 1) * hb, last), 0)),
            full2((1, D_MODEL)),
            pl.BlockSpec((1, 3, 1, D_MODEL), lambda i: (_cond_of_tile(i, RW_TM), 0, 0, 0)),
            full2((6, D_MODEL)),
            full2((D_MODEL, 2 * RWKV_RANK)),
            full2((D_MODEL, 2 * RWKV_RANK)),
            full3((2, 2 * RWKV_RANK, D_MODEL)),
            full3((2, 2 * RWKV_RANK, D_MODEL)),
            full2((2, D_MODEL)),
            full2((2, D_MODEL)),
        ],
        out_specs=[
            pl.BlockSpec((4, RW_TM, D_MODEL), lambda i: (0, i, 0)),
            pl.BlockSpec((2, RW_TM, D_MODEL), lambda i: (0, i, 0)),
            pl.BlockSpec((2, RW_TM, D_MODEL), lambda i: (0, i, 0)),
        ],
        out_shape=[
            jax.ShapeDtypeStruct((4, N_TOK, D_MODEL), BF16),
            jax.ShapeDtypeStruct((2, N_TOK, D_MODEL), F32),
            jax.ShapeDtypeStruct((2, N_TOK, D_MODEL), F32),
        ],
        compiler_params=_params("arbitrary"),
        name="rwkv_prep",
    )(x, x, x, norm_w.reshape(1, D_MODEL), mod, mu, wa2, aa2, wb_pad, ab_pad, w0, a0)


def _bmm_kernel(a_ref, w_ref, o_ref):
    o_ref[0] = _dot(a_ref[0], w_ref[...])


def _rwkv_rkvg(xm, w):
    tm = 1024
    return pl.pallas_call(
        _bmm_kernel,
        grid=(4, N_TOK // tm),
        in_specs=[
            pl.BlockSpec((1, tm, D_MODEL), lambda n, i: (n, i, 0)),
            pl.BlockSpec((D_MODEL, D_MODEL), lambda n, i: (0, n)),
        ],
        out_specs=pl.BlockSpec((1, tm, D_MODEL), lambda n, i: (n, i, 0)),
        out_shape=jax.ShapeDtypeStruct((4, N_TOK, D_MODEL), F32),
        compiler_params=_params("arbitrary", "arbitrary"),
        name="rwkv_rkvg",
    )(xm, w)


def _rwkv_scan_kernel(*refs, zero_init, n_tb):
    if zero_init:
        r_ref, k_ref, v_ref, w_ref, a_ref, kkp_ref, kap_ref, y_ref, st_ref, s_ref, vec_ref = refs
    else:
        r_ref, k_ref, v_ref, w_ref, a_ref, kkp_ref, kap_ref, s0_ref, y_ref, st_ref, s_ref, vec_ref = refs
    tb = pl.program_id(1)

    @pl.when(tb == 0)
    def _():
        if zero_init:
            s_ref[...] = jnp.zeros(s_ref.shape, F32)
        else:
            s_ref[...] = s0_ref[...]

    kkp = kkp_ref[...]
    kap = kap_ref[...]
    zero = jnp.zeros((RWKV_HD, LANES), F32)

    def step(i, carry):
        k = k_ref[i]
        a = a_ref[i]
        kk = k * kkp
        kk = kk * lax.rsqrt(jnp.maximum(jnp.sum(kk * kk, axis=0, keepdims=True), 1e-12))
        vec_ref[0] = kk
        vec_ref[1] = w_ref[i]
        vec_ref[2] = kk * a
        vec_ref[3] = k * (1.0 + (a - 1.0) * kap)
        vec_ref[4] = r_ref[i]
        v = v_ref[i]

        def row(n, kidx):
            return vec_ref[n, pl.ds(kidx, 1), :]

        def dot_kk(kidx, acc):
            return acc + s_ref[kidx] * row(0, kidx)

        sa = -lax.fori_loop(0, RWKV_HD, dot_kk, zero, unroll=8)

        def update(kidx, y):
            s_new = s_ref[kidx] * row(1, kidx) + sa * row(2, kidx) + v * row(3, kidx)
            s_ref[kidx] = s_new
            return y + s_new * row(4, kidx)

        y_ref[i] = lax.fori_loop(0, RWKV_HD, update, zero, unroll=8)
        return carry

    lax.fori_loop(0, RW_TB, step, 0)

    @pl.when(tb == n_tb - 1)
    def _():
        st_ref[...] = s_ref[...]


def _rwkv_scan(r, k, v, w, a, kkp, kap, s0):
    t, _, nc = r.shape
    n_tb = t // RW_TB
    seq_spec = pl.BlockSpec((RW_TB, RWKV_HD, LANES), lambda g, tb: (tb, 0, g))
    par_spec = pl.BlockSpec((RWKV_HD, LANES), lambda g, tb: (0, g))
    st_spec = pl.BlockSpec((RWKV_HD, RWKV_HD, LANES), lambda g, tb: (0, 0, g))
    args = [r, k, v, w, a, kkp, kap] + ([] if s0 is None else [s0])
    in_specs = [seq_spec] * 5 + [par_spec] * 2 + ([] if s0 is None else [st_spec])
    return pl.pallas_call(
        functools.partial(_rwkv_scan_kernel, zero_init=s0 is None, n_tb=n_tb),
        grid=(nc // LANES, n_tb),
        in_specs=in_specs,
        out_specs=[seq_spec, st_spec],
        out_shape=[
            jax.ShapeDtypeStruct((t, RWKV_HD, nc), F32),
            jax.ShapeDtypeStruct((RWKV_HD, RWKV_HD, nc), F32),
        ],
        scratch_shapes=[
            pltpu.VMEM((RWKV_HD, RWKV_HD, LANES), F32),
            pltpu.VMEM((5, RWKV_HD, LANES), F32),
        ],
        compiler_params=_params("arbitrary", "arbitrary"),
        name="rwkv_scan",
    )(*args)


def _rwkv_finish_kernel(y_ref, r_ref, k_ref, v_ref, a_ref, ka_ref, rk_ref, gn_ref, o_ref):
    shp = (RWKV_HEADS, RWKV_HD, RW_TT)
    par = (RWKV_HEADS, RWKV_HD, 1)
    y = (y_ref[0] + y_ref[1]).reshape(shp)
    yc = y - jnp.mean(y, axis=1, keepdims=True)
    o = yc * lax.rsqrt(jnp.mean(yc * yc, axis=1, keepdims=True) + GN_EPS) * gn_ref[...].reshape(par)
    r = r_ref[...].reshape(shp)
    k = k_ref[...].reshape(shp)
    v = v_ref[...].reshape(shp)
    ka = ka_ref[...].reshape(par)
    rk = rk_ref[...].reshape(par)
    for dr in range(2):
        kd = k * (1.0 + (a_ref[dr].reshape(shp) - 1.0) * ka)
        o = o + jnp.sum(r * kd * rk, axis=1, keepdims=True) * v
    o_ref[...] = o.reshape(D_MODEL, RW_TT)


def _rwkv_finish(y_t, r_t, k_t, v_t, a_t, ka, rk, gn):
    tok = pl.BlockSpec((D_MODEL, RW_TT), lambda i: (0, i))
    tok2 = pl.BlockSpec((2, D_MODEL, RW_TT), lambda i: (0, 0, i))
    par = pl.BlockSpec((D_MODEL, 1), lambda i: (0, 0))
    return pl.pallas_call(
        _rwkv_finish_kernel,
        grid=(N_TOK // RW_TT,),
        in_specs=[tok2, tok, tok, tok, tok2, par, par, par],
        out_specs=tok,
        out_shape=jax.ShapeDtypeStruct((D_MODEL, N_TOK), F32),
        compiler_params=_params("arbitrary"),
        name="rwkv_finish",
    )(y_t, r_t, k_t, v_t, a_t, ka.reshape(D_MODEL, 1), rk.reshape(D_MODEL, 1), gn.reshape(D_MODEL, 1))


def _to_chains(fwd, bwd, b, t):
    f = fwd.reshape(b, t, RWKV_HEADS, RWKV_HD)
    r = bwd.reshape(b, t, RWKV_HEADS, RWKV_HD)[:, ::-1]
    x = jnp.stack([f, r]).transpose(2, 4, 0, 1, 3).reshape(t, RWKV_HD, 2 * b * RWKV_HEADS)
    pad = (-x.shape[-1]) % LANES
    return jnp.pad(x, ((0, 0), (0, 0), (0, pad))) if pad else x


def _param_chains(w, b):
    x = jnp.broadcast_to(w.reshape(1, RWKV_HEADS, RWKV_HD).transpose(2, 0, 1), (RWKV_HD, 2 * b, RWKV_HEADS))
    x = x.reshape(RWKV_HD, 2 * b * RWKV_HEADS)
    pad = (-x.shape[-1]) % LANES
    return jnp.pad(x, ((0, 0), (0, pad))) if pad else x


def _from_chains(y, b, t):
    y = y[:, :, :2 * b * RWKV_HEADS].reshape(t, RWKV_HD, 2, b, RWKV_HEADS)
    y = y.transpose(2, 4, 1, 3, 0)
    y = jnp.stack([y[0], y[1, ..., ::-1]])
    return y.reshape(2, D_MODEL, b * t)


def _layer_rwkv(x, p, mod, j):
    i = N_MIXERS * j + 1
    wa, wb, aa, ab = p['rwkv_wA'][j], p['rwkv_wB'][j], p['rwkv_aA'][j], p['rwkv_aB'][j]
    z = jnp.zeros_like(wb[0])
    wa2 = jnp.concatenate([wa[0], wa[1]], axis=1).astype(BF16)
    aa2 = jnp.concatenate([aa[0], aa[1]], axis=1).astype(BF16)
    wb_pad = jnp.stack([jnp.concatenate([wb[0], z]), jnp.concatenate([z, wb[1]])]).astype(BF16)
    ab_pad = jnp.stack([jnp.concatenate([ab[0], z]), jnp.concatenate([z, ab[1]])]).astype(BF16)
    xm, dec, a = _rwkv_prep(x, p['norm_w'][i], mod, p['rwkv_mu'][j], wa2, aa2, wb_pad, ab_pad,
                            p['rwkv_w0'][j], p['rwkv_a0'][j])
    rkvg = _rwkv_rkvg(xm, p['rwkv_w_in'][j].astype(BF16))
    r, k, v = rkvg[0], rkvg[1], rkvg[2]
    ys = []
    st = None
    for lo, hi, b, t, s0 in ((0, N_PROMPT_TOK, BATCH, SEQ, None),
                             (N_PROMPT_TOK, N_TOK, DEC_BATCH, DEC_SEQ, p['state_rwkv'][:, j])):
        args = [_to_chains(u[lo:hi], u[lo:hi], b, t) for u in (r, k, v)]
        args += [_to_chains(u[0, lo:hi], u[1, lo:hi], b, t) for u in (dec, a)]
        args += [_param_chains(p['rwkv_kk'][j], b), _param_chains(p['rwkv_ka'][j], b)]
        if s0 is not None:
            s0 = s0.transpose(4, 3, 1, 0, 2).reshape(RWKV_HD, RWKV_HD, 2 * b * RWKV_HEADS)
            s0 = jnp.pad(s0, ((0, 0), (0, 0), (0, (-s0.shape[-1]) % LANES)))
        y, s_fin = _rwkv_scan(*args, s0)
        ys.append(_from_chains(y, b, t))
        if s0 is None:
            st = s_fin.reshape(RWKV_HD, RWKV_HD, 2, b, RWKV_HEADS).transpose(3, 2, 4, 1, 0)[:, None]
    y_t = jnp.concatenate(ys, axis=-1)
    o_t = _rwkv_finish(y_t, r.T, k.T, v.T, a.transpose(0, 2, 1), p['rwkv_ka'][j],
                       p['rwkv_rk'][j].reshape(D_MODEL), p['rwkv_gn'][j])
    x = _out_proj(o_t.T, rkvg[3], 0, p['rwkv_w_out'][j].astype(BF16), x, mod, p['final_norm_w'], False)
    return x, st


DIFF_W = 2 * DIFF_HD
ATT_QB = 256


def _first_half_lanes():
    return lax.broadcasted_iota(jnp.int32, (1, LANES), 1) < LANES // 2


def _diff_lambda(lam_ref, lam_init):
    lp = lam_ref[...]
    return (jnp.exp(jnp.sum(lp[0:1] * lp[1:2], keepdims=True))
            - jnp.exp(jnp.sum(lp[2:3] * lp[3:4], keepdims=True)) + lam_init)


def _diff_head(q, key_sets, lam, gn, lam_init):
    first = _first_half_lanes()
    scale = DIFF_HD ** -0.5
    probs = []
    for comp in range(2):
        qm = jnp.where(first if comp == 0 else ~first, q, 0.0).astype(BF16)
        s = [_dot_nt(qm, kb) * scale for kb, _ in key_sets]
        m = functools.reduce(jnp.maximum, [jnp.max(u, axis=-1, keepdims=True) for u in s])
        e = [jnp.exp(u - m) for u in s]
        den = functools.reduce(lambda x, y: x + y, [jnp.sum(u, axis=-1, keepdims=True) for u in e])
        probs.append([u / den for u in e])
    o = None
    for n, (_, vb) in enumerate(key_sets):
        part = _dot((probs[0][n] - lam * probs[1][n]).astype(BF16), vb)
        o = part if o is None else o + part
    o = o * lax.rsqrt(jnp.mean(o * o, axis=-1, keepdims=True) + EPS) * gn
    return o * (1.0 - lam_init)


def _diff_prompt_kernel(lam_ref, q_ref, k_ref, v_ref, gn_ref, o_ref, *, lam_init):
    lam = _diff_lambda(lam_ref, lam_init)
    for h in range(DIFF_HEADS):
        sl = slice(h * DIFF_W, (h + 1) * DIFF_W)
        keys = [(k_ref[:, sl].astype(BF16), v_ref[:, sl].astype(BF16))]
        o_ref[:, sl] = _diff_head(q_ref[:, sl], keys, lam, gn_ref[:, sl], lam_init)


def _diff_latent_kernel(lam_ref, q_ref, k_ref, v_ref, ck_ref, cv_ref, cos_ref, slo_ref, shi_ref, gn_ref,
                        _prev_ref, o_ref, *, lam_init):
    lam = _diff_lambda(lam_ref, lam_init)
    tabs = (cos_ref[...], slo_ref[...], shi_ref[...])
    q = _rope(q_ref[...], *tabs, DIFF_HD // 4)
    k = _rope(k_ref[...], *tabs, DIFF_HD // 4)
    keys = [(k.astype(BF16), v_ref[...].astype(BF16)),
            (ck_ref[0, 0, 0].astype(BF16), cv_ref[0, 0, 0].astype(BF16))]
    gn = gn_ref[...]
    for qi in range(DEC_SEQ // ATT_QB):
        sl = slice(qi * ATT_QB, (qi + 1) * ATT_QB)
        o_ref[sl, :] = _diff_head(q[sl], keys, lam, gn, lam_init)


def _diff_attention(proj, lam_p, gn_w, cache_k, cache_v, j, lam_init):
    gn = gn_w.reshape(1, D_MODEL)
    lam_spec = pl.BlockSpec((4, DIFF_HD), lambda *_: (0, 0))
    o_p = pl.pallas_call(
        functools.partial(_diff_prompt_kernel, lam_init=lam_init),
        grid=(BATCH,),
        in_specs=[
            lam_spec,
            pl.BlockSpec((SEQ, D_MODEL), lambda b: (b, 0)),
            pl.BlockSpec((SEQ, D_MODEL), lambda b: (b, 1)),
            pl.BlockSpec((SEQ, D_MODEL), lambda b: (b, 2)),
            pl.BlockSpec((1, D_MODEL), lambda b: (0, 0)),
        ],
        out_specs=pl.BlockSpec((SEQ, D_MODEL), lambda b: (b, 0)),
        out_shape=jax.ShapeDtypeStruct((N_TOK, D_MODEL), F32),
        compiler_params=_params("arbitrary"),
        name="diff_prompt",
    )(lam_p, proj, proj, proj, gn)
    cos, slo, shi = (jnp.concatenate([u, u], axis=-1) for u in _rope_tables(DIFF_HD))
    rb = N_PROMPT_TOK // DEC_SEQ
    nh = DIFF_HEADS
    tab = pl.BlockSpec((DEC_SEQ, DIFF_W), lambda b, h: (0, 0))
    cache = pl.BlockSpec((1, 1, 1, PAST_LEN, DIFF_W), lambda b, h: (b, j, h, 0, 0))
    return pl.pallas_call(
        functools.partial(_diff_latent_kernel, lam_init=lam_init),
        grid=(DEC_BATCH, nh),
        in_specs=[
            lam_spec,
            pl.BlockSpec((DEC_SEQ, DIFF_W), lambda b, h: (rb + b, h)),
            pl.BlockSpec((DEC_SEQ, DIFF_W), lambda b, h: (rb + b, nh + h)),
            pl.BlockSpec((DEC_SEQ, DIFF_W), lambda b, h: (rb + b, 2 * nh + h)),
            cache, cache, tab, tab, tab,
            pl.BlockSpec((1, DIFF_W), lambda b, h: (0, h)),
            pl.BlockSpec(memory_space=pl.ANY),
        ],
        out_specs=pl.BlockSpec((DEC_SEQ, DIFF_W), lambda b, h: (rb + b, h)),
        out_shape=jax.ShapeDtypeStruct((N_TOK, D_MODEL), F32),
        input_output_aliases={10: 0},
        compiler_params=_params("arbitrary", "arbitrary"),
        name="diff_latent",
    )(lam_p, proj, proj, proj, cache_k, cache_v, cos, slo, shi, gn, o_p)


def _heads_major(proj, col, heads):
    t = proj[:N_PROMPT_TOK, col * D_MODEL:(col + 1) * D_MODEL]
    return t.reshape(BATCH, SEQ, heads, D_MODEL // heads).transpose(0, 2, 1, 3)[:, None]


def _layer_diff(x, p, mod, j, i):
    lam_init = 0.8 - 0.6 * math.exp(-0.3 * i)
    proj = _in_proj(x, p['norm_w'][i], mod, p['diff_w_in'][j].astype(BF16), 1024)
    o = _diff_attention(proj, p['diff_lambda'][j], p['diff_gn'][j], p['cache_diff_k'], p['cache_diff_v'], j, lam_init)
    x = _out_proj(o, proj, 3, p['diff_w_out'][j].astype(BF16), x, mod, p['final_norm_w'], False)
    return x, _heads_major(proj, 1, DIFF_HEADS), _heads_major(proj, 2, DIFF_HEADS)


NA_ROWS = DEC_SEQ // GRID_W
NA_WR = min(NA_WIN_R, NA_ROWS)
NA_LOC = NA_WR * GRID_W


def _na_prompt_kernel(q_ref, k_ref, v_ref, o_ref):
    first = _first_half_lanes()
    scale = NA_HD ** -0.5
    for pr in range(NA_HEADS // 2):
        sl = slice(pr * LANES, (pr + 1) * LANES)
        q = q_ref[:, sl]
        kb = k_ref[:, sl].astype(BF16)
        vb = v_ref[:, sl].astype(BF16)
        outs = []
        for half in range(2):
            qm = jnp.where(first if half == 0 else ~first, q, 0.0).astype(BF16)
            pr_ = _softmax_rows(_dot_nt(qm, kb) * scale).astype(BF16)
            outs.append(_dot(pr_, vb))
        o_ref[:, sl] = jnp.where(first, outs[0], outs[1])


def _na_latent_kernel(q_ref, k_ref, v_ref, kc_ref, vc_ref, tab_ref, _prev_ref, o_ref):
    first = _first_half_lanes()
    scale = NA_HD ** -0.5
    kb = k_ref[...].astype(BF16)
    vb = v_ref[...].astype(BF16)
    kcb = kc_ref[0, 0].astype(BF16)
    vcb = vc_ref[0, 0].astype(BF16)
    qcol = lax.broadcasted_iota(jnp.int32, (GRID_W, NA_LOC), 0)
    kcol = lax.broadcasted_iota(jnp.int32, (GRID_W, NA_LOC), 1) & (GRID_W - 1)
    cstart = jnp.clip(qcol - NA_WIN_C // 2, 0, GRID_W - NA_WIN_C)
    col_ok = (kcol >= cstart) & (kcol < cstart + NA_WIN_C)
    for r in range(NA_ROWS):
        rs = min(max(r - NA_WR // 2, 0), NA_ROWS - NA_WR)
        rows = slice(r * GRID_W, (r + 1) * GRID_W)
        q = q_ref[rows, :]
        kl = kb[rs * GRID_W:(rs + NA_WR) * GRID_W]
        vl = vb[rs * GRID_W:(rs + NA_WR) * GRID_W]
        outs = []
        for half in range(2):
            qm = jnp.where(first if half == 0 else ~first, q, 0.0).astype(BF16)
            bias = []
            for w in range(0, NA_WR, 2):
                src = jnp.broadcast_to(tab_ref[half, rs + w - r + NA_WIN_R - 1], (GRID_W, LANES))
                bias.append(pltpu.roll(src, LANES - (NA_WIN_C - 1), axis=1, stride=1, stride_axis=0))
            s_loc = _dot_nt(qm, kl) * scale + jnp.concatenate(bias, axis=1)
            s_loc = jnp.where(col_ok, s_loc, -jnp.inf)
            s_ctx = _dot_nt(qm, kcb) * scale
            m = jnp.maximum(jnp.max(s_loc, axis=-1, keepdims=True), jnp.max(s_ctx, axis=-1, keepdims=True))
            e_loc = jnp.exp(s_loc - m)
            e_ctx = jnp.exp(s_ctx - m)
            den = jnp.sum(e_loc, axis=-1, keepdims=True) + jnp.sum(e_ctx, axis=-1, keepdims=True)
            outs.append(_dot((e_loc / den).astype(BF16), vl) + _dot((e_ctx / den).astype(BF16), vcb))
        o_ref[rows, :] = jnp.where(first, outs[0], outs[1])


def _na_bias_pairs(table):
    t = table.astype(F32)
    nc = 2 * NA_WIN_C - 1
    z = jnp.zeros(t[:, :-1].shape[:2] + (GRID_W - nc,), F32)
    return jnp.concatenate([t[:, :-1], z, t[:, 1:], z], axis=-1)[:, :, None, :]


def _pair_heads(cache):
    c = cache.reshape(DEC_BATCH, NA_HEADS // 2, 2, PAST_LEN, NA_HD)
    return c.transpose(0, 1, 3, 2, 4).reshape(DEC_BATCH, NA_HEADS // 2, PAST_LEN, LANES)


def _na_attention(proj, bias_table, cache_k, cache_v):
    o_p = pl.pallas_call(
        _na_prompt_kernel,
        grid=(BATCH,),
        in_specs=[
            pl.BlockSpec((SEQ, D_MODEL), lambda b: (b, 0)),
            pl.BlockSpec((SEQ, D_MODEL), lambda b: (b, 1)),
            pl.BlockSpec((SEQ, D_MODEL), lambda b: (b, 2)),
        ],
        out_specs=pl.BlockSpec((SEQ, D_MODEL), lambda b: (b, 0)),
        out_shape=jax.ShapeDtypeStruct((N_TOK, D_MODEL), F32),
        compiler_params=_params("arbitrary"),
        name="na_prompt",
    )(proj, proj, proj)
    rb = N_PROMPT_TOK // DEC_SEQ
    npair = NA_HEADS // 2
    cache = pl.BlockSpec((1, 1, PAST_LEN, LANES), lambda pr, b: (b, pr, 0, 0))
    return pl.pallas_call(
        _na_latent_kernel,
        grid=(npair, DEC_BATCH),
        in_specs=[
            pl.BlockSpec((DEC_SEQ, LANES), lambda pr, b: (rb + b, pr)),
            pl.BlockSpec((DEC_SEQ, LANES), lambda pr, b: (rb + b, npair + pr)),
            pl.BlockSpec((DEC_SEQ, LANES), lambda pr, b: (rb + b, 2 * npair + pr)),
            cache, cache,
            pl.BlockSpec((2, 2 * NA_WIN_R - 2, 1, LANES), lambda pr, b: (pr, 0, 0, 0)),
            pl.BlockSpec(memory_space=pl.ANY),
        ],
        out_specs=pl.BlockSpec((DEC_SEQ, LANES), lambda pr, b: (rb + b, pr)),
        out_shape=jax.ShapeDtypeStruct((N_TOK, D_MODEL), F32),
        input_output_aliases={6: 0},
        compiler_params=_params("arbitrary", "arbitrary"),
        name="na_latent",
    )(proj, proj, proj, _pair_heads(cache_k), _pair_heads(cache_v), _na_bias_pairs(bias_table), o_p)


def _layer_na(x, p, mod, j, final):
    i = N_MIXERS * j + 3
    proj = _in_proj(x, p['norm_w'][i], mod, p['na_w_in'][j].astype(BF16), 1024)
    o = _na_attention(proj, p['na_bias'][j], p['cache_na_k'][:, j], p['cache_na_v'][:, j])
    x = _out_proj(o, proj, 3, p['na_w_out'][j].astype(BF16), x, mod, p['final_norm_w'], final)
    return x, _heads_major(proj, 1, NA_HEADS), _heads_major(proj, 2, NA_HEADS)


def kernel(x_prompt, x_sample, state_ret, state_rwkv, cache_diff_k, cache_diff_v, cache_na_k, cache_na_v,
           c, c_ctx, norm_w, w_mod, b_mod, final_norm_w,
           ret_w_in, ret_decay, ret_gn, ret_w_out,
           rwkv_mu, rwkv_w_in, rwkv_w0, rwkv_wA, rwkv_wB, rwkv_a0, rwkv_aA, rwkv_aB,
           rwkv_kk, rwkv_ka, rwkv_rk, rwkv_gn, rwkv_w_out,
           diff_w_in, diff_lambda, diff_gn, diff_w_out,
           na_w_in, na_bias, na_w_out):
    p = dict(locals())
    cond = jnp.zeros((N_COND, D_MODEL), F32).at[0].set(c_ctx).at[1:1 + DEC_BATCH].set(c)
    mods = _modulation(cond, w_mod, b_mod)
    x = jnp.concatenate([x_prompt.reshape(N_PROMPT_TOK, D_MODEL), x_sample.reshape(N_SAMPLE_TOK, D_MODEL)])
    new = {n: [] for n in ('ret', 'rwkv', 'dk', 'dv', 'nk', 'nv')}
    for i in range(DEPTH):
        kind, j = i % N_MIXERS, i // N_MIXERS
        if kind == 0:
            x, st = _layer_ret(x, p, mods[i], j)
            new['ret'].append(st)
        elif kind == 1:
            x, st = _layer_rwkv(x, p, mods[i], j)
            new['rwkv'].append(st)
        elif kind == 2:
            x, ck, cv = _layer_diff(x, p, mods[i], j, i)
            new['dk'].append(ck)
            new['dv'].append(cv)
        else:
            x, ck, cv = _layer_na(x, p, mods[i], j, final=(i == DEPTH - 1))
            new['nk'].append(ck)
            new['nv'].append(cv)
    if DEPTH % N_MIXERS:
        raise NotImplementedError("the final norm is fused into the last neighbourhood-attention layer")
    cat = lambda xs: xs[0] if len(xs) == 1 else jnp.concatenate(xs, axis=1)
    return (x[:N_PROMPT_TOK].reshape(BATCH, SEQ, D_MODEL), x[N_PROMPT_TOK:].reshape(DEC_BATCH, DEC_SEQ, D_MODEL),
            cat(new['ret']), cat(new['rwkv']), cat(new['dk']), cat(new['dv']), cat(new['nk']), cat(new['nv']))
```

```python
import functools
import math

import jax
import jax.numpy as jnp
from jax import lax
from jax.experimental import pallas as pl
from jax.experimental.pallas import tpu as pltpu

F32 = jnp.float32
BF16 = jnp.bfloat16

D_MODEL = 1024
BATCH = 32
SEQ = 256
DEPTH = 4
N_MIXERS = 4
DEC_BATCH = 2
DEC_SEQ = 1024
PAST_LEN = 256
GRID_W = 64

RET_HEADS = 4
RET_DK = 256
RET_DV = 512
RET_QK = 1024
RET_V = 2048

RWKV_HD = 64
RWKV_HEADS = 16
RWKV_RANK = 64

DIFF_HEADS = 8
DIFF_HD = 64

NA_HEADS = 16
NA_HD = 64
NA_WIN_R = 8
NA_WIN_C = 16

ROPE_BASE = 10000.0
EPS = 1e-6
GN_EPS = 1e-5

N_PROMPT_TOK = BATCH * SEQ
N_SAMPLE_TOK = DEC_BATCH * DEC_SEQ
N_TOK = N_PROMPT_TOK + N_SAMPLE_TOK
N_COND = 8

LANES = 128
VMEM_LIMIT = 56 * 2 ** 20


def _params(*sem):
    return pltpu.CompilerParams(dimension_semantics=sem, vmem_limit_bytes=VMEM_LIMIT)


def _cond_of_tile(i, tm):
    npt = N_PROMPT_TOK // tm
    return jnp.where(i < npt, 0, 1 + (i - npt) // (DEC_SEQ // tm))


def _sigmoid(x):
    return 1.0 / (1.0 + jnp.exp(-x))


def _silu(x):
    return x * _sigmoid(x)


def _dot(a, b):
    return jnp.dot(a, b, preferred_element_type=F32)


def _dot_nt(a, b):
    return lax.dot_general(a, b, (((1,), (1,)), ((), ())), preferred_element_type=F32)


def _dot_tn(a, b):
    return lax.dot_general(a, b, (((0,), (0,)), ((), ())), preferred_element_type=F32)


def _softmax_rows(s):
    m = jnp.max(s, axis=-1, keepdims=True)
    e = jnp.exp(s - m)
    return e / jnp.sum(e, axis=-1, keepdims=True)


def _mod_kernel(c_ref, w_ref, b_ref, o_ref):
    s = _silu(c_ref[...])
    o_ref[0] = jnp.dot(s, w_ref[0], precision=lax.Precision.HIGHEST, preferred_element_type=F32) + b_ref[0]


def _modulation(cond, w_mod, b_mod):
    tn = D_MODEL
    out = pl.pallas_call(
        _mod_kernel,
        grid=(DEPTH, 3 * D_MODEL // tn),
        in_specs=[
            pl.BlockSpec((N_COND, D_MODEL), lambda l, j: (0, 0)),
            pl.BlockSpec((1, D_MODEL, tn), lambda l, j: (l, 0, j)),
            pl.BlockSpec((1, 1, tn), lambda l, j: (l, 0, j)),
        ],
        out_specs=pl.BlockSpec((1, N_COND, tn), lambda l, j: (l, 0, j)),
        out_shape=jax.ShapeDtypeStruct((DEPTH, N_COND, 3 * D_MODEL), F32),
        compiler_params=_params("arbitrary", "arbitrary"),
        name="modulation",
    )(cond, w_mod, b_mod.reshape(DEPTH, 1, 3 * D_MODEL))
    return out.reshape(DEPTH, N_COND, 3, 1, D_MODEL)


def _norm_mod(x, nw, mod_ref):
    ms = jnp.mean(x * x, axis=-1, keepdims=True)
    y = x * lax.rsqrt(ms + EPS) * nw
    return y * (1.0 + mod_ref[0, 1]) + mod_ref[0, 0]


IN_TM = 1024


def _in_proj_kernel(x_ref, nw_ref, mod_ref, w_ref, o_ref, h_ref):
    @pl.when(pl.program_id(1) == 0)
    def _():
        h_ref[...] = _norm_mod(x_ref[...], nw_ref[...], mod_ref).astype(BF16)

    o_ref[...] = _dot(h_ref[...], w_ref[...])


def _in_proj(x, norm_w, mod, w, tn):
    n = w.shape[1]
    return pl.pallas_call(
        _in_proj_kernel,
        grid=(N_TOK // IN_TM, n // tn),
        in_specs=[
            pl.BlockSpec((IN_TM, D_MODEL), lambda i, j: (i, 0)),
            pl.BlockSpec((1, D_MODEL), lambda i, j: (0, 0)),
            pl.BlockSpec((1, 3, 1, D_MODEL), lambda i, j: (_cond_of_tile(i, IN_TM), 0, 0, 0)),
            pl.BlockSpec((D_MODEL, tn), lambda i, j: (0, j)),
        ],
        out_specs=pl.BlockSpec((IN_TM, tn), lambda i, j: (i, j)),
        out_shape=jax.ShapeDtypeStruct((N_TOK, n), F32),
        scratch_shapes=[pltpu.VMEM((IN_TM, D_MODEL), BF16)],
        compiler_params=_params("arbitrary", "arbitrary"),
        name="in_proj",
    )(x, norm_w.reshape(1, D_MODEL), mod, w)


OUT_TM = 256


def _out_proj_kernel(o_ref, g_ref, w_ref, x_ref, mod_ref, fw_ref, y_ref, *, final):
    a = (o_ref[...] * _silu(g_ref[...])).astype(BF16)
    xn = x_ref[...] + mod_ref[0, 2] * _dot(a, w_ref[...])
    if final:
        ms = jnp.mean(xn * xn, axis=-1, keepdims=True)
        xn = xn * lax.rsqrt(ms + EPS) * fw_ref[...]
    y_ref[...] = xn


def _out_proj(o, g_arr, g_blk, w, x, mod, final_w, final):
    k = w.shape[0]
    return pl.pallas_call(
        functools.partial(_out_proj_kernel, final=final),
        grid=(N_TOK // OUT_TM,),
        in_specs=[
            pl.BlockSpec((OUT_TM, k), lambda i: (i, 0)),
            pl.BlockSpec((OUT_TM, k), lambda i: (i, g_blk)),
            pl.BlockSpec((k, D_MODEL), lambda i: (0, 0)),
            pl.BlockSpec((OUT_TM, D_MODEL), lambda i: (i, 0)),
            pl.BlockSpec((1, 3, 1, D_MODEL), lambda i: (_cond_of_tile(i, OUT_TM), 0, 0, 0)),
            pl.BlockSpec((1, D_MODEL), lambda i: (0, 0)),
        ],
        out_specs=pl.BlockSpec((OUT_TM, D_MODEL), lambda i: (i, 0)),
        out_shape=jax.ShapeDtypeStruct((N_TOK, D_MODEL), F32),
        compiler_params=_params("arbitrary"),
        name="out_proj",
    )(o, g_arr, w, x, mod, final_w.reshape(1, D_MODEL))


def _rope_tables(d):
    q = d // 4
    t = jnp.arange(DEC_SEQ)
    row = (t // GRID_W).astype(F32)
    col = (t % GRID_W).astype(F32)
    inv = ROPE_BASE ** (-jnp.arange(0, 2 * q, 2, dtype=F32) / (2 * q))
    ar = row[:, None] * inv[None, :]
    ac = col[:, None] * inv[None, :]
    z = jnp.zeros_like(ar)
    cos = jnp.concatenate([jnp.cos(ar), jnp.cos(ar), jnp.cos(ac), jnp.cos(ac)], axis=-1)
    sin_lo = jnp.concatenate([-jnp.sin(ar), z, -jnp.sin(ac), z], axis=-1)
    sin_hi = jnp.concatenate([z, jnp.sin(ar), z, jnp.sin(ac)], axis=-1)
    return cos, sin_lo, sin_hi


def _rope(x, cos, sin_lo, sin_hi, q):
    w = x.shape[-1]
    x_next = pltpu.roll(x, w - q, axis=1)
    x_prev = pltpu.roll(x, q, axis=1)
    return x * cos + x_next * sin_lo + x_prev * sin_hi


RET_QB = 256


def _ret_kernel(lg_ref, q_ref, k_ref, v_ref, gn_ref, *rest, seq, latent):
    if latent:
        cos_ref, slo_ref, shi_ref, s0_ref, _prev_ref, o_ref = rest
    else:
        o_ref, st_ref = rest
    h = pl.program_id(1)
    lgf = lg_ref[0, h]
    lgb = lg_ref[1, h]
    q = q_ref[...]
    k = k_ref[...]
    if latent:
        q = _rope(q, cos_ref[...], slo_ref[...], shi_ref[...], RET_DK // 4)
        k = _rope(k, cos_ref[...], slo_ref[...], shi_ref[...], RET_DK // 4)
    k = k * (RET_DK ** -0.5)
    kb = k.astype(BF16)
    vb = v_ref[...].astype(BF16)
    gn = gn_ref[...]
    for qi in range(seq // RET_QB):
        qblk = q[qi * RET_QB:(qi + 1) * RET_QB]
        s = _dot_nt(qblk.astype(BF16), kb)
        ii = lax.broadcasted_iota(jnp.int32, (RET_QB, seq), 0) + qi * RET_QB
        jj = lax.broadcasted_iota(jnp.int32, (RET_QB, seq), 1)
        gap = (ii - jj).astype(F32)
        dec = (jnp.where(gap >= 0, jnp.exp(lgf * jnp.maximum(gap, 0.0)), 0.0)
               + jnp.where(gap <= 0, jnp.exp(lgb * jnp.maximum(-gap, 0.0)), 0.0))
        o = _dot((s * dec).astype(BF16), vb)
        if latent:
            pos = (lax.broadcasted_iota(jnp.int32, (RET_QB, 1), 0) + qi * RET_QB).astype(F32)
            qf = qblk * jnp.exp(lgf * (pos + 1.0))
            qr = qblk * jnp.exp(lgb * (seq - pos))
            o = o + _dot(qf.astype(BF16), s0_ref[0, 0, 0, 0].astype(BF16))
            o = o + _dot(qr.astype(BF16), s0_ref[0, 0, 1, 0].astype(BF16))
        oc = o - jnp.mean(o, axis=-1, keepdims=True)
        o = oc * lax.rsqrt(jnp.mean(oc * oc, axis=-1, keepdims=True) + GN_EPS) * gn
        o_ref[qi * RET_QB:(qi + 1) * RET_QB, :] = o
    if not latent:
        pos = lax.broadcasted_iota(jnp.int32, (seq, 1), 0).astype(F32)
        kf = k * jnp.exp(lgf * (seq - 1.0 - pos))
        kr = k * jnp.exp(lgb * pos)
        st_ref[0, 0, 0, 0] = _dot_tn(kf.astype(BF16), vb)
        st_ref[0, 0, 1, 0] = _dot_tn(kr.astype(BF16), vb)


def _retention(p, log_g, gn_w, state_ret, j):
    smem = pl.BlockSpec(memory_space=pltpu.SMEM)
    gn = gn_w.reshape(1, RET_V)
    kq = RET_QK // RET_DK
    o_p, st = pl.pallas_call(
        functools.partial(_ret_kernel, seq=SEQ, latent=False),
        grid=(BATCH, RET_HEADS),
        in_specs=[
            smem,
            pl.BlockSpec((SEQ, RET_DK), lambda b, h: (b, h)),
            pl.BlockSpec((SEQ, RET_DK), lambda b, h: (b, kq + h)),
            pl.BlockSpec((SEQ, RET_DV), lambda b, h: (b, kq + h)),
            pl.BlockSpec((1, RET_DV), lambda b, h: (0, h)),
        ],
        out_specs=[
            pl.BlockSpec((SEQ, RET_DV), lambda b, h: (b, h)),
            pl.BlockSpec((1, 1, 2, 1, RET_DK, RET_DV), lambda b, h: (b, 0, 0, h, 0, 0)),
        ],
        out_shape=[
            jax.ShapeDtypeStruct((N_TOK, RET_V), F32),
            jax.ShapeDtypeStruct((BATCH, 1, 2, RET_HEADS, RET_DK, RET_DV), F32),
        ],
        compiler_params=_params("arbitrary", "arbitrary"),
        name="retention_prompt",
    )(log_g, p, p, p, gn)
    cos, slo, shi = _rope_tables(RET_DK)
    rb = N_PROMPT_TOK // DEC_SEQ
    full = pl.BlockSpec((DEC_SEQ, RET_DK), lambda b, h: (0, 0))
    o = pl.pallas_call(
        functools.partial(_ret_kernel, seq=DEC_SEQ, latent=True),
        grid=(DEC_BATCH, RET_HEADS),
        in_specs=[
            smem,
            pl.BlockSpec((DEC_SEQ, RET_DK), lambda b, h: (rb + b, h)),
            pl.BlockSpec((DEC_SEQ, RET_DK), lambda b, h: (rb + b, kq + h)),
            pl.BlockSpec((DEC_SEQ, RET_DV), lambda b, h: (rb + b, kq + h)),
            pl.BlockSpec((1, RET_DV), lambda b, h: (0, h)),
            full, full, full,
            pl.BlockSpec((1, 1, 2, 1, RET_DK, RET_DV), lambda b, h: (b, j, 0, h, 0, 0)),
            pl.BlockSpec(memory_space=pl.ANY),
        ],
        out_specs=pl.BlockSpec((DEC_SEQ, RET_DV), lambda b, h: (rb + b, h)),
        out_shape=jax.ShapeDtypeStruct((N_TOK, RET_V), F32),
        input_output_aliases={9: 0},
        compiler_params=_params("arbitrary", "arbitrary"),
        name="retention_latent",
    )(log_g, p, p, p, gn, cos, slo, shi, state_ret, o_p)
    return o, st


def _layer_ret(x, p, mod, j):
    i = N_MIXERS * j + 0
    proj = _in_proj(x, p['norm_w'][i], mod, p['ret_w_in'][j].astype(BF16), 1024)
    log_g = jax.nn.log_sigmoid(p['ret_decay'][j].astype(F32))
    o, st = _retention(proj, log_g, p['ret_gn'][j], p['state_ret'], j)
    x = _out_proj(o, proj, (2 * RET_QK + RET_V) // RET_V, p['ret_w_out'][j].astype(BF16), x, mod,
                  p['final_norm_w'], False)
    return x, st


RW_TM = 512
RW_HALO = 8
RW_C = 64
RW_LOCK = 4


def _rwkv_prep_kernel(x_ref, xp_ref, xn_ref, nw_ref, mod_ref, mu_ref, wa_ref, aa_ref, wb_ref, ab_ref,
                      w0_ref, a0_ref, xm_ref, lw_ref, a_ref):
    i = pl.program_id(0)
    nw = nw_ref[...]
    h = _norm_mod(x_ref[...], nw, mod_ref)
    h_before = _norm_mod(xp_ref[RW_HALO - 1:RW_HALO, :], nw, mod_ref)
    h_after = _norm_mod(xn_ref[0:1, :], nw, mod_ref)
    seq = jnp.where(i < N_PROMPT_TOK // RW_TM, SEQ, DEC_SEQ)
    row = lax.broadcasted_iota(jnp.int32, (RW_TM, 1), 0)
    t = (row + i * RW_TM) & (seq - 1)
    prev = jnp.where(row == 0, h_before, pltpu.roll(h, 1, axis=0))
    nxt = jnp.where(row == RW_TM - 1, h_after, pltpu.roll(h, RW_TM - 1, axis=0))
    prev = jnp.where(t == 0, 0.0, prev)
    nxt = jnp.where(t == seq - 1, 0.0, nxt)
    xx = 0.5 * (prev + nxt) - h
    for n, m in enumerate((0, 2, 3, 5)):
        xm_ref[n] = (h + xx * mu_ref[m:m + 1, :]).astype(BF16)
    xw = (h + xx * mu_ref[1:2, :]).astype(BF16)
    xa = (h + xx * mu_ref[4:5, :]).astype(BF16)
    lw = jnp.tanh(_dot(xw, wa_ref[...])).astype(BF16)
    la = _dot(xa, aa_ref[...]).astype(BF16)
    for dr in range(2):
        wl = w0_ref[dr:dr + 1, :] + _dot(lw, wb_ref[dr])
        lw_ref[dr] = -math.exp(-0.5) * _sigmoid(wl)
        a_ref[dr] = _sigmoid(a0_ref[dr:dr + 1, :] + _dot(la, ab_ref[dr]))


def _rwkv_prep(x, norm_w, mod, mu, wa2, aa2, wb_pad, ab_pad, w0, a0):
    nt = N_TOK // RW_TM
    hb = RW_TM // RW_HALO
    last = N_TOK // RW_HALO - 1
    full2 = lambda shape: pl.BlockSpec(shape, lambda i: (0, 0))
    full3 = lambda shape: pl.BlockSpec(shape, lambda i: (0, 0, 0))
    return pl.pallas_call(
        _rwkv_prep_kernel,
        grid=(nt,),
        in_specs=[
            pl.BlockSpec((RW_TM, D_MODEL), lambda i: (i, 0)),
            pl.BlockSpec((RW_HALO, D_MODEL), lambda i: (jnp.maximum(i * hb - 1, 0), 0)),
            pl.BlockSpec((RW_HALO, D_MODEL), lambda i: (jnp.minimum((i + 1) * hb, last), 0)),
            full2((1, D_MODEL)),
            pl.BlockSpec((1, 3, 1, D_MODEL), lambda i: (_cond_of_tile(i, RW_TM), 0, 0, 0)),
            full2((6, D_MODEL)),
            full2((D_MODEL, 2 * RWKV_RANK)),
            full2((D_MODEL, 2 * RWKV_RANK)),
            full3((2, 2 * RWKV_RANK, D_MODEL)),
            full3((2, 2 * RWKV_RANK, D_MODEL)),
            full2((2, D_MODEL)),
            full2((2, D_MODEL)),
        ],
        out_specs=[
            pl.BlockSpec((4, RW_TM, D_MODEL), lambda i: (0, i, 0)),
            pl.BlockSpec((2, RW_TM, D_MODEL), lambda i: (0, i, 0)),
            pl.BlockSpec((2, RW_TM, D_MODEL), lambda i: (0, i, 0)),
        ],
        out_shape=[
            jax.ShapeDtypeStruct((4, N_TOK, D_MODEL), BF16),
            jax.ShapeDtypeStruct((2, N_TOK, D_MODEL), F32),
            jax.ShapeDtypeStruct((2, N_TOK, D_MODEL), F32),
        ],
        compiler_params=_params("arbitrary"),
        name="rwkv_prep",
    )(x, x, x, norm_w.reshape(1, D_MODEL), mod, mu, wa2, aa2, wb_pad, ab_pad, w0, a0)


def _bmm_kernel(a_ref, w_ref, o_ref):
    o_ref[0] = _dot(a_ref[0], w_ref[...])


def _rwkv_rkvg(xm, w):
    tm = 1024
    return pl.pallas_call(
        _bmm_kernel,
        grid=(4, N_TOK // tm),
        in_specs=[
            pl.BlockSpec((1, tm, D_MODEL), lambda n, i: (n, i, 0)),
            pl.BlockSpec((D_MODEL, D_MODEL), lambda n, i: (0, n)),
        ],
        out_specs=pl.BlockSpec((1, tm, D_MODEL), lambda n, i: (n, i, 0)),
        out_shape=jax.ShapeDtypeStruct((4, N_TOK, D_MODEL), F32),
        compiler_params=_params("arbitrary", "arbitrary"),
        name="rwkv_rkvg",
    )(xm, w)


def _head_sum(x, first):
    s0 = jnp.sum(jnp.where(first, x, 0.0), axis=-1, keepdims=True)
    s1 = jnp.sum(jnp.where(first, 0.0, x), axis=-1, keepdims=True)
    return jnp.where(first, s0, s1)


def _stack_heads(x, first):
    return jnp.concatenate([jnp.where(first, x, 0.0), jnp.where(first, 0.0, x)], axis=0)


def _cumsum_rows(tri, x):
    hi = x.astype(BF16)
    r1 = x - hi.astype(F32)
    mid = r1.astype(BF16)
    lo = (r1 - mid.astype(F32)).astype(BF16)
    return _dot(tri, hi) + _dot(tri, mid) + _dot(tri, lo)


def _rwkv_chunk_kernel(*refs, seq, zero_init):
    if zero_init:
        (rkv_ref, lw_ref, a_ref, kkp_ref, kap_ref, rkp_ref, gn_ref, o_ref, st_ref,
         kk_scr, y_scr, tar_scr, lrb_scr, b2_scr, w2_scr, yl_scr, kv_scr, pc_scr) = refs
    else:
        (rkv_ref, lw_ref, a_ref, kkp_ref, kap_ref, rkp_ref, gn_ref, s0_ref, _prev_ref, o_ref,
         kk_scr, y_scr, tar_scr, lrb_scr, b2_scr, w2_scr, yl_scr, kv_scr, pc_scr) = refs
    c_len = RW_C
    n_ch = seq // c_len
    rows2 = 2 * c_len
    first = _first_half_lanes()
    kap = kap_ref[...]

    kk = rkv_ref[1] * kkp_ref[...]
    kk_scr[...] = kk * lax.rsqrt(jnp.maximum(_head_sum(kk * kk, first), 1e-12))

    rr = lax.broadcasted_iota(jnp.int32, (rows2, rows2), 0)
    cc = lax.broadcasted_iota(jnp.int32, (rows2, rows2), 1)
    eye = (rr == cc).astype(F32)
    tr = lax.broadcasted_iota(jnp.int32, (c_len, c_len), 0)
    tc = lax.broadcasted_iota(jnp.int32, (c_len, c_len), 1)

    def same(shift):
        return (rr >> shift) == (cc >> shift)

    head = same(6)
    strict = (head & (cc < rr), head & (cc > rr))
    incl = (head & (cc <= rr), head & (cc >= rr))
    tri = ((tc <= tr).astype(BF16), (tc >= tr).astype(BF16))
    last = (c_len - 1, 0)

    def phase1(chains):
        dirs = [dr for dr, _ in chains]
        rows = [pl.ds(pl.multiple_of(c * c_len, c_len), c_len) for _, c in chains]
        lw = [lw_ref[dr, rw, :] for dr, rw in zip(dirs, rows)]
        cum = [_cumsum_rows(tri[dr], x) for dr, x in zip(dirs, lw)]
        a2, r2, b2, k2, v2, pc = [], [], [], [], [], []
        for dr, rw, lw_c, cum_c in zip(dirs, rows, lw, cum):
            a = a_ref[dr, rw, :]
            k = rkv_ref[1, rw, :]
            kk_c = kk_scr[rw, :]
            e_inc = jnp.exp(cum_c)
            e_inv = jnp.exp(-cum_c)
            a2.append(_stack_heads(-kk_c * jnp.exp(cum_c - lw_c), first).astype(BF16))
            r2.append(_stack_heads(rkv_ref[0, rw, :] * e_inc, first).astype(BF16))
            b2.append(_stack_heads(kk_c * a * e_inv, first).astype(BF16))
            k2.append(_stack_heads(k * (1.0 + (a - 1.0) * kap) * e_inv, first).astype(BF16))
            v2.append(_stack_heads(rkv_ref[2, rw, :], first).astype(BF16))
            pc.append(e_inc[last[dr]:last[dr] + 1, :])
        g = [_dot_nt(jnp.concatenate([x, y], axis=0), jnp.concatenate([z, w], axis=0))
             for x, y, z, w in zip(a2, r2, b2, k2)]
        l_ab = [jnp.where(strict[dr], x[:rows2, :rows2], 0.0) for dr, x in zip(dirs, g)]
        t = [eye + jnp.where(same(1), x, 0.0) for x in l_ab]
        side = {}
        for shift in range(1, 6):
            sib = same(shift + 1) & ~same(shift)
            tb = [x.astype(BF16) for x in t]
            mid = [_dot(jnp.where(sib, x, 0.0).astype(BF16), y) for x, y in zip(l_ab, tb)]
            if shift == 1:
                side['lv'] = [_dot(jnp.where(strict[dr], x[:rows2, rows2:], 0.0).astype(BF16), y)
                              for dr, x, y in zip(dirs, g, v2)]
            elif shift == 2:
                side['yl'] = [_dot(jnp.where(incl[dr], x[rows2:, rows2:], 0.0).astype(BF16), y)
                              for dr, x, y in zip(dirs, g, v2)]
            elif shift == 3:
                side['kv'] = [_dot_tn(x, y) for x, y in zip(v2, k2)]
            t = [x + _dot(y, z.astype(BF16)) for x, y, z in zip(t, tb, mid)]
        tb = [x.astype(BF16) for x in t]
        ta = [_dot(x, y) for x, y in zip(tb, a2)]
        w2 = [_dot(x, y.astype(BF16)) for x, y in zip(tb, side['lv'])]
        for i, (dr, c) in enumerate(chains):
            n = dr * n_ch + c
            tar_scr[n, :rows2, :] = ta[i].astype(BF16)
            tar_scr[n, rows2:, :] = r2[i]
            w2_scr[n] = w2[i]
            yl_scr[n] = side['yl'][i]
            kv_scr[n] = side['kv'][i]
            lrb_scr[n] = jnp.where(incl[dr], g[i][rows2:, :rows2], 0.0).astype(BF16)
            b2_scr[n] = b2[i]
            pc_scr[n] = pc[i]

    def body1(grp, carry):
        phase1([(dr, grp * RW_LOCK + j) for j in range(RW_LOCK) for dr in range(2)])
        return carry

    if n_ch == RW_LOCK:
        body1(0, 0)
    else:
        lax.fori_loop(0, n_ch // RW_LOCK, body1, 0)

    def body2(i, carry):
        cs = (i, n_ch - 1 - i)
        ns = [dr * n_ch + c for dr, c in enumerate(cs)]
        x = [_dot_nt(tar_scr[n], s2.astype(BF16)) for n, s2 in zip(ns, carry)]
        u2 = [(xx[:rows2] + w2_scr[n]).astype(BF16) for n, xx in zip(ns, x)]
        upd = [_dot_tn(u, b2_scr[n]) for n, u in zip(ns, u2)]
        yb = [_dot(lrb_scr[n], u) for n, u in zip(ns, u2)]
        out = []
        for dr, (c, n) in enumerate(zip(cs, ns)):
            y2 = x[dr][rows2:] + yb[dr] + yl_scr[n]
            y_scr[dr, pl.ds(pl.multiple_of(c * c_len, c_len), c_len), :] = y2[:c_len] + y2[c_len:]
            out.append((carry[dr] + upd[dr] + kv_scr[n]) * pc_scr[n])
        return tuple(out)

    if zero_init:
        init = (jnp.zeros((rows2, LANES), F32),) * 2
    else:
        init = (s0_ref[0, 0, 0], s0_ref[0, 1, 0])
    s_f, s_b = lax.fori_loop(0, n_ch, body2, init)

    y = y_scr[0] + y_scr[1]
    yc = y - _head_sum(y, first) * (1.0 / RWKV_HD)
    o = yc * lax.rsqrt(_head_sum(yc * yc, first) * (1.0 / RWKV_HD) + GN_EPS) * gn_ref[...]
    r = rkv_ref[0]
    k = rkv_ref[1]
    v = rkv_ref[2]
    for dr in range(2):
        kd = k * (1.0 + (a_ref[dr] - 1.0) * kap)
        o = o + _head_sum(r * kd * rkp_ref[...], first) * v
    o_ref[...] = o

    if zero_init:
        for dr, s2 in enumerate((s_f, s_b)):
            st_ref[0, 0, dr, 0] = s2[:RWKV_HD, :RWKV_HD]
            st_ref[0, 0, dr, 1] = s2[RWKV_HD:, RWKV_HD:]


def _rwkv_chunked(rkvg, lw, a, kkp, kap, rkp, gn, s0_pairs):
    npair = RWKV_HEADS // 2
    par = lambda *_: pl.BlockSpec((1, LANES), lambda s, p: (0, p))

    def scratch(seq):
        n = 2 * (seq // RW_C)
        r2 = 2 * RW_C
        return [
            pltpu.VMEM((seq, LANES), F32), pltpu.VMEM((2, seq, LANES), F32),
            pltpu.VMEM((n, 2 * r2, LANES), BF16), pltpu.VMEM((n, r2, r2), BF16), pltpu.VMEM((n, r2, LANES), BF16),
            pltpu.VMEM((n, r2, LANES), F32), pltpu.VMEM((n, r2, LANES), F32), pltpu.VMEM((n, r2, LANES), F32),
            pltpu.VMEM((n, 1, LANES), F32),
        ]

    def seq_specs(seq, rb):
        return [
            pl.BlockSpec((3, seq, LANES), lambda s, p: (0, rb + s, p)),
            pl.BlockSpec((2, seq, LANES), lambda s, p: (0, rb + s, p)),
            pl.BlockSpec((2, seq, LANES), lambda s, p: (0, rb + s, p)),
            par(), par(), par(), par(),
        ]

    pars = [u.reshape(1, D_MODEL) for u in (kkp, kap, rkp, gn)]
    o_p, st = pl.pallas_call(
        functools.partial(_rwkv_chunk_kernel, seq=SEQ, zero_init=True),
        grid=(BATCH, npair),
        in_specs=seq_specs(SEQ, 0),
        out_specs=[
            pl.BlockSpec((SEQ, LANES), lambda s, p: (s, p)),
            pl.BlockSpec((1, 1, 2, 2, RWKV_HD, RWKV_HD), lambda s, p: (s, 0, 0, p, 0, 0)),
        ],
        out_shape=[
            jax.ShapeDtypeStruct((N_TOK, D_MODEL), F32),
            jax.ShapeDtypeStruct((BATCH, 1, 2, RWKV_HEADS, RWKV_HD, RWKV_HD), F32),
        ],
        scratch_shapes=scratch(SEQ),
        compiler_params=_params("arbitrary", "arbitrary"),
        name="rwkv_prompt",
    )(rkvg, lw, a, *pars)
    rb = N_PROMPT_TOK // DEC_SEQ
    o = pl.pallas_call(
        functools.partial(_rwkv_chunk_kernel, seq=DEC_SEQ, zero_init=False),
        grid=(DEC_BATCH, npair),
        in_specs=seq_specs(DEC_SEQ, rb) + [
            pl.BlockSpec((1, 2, 1, LANES, LANES), lambda s, p: (s, 0, p, 0, 0)),
            pl.BlockSpec(memory_space=pl.ANY),
        ],
        out_specs=pl.BlockSpec((DEC_SEQ, LANES), lambda s, p: (rb + s, p)),
        out_shape=jax.ShapeDtypeStruct((N_TOK, D_MODEL), F32),
        input_output_aliases={8: 0},
        scratch_shapes=scratch(DEC_SEQ),
        compiler_params=_params("arbitrary", "arbitrary"),
        name="rwkv_latent",
    )(rkvg, lw, a, *pars, s0_pairs, o_p)
    return o, st


def _state_pairs(s0):
    s = s0.reshape(DEC_BATCH, 2, RWKV_HEADS // 2, 2, RWKV_HD, RWKV_HD)
    z = jnp.zeros_like(s[:, :, :, 0])
    top = jnp.concatenate([s[:, :, :, 0], z], axis=-1)
    bot = jnp.concatenate([z, s[:, :, :, 1]], axis=-1)
    return jnp.concatenate([top, bot], axis=-2)


def _layer_rwkv(x, p, mod, j):
    i = N_MIXERS * j + 1
    wa, wb, aa, ab = p['rwkv_wA'][j], p['rwkv_wB'][j], p['rwkv_aA'][j], p['rwkv_aB'][j]
    z = jnp.zeros_like(wb[0])
    wa2 = jnp.concatenate([wa[0], wa[1]], axis=1).astype(BF16)
    aa2 = jnp.concatenate([aa[0], aa[1]], axis=1).astype(BF16)
    wb_pad = jnp.stack([jnp.concatenate([wb[0], z]), jnp.concatenate([z, wb[1]])]).astype(BF16)
    ab_pad = jnp.stack([jnp.concatenate([ab[0], z]), jnp.concatenate([z, ab[1]])]).astype(BF16)
    xm, lw, a = _rwkv_prep(x, p['norm_w'][i], mod, p['rwkv_mu'][j], wa2, aa2, wb_pad, ab_pad,
                           p['rwkv_w0'][j], p['rwkv_a0'][j])
    rkvg = _rwkv_rkvg(xm, p['rwkv_w_in'][j].astype(BF16))
    o, st = _rwkv_chunked(rkvg, lw, a, p['rwkv_kk'][j], p['rwkv_ka'][j], p['rwkv_rk'][j], p['rwkv_gn'][j],
                          _state_pairs(p['state_rwkv'][:, j]))
    x = _out_proj(o, rkvg[3], 0, p['rwkv_w_out'][j].astype(BF16), x, mod, p['final_norm_w'], False)
    return x, st


DIFF_W = 2 * DIFF_HD
ATT_QB = 256


def _first_half_lanes():
    return lax.broadcasted_iota(jnp.int32, (1, LANES), 1) < LANES // 2


def _diff_lambda(lam_ref, lam_init):
    lp = lam_ref[...]
    return (jnp.exp(jnp.sum(lp[0:1] * lp[1:2], keepdims=True))
            - jnp.exp(jnp.sum(lp[2:3] * lp[3:4], keepdims=True)) + lam_init)


def _diff_head(q, key_sets, lam, gn, lam_init):
    first = _first_half_lanes()
    scale = DIFF_HD ** -0.5
    probs = []
    for comp in range(2):
        qm = jnp.where(first if comp == 0 else ~first, q, 0.0).astype(BF16)
        s = [_dot_nt(qm, kb) * scale for kb, _ in key_sets]
        m = functools.reduce(jnp.maximum, [jnp.max(u, axis=-1, keepdims=True) for u in s])
        e = [jnp.exp(u - m) for u in s]
        den = functools.reduce(lambda x, y: x + y, [jnp.sum(u, axis=-1, keepdims=True) for u in e])
        probs.append([u / den for u in e])
    o = None
    for n, (_, vb) in enumerate(key_sets):
        part = _dot((probs[0][n] - lam * probs[1][n]).astype(BF16), vb)
        o = part if o is None else o + part
    o = o * lax.rsqrt(jnp.mean(o * o, axis=-1, keepdims=True) + EPS) * gn
    return o * (1.0 - lam_init)


def _diff_prompt_kernel(lam_ref, q_ref, k_ref, v_ref, gn_ref, o_ref, *, lam_init):
    lam = _diff_lambda(lam_ref, lam_init)
    for h in range(DIFF_HEADS):
        sl = slice(h * DIFF_W, (h + 1) * DIFF_W)
        keys = [(k_ref[:, sl].astype(BF16), v_ref[:, sl].astype(BF16))]
        o_ref[:, sl] = _diff_head(q_ref[:, sl], keys, lam, gn_ref[:, sl], lam_init)


def _diff_latent_kernel(lam_ref, q_ref, k_ref, v_ref, ck_ref, cv_ref, cos_ref, slo_ref, shi_ref, gn_ref,
                        _prev_ref, o_ref, *, lam_init):
    lam = _diff_lambda(lam_ref, lam_init)
    tabs = (cos_ref[...], slo_ref[...], shi_ref[...])
    q = _rope(q_ref[...], *tabs, DIFF_HD // 4)
    k = _rope(k_ref[...], *tabs, DIFF_HD // 4)
    keys = [(k.astype(BF16), v_ref[...].astype(BF16)),
            (ck_ref[0, 0, 0].astype(BF16), cv_ref[0, 0, 0].astype(BF16))]
    gn = gn_ref[...]
    for qi in range(DEC_SEQ // ATT_QB):
        sl = slice(qi * ATT_QB, (qi + 1) * ATT_QB)
        o_ref[sl, :] = _diff_head(q[sl], keys, lam, gn, lam_init)


def _diff_attention(proj, lam_p, gn_w, cache_k, cache_v, j, lam_init):
    gn = gn_w.reshape(1, D_MODEL)
    lam_spec = pl.BlockSpec((4, DIFF_HD), lambda *_: (0, 0))
    o_p = pl.pallas_call(
        functools.partial(_diff_prompt_kernel, lam_init=lam_init),
        grid=(BATCH,),
        in_specs=[
            lam_spec,
            pl.BlockSpec((SEQ, D_MODEL), lambda b: (b, 0)),
            pl.BlockSpec((SEQ, D_MODEL), lambda b: (b, 1)),
            pl.BlockSpec((SEQ, D_MODEL), lambda b: (b, 2)),
            pl.BlockSpec((1, D_MODEL), lambda b: (0, 0)),
        ],
        out_specs=pl.BlockSpec((SEQ, D_MODEL), lambda b: (b, 0)),
        out_shape=jax.ShapeDtypeStruct((N_TOK, D_MODEL), F32),
        compiler_params=_params("arbitrary"),
        name="diff_prompt",
    )(lam_p, proj, proj, proj, gn)
    cos, slo, shi = (jnp.concatenate([u, u], axis=-1) for u in _rope_tables(DIFF_HD))
    rb = N_PROMPT_TOK // DEC_SEQ
    nh = DIFF_HEADS
    tab = pl.BlockSpec((DEC_SEQ, DIFF_W), lambda b, h: (0, 0))
    cache = pl.BlockSpec((1, 1, 1, PAST_LEN, DIFF_W), lambda b, h: (b, j, h, 0, 0))
    return pl.pallas_call(
        functools.partial(_diff_latent_kernel, lam_init=lam_init),
        grid=(DEC_BATCH, nh),
        in_specs=[
            lam_spec,
            pl.BlockSpec((DEC_SEQ, DIFF_W), lambda b, h: (rb + b, h)),
            pl.BlockSpec((DEC_SEQ, DIFF_W), lambda b, h: (rb + b, nh + h)),
            pl.BlockSpec((DEC_SEQ, DIFF_W), lambda b, h: (rb + b, 2 * nh + h)),
            cache, cache, tab, tab, tab,
            pl.BlockSpec((1, DIFF_W), lambda b, h: (0, h)),
            pl.BlockSpec(memory_space=pl.ANY),
        ],
        out_specs=pl.BlockSpec((DEC_SEQ, DIFF_W), lambda b, h: (rb + b, h)),
        out_shape=jax.ShapeDtypeStruct((N_TOK, D_MODEL), F32),
        input_output_aliases={10: 0},
        compiler_params=_params("arbitrary", "arbitrary"),
        name="diff_latent",
    )(lam_p, proj, proj, proj, cache_k, cache_v, cos, slo, shi, gn, o_p)


def _heads_major(proj, col, heads):
    t = proj[:N_PROMPT_TOK, col * D_MODEL:(col + 1) * D_MODEL]
    return t.reshape(BATCH, SEQ, heads, D_MODEL // heads).transpose(0, 2, 1, 3)[:, None]


def _layer_diff(x, p, mod, j, i):
    lam_init = 0.8 - 0.6 * math.exp(-0.3 * i)
    proj = _in_proj(x, p['norm_w'][i], mod, p['diff_w_in'][j].astype(BF16), 1024)
    o = _diff_attention(proj, p['diff_lambda'][j], p['diff_gn'][j], p['cache_diff_k'], p['cache_diff_v'], j, lam_init)
    x = _out_proj(o, proj, 3, p['diff_w_out'][j].astype(BF16), x, mod, p['final_norm_w'], False)
    return x, _heads_major(proj, 1, DIFF_HEADS), _heads_major(proj, 2, DIFF_HEADS)


NA_ROWS = DEC_SEQ // GRID_W
NA_WR = min(NA_WIN_R, NA_ROWS)
NA_LOC = NA_WR * GRID_W


def _na_prompt_kernel(q_ref, k_ref, v_ref, o_ref):
    first = _first_half_lanes()
    scale = NA_HD ** -0.5
    for pr in range(NA_HEADS // 2):
        sl = slice(pr * LANES, (pr + 1) * LANES)
        q = q_ref[:, sl]
        kb = k_ref[:, sl].astype(BF16)
        vb = v_ref[:, sl].astype(BF16)
        outs = []
        for half in range(2):
            qm = jnp.where(first if half == 0 else ~first, q, 0.0).astype(BF16)
            pr_ = _softmax_rows(_dot_nt(qm, kb) * scale).astype(BF16)
            outs.append(_dot(pr_, vb))
        o_ref[:, sl] = jnp.where(first, outs[0], outs[1])


def _na_latent_kernel(q_ref, k_ref, v_ref, kc_ref, vc_ref, tab_ref, _prev_ref, o_ref):
    first = _first_half_lanes()
    scale = NA_HD ** -0.5
    kb = k_ref[...].astype(BF16)
    vb = v_ref[...].astype(BF16)
    kcb = kc_ref[0, 0].astype(BF16)
    vcb = vc_ref[0, 0].astype(BF16)
    qcol = lax.broadcasted_iota(jnp.int32, (GRID_W, NA_LOC), 0)
    kcol = lax.broadcasted_iota(jnp.int32, (GRID_W, NA_LOC), 1) & (GRID_W - 1)
    cstart = jnp.clip(qcol - NA_WIN_C // 2, 0, GRID_W - NA_WIN_C)
    col_ok = (kcol >= cstart) & (kcol < cstart + NA_WIN_C)
    for r in range(NA_ROWS):
        rs = min(max(r - NA_WR // 2, 0), NA_ROWS - NA_WR)
        rows = slice(r * GRID_W, (r + 1) * GRID_W)
        q = q_ref[rows, :]
        kl = kb[rs * GRID_W:(rs + NA_WR) * GRID_W]
        vl = vb[rs * GRID_W:(rs + NA_WR) * GRID_W]
        outs = []
        for half in range(2):
            qm = jnp.where(first if half == 0 else ~first, q, 0.0).astype(BF16)
            bias = []
            for w in range(0, NA_WR, 2):
                src = jnp.broadcast_to(tab_ref[half, rs + w - r + NA_WIN_R - 1], (GRID_W, LANES))
                bias.append(pltpu.roll(src, LANES - (NA_WIN_C - 1), axis=1, stride=1, stride_axis=0))
            s_loc = _dot_nt(qm, kl) * scale + jnp.concatenate(bias, axis=1)
            s_loc = jnp.where(col_ok, s_loc, -jnp.inf)
            s_ctx = _dot_nt(qm, kcb) * scale
            m = jnp.maximum(jnp.max(s_loc, axis=-1, keepdims=True), jnp.max(s_ctx, axis=-1, keepdims=True))
            e_loc = jnp.exp(s_loc - m)
            e_ctx = jnp.exp(s_ctx - m)
            den = jnp.sum(e_loc, axis=-1, keepdims=True) + jnp.sum(e_ctx, axis=-1, keepdims=True)
            outs.append(_dot((e_loc / den).astype(BF16), vl) + _dot((e_ctx / den).astype(BF16), vcb))
        o_ref[rows, :] = jnp.where(first, outs[0], outs[1])


def _na_bias_pairs(table):
    t = table.astype(F32)
    nc = 2 * NA_WIN_C - 1
    z = jnp.zeros(t[:, :-1].shape[:2] + (GRID_W - nc,), F32)
    return jnp.concatenate([t[:, :-1], z, t[:, 1:], z], axis=-1)[:, :, None, :]


def _pair_heads(cache):
    c = cache.reshape(DEC_BATCH, NA_HEADS // 2, 2, PAST_LEN, NA_HD)
    return c.transpose(0, 1, 3, 2, 4).reshape(DEC_BATCH, NA_HEADS // 2, PAST_LEN, LANES)


def _na_attention(proj, bias_table, cache_k, cache_v):
    o_p = pl.pallas_call(
        _na_prompt_kernel,
        grid=(BATCH,),
        in_specs=[
            pl.BlockSpec((SEQ, D_MODEL), lambda b: (b, 0)),
            pl.BlockSpec((SEQ, D_MODEL), lambda b: (b, 1)),
            pl.BlockSpec((SEQ, D_MODEL), lambda b: (b, 2)),
        ],
        out_specs=pl.BlockSpec((SEQ, D_MODEL), lambda b: (b, 0)),
        out_shape=jax.ShapeDtypeStruct((N_TOK, D_MODEL), F32),
        compiler_params=_params("arbitrary"),
        name="na_prompt",
    )(proj, proj, proj)
    rb = N_PROMPT_TOK // DEC_SEQ
    npair = NA_HEADS // 2
    cache = pl.BlockSpec((1, 1, PAST_LEN, LANES), lambda pr, b: (b, pr, 0, 0))
    return pl.pallas_call(
        _na_latent_kernel,
        grid=(npair, DEC_BATCH),
        in_specs=[
            pl.BlockSpec((DEC_SEQ, LANES), lambda pr, b: (rb + b, pr)),
            pl.BlockSpec((DEC_SEQ, LANES), lambda pr, b: (rb + b, npair + pr)),
            pl.BlockSpec((DEC_SEQ, LANES), lambda pr, b: (rb + b, 2 * npair + pr)),
            cache, cache,
            pl.BlockSpec((2, 2 * NA_WIN_R - 2, 1, LANES), lambda pr, b: (pr, 0, 0, 0)),
            pl.BlockSpec(memory_space=pl.ANY),
        ],
        out_specs=pl.BlockSpec((DEC_SEQ, LANES), lambda pr, b: (rb + b, pr)),
        out_shape=jax.ShapeDtypeStruct((N_TOK, D_MODEL), F32),
        input_output_aliases={6: 0},
        compiler_params=_params("arbitrary", "arbitrary"),
        name="na_latent",
    )(proj, proj, proj, _pair_heads(cache_k), _pair_heads(cache_v), _na_bias_pairs(bias_table), o_p)


def _layer_na(x, p, mod, j, final):
    i = N_MIXERS * j + 3
    proj = _in_proj(x, p['norm_w'][i], mod, p['na_w_in'][j].astype(BF16), 1024)
    o = _na_attention(proj, p['na_bias'][j], p['cache_na_k'][:, j], p['cache_na_v'][:, j])
    x = _out_proj(o, proj, 3, p['na_w_out'][j].astype(BF16), x, mod, p['final_norm_w'], final)
    return x, _heads_major(proj, 1, NA_HEADS), _heads_major(proj, 2, NA_HEADS)


def kernel(x_prompt, x_sample, state_ret, state_rwkv, cache_diff_k, cache_diff_v, cache_na_k, cache_na_v,
           c, c_ctx, norm_w, w_mod, b_mod, final_norm_w,
           ret_w_in, ret_decay, ret_gn, ret_w_out,
           rwkv_mu, rwkv_w_in, rwkv_w0, rwkv_wA, rwkv_wB, rwkv_a0, rwkv_aA, rwkv_aB,
           rwkv_kk, rwkv_ka, rwkv_rk, rwkv_gn, rwkv_w_out,
           diff_w_in, diff_lambda, diff_gn, diff_w_out,
           na_w_in, na_bias, na_w_out):
    p = dict(locals())
    cond = jnp.zeros((N_COND, D_MODEL), F32).at[0].set(c_ctx).at[1:1 + DEC_BATCH].set(c)
    mods = _modulation(cond, w_mod, b_mod)
    x = jnp.concatenate([x_prompt.reshape(N_PROMPT_TOK, D_MODEL), x_sample.reshape(N_SAMPLE_TOK, D_MODEL)])
    new = {n: [] for n in ('ret', 'rwkv', 'dk', 'dv', 'nk', 'nv')}
    for i in range(DEPTH):
        kind, j = i % N_MIXERS, i // N_MIXERS
        if kind == 0:
            x, st = _layer_ret(x, p, mods[i], j)
            new['ret'].append(st)
        elif kind == 1:
            x, st = _layer_rwkv(x, p, mods[i], j)
            new['rwkv'].append(st)
        elif kind == 2:
            x, ck, cv = _layer_diff(x, p, mods[i], j, i)
            new['dk'].append(ck)
            new['dv'].append(cv)
        else:
            x, ck, cv = _layer_na(x, p, mods[i], j, final=(i == DEPTH - 1))
            new['nk'].append(ck)
            new['nv'].append(cv)
    if DEPTH % N_MIXERS:
        raise NotImplementedError("the final norm is fused into the last neighbourhood-attention layer")
    cat = lambda xs: xs[0] if len(xs) == 1 else jnp.concatenate(xs, axis=1)
    return (x[:N_PROMPT_TOK].reshape(BATCH, SEQ, D_MODEL), x[N_PROMPT_TOK:].reshape(DEC_BATCH, DEC_SEQ, D_MODEL),
            cat(new['ret']), cat(new['rwkv']), cat(new['dk']), cat(new['dv']), cat(new['nk']), cat(new['nv']))
```

```python
import functools
import math

import jax
import jax.numpy as jnp
from jax import lax
from jax.experimental import pallas as pl
from jax.experimental.pallas import tpu as pltpu

F32 = jnp.float32
BF16 = jnp.bfloat16

D_MODEL = 1024
BATCH = 32
SEQ = 256
DEPTH = 4
N_MIXERS = 4
DEC_BATCH = 2
DEC_SEQ = 1024
PAST_LEN = 256
GRID_W = 64

RET_HEADS = 4
RET_DK = 256
RET_DV = 512
RET_QK = 1024
RET_V = 2048

RWKV_HD = 64
RWKV_HEADS = 16
RWKV_RANK = 64

DIFF_HEADS = 8
DIFF_HD = 64

NA_HEADS = 16
NA_HD = 64
NA_WIN_R = 8
NA_WIN_C = 16

ROPE_BASE = 10000.0
EPS = 1e-6
GN_EPS = 1e-5

N_PROMPT_TOK = BATCH * SEQ
N_SAMPLE_TOK = DEC_BATCH * DEC_SEQ
N_TOK = N_PROMPT_TOK + N_SAMPLE_TOK
N_COND = 8

LANES = 128
VMEM_LIMIT = 56 * 2 ** 20


def _params(*sem):
    return pltpu.CompilerParams(dimension_semantics=sem, vmem_limit_bytes=VMEM_LIMIT)


def _cond_of_tile(i, tm):
    npt = N_PROMPT_TOK // tm
    return jnp.where(i < npt, 0, 1 + (i - npt) // (DEC_SEQ // tm))


def _sigmoid(x):
    return 1.0 / (1.0 + jnp.exp(-x))


def _silu(x):
    return x * _sigmoid(x)


def _dot(a, b):
    return jnp.dot(a, b, preferred_element_type=F32)


def _dot_nt(a, b):
    return lax.dot_general(a, b, (((1,), (1,)), ((), ())), preferred_element_type=F32)


def _dot_tn(a, b):
    return lax.dot_general(a, b, (((0,), (0,)), ((), ())), preferred_element_type=F32)


def _softmax_rows(s):
    m = jnp.max(s, axis=-1, keepdims=True)
    e = jnp.exp(s - m)
    return e / jnp.sum(e, axis=-1, keepdims=True)


def _mod_kernel(c_ref, w_ref, b_ref, o_ref):
    s = _silu(c_ref[...])
    o_ref[0] = jnp.dot(s, w_ref[0], precision=lax.Precision.HIGHEST, preferred_element_type=F32) + b_ref[0]


def _modulation(cond, w_mod, b_mod):
    tn = D_MODEL
    out = pl.pallas_call(
        _mod_kernel,
        grid=(DEPTH, 3 * D_MODEL // tn),
        in_specs=[
            pl.BlockSpec((N_COND, D_MODEL), lambda l, j: (0, 0)),
            pl.BlockSpec((1, D_MODEL, tn), lambda l, j: (l, 0, j)),
            pl.BlockSpec((1, 1, tn), lambda l, j: (l, 0, j)),
        ],
        out_specs=pl.BlockSpec((1, N_COND, tn), lambda l, j: (l, 0, j)),
        out_shape=jax.ShapeDtypeStruct((DEPTH, N_COND, 3 * D_MODEL), F32),
        compiler_params=_params("arbitrary", "arbitrary"),
        name="modulation",
    )(cond, w_mod, b_mod.reshape(DEPTH, 1, 3 * D_MODEL))
    return out.reshape(DEPTH, N_COND, 3, 1, D_MODEL)


def _norm_mod(x, nw, mod_ref):
    ms = jnp.mean(x * x, axis=-1, keepdims=True)
    y = x * lax.rsqrt(ms + EPS) * nw
    return y * (1.0 + mod_ref[0, 1]) + mod_ref[0, 0]


IN_TM = 1024


def _in_proj_kernel(x_ref, nw_ref, mod_ref, w_ref, o_ref, h_ref):
    @pl.when(pl.program_id(1) == 0)
    def _():
        h_ref[...] = _norm_mod(x_ref[...], nw_ref[...], mod_ref).astype(BF16)

    o_ref[...] = _dot(h_ref[...], w_ref[...].astype(BF16))


def _in_proj(x, norm_w, mod, w, tn):
    n = w.shape[1]
    return pl.pallas_call(
        _in_proj_kernel,
        grid=(N_TOK // IN_TM, n // tn),
        in_specs=[
            pl.BlockSpec((IN_TM, D_MODEL), lambda i, j: (i, 0)),
            pl.BlockSpec((1, D_MODEL), lambda i, j: (0, 0)),
            pl.BlockSpec((1, 3, 1, D_MODEL), lambda i, j: (_cond_of_tile(i, IN_TM), 0, 0, 0)),
            pl.BlockSpec((D_MODEL, tn), lambda i, j: (0, j)),
        ],
        out_specs=pl.BlockSpec((IN_TM, tn), lambda i, j: (i, j)),
        out_shape=jax.ShapeDtypeStruct((N_TOK, n), F32),
        scratch_shapes=[pltpu.VMEM((IN_TM, D_MODEL), BF16)],
        compiler_params=_params("arbitrary", "arbitrary"),
        name="in_proj",
    )(x, norm_w.reshape(1, D_MODEL), mod, w)


OUT_TM = 256


def _out_proj_kernel(o_ref, g_ref, w_ref, x_ref, mod_ref, fw_ref, y_ref, wb_ref, *, final):
    @pl.when(pl.program_id(0) == 0)
    def _():
        wb_ref[...] = w_ref[...].astype(BF16)

    a = (o_ref[...] * _silu(g_ref[...])).astype(BF16)
    xn = x_ref[...] + mod_ref[0, 2] * _dot(a, wb_ref[...])
    if final:
        ms = jnp.mean(xn * xn, axis=-1, keepdims=True)
        xn = xn * lax.rsqrt(ms + EPS) * fw_ref[...]
    y_ref[...] = xn


def _out_proj(o, g_arr, g_blk, w, x, mod, final_w, final, rows=(0, N_TOK)):
    k = w.shape[0]
    t0 = rows[0] // OUT_TM
    return pl.pallas_call(
        functools.partial(_out_proj_kernel, final=final),
        grid=((rows[1] - rows[0]) // OUT_TM,),
        in_specs=[
            pl.BlockSpec((OUT_TM, k), lambda i: (t0 + i, 0)),
            pl.BlockSpec((OUT_TM, k), lambda i: (t0 + i, g_blk)),
            pl.BlockSpec((k, D_MODEL), lambda i: (0, 0)),
            pl.BlockSpec((OUT_TM, D_MODEL), lambda i: (t0 + i, 0)),
            pl.BlockSpec((1, 3, 1, D_MODEL), lambda i: (_cond_of_tile(t0 + i, OUT_TM), 0, 0, 0)),
            pl.BlockSpec((1, D_MODEL), lambda i: (0, 0)),
        ],
        out_specs=pl.BlockSpec((OUT_TM, D_MODEL), lambda i: (i, 0)),
        out_shape=jax.ShapeDtypeStruct((rows[1] - rows[0], D_MODEL), F32),
        scratch_shapes=[pltpu.VMEM((k, D_MODEL), BF16)],
        compiler_params=_params("arbitrary"),
        name="out_proj",
    )(o, g_arr, w, x, mod, final_w.reshape(1, D_MODEL))


def _rope_tables(d):
    q = d // 4
    t = jnp.arange(DEC_SEQ)
    row = (t // GRID_W).astype(F32)
    col = (t % GRID_W).astype(F32)
    inv = ROPE_BASE ** (-jnp.arange(0, 2 * q, 2, dtype=F32) / (2 * q))
    ar = row[:, None] * inv[None, :]
    ac = col[:, None] * inv[None, :]
    z = jnp.zeros_like(ar)
    cos = jnp.concatenate([jnp.cos(ar), jnp.cos(ar), jnp.cos(ac), jnp.cos(ac)], axis=-1)
    sin_lo = jnp.concatenate([-jnp.sin(ar), z, -jnp.sin(ac), z], axis=-1)
    sin_hi = jnp.concatenate([z, jnp.sin(ar), z, jnp.sin(ac)], axis=-1)
    return cos, sin_lo, sin_hi


def _rope(x, cos, sin_lo, sin_hi, q):
    w = x.shape[-1]
    x_next = pltpu.roll(x, w - q, axis=1)
    x_prev = pltpu.roll(x, q, axis=1)
    return x * cos + x_next * sin_lo + x_prev * sin_hi


RET_QB = 256


def _ret_kernel(lg_ref, q_ref, k_ref, v_ref, gn_ref, *rest, seq, latent):
    if latent:
        cos_ref, slo_ref, shi_ref, s0_ref, _prev_ref, o_ref = rest
    else:
        o_ref, st_ref = rest
    h = pl.program_id(1)
    lgf = lg_ref[0, h]
    lgb = lg_ref[1, h]
    q = q_ref[...]
    k = k_ref[...]
    if latent:
        q = _rope(q, cos_ref[...], slo_ref[...], shi_ref[...], RET_DK // 4)
        k = _rope(k, cos_ref[...], slo_ref[...], shi_ref[...], RET_DK // 4)
    k = k * (RET_DK ** -0.5)
    kb = k.astype(BF16)
    vb = v_ref[...].astype(BF16)
    gn = gn_ref[...]
    for qi in range(seq // RET_QB):
        qblk = q[qi * RET_QB:(qi + 1) * RET_QB]
        s = _dot_nt(qblk.astype(BF16), kb)
        ii = lax.broadcasted_iota(jnp.int32, (RET_QB, seq), 0) + qi * RET_QB
        jj = lax.broadcasted_iota(jnp.int32, (RET_QB, seq), 1)
        gap = (ii - jj).astype(F32)
        dec = (jnp.where(gap >= 0, jnp.exp(lgf * jnp.maximum(gap, 0.0)), 0.0)
               + jnp.where(gap <= 0, jnp.exp(lgb * jnp.maximum(-gap, 0.0)), 0.0))
        o = _dot((s * dec).astype(BF16), vb)
        if latent:
            pos = (lax.broadcasted_iota(jnp.int32, (RET_QB, 1), 0) + qi * RET_QB).astype(F32)
            qf = qblk * jnp.exp(lgf * (pos + 1.0))
            qr = qblk * jnp.exp(lgb * (seq - pos))
            o = o + _dot(qf.astype(BF16), s0_ref[0, 0, 0, 0].astype(BF16))
            o = o + _dot(qr.astype(BF16), s0_ref[0, 0, 1, 0].astype(BF16))
        oc = o - jnp.mean(o, axis=-1, keepdims=True)
        o = oc * lax.rsqrt(jnp.mean(oc * oc, axis=-1, keepdims=True) + GN_EPS) * gn
        o_ref[qi * RET_QB:(qi + 1) * RET_QB, :] = o
    if not latent:
        pos = lax.broadcasted_iota(jnp.int32, (seq, 1), 0).astype(F32)
        kf = k * jnp.exp(lgf * (seq - 1.0 - pos))
        kr = k * jnp.exp(lgb * pos)
        st_ref[0, 0, 0, 0] = _dot_tn(kf.astype(BF16), vb)
        st_ref[0, 0, 1, 0] = _dot_tn(kr.astype(BF16), vb)


def _retention(p, log_g, gn_w, state_ret, j):
    smem = pl.BlockSpec(memory_space=pltpu.SMEM)
    gn = gn_w.reshape(1, RET_V)
    kq = RET_QK // RET_DK
    o_p, st = pl.pallas_call(
        functools.partial(_ret_kernel, seq=SEQ, latent=False),
        grid=(BATCH, RET_HEADS),
        in_specs=[
            smem,
            pl.BlockSpec((SEQ, RET_DK), lambda b, h: (b, h)),
            pl.BlockSpec((SEQ, RET_DK), lambda b, h: (b, kq + h)),
            pl.BlockSpec((SEQ, RET_DV), lambda b, h: (b, kq + h)),
            pl.BlockSpec((1, RET_DV), lambda b, h: (0, h)),
        ],
        out_specs=[
            pl.BlockSpec((SEQ, RET_DV), lambda b, h: (b, h)),
            pl.BlockSpec((1, 1, 2, 1, RET_DK, RET_DV), lambda b, h: (b, 0, 0, h, 0, 0)),
        ],
        out_shape=[
            jax.ShapeDtypeStruct((N_TOK, RET_V), F32),
            jax.ShapeDtypeStruct((BATCH, 1, 2, RET_HEADS, RET_DK, RET_DV), F32),
        ],
        compiler_params=_params("arbitrary", "arbitrary"),
        name="retention_prompt",
    )(log_g, p, p, p, gn)
    cos, slo, shi = _rope_tables(RET_DK)
    rb = N_PROMPT_TOK // DEC_SEQ
    full = pl.BlockSpec((DEC_SEQ, RET_DK), lambda b, h: (0, 0))
    o = pl.pallas_call(
        functools.partial(_ret_kernel, seq=DEC_SEQ, latent=True),
        grid=(DEC_BATCH, RET_HEADS),
        in_specs=[
            smem,
            pl.BlockSpec((DEC_SEQ, RET_DK), lambda b, h: (rb + b, h)),
            pl.BlockSpec((DEC_SEQ, RET_DK), lambda b, h: (rb + b, kq + h)),
            pl.BlockSpec((DEC_SEQ, RET_DV), lambda b, h: (rb + b, kq + h)),
            pl.BlockSpec((1, RET_DV), lambda b, h: (0, h)),
            full, full, full,
            pl.BlockSpec((1, 1, 2, 1, RET_DK, RET_DV), lambda b, h: (b, j, 0, h, 0, 0)),
            pl.BlockSpec(memory_space=pl.ANY),
        ],
        out_specs=pl.BlockSpec((DEC_SEQ, RET_DV), lambda b, h: (rb + b, h)),
        out_shape=jax.ShapeDtypeStruct((N_TOK, RET_V), F32),
        input_output_aliases={9: 0},
        compiler_params=_params("arbitrary", "arbitrary"),
        name="retention_latent",
    )(log_g, p, p, p, gn, cos, slo, shi, state_ret, o_p)
    return o, st


def _layer_ret(x, p, mod, j):
    i = N_MIXERS * j + 0
    proj = _in_proj(x, p['norm_w'][i], mod, p['ret_w_in'][j], 1024)
    log_g = jax.nn.log_sigmoid(p['ret_decay'][j].astype(F32))
    o, st = _retention(proj, log_g, p['ret_gn'][j], p['state_ret'], j)
    x = _out_proj(o, proj, (2 * RET_QK + RET_V) // RET_V, p['ret_w_out'][j], x, mod, p['final_norm_w'], False)
    return x, st


RW_TM = 512
RW_HALO = 8
RW_C = 64
RW_LOCK = 4


def _rwkv_prep_kernel(x_ref, xp_ref, xn_ref, nw_ref, mod_ref, mu_ref, wa_ref, aa_ref, wb_ref, ab_ref,
                      w0_ref, a0_ref, xm_ref, lw_ref, a_ref):
    i = pl.program_id(0)
    nw = nw_ref[...]
    h = _norm_mod(x_ref[...], nw, mod_ref)
    h_before = _norm_mod(xp_ref[RW_HALO - 1:RW_HALO, :], nw, mod_ref)
    h_after = _norm_mod(xn_ref[0:1, :], nw, mod_ref)
    seq = jnp.where(i < N_PROMPT_TOK // RW_TM, SEQ, DEC_SEQ)
    row = lax.broadcasted_iota(jnp.int32, (RW_TM, 1), 0)
    t = (row + i * RW_TM) & (seq - 1)
    prev = jnp.where(row == 0, h_before, pltpu.roll(h, 1, axis=0))
    nxt = jnp.where(row == RW_TM - 1, h_after, pltpu.roll(h, RW_TM - 1, axis=0))
    prev = jnp.where(t == 0, 0.0, prev)
    nxt = jnp.where(t == seq - 1, 0.0, nxt)
    xx = 0.5 * (prev + nxt) - h
    for n, m in enumerate((0, 2, 3, 5)):
        xm_ref[n] = (h + xx * mu_ref[m:m + 1, :]).astype(BF16)
    xw = (h + xx * mu_ref[1:2, :]).astype(BF16)
    xa = (h + xx * mu_ref[4:5, :]).astype(BF16)
    lw = jnp.tanh(_dot(xw, wa_ref[...])).astype(BF16)
    la = _dot(xa, aa_ref[...]).astype(BF16)
    for dr in range(2):
        wl = w0_ref[dr:dr + 1, :] + _dot(lw, wb_ref[dr])
        lw_ref[dr] = -math.exp(-0.5) * _sigmoid(wl)
        a_ref[dr] = _sigmoid(a0_ref[dr:dr + 1, :] + _dot(la, ab_ref[dr]))


def _rwkv_prep(x, norm_w, mod, mu, wa2, aa2, wb_pad, ab_pad, w0, a0):
    nt = N_TOK // RW_TM
    hb = RW_TM // RW_HALO
    last = N_TOK // RW_HALO - 1
    full2 = lambda shape: pl.BlockSpec(shape, lambda i: (0, 0))
    full3 = lambda shape: pl.BlockSpec(shape, lambda i: (0, 0, 0))
    return pl.pallas_call(
        _rwkv_prep_kernel,
        grid=(nt,),
        in_specs=[
            pl.BlockSpec((RW_TM, D_MODEL), lambda i: (i, 0)),
            pl.BlockSpec((RW_HALO, D_MODEL), lambda i: (jnp.maximum(i * hb - 1, 0), 0)),
            pl.BlockSpec((RW_HALO, D_MODEL), lambda i: (jnp.minimum((i + 1) * hb, last), 0)),
            full2((1, D_MODEL)),
            pl.BlockSpec((1, 3, 1, D_MODEL), lambda i: (_cond_of_tile(i, RW_TM), 0, 0, 0)),
            full2((6, D_MODEL)),
            full2((D_MODEL, 2 * RWKV_RANK)),
            full2((D_MODEL, 2 * RWKV_RANK)),
            full3((2, 2 * RWKV_RANK, D_MODEL)),
            full3((2, 2 * RWKV_RANK, D_MODEL)),
            full2((2, D_MODEL)),
            full2((2, D_MODEL)),
        ],
        out_specs=[
            pl.BlockSpec((4, RW_TM, D_MODEL), lambda i: (0, i, 0)),
            pl.BlockSpec((2, RW_TM, D_MODEL), lambda i: (0, i, 0)),
            pl.BlockSpec((2, RW_TM, D_MODEL), lambda i: (0, i, 0)),
        ],
        out_shape=[
            jax.ShapeDtypeStruct((4, N_TOK, D_MODEL), BF16),
            jax.ShapeDtypeStruct((2, N_TOK, D_MODEL), F32),
            jax.ShapeDtypeStruct((2, N_TOK, D_MODEL), F32),
        ],
        compiler_params=_params("arbitrary"),
        name="rwkv_prep",
    )(x, x, x, norm_w.reshape(1, D_MODEL), mod, mu, wa2, aa2, wb_pad, ab_pad, w0, a0)


def _bmm_kernel(a_ref, w_ref, o_ref):
    o_ref[0] = _dot(a_ref[0], w_ref[...].astype(BF16))


def _rwkv_rkvg(xm, w):
    tm = 1024
    return pl.pallas_call(
        _bmm_kernel,
        grid=(4, N_TOK // tm),
        in_specs=[
            pl.BlockSpec((1, tm, D_MODEL), lambda n, i: (n, i, 0)),
            pl.BlockSpec((D_MODEL, D_MODEL), lambda n, i: (0, n)),
        ],
        out_specs=pl.BlockSpec((1, tm, D_MODEL), lambda n, i: (n, i, 0)),
        out_shape=jax.ShapeDtypeStruct((4, N_TOK, D_MODEL), F32),
        compiler_params=_params("arbitrary", "arbitrary"),
        name="rwkv_rkvg",
    )(xm, w)


def _head_sum(x, first):
    s0 = jnp.sum(jnp.where(first, x, 0.0), axis=-1, keepdims=True)
    s1 = jnp.sum(jnp.where(first, 0.0, x), axis=-1, keepdims=True)
    return jnp.where(first, s0, s1)


def _stack_heads(x, first):
    return jnp.concatenate([jnp.where(first, x, 0.0), jnp.where(first, 0.0, x)], axis=0)


def _cumsum_rows(tri, x):
    hi = x.astype(BF16)
    r1 = x - hi.astype(F32)
    mid = r1.astype(BF16)
    lo = (r1 - mid.astype(F32)).astype(BF16)
    return _dot(tri, hi) + _dot(tri, mid) + _dot(tri, lo)


def _rwkv_chunk_kernel(*refs, seq, zero_init):
    if zero_init:
        (rkv_ref, lw_ref, a_ref, kkp_ref, kap_ref, rkp_ref, gn_ref, o_ref, st_ref,
         kk_scr, y_scr, tar_scr, lrb_scr, b2_scr, w2_scr, yl_scr, kv_scr, pc_scr) = refs
    else:
        (rkv_ref, lw_ref, a_ref, kkp_ref, kap_ref, rkp_ref, gn_ref, s0_ref, _prev_ref, o_ref,
         kk_scr, y_scr, tar_scr, lrb_scr, b2_scr, w2_scr, yl_scr, kv_scr, pc_scr) = refs
    c_len = RW_C
    n_ch = seq // c_len
    rows2 = 2 * c_len
    first = _first_half_lanes()
    kap = kap_ref[...]

    kk = rkv_ref[1] * kkp_ref[...]
    kk_scr[...] = kk * lax.rsqrt(jnp.maximum(_head_sum(kk * kk, first), 1e-12))

    rr = lax.broadcasted_iota(jnp.int32, (rows2, rows2), 0)
    cc = lax.broadcasted_iota(jnp.int32, (rows2, rows2), 1)
    eye = (rr == cc).astype(F32)
    tr = lax.broadcasted_iota(jnp.int32, (c_len, c_len), 0)
    tc = lax.broadcasted_iota(jnp.int32, (c_len, c_len), 1)

    def same(shift):
        return (rr >> shift) == (cc >> shift)

    head = same(6)
    strict = (head & (cc < rr), head & (cc > rr))
    incl = (head & (cc <= rr), head & (cc >= rr))
    tri = ((tc <= tr).astype(BF16), (tc >= tr).astype(BF16))
    last = (c_len - 1, 0)

    def phase1(chains):
        dirs = [dr for dr, _ in chains]
        rows = [pl.ds(pl.multiple_of(c * c_len, c_len), c_len) for _, c in chains]
        lw = [lw_ref[dr, rw, :] for dr, rw in zip(dirs, rows)]
        cum = [_cumsum_rows(tri[dr], x) for dr, x in zip(dirs, lw)]
        a2, r2, b2, k2, v2, pc = [], [], [], [], [], []
        for dr, rw, lw_c, cum_c in zip(dirs, rows, lw, cum):
            a = a_ref[dr, rw, :]
            k = rkv_ref[1, rw, :]
            kk_c = kk_scr[rw, :]
            e_inc = jnp.exp(cum_c)
            e_inv = jnp.exp(-cum_c)
            a2.append(_stack_heads(-kk_c * jnp.exp(cum_c - lw_c), first).astype(BF16))
            r2.append(_stack_heads(rkv_ref[0, rw, :] * e_inc, first).astype(BF16))
            b2.append(_stack_heads(kk_c * a * e_inv, first).astype(BF16))
            k2.append(_stack_heads(k * (1.0 + (a - 1.0) * kap) * e_inv, first).astype(BF16))
            v2.append(_stack_heads(rkv_ref[2, rw, :], first).astype(BF16))
            pc.append(e_inc[last[dr]:last[dr] + 1, :])
        g = [_dot_nt(jnp.concatenate([x, y], axis=0), jnp.concatenate([z, w], axis=0))
             for x, y, z, w in zip(a2, r2, b2, k2)]
        l_ab = [jnp.where(strict[dr], x[:rows2, :rows2], 0.0) for dr, x in zip(dirs, g)]
        t = [eye + jnp.where(same(1), x, 0.0) for x in l_ab]
        side = {}
        for shift in range(1, 6):
            sib = same(shift + 1) & ~same(shift)
            tb = [x.astype(BF16) for x in t]
            mid = [_dot(jnp.where(sib, x, 0.0).astype(BF16), y) for x, y in zip(l_ab, tb)]
            if shift == 1:
                side['lv'] = [_dot(jnp.where(strict[dr], x[:rows2, rows2:], 0.0).astype(BF16), y)
                              for dr, x, y in zip(dirs, g, v2)]
            elif shift == 2:
                side['yl'] = [_dot(jnp.where(incl[dr], x[rows2:, rows2:], 0.0).astype(BF16), y)
                              for dr, x, y in zip(dirs, g, v2)]
            elif shift == 3:
                side['kv'] = [_dot_tn(x, y) for x, y in zip(v2, k2)]
            t = [x + _dot(y, z.astype(BF16)) for x, y, z in zip(t, tb, mid)]
        tb = [x.astype(BF16) for x in t]
        ta = [_dot(x, y) for x, y in zip(tb, a2)]
        w2 = [_dot(x, y.astype(BF16)) for x, y in zip(tb, side['lv'])]
        for i, (dr, c) in enumerate(chains):
            n = dr * n_ch + c
            tar_scr[n, :rows2, :] = ta[i].astype(BF16)
            tar_scr[n, rows2:, :] = r2[i]
            w2_scr[n] = w2[i]
            yl_scr[n] = side['yl'][i]
            kv_scr[n] = side['kv'][i]
            lrb_scr[n] = jnp.where(incl[dr], g[i][rows2:, :rows2], 0.0).astype(BF16)
            b2_scr[n] = b2[i]
            pc_scr[n] = pc[i]

    def body1(grp, carry):
        phase1([(dr, grp * RW_LOCK + j) for j in range(RW_LOCK) for dr in range(2)])
        return carry

    if n_ch == RW_LOCK:
        body1(0, 0)
    else:
        lax.fori_loop(0, n_ch // RW_LOCK, body1, 0)

    def body2(i, carry):
        cs = (i, n_ch - 1 - i)
        ns = [dr * n_ch + c for dr, c in enumerate(cs)]
        x = [_dot_nt(tar_scr[n], s2.astype(BF16)) for n, s2 in zip(ns, carry)]
        u2 = [(xx[:rows2] + w2_scr[n]).astype(BF16) for n, xx in zip(ns, x)]
        upd = [_dot_tn(u, b2_scr[n]) for n, u in zip(ns, u2)]
        yb = [_dot(lrb_scr[n], u) for n, u in zip(ns, u2)]
        out = []
        for dr, (c, n) in enumerate(zip(cs, ns)):
            y2 = x[dr][rows2:] + yb[dr] + yl_scr[n]
            y_scr[dr, pl.ds(pl.multiple_of(c * c_len, c_len), c_len), :] = y2[:c_len] + y2[c_len:]
            out.append((carry[dr] + upd[dr] + kv_scr[n]) * pc_scr[n])
        return tuple(out)

    if zero_init:
        init = (jnp.zeros((rows2, LANES), F32),) * 2
    else:
        init = (s0_ref[0, 0, 0], s0_ref[0, 1, 0])
    s_f, s_b = lax.fori_loop(0, n_ch, body2, init)

    y = y_scr[0] + y_scr[1]
    yc = y - _head_sum(y, first) * (1.0 / RWKV_HD)
    o = yc * lax.rsqrt(_head_sum(yc * yc, first) * (1.0 / RWKV_HD) + GN_EPS) * gn_ref[...]
    r = rkv_ref[0]
    k = rkv_ref[1]
    v = rkv_ref[2]
    for dr in range(2):
        kd = k * (1.0 + (a_ref[dr] - 1.0) * kap)
        o = o + _head_sum(r * kd * rkp_ref[...], first) * v
    o_ref[...] = o

    if zero_init:
        for dr, s2 in enumerate((s_f, s_b)):
            st_ref[0, 0, dr, 0] = s2[:RWKV_HD, :RWKV_HD]
            st_ref[0, 0, dr, 1] = s2[RWKV_HD:, RWKV_HD:]


def _rwkv_chunked(rkvg, lw, a, kkp, kap, rkp, gn, s0_pairs):
    npair = RWKV_HEADS // 2
    par = lambda *_: pl.BlockSpec((1, LANES), lambda s, p: (0, p))

    def scratch(seq):
        n = 2 * (seq // RW_C)
        r2 = 2 * RW_C
        return [
            pltpu.VMEM((seq, LANES), F32), pltpu.VMEM((2, seq, LANES), F32),
            pltpu.VMEM((n, 2 * r2, LANES), BF16), pltpu.VMEM((n, r2, r2), BF16), pltpu.VMEM((n, r2, LANES), BF16),
            pltpu.VMEM((n, r2, LANES), F32), pltpu.VMEM((n, r2, LANES), F32), pltpu.VMEM((n, r2, LANES), F32),
            pltpu.VMEM((n, 1, LANES), F32),
        ]

    def seq_specs(seq, rb):
        return [
            pl.BlockSpec((3, seq, LANES), lambda s, p: (0, rb + s, p)),
            pl.BlockSpec((2, seq, LANES), lambda s, p: (0, rb + s, p)),
            pl.BlockSpec((2, seq, LANES), lambda s, p: (0, rb + s, p)),
            par(), par(), par(), par(),
        ]

    pars = [u.reshape(1, D_MODEL) for u in (kkp, kap, rkp, gn)]
    o_p, st = pl.pallas_call(
        functools.partial(_rwkv_chunk_kernel, seq=SEQ, zero_init=True),
        grid=(BATCH, npair),
        in_specs=seq_specs(SEQ, 0),
        out_specs=[
            pl.BlockSpec((SEQ, LANES), lambda s, p: (s, p)),
            pl.BlockSpec((1, 1, 2, 2, RWKV_HD, RWKV_HD), lambda s, p: (s, 0, 0, p, 0, 0)),
        ],
        out_shape=[
            jax.ShapeDtypeStruct((N_TOK, D_MODEL), F32),
            jax.ShapeDtypeStruct((BATCH, 1, 2, RWKV_HEADS, RWKV_HD, RWKV_HD), F32),
        ],
        scratch_shapes=scratch(SEQ),
        compiler_params=_params("arbitrary", "arbitrary"),
        name="rwkv_prompt",
    )(rkvg, lw, a, *pars)
    rb = N_PROMPT_TOK // DEC_SEQ
    o = pl.pallas_call(
        functools.partial(_rwkv_chunk_kernel, seq=DEC_SEQ, zero_init=False),
        grid=(DEC_BATCH, npair),
        in_specs=seq_specs(DEC_SEQ, rb) + [
            pl.BlockSpec((1, 2, 1, LANES, LANES), lambda s, p: (s, 0, p, 0, 0)),
            pl.BlockSpec(memory_space=pl.ANY),
        ],
        out_specs=pl.BlockSpec((DEC_SEQ, LANES), lambda s, p: (rb + s, p)),
        out_shape=jax.ShapeDtypeStruct((N_TOK, D_MODEL), F32),
        input_output_aliases={8: 0},
        scratch_shapes=scratch(DEC_SEQ),
        compiler_params=_params("arbitrary", "arbitrary"),
        name="rwkv_latent",
    )(rkvg, lw, a, *pars, s0_pairs, o_p)
    return o, st


def _state_pairs(s0):
    s = s0.reshape(DEC_BATCH, 2, RWKV_HEADS // 2, 2, RWKV_HD, RWKV_HD)
    z = jnp.zeros_like(s[:, :, :, 0])
    top = jnp.concatenate([s[:, :, :, 0], z], axis=-1)
    bot = jnp.concatenate([z, s[:, :, :, 1]], axis=-1)
    return jnp.concatenate([top, bot], axis=-2)


def _layer_rwkv(x, p, mod, j):
    i = N_MIXERS * j + 1
    wa, wb, aa, ab = p['rwkv_wA'][j], p['rwkv_wB'][j], p['rwkv_aA'][j], p['rwkv_aB'][j]
    z = jnp.zeros_like(wb[0])
    wa2 = jnp.concatenate([wa[0], wa[1]], axis=1).astype(BF16)
    aa2 = jnp.concatenate([aa[0], aa[1]], axis=1).astype(BF16)
    wb_pad = jnp.stack([jnp.concatenate([wb[0], z]), jnp.concatenate([z, wb[1]])]).astype(BF16)
    ab_pad = jnp.stack([jnp.concatenate([ab[0], z]), jnp.concatenate([z, ab[1]])]).astype(BF16)
    xm, lw, a = _rwkv_prep(x, p['norm_w'][i], mod, p['rwkv_mu'][j], wa2, aa2, wb_pad, ab_pad,
                           p['rwkv_w0'][j], p['rwkv_a0'][j])
    rkvg = _rwkv_rkvg(xm, p['rwkv_w_in'][j])
    o, st = _rwkv_chunked(rkvg, lw, a, p['rwkv_kk'][j], p['rwkv_ka'][j], p['rwkv_rk'][j], p['rwkv_gn'][j],
                          _state_pairs(p['state_rwkv'][:, j]))
    x = _out_proj(o, rkvg[3], 0, p['rwkv_w_out'][j], x, mod, p['final_norm_w'], False)
    return x, st


DIFF_W = 2 * DIFF_HD
ATT_QB = 256


def _first_half_lanes():
    return lax.broadcasted_iota(jnp.int32, (1, LANES), 1) < LANES // 2


def _diff_lambda(lam_ref, lam_init):
    lp = lam_ref[...]
    return (jnp.exp(jnp.sum(lp[0:1] * lp[1:2], keepdims=True))
            - jnp.exp(jnp.sum(lp[2:3] * lp[3:4], keepdims=True)) + lam_init)


def _diff_head(q, key_sets, lam, gn, lam_init):
    first = _first_half_lanes()
    scale = DIFF_HD ** -0.5
    probs = []
    for comp in range(2):
        qm = jnp.where(first if comp == 0 else ~first, q, 0.0).astype(BF16)
        s = [_dot_nt(qm, kb) * scale for kb, _ in key_sets]
        m = functools.reduce(jnp.maximum, [jnp.max(u, axis=-1, keepdims=True) for u in s])
        e = [jnp.exp(u - m) for u in s]
        den = functools.reduce(lambda x, y: x + y, [jnp.sum(u, axis=-1, keepdims=True) for u in e])
        probs.append([u / den for u in e])
    o = None
    for n, (_, vb) in enumerate(key_sets):
        part = _dot((probs[0][n] - lam * probs[1][n]).astype(BF16), vb)
        o = part if o is None else o + part
    o = o * lax.rsqrt(jnp.mean(o * o, axis=-1, keepdims=True) + EPS) * gn
    return o * (1.0 - lam_init)


def _diff_prompt_kernel(lam_ref, q_ref, k_ref, v_ref, gn_ref, o_ref, ck_ref, cv_ref, *, lam_init):
    lam = _diff_lambda(lam_ref, lam_init)
    for h in range(DIFF_HEADS):
        sl = slice(h * DIFF_W, (h + 1) * DIFF_W)
        k = k_ref[:, sl]
        v = v_ref[:, sl]
        ck_ref[0, 0, h] = k
        cv_ref[0, 0, h] = v
        keys = [(k.astype(BF16), v.astype(BF16))]
        o_ref[:, sl] = _diff_head(q_ref[:, sl], keys, lam, gn_ref[:, sl], lam_init)


def _diff_latent_kernel(lam_ref, q_ref, k_ref, v_ref, ck_ref, cv_ref, cos_ref, slo_ref, shi_ref, gn_ref,
                        _prev_ref, o_ref, *, lam_init):
    lam = _diff_lambda(lam_ref, lam_init)
    tabs = (cos_ref[...], slo_ref[...], shi_ref[...])
    q = _rope(q_ref[...], *tabs, DIFF_HD // 4)
    k = _rope(k_ref[...], *tabs, DIFF_HD // 4)
    keys = [(k.astype(BF16), v_ref[...].astype(BF16)),
            (ck_ref[0, 0, 0].astype(BF16), cv_ref[0, 0, 0].astype(BF16))]
    gn = gn_ref[...]
    for qi in range(DEC_SEQ // ATT_QB):
        sl = slice(qi * ATT_QB, (qi + 1) * ATT_QB)
        o_ref[sl, :] = _diff_head(q[sl], keys, lam, gn, lam_init)


def _diff_attention(proj, lam_p, gn_w, cache_k, cache_v, j, lam_init):
    gn = gn_w.reshape(1, D_MODEL)
    lam_spec = pl.BlockSpec((4, DIFF_HD), lambda *_: (0, 0))
    cache_out = pl.BlockSpec((1, 1, DIFF_HEADS, SEQ, DIFF_W), lambda b: (b, 0, 0, 0, 0))
    cache_shape = jax.ShapeDtypeStruct((BATCH, 1, DIFF_HEADS, SEQ, DIFF_W), F32)
    o_p, new_k, new_v = pl.pallas_call(
        functools.partial(_diff_prompt_kernel, lam_init=lam_init),
        grid=(BATCH,),
        in_specs=[
            lam_spec,
            pl.BlockSpec((SEQ, D_MODEL), lambda b: (b, 0)),
            pl.BlockSpec((SEQ, D_MODEL), lambda b: (b, 1)),
            pl.BlockSpec((SEQ, D_MODEL), lambda b: (b, 2)),
            pl.BlockSpec((1, D_MODEL), lambda b: (0, 0)),
        ],
        out_specs=[pl.BlockSpec((SEQ, D_MODEL), lambda b: (b, 0)), cache_out, cache_out],
        out_shape=[jax.ShapeDtypeStruct((N_TOK, D_MODEL), F32), cache_shape, cache_shape],
        compiler_params=_params("arbitrary"),
        name="diff_prompt",
    )(lam_p, proj, proj, proj, gn)
    cos, slo, shi = (jnp.concatenate([u, u], axis=-1) for u in _rope_tables(DIFF_HD))
    rb = N_PROMPT_TOK // DEC_SEQ
    nh = DIFF_HEADS
    tab = pl.BlockSpec((DEC_SEQ, DIFF_W), lambda b, h: (0, 0))
    cache = pl.BlockSpec((1, 1, 1, PAST_LEN, DIFF_W), lambda b, h: (b, j, h, 0, 0))
    o = pl.pallas_call(
        functools.partial(_diff_latent_kernel, lam_init=lam_init),
        grid=(DEC_BATCH, nh),
        in_specs=[
            lam_spec,
            pl.BlockSpec((DEC_SEQ, DIFF_W), lambda b, h: (rb + b, h)),
            pl.BlockSpec((DEC_SEQ, DIFF_W), lambda b, h: (rb + b, nh + h)),
            pl.BlockSpec((DEC_SEQ, DIFF_W), lambda b, h: (rb + b, 2 * nh + h)),
            cache, cache, tab, tab, tab,
            pl.BlockSpec((1, DIFF_W), lambda b, h: (0, h)),
            pl.BlockSpec(memory_space=pl.ANY),
        ],
        out_specs=pl.BlockSpec((DEC_SEQ, DIFF_W), lambda b, h: (rb + b, h)),
        out_shape=jax.ShapeDtypeStruct((N_TOK, D_MODEL), F32),
        input_output_aliases={10: 0},
        compiler_params=_params("arbitrary", "arbitrary"),
        name="diff_latent",
    )(lam_p, proj, proj, proj, cache_k, cache_v, cos, slo, shi, gn, o_p)
    return o, new_k, new_v


def _layer_diff(x, p, mod, j, i):
    lam_init = 0.8 - 0.6 * math.exp(-0.3 * i)
    proj = _in_proj(x, p['norm_w'][i], mod, p['diff_w_in'][j], 1024)
    o, new_k, new_v = _diff_attention(proj, p['diff_lambda'][j], p['diff_gn'][j], p['cache_diff_k'],
                                      p['cache_diff_v'], j, lam_init)
    x = _out_proj(o, proj, 3, p['diff_w_out'][j], x, mod, p['final_norm_w'], False)
    return x, new_k, new_v


NA_ROWS = DEC_SEQ // GRID_W
NA_WR = min(NA_WIN_R, NA_ROWS)
NA_LOC = NA_WR * GRID_W


def _na_prompt_kernel(q_ref, k_ref, v_ref, o_ref, ck_ref, cv_ref):
    first = _first_half_lanes()
    scale = NA_HD ** -0.5
    for pr in range(NA_HEADS // 2):
        sl = slice(pr * LANES, (pr + 1) * LANES)
        q = q_ref[:, sl]
        k = k_ref[:, sl]
        v = v_ref[:, sl]
        for half in range(2):
            ck_ref[0, 0, 2 * pr + half] = k[:, half * NA_HD:(half + 1) * NA_HD]
            cv_ref[0, 0, 2 * pr + half] = v[:, half * NA_HD:(half + 1) * NA_HD]
        kb = k.astype(BF16)
        vb = v.astype(BF16)
        outs = []
        for half in range(2):
            qm = jnp.where(first if half == 0 else ~first, q, 0.0).astype(BF16)
            pr_ = _softmax_rows(_dot_nt(qm, kb) * scale).astype(BF16)
            outs.append(_dot(pr_, vb))
        o_ref[:, sl] = jnp.where(first, outs[0], outs[1])


def _na_latent_kernel(q_ref, k_ref, v_ref, kc_ref, vc_ref, tab_ref, _prev_ref, o_ref):
    first = _first_half_lanes()
    scale = NA_HD ** -0.5
    kb = k_ref[...].astype(BF16)
    vb = v_ref[...].astype(BF16)
    kcb = kc_ref[0, 0].astype(BF16)
    vcb = vc_ref[0, 0].astype(BF16)
    qcol = lax.broadcasted_iota(jnp.int32, (GRID_W, NA_LOC), 0)
    kcol = lax.broadcasted_iota(jnp.int32, (GRID_W, NA_LOC), 1) & (GRID_W - 1)
    cstart = jnp.clip(qcol - NA_WIN_C // 2, 0, GRID_W - NA_WIN_C)
    col_ok = (kcol >= cstart) & (kcol < cstart + NA_WIN_C)
    for r in range(NA_ROWS):
        rs = min(max(r - NA_WR // 2, 0), NA_ROWS - NA_WR)
        rows = slice(r * GRID_W, (r + 1) * GRID_W)
        q = q_ref[rows, :]
        kl = kb[rs * GRID_W:(rs + NA_WR) * GRID_W]
        vl = vb[rs * GRID_W:(rs + NA_WR) * GRID_W]
        outs = []
        for half in range(2):
            qm = jnp.where(first if half == 0 else ~first, q, 0.0).astype(BF16)
            bias = []
            for w in range(0, NA_WR, 2):
                src = jnp.broadcast_to(tab_ref[half, rs + w - r + NA_WIN_R - 1], (GRID_W, LANES))
                bias.append(pltpu.roll(src, LANES - (NA_WIN_C - 1), axis=1, stride=1, stride_axis=0))
            s_loc = _dot_nt(qm, kl) * scale + jnp.concatenate(bias, axis=1)
            s_loc = jnp.where(col_ok, s_loc, -jnp.inf)
            s_ctx = _dot_nt(qm, kcb) * scale
            m = jnp.maximum(jnp.max(s_loc, axis=-1, keepdims=True), jnp.max(s_ctx, axis=-1, keepdims=True))
            e_loc = jnp.exp(s_loc - m)
            e_ctx = jnp.exp(s_ctx - m)
            den = jnp.sum(e_loc, axis=-1, keepdims=True) + jnp.sum(e_ctx, axis=-1, keepdims=True)
            outs.append(_dot((e_loc / den).astype(BF16), vl) + _dot((e_ctx / den).astype(BF16), vcb))
        o_ref[rows, :] = jnp.where(first, outs[0], outs[1])


def _na_bias_pairs(table):
    t = table.astype(F32)
    nc = 2 * NA_WIN_C - 1
    z = jnp.zeros(t[:, :-1].shape[:2] + (GRID_W - nc,), F32)
    return jnp.concatenate([t[:, :-1], z, t[:, 1:], z], axis=-1)[:, :, None, :]


def _pair_heads(cache):
    c = cache.reshape(DEC_BATCH, NA_HEADS // 2, 2, PAST_LEN, NA_HD)
    return c.transpose(0, 1, 3, 2, 4).reshape(DEC_BATCH, NA_HEADS // 2, PAST_LEN, LANES)


def _na_attention(proj, bias_table, cache_k, cache_v):
    cache_out = pl.BlockSpec((1, 1, NA_HEADS, SEQ, NA_HD), lambda b: (b, 0, 0, 0, 0))
    cache_shape = jax.ShapeDtypeStruct((BATCH, 1, NA_HEADS, SEQ, NA_HD), F32)
    o_p, new_k, new_v = pl.pallas_call(
        _na_prompt_kernel,
        grid=(BATCH,),
        in_specs=[
            pl.BlockSpec((SEQ, D_MODEL), lambda b: (b, 0)),
            pl.BlockSpec((SEQ, D_MODEL), lambda b: (b, 1)),
            pl.BlockSpec((SEQ, D_MODEL), lambda b: (b, 2)),
        ],
        out_specs=[pl.BlockSpec((SEQ, D_MODEL), lambda b: (b, 0)), cache_out, cache_out],
        out_shape=[jax.ShapeDtypeStruct((N_TOK, D_MODEL), F32), cache_shape, cache_shape],
        compiler_params=_params("arbitrary"),
        name="na_prompt",
    )(proj, proj, proj)
    rb = N_PROMPT_TOK // DEC_SEQ
    npair = NA_HEADS // 2
    cache = pl.BlockSpec((1, 1, PAST_LEN, LANES), lambda pr, b: (b, pr, 0, 0))
    o = pl.pallas_call(
        _na_latent_kernel,
        grid=(npair, DEC_BATCH),
        in_specs=[
            pl.BlockSpec((DEC_SEQ, LANES), lambda pr, b: (rb + b, pr)),
            pl.BlockSpec((DEC_SEQ, LANES), lambda pr, b: (rb + b, npair + pr)),
            pl.BlockSpec((DEC_SEQ, LANES), lambda pr, b: (rb + b, 2 * npair + pr)),
            cache, cache,
            pl.BlockSpec((2, 2 * NA_WIN_R - 2, 1, LANES), lambda pr, b: (pr, 0, 0, 0)),
            pl.BlockSpec(memory_space=pl.ANY),
        ],
        out_specs=pl.BlockSpec((DEC_SEQ, LANES), lambda pr, b: (rb + b, pr)),
        out_shape=jax.ShapeDtypeStruct((N_TOK, D_MODEL), F32),
        input_output_aliases={6: 0},
        compiler_params=_params("arbitrary", "arbitrary"),
        name="na_latent",
    )(proj, proj, proj, _pair_heads(cache_k), _pair_heads(cache_v), _na_bias_pairs(bias_table), o_p)
    return o, new_k, new_v


def _layer_na(x, p, mod, j, final):
    i = N_MIXERS * j + 3
    proj = _in_proj(x, p['norm_w'][i], mod, p['na_w_in'][j], 1024)
    o, new_k, new_v = _na_attention(proj, p['na_bias'][j], p['cache_na_k'][:, j], p['cache_na_v'][:, j])
    args = (o, proj, 3, p['na_w_out'][j], x, mod, p['final_norm_w'])
    if final:
        x = (_out_proj(*args, True, rows=(0, N_PROMPT_TOK)), _out_proj(*args, True, rows=(N_PROMPT_TOK, N_TOK)))
    else:
        x = _out_proj(*args, False)
    return x, new_k, new_v


def kernel(x_prompt, x_sample, state_ret, state_rwkv, cache_diff_k, cache_diff_v, cache_na_k, cache_na_v,
           c, c_ctx, norm_w, w_mod, b_mod, final_norm_w,
           ret_w_in, ret_decay, ret_gn, ret_w_out,
           rwkv_mu, rwkv_w_in, rwkv_w0, rwkv_wA, rwkv_wB, rwkv_a0, rwkv_aA, rwkv_aB,
           rwkv_kk, rwkv_ka, rwkv_rk, rwkv_gn, rwkv_w_out,
           diff_w_in, diff_lambda, diff_gn, diff_w_out,
           na_w_in, na_bias, na_w_out):
    p = dict(locals())
    cond = jnp.zeros((N_COND, D_MODEL), F32).at[0].set(c_ctx).at[1:1 + DEC_BATCH].set(c)
    mods = _modulation(cond, w_mod, b_mod)
    x = jnp.concatenate([x_prompt.reshape(N_PROMPT_TOK, D_MODEL), x_sample.reshape(N_SAMPLE_TOK, D_MODEL)])
    new = {n: [] for n in ('ret', 'rwkv', 'dk', 'dv', 'nk', 'nv')}
    for i in range(DEPTH):
        kind, j = i % N_MIXERS, i // N_MIXERS
        if kind == 0:
            x, st = _layer_ret(x, p, mods[i], j)
            new['ret'].append(st)
        elif kind == 1:
            x, st = _layer_rwkv(x, p, mods[i], j)
            new['rwkv'].append(st)
        elif kind == 2:
            x, ck, cv = _layer_diff(x, p, mods[i], j, i)
            new['dk'].append(ck)
            new['dv'].append(cv)
        else:
            x, ck, cv = _layer_na(x, p, mods[i], j, final=(i == DEPTH - 1))
            new['nk'].append(ck)
            new['nv'].append(cv)
    if DEPTH % N_MIXERS:
        raise NotImplementedError("the final norm is fused into the last neighbourhood-attention layer")
    cat = lambda xs: xs[0] if len(xs) == 1 else jnp.concatenate(xs, axis=1)
    return (x[0].reshape(BATCH, SEQ, D_MODEL), x[1].reshape(DEC_BATCH, DEC_SEQ, D_MODEL),
            cat(new['ret']), cat(new['rwkv']), cat(new['dk']), cat(new['dv']), cat(new['nk']), cat(new['nv']))
```

```python
import functools
import math

import jax
import jax.numpy as jnp
from jax import lax
from jax.experimental import pallas as pl
from jax.experimental.pallas import tpu as pltpu

F32 = jnp.float32
BF16 = jnp.bfloat16

D_MODEL = 1024
BATCH = 32
SEQ = 256
DEPTH = 4
N_MIXERS = 4
DEC_BATCH = 2
DEC_SEQ = 1024
PAST_LEN = 256
GRID_W = 64

RET_HEADS = 4
RET_DK = 256
RET_DV = 512
RET_QK = 1024
RET_V = 2048

RWKV_HD = 64
RWKV_HEADS = 16
RWKV_RANK = 64

DIFF_HEADS = 8
DIFF_HD = 64

NA_HEADS = 16
NA_HD = 64
NA_WIN_R = 8
NA_WIN_C = 16

ROPE_BASE = 10000.0
EPS = 1e-6
GN_EPS = 1e-5

N_PROMPT_TOK = BATCH * SEQ
N_SAMPLE_TOK = DEC_BATCH * DEC_SEQ
N_TOK = N_PROMPT_TOK + N_SAMPLE_TOK
N_COND = 8

LANES = 128
VMEM_LIMIT = 56 * 2 ** 20


def _params(*sem):
    return pltpu.CompilerParams(dimension_semantics=sem, vmem_limit_bytes=VMEM_LIMIT)


def _cond_of_tile(i, tm):
    npt = N_PROMPT_TOK // tm
    return jnp.where(i < npt, 0, 1 + (i - npt) // (DEC_SEQ // tm))


def _sigmoid(x):
    return 1.0 / (1.0 + jnp.exp(-x))


def _silu(x):
    return x * _sigmoid(x)


def _dot(a, b):
    return jnp.dot(a, b, preferred_element_type=F32)


def _dot_nt(a, b):
    return lax.dot_general(a, b, (((1,), (1,)), ((), ())), preferred_element_type=F32)


def _dot_tn(a, b):
    return lax.dot_general(a, b, (((0,), (0,)), ((), ())), preferred_element_type=F32)


def _softmax_rows(s):
    m = jnp.max(s, axis=-1, keepdims=True)
    e = jnp.exp(s - m)
    return e / jnp.sum(e, axis=-1, keepdims=True)


def _mod_kernel(c_ref, w_ref, b_ref, o_ref):
    s = _silu(c_ref[...])
    o_ref[0] = jnp.dot(s, w_ref[0], precision=lax.Precision.HIGHEST, preferred_element_type=F32) + b_ref[0]


def _modulation(cond, w_mod, b_mod):
    tn = D_MODEL
    out = pl.pallas_call(
        _mod_kernel,
        grid=(DEPTH, 3 * D_MODEL // tn),
        in_specs=[
            pl.BlockSpec((N_COND, D_MODEL), lambda l, j: (0, 0)),
            pl.BlockSpec((1, D_MODEL, tn), lambda l, j: (l, 0, j)),
            pl.BlockSpec((1, 1, tn), lambda l, j: (l, 0, j)),
        ],
        out_specs=pl.BlockSpec((1, N_COND, tn), lambda l, j: (l, 0, j)),
        out_shape=jax.ShapeDtypeStruct((DEPTH, N_COND, 3 * D_MODEL), F32),
        compiler_params=_params("arbitrary", "arbitrary"),
        name="modulation",
    )(cond, w_mod, b_mod.reshape(DEPTH, 1, 3 * D_MODEL))
    return out.reshape(DEPTH, N_COND, 3, 1, D_MODEL)


def _norm_mod(x, nw, mod_ref):
    ms = jnp.mean(x * x, axis=-1, keepdims=True)
    y = x * lax.rsqrt(ms + EPS) * nw
    return y * (1.0 + mod_ref[0, 1]) + mod_ref[0, 0]


IN_TM = 1024


def _in_proj_kernel(x_ref, nw_ref, mod_ref, w_ref, o_ref, h_ref):
    @pl.when(pl.program_id(1) == 0)
    def _():
        h_ref[...] = _norm_mod(x_ref[...], nw_ref[...], mod_ref).astype(BF16)

    o_ref[...] = _dot(h_ref[...], w_ref[...])


def _in_proj(x, norm_w, mod, w, tn):
    n = w.shape[1]
    w = w.astype(BF16)
    return pl.pallas_call(
        _in_proj_kernel,
        grid=(N_TOK // IN_TM, n // tn),
        in_specs=[
            pl.BlockSpec((IN_TM, D_MODEL), lambda i, j: (i, 0)),
            pl.BlockSpec((1, D_MODEL), lambda i, j: (0, 0)),
            pl.BlockSpec((1, 3, 1, D_MODEL), lambda i, j: (_cond_of_tile(i, IN_TM), 0, 0, 0)),
            pl.BlockSpec((D_MODEL, tn), lambda i, j: (0, j)),
        ],
        out_specs=pl.BlockSpec((IN_TM, tn), lambda i, j: (i, j)),
        out_shape=jax.ShapeDtypeStruct((N_TOK, n), F32),
        scratch_shapes=[pltpu.VMEM((IN_TM, D_MODEL), BF16)],
        compiler_params=_params("arbitrary", "arbitrary"),
        name="in_proj",
    )(x, norm_w.reshape(1, D_MODEL), mod, w)


OUT_TM = 256


def _out_proj_kernel(o_ref, g_ref, w_ref, x_ref, mod_ref, fw_ref, y_ref, wb_ref, *, final):
    @pl.when(pl.program_id(0) == 0)
    def _():
        wb_ref[...] = w_ref[...].astype(BF16)

    a = (o_ref[...] * _silu(g_ref[...])).astype(BF16)
    xn = x_ref[...] + mod_ref[0, 2] * _dot(a, wb_ref[...])
    if final:
        ms = jnp.mean(xn * xn, axis=-1, keepdims=True)
        xn = xn * lax.rsqrt(ms + EPS) * fw_ref[...]
    y_ref[...] = xn


def _out_proj(o, g_arr, g_blk, w, x, mod, final_w, final, rows=(0, N_TOK)):
    k = w.shape[0]
    t0 = rows[0] // OUT_TM
    return pl.pallas_call(
        functools.partial(_out_proj_kernel, final=final),
        grid=((rows[1] - rows[0]) // OUT_TM,),
        in_specs=[
            pl.BlockSpec((OUT_TM, k), lambda i: (t0 + i, 0)),
            pl.BlockSpec((OUT_TM, k), lambda i: (t0 + i, g_blk)),
            pl.BlockSpec((k, D_MODEL), lambda i: (0, 0)),
            pl.BlockSpec((OUT_TM, D_MODEL), lambda i: (t0 + i, 0)),
            pl.BlockSpec((1, 3, 1, D_MODEL), lambda i: (_cond_of_tile(t0 + i, OUT_TM), 0, 0, 0)),
            pl.BlockSpec((1, D_MODEL), lambda i: (0, 0)),
        ],
        out_specs=pl.BlockSpec((OUT_TM, D_MODEL), lambda i: (i, 0)),
        out_shape=jax.ShapeDtypeStruct((rows[1] - rows[0], D_MODEL), F32),
        scratch_shapes=[pltpu.VMEM((k, D_MODEL), BF16)],
        compiler_params=_params("arbitrary"),
        name="out_proj",
    )(o, g_arr, w, x, mod, final_w.reshape(1, D_MODEL))


def _rope_tables(d):
    q = d // 4
    t = jnp.arange(DEC_SEQ)
    row = (t // GRID_W).astype(F32)
    col = (t % GRID_W).astype(F32)
    inv = ROPE_BASE ** (-jnp.arange(0, 2 * q, 2, dtype=F32) / (2 * q))
    ar = row[:, None] * inv[None, :]
    ac = col[:, None] * inv[None, :]
    z = jnp.zeros_like(ar)
    cos = jnp.concatenate([jnp.cos(ar), jnp.cos(ar), jnp.cos(ac), jnp.cos(ac)], axis=-1)
    sin_lo = jnp.concatenate([-jnp.sin(ar), z, -jnp.sin(ac), z], axis=-1)
    sin_hi = jnp.concatenate([z, jnp.sin(ar), z, jnp.sin(ac)], axis=-1)
    return cos, sin_lo, sin_hi


def _rope(x, cos, sin_lo, sin_hi, q):
    w = x.shape[-1]
    x_next = pltpu.roll(x, w - q, axis=1)
    x_prev = pltpu.roll(x, q, axis=1)
    return x * cos + x_next * sin_lo + x_prev * sin_hi


RET_QB = 256


def _ret_kernel(lg_ref, q_ref, k_ref, v_ref, gn_ref, *rest, seq, latent):
    if latent:
        cos_ref, slo_ref, shi_ref, s0_ref, _prev_ref, o_ref = rest
    else:
        o_ref, st_ref = rest
    h = pl.program_id(1)
    lgf = lg_ref[0, h]
    lgb = lg_ref[1, h]
    q = q_ref[...]
    k = k_ref[...]
    if latent:
        q = _rope(q, cos_ref[...], slo_ref[...], shi_ref[...], RET_DK // 4)
        k = _rope(k, cos_ref[...], slo_ref[...], shi_ref[...], RET_DK // 4)
    k = k * (RET_DK ** -0.5)
    kb = k.astype(BF16)
    vb = v_ref[...].astype(BF16)
    gn = gn_ref[...]
    for qi in range(seq // RET_QB):
        qblk = q[qi * RET_QB:(qi + 1) * RET_QB]
        s = _dot_nt(qblk.astype(BF16), kb)
        ii = lax.broadcasted_iota(jnp.int32, (RET_QB, seq), 0) + qi * RET_QB
        jj = lax.broadcasted_iota(jnp.int32, (RET_QB, seq), 1)
        gap = (ii - jj).astype(F32)
        dec = (jnp.where(gap >= 0, jnp.exp(lgf * jnp.maximum(gap, 0.0)), 0.0)
               + jnp.where(gap <= 0, jnp.exp(lgb * jnp.maximum(-gap, 0.0)), 0.0))
        o = _dot((s * dec).astype(BF16), vb)
        if latent:
            pos = (lax.broadcasted_iota(jnp.int32, (RET_QB, 1), 0) + qi * RET_QB).astype(F32)
            qf = qblk * jnp.exp(lgf * (pos + 1.0))
            qr = qblk * jnp.exp(lgb * (seq - pos))
            o = o + _dot(qf.astype(BF16), s0_ref[0, 0, 0, 0].astype(BF16))
            o = o + _dot(qr.astype(BF16), s0_ref[0, 0, 1, 0].astype(BF16))
        oc = o - jnp.mean(o, axis=-1, keepdims=True)
        o = oc * lax.rsqrt(jnp.mean(oc * oc, axis=-1, keepdims=True) + GN_EPS) * gn
        o_ref[qi * RET_QB:(qi + 1) * RET_QB, :] = o
    if not latent:
        pos = lax.broadcasted_iota(jnp.int32, (seq, 1), 0).astype(F32)
        kf = k * jnp.exp(lgf * (seq - 1.0 - pos))
        kr = k * jnp.exp(lgb * pos)
        st_ref[0, 0, 0, 0] = _dot_tn(kf.astype(BF16), vb)
        st_ref[0, 0, 1, 0] = _dot_tn(kr.astype(BF16), vb)


def _retention(p, log_g, gn_w, state_ret, j):
    smem = pl.BlockSpec(memory_space=pltpu.SMEM)
    gn = gn_w.reshape(1, RET_V)
    kq = RET_QK // RET_DK
    o_p, st = pl.pallas_call(
        functools.partial(_ret_kernel, seq=SEQ, latent=False),
        grid=(BATCH, RET_HEADS),
        in_specs=[
            smem,
            pl.BlockSpec((SEQ, RET_DK), lambda b, h: (b, h)),
            pl.BlockSpec((SEQ, RET_DK), lambda b, h: (b, kq + h)),
            pl.BlockSpec((SEQ, RET_DV), lambda b, h: (b, kq + h)),
            pl.BlockSpec((1, RET_DV), lambda b, h: (0, h)),
        ],
        out_specs=[
            pl.BlockSpec((SEQ, RET_DV), lambda b, h: (b, h)),
            pl.BlockSpec((1, 1, 2, 1, RET_DK, RET_DV), lambda b, h: (b, 0, 0, h, 0, 0)),
        ],
        out_shape=[
            jax.ShapeDtypeStruct((N_TOK, RET_V), F32),
            jax.ShapeDtypeStruct((BATCH, 1, 2, RET_HEADS, RET_DK, RET_DV), F32),
        ],
        compiler_params=_params("arbitrary", "arbitrary"),
        name="retention_prompt",
    )(log_g, p, p, p, gn)
    cos, slo, shi = _rope_tables(RET_DK)
    rb = N_PROMPT_TOK // DEC_SEQ
    full = pl.BlockSpec((DEC_SEQ, RET_DK), lambda b, h: (0, 0))
    o = pl.pallas_call(
        functools.partial(_ret_kernel, seq=DEC_SEQ, latent=True),
        grid=(DEC_BATCH, RET_HEADS),
        in_specs=[
            smem,
            pl.BlockSpec((DEC_SEQ, RET_DK), lambda b, h: (rb + b, h)),
            pl.BlockSpec((DEC_SEQ, RET_DK), lambda b, h: (rb + b, kq + h)),
            pl.BlockSpec((DEC_SEQ, RET_DV), lambda b, h: (rb + b, kq + h)),
            pl.BlockSpec((1, RET_DV), lambda b, h: (0, h)),
            full, full, full,
            pl.BlockSpec((1, 1, 2, 1, RET_DK, RET_DV), lambda b, h: (b, j, 0, h, 0, 0)),
            pl.BlockSpec(memory_space=pl.ANY),
        ],
        out_specs=pl.BlockSpec((DEC_SEQ, RET_DV), lambda b, h: (rb + b, h)),
        out_shape=jax.ShapeDtypeStruct((N_TOK, RET_V), F32),
        input_output_aliases={9: 0},
        compiler_params=_params("arbitrary", "arbitrary"),
        name="retention_latent",
    )(log_g, p, p, p, gn, cos, slo, shi, state_ret, o_p)
    return o, st


def _layer_ret(x, p, mod, j):
    i = N_MIXERS * j + 0
    proj = _in_proj(x, p['norm_w'][i], mod, p['ret_w_in'][j], 1024)
    log_g = jax.nn.log_sigmoid(p['ret_decay'][j].astype(F32))
    o, st = _retention(proj, log_g, p['ret_gn'][j], p['state_ret'], j)
    x = _out_proj(o, proj, (2 * RET_QK + RET_V) // RET_V, p['ret_w_out'][j], x, mod, p['final_norm_w'], False)
    return x, st


RW_TM = 512
RW_HALO = 8
RW_C = 64
RW_LOCK = 4
RW_PAIRS = 2


def _rwkv_prep_kernel(x_ref, xp_ref, xn_ref, nw_ref, mod_ref, mu_ref, wa_ref, aa_ref, wb_ref, ab_ref,
                      w0_ref, a0_ref, xm_ref, lw_ref, a_ref):
    i = pl.program_id(0)
    nw = nw_ref[...]
    h = _norm_mod(x_ref[...], nw, mod_ref)
    h_before = _norm_mod(xp_ref[RW_HALO - 1:RW_HALO, :], nw, mod_ref)
    h_after = _norm_mod(xn_ref[0:1, :], nw, mod_ref)
    seq = jnp.where(i < N_PROMPT_TOK // RW_TM, SEQ, DEC_SEQ)
    row = lax.broadcasted_iota(jnp.int32, (RW_TM, 1), 0)
    t = (row + i * RW_TM) & (seq - 1)
    prev = jnp.where(row == 0, h_before, pltpu.roll(h, 1, axis=0))
    nxt = jnp.where(row == RW_TM - 1, h_after, pltpu.roll(h, RW_TM - 1, axis=0))
    prev = jnp.where(t == 0, 0.0, prev)
    nxt = jnp.where(t == seq - 1, 0.0, nxt)
    xx = 0.5 * (prev + nxt) - h
    for n, m in enumerate((0, 2, 3, 5)):
        xm_ref[n] = (h + xx * mu_ref[m:m + 1, :]).astype(BF16)
    xw = (h + xx * mu_ref[1:2, :]).astype(BF16)
    xa = (h + xx * mu_ref[4:5, :]).astype(BF16)
    lw = jnp.tanh(_dot(xw, wa_ref[...])).astype(BF16)
    la = _dot(xa, aa_ref[...]).astype(BF16)
    for dr in range(2):
        wl = w0_ref[dr:dr + 1, :] + _dot(lw, wb_ref[dr])
        lw_ref[dr] = -math.exp(-0.5) * _sigmoid(wl)
        a_ref[dr] = _sigmoid(a0_ref[dr:dr + 1, :] + _dot(la, ab_ref[dr]))


def _rwkv_prep(x, norm_w, mod, mu, wa2, aa2, wb_pad, ab_pad, w0, a0):
    nt = N_TOK // RW_TM
    hb = RW_TM // RW_HALO
    last = N_TOK // RW_HALO - 1
    full2 = lambda shape: pl.BlockSpec(shape, lambda i: (0, 0))
    full3 = lambda shape: pl.BlockSpec(shape, lambda i: (0, 0, 0))
    return pl.pallas_call(
        _rwkv_prep_kernel,
        grid=(nt,),
        in_specs=[
            pl.BlockSpec((RW_TM, D_MODEL), lambda i: (i, 0)),
            pl.BlockSpec((RW_HALO, D_MODEL), lambda i: (jnp.maximum(i * hb - 1, 0), 0)),
            pl.BlockSpec((RW_HALO, D_MODEL), lambda i: (jnp.minimum((i + 1) * hb, last), 0)),
            full2((1, D_MODEL)),
            pl.BlockSpec((1, 3, 1, D_MODEL), lambda i: (_cond_of_tile(i, RW_TM), 0, 0, 0)),
            full2((6, D_MODEL)),
            full2((D_MODEL, 2 * RWKV_RANK)),
            full2((D_MODEL, 2 * RWKV_RANK)),
            full3((2, 2 * RWKV_RANK, D_MODEL)),
            full3((2, 2 * RWKV_RANK, D_MODEL)),
            full2((2, D_MODEL)),
            full2((2, D_MODEL)),
        ],
        out_specs=[
            pl.BlockSpec((4, RW_TM, D_MODEL), lambda i: (0, i, 0)),
            pl.BlockSpec((2, RW_TM, D_MODEL), lambda i: (0, i, 0)),
            pl.BlockSpec((2, RW_TM, D_MODEL), lambda i: (0, i, 0)),
        ],
        out_shape=[
            jax.ShapeDtypeStruct((4, N_TOK, D_MODEL), BF16),
            jax.ShapeDtypeStruct((2, N_TOK, D_MODEL), F32),
            jax.ShapeDtypeStruct((2, N_TOK, D_MODEL), F32),
        ],
        compiler_params=_params("arbitrary"),
        name="rwkv_prep",
    )(x, x, x, norm_w.reshape(1, D_MODEL), mod, mu, wa2, aa2, wb_pad, ab_pad, w0, a0)


def _bmm_kernel(a_ref, w_ref, o_ref):
    o_ref[0] = _dot(a_ref[0], w_ref[...])


def _rwkv_rkvg(xm, w):
    tm = 1024
    return pl.pallas_call(
        _bmm_kernel,
        grid=(4, N_TOK // tm),
        in_specs=[
            pl.BlockSpec((1, tm, D_MODEL), lambda n, i: (n, i, 0)),
            pl.BlockSpec((D_MODEL, D_MODEL), lambda n, i: (0, n)),
        ],
        out_specs=pl.BlockSpec((1, tm, D_MODEL), lambda n, i: (n, i, 0)),
        out_shape=jax.ShapeDtypeStruct((4, N_TOK, D_MODEL), F32),
        compiler_params=_params("arbitrary", "arbitrary"),
        name="rwkv_rkvg",
    )(xm, w)


def _head_sum(x, first):
    s0 = jnp.sum(jnp.where(first, x, 0.0), axis=-1, keepdims=True)
    s1 = jnp.sum(jnp.where(first, 0.0, x), axis=-1, keepdims=True)
    return jnp.where(first, s0, s1)


def _stack_heads(x, first):
    return jnp.concatenate([jnp.where(first, x, 0.0), jnp.where(first, 0.0, x)], axis=0)


def _cumsum_rows(tri, x):
    hi = x.astype(BF16)
    r1 = x - hi.astype(F32)
    mid = r1.astype(BF16)
    lo = (r1 - mid.astype(F32)).astype(BF16)
    return _dot(tri, hi) + _dot(tri, mid) + _dot(tri, lo)


def _rwkv_chunk_kernel(*refs, seq, zero_init):
    if zero_init:
        (rkv_ref, lw_ref, a_ref, kkp_ref, kap_ref, rkp_ref, gn_ref, o_ref, st_ref,
         kk_scr, y_scr, tar_scr, lrb_scr, b2_scr, w2_scr, yl_scr, kv_scr, pc_scr) = refs
    else:
        (rkv_ref, lw_ref, a_ref, kkp_ref, kap_ref, rkp_ref, gn_ref, s0_ref, _prev_ref, o_ref,
         kk_scr, y_scr, tar_scr, lrb_scr, b2_scr, w2_scr, yl_scr, kv_scr, pc_scr) = refs
    c_len = RW_C
    n_ch = seq // c_len
    rows2 = 2 * c_len
    first = _first_half_lanes()
    kap = kap_ref[...]

    kk = rkv_ref[1] * kkp_ref[...]
    kk_scr[...] = kk * lax.rsqrt(jnp.maximum(_head_sum(kk * kk, first), 1e-12))

    rr = lax.broadcasted_iota(jnp.int32, (rows2, rows2), 0)
    cc = lax.broadcasted_iota(jnp.int32, (rows2, rows2), 1)
    eye = (rr == cc).astype(F32)
    tr = lax.broadcasted_iota(jnp.int32, (c_len, c_len), 0)
    tc = lax.broadcasted_iota(jnp.int32, (c_len, c_len), 1)

    def same(shift):
        return (rr >> shift) == (cc >> shift)

    head = same(6)
    strict = (head & (cc < rr), head & (cc > rr))
    incl = (head & (cc <= rr), head & (cc >= rr))
    tri = ((tc <= tr).astype(BF16), (tc >= tr).astype(BF16))
    last = (c_len - 1, 0)

    def phase1(chains):
        dirs = [dr for dr, _ in chains]
        rows = [pl.ds(pl.multiple_of(c * c_len, c_len), c_len) for _, c in chains]
        lw = [lw_ref[dr, rw, :] for dr, rw in zip(dirs, rows)]
        cum = [_cumsum_rows(tri[dr], x) for dr, x in zip(dirs, lw)]
        a2, r2, b2, k2, v2, pc = [], [], [], [], [], []
        for dr, rw, lw_c, cum_c in zip(dirs, rows, lw, cum):
            a = a_ref[dr, rw, :]
            k = rkv_ref[1, rw, :]
            kk_c = kk_scr[rw, :]
            e_inc = jnp.exp(cum_c)
            e_inv = jnp.exp(-cum_c)
            a2.append(_stack_heads(-kk_c * jnp.exp(cum_c - lw_c), first).astype(BF16))
            r2.append(_stack_heads(rkv_ref[0, rw, :] * e_inc, first).astype(BF16))
            b2.append(_stack_heads(kk_c * a * e_inv, first).astype(BF16))
            k2.append(_stack_heads(k * (1.0 + (a - 1.0) * kap) * e_inv, first).astype(BF16))
            v2.append(_stack_heads(rkv_ref[2, rw, :], first).astype(BF16))
            pc.append(e_inc[last[dr]:last[dr] + 1, :])
        g = [_dot_nt(jnp.concatenate([x, y], axis=0), jnp.concatenate([z, w], axis=0))
             for x, y, z, w in zip(a2, r2, b2, k2)]
        l_ab = [jnp.where(strict[dr], x[:rows2, :rows2], 0.0) for dr, x in zip(dirs, g)]
        t = [eye + jnp.where(same(1), x, 0.0) for x in l_ab]
        side = {}
        for shift in range(1, 6):
            sib = same(shift + 1) & ~same(shift)
            tb = [x.astype(BF16) for x in t]
            mid = [_dot(jnp.where(sib, x, 0.0).astype(BF16), y) for x, y in zip(l_ab, tb)]
            if shift == 1:
                side['lv'] = [_dot(jnp.where(strict[dr], x[:rows2, rows2:], 0.0).astype(BF16), y)
                              for dr, x, y in zip(dirs, g, v2)]
            elif shift == 2:
                side['yl'] = [_dot(jnp.where(incl[dr], x[rows2:, rows2:], 0.0).astype(BF16), y)
                              for dr, x, y in zip(dirs, g, v2)]
            elif shift == 3:
                side['kv'] = [_dot_tn(x, y) for x, y in zip(v2, k2)]
            t = [x + _dot(y, z.astype(BF16)) for x, y, z in zip(t, tb, mid)]
        tb = [x.astype(BF16) for x in t]
        ta = [_dot(x, y) for x, y in zip(tb, a2)]
        w2 = [_dot(x, y.astype(BF16)) for x, y in zip(tb, side['lv'])]
        for i, (dr, c) in enumerate(chains):
            n = dr * n_ch + c
            tar_scr[n, :rows2, :] = ta[i].astype(BF16)
            tar_scr[n, rows2:, :] = r2[i]
            w2_scr[n] = w2[i]
            yl_scr[n] = side['yl'][i]
            kv_scr[n] = side['kv'][i]
            lrb_scr[n] = jnp.where(incl[dr], g[i][rows2:, :rows2], 0.0).astype(BF16)
            b2_scr[n] = b2[i]
            pc_scr[n] = pc[i]

    def body1(grp, carry):
        phase1([(dr, grp * RW_LOCK + j) for j in range(RW_LOCK) for dr in range(2)])
        return carry

    if n_ch == RW_LOCK:
        body1(0, 0)
    else:
        lax.fori_loop(0, n_ch // RW_LOCK, body1, 0)

    def body2(i, carry):
        cs = (i, n_ch - 1 - i)
        ns = [dr * n_ch + c for dr, c in enumerate(cs)]
        x = [_dot_nt(tar_scr[n], s2.astype(BF16)) for n, s2 in zip(ns, carry)]
        u2 = [(xx[:rows2] + w2_scr[n]).astype(BF16) for n, xx in zip(ns, x)]
        upd = [_dot_tn(u, b2_scr[n]) for n, u in zip(ns, u2)]
        yb = [_dot(lrb_scr[n], u) for n, u in zip(ns, u2)]
        out = []
        for dr, (c, n) in enumerate(zip(cs, ns)):
            y2 = x[dr][rows2:] + yb[dr] + yl_scr[n]
            y_scr[dr, pl.ds(pl.multiple_of(c * c_len, c_len), c_len), :] = y2[:c_len] + y2[c_len:]
            out.append((carry[dr] + upd[dr] + kv_scr[n]) * pc_scr[n])
        return tuple(out)

    if zero_init:
        init = (jnp.zeros((rows2, LANES), F32),) * 2
    else:
        init = (s0_ref[0, 0, 0], s0_ref[0, 1, 0])
    s_f, s_b = lax.fori_loop(0, n_ch, body2, init)

    y = y_scr[0] + y_scr[1]
    yc = y - _head_sum(y, first) * (1.0 / RWKV_HD)
    o = yc * lax.rsqrt(_head_sum(yc * yc, first) * (1.0 / RWKV_HD) + GN_EPS) * gn_ref[...]
    r = rkv_ref[0]
    k = rkv_ref[1]
    v = rkv_ref[2]
    for dr in range(2):
        kd = k * (1.0 + (a_ref[dr] - 1.0) * kap)
        o = o + _head_sum(r * kd * rkp_ref[...], first) * v
    o_ref[...] = o

    if zero_init:
        for dr, s2 in enumerate((s_f, s_b)):
            st_ref[0, 0, dr, 0] = s2[:RWKV_HD, :RWKV_HD]
            st_ref[0, 0, dr, 1] = s2[RWKV_HD:, RWKV_HD:]


def _rwkv_chunked(rkvg, lw, a, kkp, kap, rkp, gn, s0_pairs):
    npair = RWKV_HEADS // 2
    par = lambda *_: pl.BlockSpec((1, LANES), lambda s, p: (0, p))

    def scratch(seq):
        n = 2 * (seq // RW_C)
        r2 = 2 * RW_C
        return [
            pltpu.VMEM((seq, LANES), F32), pltpu.VMEM((2, seq, LANES), F32),
            pltpu.VMEM((n, 2 * r2, LANES), BF16), pltpu.VMEM((n, r2, r2), BF16), pltpu.VMEM((n, r2, LANES), BF16),
            pltpu.VMEM((n, r2, LANES), F32), pltpu.VMEM((n, r2, LANES), F32), pltpu.VMEM((n, r2, LANES), F32),
            pltpu.VMEM((n, 1, LANES), F32),
        ]

    def seq_specs(seq, rb):
        return [
            pl.BlockSpec((3, seq, LANES), lambda s, p: (0, rb + s, p)),
            pl.BlockSpec((2, seq, LANES), lambda s, p: (0, rb + s, p)),
            pl.BlockSpec((2, seq, LANES), lambda s, p: (0, rb + s, p)),
            par(), par(), par(), par(),
        ]

    pars = [u.reshape(1, D_MODEL) for u in (kkp, kap, rkp, gn)]
    o_p, st = pl.pallas_call(
        functools.partial(_rwkv_chunk_kernel, seq=SEQ, zero_init=True),
        grid=(BATCH, npair),
        in_specs=seq_specs(SEQ, 0),
        out_specs=[
            pl.BlockSpec((SEQ, LANES), lambda s, p: (s, p)),
            pl.BlockSpec((1, 1, 2, 2, RWKV_HD, RWKV_HD), lambda s, p: (s, 0, 0, p, 0, 0)),
        ],
        out_shape=[
            jax.ShapeDtypeStruct((N_TOK, D_MODEL), F32),
            jax.ShapeDtypeStruct((BATCH, 1, 2, RWKV_HEADS, RWKV_HD, RWKV_HD), F32),
        ],
        scratch_shapes=scratch(SEQ),
        compiler_params=_params("arbitrary", "arbitrary"),
        name="rwkv_prompt",
    )(rkvg, lw, a, *pars)
    rb = N_PROMPT_TOK // DEC_SEQ
    o = pl.pallas_call(
        functools.partial(_rwkv_chunk_kernel, seq=DEC_SEQ, zero_init=False),
        grid=(DEC_BATCH, npair),
        in_specs=seq_specs(DEC_SEQ, rb) + [
            pl.BlockSpec((1, 2, 1, LANES, LANES), lambda s, p: (s, 0, p, 0, 0)),
            pl.BlockSpec(memory_space=pl.ANY),
        ],
        out_specs=pl.BlockSpec((DEC_SEQ, LANES), lambda s, p: (rb + s, p)),
        out_shape=jax.ShapeDtypeStruct((N_TOK, D_MODEL), F32),
        input_output_aliases={8: 0},
        scratch_shapes=scratch(DEC_SEQ),
        compiler_params=_params("arbitrary", "arbitrary"),
        name="rwkv_latent",
    )(rkvg, lw, a, *pars, s0_pairs, o_p)
    return o, st


def _split3(x):
    hi = x.astype(BF16)
    r1 = x - hi.astype(F32)
    mid = r1.astype(BF16)
    return hi, mid, (r1 - mid.astype(F32)).astype(BF16)


def _rwkv_kernel(*refs, seq, zero_init, pg, np2):
    n_in = 7 if zero_init else 9
    rkv_ref, lw_ref, a_ref, kkp_ref, kap_ref, rkp_ref, gn_ref = refs[:7]
    if zero_init:
        o_ref, st_ref = refs[n_in:n_in + 2]
        scr = refs[n_in + 2:]
    else:
        s0_ref = refs[7]
        o_ref = refs[n_in]
        scr = refs[n_in + 1:]
    kk_scr, cum_scr, bon_scr, y_scr, s_scr, tar_scr, lrb_scr, b2_scr, w2_scr, yl_scr, kv_scr, pc_scr = scr
    c_len = RW_C
    n_ch = seq // c_len
    rows2 = 2 * c_len
    grp = pl.program_id(1)
    defer = np2 > pg
    base = grp * pg if defer else 0
    first = _first_half_lanes()

    rr = lax.broadcasted_iota(jnp.int32, (rows2, rows2), 0)
    cc = lax.broadcasted_iota(jnp.int32, (rows2, rows2), 1)
    eye = (rr == cc).astype(F32)

    def same(shift):
        return (rr >> shift) == (cc >> shift)

    head = same(6)
    strict = (head & (cc < rr), head & (cc > rr))
    incl = (head & (cc <= rr), head & (cc >= rr))
    last = (c_len - 1, 0)
    head_ones = head.astype(BF16)

    cs_rows = min(seq, 256)
    tr = lax.broadcasted_iota(jnp.int32, (cs_rows, cs_rows), 0)
    tc = lax.broadcasted_iota(jnp.int32, (cs_rows, cs_rows), 1)
    chunk = (tr >> 6) == (tc >> 6)
    tri = ((chunk & (tc <= tr)).astype(BF16), (chunk & (tc >= tr)).astype(BF16))
    for p in range(pg):
        ln = slice(p * LANES, (p + 1) * LANES)
        kap = kap_ref[:, ln]
        r = rkv_ref[0, :, ln]
        k = rkv_ref[1, :, ln]
        v = rkv_ref[2, :, ln]
        kk = k * kkp_ref[:, ln]
        kk_scr[p] = kk * lax.rsqrt(jnp.maximum(_head_sum(kk * kk, first), 1e-12))
        bonus = None
        for dr in range(2):
            for r0 in range(0, seq, cs_rows):
                parts = _split3(lw_ref[dr, r0:r0 + cs_rows, ln])
                cum_scr[p, dr, r0:r0 + cs_rows, :] = (_dot(tri[dr], parts[0]) + _dot(tri[dr], parts[1])
                                                      + _dot(tri[dr], parts[2]))
            kd = k * (1.0 + (a_ref[dr, :, ln] - 1.0) * kap)
            term = _head_sum(r * kd * rkp_ref[:, ln], first) * v
            bonus = term if bonus is None else bonus + term
        bon_scr[base + p] = bonus

    def phase1(chains):
        dirs = [dr for _, dr, _ in chains]
        a2, r2, b2, k2, v2, pc = [], [], [], [], [], []
        for p, dr, c in chains:
            ln = slice(p * LANES, (p + 1) * LANES)
            rw = pl.ds(pl.multiple_of(c * c_len, c_len), c_len)
            a = a_ref[dr, rw, ln]
            k = rkv_ref[1, rw, ln]
            kk_c = kk_scr[p, rw, :]
            cum_c = cum_scr[p, dr, rw, :]
            e_inc = jnp.exp(cum_c)
            e_inv = jnp.exp(-cum_c)
            a2.append(_stack_heads(-kk_c * jnp.exp(cum_c - lw_ref[dr, rw, ln]), first).astype(BF16))
            r2.append(_stack_heads(rkv_ref[0, rw, ln] * e_inc, first).astype(BF16))
            b2.append(_stack_heads(kk_c * a * e_inv, first).astype(BF16))
            k2.append(_stack_heads(k * (1.0 + (a - 1.0) * kap_ref[:, ln]) * e_inv, first).astype(BF16))
            v2.append(_stack_heads(rkv_ref[2, rw, ln], first).astype(BF16))
            pc.append(e_inc[last[dr]:last[dr] + 1, :])
        g = [_dot_nt(jnp.concatenate([x, y], axis=0), jnp.concatenate([z, w], axis=0))
             for x, y, z, w in zip(a2, r2, b2, k2)]
        l_ab = [jnp.where(strict[dr], x[:rows2, :rows2], 0.0) for dr, x in zip(dirs, g)]
        t = [eye + jnp.where(same(1), x, 0.0) for x in l_ab]
        side = {}
        for shift in range(1, 6):
            sib = same(shift + 1) & ~same(shift)
            tb = [x.astype(BF16) for x in t]
            mid = [_dot(jnp.where(sib, x, 0.0).astype(BF16), y) for x, y in zip(l_ab, tb)]
            if shift == 1:
                side['lv'] = [_dot(jnp.where(strict[dr], x[:rows2, rows2:], 0.0).astype(BF16), y)
                              for dr, x, y in zip(dirs, g, v2)]
            elif shift == 2:
                side['yl'] = [_dot(jnp.where(incl[dr], x[rows2:, rows2:], 0.0).astype(BF16), y)
                              for dr, x, y in zip(dirs, g, v2)]
            elif shift == 3:
                side['kv'] = [_dot_tn(x, y) for x, y in zip(v2, k2)]
            t = [x + _dot(y, z.astype(BF16)) for x, y, z in zip(t, tb, mid)]
        tb = [x.astype(BF16) for x in t]
        ta = [_dot(x, y) for x, y in zip(tb, a2)]
        w2 = [_dot(x, y.astype(BF16)) for x, y in zip(tb, side['lv'])]
        for i, (p, dr, c) in enumerate(chains):
            n = ((base + p) * 2 + dr) * n_ch + c
            tar_scr[n, :rows2, :] = ta[i].astype(BF16)
            tar_scr[n, rows2:, :] = r2[i]
            w2_scr[n] = w2[i]
            yl_scr[n] = side['yl'][i]
            kv_scr[n] = side['kv'][i]
            lrb_scr[n] = jnp.where(incl[dr], g[i][rows2:, :rows2], 0.0).astype(BF16)
            b2_scr[n] = b2[i]
            pc_scr[n] = pc[i]

    def body1(cg, carry):
        phase1([(p, dr, cg * RW_LOCK + j) for p in range(pg) for j in range(RW_LOCK) for dr in range(2)])
        return carry

    if n_ch == RW_LOCK:
        body1(0, 0)
    else:
        lax.fori_loop(0, n_ch // RW_LOCK, body1, 0)

    def finish():
        for p in range(np2):
            for dr in range(2):
                if zero_init:
                    s_scr[2 * p + dr] = jnp.zeros((rows2, LANES), F32)
                else:
                    s_scr[2 * p + dr] = s0_ref[0, dr, p]

        def body2(i, carry):
            cs = (i, n_ch - 1 - i)
            ids = [(p, dr) for p in range(np2) for dr in range(2)]
            ns = [(p * 2 + dr) * n_ch + cs[dr] for p, dr in ids]
            x = [_dot_nt(tar_scr[n], s_scr[2 * p + dr].astype(BF16)) for n, (p, dr) in zip(ns, ids)]
            u2 = [(xx[:rows2] + w2_scr[n]).astype(BF16) for n, xx in zip(ns, x)]
            upd = [_dot_tn(u, b2_scr[n]) for n, u in zip(ns, u2)]
            yb = [_dot(lrb_scr[n], u) for n, u in zip(ns, u2)]
            for j, (n, (p, dr)) in enumerate(zip(ns, ids)):
                y2 = x[j][rows2:] + yb[j] + yl_scr[n]
                y_scr[p, dr, pl.ds(pl.multiple_of(cs[dr] * c_len, c_len), c_len), :] = y2[:c_len] + y2[c_len:]
                s_scr[2 * p + dr] = (s_scr[2 * p + dr] + upd[j] + kv_scr[n]) * pc_scr[n]
            return carry

        lax.fori_loop(0, n_ch, body2, 0)

        def head_mean(xs):
            parts = [_split3(x) for x in xs]
            return [(_dot(a, head_ones) + _dot(b, head_ones) + _dot(c, head_ones)) * (1.0 / RWKV_HD)
                    for a, b, c in parts]

        ys = [y_scr[p, 0] + y_scr[p, 1] for p in range(np2)]
        yc = [y - m for y, m in zip(ys, head_mean(ys))]
        var = head_mean([x * x for x in yc])
        for p in range(np2):
            ln = slice(p * LANES, (p + 1) * LANES)
            o_ref[:, ln] = yc[p] * lax.rsqrt(var[p] + GN_EPS) * gn_ref[:, ln] + bon_scr[p]
            if zero_init:
                for dr in range(2):
                    s2 = s_scr[2 * p + dr]
                    st_ref[0, 0, dr, 2 * p] = s2[:RWKV_HD, :RWKV_HD]
                    st_ref[0, 0, dr, 2 * p + 1] = s2[RWKV_HD:, RWKV_HD:]

    if defer:
        pl.when(grp == pl.num_programs(1) - 1)(finish)
    else:
        finish()


def _rwkv_mixer(rkvg, lw, a, kkp, kap, rkp, gn, s0_pairs):
    npair = RWKV_HEADS // 2
    pg = RW_PAIRS
    wl = pg * LANES
    r2 = 2 * RW_C

    def scratch(seq, np2):
        n = 2 * np2 * (seq // RW_C)
        return [
            pltpu.VMEM((pg, seq, LANES), F32), pltpu.VMEM((pg, 2, seq, LANES), F32),
            pltpu.VMEM((np2, seq, LANES), F32), pltpu.VMEM((np2, 2, seq, LANES), F32),
            pltpu.VMEM((2 * np2, r2, LANES), F32),
            pltpu.VMEM((n, 2 * r2, LANES), BF16), pltpu.VMEM((n, r2, r2), BF16), pltpu.VMEM((n, r2, LANES), BF16),
            pltpu.VMEM((n, r2, LANES), F32), pltpu.VMEM((n, r2, LANES), F32), pltpu.VMEM((n, r2, LANES), F32),
            pltpu.VMEM((n, 1, LANES), F32),
        ]

    def seq_specs(seq, rb, gn_spec):
        par = pl.BlockSpec((1, wl), lambda s, g: (0, g))
        return [
            pl.BlockSpec((3, seq, wl), lambda s, g: (0, rb + s, g)),
            pl.BlockSpec((2, seq, wl), lambda s, g: (0, rb + s, g)),
            pl.BlockSpec((2, seq, wl), lambda s, g: (0, rb + s, g)),
            par, par, par, gn_spec,
        ]

    pars = [u.reshape(1, D_MODEL) for u in (kkp, kap, rkp, gn)]
    o_p, st = pl.pallas_call(
        functools.partial(_rwkv_kernel, seq=SEQ, zero_init=True, pg=pg, np2=npair),
        grid=(BATCH, npair // pg),
        in_specs=seq_specs(SEQ, 0, pl.BlockSpec((1, D_MODEL), lambda s, g: (0, 0))),
        out_specs=[
            pl.BlockSpec((SEQ, D_MODEL), lambda s, g: (s, 0)),
            pl.BlockSpec((1, 1, 2, RWKV_HEADS, RWKV_HD, RWKV_HD), lambda s, g: (s, 0, 0, 0, 0, 0)),
        ],
        out_shape=[
            jax.ShapeDtypeStruct((N_TOK, D_MODEL), F32),
            jax.ShapeDtypeStruct((BATCH, 1, 2, RWKV_HEADS, RWKV_HD, RWKV_HD), F32),
        ],
        scratch_shapes=scratch(SEQ, npair),
        compiler_params=_params("arbitrary", "arbitrary"),
        name="rwkv_prompt",
    )(rkvg, lw, a, *pars)
    rb = N_PROMPT_TOK // DEC_SEQ
    o = pl.pallas_call(
        functools.partial(_rwkv_kernel, seq=DEC_SEQ, zero_init=False, pg=pg, np2=pg),
        grid=(DEC_BATCH, npair // pg),
        in_specs=seq_specs(DEC_SEQ, rb, pl.BlockSpec((1, wl), lambda s, g: (0, g))) + [
            pl.BlockSpec((1, 2, pg, LANES, LANES), lambda s, g: (s, 0, g, 0, 0)),
            pl.BlockSpec(memory_space=pl.ANY),
        ],
        out_specs=pl.BlockSpec((DEC_SEQ, wl), lambda s, g: (rb + s, g)),
        out_shape=jax.ShapeDtypeStruct((N_TOK, D_MODEL), F32),
        input_output_aliases={8: 0},
        scratch_shapes=scratch(DEC_SEQ, pg),
        compiler_params=_params("arbitrary", "arbitrary"),
        name="rwkv_latent",
    )(rkvg, lw, a, *pars, s0_pairs, o_p)
    return o, st


def _state_pairs(s0):
    s = s0.reshape(DEC_BATCH, 2, RWKV_HEADS // 2, 2, RWKV_HD, RWKV_HD)
    z = jnp.zeros_like(s[:, :, :, 0])
    top = jnp.concatenate([s[:, :, :, 0], z], axis=-1)
    bot = jnp.concatenate([z, s[:, :, :, 1]], axis=-1)
    return jnp.concatenate([top, bot], axis=-2)


def _layer_rwkv(x, p, mod, j):
    i = N_MIXERS * j + 1
    wa, wb, aa, ab = p['rwkv_wA'][j], p['rwkv_wB'][j], p['rwkv_aA'][j], p['rwkv_aB'][j]
    z = jnp.zeros_like(wb[0])
    wa2 = jnp.concatenate([wa[0], wa[1]], axis=1).astype(BF16)
    aa2 = jnp.concatenate([aa[0], aa[1]], axis=1).astype(BF16)
    wb_pad = jnp.stack([jnp.concatenate([wb[0], z]), jnp.concatenate([z, wb[1]])]).astype(BF16)
    ab_pad = jnp.stack([jnp.concatenate([ab[0], z]), jnp.concatenate([z, ab[1]])]).astype(BF16)
    xm, lw, a = _rwkv_prep(x, p['norm_w'][i], mod, p['rwkv_mu'][j], wa2, aa2, wb_pad, ab_pad,
                           p['rwkv_w0'][j], p['rwkv_a0'][j])
    rkvg = _rwkv_rkvg(xm, p['rwkv_w_in'][j].astype(BF16))
    o, st = _rwkv_mixer(rkvg, lw, a, p['rwkv_kk'][j], p['rwkv_ka'][j], p['rwkv_rk'][j], p['rwkv_gn'][j],
                          _state_pairs(p['state_rwkv'][:, j]))
    x = _out_proj(o, rkvg[3], 0, p['rwkv_w_out'][j], x, mod, p['final_norm_w'], False)
    return x, st


DIFF_W = 2 * DIFF_HD
ATT_QB = 256


def _first_half_lanes():
    return lax.broadcasted_iota(jnp.int32, (1, LANES), 1) < LANES // 2


def _diff_lambda(lam_ref, lam_init):
    lp = lam_ref[...]
    return (jnp.exp(jnp.sum(lp[0:1] * lp[1:2], keepdims=True))
            - jnp.exp(jnp.sum(lp[2:3] * lp[3:4], keepdims=True)) + lam_init)


def _diff_head(q, key_sets, lam, gn, lam_init):
    first = _first_half_lanes()
    scale = DIFF_HD ** -0.5
    probs = []
    for comp in range(2):
        qm = jnp.where(first if comp == 0 else ~first, q, 0.0).astype(BF16)
        s = [_dot_nt(qm, kb) * scale for kb, _ in key_sets]
        m = functools.reduce(jnp.maximum, [jnp.max(u, axis=-1, keepdims=True) for u in s])
        e = [jnp.exp(u - m) for u in s]
        den = functools.reduce(lambda x, y: x + y, [jnp.sum(u, axis=-1, keepdims=True) for u in e])
        probs.append([u / den for u in e])
    o = None
    for n, (_, vb) in enumerate(key_sets):
        part = _dot((probs[0][n] - lam * probs[1][n]).astype(BF16), vb)
        o = part if o is None else o + part
    o = o * lax.rsqrt(jnp.mean(o * o, axis=-1, keepdims=True) + EPS) * gn
    return o * (1.0 - lam_init)


def _diff_prompt_kernel(lam_ref, q_ref, k_ref, v_ref, gn_ref, o_ref, ck_ref, cv_ref, *, lam_init):
    lam = _diff_lambda(lam_ref, lam_init)
    for h in range(DIFF_HEADS):
        sl = slice(h * DIFF_W, (h + 1) * DIFF_W)
        k = k_ref[:, sl]
        v = v_ref[:, sl]
        ck_ref[0, 0, h] = k
        cv_ref[0, 0, h] = v
        keys = [(k.astype(BF16), v.astype(BF16))]
        o_ref[:, sl] = _diff_head(q_ref[:, sl], keys, lam, gn_ref[:, sl], lam_init)


def _diff_latent_kernel(lam_ref, q_ref, k_ref, v_ref, ck_ref, cv_ref, cos_ref, slo_ref, shi_ref, gn_ref,
                        _prev_ref, o_ref, *, lam_init):
    lam = _diff_lambda(lam_ref, lam_init)
    tabs = (cos_ref[...], slo_ref[...], shi_ref[...])
    q = _rope(q_ref[...], *tabs, DIFF_HD // 4)
    k = _rope(k_ref[...], *tabs, DIFF_HD // 4)
    keys = [(k.astype(BF16), v_ref[...].astype(BF16)),
            (ck_ref[0, 0, 0].astype(BF16), cv_ref[0, 0, 0].astype(BF16))]
    gn = gn_ref[...]
    for qi in range(DEC_SEQ // ATT_QB):
        sl = slice(qi * ATT_QB, (qi + 1) * ATT_QB)
        o_ref[sl, :] = _diff_head(q[sl], keys, lam, gn, lam_init)


def _diff_attention(proj, lam_p, gn_w, cache_k, cache_v, j, lam_init):
    gn = gn_w.reshape(1, D_MODEL)
    lam_spec = pl.BlockSpec((4, DIFF_HD), lambda *_: (0, 0))
    cache_out = pl.BlockSpec((1, 1, DIFF_HEADS, SEQ, DIFF_W), lambda b: (b, 0, 0, 0, 0))
    cache_shape = jax.ShapeDtypeStruct((BATCH, 1, DIFF_HEADS, SEQ, DIFF_W), F32)
    o_p, new_k, new_v = pl.pallas_call(
        functools.partial(_diff_prompt_kernel, lam_init=lam_init),
        grid=(BATCH,),
        in_specs=[
            lam_spec,
            pl.BlockSpec((SEQ, D_MODEL), lambda b: (b, 0)),
            pl.BlockSpec((SEQ, D_MODEL), lambda b: (b, 1)),
            pl.BlockSpec((SEQ, D_MODEL), lambda b: (b, 2)),
            pl.BlockSpec((1, D_MODEL), lambda b: (0, 0)),
        ],
        out_specs=[pl.BlockSpec((SEQ, D_MODEL), lambda b: (b, 0)), cache_out, cache_out],
        out_shape=[jax.ShapeDtypeStruct((N_TOK, D_MODEL), F32), cache_shape, cache_shape],
        compiler_params=_params("arbitrary"),
        name="diff_prompt",
    )(lam_p, proj, proj, proj, gn)
    cos, slo, shi = (jnp.concatenate([u, u], axis=-1) for u in _rope_tables(DIFF_HD))
    rb = N_PROMPT_TOK // DEC_SEQ
    nh = DIFF_HEADS
    tab = pl.BlockSpec((DEC_SEQ, DIFF_W), lambda b, h: (0, 0))
    cache = pl.BlockSpec((1, 1, 1, PAST_LEN, DIFF_W), lambda b, h: (b, j, h, 0, 0))
    o = pl.pallas_call(
        functools.partial(_diff_latent_kernel, lam_init=lam_init),
        grid=(DEC_BATCH, nh),
        in_specs=[
            lam_spec,
            pl.BlockSpec((DEC_SEQ, DIFF_W), lambda b, h: (rb + b, h)),
            pl.BlockSpec((DEC_SEQ, DIFF_W), lambda b, h: (rb + b, nh + h)),
            pl.BlockSpec((DEC_SEQ, DIFF_W), lambda b, h: (rb + b, 2 * nh + h)),
            cache, cache, tab, tab, tab,
            pl.BlockSpec((1, DIFF_W), lambda b, h: (0, h)),
            pl.BlockSpec(memory_space=pl.ANY),
        ],
        out_specs=pl.BlockSpec((DEC_SEQ, DIFF_W), lambda b, h: (rb + b, h)),
        out_shape=jax.ShapeDtypeStruct((N_TOK, D_MODEL), F32),
        input_output_aliases={10: 0},
        compiler_params=_params("arbitrary", "arbitrary"),
        name="diff_latent",
    )(lam_p, proj, proj, proj, cache_k, cache_v, cos, slo, shi, gn, o_p)
    return o, new_k, new_v


def _layer_diff(x, p, mod, j, i):
    lam_init = 0.8 - 0.6 * math.exp(-0.3 * i)
    proj = _in_proj(x, p['norm_w'][i], mod, p['diff_w_in'][j], 1024)
    o, new_k, new_v = _diff_attention(proj, p['diff_lambda'][j], p['diff_gn'][j], p['cache_diff_k'],
                                      p['cache_diff_v'], j, lam_init)
    x = _out_proj(o, proj, 3, p['diff_w_out'][j], x, mod, p['final_norm_w'], False)
    return x, new_k, new_v


NA_ROWS = DEC_SEQ // GRID_W
NA_WR = min(NA_WIN_R, NA_ROWS)
NA_LOC = NA_WR * GRID_W


def _na_prompt_kernel(q_ref, k_ref, v_ref, o_ref, ck_ref, cv_ref):
    first = _first_half_lanes()
    scale = NA_HD ** -0.5
    for pr in range(NA_HEADS // 2):
        sl = slice(pr * LANES, (pr + 1) * LANES)
        q = q_ref[:, sl]
        k = k_ref[:, sl]
        v = v_ref[:, sl]
        for half in range(2):
            ck_ref[0, 0, 2 * pr + half] = k[:, half * NA_HD:(half + 1) * NA_HD]
            cv_ref[0, 0, 2 * pr + half] = v[:, half * NA_HD:(half + 1) * NA_HD]
        kb = k.astype(BF16)
        vb = v.astype(BF16)
        outs = []
        for half in range(2):
            qm = jnp.where(first if half == 0 else ~first, q, 0.0).astype(BF16)
            pr_ = _softmax_rows(_dot_nt(qm, kb) * scale).astype(BF16)
            outs.append(_dot(pr_, vb))
        o_ref[:, sl] = jnp.where(first, outs[0], outs[1])


def _na_latent_kernel(q_ref, k_ref, v_ref, kc_ref, vc_ref, tab_ref, _prev_ref, o_ref):
    first = _first_half_lanes()
    scale = NA_HD ** -0.5
    kb = k_ref[...].astype(BF16)
    vb = v_ref[...].astype(BF16)
    kcb = kc_ref[0, 0].astype(BF16)
    vcb = vc_ref[0, 0].astype(BF16)
    qcol = lax.broadcasted_iota(jnp.int32, (GRID_W, NA_LOC), 0)
    kcol = lax.broadcasted_iota(jnp.int32, (GRID_W, NA_LOC), 1) & (GRID_W - 1)
    cstart = jnp.clip(qcol - NA_WIN_C // 2, 0, GRID_W - NA_WIN_C)
    col_ok = (kcol >= cstart) & (kcol < cstart + NA_WIN_C)
    for r in range(NA_ROWS):
        rs = min(max(r - NA_WR // 2, 0), NA_ROWS - NA_WR)
        rows = slice(r * GRID_W, (r + 1) * GRID_W)
        q = q_ref[rows, :]
        kl = kb[rs * GRID_W:(rs + NA_WR) * GRID_W]
        vl = vb[rs * GRID_W:(rs + NA_WR) * GRID_W]
        outs = []
        for half in range(2):
            qm = jnp.where(first if half == 0 else ~first, q, 0.0).astype(BF16)
            bias = []
            for w in range(0, NA_WR, 2):
                src = jnp.broadcast_to(tab_ref[half, rs + w - r + NA_WIN_R - 1], (GRID_W, LANES))
                bias.append(pltpu.roll(src, LANES - (NA_WIN_C - 1), axis=1, stride=1, stride_axis=0))
            s_loc = _dot_nt(qm, kl) * scale + jnp.concatenate(bias, axis=1)
            s_loc = jnp.where(col_ok, s_loc, -jnp.inf)
            s_ctx = _dot_nt(qm, kcb) * scale
            m = jnp.maximum(jnp.max(s_loc, axis=-1, keepdims=True), jnp.max(s_ctx, axis=-1, keepdims=True))
            e_loc = jnp.exp(s_loc - m)
            e_ctx = jnp.exp(s_ctx - m)
            den = jnp.sum(e_loc, axis=-1, keepdims=True) + jnp.sum(e_ctx, axis=-1, keepdims=True)
            outs.append(_dot((e_loc / den).astype(BF16), vl) + _dot((e_ctx / den).astype(BF16), vcb))
        o_ref[rows, :] = jnp.where(first, outs[0], outs[1])


def _na_bias_pairs(table):
    t = table.astype(F32)
    nc = 2 * NA_WIN_C - 1
    z = jnp.zeros(t[:, :-1].shape[:2] + (GRID_W - nc,), F32)
    return jnp.concatenate([t[:, :-1], z, t[:, 1:], z], axis=-1)[:, :, None, :]


def _pair_heads(cache):
    c = cache.reshape(DEC_BATCH, NA_HEADS // 2, 2, PAST_LEN, NA_HD)
    return c.transpose(0, 1, 3, 2, 4).reshape(DEC_BATCH, NA_HEADS // 2, PAST_LEN, LANES)


def _na_attention(proj, bias_table, cache_k, cache_v):
    cache_out = pl.BlockSpec((1, 1, NA_HEADS, SEQ, NA_HD), lambda b: (b, 0, 0, 0, 0))
    cache_shape = jax.ShapeDtypeStruct((BATCH, 1, NA_HEADS, SEQ, NA_HD), F32)
    o_p, new_k, new_v = pl.pallas_call(
        _na_prompt_kernel,
        grid=(BATCH,),
        in_specs=[
            pl.BlockSpec((SEQ, D_MODEL), lambda b: (b, 0)),
            pl.BlockSpec((SEQ, D_MODEL), lambda b: (b, 1)),
            pl.BlockSpec((SEQ, D_MODEL), lambda b: (b, 2)),
        ],
        out_specs=[pl.BlockSpec((SEQ, D_MODEL), lambda b: (b, 0)), cache_out, cache_out],
        out_shape=[jax.ShapeDtypeStruct((N_TOK, D_MODEL), F32), cache_shape, cache_shape],
        compiler_params=_params("arbitrary"),
        name="na_prompt",
    )(proj, proj, proj)
    rb = N_PROMPT_TOK // DEC_SEQ
    npair = NA_HEADS // 2
    cache = pl.BlockSpec((1, 1, PAST_LEN, LANES), lambda pr, b: (b, pr, 0, 0))
    o = pl.pallas_call(
        _na_latent_kernel,
        grid=(npair, DEC_BATCH),
        in_specs=[
            pl.BlockSpec((DEC_SEQ, LANES), lambda pr, b: (rb + b, pr)),
            pl.BlockSpec((DEC_SEQ, LANES), lambda pr, b: (rb + b, npair + pr)),
            pl.BlockSpec((DEC_SEQ, LANES), lambda pr, b: (rb + b, 2 * npair + pr)),
            cache, cache,
            pl.BlockSpec((2, 2 * NA_WIN_R - 2, 1, LANES), lambda pr, b: (pr, 0, 0, 0)),
            pl.BlockSpec(memory_space=pl.ANY),
        ],
        out_specs=pl.BlockSpec((DEC_SEQ, LANES), lambda pr, b: (rb + b, pr)),
        out_shape=jax.ShapeDtypeStruct((N_TOK, D_MODEL), F32),
        input_output_aliases={6: 0},
        compiler_params=_params("arbitrary", "arbitrary"),
        name="na_latent",
    )(proj, proj, proj, _pair_heads(cache_k), _pair_heads(cache_v), _na_bias_pairs(bias_table), o_p)
    return o, new_k, new_v


def _layer_na(x, p, mod, j, final):
    i = N_MIXERS * j + 3
    proj = _in_proj(x, p['norm_w'][i], mod, p['na_w_in'][j], 1024)
    o, new_k, new_v = _na_attention(proj, p['na_bias'][j], p['cache_na_k'][:, j], p['cache_na_v'][:, j])
    args = (o, proj, 3, p['na_w_out'][j], x, mod, p['final_norm_w'])
    if final:
        x = (_out_proj(*args, True, rows=(0, N_PROMPT_TOK)), _out_proj(*args, True, rows=(N_PROMPT_TOK, N_TOK)))
    else:
        x = _out_proj(*args, False)
    return x, new_k, new_v


def kernel(x_prompt, x_sample, state_ret, state_rwkv, cache_diff_k, cache_diff_v, cache_na_k, cache_na_v,
           c, c_ctx, norm_w, w_mod, b_mod, final_norm_w,
           ret_w_in, ret_decay, ret_gn, ret_w_out,
           rwkv_mu, rwkv_w_in, rwkv_w0, rwkv_wA, rwkv_wB, rwkv_a0, rwkv_aA, rwkv_aB,
           rwkv_kk, rwkv_ka, rwkv_rk, rwkv_gn, rwkv_w_out,
           diff_w_in, diff_lambda, diff_gn, diff_w_out,
           na_w_in, na_bias, na_w_out):
    p = dict(locals())
    cond = jnp.zeros((N_COND, D_MODEL), F32).at[0].set(c_ctx).at[1:1 + DEC_BATCH].set(c)
    mods = _modulation(cond, w_mod, b_mod)
    x = jnp.concatenate([x_prompt.reshape(N_PROMPT_TOK, D_MODEL), x_sample.reshape(N_SAMPLE_TOK, D_MODEL)])
    new = {n: [] for n in ('ret', 'rwkv', 'dk', 'dv', 'nk', 'nv')}
    for i in range(DEPTH):
        kind, j = i % N_MIXERS, i // N_MIXERS
        if kind == 0:
            x, st = _layer_ret(x, p, mods[i], j)
            new['ret'].append(st)
        elif kind == 1:
            x, st = _layer_rwkv(x, p, mods[i], j)
            new['rwkv'].append(st)
        elif kind == 2:
            x, ck, cv = _layer_diff(x, p, mods[i], j, i)
            new['dk'].append(ck)
            new['dv'].append(cv)
        else:
            x, ck, cv = _layer_na(x, p, mods[i], j, final=(i == DEPTH - 1))
            new['nk'].append(ck)
            new['nv'].append(cv)
    if DEPTH % N_MIXERS:
        raise NotImplementedError("the final norm is fused into the last neighbourhood-attention layer")
    cat = lambda xs: xs[0] if len(xs) == 1 else jnp.concatenate(xs, axis=1)
    return (x[0].reshape(BATCH, SEQ, D_MODEL), x[1].reshape(DEC_BATCH, DEC_SEQ, D_MODEL),
            cat(new['ret']), cat(new['rwkv']), cat(new['dk']), cat(new['dv']), cat(new['nk']), cat(new['nv']))
```

```python
import functools
import math

import jax
import jax.numpy as jnp
from jax import lax
from jax.experimental import pallas as pl
from jax.experimental.pallas import tpu as pltpu

F32 = jnp.float32
BF16 = jnp.bfloat16

D_MODEL = 1024
BATCH = 32
SEQ = 256
DEPTH = 4
N_MIXERS = 4
DEC_BATCH = 2
DEC_SEQ = 1024
PAST_LEN = 256
GRID_W = 64

RET_HEADS = 4
RET_DK = 256
RET_DV = 512
RET_QK = 1024
RET_V = 2048

RWKV_HD = 64
RWKV_HEADS = 16
RWKV_RANK = 64

DIFF_HEADS = 8
DIFF_HD = 64

NA_HEADS = 16
NA_HD = 64
NA_WIN_R = 8
NA_WIN_C = 16

ROPE_BASE = 10000.0
EPS = 1e-6
GN_EPS = 1e-5

N_PROMPT_TOK = BATCH * SEQ
N_SAMPLE_TOK = DEC_BATCH * DEC_SEQ
N_TOK = N_PROMPT_TOK + N_SAMPLE_TOK
N_COND = 8

LANES = 128
VMEM_LIMIT = 56 * 2 ** 20


def _params(*sem):
    return pltpu.CompilerParams(dimension_semantics=sem, vmem_limit_bytes=VMEM_LIMIT)


def _cond_of_tile(i, tm):
    npt = N_PROMPT_TOK // tm
    return jnp.where(i < npt, 0, 1 + (i - npt) // (DEC_SEQ // tm))


def _sigmoid(x):
    return 1.0 / (1.0 + jnp.exp(-x))


def _silu(x):
    return x * _sigmoid(x)


def _dot(a, b):
    return jnp.dot(a, b, preferred_element_type=F32)


def _dot_nt(a, b):
    return lax.dot_general(a, b, (((1,), (1,)), ((), ())), preferred_element_type=F32)


def _dot_tn(a, b):
    return lax.dot_general(a, b, (((0,), (0,)), ((), ())), preferred_element_type=F32)


def _softmax_rows(s):
    m = jnp.max(s, axis=-1, keepdims=True)
    e = jnp.exp(s - m)
    return e / jnp.sum(e, axis=-1, keepdims=True)


def _mod_kernel(c_ref, w_ref, b_ref, o_ref):
    s = _silu(c_ref[...])
    o_ref[0] = jnp.dot(s, w_ref[0], precision=lax.Precision.HIGHEST, preferred_element_type=F32) + b_ref[0]


def _modulation(cond, w_mod, b_mod):
    tn = D_MODEL
    out = pl.pallas_call(
        _mod_kernel,
        grid=(DEPTH, 3 * D_MODEL // tn),
        in_specs=[
            pl.BlockSpec((N_COND, D_MODEL), lambda l, j: (0, 0)),
            pl.BlockSpec((1, D_MODEL, tn), lambda l, j: (l, 0, j)),
            pl.BlockSpec((1, 1, tn), lambda l, j: (l, 0, j)),
        ],
        out_specs=pl.BlockSpec((1, N_COND, tn), lambda l, j: (l, 0, j)),
        out_shape=jax.ShapeDtypeStruct((DEPTH, N_COND, 3 * D_MODEL), F32),
        compiler_params=_params("arbitrary", "arbitrary"),
        name="modulation",
    )(cond, w_mod, b_mod.reshape(DEPTH, 1, 3 * D_MODEL))
    return out.reshape(DEPTH, N_COND, 3, 1, D_MODEL)


def _norm_mod(x, nw, mod_ref):
    ms = jnp.mean(x * x, axis=-1, keepdims=True)
    y = x * lax.rsqrt(ms + EPS) * nw
    return y * (1.0 + mod_ref[0, 1]) + mod_ref[0, 0]


IN_TM = 1024


def _in_proj_kernel(x_ref, nw_ref, mod_ref, w_ref, o_ref, h_ref):
    @pl.when(pl.program_id(1) == 0)
    def _():
        h_ref[...] = _norm_mod(x_ref[...], nw_ref[...], mod_ref).astype(BF16)

    o_ref[...] = _dot(h_ref[...], w_ref[...])


def _in_proj(x, norm_w, mod, w, tn):
    n = w.shape[1]
    w = w.astype(BF16)
    return pl.pallas_call(
        _in_proj_kernel,
        grid=(N_TOK // IN_TM, n // tn),
        in_specs=[
            pl.BlockSpec((IN_TM, D_MODEL), lambda i, j: (i, 0)),
            pl.BlockSpec((1, D_MODEL), lambda i, j: (0, 0)),
            pl.BlockSpec((1, 3, 1, D_MODEL), lambda i, j: (_cond_of_tile(i, IN_TM), 0, 0, 0)),
            pl.BlockSpec((D_MODEL, tn), lambda i, j: (0, j)),
        ],
        out_specs=pl.BlockSpec((IN_TM, tn), lambda i, j: (i, j)),
        out_shape=jax.ShapeDtypeStruct((N_TOK, n), F32),
        scratch_shapes=[pltpu.VMEM((IN_TM, D_MODEL), BF16)],
        compiler_params=_params("arbitrary", "arbitrary"),
        name="in_proj",
    )(x, norm_w.reshape(1, D_MODEL), mod, w)


OUT_TM = 256


def _out_proj_kernel(o_ref, g_ref, w_ref, x_ref, mod_ref, fw_ref, y_ref, wb_ref, *, final):
    @pl.when(pl.program_id(0) == 0)
    def _():
        wb_ref[...] = w_ref[...].astype(BF16)

    a = (o_ref[...] * _silu(g_ref[...])).astype(BF16)
    xn = x_ref[...] + mod_ref[0, 2] * _dot(a, wb_ref[...])
    if final:
        ms = jnp.mean(xn * xn, axis=-1, keepdims=True)
        xn = xn * lax.rsqrt(ms + EPS) * fw_ref[...]
    y_ref[...] = xn


def _out_proj(o, g_arr, g_blk, w, x, mod, final_w, final, rows=(0, N_TOK)):
    k = w.shape[0]
    t0 = rows[0] // OUT_TM
    return pl.pallas_call(
        functools.partial(_out_proj_kernel, final=final),
        grid=((rows[1] - rows[0]) // OUT_TM,),
        in_specs=[
            pl.BlockSpec((OUT_TM, k), lambda i: (t0 + i, 0)),
            pl.BlockSpec((OUT_TM, k), lambda i: (t0 + i, g_blk)),
            pl.BlockSpec((k, D_MODEL), lambda i: (0, 0)),
            pl.BlockSpec((OUT_TM, D_MODEL), lambda i: (t0 + i, 0)),
            pl.BlockSpec((1, 3, 1, D_MODEL), lambda i: (_cond_of_tile(t0 + i, OUT_TM), 0, 0, 0)),
            pl.BlockSpec((1, D_MODEL), lambda i: (0, 0)),
        ],
        out_specs=pl.BlockSpec((OUT_TM, D_MODEL), lambda i: (i, 0)),
        out_shape=jax.ShapeDtypeStruct((rows[1] - rows[0], D_MODEL), F32),
        scratch_shapes=[pltpu.VMEM((k, D_MODEL), BF16)],
        compiler_params=_params("arbitrary"),
        name="out_proj",
    )(o, g_arr, w, x, mod, final_w.reshape(1, D_MODEL))


def _rope_tables(d):
    q = d // 4
    t = jnp.arange(DEC_SEQ)
    row = (t // GRID_W).astype(F32)
    col = (t % GRID_W).astype(F32)
    inv = ROPE_BASE ** (-jnp.arange(0, 2 * q, 2, dtype=F32) / (2 * q))
    ar = row[:, None] * inv[None, :]
    ac = col[:, None] * inv[None, :]
    z = jnp.zeros_like(ar)
    cos = jnp.concatenate([jnp.cos(ar), jnp.cos(ar), jnp.cos(ac), jnp.cos(ac)], axis=-1)
    sin_lo = jnp.concatenate([-jnp.sin(ar), z, -jnp.sin(ac), z], axis=-1)
    sin_hi = jnp.concatenate([z, jnp.sin(ar), z, jnp.sin(ac)], axis=-1)
    return cos, sin_lo, sin_hi


def _rope(x, cos, sin_lo, sin_hi, q):
    w = x.shape[-1]
    x_next = pltpu.roll(x, w - q, axis=1)
    x_prev = pltpu.roll(x, q, axis=1)
    return x * cos + x_next * sin_lo + x_prev * sin_hi


RET_QB = 256


def _ret_kernel(lg_ref, q_ref, k_ref, v_ref, gn_ref, *rest, seq, latent):
    if latent:
        cos_ref, slo_ref, shi_ref, s0_ref, _prev_ref, o_ref = rest
    else:
        o_ref, st_ref = rest
    h = pl.program_id(1)
    lgf = lg_ref[0, h]
    lgb = lg_ref[1, h]
    q = q_ref[...]
    k = k_ref[...]
    if latent:
        q = _rope(q, cos_ref[...], slo_ref[...], shi_ref[...], RET_DK // 4)
        k = _rope(k, cos_ref[...], slo_ref[...], shi_ref[...], RET_DK // 4)
    k = k * (RET_DK ** -0.5)
    kb = k.astype(BF16)
    vb = v_ref[...].astype(BF16)
    gn = gn_ref[...]
    for qi in range(seq // RET_QB):
        qblk = q[qi * RET_QB:(qi + 1) * RET_QB]
        s = _dot_nt(qblk.astype(BF16), kb)
        ii = lax.broadcasted_iota(jnp.int32, (RET_QB, seq), 0) + qi * RET_QB
        jj = lax.broadcasted_iota(jnp.int32, (RET_QB, seq), 1)
        gap = (ii - jj).astype(F32)
        dec = (jnp.where(gap >= 0, jnp.exp(lgf * jnp.maximum(gap, 0.0)), 0.0)
               + jnp.where(gap <= 0, jnp.exp(lgb * jnp.maximum(-gap, 0.0)), 0.0))
        o = _dot((s * dec).astype(BF16), vb)
        if latent:
            pos = (lax.broadcasted_iota(jnp.int32, (RET_QB, 1), 0) + qi * RET_QB).astype(F32)
            qf = qblk * jnp.exp(lgf * (pos + 1.0))
            qr = qblk * jnp.exp(lgb * (seq - pos))
            o = o + _dot(qf.astype(BF16), s0_ref[0, 0, 0, 0].astype(BF16))
            o = o + _dot(qr.astype(BF16), s0_ref[0, 0, 1, 0].astype(BF16))
        oc = o - jnp.mean(o, axis=-1, keepdims=True)
        o = oc * lax.rsqrt(jnp.mean(oc * oc, axis=-1, keepdims=True) + GN_EPS) * gn
        o_ref[qi * RET_QB:(qi + 1) * RET_QB, :] = o
    if not latent:
        pos = lax.broadcasted_iota(jnp.int32, (seq, 1), 0).astype(F32)
        kf = k * jnp.exp(lgf * (seq - 1.0 - pos))
        kr = k * jnp.exp(lgb * pos)
        st_ref[0, 0, 0, 0] = _dot_tn(kf.astype(BF16), vb)
        st_ref[0, 0, 1, 0] = _dot_tn(kr.astype(BF16), vb)


def _retention(p, log_g, gn_w, state_ret, j):
    smem = pl.BlockSpec(memory_space=pltpu.SMEM)
    gn = gn_w.reshape(1, RET_V)
    kq = RET_QK // RET_DK
    o_p, st = pl.pallas_call(
        functools.partial(_ret_kernel, seq=SEQ, latent=False),
        grid=(BATCH, RET_HEADS),
        in_specs=[
            smem,
            pl.BlockSpec((SEQ, RET_DK), lambda b, h: (b, h)),
            pl.BlockSpec((SEQ, RET_DK), lambda b, h: (b, kq + h)),
            pl.BlockSpec((SEQ, RET_DV), lambda b, h: (b, kq + h)),
            pl.BlockSpec((1, RET_DV), lambda b, h: (0, h)),
        ],
        out_specs=[
            pl.BlockSpec((SEQ, RET_DV), lambda b, h: (b, h)),
            pl.BlockSpec((1, 1, 2, 1, RET_DK, RET_DV), lambda b, h: (b, 0, 0, h, 0, 0)),
        ],
        out_shape=[
            jax.ShapeDtypeStruct((N_TOK, RET_V), F32),
            jax.ShapeDtypeStruct((BATCH, 1, 2, RET_HEADS, RET_DK, RET_DV), F32),
        ],
        compiler_params=_params("arbitrary", "arbitrary"),
        name="retention_prompt",
    )(log_g, p, p, p, gn)
    cos, slo, shi = _rope_tables(RET_DK)
    rb = N_PROMPT_TOK // DEC_SEQ
    full = pl.BlockSpec((DEC_SEQ, RET_DK), lambda b, h: (0, 0))
    o = pl.pallas_call(
        functools.partial(_ret_kernel, seq=DEC_SEQ, latent=True),
        grid=(DEC_BATCH, RET_HEADS),
        in_specs=[
            smem,
            pl.BlockSpec((DEC_SEQ, RET_DK), lambda b, h: (rb + b, h)),
            pl.BlockSpec((DEC_SEQ, RET_DK), lambda b, h: (rb + b, kq + h)),
            pl.BlockSpec((DEC_SEQ, RET_DV), lambda b, h: (rb + b, kq + h)),
            pl.BlockSpec((1, RET_DV), lambda b, h: (0, h)),
            full, full, full,
            pl.BlockSpec((1, 1, 2, 1, RET_DK, RET_DV), lambda b, h: (b, j, 0, h, 0, 0)),
            pl.BlockSpec(memory_space=pl.ANY),
        ],
        out_specs=pl.BlockSpec((DEC_SEQ, RET_DV), lambda b, h: (rb + b, h)),
        out_shape=jax.ShapeDtypeStruct((N_TOK, RET_V), F32),
        input_output_aliases={9: 0},
        compiler_params=_params("arbitrary", "arbitrary"),
        name="retention_latent",
    )(log_g, p, p, p, gn, cos, slo, shi, state_ret, o_p)
    return o, st


def _layer_ret(x, p, mod, j):
    i = N_MIXERS * j + 0
    proj = _in_proj(x, p['norm_w'][i], mod, p['ret_w_in'][j], 1024)
    log_g = jax.nn.log_sigmoid(p['ret_decay'][j].astype(F32))
    o, st = _retention(proj, log_g, p['ret_gn'][j], p['state_ret'], j)
    x = _out_proj(o, proj, (2 * RET_QK + RET_V) // RET_V, p['ret_w_out'][j], x, mod, p['final_norm_w'], False)
    return x, st


RW_TM = 512
RW_HALO = 8
RW_C = 64
RW_LOCK = 4
RW_PAIRS = 2


def _rwkv_prep_kernel(x_ref, xp_ref, xn_ref, nw_ref, mod_ref, mu_ref, wa_ref, aa_ref, wb_ref, ab_ref,
                      w0_ref, a0_ref, xm_ref, lw_ref, a_ref):
    i = pl.program_id(0)
    nw = nw_ref[...]
    h = _norm_mod(x_ref[...], nw, mod_ref)
    h_before = _norm_mod(xp_ref[RW_HALO - 1:RW_HALO, :], nw, mod_ref)
    h_after = _norm_mod(xn_ref[0:1, :], nw, mod_ref)
    seq = jnp.where(i < N_PROMPT_TOK // RW_TM, SEQ, DEC_SEQ)
    row = lax.broadcasted_iota(jnp.int32, (RW_TM, 1), 0)
    t = (row + i * RW_TM) & (seq - 1)
    prev = jnp.where(row == 0, h_before, pltpu.roll(h, 1, axis=0))
    nxt = jnp.where(row == RW_TM - 1, h_after, pltpu.roll(h, RW_TM - 1, axis=0))
    prev = jnp.where(t == 0, 0.0, prev)
    nxt = jnp.where(t == seq - 1, 0.0, nxt)
    xx = 0.5 * (prev + nxt) - h
    for n, m in enumerate((0, 2, 3, 5)):
        xm_ref[n] = (h + xx * mu_ref[m:m + 1, :]).astype(BF16)
    xw = (h + xx * mu_ref[1:2, :]).astype(BF16)
    xa = (h + xx * mu_ref[4:5, :]).astype(BF16)
    lw = jnp.tanh(_dot(xw, wa_ref[...])).astype(BF16)
    la = _dot(xa, aa_ref[...]).astype(BF16)
    for dr in range(2):
        wl = w0_ref[dr:dr + 1, :] + _dot(lw, wb_ref[dr])
        lw_ref[dr] = -math.exp(-0.5) * _sigmoid(wl)
        a_ref[dr] = _sigmoid(a0_ref[dr:dr + 1, :] + _dot(la, ab_ref[dr]))


def _rwkv_prep(x, norm_w, mod, mu, wa2, aa2, wb_pad, ab_pad, w0, a0):
    nt = N_TOK // RW_TM
    hb = RW_TM // RW_HALO
    last = N_TOK // RW_HALO - 1
    full2 = lambda shape: pl.BlockSpec(shape, lambda i: (0, 0))
    full3 = lambda shape: pl.BlockSpec(shape, lambda i: (0, 0, 0))
    return pl.pallas_call(
        _rwkv_prep_kernel,
        grid=(nt,),
        in_specs=[
            pl.BlockSpec((RW_TM, D_MODEL), lambda i: (i, 0)),
            pl.BlockSpec((RW_HALO, D_MODEL), lambda i: (jnp.maximum(i * hb - 1, 0), 0)),
            pl.BlockSpec((RW_HALO, D_MODEL), lambda i: (jnp.minimum((i + 1) * hb, last), 0)),
            full2((1, D_MODEL)),
            pl.BlockSpec((1, 3, 1, D_MODEL), lambda i: (_cond_of_tile(i, RW_TM), 0, 0, 0)),
            full2((6, D_MODEL)),
            full2((D_MODEL, 2 * RWKV_RANK)),
            full2((D_MODEL, 2 * RWKV_RANK)),
            full3((2, 2 * RWKV_RANK, D_MODEL)),
            full3((2, 2 * RWKV_RANK, D_MODEL)),
            full2((2, D_MODEL)),
            full2((2, D_MODEL)),
        ],
        out_specs=[
            pl.BlockSpec((4, RW_TM, D_MODEL), lambda i: (0, i, 0)),
            pl.BlockSpec((2, RW_TM, D_MODEL), lambda i: (0, i, 0)),
            pl.BlockSpec((2, RW_TM, D_MODEL), lambda i: (0, i, 0)),
        ],
        out_shape=[
            jax.ShapeDtypeStruct((4, N_TOK, D_MODEL), BF16),
            jax.ShapeDtypeStruct((2, N_TOK, D_MODEL), F32),
            jax.ShapeDtypeStruct((2, N_TOK, D_MODEL), F32),
        ],
        compiler_params=_params("arbitrary"),
        name="rwkv_prep",
    )(x, x, x, norm_w.reshape(1, D_MODEL), mod, mu, wa2, aa2, wb_pad, ab_pad, w0, a0)


def _bmm_kernel(a_ref, w_ref, o_ref):
    o_ref[0] = _dot(a_ref[0], w_ref[...])


def _rwkv_rkvg(xm, w):
    tm = 1024
    return pl.pallas_call(
        _bmm_kernel,
        grid=(4, N_TOK // tm),
        in_specs=[
            pl.BlockSpec((1, tm, D_MODEL), lambda n, i: (n, i, 0)),
            pl.BlockSpec((D_MODEL, D_MODEL), lambda n, i: (0, n)),
        ],
        out_specs=pl.BlockSpec((1, tm, D_MODEL), lambda n, i: (n, i, 0)),
        out_shape=jax.ShapeDtypeStruct((4, N_TOK, D_MODEL), F32),
        compiler_params=_params("arbitrary", "arbitrary"),
        name="rwkv_rkvg",
    )(xm, w)


def _head_sum(x, first):
    s0 = jnp.sum(jnp.where(first, x, 0.0), axis=-1, keepdims=True)
    s1 = jnp.sum(jnp.where(first, 0.0, x), axis=-1, keepdims=True)
    return jnp.where(first, s0, s1)


def _stack_heads(x, first):
    return jnp.concatenate([jnp.where(first, x, 0.0), jnp.where(first, 0.0, x)], axis=0)


def _cumsum_rows(tri, x):
    hi = x.astype(BF16)
    r1 = x - hi.astype(F32)
    mid = r1.astype(BF16)
    lo = (r1 - mid.astype(F32)).astype(BF16)
    return _dot(tri, hi) + _dot(tri, mid) + _dot(tri, lo)


def _rwkv_chunk_kernel(*refs, seq, zero_init):
    if zero_init:
        (rkv_ref, lw_ref, a_ref, kkp_ref, kap_ref, rkp_ref, gn_ref, o_ref, st_ref,
         kk_scr, y_scr, tar_scr, lrb_scr, b2_scr, w2_scr, yl_scr, kv_scr, pc_scr) = refs
    else:
        (rkv_ref, lw_ref, a_ref, kkp_ref, kap_ref, rkp_ref, gn_ref, s0_ref, _prev_ref, o_ref,
         kk_scr, y_scr, tar_scr, lrb_scr, b2_scr, w2_scr, yl_scr, kv_scr, pc_scr) = refs
    c_len = RW_C
    n_ch = seq // c_len
    rows2 = 2 * c_len
    first = _first_half_lanes()
    kap = kap_ref[...]

    kk = rkv_ref[1] * kkp_ref[...]
    kk_scr[...] = kk * lax.rsqrt(jnp.maximum(_head_sum(kk * kk, first), 1e-12))

    rr = lax.broadcasted_iota(jnp.int32, (rows2, rows2), 0)
    cc = lax.broadcasted_iota(jnp.int32, (rows2, rows2), 1)
    eye = (rr == cc).astype(F32)
    tr = lax.broadcasted_iota(jnp.int32, (c_len, c_len), 0)
    tc = lax.broadcasted_iota(jnp.int32, (c_len, c_len), 1)

    def same(shift):
        return (rr >> shift) == (cc >> shift)

    head = same(6)
    strict = (head & (cc < rr), head & (cc > rr))
    incl = (head & (cc <= rr), head & (cc >= rr))
    tri = ((tc <= tr).astype(BF16), (tc >= tr).astype(BF16))
    last = (c_len - 1, 0)

    def phase1(chains):
        dirs = [dr for dr, _ in chains]
        rows = [pl.ds(pl.multiple_of(c * c_len, c_len), c_len) for _, c in chains]
        lw = [lw_ref[dr, rw, :] for dr, rw in zip(dirs, rows)]
        cum = [_cumsum_rows(tri[dr], x) for dr, x in zip(dirs, lw)]
        a2, r2, b2, k2, v2, pc = [], [], [], [], [], []
        for dr, rw, lw_c, cum_c in zip(dirs, rows, lw, cum):
            a = a_ref[dr, rw, :]
            k = rkv_ref[1, rw, :]
            kk_c = kk_scr[rw, :]
            e_inc = jnp.exp(cum_c)
            e_inv = jnp.exp(-cum_c)
            a2.append(_stack_heads(-kk_c * jnp.exp(cum_c - lw_c), first).astype(BF16))
            r2.append(_stack_heads(rkv_ref[0, rw, :] * e_inc, first).astype(BF16))
            b2.append(_stack_heads(kk_c * a * e_inv, first).astype(BF16))
            k2.append(_stack_heads(k * (1.0 + (a - 1.0) * kap) * e_inv, first).astype(BF16))
            v2.append(_stack_heads(rkv_ref[2, rw, :], first).astype(BF16))
            pc.append(e_inc[last[dr]:last[dr] + 1, :])
        g = [_dot_nt(jnp.concatenate([x, y], axis=0), jnp.concatenate([z, w], axis=0))
             for x, y, z, w in zip(a2, r2, b2, k2)]
        l_ab = [jnp.where(strict[dr], x[:rows2, :rows2], 0.0) for dr, x in zip(dirs, g)]
        t = [eye + jnp.where(same(1), x, 0.0) for x in l_ab]
        side = {}
        for shift in range(1, 6):
            sib = same(shift + 1) & ~same(shift)
            tb = [x.astype(BF16) for x in t]
            mid = [_dot(jnp.where(sib, x, 0.0).astype(BF16), y) for x, y in zip(l_ab, tb)]
            if shift == 1:
                side['lv'] = [_dot(jnp.where(strict[dr], x[:rows2, rows2:], 0.0).astype(BF16), y)
                              for dr, x, y in zip(dirs, g, v2)]
            elif shift == 2:
                side['yl'] = [_dot(jnp.where(incl[dr], x[rows2:, rows2:], 0.0).astype(BF16), y)
                              for dr, x, y in zip(dirs, g, v2)]
            elif shift == 3:
                side['kv'] = [_dot_tn(x, y) for x, y in zip(v2, k2)]
            t = [x + _dot(y, z.astype(BF16)) for x, y, z in zip(t, tb, mid)]
        tb = [x.astype(BF16) for x in t]
        ta = [_dot(x, y) for x, y in zip(tb, a2)]
        w2 = [_dot(x, y.astype(BF16)) for x, y in zip(tb, side['lv'])]
        for i, (dr, c) in enumerate(chains):
            n = dr * n_ch + c
            tar_scr[n, :rows2, :] = ta[i].astype(BF16)
            tar_scr[n, rows2:, :] = r2[i]
            w2_scr[n] = w2[i]
            yl_scr[n] = side['yl'][i]
            kv_scr[n] = side['kv'][i]
            lrb_scr[n] = jnp.where(incl[dr], g[i][rows2:, :rows2], 0.0).astype(BF16)
            b2_scr[n] = b2[i]
            pc_scr[n] = pc[i]

    def body1(grp, carry):
        phase1([(dr, grp * RW_LOCK + j) for j in range(RW_LOCK) for dr in range(2)])
        return carry

    if n_ch == RW_LOCK:
        body1(0, 0)
    else:
        lax.fori_loop(0, n_ch // RW_LOCK, body1, 0)

    def body2(i, carry):
        cs = (i, n_ch - 1 - i)
        ns = [dr * n_ch + c for dr, c in enumerate(cs)]
        x = [_dot_nt(tar_scr[n], s2.astype(BF16)) for n, s2 in zip(ns, carry)]
        u2 = [(xx[:rows2] + w2_scr[n]).astype(BF16) for n, xx in zip(ns, x)]
        upd = [_dot_tn(u, b2_scr[n]) for n, u in zip(ns, u2)]
        yb = [_dot(lrb_scr[n], u) for n, u in zip(ns, u2)]
        out = []
        for dr, (c, n) in enumerate(zip(cs, ns)):
            y2 = x[dr][rows2:] + yb[dr] + yl_scr[n]
            y_scr[dr, pl.ds(pl.multiple_of(c * c_len, c_len), c_len), :] = y2[:c_len] + y2[c_len:]
            out.append((carry[dr] + upd[dr] + kv_scr[n]) * pc_scr[n])
        return tuple(out)

    if zero_init:
        init = (jnp.zeros((rows2, LANES), F32),) * 2
    else:
        init = (s0_ref[0, 0, 0], s0_ref[0, 1, 0])
    s_f, s_b = lax.fori_loop(0, n_ch, body2, init)

    y = y_scr[0] + y_scr[1]
    yc = y - _head_sum(y, first) * (1.0 / RWKV_HD)
    o = yc * lax.rsqrt(_head_sum(yc * yc, first) * (1.0 / RWKV_HD) + GN_EPS) * gn_ref[...]
    r = rkv_ref[0]
    k = rkv_ref[1]
    v = rkv_ref[2]
    for dr in range(2):
        kd = k * (1.0 + (a_ref[dr] - 1.0) * kap)
        o = o + _head_sum(r * kd * rkp_ref[...], first) * v
    o_ref[...] = o

    if zero_init:
        for dr, s2 in enumerate((s_f, s_b)):
            st_ref[0, 0, dr, 0] = s2[:RWKV_HD, :RWKV_HD]
            st_ref[0, 0, dr, 1] = s2[RWKV_HD:, RWKV_HD:]


def _rwkv_chunked(rkvg, lw, a, kkp, kap, rkp, gn, s0_pairs):
    npair = RWKV_HEADS // 2
    par = lambda *_: pl.BlockSpec((1, LANES), lambda s, p: (0, p))

    def scratch(seq):
        n = 2 * (seq // RW_C)
        r2 = 2 * RW_C
        return [
            pltpu.VMEM((seq, LANES), F32), pltpu.VMEM((2, seq, LANES), F32),
            pltpu.VMEM((n, 2 * r2, LANES), BF16), pltpu.VMEM((n, r2, r2), BF16), pltpu.VMEM((n, r2, LANES), BF16),
            pltpu.VMEM((n, r2, LANES), F32), pltpu.VMEM((n, r2, LANES), F32), pltpu.VMEM((n, r2, LANES), F32),
            pltpu.VMEM((n, 1, LANES), F32),
        ]

    def seq_specs(seq, rb):
        return [
            pl.BlockSpec((3, seq, LANES), lambda s, p: (0, rb + s, p)),
            pl.BlockSpec((2, seq, LANES), lambda s, p: (0, rb + s, p)),
            pl.BlockSpec((2, seq, LANES), lambda s, p: (0, rb + s, p)),
            par(), par(), par(), par(),
        ]

    pars = [u.reshape(1, D_MODEL) for u in (kkp, kap, rkp, gn)]
    o_p, st = pl.pallas_call(
        functools.partial(_rwkv_chunk_kernel, seq=SEQ, zero_init=True),
        grid=(BATCH, npair),
        in_specs=seq_specs(SEQ, 0),
        out_specs=[
            pl.BlockSpec((SEQ, LANES), lambda s, p: (s, p)),
            pl.BlockSpec((1, 1, 2, 2, RWKV_HD, RWKV_HD), lambda s, p: (s, 0, 0, p, 0, 0)),
        ],
        out_shape=[
            jax.ShapeDtypeStruct((N_TOK, D_MODEL), F32),
            jax.ShapeDtypeStruct((BATCH, 1, 2, RWKV_HEADS, RWKV_HD, RWKV_HD), F32),
        ],
        scratch_shapes=scratch(SEQ),
        compiler_params=_params("arbitrary", "arbitrary"),
        name="rwkv_prompt",
    )(rkvg, lw, a, *pars)
    rb = N_PROMPT_TOK // DEC_SEQ
    o = pl.pallas_call(
        functools.partial(_rwkv_chunk_kernel, seq=DEC_SEQ, zero_init=False),
        grid=(DEC_BATCH, npair),
        in_specs=seq_specs(DEC_SEQ, rb) + [
            pl.BlockSpec((1, 2, 1, LANES, LANES), lambda s, p: (s, 0, p, 0, 0)),
            pl.BlockSpec(memory_space=pl.ANY),
        ],
        out_specs=pl.BlockSpec((DEC_SEQ, LANES), lambda s, p: (rb + s, p)),
        out_shape=jax.ShapeDtypeStruct((N_TOK, D_MODEL), F32),
        input_output_aliases={8: 0},
        scratch_shapes=scratch(DEC_SEQ),
        compiler_params=_params("arbitrary", "arbitrary"),
        name="rwkv_latent",
    )(rkvg, lw, a, *pars, s0_pairs, o_p)
    return o, st


def _split3(x):
    hi = x.astype(BF16)
    r1 = x - hi.astype(F32)
    mid = r1.astype(BF16)
    return hi, mid, (r1 - mid.astype(F32)).astype(BF16)


def _rwkv_kernel(*refs, seq, zero_init, pg, np2):
    n_in = 7 if zero_init else 9
    rkv_ref, lw_ref, a_ref, kkp_ref, kap_ref, rkp_ref, gn_ref = refs[:7]
    if zero_init:
        o_ref, st_ref = refs[n_in:n_in + 2]
        scr = refs[n_in + 2:]
    else:
        s0_ref = refs[7]
        o_ref = refs[n_in]
        scr = refs[n_in + 1:]
    kk_scr, cum_scr, bon_scr, y_scr, s_scr, tar_scr, lrb_scr, b2_scr, w2_scr, yl_scr, kv_scr, pc_scr = scr
    c_len = RW_C
    n_ch = seq // c_len
    rows2 = 2 * c_len
    grp = pl.program_id(1)
    defer = np2 > pg
    base = grp * pg if defer else 0
    first = _first_half_lanes()

    rr = lax.broadcasted_iota(jnp.int32, (rows2, rows2), 0)
    cc = lax.broadcasted_iota(jnp.int32, (rows2, rows2), 1)
    eye = (rr == cc).astype(F32)

    def same(shift):
        return (rr >> shift) == (cc >> shift)

    head = same(6)
    strict = (head & (cc < rr), head & (cc > rr))
    incl = (head & (cc <= rr), head & (cc >= rr))
    last = (c_len - 1, 0)
    head_ones = head.astype(BF16)

    cs_rows = min(seq, 256)
    tr = lax.broadcasted_iota(jnp.int32, (cs_rows, cs_rows), 0)
    tc = lax.broadcasted_iota(jnp.int32, (cs_rows, cs_rows), 1)
    chunk = (tr >> 6) == (tc >> 6)
    tri = ((chunk & (tc <= tr)).astype(BF16), (chunk & (tc >= tr)).astype(BF16))
    for p in range(pg):
        ln = slice(p * LANES, (p + 1) * LANES)
        kap = kap_ref[:, ln]
        r = rkv_ref[0, :, ln]
        k = rkv_ref[1, :, ln]
        v = rkv_ref[2, :, ln]
        kk = k * kkp_ref[:, ln]
        kk_scr[p] = kk * lax.rsqrt(jnp.maximum(_head_sum(kk * kk, first), 1e-12))
        bonus = None
        for dr in range(2):
            for r0 in range(0, seq, cs_rows):
                parts = _split3(lw_ref[dr, r0:r0 + cs_rows, ln])
                cum_scr[p, dr, r0:r0 + cs_rows, :] = (_dot(tri[dr], parts[0]) + _dot(tri[dr], parts[1])
                                                      + _dot(tri[dr], parts[2]))
            kd = k * (1.0 + (a_ref[dr, :, ln] - 1.0) * kap)
            term = _head_sum(r * kd * rkp_ref[:, ln], first) * v
            bonus = term if bonus is None else bonus + term
        bon_scr[base + p] = bonus

    def phase1(chains):
        dirs = [dr for _, dr, _ in chains]
        a2, r2, b2, k2, v2, pc = [], [], [], [], [], []
        for p, dr, c in chains:
            ln = slice(p * LANES, (p + 1) * LANES)
            rw = pl.ds(pl.multiple_of(c * c_len, c_len), c_len)
            a = a_ref[dr, rw, ln]
            k = rkv_ref[1, rw, ln]
            kk_c = kk_scr[p, rw, :]
            cum_c = cum_scr[p, dr, rw, :]
            e_inc = jnp.exp(cum_c)
            e_inv = jnp.exp(-cum_c)
            a2.append(_stack_heads(-kk_c * jnp.exp(cum_c - lw_ref[dr, rw, ln]), first).astype(BF16))
            r2.append(_stack_heads(rkv_ref[0, rw, ln] * e_inc, first).astype(BF16))
            b2.append(_stack_heads(kk_c * a * e_inv, first).astype(BF16))
            k2.append(_stack_heads(k * (1.0 + (a - 1.0) * kap_ref[:, ln]) * e_inv, first).astype(BF16))
            v2.append(_stack_heads(rkv_ref[2, rw, ln], first).astype(BF16))
            pc.append(e_inc[last[dr]:last[dr] + 1, :])
        g = [_dot_nt(jnp.concatenate([x, y], axis=0), jnp.concatenate([z, w], axis=0))
             for x, y, z, w in zip(a2, r2, b2, k2)]
        l_ab = [jnp.where(strict[dr], x[:rows2, :rows2], 0.0) for dr, x in zip(dirs, g)]
        t = [eye + jnp.where(same(1), x, 0.0) for x in l_ab]
        side = {}
        for shift in range(1, 6):
            sib = same(shift + 1) & ~same(shift)
            tb = [x.astype(BF16) for x in t]
            mid = [_dot(jnp.where(sib, x, 0.0).astype(BF16), y) for x, y in zip(l_ab, tb)]
            if shift == 1:
                side['lv'] = [_dot(jnp.where(strict[dr], x[:rows2, rows2:], 0.0).astype(BF16), y)
                              for dr, x, y in zip(dirs, g, v2)]
            elif shift == 2:
                side['yl'] = [_dot(jnp.where(incl[dr], x[rows2:, rows2:], 0.0).astype(BF16), y)
                              for dr, x, y in zip(dirs, g, v2)]
            elif shift == 3:
                side['kv'] = [_dot_tn(x, y) for x, y in zip(v2, k2)]
            t = [x + _dot(y, z.astype(BF16)) for x, y, z in zip(t, tb, mid)]
        tb = [x.astype(BF16) for x in t]
        ta = [_dot(x, y) for x, y in zip(tb, a2)]
        w2 = [_dot(x, y.astype(BF16)) for x, y in zip(tb, side['lv'])]
        for i, (p, dr, c) in enumerate(chains):
            n = ((base + p) * 2 + dr) * n_ch + c
            tar_scr[n, :rows2, :] = ta[i].astype(BF16)
            tar_scr[n, rows2:, :] = r2[i]
            w2_scr[n] = w2[i]
            yl_scr[n] = side['yl'][i]
            kv_scr[n] = side['kv'][i]
            lrb_scr[n] = jnp.where(incl[dr], g[i][rows2:, :rows2], 0.0).astype(BF16)
            b2_scr[n] = b2[i]
            pc_scr[n] = pc[i]

    def body1(cg, carry):
        phase1([(p, dr, cg * RW_LOCK + j) for p in range(pg) for j in range(RW_LOCK) for dr in range(2)])
        return carry

    if n_ch == RW_LOCK:
        body1(0, 0)
    else:
        lax.fori_loop(0, n_ch // RW_LOCK, body1, 0)

    def finish():
        for p in range(np2):
            for dr in range(2):
                if zero_init:
                    s_scr[2 * p + dr] = jnp.zeros((rows2, LANES), F32)
                else:
                    s_scr[2 * p + dr] = s0_ref[0, dr, p]

        def body2(i, carry):
            cs = (i, n_ch - 1 - i)
            ids = [(p, dr) for p in range(np2) for dr in range(2)]
            ns = [(p * 2 + dr) * n_ch + cs[dr] for p, dr in ids]
            x = [_dot_nt(tar_scr[n], s_scr[2 * p + dr].astype(BF16)) for n, (p, dr) in zip(ns, ids)]
            u2 = [(xx[:rows2] + w2_scr[n]).astype(BF16) for n, xx in zip(ns, x)]
            upd = [_dot_tn(u, b2_scr[n]) for n, u in zip(ns, u2)]
            yb = [_dot(lrb_scr[n], u) for n, u in zip(ns, u2)]
            for j, (n, (p, dr)) in enumerate(zip(ns, ids)):
                y2 = x[j][rows2:] + yb[j] + yl_scr[n]
                y_scr[p, dr, pl.ds(pl.multiple_of(cs[dr] * c_len, c_len), c_len), :] = y2[:c_len] + y2[c_len:]
                s_scr[2 * p + dr] = (s_scr[2 * p + dr] + upd[j] + kv_scr[n]) * pc_scr[n]
            return carry

        lax.fori_loop(0, n_ch, body2, 0)

        def head_mean(xs):
            parts = [_split3(x) for x in xs]
            return [(_dot(a, head_ones) + _dot(b, head_ones) + _dot(c, head_ones)) * (1.0 / RWKV_HD)
                    for a, b, c in parts]

        ys = [y_scr[p, 0] + y_scr[p, 1] for p in range(np2)]
        yc = [y - m for y, m in zip(ys, head_mean(ys))]
        var = head_mean([x * x for x in yc])
        for p in range(np2):
            ln = slice(p * LANES, (p + 1) * LANES)
            o_ref[:, ln] = yc[p] * lax.rsqrt(var[p] + GN_EPS) * gn_ref[:, ln] + bon_scr[p]
            if zero_init:
                for dr in range(2):
                    s2 = s_scr[2 * p + dr]
                    st_ref[0, 0, dr, 2 * p] = s2[:RWKV_HD, :RWKV_HD]
                    st_ref[0, 0, dr, 2 * p + 1] = s2[RWKV_HD:, RWKV_HD:]

    if defer:
        pl.when(grp == pl.num_programs(1) - 1)(finish)
    else:
        finish()


def _rwkv_mixer(rkvg, lw, a, kkp, kap, rkp, gn, s0_pairs):
    npair = RWKV_HEADS // 2
    pg = RW_PAIRS
    wl = pg * LANES
    r2 = 2 * RW_C

    def scratch(seq, np2):
        n = 2 * np2 * (seq // RW_C)
        return [
            pltpu.VMEM((pg, seq, LANES), F32), pltpu.VMEM((pg, 2, seq, LANES), F32),
            pltpu.VMEM((np2, seq, LANES), F32), pltpu.VMEM((np2, 2, seq, LANES), F32),
            pltpu.VMEM((2 * np2, r2, LANES), F32),
            pltpu.VMEM((n, 2 * r2, LANES), BF16), pltpu.VMEM((n, r2, r2), BF16), pltpu.VMEM((n, r2, LANES), BF16),
            pltpu.VMEM((n, r2, LANES), F32), pltpu.VMEM((n, r2, LANES), F32), pltpu.VMEM((n, r2, LANES), F32),
            pltpu.VMEM((n, 1, LANES), F32),
        ]

    def seq_specs(seq, rb, gn_spec):
        par = pl.BlockSpec((1, wl), lambda s, g: (0, g))
        return [
            pl.BlockSpec((3, seq, wl), lambda s, g: (0, rb + s, g)),
            pl.BlockSpec((2, seq, wl), lambda s, g: (0, rb + s, g)),
            pl.BlockSpec((2, seq, wl), lambda s, g: (0, rb + s, g)),
            par, par, par, gn_spec,
        ]

    pars = [u.reshape(1, D_MODEL) for u in (kkp, kap, rkp, gn)]
    o_p, st = pl.pallas_call(
        functools.partial(_rwkv_kernel, seq=SEQ, zero_init=True, pg=pg, np2=npair),
        grid=(BATCH, npair // pg),
        in_specs=seq_specs(SEQ, 0, pl.BlockSpec((1, D_MODEL), lambda s, g: (0, 0))),
        out_specs=[
            pl.BlockSpec((SEQ, D_MODEL), lambda s, g: (s, 0)),
            pl.BlockSpec((1, 1, 2, RWKV_HEADS, RWKV_HD, RWKV_HD), lambda s, g: (s, 0, 0, 0, 0, 0)),
        ],
        out_shape=[
            jax.ShapeDtypeStruct((N_TOK, D_MODEL), F32),
            jax.ShapeDtypeStruct((BATCH, 1, 2, RWKV_HEADS, RWKV_HD, RWKV_HD), F32),
        ],
        scratch_shapes=scratch(SEQ, npair),
        compiler_params=_params("arbitrary", "arbitrary"),
        name="rwkv_prompt",
    )(rkvg, lw, a, *pars)
    rb = N_PROMPT_TOK // DEC_SEQ
    o = pl.pallas_call(
        functools.partial(_rwkv_kernel, seq=DEC_SEQ, zero_init=False, pg=pg, np2=pg),
        grid=(DEC_BATCH, npair // pg),
        in_specs=seq_specs(DEC_SEQ, rb, pl.BlockSpec((1, wl), lambda s, g: (0, g))) + [
            pl.BlockSpec((1, 2, pg, LANES, LANES), lambda s, g: (s, 0, g, 0, 0)),
            pl.BlockSpec(memory_space=pl.ANY),
        ],
        out_specs=pl.BlockSpec((DEC_SEQ, wl), lambda s, g: (rb + s, g)),
        out_shape=jax.ShapeDtypeStruct((N_TOK, D_MODEL), F32),
        input_output_aliases={8: 0},
        scratch_shapes=scratch(DEC_SEQ, pg),
        compiler_params=_params("arbitrary", "arbitrary"),
        name="rwkv_latent",
    )(rkvg, lw, a, *pars, s0_pairs, o_p)
    return o, st


def _state_pairs(s0):
    s = s0.reshape(DEC_BATCH, 2, RWKV_HEADS // 2, 2, RWKV_HD, RWKV_HD)
    z = jnp.zeros_like(s[:, :, :, 0])
    top = jnp.concatenate([s[:, :, :, 0], z], axis=-1)
    bot = jnp.concatenate([z, s[:, :, :, 1]], axis=-1)
    return jnp.concatenate([top, bot], axis=-2)


def _layer_rwkv(x, p, mod, j):
    i = N_MIXERS * j + 1
    wa, wb, aa, ab = p['rwkv_wA'][j], p['rwkv_wB'][j], p['rwkv_aA'][j], p['rwkv_aB'][j]
    z = jnp.zeros_like(wb[0])
    wa2 = jnp.concatenate([wa[0], wa[1]], axis=1).astype(BF16)
    aa2 = jnp.concatenate([aa[0], aa[1]], axis=1).astype(BF16)
    wb_pad = jnp.stack([jnp.concatenate([wb[0], z]), jnp.concatenate([z, wb[1]])]).astype(BF16)
    ab_pad = jnp.stack([jnp.concatenate([ab[0], z]), jnp.concatenate([z, ab[1]])]).astype(BF16)
    xm, lw, a = _rwkv_prep(x, p['norm_w'][i], mod, p['rwkv_mu'][j], wa2, aa2, wb_pad, ab_pad,
                           p['rwkv_w0'][j], p['rwkv_a0'][j])
    rkvg = _rwkv_rkvg(xm, p['rwkv_w_in'][j].astype(BF16))
    o, st = _rwkv_mixer(rkvg, lw, a, p['rwkv_kk'][j], p['rwkv_ka'][j], p['rwkv_rk'][j], p['rwkv_gn'][j],
                          _state_pairs(p['state_rwkv'][:, j]))
    x = _out_proj(o, rkvg[3], 0, p['rwkv_w_out'][j], x, mod, p['final_norm_w'], False)
    return x, st


DIFF_W = 2 * DIFF_HD
ATT_QB = 256
DIFF_GROUP = 4


def _first_half_lanes():
    return lax.broadcasted_iota(jnp.int32, (1, LANES), 1) < LANES // 2


def _diff_lambda(lam_ref, lam_init):
    lp = lam_ref[...]
    return (jnp.exp(jnp.sum(lp[0:1] * lp[1:2], keepdims=True))
            - jnp.exp(jnp.sum(lp[2:3] * lp[3:4], keepdims=True)) + lam_init)


def _diff_heads(items, lam, lam_init):
    first = _first_half_lanes()
    scale = DIFF_HD ** -0.5
    sub = [(q, keys, comp) for q, keys, _ in items for comp in range(2)]
    qm = [jnp.where(first if comp == 0 else ~first, q, 0.0).astype(BF16) for q, _, comp in sub]
    s = [[_dot_nt(x, kb) * scale for kb, _ in keys] for x, (_, keys, _) in zip(qm, sub)]
    m = [functools.reduce(jnp.maximum, [jnp.max(u, axis=-1, keepdims=True) for u in ss]) for ss in s]
    e = [[jnp.exp(u - mm) for u in ss] for ss, mm in zip(s, m)]
    inv = [1.0 / functools.reduce(lambda x, y: x + y, [jnp.sum(u, axis=-1, keepdims=True) for u in ee]) for ee in e]
    outs = []
    for i, (_, keys, gn) in enumerate(items):
        o = None
        for n, (_, vb) in enumerate(keys):
            p = e[2 * i][n] * inv[2 * i] - lam * (e[2 * i + 1][n] * inv[2 * i + 1])
            part = _dot(p.astype(BF16), vb)
            o = part if o is None else o + part
        outs.append(o)
    return [o * lax.rsqrt(jnp.mean(o * o, axis=-1, keepdims=True) + EPS) * gn * (1.0 - lam_init)
            for o, (_, _, gn) in zip(outs, items)]


def _diff_prompt_kernel(lam_ref, q_ref, k_ref, v_ref, gn_ref, o_ref, ck_ref, cv_ref, *, lam_init):
    lam = _diff_lambda(lam_ref, lam_init)
    for h0 in range(0, DIFF_HEADS, DIFF_GROUP):
        items = []
        for h in range(h0, h0 + DIFF_GROUP):
            sl = slice(h * DIFF_W, (h + 1) * DIFF_W)
            k = k_ref[:, sl]
            v = v_ref[:, sl]
            ck_ref[0, 0, h] = k
            cv_ref[0, 0, h] = v
            items.append((q_ref[:, sl], [(k.astype(BF16), v.astype(BF16))], gn_ref[:, sl]))
        for h, o in zip(range(h0, h0 + DIFF_GROUP), _diff_heads(items, lam, lam_init)):
            o_ref[:, h * DIFF_W:(h + 1) * DIFF_W] = o


def _diff_latent_kernel(lam_ref, q_ref, k_ref, v_ref, ck_ref, cv_ref, cos_ref, slo_ref, shi_ref, gn_ref,
                        _prev_ref, o_ref, *, lam_init):
    lam = _diff_lambda(lam_ref, lam_init)
    tabs = (cos_ref[...], slo_ref[...], shi_ref[...])
    q = _rope(q_ref[...], *tabs, DIFF_HD // 4)
    k = _rope(k_ref[...], *tabs, DIFF_HD // 4)
    keys = [(k.astype(BF16), v_ref[...].astype(BF16)),
            (ck_ref[0, 0, 0].astype(BF16), cv_ref[0, 0, 0].astype(BF16))]
    gn = gn_ref[...]
    n_blk = DEC_SEQ // ATT_QB
    items = [(q[qi * ATT_QB:(qi + 1) * ATT_QB], keys, gn) for qi in range(n_blk)]
    for qi, o in enumerate(_diff_heads(items, lam, lam_init)):
        o_ref[qi * ATT_QB:(qi + 1) * ATT_QB, :] = o


def _diff_attention(proj, lam_p, gn_w, cache_k, cache_v, j, lam_init):
    gn = gn_w.reshape(1, D_MODEL)
    lam_spec = pl.BlockSpec((4, DIFF_HD), lambda *_: (0, 0))
    cache_out = pl.BlockSpec((1, 1, DIFF_HEADS, SEQ, DIFF_W), lambda b: (b, 0, 0, 0, 0))
    cache_shape = jax.ShapeDtypeStruct((BATCH, 1, DIFF_HEADS, SEQ, DIFF_W), F32)
    o_p, new_k, new_v = pl.pallas_call(
        functools.partial(_diff_prompt_kernel, lam_init=lam_init),
        grid=(BATCH,),
        in_specs=[
            lam_spec,
            pl.BlockSpec((SEQ, D_MODEL), lambda b: (b, 0)),
            pl.BlockSpec((SEQ, D_MODEL), lambda b: (b, 1)),
            pl.BlockSpec((SEQ, D_MODEL), lambda b: (b, 2)),
            pl.BlockSpec((1, D_MODEL), lambda b: (0, 0)),
        ],
        out_specs=[pl.BlockSpec((SEQ, D_MODEL), lambda b: (b, 0)), cache_out, cache_out],
        out_shape=[jax.ShapeDtypeStruct((N_TOK, D_MODEL), F32), cache_shape, cache_shape],
        compiler_params=_params("arbitrary"),
        name="diff_prompt",
    )(lam_p, proj, proj, proj, gn)
    cos, slo, shi = (jnp.concatenate([u, u], axis=-1) for u in _rope_tables(DIFF_HD))
    rb = N_PROMPT_TOK // DEC_SEQ
    nh = DIFF_HEADS
    tab = pl.BlockSpec((DEC_SEQ, DIFF_W), lambda b, h: (0, 0))
    cache = pl.BlockSpec((1, 1, 1, PAST_LEN, DIFF_W), lambda b, h: (b, j, h, 0, 0))
    o = pl.pallas_call(
        functools.partial(_diff_latent_kernel, lam_init=lam_init),
        grid=(DEC_BATCH, nh),
        in_specs=[
            lam_spec,
            pl.BlockSpec((DEC_SEQ, DIFF_W), lambda b, h: (rb + b, h)),
            pl.BlockSpec((DEC_SEQ, DIFF_W), lambda b, h: (rb + b, nh + h)),
            pl.BlockSpec((DEC_SEQ, DIFF_W), lambda b, h: (rb + b, 2 * nh + h)),
            cache, cache, tab, tab, tab,
            pl.BlockSpec((1, DIFF_W), lambda b, h: (0, h)),
            pl.BlockSpec(memory_space=pl.ANY),
        ],
        out_specs=pl.BlockSpec((DEC_SEQ, DIFF_W), lambda b, h: (rb + b, h)),
        out_shape=jax.ShapeDtypeStruct((N_TOK, D_MODEL), F32),
        input_output_aliases={10: 0},
        compiler_params=_params("arbitrary", "arbitrary"),
        name="diff_latent",
    )(lam_p, proj, proj, proj, cache_k, cache_v, cos, slo, shi, gn, o_p)
    return o, new_k, new_v


def _layer_diff(x, p, mod, j, i):
    lam_init = 0.8 - 0.6 * math.exp(-0.3 * i)
    proj = _in_proj(x, p['norm_w'][i], mod, p['diff_w_in'][j], 1024)
    o, new_k, new_v = _diff_attention(proj, p['diff_lambda'][j], p['diff_gn'][j], p['cache_diff_k'],
                                      p['cache_diff_v'], j, lam_init)
    x = _out_proj(o, proj, 3, p['diff_w_out'][j], x, mod, p['final_norm_w'], False)
    return x, new_k, new_v


NA_ROWS = DEC_SEQ // GRID_W
NA_WR = min(NA_WIN_R, NA_ROWS)
NA_LOC = NA_WR * GRID_W
NA_ROW_GROUP = 4
NA_PAIR_GROUP = 2


def _na_prompt_kernel(q_ref, k_ref, v_ref, o_ref, ck_ref, cv_ref):
    first = _first_half_lanes()
    scale = NA_HD ** -0.5
    for p0 in range(0, NA_HEADS // 2, NA_PAIR_GROUP):
        pairs = range(p0, p0 + NA_PAIR_GROUP)
        kb, vb = [], []
        for pr in pairs:
            sl = slice(pr * LANES, (pr + 1) * LANES)
            k = k_ref[:, sl]
            v = v_ref[:, sl]
            for half in range(2):
                ck_ref[0, 0, 2 * pr + half] = k[:, half * NA_HD:(half + 1) * NA_HD]
                cv_ref[0, 0, 2 * pr + half] = v[:, half * NA_HD:(half + 1) * NA_HD]
            kb.append(k.astype(BF16))
            vb.append(v.astype(BF16))
        items = [(i, half) for i in range(NA_PAIR_GROUP) for half in range(2)]
        qm = [jnp.where(first if half == 0 else ~first, q_ref[:, (p0 + i) * LANES:(p0 + i + 1) * LANES], 0.0)
              .astype(BF16) for i, half in items]
        s = [_dot_nt(x, kb[i]) * scale for x, (i, _) in zip(qm, items)]
        e = [jnp.exp(x - jnp.max(x, axis=-1, keepdims=True)) for x in s]
        inv = [1.0 / jnp.sum(x, axis=-1, keepdims=True) for x in e]
        outs = [_dot(x.astype(BF16), vb[i]) * z for x, z, (i, _) in zip(e, inv, items)]
        for i in range(NA_PAIR_GROUP):
            o_ref[:, (p0 + i) * LANES:(p0 + i + 1) * LANES] = jnp.where(first, outs[2 * i], outs[2 * i + 1])


def _na_latent_kernel(q_ref, k_ref, v_ref, kc_ref, vc_ref, tab_ref, _prev_ref, o_ref):
    first = _first_half_lanes()
    scale = NA_HD ** -0.5
    kb = k_ref[...].astype(BF16)
    vb = v_ref[...].astype(BF16)
    kcb = kc_ref[0, 0].astype(BF16)
    vcb = vc_ref[0, 0].astype(BF16)
    qcol = lax.broadcasted_iota(jnp.int32, (GRID_W, NA_LOC), 0)
    kcol = lax.broadcasted_iota(jnp.int32, (GRID_W, NA_LOC), 1) & (GRID_W - 1)
    cstart = jnp.clip(qcol - NA_WIN_C // 2, 0, GRID_W - NA_WIN_C)
    col_ok = (kcol >= cstart) & (kcol < cstart + NA_WIN_C)
    def bias_of(r, rs, half):
        parts = []
        for w in range(0, NA_WR, 2):
            src = jnp.broadcast_to(tab_ref[half, rs + w - r + NA_WIN_R - 1], (GRID_W, LANES))
            parts.append(pltpu.roll(src, LANES - (NA_WIN_C - 1), axis=1, stride=1, stride_axis=0))
        return jnp.concatenate(parts, axis=1)

    for r0 in range(0, NA_ROWS, NA_ROW_GROUP):
        items = [(r, min(max(r - NA_WR // 2, 0), NA_ROWS - NA_WR), half)
                 for r in range(r0, r0 + NA_ROW_GROUP) for half in range(2)]
        qm = [jnp.where(first if half == 0 else ~first, q_ref[r * GRID_W:(r + 1) * GRID_W, :], 0.0).astype(BF16)
              for r, _, half in items]
        s_loc = [_dot_nt(x, kb[rs * GRID_W:(rs + NA_WR) * GRID_W]) for x, (_, rs, _) in zip(qm, items)]
        s_ctx = [_dot_nt(x, kcb) * scale for x in qm]
        s_loc = [jnp.where(col_ok, x * scale + bias_of(*it), -jnp.inf) for x, it in zip(s_loc, items)]
        m = [jnp.maximum(jnp.max(x, axis=-1, keepdims=True), jnp.max(y, axis=-1, keepdims=True))
             for x, y in zip(s_loc, s_ctx)]
        e_loc = [jnp.exp(x - mm) for x, mm in zip(s_loc, m)]
        e_ctx = [jnp.exp(x - mm) for x, mm in zip(s_ctx, m)]
        inv = [1.0 / (jnp.sum(x, axis=-1, keepdims=True) + jnp.sum(y, axis=-1, keepdims=True))
               for x, y in zip(e_loc, e_ctx)]
        pv = [_dot(x.astype(BF16), vb[rs * GRID_W:(rs + NA_WR) * GRID_W]) for x, (_, rs, _) in zip(e_loc, items)]
        pc = [_dot(x.astype(BF16), vcb) for x in e_ctx]
        outs = [(x + y) * z for x, y, z in zip(pv, pc, inv)]
        for n in range(0, len(items), 2):
            r = items[n][0]
            o_ref[r * GRID_W:(r + 1) * GRID_W, :] = jnp.where(first, outs[n], outs[n + 1])


def _na_bias_pairs(table):
    t = table.astype(F32)
    nc = 2 * NA_WIN_C - 1
    z = jnp.zeros(t[:, :-1].shape[:2] + (GRID_W - nc,), F32)
    return jnp.concatenate([t[:, :-1], z, t[:, 1:], z], axis=-1)[:, :, None, :]


def _pair_heads(cache):
    c = cache.reshape(DEC_BATCH, NA_HEADS // 2, 2, PAST_LEN, NA_HD)
    return c.transpose(0, 1, 3, 2, 4).reshape(DEC_BATCH, NA_HEADS // 2, PAST_LEN, LANES)


def _na_attention(proj, bias_table, cache_k, cache_v):
    cache_out = pl.BlockSpec((1, 1, NA_HEADS, SEQ, NA_HD), lambda b: (b, 0, 0, 0, 0))
    cache_shape = jax.ShapeDtypeStruct((BATCH, 1, NA_HEADS, SEQ, NA_HD), F32)
    o_p, new_k, new_v = pl.pallas_call(
        _na_prompt_kernel,
        grid=(BATCH,),
        in_specs=[
            pl.BlockSpec((SEQ, D_MODEL), lambda b: (b, 0)),
            pl.BlockSpec((SEQ, D_MODEL), lambda b: (b, 1)),
            pl.BlockSpec((SEQ, D_MODEL), lambda b: (b, 2)),
        ],
        out_specs=[pl.BlockSpec((SEQ, D_MODEL), lambda b: (b, 0)), cache_out, cache_out],
        out_shape=[jax.ShapeDtypeStruct((N_TOK, D_MODEL), F32), cache_shape, cache_shape],
        compiler_params=_params("arbitrary"),
        name="na_prompt",
    )(proj, proj, proj)
    rb = N_PROMPT_TOK // DEC_SEQ
    npair = NA_HEADS // 2
    cache = pl.BlockSpec((1, 1, PAST_LEN, LANES), lambda pr, b: (b, pr, 0, 0))
    o = pl.pallas_call(
        _na_latent_kernel,
        grid=(npair, DEC_BATCH),
        in_specs=[
            pl.BlockSpec((DEC_SEQ, LANES), lambda pr, b: (rb + b, pr)),
            pl.BlockSpec((DEC_SEQ, LANES), lambda pr, b: (rb + b, npair + pr)),
            pl.BlockSpec((DEC_SEQ, LANES), lambda pr, b: (rb + b, 2 * npair + pr)),
            cache, cache,
            pl.BlockSpec((2, 2 * NA_WIN_R - 2, 1, LANES), lambda pr, b: (pr, 0, 0, 0)),
            pl.BlockSpec(memory_space=pl.ANY),
        ],
        out_specs=pl.BlockSpec((DEC_SEQ, LANES), lambda pr, b: (rb + b, pr)),
        out_shape=jax.ShapeDtypeStruct((N_TOK, D_MODEL), F32),
        input_output_aliases={6: 0},
        compiler_params=_params("arbitrary", "arbitrary"),
        name="na_latent",
    )(proj, proj, proj, _pair_heads(cache_k), _pair_heads(cache_v), _na_bias_pairs(bias_table), o_p)
    return o, new_k, new_v


def _layer_na(x, p, mod, j, final):
    i = N_MIXERS * j + 3
    proj = _in_proj(x, p['norm_w'][i], mod, p['na_w_in'][j], 1024)
    o, new_k, new_v = _na_attention(proj, p['na_bias'][j], p['cache_na_k'][:, j], p['cache_na_v'][:, j])
    args = (o, proj, 3, p['na_w_out'][j], x, mod, p['final_norm_w'])
    if final:
        x = (_out_proj(*args, True, rows=(0, N_PROMPT_TOK)), _out_proj(*args, True, rows=(N_PROMPT_TOK, N_TOK)))
    else:
        x = _out_proj(*args, False)
    return x, new_k, new_v


def kernel(x_prompt, x_sample, state_ret, state_rwkv, cache_diff_k, cache_diff_v, cache_na_k, cache_na_v,
           c, c_ctx, norm_w, w_mod, b_mod, final_norm_w,
           ret_w_in, ret_decay, ret_gn, ret_w_out,
           rwkv_mu, rwkv_w_in, rwkv_w0, rwkv_wA, rwkv_wB, rwkv_a0, rwkv_aA, rwkv_aB,
           rwkv_kk, rwkv_ka, rwkv_rk, rwkv_gn, rwkv_w_out,
           diff_w_in, diff_lambda, diff_gn, diff_w_out,
           na_w_in, na_bias, na_w_out):
    p = dict(locals())
    cond = jnp.zeros((N_COND, D_MODEL), F32).at[0].set(c_ctx).at[1:1 + DEC_BATCH].set(c)
    mods = _modulation(cond, w_mod, b_mod)
    x = jnp.concatenate([x_prompt.reshape(N_PROMPT_TOK, D_MODEL), x_sample.reshape(N_SAMPLE_TOK, D_MODEL)])
    new = {n: [] for n in ('ret', 'rwkv', 'dk', 'dv', 'nk', 'nv')}
    for i in range(DEPTH):
        kind, j = i % N_MIXERS, i // N_MIXERS
        if kind == 0:
            x, st = _layer_ret(x, p, mods[i], j)
            new['ret'].append(st)
        elif kind == 1:
            x, st = _layer_rwkv(x, p, mods[i], j)
            new['rwkv'].append(st)
        elif kind == 2:
            x, ck, cv = _layer_diff(x, p, mods[i], j, i)
            new['dk'].append(ck)
            new['dv'].append(cv)
        else:
            x, ck, cv = _layer_na(x, p, mods[i], j, final=(i == DEPTH - 1))
            new['nk'].append(ck)
            new['nv'].append(cv)
    if DEPTH % N_MIXERS:
        raise NotImplementedError("the final norm is fused into the last neighbourhood-attention layer")
    cat = lambda xs: xs[0] if len(xs) == 1 else jnp.concatenate(xs, axis=1)
    return (x[0].reshape(BATCH, SEQ, D_MODEL), x[1].reshape(DEC_BATCH, DEC_SEQ, D_MODEL),
            cat(new['ret']), cat(new['rwkv']), cat(new['dk']), cat(new['dv']), cat(new['nk']), cat(new['nv']))
```

```python
import functools
import math

import jax
import jax.numpy as jnp
from jax import lax
from jax.experimental import pallas as pl
from jax.experimental.pallas import tpu as pltpu

F32 = jnp.float32
BF16 = jnp.bfloat16

D_MODEL = 1024
BATCH = 32
SEQ = 256
DEPTH = 4
N_MIXERS = 4
DEC_BATCH = 2
DEC_SEQ = 1024
PAST_LEN = 256
GRID_W = 64

RET_HEADS = 4
RET_DK = 256
RET_DV = 512
RET_QK = 1024
RET_V = 2048

RWKV_HD = 64
RWKV_HEADS = 16
RWKV_RANK = 64

DIFF_HEADS = 8
DIFF_HD = 64

NA_HEADS = 16
NA_HD = 64
NA_WIN_R = 8
NA_WIN_C = 16

ROPE_BASE = 10000.0
EPS = 1e-6
GN_EPS = 1e-5

N_PROMPT_TOK = BATCH * SEQ
N_SAMPLE_TOK = DEC_BATCH * DEC_SEQ
N_TOK = N_PROMPT_TOK + N_SAMPLE_TOK
N_COND = 8

LANES = 128
VMEM_LIMIT = 56 * 2 ** 20


def _params(*sem):
    return pltpu.CompilerParams(dimension_semantics=sem, vmem_limit_bytes=VMEM_LIMIT)


def _cond_of_tile(i, tm):
    npt = N_PROMPT_TOK // tm
    return jnp.where(i < npt, 0, 1 + (i - npt) // (DEC_SEQ // tm))


def _sigmoid(x):
    return 1.0 / (1.0 + jnp.exp(-x))


def _silu(x):
    return x * _sigmoid(x)


def _dot(a, b):
    return jnp.dot(a, b, preferred_element_type=F32)


def _dot_nt(a, b):
    return lax.dot_general(a, b, (((1,), (1,)), ((), ())), preferred_element_type=F32)


def _dot_tn(a, b):
    return lax.dot_general(a, b, (((0,), (0,)), ((), ())), preferred_element_type=F32)


def _softmax_rows(s):
    m = jnp.max(s, axis=-1, keepdims=True)
    e = jnp.exp(s - m)
    return e / jnp.sum(e, axis=-1, keepdims=True)


def _mod_kernel(c_ref, w_ref, b_ref, o_ref):
    s = _silu(c_ref[...])
    o_ref[0] = jnp.dot(s, w_ref[0], precision=lax.Precision.HIGHEST, preferred_element_type=F32) + b_ref[0]


def _modulation(cond, w_mod, b_mod):
    tn = D_MODEL
    out = pl.pallas_call(
        _mod_kernel,
        grid=(DEPTH, 3 * D_MODEL // tn),
        in_specs=[
            pl.BlockSpec((N_COND, D_MODEL), lambda l, j: (0, 0)),
            pl.BlockSpec((1, D_MODEL, tn), lambda l, j: (l, 0, j)),
            pl.BlockSpec((1, 1, tn), lambda l, j: (l, 0, j)),
        ],
        out_specs=pl.BlockSpec((1, N_COND, tn), lambda l, j: (l, 0, j)),
        out_shape=jax.ShapeDtypeStruct((DEPTH, N_COND, 3 * D_MODEL), F32),
        compiler_params=_params("arbitrary", "arbitrary"),
        name="modulation",
    )(cond, w_mod, b_mod.reshape(DEPTH, 1, 3 * D_MODEL))
    return out.reshape(DEPTH, N_COND, 3, 1, D_MODEL)


def _norm_mod(x, nw, mod_ref):
    ms = jnp.mean(x * x, axis=-1, keepdims=True)
    y = x * lax.rsqrt(ms + EPS) * nw
    return y * (1.0 + mod_ref[0, 1]) + mod_ref[0, 0]


IN_TM = 1024


def _in_proj_kernel(x_ref, nw_ref, mod_ref, w_ref, o_ref, h_ref):
    @pl.when(pl.program_id(1) == 0)
    def _():
        h_ref[...] = _norm_mod(x_ref[...], nw_ref[...], mod_ref).astype(BF16)

    o_ref[...] = _dot(h_ref[...], w_ref[...]).astype(o_ref.dtype)


def _in_proj(x, norm_w, mod, w, tn, out_dtype=F32):
    n = w.shape[1]
    w = w.astype(BF16)
    return pl.pallas_call(
        _in_proj_kernel,
        grid=(N_TOK // IN_TM, n // tn),
        in_specs=[
            pl.BlockSpec((IN_TM, D_MODEL), lambda i, j: (i, 0)),
            pl.BlockSpec((1, D_MODEL), lambda i, j: (0, 0)),
            pl.BlockSpec((1, 3, 1, D_MODEL), lambda i, j: (_cond_of_tile(i, IN_TM), 0, 0, 0)),
            pl.BlockSpec((D_MODEL, tn), lambda i, j: (0, j)),
        ],
        out_specs=pl.BlockSpec((IN_TM, tn), lambda i, j: (i, j)),
        out_shape=jax.ShapeDtypeStruct((N_TOK, n), out_dtype),
        scratch_shapes=[pltpu.VMEM((IN_TM, D_MODEL), BF16)],
        compiler_params=_params("arbitrary", "arbitrary"),
        name="in_proj",
    )(x, norm_w.reshape(1, D_MODEL), mod, w)


OUT_TM = 256


def _out_proj_kernel(o_ref, g_ref, w_ref, x_ref, mod_ref, fw_ref, y_ref, wb_ref, *, final):
    @pl.when(pl.program_id(0) == 0)
    def _():
        wb_ref[...] = w_ref[...].astype(BF16)

    a = (o_ref[...] * _silu(g_ref[...].astype(F32))).astype(BF16)
    xn = x_ref[...] + mod_ref[0, 2] * _dot(a, wb_ref[...])
    if final:
        ms = jnp.mean(xn * xn, axis=-1, keepdims=True)
        xn = xn * lax.rsqrt(ms + EPS) * fw_ref[...]
    y_ref[...] = xn


def _out_proj(o, g_arr, g_blk, w, x, mod, final_w, final, rows=(0, N_TOK), g_row0=0):
    k = w.shape[0]
    t0 = rows[0] // OUT_TM
    g0 = g_row0 // OUT_TM
    return pl.pallas_call(
        functools.partial(_out_proj_kernel, final=final),
        grid=((rows[1] - rows[0]) // OUT_TM,),
        in_specs=[
            pl.BlockSpec((OUT_TM, k), lambda i: (t0 + i, 0)),
            pl.BlockSpec((OUT_TM, k), lambda i: (g0 + t0 + i, g_blk)),
            pl.BlockSpec((k, D_MODEL), lambda i: (0, 0)),
            pl.BlockSpec((OUT_TM, D_MODEL), lambda i: (t0 + i, 0)),
            pl.BlockSpec((1, 3, 1, D_MODEL), lambda i: (_cond_of_tile(t0 + i, OUT_TM), 0, 0, 0)),
            pl.BlockSpec((1, D_MODEL), lambda i: (0, 0)),
        ],
        out_specs=pl.BlockSpec((OUT_TM, D_MODEL), lambda i: (i, 0)),
        out_shape=jax.ShapeDtypeStruct((rows[1] - rows[0], D_MODEL), F32),
        scratch_shapes=[pltpu.VMEM((k, D_MODEL), BF16)],
        compiler_params=_params("arbitrary"),
        name="out_proj",
    )(o, g_arr, w, x, mod, final_w.reshape(1, D_MODEL))


def _rope_tables(d):
    q = d // 4
    t = jnp.arange(DEC_SEQ)
    row = (t // GRID_W).astype(F32)
    col = (t % GRID_W).astype(F32)
    inv = ROPE_BASE ** (-jnp.arange(0, 2 * q, 2, dtype=F32) / (2 * q))
    ar = row[:, None] * inv[None, :]
    ac = col[:, None] * inv[None, :]
    z = jnp.zeros_like(ar)
    cos = jnp.concatenate([jnp.cos(ar), jnp.cos(ar), jnp.cos(ac), jnp.cos(ac)], axis=-1)
    sin_lo = jnp.concatenate([-jnp.sin(ar), z, -jnp.sin(ac), z], axis=-1)
    sin_hi = jnp.concatenate([z, jnp.sin(ar), z, jnp.sin(ac)], axis=-1)
    return cos, sin_lo, sin_hi


def _rope(x, cos, sin_lo, sin_hi, q):
    w = x.shape[-1]
    x_next = pltpu.roll(x, w - q, axis=1)
    x_prev = pltpu.roll(x, q, axis=1)
    return x * cos + x_next * sin_lo + x_prev * sin_hi


RET_QB = 256


def _ret_kernel(lg_ref, q_ref, k_ref, v_ref, gn_ref, *rest, seq, latent):
    if latent:
        cos_ref, slo_ref, shi_ref, s0_ref, _prev_ref, o_ref = rest
    else:
        o_ref, st_ref = rest
    h = pl.program_id(1)
    lgf = lg_ref[0, h]
    lgb = lg_ref[1, h]
    q = q_ref[...].astype(F32)
    k = k_ref[...].astype(F32)
    if latent:
        q = _rope(q, cos_ref[...], slo_ref[...], shi_ref[...], RET_DK // 4)
        k = _rope(k, cos_ref[...], slo_ref[...], shi_ref[...], RET_DK // 4)
    k = k * (RET_DK ** -0.5)
    kb = k.astype(BF16)
    vb = v_ref[...].astype(BF16)
    gn = gn_ref[...]
    for qi in range(seq // RET_QB):
        qblk = q[qi * RET_QB:(qi + 1) * RET_QB]
        s = _dot_nt(qblk.astype(BF16), kb)
        ii = lax.broadcasted_iota(jnp.int32, (RET_QB, seq), 0) + qi * RET_QB
        jj = lax.broadcasted_iota(jnp.int32, (RET_QB, seq), 1)
        gap = (ii - jj).astype(F32)
        dec = (jnp.where(gap >= 0, jnp.exp(lgf * jnp.maximum(gap, 0.0)), 0.0)
               + jnp.where(gap <= 0, jnp.exp(lgb * jnp.maximum(-gap, 0.0)), 0.0))
        o = _dot((s * dec).astype(BF16), vb)
        if latent:
            pos = (lax.broadcasted_iota(jnp.int32, (RET_QB, 1), 0) + qi * RET_QB).astype(F32)
            qf = qblk * jnp.exp(lgf * (pos + 1.0))
            qr = qblk * jnp.exp(lgb * (seq - pos))
            o = o + _dot(qf.astype(BF16), s0_ref[0, 0, 0, 0].astype(BF16))
            o = o + _dot(qr.astype(BF16), s0_ref[0, 0, 1, 0].astype(BF16))
        oc = o - jnp.mean(o, axis=-1, keepdims=True)
        o = oc * lax.rsqrt(jnp.mean(oc * oc, axis=-1, keepdims=True) + GN_EPS) * gn
        o_ref[qi * RET_QB:(qi + 1) * RET_QB, :] = o
    if not latent:
        pos = lax.broadcasted_iota(jnp.int32, (seq, 1), 0).astype(F32)
        kf = k * jnp.exp(lgf * (seq - 1.0 - pos))
        kr = k * jnp.exp(lgb * pos)
        st_ref[0, 0, 0, 0] = _dot_tn(kf.astype(BF16), vb)
        st_ref[0, 0, 1, 0] = _dot_tn(kr.astype(BF16), vb)


def _retention(p, log_g, gn_w, state_ret, j):
    smem = pl.BlockSpec(memory_space=pltpu.SMEM)
    gn = gn_w.reshape(1, RET_V)
    kq = RET_QK // RET_DK
    o_p, st = pl.pallas_call(
        functools.partial(_ret_kernel, seq=SEQ, latent=False),
        grid=(BATCH, RET_HEADS),
        in_specs=[
            smem,
            pl.BlockSpec((SEQ, RET_DK), lambda b, h: (b, h)),
            pl.BlockSpec((SEQ, RET_DK), lambda b, h: (b, kq + h)),
            pl.BlockSpec((SEQ, RET_DV), lambda b, h: (b, kq + h)),
            pl.BlockSpec((1, RET_DV), lambda b, h: (0, h)),
        ],
        out_specs=[
            pl.BlockSpec((SEQ, RET_DV), lambda b, h: (b, h)),
            pl.BlockSpec((1, 1, 2, 1, RET_DK, RET_DV), lambda b, h: (b, 0, 0, h, 0, 0)),
        ],
        out_shape=[
            jax.ShapeDtypeStruct((N_TOK, RET_V), F32),
            jax.ShapeDtypeStruct((BATCH, 1, 2, RET_HEADS, RET_DK, RET_DV), F32),
        ],
        compiler_params=_params("arbitrary", "arbitrary"),
        name="retention_prompt",
    )(log_g, p, p, p, gn)
    cos, slo, shi = _rope_tables(RET_DK)
    rb = N_PROMPT_TOK // DEC_SEQ
    full = pl.BlockSpec((DEC_SEQ, RET_DK), lambda b, h: (0, 0))
    o = pl.pallas_call(
        functools.partial(_ret_kernel, seq=DEC_SEQ, latent=True),
        grid=(DEC_BATCH, RET_HEADS),
        in_specs=[
            smem,
            pl.BlockSpec((DEC_SEQ, RET_DK), lambda b, h: (rb + b, h)),
            pl.BlockSpec((DEC_SEQ, RET_DK), lambda b, h: (rb + b, kq + h)),
            pl.BlockSpec((DEC_SEQ, RET_DV), lambda b, h: (rb + b, kq + h)),
            pl.BlockSpec((1, RET_DV), lambda b, h: (0, h)),
            full, full, full,
            pl.BlockSpec((1, 1, 2, 1, RET_DK, RET_DV), lambda b, h: (b, j, 0, h, 0, 0)),
            pl.BlockSpec(memory_space=pl.ANY),
        ],
        out_specs=pl.BlockSpec((DEC_SEQ, RET_DV), lambda b, h: (rb + b, h)),
        out_shape=jax.ShapeDtypeStruct((N_TOK, RET_V), F32),
        input_output_aliases={9: 0},
        compiler_params=_params("arbitrary", "arbitrary"),
        name="retention_latent",
    )(log_g, p, p, p, gn, cos, slo, shi, state_ret, o_p)
    return o, st


def _layer_ret(x, p, mod, j):
    i = N_MIXERS * j + 0
    proj = _in_proj(x, p['norm_w'][i], mod, p['ret_w_in'][j], 1024, out_dtype=BF16)
    log_g = jax.nn.log_sigmoid(p['ret_decay'][j].astype(F32))
    o, st = _retention(proj, log_g, p['ret_gn'][j], p['state_ret'], j)
    x = _out_proj(o, proj, (2 * RET_QK + RET_V) // RET_V, p['ret_w_out'][j], x, mod, p['final_norm_w'], False)
    return x, st


RW_TM = 512
RW_HALO = 8
RW_C = 64
RW_LOCK = 4
RW_PAIRS = 2


def _rwkv_prep_kernel(x_ref, xp_ref, xn_ref, nw_ref, mod_ref, mu_ref, wa_ref, aa_ref, wb_ref, ab_ref,
                      w0_ref, a0_ref, xm_ref, lw_ref, a_ref):
    i = pl.program_id(0)
    nw = nw_ref[...]
    h = _norm_mod(x_ref[...], nw, mod_ref)
    h_before = _norm_mod(xp_ref[RW_HALO - 1:RW_HALO, :], nw, mod_ref)
    h_after = _norm_mod(xn_ref[0:1, :], nw, mod_ref)
    seq = jnp.where(i < N_PROMPT_TOK // RW_TM, SEQ, DEC_SEQ)
    row = lax.broadcasted_iota(jnp.int32, (RW_TM, 1), 0)
    t = (row + i * RW_TM) & (seq - 1)
    prev = jnp.where(row == 0, h_before, pltpu.roll(h, 1, axis=0))
    nxt = jnp.where(row == RW_TM - 1, h_after, pltpu.roll(h, RW_TM - 1, axis=0))
    prev = jnp.where(t == 0, 0.0, prev)
    nxt = jnp.where(t == seq - 1, 0.0, nxt)
    xx = 0.5 * (prev + nxt) - h
    for n, m in enumerate((0, 2, 3, 5)):
        xm_ref[n] = (h + xx * mu_ref[m:m + 1, :]).astype(BF16)
    xw = (h + xx * mu_ref[1:2, :]).astype(BF16)
    xa = (h + xx * mu_ref[4:5, :]).astype(BF16)
    lw = jnp.tanh(_dot(xw, wa_ref[...])).astype(BF16)
    la = _dot(xa, aa_ref[...]).astype(BF16)
    for dr in range(2):
        wl = w0_ref[dr:dr + 1, :] + _dot(lw, wb_ref[dr])
        lw_ref[dr] = -math.exp(-0.5) * _sigmoid(wl)
        a_ref[dr] = _sigmoid(a0_ref[dr:dr + 1, :] + _dot(la, ab_ref[dr]))


def _rwkv_prep(x, norm_w, mod, mu, wa2, aa2, wb_pad, ab_pad, w0, a0):
    nt = N_TOK // RW_TM
    hb = RW_TM // RW_HALO
    last = N_TOK // RW_HALO - 1
    full2 = lambda shape: pl.BlockSpec(shape, lambda i: (0, 0))
    full3 = lambda shape: pl.BlockSpec(shape, lambda i: (0, 0, 0))
    return pl.pallas_call(
        _rwkv_prep_kernel,
        grid=(nt,),
        in_specs=[
            pl.BlockSpec((RW_TM, D_MODEL), lambda i: (i, 0)),
            pl.BlockSpec((RW_HALO, D_MODEL), lambda i: (jnp.maximum(i * hb - 1, 0), 0)),
            pl.BlockSpec((RW_HALO, D_MODEL), lambda i: (jnp.minimum((i + 1) * hb, last), 0)),
            full2((1, D_MODEL)),
            pl.BlockSpec((1, 3, 1, D_MODEL), lambda i: (_cond_of_tile(i, RW_TM), 0, 0, 0)),
            full2((6, D_MODEL)),
            full2((D_MODEL, 2 * RWKV_RANK)),
            full2((D_MODEL, 2 * RWKV_RANK)),
            full3((2, 2 * RWKV_RANK, D_MODEL)),
            full3((2, 2 * RWKV_RANK, D_MODEL)),
            full2((2, D_MODEL)),
            full2((2, D_MODEL)),
        ],
        out_specs=[
            pl.BlockSpec((4, RW_TM, D_MODEL), lambda i: (0, i, 0)),
            pl.BlockSpec((2, RW_TM, D_MODEL), lambda i: (0, i, 0)),
            pl.BlockSpec((2, RW_TM, D_MODEL), lambda i: (0, i, 0)),
        ],
        out_shape=[
            jax.ShapeDtypeStruct((4, N_TOK, D_MODEL), BF16),
            jax.ShapeDtypeStruct((2, N_TOK, D_MODEL), F32),
            jax.ShapeDtypeStruct((2, N_TOK, D_MODEL), F32),
        ],
        compiler_params=_params("arbitrary"),
        name="rwkv_prep",
    )(x, x, x, norm_w.reshape(1, D_MODEL), mod, mu, wa2, aa2, wb_pad, ab_pad, w0, a0)


def _bmm_kernel(a_ref, w_ref, o_ref):
    o_ref[0] = _dot(a_ref[0], w_ref[...])


def _rwkv_rkvg(xm, w):
    tm = 1024
    return pl.pallas_call(
        _bmm_kernel,
        grid=(4, N_TOK // tm),
        in_specs=[
            pl.BlockSpec((1, tm, D_MODEL), lambda n, i: (n, i, 0)),
            pl.BlockSpec((D_MODEL, D_MODEL), lambda n, i: (0, n)),
        ],
        out_specs=pl.BlockSpec((1, tm, D_MODEL), lambda n, i: (n, i, 0)),
        out_shape=jax.ShapeDtypeStruct((4, N_TOK, D_MODEL), F32),
        compiler_params=_params("arbitrary", "arbitrary"),
        name="rwkv_rkvg",
    )(xm, w)


def _head_sum(x, first):
    s0 = jnp.sum(jnp.where(first, x, 0.0), axis=-1, keepdims=True)
    s1 = jnp.sum(jnp.where(first, 0.0, x), axis=-1, keepdims=True)
    return jnp.where(first, s0, s1)


def _stack_heads(x, first):
    return jnp.concatenate([jnp.where(first, x, 0.0), jnp.where(first, 0.0, x)], axis=0)


def _cumsum_rows(tri, x):
    hi = x.astype(BF16)
    r1 = x - hi.astype(F32)
    mid = r1.astype(BF16)
    lo = (r1 - mid.astype(F32)).astype(BF16)
    return _dot(tri, hi) + _dot(tri, mid) + _dot(tri, lo)


def _rwkv_chunk_kernel(*refs, seq, zero_init):
    if zero_init:
        (rkv_ref, lw_ref, a_ref, kkp_ref, kap_ref, rkp_ref, gn_ref, o_ref, st_ref,
         kk_scr, y_scr, tar_scr, lrb_scr, b2_scr, w2_scr, yl_scr, kv_scr, pc_scr) = refs
    else:
        (rkv_ref, lw_ref, a_ref, kkp_ref, kap_ref, rkp_ref, gn_ref, s0_ref, _prev_ref, o_ref,
         kk_scr, y_scr, tar_scr, lrb_scr, b2_scr, w2_scr, yl_scr, kv_scr, pc_scr) = refs
    c_len = RW_C
    n_ch = seq // c_len
    rows2 = 2 * c_len
    first = _first_half_lanes()
    kap = kap_ref[...]

    kk = rkv_ref[1] * kkp_ref[...]
    kk_scr[...] = kk * lax.rsqrt(jnp.maximum(_head_sum(kk * kk, first), 1e-12))

    rr = lax.broadcasted_iota(jnp.int32, (rows2, rows2), 0)
    cc = lax.broadcasted_iota(jnp.int32, (rows2, rows2), 1)
    eye = (rr == cc).astype(F32)
    tr = lax.broadcasted_iota(jnp.int32, (c_len, c_len), 0)
    tc = lax.broadcasted_iota(jnp.int32, (c_len, c_len), 1)

    def same(shift):
        return (rr >> shift) == (cc >> shift)

    head = same(6)
    strict = (head & (cc < rr), head & (cc > rr))
    incl = (head & (cc <= rr), head & (cc >= rr))
    tri = ((tc <= tr).astype(BF16), (tc >= tr).astype(BF16))
    last = (c_len - 1, 0)

    def phase1(chains):
        dirs = [dr for dr, _ in chains]
        rows = [pl.ds(pl.multiple_of(c * c_len, c_len), c_len) for _, c in chains]
        lw = [lw_ref[dr, rw, :] for dr, rw in zip(dirs, rows)]
        cum = [_cumsum_rows(tri[dr], x) for dr, x in zip(dirs, lw)]
        a2, r2, b2, k2, v2, pc = [], [], [], [], [], []
        for dr, rw, lw_c, cum_c in zip(dirs, rows, lw, cum):
            a = a_ref[dr, rw, :]
            k = rkv_ref[1, rw, :]
            kk_c = kk_scr[rw, :]
            e_inc = jnp.exp(cum_c)
            e_inv = jnp.exp(-cum_c)
            a2.append(_stack_heads(-kk_c * jnp.exp(cum_c - lw_c), first).astype(BF16))
            r2.append(_stack_heads(rkv_ref[0, rw, :] * e_inc, first).astype(BF16))
            b2.append(_stack_heads(kk_c * a * e_inv, first).astype(BF16))
            k2.append(_stack_heads(k * (1.0 + (a - 1.0) * kap) * e_inv, first).astype(BF16))
            v2.append(_stack_heads(rkv_ref[2, rw, :], first).astype(BF16))
            pc.append(e_inc[last[dr]:last[dr] + 1, :])
        g = [_dot_nt(jnp.concatenate([x, y], axis=0), jnp.concatenate([z, w], axis=0))
             for x, y, z, w in zip(a2, r2, b2, k2)]
        l_ab = [jnp.where(strict[dr], x[:rows2, :rows2], 0.0) for dr, x in zip(dirs, g)]
        t = [eye + jnp.where(same(1), x, 0.0) for x in l_ab]
        side = {}
        for shift in range(1, 6):
            sib = same(shift + 1) & ~same(shift)
            tb = [x.astype(BF16) for x in t]
            mid = [_dot(jnp.where(sib, x, 0.0).astype(BF16), y) for x, y in zip(l_ab, tb)]
            if shift == 1:
                side['lv'] = [_dot(jnp.where(strict[dr], x[:rows2, rows2:], 0.0).astype(BF16), y)
                              for dr, x, y in zip(dirs, g, v2)]
            elif shift == 2:
                side['yl'] = [_dot(jnp.where(incl[dr], x[rows2:, rows2:], 0.0).astype(BF16), y)
                              for dr, x, y in zip(dirs, g, v2)]
            elif shift == 3:
                side['kv'] = [_dot_tn(x, y) for x, y in zip(v2, k2)]
            t = [x + _dot(y, z.astype(BF16)) for x, y, z in zip(t, tb, mid)]
        tb = [x.astype(BF16) for x in t]
        ta = [_dot(x, y) for x, y in zip(tb, a2)]
        w2 = [_dot(x, y.astype(BF16)) for x, y in zip(tb, side['lv'])]
        for i, (dr, c) in enumerate(chains):
            n = dr * n_ch + c
            tar_scr[n, :rows2, :] = ta[i].astype(BF16)
            tar_scr[n, rows2:, :] = r2[i]
            w2_scr[n] = w2[i]
            yl_scr[n] = side['yl'][i]
            kv_scr[n] = side['kv'][i]
            lrb_scr[n] = jnp.where(incl[dr], g[i][rows2:, :rows2], 0.0).astype(BF16)
            b2_scr[n] = b2[i]
            pc_scr[n] = pc[i]

    def body1(grp, carry):
        phase1([(dr, grp * RW_LOCK + j) for j in range(RW_LOCK) for dr in range(2)])
        return carry

    if n_ch == RW_LOCK:
        body1(0, 0)
    else:
        lax.fori_loop(0, n_ch // RW_LOCK, body1, 0)

    def body2(i, carry):
        cs = (i, n_ch - 1 - i)
        ns = [dr * n_ch + c for dr, c in enumerate(cs)]
        x = [_dot_nt(tar_scr[n], s2.astype(BF16)) for n, s2 in zip(ns, carry)]
        u2 = [(xx[:rows2] + w2_scr[n]).astype(BF16) for n, xx in zip(ns, x)]
        upd = [_dot_tn(u, b2_scr[n]) for n, u in zip(ns, u2)]
        yb = [_dot(lrb_scr[n], u) for n, u in zip(ns, u2)]
        out = []
        for dr, (c, n) in enumerate(zip(cs, ns)):
            y2 = x[dr][rows2:] + yb[dr] + yl_scr[n]
            y_scr[dr, pl.ds(pl.multiple_of(c * c_len, c_len), c_len), :] = y2[:c_len] + y2[c_len:]
            out.append((carry[dr] + upd[dr] + kv_scr[n]) * pc_scr[n])
        return tuple(out)

    if zero_init:
        init = (jnp.zeros((rows2, LANES), F32),) * 2
    else:
        init = (s0_ref[0, 0, 0], s0_ref[0, 1, 0])
    s_f, s_b = lax.fori_loop(0, n_ch, body2, init)

    y = y_scr[0] + y_scr[1]
    yc = y - _head_sum(y, first) * (1.0 / RWKV_HD)
    o = yc * lax.rsqrt(_head_sum(yc * yc, first) * (1.0 / RWKV_HD) + GN_EPS) * gn_ref[...]
    r = rkv_ref[0]
    k = rkv_ref[1]
    v = rkv_ref[2]
    for dr in range(2):
        kd = k * (1.0 + (a_ref[dr] - 1.0) * kap)
        o = o + _head_sum(r * kd * rkp_ref[...], first) * v
    o_ref[...] = o

    if zero_init:
        for dr, s2 in enumerate((s_f, s_b)):
            st_ref[0, 0, dr, 0] = s2[:RWKV_HD, :RWKV_HD]
            st_ref[0, 0, dr, 1] = s2[RWKV_HD:, RWKV_HD:]


def _rwkv_chunked(rkvg, lw, a, kkp, kap, rkp, gn, s0_pairs):
    npair = RWKV_HEADS // 2
    par = lambda *_: pl.BlockSpec((1, LANES), lambda s, p: (0, p))

    def scratch(seq):
        n = 2 * (seq // RW_C)
        r2 = 2 * RW_C
        return [
            pltpu.VMEM((seq, LANES), F32), pltpu.VMEM((2, seq, LANES), F32),
            pltpu.VMEM((n, 2 * r2, LANES), BF16), pltpu.VMEM((n, r2, r2), BF16), pltpu.VMEM((n, r2, LANES), BF16),
            pltpu.VMEM((n, r2, LANES), F32), pltpu.VMEM((n, r2, LANES), F32), pltpu.VMEM((n, r2, LANES), F32),
            pltpu.VMEM((n, 1, LANES), F32),
        ]

    def seq_specs(seq, rb):
        return [
            pl.BlockSpec((3, seq, LANES), lambda s, p: (0, rb + s, p)),
            pl.BlockSpec((2, seq, LANES), lambda s, p: (0, rb + s, p)),
            pl.BlockSpec((2, seq, LANES), lambda s, p: (0, rb + s, p)),
            par(), par(), par(), par(),
        ]

    pars = [u.reshape(1, D_MODEL) for u in (kkp, kap, rkp, gn)]
    o_p, st = pl.pallas_call(
        functools.partial(_rwkv_chunk_kernel, seq=SEQ, zero_init=True),
        grid=(BATCH, npair),
        in_specs=seq_specs(SEQ, 0),
        out_specs=[
            pl.BlockSpec((SEQ, LANES), lambda s, p: (s, p)),
            pl.BlockSpec((1, 1, 2, 2, RWKV_HD, RWKV_HD), lambda s, p: (s, 0, 0, p, 0, 0)),
        ],
        out_shape=[
            jax.ShapeDtypeStruct((N_TOK, D_MODEL), F32),
            jax.ShapeDtypeStruct((BATCH, 1, 2, RWKV_HEADS, RWKV_HD, RWKV_HD), F32),
        ],
        scratch_shapes=scratch(SEQ),
        compiler_params=_params("arbitrary", "arbitrary"),
        name="rwkv_prompt",
    )(rkvg, lw, a, *pars)
    rb = N_PROMPT_TOK // DEC_SEQ
    o = pl.pallas_call(
        functools.partial(_rwkv_chunk_kernel, seq=DEC_SEQ, zero_init=False),
        grid=(DEC_BATCH, npair),
        in_specs=seq_specs(DEC_SEQ, rb) + [
            pl.BlockSpec((1, 2, 1, LANES, LANES), lambda s, p: (s, 0, p, 0, 0)),
            pl.BlockSpec(memory_space=pl.ANY),
        ],
        out_specs=pl.BlockSpec((DEC_SEQ, LANES), lambda s, p: (rb + s, p)),
        out_shape=jax.ShapeDtypeStruct((N_TOK, D_MODEL), F32),
        input_output_aliases={8: 0},
        scratch_shapes=scratch(DEC_SEQ),
        compiler_params=_params("arbitrary", "arbitrary"),
        name="rwkv_latent",
    )(rkvg, lw, a, *pars, s0_pairs, o_p)
    return o, st


def _split3(x):
    hi = x.astype(BF16)
    r1 = x - hi.astype(F32)
    mid = r1.astype(BF16)
    return hi, mid, (r1 - mid.astype(F32)).astype(BF16)


def _rwkv_kernel(*refs, seq, zero_init, pg, np2):
    n_in = 7 if zero_init else 9
    rkv_ref, lw_ref, a_ref, kkp_ref, kap_ref, rkp_ref, gn_ref = refs[:7]
    if zero_init:
        o_ref, st_ref = refs[n_in:n_in + 2]
        scr = refs[n_in + 2:]
    else:
        s0_ref = refs[7]
        o_ref = refs[n_in]
        scr = refs[n_in + 1:]
    kk_scr, cum_scr, bon_scr, y_scr, s_scr, tar_scr, lrb_scr, b2_scr, w2_scr, yl_scr, kv_scr, pc_scr = scr
    c_len = RW_C
    n_ch = seq // c_len
    rows2 = 2 * c_len
    grp = pl.program_id(1)
    defer = np2 > pg
    base = grp * pg if defer else 0
    first = _first_half_lanes()

    rr = lax.broadcasted_iota(jnp.int32, (rows2, rows2), 0)
    cc = lax.broadcasted_iota(jnp.int32, (rows2, rows2), 1)
    eye = (rr == cc).astype(F32)

    def same(shift):
        return (rr >> shift) == (cc >> shift)

    head = same(6)
    strict = (head & (cc < rr), head & (cc > rr))
    incl = (head & (cc <= rr), head & (cc >= rr))
    last = (c_len - 1, 0)
    head_ones = head.astype(BF16)

    cs_rows = min(seq, 256)
    tr = lax.broadcasted_iota(jnp.int32, (cs_rows, cs_rows), 0)
    tc = lax.broadcasted_iota(jnp.int32, (cs_rows, cs_rows), 1)
    chunk = (tr >> 6) == (tc >> 6)
    tri = ((chunk & (tc <= tr)).astype(BF16), (chunk & (tc >= tr)).astype(BF16))
    for p in range(pg):
        ln = slice(p * LANES, (p + 1) * LANES)
        kap = kap_ref[:, ln]
        r = rkv_ref[0, :, ln]
        k = rkv_ref[1, :, ln]
        v = rkv_ref[2, :, ln]
        kk = k * kkp_ref[:, ln]
        kk_scr[p] = kk * lax.rsqrt(jnp.maximum(_head_sum(kk * kk, first), 1e-12))
        bonus = None
        for dr in range(2):
            kd = k * (1.0 + (a_ref[dr, :, ln] - 1.0) * kap)
            term = _head_sum(r * kd * rkp_ref[:, ln], first) * v
            bonus = term if bonus is None else bonus + term
        bon_scr[base + p] = bonus
    for p in range(0, pg, 2):
        for dr in range(2):
            for r0 in range(0, seq, cs_rows):
                parts = _split3(lw_ref[dr, r0:r0 + cs_rows, p * LANES:(p + 2) * LANES])
                cum = _dot(tri[dr], parts[0]) + _dot(tri[dr], parts[1]) + _dot(tri[dr], parts[2])
                cum_scr[p, dr, r0:r0 + cs_rows, :] = cum[:, :LANES]
                cum_scr[p + 1, dr, r0:r0 + cs_rows, :] = cum[:, LANES:]

    def phase1(chains):
        dirs = [dr for _, dr, _ in chains]
        a2, r2, b2, k2, v2, pc = [], [], [], [], [], []
        for p, dr, c in chains:
            ln = slice(p * LANES, (p + 1) * LANES)
            rw = pl.ds(pl.multiple_of(c * c_len, c_len), c_len)
            a = a_ref[dr, rw, ln]
            k = rkv_ref[1, rw, ln]
            kk_c = kk_scr[p, rw, :]
            cum_c = cum_scr[p, dr, rw, :]
            e_inc = jnp.exp(cum_c)
            e_inv = jnp.exp(-cum_c)
            a2.append(_stack_heads(-kk_c * jnp.exp(cum_c - lw_ref[dr, rw, ln]), first).astype(BF16))
            r2.append(_stack_heads(rkv_ref[0, rw, ln] * e_inc, first).astype(BF16))
            b2.append(_stack_heads(kk_c * a * e_inv, first).astype(BF16))
            k2.append(_stack_heads(k * (1.0 + (a - 1.0) * kap_ref[:, ln]) * e_inv, first).astype(BF16))
            v2.append(_stack_heads(rkv_ref[2, rw, ln], first).astype(BF16))
            pc.append(e_inc[last[dr]:last[dr] + 1, :])
        g = [_dot_nt(jnp.concatenate([x, y], axis=0), jnp.concatenate([z, w], axis=0))
             for x, y, z, w in zip(a2, r2, b2, k2)]
        l_ab = [jnp.where(strict[dr], x[:rows2, :rows2], 0.0) for dr, x in zip(dirs, g)]
        t = [eye + jnp.where(same(1), x, 0.0) for x in l_ab]
        side = {}
        for shift in range(1, 6):
            sib = same(shift + 1) & ~same(shift)
            tb = [x.astype(BF16) for x in t]
            mid = [_dot(jnp.where(sib, x, 0.0).astype(BF16), y) for x, y in zip(l_ab, tb)]
            if shift == 1:
                side['lv'] = [_dot(jnp.where(strict[dr], x[:rows2, rows2:], 0.0).astype(BF16), y)
                              for dr, x, y in zip(dirs, g, v2)]
            elif shift == 2:
                side['yl'] = [_dot(jnp.where(incl[dr], x[rows2:, rows2:], 0.0).astype(BF16), y)
                              for dr, x, y in zip(dirs, g, v2)]
            elif shift == 3:
                side['kv'] = [_dot_tn(x, y) for x, y in zip(v2, k2)]
            t = [x + _dot(y, z.astype(BF16)) for x, y, z in zip(t, tb, mid)]
        tb = [x.astype(BF16) for x in t]
        ta = [_dot(x, y) for x, y in zip(tb, a2)]
        w2 = [_dot(x, y.astype(BF16)) for x, y in zip(tb, side['lv'])]
        for i, (p, dr, c) in enumerate(chains):
            n = ((base + p) * 2 + dr) * n_ch + c
            tar_scr[n, :rows2, :] = ta[i].astype(BF16)
            tar_scr[n, rows2:, :] = r2[i]
            w2_scr[n] = w2[i]
            yl_scr[n] = side['yl'][i]
            kv_scr[n] = side['kv'][i]
            lrb_scr[n] = jnp.where(incl[dr], g[i][rows2:, :rows2], 0.0).astype(BF16)
            b2_scr[n] = b2[i]
            pc_scr[n] = pc[i]

    def body1(cg, carry):
        phase1([(p, dr, cg * RW_LOCK + j) for p in range(pg) for j in range(RW_LOCK) for dr in range(2)])
        return carry

    if n_ch == RW_LOCK:
        body1(0, 0)
    else:
        lax.fori_loop(0, n_ch // RW_LOCK, body1, 0)

    def finish():
        for p in range(np2):
            for dr in range(2):
                if zero_init:
                    s_scr[2 * p + dr] = jnp.zeros((rows2, LANES), F32)
                else:
                    s_scr[2 * p + dr] = s0_ref[0, dr, p]

        def body2(i, carry):
            cs = (i, n_ch - 1 - i)
            ids = [(p, dr) for p in range(np2) for dr in range(2)]
            ns = [(p * 2 + dr) * n_ch + cs[dr] for p, dr in ids]
            x = [_dot_nt(tar_scr[n], s_scr[2 * p + dr].astype(BF16)) for n, (p, dr) in zip(ns, ids)]
            u2 = [(xx[:rows2] + w2_scr[n]).astype(BF16) for n, xx in zip(ns, x)]
            upd = [_dot_tn(u, b2_scr[n]) for n, u in zip(ns, u2)]
            yb = [_dot(lrb_scr[n], u) for n, u in zip(ns, u2)]
            for j, (n, (p, dr)) in enumerate(zip(ns, ids)):
                y2 = x[j][rows2:] + yb[j] + yl_scr[n]
                y_scr[p, dr, pl.ds(pl.multiple_of(cs[dr] * c_len, c_len), c_len), :] = y2[:c_len] + y2[c_len:]
                s_scr[2 * p + dr] = (s_scr[2 * p + dr] + upd[j] + kv_scr[n]) * pc_scr[n]
            return carry

        lax.fori_loop(0, n_ch, body2, 0)

        def head_mean(xs):
            parts = [_split3(x) for x in xs]
            return [(_dot(a, head_ones) + _dot(b, head_ones) + _dot(c, head_ones)) * (1.0 / RWKV_HD)
                    for a, b, c in parts]

        ys = [y_scr[p, 0] + y_scr[p, 1] for p in range(np2)]
        yc = [y - m for y, m in zip(ys, head_mean(ys))]
        var = head_mean([x * x for x in yc])
        for p in range(np2):
            ln = slice(p * LANES, (p + 1) * LANES)
            o_ref[:, ln] = yc[p] * lax.rsqrt(var[p] + GN_EPS) * gn_ref[:, ln] + bon_scr[p]
            if zero_init:
                for dr in range(2):
                    s2 = s_scr[2 * p + dr]
                    st_ref[0, 0, dr, 2 * p] = s2[:RWKV_HD, :RWKV_HD]
                    st_ref[0, 0, dr, 2 * p + 1] = s2[RWKV_HD:, RWKV_HD:]

    if defer:
        pl.when(grp == pl.num_programs(1) - 1)(finish)
    else:
        finish()


def _rwkv_mixer(rkvg, lw, a, kkp, kap, rkp, gn, s0_pairs):
    npair = RWKV_HEADS // 2
    pg = RW_PAIRS
    wl = pg * LANES
    r2 = 2 * RW_C

    def scratch(seq, np2):
        n = 2 * np2 * (seq // RW_C)
        return [
            pltpu.VMEM((pg, seq, LANES), F32), pltpu.VMEM((pg, 2, seq, LANES), F32),
            pltpu.VMEM((np2, seq, LANES), F32), pltpu.VMEM((np2, 2, seq, LANES), F32),
            pltpu.VMEM((2 * np2, r2, LANES), F32),
            pltpu.VMEM((n, 2 * r2, LANES), BF16), pltpu.VMEM((n, r2, r2), BF16), pltpu.VMEM((n, r2, LANES), BF16),
            pltpu.VMEM((n, r2, LANES), F32), pltpu.VMEM((n, r2, LANES), F32), pltpu.VMEM((n, r2, LANES), F32),
            pltpu.VMEM((n, 1, LANES), F32),
        ]

    def seq_specs(seq, rb, gn_spec):
        par = pl.BlockSpec((1, wl), lambda s, g: (0, g))
        return [
            pl.BlockSpec((3, seq, wl), lambda s, g: (0, rb + s, g)),
            pl.BlockSpec((2, seq, wl), lambda s, g: (0, rb + s, g)),
            pl.BlockSpec((2, seq, wl), lambda s, g: (0, rb + s, g)),
            par, par, par, gn_spec,
        ]

    pars = [u.reshape(1, D_MODEL) for u in (kkp, kap, rkp, gn)]
    o_p, st = pl.pallas_call(
        functools.partial(_rwkv_kernel, seq=SEQ, zero_init=True, pg=pg, np2=npair),
        grid=(BATCH, npair // pg),
        in_specs=seq_specs(SEQ, 0, pl.BlockSpec((1, D_MODEL), lambda s, g: (0, 0))),
        out_specs=[
            pl.BlockSpec((SEQ, D_MODEL), lambda s, g: (s, 0)),
            pl.BlockSpec((1, 1, 2, RWKV_HEADS, RWKV_HD, RWKV_HD), lambda s, g: (s, 0, 0, 0, 0, 0)),
        ],
        out_shape=[
            jax.ShapeDtypeStruct((N_TOK, D_MODEL), F32),
            jax.ShapeDtypeStruct((BATCH, 1, 2, RWKV_HEADS, RWKV_HD, RWKV_HD), F32),
        ],
        scratch_shapes=scratch(SEQ, npair),
        compiler_params=_params("arbitrary", "arbitrary"),
        name="rwkv_prompt",
    )(rkvg, lw, a, *pars)
    rb = N_PROMPT_TOK // DEC_SEQ
    o = pl.pallas_call(
        functools.partial(_rwkv_kernel, seq=DEC_SEQ, zero_init=False, pg=pg, np2=pg),
        grid=(DEC_BATCH, npair // pg),
        in_specs=seq_specs(DEC_SEQ, rb, pl.BlockSpec((1, wl), lambda s, g: (0, g))) + [
            pl.BlockSpec((1, 2, pg, LANES, LANES), lambda s, g: (s, 0, g, 0, 0)),
            pl.BlockSpec(memory_space=pl.ANY),
        ],
        out_specs=pl.BlockSpec((DEC_SEQ, wl), lambda s, g: (rb + s, g)),
        out_shape=jax.ShapeDtypeStruct((N_TOK, D_MODEL), F32),
        input_output_aliases={8: 0},
        scratch_shapes=scratch(DEC_SEQ, pg),
        compiler_params=_params("arbitrary", "arbitrary"),
        name="rwkv_latent",
    )(rkvg, lw, a, *pars, s0_pairs, o_p)
    return o, st


def _state_pairs(s0):
    s = s0.reshape(DEC_BATCH, 2, RWKV_HEADS // 2, 2, RWKV_HD, RWKV_HD)
    z = jnp.zeros_like(s[:, :, :, 0])
    top = jnp.concatenate([s[:, :, :, 0], z], axis=-1)
    bot = jnp.concatenate([z, s[:, :, :, 1]], axis=-1)
    return jnp.concatenate([top, bot], axis=-2)


def _layer_rwkv(x, p, mod, j):
    i = N_MIXERS * j + 1
    wa, wb, aa, ab = p['rwkv_wA'][j], p['rwkv_wB'][j], p['rwkv_aA'][j], p['rwkv_aB'][j]
    z = jnp.zeros_like(wb[0])
    wa2 = jnp.concatenate([wa[0], wa[1]], axis=1).astype(BF16)
    aa2 = jnp.concatenate([aa[0], aa[1]], axis=1).astype(BF16)
    wb_pad = jnp.stack([jnp.concatenate([wb[0], z]), jnp.concatenate([z, wb[1]])]).astype(BF16)
    ab_pad = jnp.stack([jnp.concatenate([ab[0], z]), jnp.concatenate([z, ab[1]])]).astype(BF16)
    xm, lw, a = _rwkv_prep(x, p['norm_w'][i], mod, p['rwkv_mu'][j], wa2, aa2, wb_pad, ab_pad,
                           p['rwkv_w0'][j], p['rwkv_a0'][j])
    rkvg = _rwkv_rkvg(xm, p['rwkv_w_in'][j].astype(BF16))
    o, st = _rwkv_mixer(rkvg, lw, a, p['rwkv_kk'][j], p['rwkv_ka'][j], p['rwkv_rk'][j], p['rwkv_gn'][j],
                          _state_pairs(p['state_rwkv'][:, j]))
    x = _out_proj(o, rkvg.reshape(4 * N_TOK, D_MODEL), 0, p['rwkv_w_out'][j], x, mod, p['final_norm_w'], False,
                  g_row0=3 * N_TOK)
    return x, st


DIFF_W = 2 * DIFF_HD
ATT_QB = 256
DIFF_GROUP = 4


def _first_half_lanes():
    return lax.broadcasted_iota(jnp.int32, (1, LANES), 1) < LANES // 2


def _diff_lambda(lam_ref, lam_init):
    lp = lam_ref[...]
    return (jnp.exp(jnp.sum(lp[0:1] * lp[1:2], keepdims=True))
            - jnp.exp(jnp.sum(lp[2:3] * lp[3:4], keepdims=True)) + lam_init)


def _diff_heads(items, lam, lam_init):
    first = _first_half_lanes()
    scale = DIFF_HD ** -0.5
    sub = [(q, keys, comp) for q, keys, _ in items for comp in range(2)]
    qm = [jnp.where(first if comp == 0 else ~first, q, 0.0).astype(BF16) for q, _, comp in sub]
    s = [[_dot_nt(x, kb) * scale for kb, _ in keys] for x, (_, keys, _) in zip(qm, sub)]
    m = [functools.reduce(jnp.maximum, [jnp.max(u, axis=-1, keepdims=True) for u in ss]) for ss in s]
    e = [[jnp.exp(u - mm) for u in ss] for ss, mm in zip(s, m)]
    inv = [1.0 / functools.reduce(lambda x, y: x + y, [jnp.sum(u, axis=-1, keepdims=True) for u in ee]) for ee in e]
    outs = []
    for i, (_, keys, gn) in enumerate(items):
        o = None
        for n, (_, vb) in enumerate(keys):
            p = e[2 * i][n] * inv[2 * i] - lam * (e[2 * i + 1][n] * inv[2 * i + 1])
            part = _dot(p.astype(BF16), vb)
            o = part if o is None else o + part
        outs.append(o)
    return [o * lax.rsqrt(jnp.mean(o * o, axis=-1, keepdims=True) + EPS) * gn * (1.0 - lam_init)
            for o, (_, _, gn) in zip(outs, items)]


def _diff_prompt_kernel(lam_ref, q_ref, k_ref, v_ref, gn_ref, o_ref, ck_ref, cv_ref, *, lam_init):
    lam = _diff_lambda(lam_ref, lam_init)
    for h0 in range(0, DIFF_HEADS, DIFF_GROUP):
        items = []
        for h in range(h0, h0 + DIFF_GROUP):
            sl = slice(h * DIFF_W, (h + 1) * DIFF_W)
            k = k_ref[:, sl]
            v = v_ref[:, sl]
            ck_ref[0, 0, h] = k
            cv_ref[0, 0, h] = v
            items.append((q_ref[:, sl], [(k.astype(BF16), v.astype(BF16))], gn_ref[:, sl]))
        for h, o in zip(range(h0, h0 + DIFF_GROUP), _diff_heads(items, lam, lam_init)):
            o_ref[:, h * DIFF_W:(h + 1) * DIFF_W] = o


def _diff_latent_kernel(lam_ref, q_ref, k_ref, v_ref, ck_ref, cv_ref, cos_ref, slo_ref, shi_ref, gn_ref,
                        _prev_ref, o_ref, *, lam_init):
    lam = _diff_lambda(lam_ref, lam_init)
    tabs = (cos_ref[...], slo_ref[...], shi_ref[...])
    q = _rope(q_ref[...], *tabs, DIFF_HD // 4)
    k = _rope(k_ref[...], *tabs, DIFF_HD // 4)
    keys = [(k.astype(BF16), v_ref[...].astype(BF16)),
            (ck_ref[0, 0, 0].astype(BF16), cv_ref[0, 0, 0].astype(BF16))]
    gn = gn_ref[...]
    n_blk = DEC_SEQ // ATT_QB
    items = [(q[qi * ATT_QB:(qi + 1) * ATT_QB], keys, gn) for qi in range(n_blk)]
    for qi, o in enumerate(_diff_heads(items, lam, lam_init)):
        o_ref[qi * ATT_QB:(qi + 1) * ATT_QB, :] = o


def _diff_attention(proj, lam_p, gn_w, cache_k, cache_v, j, lam_init):
    gn = gn_w.reshape(1, D_MODEL)
    lam_spec = pl.BlockSpec((4, DIFF_HD), lambda *_: (0, 0))
    cache_out = pl.BlockSpec((1, 1, DIFF_HEADS, SEQ, DIFF_W), lambda b: (b, 0, 0, 0, 0))
    cache_shape = jax.ShapeDtypeStruct((BATCH, 1, DIFF_HEADS, SEQ, DIFF_W), F32)
    o_p, new_k, new_v = pl.pallas_call(
        functools.partial(_diff_prompt_kernel, lam_init=lam_init),
        grid=(BATCH,),
        in_specs=[
            lam_spec,
            pl.BlockSpec((SEQ, D_MODEL), lambda b: (b, 0)),
            pl.BlockSpec((SEQ, D_MODEL), lambda b: (b, 1)),
            pl.BlockSpec((SEQ, D_MODEL), lambda b: (b, 2)),
            pl.BlockSpec((1, D_MODEL), lambda b: (0, 0)),
        ],
        out_specs=[pl.BlockSpec((SEQ, D_MODEL), lambda b: (b, 0)), cache_out, cache_out],
        out_shape=[jax.ShapeDtypeStruct((N_TOK, D_MODEL), F32), cache_shape, cache_shape],
        compiler_params=_params("arbitrary"),
        name="diff_prompt",
    )(lam_p, proj, proj, proj, gn)
    cos, slo, shi = (jnp.concatenate([u, u], axis=-1) for u in _rope_tables(DIFF_HD))
    rb = N_PROMPT_TOK // DEC_SEQ
    nh = DIFF_HEADS
    tab = pl.BlockSpec((DEC_SEQ, DIFF_W), lambda b, h: (0, 0))
    cache = pl.BlockSpec((1, 1, 1, PAST_LEN, DIFF_W), lambda b, h: (b, j, h, 0, 0))
    o = pl.pallas_call(
        functools.partial(_diff_latent_kernel, lam_init=lam_init),
        grid=(DEC_BATCH, nh),
        in_specs=[
            lam_spec,
            pl.BlockSpec((DEC_SEQ, DIFF_W), lambda b, h: (rb + b, h)),
            pl.BlockSpec((DEC_SEQ, DIFF_W), lambda b, h: (rb + b, nh + h)),
            pl.BlockSpec((DEC_SEQ, DIFF_W), lambda b, h: (rb + b, 2 * nh + h)),
            cache, cache, tab, tab, tab,
            pl.BlockSpec((1, DIFF_W), lambda b, h: (0, h)),
            pl.BlockSpec(memory_space=pl.ANY),
        ],
        out_specs=pl.BlockSpec((DEC_SEQ, DIFF_W), lambda b, h: (rb + b, h)),
        out_shape=jax.ShapeDtypeStruct((N_TOK, D_MODEL), F32),
        input_output_aliases={10: 0},
        compiler_params=_params("arbitrary", "arbitrary"),
        name="diff_latent",
    )(lam_p, proj, proj, proj, cache_k, cache_v, cos, slo, shi, gn, o_p)
    return o, new_k, new_v


def _layer_diff(x, p, mod, j, i):
    lam_init = 0.8 - 0.6 * math.exp(-0.3 * i)
    proj = _in_proj(x, p['norm_w'][i], mod, p['diff_w_in'][j], 1024)
    o, new_k, new_v = _diff_attention(proj, p['diff_lambda'][j], p['diff_gn'][j], p['cache_diff_k'],
                                      p['cache_diff_v'], j, lam_init)
    x = _out_proj(o, proj, 3, p['diff_w_out'][j], x, mod, p['final_norm_w'], False)
    return x, new_k, new_v


NA_ROWS = DEC_SEQ // GRID_W
NA_WR = min(NA_WIN_R, NA_ROWS)
NA_LOC = NA_WR * GRID_W
NA_ROW_GROUP = 4
NA_PAIR_GROUP = 2


def _na_prompt_kernel(q_ref, k_ref, v_ref, o_ref, ck_ref, cv_ref):
    first = _first_half_lanes()
    scale = NA_HD ** -0.5
    for p0 in range(0, NA_HEADS // 2, NA_PAIR_GROUP):
        pairs = range(p0, p0 + NA_PAIR_GROUP)
        kb, vb = [], []
        for pr in pairs:
            sl = slice(pr * LANES, (pr + 1) * LANES)
            k = k_ref[:, sl]
            v = v_ref[:, sl]
            for half in range(2):
                ck_ref[0, 0, 2 * pr + half] = k[:, half * NA_HD:(half + 1) * NA_HD]
                cv_ref[0, 0, 2 * pr + half] = v[:, half * NA_HD:(half + 1) * NA_HD]
            kb.append(k.astype(BF16))
            vb.append(v.astype(BF16))
        items = [(i, half) for i in range(NA_PAIR_GROUP) for half in range(2)]
        qm = [jnp.where(first if half == 0 else ~first, q_ref[:, (p0 + i) * LANES:(p0 + i + 1) * LANES], 0.0)
              .astype(BF16) for i, half in items]
        s = [_dot_nt(x, kb[i]) * scale for x, (i, _) in zip(qm, items)]
        e = [jnp.exp(x - jnp.max(x, axis=-1, keepdims=True)) for x in s]
        inv = [1.0 / jnp.sum(x, axis=-1, keepdims=True) for x in e]
        outs = [_dot(x.astype(BF16), vb[i]) * z for x, z, (i, _) in zip(e, inv, items)]
        for i in range(NA_PAIR_GROUP):
            o_ref[:, (p0 + i) * LANES:(p0 + i + 1) * LANES] = jnp.where(first, outs[2 * i], outs[2 * i + 1])


def _na_latent_kernel(q_ref, k_ref, v_ref, kc_ref, vc_ref, tab_ref, _prev_ref, o_ref):
    first = _first_half_lanes()
    scale = NA_HD ** -0.5
    kb = k_ref[...].astype(BF16)
    vb = v_ref[...].astype(BF16)
    kcb = kc_ref[0, 0].astype(BF16)
    vcb = vc_ref[0, 0].astype(BF16)
    qcol = lax.broadcasted_iota(jnp.int32, (GRID_W, NA_LOC), 0)
    kcol = lax.broadcasted_iota(jnp.int32, (GRID_W, NA_LOC), 1) & (GRID_W - 1)
    cstart = jnp.clip(qcol - NA_WIN_C // 2, 0, GRID_W - NA_WIN_C)
    col_ok = (kcol >= cstart) & (kcol < cstart + NA_WIN_C)
    def bias_of(r, rs, half):
        parts = []
        for w in range(0, NA_WR, 2):
            src = jnp.broadcast_to(tab_ref[half, rs + w - r + NA_WIN_R - 1], (GRID_W, LANES))
            parts.append(pltpu.roll(src, LANES - (NA_WIN_C - 1), axis=1, stride=1, stride_axis=0))
        return jnp.concatenate(parts, axis=1)

    for r0 in range(0, NA_ROWS, NA_ROW_GROUP):
        items = [(r, min(max(r - NA_WR // 2, 0), NA_ROWS - NA_WR), half)
                 for r in range(r0, r0 + NA_ROW_GROUP) for half in range(2)]
        qm = [jnp.where(first if half == 0 else ~first, q_ref[r * GRID_W:(r + 1) * GRID_W, :], 0.0).astype(BF16)
              for r, _, half in items]
        s_loc = [_dot_nt(x, kb[rs * GRID_W:(rs + NA_WR) * GRID_W]) for x, (_, rs, _) in zip(qm, items)]
        s_ctx = [_dot_nt(x, kcb) * scale for x in qm]
        s_loc = [jnp.where(col_ok, x * scale + bias_of(*it), -jnp.inf) for x, it in zip(s_loc, items)]
        m = [jnp.maximum(jnp.max(x, axis=-1, keepdims=True), jnp.max(y, axis=-1, keepdims=True))
             for x, y in zip(s_loc, s_ctx)]
        e_loc = [jnp.exp(x - mm) for x, mm in zip(s_loc, m)]
        e_ctx = [jnp.exp(x - mm) for x, mm in zip(s_ctx, m)]
        inv = [1.0 / (jnp.sum(x, axis=-1, keepdims=True) + jnp.sum(y, axis=-1, keepdims=True))
               for x, y in zip(e_loc, e_ctx)]
        pv = [_dot(x.astype(BF16), vb[rs * GRID_W:(rs + NA_WR) * GRID_W]) for x, (_, rs, _) in zip(e_loc, items)]
        pc = [_dot(x.astype(BF16), vcb) for x in e_ctx]
        outs = [(x + y) * z for x, y, z in zip(pv, pc, inv)]
        for n in range(0, len(items), 2):
            r = items[n][0]
            o_ref[r * GRID_W:(r + 1) * GRID_W, :] = jnp.where(first, outs[n], outs[n + 1])


def _na_bias_pairs(table):
    t = table.astype(F32)
    nc = 2 * NA_WIN_C - 1
    z = jnp.zeros(t[:, :-1].shape[:2] + (GRID_W - nc,), F32)
    return jnp.concatenate([t[:, :-1], z, t[:, 1:], z], axis=-1)[:, :, None, :]


def _pair_heads(cache):
    c = cache.reshape(DEC_BATCH, NA_HEADS // 2, 2, PAST_LEN, NA_HD)
    return c.transpose(0, 1, 3, 2, 4).reshape(DEC_BATCH, NA_HEADS // 2, PAST_LEN, LANES)


def _na_attention(proj, bias_table, cache_k, cache_v):
    cache_out = pl.BlockSpec((1, 1, NA_HEADS, SEQ, NA_HD), lambda b: (b, 0, 0, 0, 0))
    cache_shape = jax.ShapeDtypeStruct((BATCH, 1, NA_HEADS, SEQ, NA_HD), F32)
    o_p, new_k, new_v = pl.pallas_call(
        _na_prompt_kernel,
        grid=(BATCH,),
        in_specs=[
            pl.BlockSpec((SEQ, D_MODEL), lambda b: (b, 0)),
            pl.BlockSpec((SEQ, D_MODEL), lambda b: (b, 1)),
            pl.BlockSpec((SEQ, D_MODEL), lambda b: (b, 2)),
        ],
        out_specs=[pl.BlockSpec((SEQ, D_MODEL), lambda b: (b, 0)), cache_out, cache_out],
        out_shape=[jax.ShapeDtypeStruct((N_TOK, D_MODEL), F32), cache_shape, cache_shape],
        compiler_params=_params("arbitrary"),
        name="na_prompt",
    )(proj, proj, proj)
    rb = N_PROMPT_TOK // DEC_SEQ
    npair = NA_HEADS // 2
    cache = pl.BlockSpec((1, 1, PAST_LEN, LANES), lambda pr, b: (b, pr, 0, 0))
    o = pl.pallas_call(
        _na_latent_kernel,
        grid=(npair, DEC_BATCH),
        in_specs=[
            pl.BlockSpec((DEC_SEQ, LANES), lambda pr, b: (rb + b, pr)),
            pl.BlockSpec((DEC_SEQ, LANES), lambda pr, b: (rb + b, npair + pr)),
            pl.BlockSpec((DEC_SEQ, LANES), lambda pr, b: (rb + b, 2 * npair + pr)),
            cache, cache,
            pl.BlockSpec((2, 2 * NA_WIN_R - 2, 1, LANES), lambda pr, b: (pr, 0, 0, 0)),
            pl.BlockSpec(memory_space=pl.ANY),
        ],
        out_specs=pl.BlockSpec((DEC_SEQ, LANES), lambda pr, b: (rb + b, pr)),
        out_shape=jax.ShapeDtypeStruct((N_TOK, D_MODEL), F32),
        input_output_aliases={6: 0},
        compiler_params=_params("arbitrary", "arbitrary"),
        name="na_latent",
    )(proj, proj, proj, _pair_heads(cache_k), _pair_heads(cache_v), _na_bias_pairs(bias_table), o_p)
    return o, new_k, new_v


def _layer_na(x, p, mod, j, final):
    i = N_MIXERS * j + 3
    proj = _in_proj(x, p['norm_w'][i], mod, p['na_w_in'][j], 1024)
    o, new_k, new_v = _na_attention(proj, p['na_bias'][j], p['cache_na_k'][:, j], p['cache_na_v'][:, j])
    args = (o, proj, 3, p['na_w_out'][j], x, mod, p['final_norm_w'])
    if final:
        x = (_out_proj(*args, True, rows=(0, N_PROMPT_TOK)), _out_proj(*args, True, rows=(N_PROMPT_TOK, N_TOK)))
    else:
        x = _out_proj(*args, False)
    return x, new_k, new_v


def kernel(x_prompt, x_sample, state_ret, state_rwkv, cache_diff_k, cache_diff_v, cache_na_k, cache_na_v,
           c, c_ctx, norm_w, w_mod, b_mod, final_norm_w,
           ret_w_in, ret_decay, ret_gn, ret_w_out,
           rwkv_mu, rwkv_w_in, rwkv_w0, rwkv_wA, rwkv_wB, rwkv_a0, rwkv_aA, rwkv_aB,
           rwkv_kk, rwkv_ka, rwkv_rk, rwkv_gn, rwkv_w_out,
           diff_w_in, diff_lambda, diff_gn, diff_w_out,
           na_w_in, na_bias, na_w_out):
    p = dict(locals())
    cond = jnp.zeros((N_COND, D_MODEL), F32).at[0].set(c_ctx).at[1:1 + DEC_BATCH].set(c)
    mods = _modulation(cond, w_mod, b_mod)
    x = jnp.concatenate([x_prompt.reshape(N_PROMPT_TOK, D_MODEL), x_sample.reshape(N_SAMPLE_TOK, D_MODEL)])
    new = {n: [] for n in ('ret', 'rwkv', 'dk', 'dv', 'nk', 'nv')}
    for i in range(DEPTH):
        kind, j = i % N_MIXERS, i // N_MIXERS
        if kind == 0:
            x, st = _layer_ret(x, p, mods[i], j)
            new['ret'].append(st)
        elif kind == 1:
            x, st = _layer_rwkv(x, p, mods[i], j)
            new['rwkv'].append(st)
        elif kind == 2:
            x, ck, cv = _layer_diff(x, p, mods[i], j, i)
            new['dk'].append(ck)
            new['dv'].append(cv)
        else:
            x, ck, cv = _layer_na(x, p, mods[i], j, final=(i == DEPTH - 1))
            new['nk'].append(ck)
            new['nv'].append(cv)
    if DEPTH % N_MIXERS:
        raise NotImplementedError("the final norm is fused into the last neighbourhood-attention layer")
    cat = lambda xs: xs[0] if len(xs) == 1 else jnp.concatenate(xs, axis=1)
    return (x[0].reshape(BATCH, SEQ, D_MODEL), x[1].reshape(DEC_BATCH, DEC_SEQ, D_MODEL),
            cat(new['ret']), cat(new['rwkv']), cat(new['dk']), cat(new['dv']), cat(new['nk']), cat(new['nv']))
```

```python
import functools
import math

import jax
import jax.numpy as jnp
from jax import lax
from jax.experimental import pallas as pl
from jax.experimental.pallas import tpu as pltpu

F32 = jnp.float32
BF16 = jnp.bfloat16

D_MODEL = 1024
BATCH = 32
SEQ = 256
DEPTH = 4
N_MIXERS = 4
DEC_BATCH = 2
DEC_SEQ = 1024
PAST_LEN = 256
GRID_W = 64

RET_HEADS = 4
RET_DK = 256
RET_DV = 512
RET_QK = 1024
RET_V = 2048

RWKV_HD = 64
RWKV_HEADS = 16
RWKV_RANK = 64

DIFF_HEADS = 8
DIFF_HD = 64

NA_HEADS = 16
NA_HD = 64
NA_WIN_R = 8
NA_WIN_C = 16

ROPE_BASE = 10000.0
EPS = 1e-6
GN_EPS = 1e-5

N_PROMPT_TOK = BATCH * SEQ
N_SAMPLE_TOK = DEC_BATCH * DEC_SEQ
N_TOK = N_PROMPT_TOK + N_SAMPLE_TOK
N_COND = 8

LANES = 128
VMEM_LIMIT = 56 * 2 ** 20


def _params(*sem):
    return pltpu.CompilerParams(dimension_semantics=sem, vmem_limit_bytes=VMEM_LIMIT)


def _cond_of_tile(i, tm):
    npt = N_PROMPT_TOK // tm
    return jnp.where(i < npt, 0, 1 + (i - npt) // (DEC_SEQ // tm))


def _sigmoid(x):
    return 1.0 / (1.0 + jnp.exp(-x))


def _silu(x):
    return x * _sigmoid(x)


def _dot(a, b):
    return jnp.dot(a, b, preferred_element_type=F32)


def _dot_nt(a, b):
    return lax.dot_general(a, b, (((1,), (1,)), ((), ())), preferred_element_type=F32)


def _dot_tn(a, b):
    return lax.dot_general(a, b, (((0,), (0,)), ((), ())), preferred_element_type=F32)


def _softmax_rows(s):
    m = jnp.max(s, axis=-1, keepdims=True)
    e = jnp.exp(s - m)
    return e / jnp.sum(e, axis=-1, keepdims=True)


def _mod_kernel(c_ref, w_ref, b_ref, o_ref):
    s = _silu(c_ref[...])
    o_ref[0] = jnp.dot(s, w_ref[0], precision=lax.Precision.HIGHEST, preferred_element_type=F32) + b_ref[0]


def _modulation(cond, w_mod, b_mod):
    tn = D_MODEL
    out = pl.pallas_call(
        _mod_kernel,
        grid=(DEPTH, 3 * D_MODEL // tn),
        in_specs=[
            pl.BlockSpec((N_COND, D_MODEL), lambda l, j: (0, 0)),
            pl.BlockSpec((1, D_MODEL, tn), lambda l, j: (l, 0, j)),
            pl.BlockSpec((1, 1, tn), lambda l, j: (l, 0, j)),
        ],
        out_specs=pl.BlockSpec((1, N_COND, tn), lambda l, j: (l, 0, j)),
        out_shape=jax.ShapeDtypeStruct((DEPTH, N_COND, 3 * D_MODEL), F32),
        compiler_params=_params("arbitrary", "arbitrary"),
        name="modulation",
    )(cond, w_mod, b_mod.reshape(DEPTH, 1, 3 * D_MODEL))
    return out.reshape(DEPTH, N_COND, 3, 1, D_MODEL)


def _norm_mod(x, nw, mod_ref):
    ms = jnp.mean(x * x, axis=-1, keepdims=True)
    y = x * lax.rsqrt(ms + EPS) * nw
    return y * (1.0 + mod_ref[0, 1]) + mod_ref[0, 0]


IN_TM = 1024


def _x_specs(x, tm, tile_of):
    if not isinstance(x, tuple):
        return [pl.BlockSpec((tm, D_MODEL), lambda *g: (tile_of(*g), 0))], (x,)
    npt = N_PROMPT_TOK // tm
    return [pl.BlockSpec((tm, D_MODEL), lambda *g: (jnp.minimum(tile_of(*g), npt - 1), 0)),
            pl.BlockSpec((tm, D_MODEL), lambda *g: (jnp.maximum(tile_of(*g) - npt, 0), 0))], x


def _read_x(x_refs, tile, tm):
    if len(x_refs) == 1:
        return x_refs[0][...]
    return jnp.where(tile < N_PROMPT_TOK // tm, x_refs[0][...], x_refs[1][...])


def _in_proj_kernel(*refs, n_x):
    x_refs = refs[:n_x]
    nw_ref, mod_ref, w_ref, o_ref, h_ref = refs[n_x:]

    @pl.when(pl.program_id(1) == 0)
    def _():
        x = _read_x(x_refs, pl.program_id(0), IN_TM)
        h_ref[...] = _norm_mod(x, nw_ref[...], mod_ref).astype(BF16)

    o_ref[...] = _dot(h_ref[...], w_ref[...]).astype(o_ref.dtype)


def _in_proj(x, norm_w, mod, w, tn, out_dtype=F32):
    n = w.shape[1]
    w = w.astype(BF16)
    x_specs, xs = _x_specs(x, IN_TM, lambda i, j: i)
    return pl.pallas_call(
        functools.partial(_in_proj_kernel, n_x=len(xs)),
        grid=(N_TOK // IN_TM, n // tn),
        in_specs=x_specs + [
            pl.BlockSpec((1, D_MODEL), lambda i, j: (0, 0)),
            pl.BlockSpec((1, 3, 1, D_MODEL), lambda i, j: (_cond_of_tile(i, IN_TM), 0, 0, 0)),
            pl.BlockSpec((D_MODEL, tn), lambda i, j: (0, j)),
        ],
        out_specs=pl.BlockSpec((IN_TM, tn), lambda i, j: (i, j)),
        out_shape=jax.ShapeDtypeStruct((N_TOK, n), out_dtype),
        scratch_shapes=[pltpu.VMEM((IN_TM, D_MODEL), BF16)],
        compiler_params=_params("arbitrary", "arbitrary"),
        name="in_proj",
    )(*xs, norm_w.reshape(1, D_MODEL), mod, w)


OUT_TM = 256


def _out_proj_kernel(*refs, n_x, t0, final):
    x_refs = refs[:n_x]
    o_ref, g_ref, w_ref, mod_ref, fw_ref, y_ref, wb_ref = refs[n_x:]

    @pl.when(pl.program_id(0) == 0)
    def _():
        wb_ref[...] = w_ref[...].astype(BF16)

    a = (o_ref[...] * _silu(g_ref[...].astype(F32))).astype(BF16)
    xn = _read_x(x_refs, t0 + pl.program_id(0), OUT_TM) + mod_ref[0, 2] * _dot(a, wb_ref[...])
    if final:
        ms = jnp.mean(xn * xn, axis=-1, keepdims=True)
        xn = xn * lax.rsqrt(ms + EPS) * fw_ref[...]
    y_ref[...] = xn


def _out_proj(o, g_arr, g_blk, w, x, mod, final_w, final, rows=(0, N_TOK), g_row0=0):
    k = w.shape[0]
    t0 = rows[0] // OUT_TM
    g0 = g_row0 // OUT_TM
    x_specs, xs = _x_specs(x, OUT_TM, lambda i: t0 + i)
    return pl.pallas_call(
        functools.partial(_out_proj_kernel, n_x=len(xs), t0=t0, final=final),
        grid=((rows[1] - rows[0]) // OUT_TM,),
        in_specs=x_specs + [
            pl.BlockSpec((OUT_TM, k), lambda i: (t0 + i, 0)),
            pl.BlockSpec((OUT_TM, k), lambda i: (g0 + t0 + i, g_blk)),
            pl.BlockSpec((k, D_MODEL), lambda i: (0, 0)),
            pl.BlockSpec((1, 3, 1, D_MODEL), lambda i: (_cond_of_tile(t0 + i, OUT_TM), 0, 0, 0)),
            pl.BlockSpec((1, D_MODEL), lambda i: (0, 0)),
        ],
        out_specs=pl.BlockSpec((OUT_TM, D_MODEL), lambda i: (i, 0)),
        out_shape=jax.ShapeDtypeStruct((rows[1] - rows[0], D_MODEL), F32),
        scratch_shapes=[pltpu.VMEM((k, D_MODEL), BF16)],
        compiler_params=_params("arbitrary"),
        name="out_proj",
    )(*xs, o, g_arr, w, mod, final_w.reshape(1, D_MODEL))


def _rope_tables(d):
    q = d // 4
    t = jnp.arange(DEC_SEQ)
    row = (t // GRID_W).astype(F32)
    col = (t % GRID_W).astype(F32)
    inv = ROPE_BASE ** (-jnp.arange(0, 2 * q, 2, dtype=F32) / (2 * q))
    ar = row[:, None] * inv[None, :]
    ac = col[:, None] * inv[None, :]
    z = jnp.zeros_like(ar)
    cos = jnp.concatenate([jnp.cos(ar), jnp.cos(ar), jnp.cos(ac), jnp.cos(ac)], axis=-1)
    sin_lo = jnp.concatenate([-jnp.sin(ar), z, -jnp.sin(ac), z], axis=-1)
    sin_hi = jnp.concatenate([z, jnp.sin(ar), z, jnp.sin(ac)], axis=-1)
    return cos, sin_lo, sin_hi


def _rope(x, cos, sin_lo, sin_hi, q):
    w = x.shape[-1]
    x_next = pltpu.roll(x, w - q, axis=1)
    x_prev = pltpu.roll(x, q, axis=1)
    return x * cos + x_next * sin_lo + x_prev * sin_hi


RET_QB = 256


def _ret_kernel(lg_ref, q_ref, k_ref, v_ref, gn_ref, *rest, seq, latent):
    if latent:
        cos_ref, slo_ref, shi_ref, s0_ref, _prev_ref, o_ref = rest
    else:
        o_ref, st_ref, dec_ref = rest
    h = pl.program_id(1 if latent else 0)
    lgf = lg_ref[0, h]
    lgb = lg_ref[1, h]
    q = q_ref[...].astype(F32)
    k = k_ref[...].astype(F32)
    if latent:
        q = _rope(q, cos_ref[...], slo_ref[...], shi_ref[...], RET_DK // 4)
        k = _rope(k, cos_ref[...], slo_ref[...], shi_ref[...], RET_DK // 4)
    k = k * (RET_DK ** -0.5)
    kb = k.astype(BF16)
    vb = v_ref[...].astype(BF16)
    gn = gn_ref[...]

    def decay(qi):
        ii = lax.broadcasted_iota(jnp.int32, (RET_QB, seq), 0) + qi * RET_QB
        jj = lax.broadcasted_iota(jnp.int32, (RET_QB, seq), 1)
        gap = (ii - jj).astype(F32)
        return (jnp.where(gap >= 0, jnp.exp(lgf * jnp.maximum(gap, 0.0)), 0.0)
                + jnp.where(gap <= 0, jnp.exp(lgb * jnp.maximum(-gap, 0.0)), 0.0))

    if not latent:
        @pl.when(pl.program_id(1) == 0)
        def _():
            dec_ref[...] = decay(0)

    for qi in range(seq // RET_QB):
        qblk = q[qi * RET_QB:(qi + 1) * RET_QB]
        s = _dot_nt(qblk.astype(BF16), kb)
        dec = decay(qi) if latent else dec_ref[...]
        o = _dot((s * dec).astype(BF16), vb)
        if latent:
            pos = (lax.broadcasted_iota(jnp.int32, (RET_QB, 1), 0) + qi * RET_QB).astype(F32)
            qf = qblk * jnp.exp(lgf * (pos + 1.0))
            qr = qblk * jnp.exp(lgb * (seq - pos))
            o = o + _dot(qf.astype(BF16), s0_ref[0, 0, 0, 0].astype(BF16))
            o = o + _dot(qr.astype(BF16), s0_ref[0, 0, 1, 0].astype(BF16))
        oc = o - jnp.mean(o, axis=-1, keepdims=True)
        o = oc * lax.rsqrt(jnp.mean(oc * oc, axis=-1, keepdims=True) + GN_EPS) * gn
        o_ref[qi * RET_QB:(qi + 1) * RET_QB, :] = o
    if not latent:
        pos = lax.broadcasted_iota(jnp.int32, (seq, 1), 0).astype(F32)
        kf = k * jnp.exp(lgf * (seq - 1.0 - pos))
        kr = k * jnp.exp(lgb * pos)
        st_ref[0, 0, 0, 0] = _dot_tn(kf.astype(BF16), vb)
        st_ref[0, 0, 1, 0] = _dot_tn(kr.astype(BF16), vb)


def _retention(p, log_g, gn_w, state_ret, j):
    smem = pl.BlockSpec(memory_space=pltpu.SMEM)
    gn = gn_w.reshape(1, RET_V)
    kq = RET_QK // RET_DK
    o_p, st = pl.pallas_call(
        functools.partial(_ret_kernel, seq=SEQ, latent=False),
        grid=(RET_HEADS, BATCH),
        in_specs=[
            smem,
            pl.BlockSpec((SEQ, RET_DK), lambda h, b: (b, h)),
            pl.BlockSpec((SEQ, RET_DK), lambda h, b: (b, kq + h)),
            pl.BlockSpec((SEQ, RET_DV), lambda h, b: (b, kq + h)),
            pl.BlockSpec((1, RET_DV), lambda h, b: (0, h)),
        ],
        out_specs=[
            pl.BlockSpec((SEQ, RET_DV), lambda h, b: (b, h)),
            pl.BlockSpec((1, 1, 2, 1, RET_DK, RET_DV), lambda h, b: (b, 0, 0, h, 0, 0)),
        ],
        out_shape=[
            jax.ShapeDtypeStruct((N_TOK, RET_V), F32),
            jax.ShapeDtypeStruct((BATCH, 1, 2, RET_HEADS, RET_DK, RET_DV), F32),
        ],
        scratch_shapes=[pltpu.VMEM((RET_QB, SEQ), F32)],
        compiler_params=_params("arbitrary", "arbitrary"),
        name="retention_prompt",
    )(log_g, p, p, p, gn)
    cos, slo, shi = _rope_tables(RET_DK)
    rb = N_PROMPT_TOK // DEC_SEQ
    full = pl.BlockSpec((DEC_SEQ, RET_DK), lambda b, h: (0, 0))
    o = pl.pallas_call(
        functools.partial(_ret_kernel, seq=DEC_SEQ, latent=True),
        grid=(DEC_BATCH, RET_HEADS),
        in_specs=[
            smem,
            pl.BlockSpec((DEC_SEQ, RET_DK), lambda b, h: (rb + b, h)),
            pl.BlockSpec((DEC_SEQ, RET_DK), lambda b, h: (rb + b, kq + h)),
            pl.BlockSpec((DEC_SEQ, RET_DV), lambda b, h: (rb + b, kq + h)),
            pl.BlockSpec((1, RET_DV), lambda b, h: (0, h)),
            full, full, full,
            pl.BlockSpec((1, 1, 2, 1, RET_DK, RET_DV), lambda b, h: (b, j, 0, h, 0, 0)),
            pl.BlockSpec(memory_space=pl.ANY),
        ],
        out_specs=pl.BlockSpec((DEC_SEQ, RET_DV), lambda b, h: (rb + b, h)),
        out_shape=jax.ShapeDtypeStruct((N_TOK, RET_V), F32),
        input_output_aliases={9: 0},
        compiler_params=_params("arbitrary", "arbitrary"),
        name="retention_latent",
    )(log_g, p, p, p, gn, cos, slo, shi, state_ret, o_p)
    return o, st


def _layer_ret(x, p, mod, j):
    i = N_MIXERS * j + 0
    proj = _in_proj(x, p['norm_w'][i], mod, p['ret_w_in'][j], 1024, out_dtype=BF16)
    log_g = jax.nn.log_sigmoid(p['ret_decay'][j].astype(F32))
    o, st = _retention(proj, log_g, p['ret_gn'][j], p['state_ret'], j)
    x = _out_proj(o, proj, (2 * RET_QK + RET_V) // RET_V, p['ret_w_out'][j], x, mod, p['final_norm_w'], False)
    return x, st


RW_TM = 512
RW_HALO = 8
RW_C = 64
RW_LOCK = 4
RW_PAIRS = 2


def _rwkv_prep_kernel(x_ref, xp_ref, xn_ref, nw_ref, mod_ref, mu_ref, wa_ref, aa_ref, wb_ref, ab_ref,
                      w0_ref, a0_ref, xm_ref, lw_ref, a_ref):
    i = pl.program_id(0)
    nw = nw_ref[...]
    h = _norm_mod(x_ref[...], nw, mod_ref)
    h_before = _norm_mod(xp_ref[RW_HALO - 1:RW_HALO, :], nw, mod_ref)
    h_after = _norm_mod(xn_ref[0:1, :], nw, mod_ref)
    seq = jnp.where(i < N_PROMPT_TOK // RW_TM, SEQ, DEC_SEQ)
    row = lax.broadcasted_iota(jnp.int32, (RW_TM, 1), 0)
    t = (row + i * RW_TM) & (seq - 1)
    prev = jnp.where(row == 0, h_before, pltpu.roll(h, 1, axis=0))
    nxt = jnp.where(row == RW_TM - 1, h_after, pltpu.roll(h, RW_TM - 1, axis=0))
    prev = jnp.where(t == 0, 0.0, prev)
    nxt = jnp.where(t == seq - 1, 0.0, nxt)
    xx = 0.5 * (prev + nxt) - h
    for n, m in enumerate((0, 2, 3, 5)):
        xm_ref[n] = (h + xx * mu_ref[m:m + 1, :]).astype(BF16)
    xw = (h + xx * mu_ref[1:2, :]).astype(BF16)
    xa = (h + xx * mu_ref[4:5, :]).astype(BF16)
    lw = jnp.tanh(_dot(xw, wa_ref[...])).astype(BF16)
    la = _dot(xa, aa_ref[...]).astype(BF16)
    for dr in range(2):
        wl = w0_ref[dr:dr + 1, :] + _dot(lw, wb_ref[dr])
        lw_ref[dr] = -math.exp(-0.5) * _sigmoid(wl)
        a_ref[dr] = _sigmoid(a0_ref[dr:dr + 1, :] + _dot(la, ab_ref[dr]))


def _rwkv_prep(x, norm_w, mod, mu, wa2, aa2, wb_pad, ab_pad, w0, a0):
    nt = N_TOK // RW_TM
    hb = RW_TM // RW_HALO
    last = N_TOK // RW_HALO - 1
    full2 = lambda shape: pl.BlockSpec(shape, lambda i: (0, 0))
    full3 = lambda shape: pl.BlockSpec(shape, lambda i: (0, 0, 0))
    return pl.pallas_call(
        _rwkv_prep_kernel,
        grid=(nt,),
        in_specs=[
            pl.BlockSpec((RW_TM, D_MODEL), lambda i: (i, 0)),
            pl.BlockSpec((RW_HALO, D_MODEL), lambda i: (jnp.maximum(i * hb - 1, 0), 0)),
            pl.BlockSpec((RW_HALO, D_MODEL), lambda i: (jnp.minimum((i + 1) * hb, last), 0)),
            full2((1, D_MODEL)),
            pl.BlockSpec((1, 3, 1, D_MODEL), lambda i: (_cond_of_tile(i, RW_TM), 0, 0, 0)),
            full2((6, D_MODEL)),
            full2((D_MODEL, 2 * RWKV_RANK)),
            full2((D_MODEL, 2 * RWKV_RANK)),
            full3((2, 2 * RWKV_RANK, D_MODEL)),
            full3((2, 2 * RWKV_RANK, D_MODEL)),
            full2((2, D_MODEL)),
            full2((2, D_MODEL)),
        ],
        out_specs=[
            pl.BlockSpec((4, RW_TM, D_MODEL), lambda i: (0, i, 0)),
            pl.BlockSpec((2, RW_TM, D_MODEL), lambda i: (0, i, 0)),
            pl.BlockSpec((2, RW_TM, D_MODEL), lambda i: (0, i, 0)),
        ],
        out_shape=[
            jax.ShapeDtypeStruct((4, N_TOK, D_MODEL), BF16),
            jax.ShapeDtypeStruct((2, N_TOK, D_MODEL), F32),
            jax.ShapeDtypeStruct((2, N_TOK, D_MODEL), F32),
        ],
        compiler_params=_params("arbitrary"),
        name="rwkv_prep",
    )(x, x, x, norm_w.reshape(1, D_MODEL), mod, mu, wa2, aa2, wb_pad, ab_pad, w0, a0)


def _bmm_kernel(a_ref, w_ref, o_ref):
    o_ref[0] = _dot(a_ref[0], w_ref[...])


def _rwkv_rkvg(xm, w):
    tm = 1024
    return pl.pallas_call(
        _bmm_kernel,
        grid=(4, N_TOK // tm),
        in_specs=[
            pl.BlockSpec((1, tm, D_MODEL), lambda n, i: (n, i, 0)),
            pl.BlockSpec((D_MODEL, D_MODEL), lambda n, i: (0, n)),
        ],
        out_specs=pl.BlockSpec((1, tm, D_MODEL), lambda n, i: (n, i, 0)),
        out_shape=jax.ShapeDtypeStruct((4, N_TOK, D_MODEL), F32),
        compiler_params=_params("arbitrary", "arbitrary"),
        name="rwkv_rkvg",
    )(xm, w)


def _head_sum(x, first):
    s0 = jnp.sum(jnp.where(first, x, 0.0), axis=-1, keepdims=True)
    s1 = jnp.sum(jnp.where(first, 0.0, x), axis=-1, keepdims=True)
    return jnp.where(first, s0, s1)


def _stack_heads(x, first):
    return jnp.concatenate([jnp.where(first, x, 0.0), jnp.where(first, 0.0, x)], axis=0)


def _cumsum_rows(tri, x):
    hi = x.astype(BF16)
    r1 = x - hi.astype(F32)
    mid = r1.astype(BF16)
    lo = (r1 - mid.astype(F32)).astype(BF16)
    return _dot(tri, hi) + _dot(tri, mid) + _dot(tri, lo)


def _rwkv_chunk_kernel(*refs, seq, zero_init):
    if zero_init:
        (rkv_ref, lw_ref, a_ref, kkp_ref, kap_ref, rkp_ref, gn_ref, o_ref, st_ref,
         kk_scr, y_scr, tar_scr, lrb_scr, b2_scr, w2_scr, yl_scr, kv_scr, pc_scr) = refs
    else:
        (rkv_ref, lw_ref, a_ref, kkp_ref, kap_ref, rkp_ref, gn_ref, s0_ref, _prev_ref, o_ref,
         kk_scr, y_scr, tar_scr, lrb_scr, b2_scr, w2_scr, yl_scr, kv_scr, pc_scr) = refs
    c_len = RW_C
    n_ch = seq // c_len
    rows2 = 2 * c_len
    first = _first_half_lanes()
    kap = kap_ref[...]

    kk = rkv_ref[1] * kkp_ref[...]
    kk_scr[...] = kk * lax.rsqrt(jnp.maximum(_head_sum(kk * kk, first), 1e-12))

    rr = lax.broadcasted_iota(jnp.int32, (rows2, rows2), 0)
    cc = lax.broadcasted_iota(jnp.int32, (rows2, rows2), 1)
    eye = (rr == cc).astype(F32)
    tr = lax.broadcasted_iota(jnp.int32, (c_len, c_len), 0)
    tc = lax.broadcasted_iota(jnp.int32, (c_len, c_len), 1)

    def same(shift):
        return (rr >> shift) == (cc >> shift)

    head = same(6)
    strict = (head & (cc < rr), head & (cc > rr))
    incl = (head & (cc <= rr), head & (cc >= rr))
    tri = ((tc <= tr).astype(BF16), (tc >= tr).astype(BF16))
    last = (c_len - 1, 0)

    def phase1(chains):
        dirs = [dr for dr, _ in chains]
        rows = [pl.ds(pl.multiple_of(c * c_len, c_len), c_len) for _, c in chains]
        lw = [lw_ref[dr, rw, :] for dr, rw in zip(dirs, rows)]
        cum = [_cumsum_rows(tri[dr], x) for dr, x in zip(dirs, lw)]
        a2, r2, b2, k2, v2, pc = [], [], [], [], [], []
        for dr, rw, lw_c, cum_c in zip(dirs, rows, lw, cum):
            a = a_ref[dr, rw, :]
            k = rkv_ref[1, rw, :]
            kk_c = kk_scr[rw, :]
            e_inc = jnp.exp(cum_c)
            e_inv = jnp.exp(-cum_c)
            a2.append(_stack_heads(-kk_c * jnp.exp(cum_c - lw_c), first).astype(BF16))
            r2.append(_stack_heads(rkv_ref[0, rw, :] * e_inc, first).astype(BF16))
            b2.append(_stack_heads(kk_c * a * e_inv, first).astype(BF16))
            k2.append(_stack_heads(k * (1.0 + (a - 1.0) * kap) * e_inv, first).astype(BF16))
            v2.append(_stack_heads(rkv_ref[2, rw, :], first).astype(BF16))
            pc.append(e_inc[last[dr]:last[dr] + 1, :])
        g = [_dot_nt(jnp.concatenate([x, y], axis=0), jnp.concatenate([z, w], axis=0))
             for x, y, z, w in zip(a2, r2, b2, k2)]
        l_ab = [jnp.where(strict[dr], x[:rows2, :rows2], 0.0) for dr, x in zip(dirs, g)]
        t = [eye + jnp.where(same(1), x, 0.0) for x in l_ab]
        side = {}
        for shift in range(1, 6):
            sib = same(shift + 1) & ~same(shift)
            tb = [x.astype(BF16) for x in t]
            mid = [_dot(jnp.where(sib, x, 0.0).astype(BF16), y) for x, y in zip(l_ab, tb)]
            if shift == 1:
                side['lv'] = [_dot(jnp.where(strict[dr], x[:rows2, rows2:], 0.0).astype(BF16), y)
                              for dr, x, y in zip(dirs, g, v2)]
            elif shift == 2:
                side['yl'] = [_dot(jnp.where(incl[dr], x[rows2:, rows2:], 0.0).astype(BF16), y)
                              for dr, x, y in zip(dirs, g, v2)]
            elif shift == 3:
                side['kv'] = [_dot_tn(x, y) for x, y in zip(v2, k2)]
            t = [x + _dot(y, z.astype(BF16)) for x, y, z in zip(t, tb, mid)]
        tb = [x.astype(BF16) for x in t]
        ta = [_dot(x, y) for x, y in zip(tb, a2)]
        w2 = [_dot(x, y.astype(BF16)) for x, y in zip(tb, side['lv'])]
        for i, (dr, c) in enumerate(chains):
            n = dr * n_ch + c
            tar_scr[n, :rows2, :] = ta[i].astype(BF16)
            tar_scr[n, rows2:, :] = r2[i]
            w2_scr[n] = w2[i]
            yl_scr[n] = side['yl'][i]
            kv_scr[n] = side['kv'][i]
            lrb_scr[n] = jnp.where(incl[dr], g[i][rows2:, :rows2], 0.0).astype(BF16)
            b2_scr[n] = b2[i]
            pc_scr[n] = pc[i]

    def body1(grp, carry):
        phase1([(dr, grp * RW_LOCK + j) for j in range(RW_LOCK) for dr in range(2)])
        return carry

    if n_ch == RW_LOCK:
        body1(0, 0)
    else:
        lax.fori_loop(0, n_ch // RW_LOCK, body1, 0)

    def body2(i, carry):
        cs = (i, n_ch - 1 - i)
        ns = [dr * n_ch + c for dr, c in enumerate(cs)]
        x = [_dot_nt(tar_scr[n], s2.astype(BF16)) for n, s2 in zip(ns, carry)]
        u2 = [(xx[:rows2] + w2_scr[n]).astype(BF16) for n, xx in zip(ns, x)]
        upd = [_dot_tn(u, b2_scr[n]) for n, u in zip(ns, u2)]
        yb = [_dot(lrb_scr[n], u) for n, u in zip(ns, u2)]
        out = []
        for dr, (c, n) in enumerate(zip(cs, ns)):
            y2 = x[dr][rows2:] + yb[dr] + yl_scr[n]
            y_scr[dr, pl.ds(pl.multiple_of(c * c_len, c_len), c_len), :] = y2[:c_len] + y2[c_len:]
            out.append((carry[dr] + upd[dr] + kv_scr[n]) * pc_scr[n])
        return tuple(out)

    if zero_init:
        init = (jnp.zeros((rows2, LANES), F32),) * 2
    else:
        init = (s0_ref[0, 0, 0], s0_ref[0, 1, 0])
    s_f, s_b = lax.fori_loop(0, n_ch, body2, init)

    y = y_scr[0] + y_scr[1]
    yc = y - _head_sum(y, first) * (1.0 / RWKV_HD)
    o = yc * lax.rsqrt(_head_sum(yc * yc, first) * (1.0 / RWKV_HD) + GN_EPS) * gn_ref[...]
    r = rkv_ref[0]
    k = rkv_ref[1]
    v = rkv_ref[2]
    for dr in range(2):
        kd = k * (1.0 + (a_ref[dr] - 1.0) * kap)
        o = o + _head_sum(r * kd * rkp_ref[...], first) * v
    o_ref[...] = o

    if zero_init:
        for dr, s2 in enumerate((s_f, s_b)):
            st_ref[0, 0, dr, 0] = s2[:RWKV_HD, :RWKV_HD]
            st_ref[0, 0, dr, 1] = s2[RWKV_HD:, RWKV_HD:]


def _rwkv_chunked(rkvg, lw, a, kkp, kap, rkp, gn, s0_pairs):
    npair = RWKV_HEADS // 2
    par = lambda *_: pl.BlockSpec((1, LANES), lambda s, p: (0, p))

    def scratch(seq):
        n = 2 * (seq // RW_C)
        r2 = 2 * RW_C
        return [
            pltpu.VMEM((seq, LANES), F32), pltpu.VMEM((2, seq, LANES), F32),
            pltpu.VMEM((n, 2 * r2, LANES), BF16), pltpu.VMEM((n, r2, r2), BF16), pltpu.VMEM((n, r2, LANES), BF16),
            pltpu.VMEM((n, r2, LANES), F32), pltpu.VMEM((n, r2, LANES), F32), pltpu.VMEM((n, r2, LANES), F32),
            pltpu.VMEM((n, 1, LANES), F32),
        ]

    def seq_specs(seq, rb):
        return [
            pl.BlockSpec((3, seq, LANES), lambda s, p: (0, rb + s, p)),
            pl.BlockSpec((2, seq, LANES), lambda s, p: (0, rb + s, p)),
            pl.BlockSpec((2, seq, LANES), lambda s, p: (0, rb + s, p)),
            par(), par(), par(), par(),
        ]

    pars = [u.reshape(1, D_MODEL) for u in (kkp, kap, rkp, gn)]
    o_p, st = pl.pallas_call(
        functools.partial(_rwkv_chunk_kernel, seq=SEQ, zero_init=True),
        grid=(BATCH, npair),
        in_specs=seq_specs(SEQ, 0),
        out_specs=[
            pl.BlockSpec((SEQ, LANES), lambda s, p: (s, p)),
            pl.BlockSpec((1, 1, 2, 2, RWKV_HD, RWKV_HD), lambda s, p: (s, 0, 0, p, 0, 0)),
        ],
        out_shape=[
            jax.ShapeDtypeStruct((N_TOK, D_MODEL), F32),
            jax.ShapeDtypeStruct((BATCH, 1, 2, RWKV_HEADS, RWKV_HD, RWKV_HD), F32),
        ],
        scratch_shapes=scratch(SEQ),
        compiler_params=_params("arbitrary", "arbitrary"),
        name="rwkv_prompt",
    )(rkvg, lw, a, *pars)
    rb = N_PROMPT_TOK // DEC_SEQ
    o = pl.pallas_call(
        functools.partial(_rwkv_chunk_kernel, seq=DEC_SEQ, zero_init=False),
        grid=(DEC_BATCH, npair),
        in_specs=seq_specs(DEC_SEQ, rb) + [
            pl.BlockSpec((1, 2, 1, LANES, LANES), lambda s, p: (s, 0, p, 0, 0)),
            pl.BlockSpec(memory_space=pl.ANY),
        ],
        out_specs=pl.BlockSpec((DEC_SEQ, LANES), lambda s, p: (rb + s, p)),
        out_shape=jax.ShapeDtypeStruct((N_TOK, D_MODEL), F32),
        input_output_aliases={8: 0},
        scratch_shapes=scratch(DEC_SEQ),
        compiler_params=_params("arbitrary", "arbitrary"),
        name="rwkv_latent",
    )(rkvg, lw, a, *pars, s0_pairs, o_p)
    return o, st


def _split3(x):
    hi = x.astype(BF16)
    r1 = x - hi.astype(F32)
    mid = r1.astype(BF16)
    return hi, mid, (r1 - mid.astype(F32)).astype(BF16)


def _rwkv_kernel(*refs, seq, zero_init, pg, np2):
    n_in = 7 if zero_init else 9
    rkv_ref, lw_ref, a_ref, kkp_ref, kap_ref, rkp_ref, gn_ref = refs[:7]
    if zero_init:
        o_ref, st_ref = refs[n_in:n_in + 2]
        scr = refs[n_in + 2:]
    else:
        s0_ref = refs[7]
        o_ref = refs[n_in]
        scr = refs[n_in + 1:]
    kk_scr, cum_scr, bon_scr, y_scr, s_scr, tar_scr, lrb_scr, b2_scr, w2_scr, yl_scr, kv_scr, pc_scr = scr
    c_len = RW_C
    n_ch = seq // c_len
    rows2 = 2 * c_len
    grp = pl.program_id(1)
    defer = np2 > pg
    base = grp * pg if defer else 0
    first = _first_half_lanes()

    rr = lax.broadcasted_iota(jnp.int32, (rows2, rows2), 0)
    cc = lax.broadcasted_iota(jnp.int32, (rows2, rows2), 1)
    eye = (rr == cc).astype(F32)

    def same(shift):
        return (rr >> shift) == (cc >> shift)

    head = same(6)
    strict = (head & (cc < rr), head & (cc > rr))
    incl = (head & (cc <= rr), head & (cc >= rr))
    last = (c_len - 1, 0)
    head_ones = head.astype(BF16)

    cs_rows = min(seq, 256)
    tr = lax.broadcasted_iota(jnp.int32, (cs_rows, cs_rows), 0)
    tc = lax.broadcasted_iota(jnp.int32, (cs_rows, cs_rows), 1)
    chunk = (tr >> 6) == (tc >> 6)
    tri = ((chunk & (tc <= tr)).astype(BF16), (chunk & (tc >= tr)).astype(BF16))
    for p in range(pg):
        ln = slice(p * LANES, (p + 1) * LANES)
        kap = kap_ref[:, ln]
        r = rkv_ref[0, :, ln]
        k = rkv_ref[1, :, ln]
        v = rkv_ref[2, :, ln]
        kk = k * kkp_ref[:, ln]
        kk_scr[p] = kk * lax.rsqrt(jnp.maximum(_head_sum(kk * kk, first), 1e-12))
        bonus = None
        for dr in range(2):
            kd = k * (1.0 + (a_ref[dr, :, ln] - 1.0) * kap)
            term = _head_sum(r * kd * rkp_ref[:, ln], first) * v
            bonus = term if bonus is None else bonus + term
        bon_scr[base + p] = bonus
    for p in range(0, pg, 2):
        for dr in range(2):
            for r0 in range(0, seq, cs_rows):
                parts = _split3(lw_ref[dr, r0:r0 + cs_rows, p * LANES:(p + 2) * LANES])
                cum = _dot(tri[dr], parts[0]) + _dot(tri[dr], parts[1]) + _dot(tri[dr], parts[2])
                cum_scr[p, dr, r0:r0 + cs_rows, :] = cum[:, :LANES]
                cum_scr[p + 1, dr, r0:r0 + cs_rows, :] = cum[:, LANES:]

    def phase1(chains):
        dirs = [dr for _, dr, _ in chains]
        a2, r2, b2, k2, v2, pc = [], [], [], [], [], []
        for p, dr, c in chains:
            ln = slice(p * LANES, (p + 1) * LANES)
            rw = pl.ds(pl.multiple_of(c * c_len, c_len), c_len)
            a = a_ref[dr, rw, ln]
            k = rkv_ref[1, rw, ln]
            kk_c = kk_scr[p, rw, :]
            cum_c = cum_scr[p, dr, rw, :]
            e_inc = jnp.exp(cum_c)
            e_inv = jnp.exp(-cum_c)
            a2.append(_stack_heads(-kk_c * jnp.exp(cum_c - lw_ref[dr, rw, ln]), first).astype(BF16))
            r2.append(_stack_heads(rkv_ref[0, rw, ln] * e_inc, first).astype(BF16))
            b2.append(_stack_heads(kk_c * a * e_inv, first).astype(BF16))
            k2.append(_stack_heads(k * (1.0 + (a - 1.0) * kap_ref[:, ln]) * e_inv, first).astype(BF16))
            v2.append(_stack_heads(rkv_ref[2, rw, ln], first).astype(BF16))
            pc.append(e_inc[last[dr]:last[dr] + 1, :])
        g = [_dot_nt(jnp.concatenate([x, y], axis=0), jnp.concatenate([z, w], axis=0))
             for x, y, z, w in zip(a2, r2, b2, k2)]
        l_ab = [jnp.where(strict[dr], x[:rows2, :rows2], 0.0) for dr, x in zip(dirs, g)]
        t = [eye + jnp.where(same(1), x, 0.0) for x in l_ab]
        side = {}
        for shift in range(1, 6):
            sib = same(shift + 1) & ~same(shift)
            tb = [x.astype(BF16) for x in t]
            mid = [_dot(jnp.where(sib, x, 0.0).astype(BF16), y) for x, y in zip(l_ab, tb)]
            if shift == 1:
                side['lv'] = [_dot(jnp.where(strict[dr], x[:rows2, rows2:], 0.0).astype(BF16), y)
                              for dr, x, y in zip(dirs, g, v2)]
            elif shift == 2:
                side['yl'] = [_dot(jnp.where(incl[dr], x[rows2:, rows2:], 0.0).astype(BF16), y)
                              for dr, x, y in zip(dirs, g, v2)]
            elif shift == 3:
                side['kv'] = [_dot_tn(x, y) for x, y in zip(v2, k2)]
            t = [x + _dot(y, z.astype(BF16)) for x, y, z in zip(t, tb, mid)]
        tb = [x.astype(BF16) for x in t]
        ta = [_dot(x, y) for x, y in zip(tb, a2)]
        w2 = [_dot(x, y.astype(BF16)) for x, y in zip(tb, side['lv'])]
        for i, (p, dr, c) in enumerate(chains):
            n = ((base + p) * 2 + dr) * n_ch + c
            tar_scr[n, :rows2, :] = ta[i].astype(BF16)
            tar_scr[n, rows2:, :] = r2[i]
            w2_scr[n] = w2[i]
            yl_scr[n] = side['yl'][i]
            kv_scr[n] = side['kv'][i]
            lrb_scr[n] = jnp.where(incl[dr], g[i][rows2:, :rows2], 0.0).astype(BF16)
            b2_scr[n] = b2[i]
            pc_scr[n] = pc[i]

    def body1(cg, carry):
        phase1([(p, dr, cg * RW_LOCK + j) for p in range(pg) for j in range(RW_LOCK) for dr in range(2)])
        return carry

    if n_ch == RW_LOCK:
        body1(0, 0)
    else:
        lax.fori_loop(0, n_ch // RW_LOCK, body1, 0)

    def finish():
        for p in range(np2):
            for dr in range(2):
                if zero_init:
                    s_scr[2 * p + dr] = jnp.zeros((rows2, LANES), F32)
                else:
                    s_scr[2 * p + dr] = s0_ref[0, dr, p]

        def body2(i, carry):
            cs = (i, n_ch - 1 - i)
            ids = [(p, dr) for p in range(np2) for dr in range(2)]
            ns = [(p * 2 + dr) * n_ch + cs[dr] for p, dr in ids]
            x = [_dot_nt(tar_scr[n], s_scr[2 * p + dr].astype(BF16)) for n, (p, dr) in zip(ns, ids)]
            u2 = [(xx[:rows2] + w2_scr[n]).astype(BF16) for n, xx in zip(ns, x)]
            upd = [_dot_tn(u, b2_scr[n]) for n, u in zip(ns, u2)]
            yb = [_dot(lrb_scr[n], u) for n, u in zip(ns, u2)]
            for j, (n, (p, dr)) in enumerate(zip(ns, ids)):
                y2 = x[j][rows2:] + yb[j] + yl_scr[n]
                y_scr[p, dr, pl.ds(pl.multiple_of(cs[dr] * c_len, c_len), c_len), :] = y2[:c_len] + y2[c_len:]
                s_scr[2 * p + dr] = (s_scr[2 * p + dr] + upd[j] + kv_scr[n]) * pc_scr[n]
            return carry

        lax.fori_loop(0, n_ch, body2, 0)

        def head_mean(xs):
            parts = [_split3(x) for x in xs]
            return [(_dot(a, head_ones) + _dot(b, head_ones) + _dot(c, head_ones)) * (1.0 / RWKV_HD)
                    for a, b, c in parts]

        ys = [y_scr[p, 0] + y_scr[p, 1] for p in range(np2)]
        yc = [y - m for y, m in zip(ys, head_mean(ys))]
        var = head_mean([x * x for x in yc])
        for p in range(np2):
            ln = slice(p * LANES, (p + 1) * LANES)
            o_ref[:, ln] = yc[p] * lax.rsqrt(var[p] + GN_EPS) * gn_ref[:, ln] + bon_scr[p]
            if zero_init:
                for dr in range(2):
                    s2 = s_scr[2 * p + dr]
                    st_ref[0, 0, dr, 2 * p] = s2[:RWKV_HD, :RWKV_HD]
                    st_ref[0, 0, dr, 2 * p + 1] = s2[RWKV_HD:, RWKV_HD:]

    if defer:
        pl.when(grp == pl.num_programs(1) - 1)(finish)
    else:
        finish()


def _rwkv_mixer(rkvg, lw, a, kkp, kap, rkp, gn, s0_pairs):
    npair = RWKV_HEADS // 2
    pg = RW_PAIRS
    wl = pg * LANES
    r2 = 2 * RW_C

    def scratch(seq, np2):
        n = 2 * np2 * (seq // RW_C)
        return [
            pltpu.VMEM((pg, seq, LANES), F32), pltpu.VMEM((pg, 2, seq, LANES), F32),
            pltpu.VMEM((np2, seq, LANES), F32), pltpu.VMEM((np2, 2, seq, LANES), F32),
            pltpu.VMEM((2 * np2, r2, LANES), F32),
            pltpu.VMEM((n, 2 * r2, LANES), BF16), pltpu.VMEM((n, r2, r2), BF16), pltpu.VMEM((n, r2, LANES), BF16),
            pltpu.VMEM((n, r2, LANES), F32), pltpu.VMEM((n, r2, LANES), F32), pltpu.VMEM((n, r2, LANES), F32),
            pltpu.VMEM((n, 1, LANES), F32),
        ]

    def seq_specs(seq, rb, gn_spec):
        par = pl.BlockSpec((1, wl), lambda s, g: (0, g))
        return [
            pl.BlockSpec((3, seq, wl), lambda s, g: (0, rb + s, g)),
            pl.BlockSpec((2, seq, wl), lambda s, g: (0, rb + s, g)),
            pl.BlockSpec((2, seq, wl), lambda s, g: (0, rb + s, g)),
            par, par, par, gn_spec,
        ]

    pars = [u.reshape(1, D_MODEL) for u in (kkp, kap, rkp, gn)]
    o_p, st = pl.pallas_call(
        functools.partial(_rwkv_kernel, seq=SEQ, zero_init=True, pg=pg, np2=npair),
        grid=(BATCH, npair // pg),
        in_specs=seq_specs(SEQ, 0, pl.BlockSpec((1, D_MODEL), lambda s, g: (0, 0))),
        out_specs=[
            pl.BlockSpec((SEQ, D_MODEL), lambda s, g: (s, 0)),
            pl.BlockSpec((1, 1, 2, RWKV_HEADS, RWKV_HD, RWKV_HD), lambda s, g: (s, 0, 0, 0, 0, 0)),
        ],
        out_shape=[
            jax.ShapeDtypeStruct((N_TOK, D_MODEL), F32),
            jax.ShapeDtypeStruct((BATCH, 1, 2, RWKV_HEADS, RWKV_HD, RWKV_HD), F32),
        ],
        scratch_shapes=scratch(SEQ, npair),
        compiler_params=_params("arbitrary", "arbitrary"),
        name="rwkv_prompt",
    )(rkvg, lw, a, *pars)
    rb = N_PROMPT_TOK // DEC_SEQ
    o = pl.pallas_call(
        functools.partial(_rwkv_kernel, seq=DEC_SEQ, zero_init=False, pg=pg, np2=pg),
        grid=(DEC_BATCH, npair // pg),
        in_specs=seq_specs(DEC_SEQ, rb, pl.BlockSpec((1, wl), lambda s, g: (0, g))) + [
            pl.BlockSpec((1, 2, pg, LANES, LANES), lambda s, g: (s, 0, g, 0, 0)),
            pl.BlockSpec(memory_space=pl.ANY),
        ],
        out_specs=pl.BlockSpec((DEC_SEQ, wl), lambda s, g: (rb + s, g)),
        out_shape=jax.ShapeDtypeStruct((N_TOK, D_MODEL), F32),
        input_output_aliases={8: 0},
        scratch_shapes=scratch(DEC_SEQ, pg),
        compiler_params=_params("arbitrary", "arbitrary"),
        name="rwkv_latent",
    )(rkvg, lw, a, *pars, s0_pairs, o_p)
    return o, st


def _state_pairs(s0):
    s = s0.reshape(DEC_BATCH, 2, RWKV_HEADS // 2, 2, RWKV_HD, RWKV_HD)
    z = jnp.zeros_like(s[:, :, :, 0])
    top = jnp.concatenate([s[:, :, :, 0], z], axis=-1)
    bot = jnp.concatenate([z, s[:, :, :, 1]], axis=-1)
    return jnp.concatenate([top, bot], axis=-2)


def _layer_rwkv(x, p, mod, j):
    i = N_MIXERS * j + 1
    wa, wb, aa, ab = p['rwkv_wA'][j], p['rwkv_wB'][j], p['rwkv_aA'][j], p['rwkv_aB'][j]
    z = jnp.zeros_like(wb[0])
    wa2 = jnp.concatenate([wa[0], wa[1]], axis=1).astype(BF16)
    aa2 = jnp.concatenate([aa[0], aa[1]], axis=1).astype(BF16)
    wb_pad = jnp.stack([jnp.concatenate([wb[0], z]), jnp.concatenate([z, wb[1]])]).astype(BF16)
    ab_pad = jnp.stack([jnp.concatenate([ab[0], z]), jnp.concatenate([z, ab[1]])]).astype(BF16)
    xm, lw, a = _rwkv_prep(x, p['norm_w'][i], mod, p['rwkv_mu'][j], wa2, aa2, wb_pad, ab_pad,
                           p['rwkv_w0'][j], p['rwkv_a0'][j])
    rkvg = _rwkv_rkvg(xm, p['rwkv_w_in'][j].astype(BF16))
    o, st = _rwkv_mixer(rkvg, lw, a, p['rwkv_kk'][j], p['rwkv_ka'][j], p['rwkv_rk'][j], p['rwkv_gn'][j],
                          _state_pairs(p['state_rwkv'][:, j]))
    x = _out_proj(o, rkvg.reshape(4 * N_TOK, D_MODEL), 0, p['rwkv_w_out'][j], x, mod, p['final_norm_w'], False,
                  g_row0=3 * N_TOK)
    return x, st


DIFF_W = 2 * DIFF_HD
ATT_QB = 256
DIFF_GROUP = 4


def _first_half_lanes():
    return lax.broadcasted_iota(jnp.int32, (1, LANES), 1) < LANES // 2


def _diff_lambda(lam_ref, lam_init):
    lp = lam_ref[...]
    return (jnp.exp(jnp.sum(lp[0:1] * lp[1:2], keepdims=True))
            - jnp.exp(jnp.sum(lp[2:3] * lp[3:4], keepdims=True)) + lam_init)


def _diff_heads(items, lam, lam_init):
    first = _first_half_lanes()
    scale = DIFF_HD ** -0.5
    sub = [(q, keys, comp) for q, keys, _ in items for comp in range(2)]
    qm = [jnp.where(first if comp == 0 else ~first, q, 0.0).astype(BF16) for q, _, comp in sub]
    s = [[_dot_nt(x, kb) * scale for kb, _ in keys] for x, (_, keys, _) in zip(qm, sub)]
    m = [functools.reduce(jnp.maximum, [jnp.max(u, axis=-1, keepdims=True) for u in ss]) for ss in s]
    e = [[jnp.exp(u - mm) for u in ss] for ss, mm in zip(s, m)]
    inv = [1.0 / functools.reduce(lambda x, y: x + y, [jnp.sum(u, axis=-1, keepdims=True) for u in ee]) for ee in e]
    outs = []
    for i, (_, keys, gn) in enumerate(items):
        o = None
        for n, (_, vb) in enumerate(keys):
            p = e[2 * i][n] * inv[2 * i] - lam * (e[2 * i + 1][n] * inv[2 * i + 1])
            part = _dot(p.astype(BF16), vb)
            o = part if o is None else o + part
        outs.append(o)
    return [o * lax.rsqrt(jnp.mean(o * o, axis=-1, keepdims=True) + EPS) * gn * (1.0 - lam_init)
            for o, (_, _, gn) in zip(outs, items)]


def _diff_prompt_kernel(lam_ref, q_ref, k_ref, v_ref, gn_ref, o_ref, ck_ref, cv_ref, *, lam_init):
    lam = _diff_lambda(lam_ref, lam_init)
    for h0 in range(0, DIFF_HEADS, DIFF_GROUP):
        items = []
        for h in range(h0, h0 + DIFF_GROUP):
            sl = slice(h * DIFF_W, (h + 1) * DIFF_W)
            k = k_ref[:, sl]
            v = v_ref[:, sl]
            ck_ref[0, 0, h] = k
            cv_ref[0, 0, h] = v
            items.append((q_ref[:, sl], [(k.astype(BF16), v.astype(BF16))], gn_ref[:, sl]))
        for h, o in zip(range(h0, h0 + DIFF_GROUP), _diff_heads(items, lam, lam_init)):
            o_ref[:, h * DIFF_W:(h + 1) * DIFF_W] = o


def _diff_latent_kernel(lam_ref, q_ref, k_ref, v_ref, ck_ref, cv_ref, cos_ref, slo_ref, shi_ref, gn_ref,
                        _prev_ref, o_ref, *, lam_init):
    lam = _diff_lambda(lam_ref, lam_init)
    tabs = (cos_ref[...], slo_ref[...], shi_ref[...])
    q = _rope(q_ref[...], *tabs, DIFF_HD // 4)
    k = _rope(k_ref[...], *tabs, DIFF_HD // 4)
    keys = [(k.astype(BF16), v_ref[...].astype(BF16)),
            (ck_ref[0, 0, 0].astype(BF16), cv_ref[0, 0, 0].astype(BF16))]
    gn = gn_ref[...]
    n_blk = DEC_SEQ // ATT_QB
    items = [(q[qi * ATT_QB:(qi + 1) * ATT_QB], keys, gn) for qi in range(n_blk)]
    for qi, o in enumerate(_diff_heads(items, lam, lam_init)):
        o_ref[qi * ATT_QB:(qi + 1) * ATT_QB, :] = o


def _diff_attention(proj, lam_p, gn_w, cache_k, cache_v, j, lam_init):
    gn = gn_w.reshape(1, D_MODEL)
    lam_spec = pl.BlockSpec((4, DIFF_HD), lambda *_: (0, 0))
    cache_out = pl.BlockSpec((1, 1, DIFF_HEADS, SEQ, DIFF_W), lambda b: (b, 0, 0, 0, 0))
    cache_shape = jax.ShapeDtypeStruct((BATCH, 1, DIFF_HEADS, SEQ, DIFF_W), F32)
    o_p, new_k, new_v = pl.pallas_call(
        functools.partial(_diff_prompt_kernel, lam_init=lam_init),
        grid=(BATCH,),
        in_specs=[
            lam_spec,
            pl.BlockSpec((SEQ, D_MODEL), lambda b: (b, 0)),
            pl.BlockSpec((SEQ, D_MODEL), lambda b: (b, 1)),
            pl.BlockSpec((SEQ, D_MODEL), lambda b: (b, 2)),
            pl.BlockSpec((1, D_MODEL), lambda b: (0, 0)),
        ],
        out_specs=[pl.BlockSpec((SEQ, D_MODEL), lambda b: (b, 0)), cache_out, cache_out],
        out_shape=[jax.ShapeDtypeStruct((N_TOK, D_MODEL), F32), cache_shape, cache_shape],
        compiler_params=_params("arbitrary"),
        name="diff_prompt",
    )(lam_p, proj, proj, proj, gn)
    cos, slo, shi = (jnp.concatenate([u, u], axis=-1) for u in _rope_tables(DIFF_HD))
    rb = N_PROMPT_TOK // DEC_SEQ
    nh = DIFF_HEADS
    tab = pl.BlockSpec((DEC_SEQ, DIFF_W), lambda b, h: (0, 0))
    cache = pl.BlockSpec((1, 1, 1, PAST_LEN, DIFF_W), lambda b, h: (b, j, h, 0, 0))
    o = pl.pallas_call(
        functools.partial(_diff_latent_kernel, lam_init=lam_init),
        grid=(DEC_BATCH, nh),
        in_specs=[
            lam_spec,
            pl.BlockSpec((DEC_SEQ, DIFF_W), lambda b, h: (rb + b, h)),
            pl.BlockSpec((DEC_SEQ, DIFF_W), lambda b, h: (rb + b, nh + h)),
            pl.BlockSpec((DEC_SEQ, DIFF_W), lambda b, h: (rb + b, 2 * nh + h)),
            cache, cache, tab, tab, tab,
            pl.BlockSpec((1, DIFF_W), lambda b, h: (0, h)),
            pl.BlockSpec(memory_space=pl.ANY),
        ],
        out_specs=pl.BlockSpec((DEC_SEQ, DIFF_W), lambda b, h: (rb + b, h)),
        out_shape=jax.ShapeDtypeStruct((N_TOK, D_MODEL), F32),
        input_output_aliases={10: 0},
        compiler_params=_params("arbitrary", "arbitrary"),
        name="diff_latent",
    )(lam_p, proj, proj, proj, cache_k, cache_v, cos, slo, shi, gn, o_p)
    return o, new_k, new_v


def _layer_diff(x, p, mod, j, i):
    lam_init = 0.8 - 0.6 * math.exp(-0.3 * i)
    proj = _in_proj(x, p['norm_w'][i], mod, p['diff_w_in'][j], 1024)
    o, new_k, new_v = _diff_attention(proj, p['diff_lambda'][j], p['diff_gn'][j], p['cache_diff_k'],
                                      p['cache_diff_v'], j, lam_init)
    x = _out_proj(o, proj, 3, p['diff_w_out'][j], x, mod, p['final_norm_w'], False)
    return x, new_k, new_v


NA_ROWS = DEC_SEQ // GRID_W
NA_WR = min(NA_WIN_R, NA_ROWS)
NA_LOC = NA_WR * GRID_W
NA_ROW_GROUP = 4
NA_PAIR_GROUP = 2


def _na_prompt_kernel(q_ref, k_ref, v_ref, o_ref, ck_ref, cv_ref):
    first = _first_half_lanes()
    scale = NA_HD ** -0.5
    for p0 in range(0, NA_HEADS // 2, NA_PAIR_GROUP):
        pairs = range(p0, p0 + NA_PAIR_GROUP)
        kb, vb = [], []
        for pr in pairs:
            sl = slice(pr * LANES, (pr + 1) * LANES)
            k = k_ref[:, sl]
            v = v_ref[:, sl]
            for half in range(2):
                ck_ref[0, 0, 2 * pr + half] = k[:, half * NA_HD:(half + 1) * NA_HD]
                cv_ref[0, 0, 2 * pr + half] = v[:, half * NA_HD:(half + 1) * NA_HD]
            kb.append(k.astype(BF16))
            vb.append(v.astype(BF16))
        items = [(i, half) for i in range(NA_PAIR_GROUP) for half in range(2)]
        qm = [jnp.where(first if half == 0 else ~first, q_ref[:, (p0 + i) * LANES:(p0 + i + 1) * LANES], 0.0)
              .astype(BF16) for i, half in items]
        s = [_dot_nt(x, kb[i]) * scale for x, (i, _) in zip(qm, items)]
        e = [jnp.exp(x - jnp.max(x, axis=-1, keepdims=True)) for x in s]
        inv = [1.0 / jnp.sum(x, axis=-1, keepdims=True) for x in e]
        outs = [_dot(x.astype(BF16), vb[i]) * z for x, z, (i, _) in zip(e, inv, items)]
        for i in range(NA_PAIR_GROUP):
            o_ref[:, (p0 + i) * LANES:(p0 + i + 1) * LANES] = jnp.where(first, outs[2 * i], outs[2 * i + 1])


def _na_latent_kernel(q_ref, k_ref, v_ref, kc_ref, vc_ref, tab_ref, _prev_ref, o_ref):
    first = _first_half_lanes()
    scale = NA_HD ** -0.5
    kb = k_ref[...].astype(BF16)
    vb = v_ref[...].astype(BF16)
    kcb = kc_ref[0, 0].astype(BF16)
    vcb = vc_ref[0, 0].astype(BF16)
    qcol = lax.broadcasted_iota(jnp.int32, (GRID_W, NA_LOC), 0)
    kcol = lax.broadcasted_iota(jnp.int32, (GRID_W, NA_LOC), 1) & (GRID_W - 1)
    cstart = jnp.clip(qcol - NA_WIN_C // 2, 0, GRID_W - NA_WIN_C)
    col_ok = (kcol >= cstart) & (kcol < cstart + NA_WIN_C)
    def bias_of(r, rs, half):
        parts = []
        for w in range(0, NA_WR, 2):
            src = jnp.broadcast_to(tab_ref[half, rs + w - r + NA_WIN_R - 1], (GRID_W, LANES))
            parts.append(pltpu.roll(src, LANES - (NA_WIN_C - 1), axis=1, stride=1, stride_axis=0))
        return jnp.concatenate(parts, axis=1)

    for r0 in range(0, NA_ROWS, NA_ROW_GROUP):
        items = [(r, min(max(r - NA_WR // 2, 0), NA_ROWS - NA_WR), half)
                 for r in range(r0, r0 + NA_ROW_GROUP) for half in range(2)]
        qm = [jnp.where(first if half == 0 else ~first, q_ref[r * GRID_W:(r + 1) * GRID_W, :], 0.0).astype(BF16)
              for r, _, half in items]
        s_loc = [_dot_nt(x, kb[rs * GRID_W:(rs + NA_WR) * GRID_W]) for x, (_, rs, _) in zip(qm, items)]
        s_ctx = [_dot_nt(x, kcb) * scale for x in qm]
        s_loc = [jnp.where(col_ok, x * scale + bias_of(*it), -jnp.inf) for x, it in zip(s_loc, items)]
        m = [jnp.maximum(jnp.max(x, axis=-1, keepdims=True), jnp.max(y, axis=-1, keepdims=True))
             for x, y in zip(s_loc, s_ctx)]
        e_loc = [jnp.exp(x - mm) for x, mm in zip(s_loc, m)]
        e_ctx = [jnp.exp(x - mm) for x, mm in zip(s_ctx, m)]
        inv = [1.0 / (jnp.sum(x, axis=-1, keepdims=True) + jnp.sum(y, axis=-1, keepdims=True))
               for x, y in zip(e_loc, e_ctx)]
        pv = [_dot(x.astype(BF16), vb[rs * GRID_W:(rs + NA_WR) * GRID_W]) for x, (_, rs, _) in zip(e_loc, items)]
        pc = [_dot(x.astype(BF16), vcb) for x in e_ctx]
        outs = [(x + y) * z for x, y, z in zip(pv, pc, inv)]
        for n in range(0, len(items), 2):
            r = items[n][0]
            o_ref[r * GRID_W:(r + 1) * GRID_W, :] = jnp.where(first, outs[n], outs[n + 1])


def _na_bias_pairs(table):
    t = table.astype(F32)
    nc = 2 * NA_WIN_C - 1
    z = jnp.zeros(t[:, :-1].shape[:2] + (GRID_W - nc,), F32)
    return jnp.concatenate([t[:, :-1], z, t[:, 1:], z], axis=-1)[:, :, None, :]


def _pair_heads(cache):
    c = cache.reshape(DEC_BATCH, NA_HEADS // 2, 2, PAST_LEN, NA_HD)
    return c.transpose(0, 1, 3, 2, 4).reshape(DEC_BATCH, NA_HEADS // 2, PAST_LEN, LANES)


def _na_attention(proj, bias_table, cache_k, cache_v):
    cache_out = pl.BlockSpec((1, 1, NA_HEADS, SEQ, NA_HD), lambda b: (b, 0, 0, 0, 0))
    cache_shape = jax.ShapeDtypeStruct((BATCH, 1, NA_HEADS, SEQ, NA_HD), F32)
    o_p, new_k, new_v = pl.pallas_call(
        _na_prompt_kernel,
        grid=(BATCH,),
        in_specs=[
            pl.BlockSpec((SEQ, D_MODEL), lambda b: (b, 0)),
            pl.BlockSpec((SEQ, D_MODEL), lambda b: (b, 1)),
            pl.BlockSpec((SEQ, D_MODEL), lambda b: (b, 2)),
        ],
        out_specs=[pl.BlockSpec((SEQ, D_MODEL), lambda b: (b, 0)), cache_out, cache_out],
        out_shape=[jax.ShapeDtypeStruct((N_TOK, D_MODEL), F32), cache_shape, cache_shape],
        compiler_params=_params("arbitrary"),
        name="na_prompt",
    )(proj, proj, proj)
    rb = N_PROMPT_TOK // DEC_SEQ
    npair = NA_HEADS // 2
    cache = pl.BlockSpec((1, 1, PAST_LEN, LANES), lambda pr, b: (b, pr, 0, 0))
    o = pl.pallas_call(
        _na_latent_kernel,
        grid=(npair, DEC_BATCH),
        in_specs=[
            pl.BlockSpec((DEC_SEQ, LANES), lambda pr, b: (rb + b, pr)),
            pl.BlockSpec((DEC_SEQ, LANES), lambda pr, b: (rb + b, npair + pr)),
            pl.BlockSpec((DEC_SEQ, LANES), lambda pr, b: (rb + b, 2 * npair + pr)),
            cache, cache,
            pl.BlockSpec((2, 2 * NA_WIN_R - 2, 1, LANES), lambda pr, b: (pr, 0, 0, 0)),
            pl.BlockSpec(memory_space=pl.ANY),
        ],
        out_specs=pl.BlockSpec((DEC_SEQ, LANES), lambda pr, b: (rb + b, pr)),
        out_shape=jax.ShapeDtypeStruct((N_TOK, D_MODEL), F32),
        input_output_aliases={6: 0},
        compiler_params=_params("arbitrary", "arbitrary"),
        name="na_latent",
    )(proj, proj, proj, _pair_heads(cache_k), _pair_heads(cache_v), _na_bias_pairs(bias_table), o_p)
    return o, new_k, new_v


def _layer_na(x, p, mod, j, final):
    i = N_MIXERS * j + 3
    proj = _in_proj(x, p['norm_w'][i], mod, p['na_w_in'][j], 1024)
    o, new_k, new_v = _na_attention(proj, p['na_bias'][j], p['cache_na_k'][:, j], p['cache_na_v'][:, j])
    args = (o, proj, 3, p['na_w_out'][j], x, mod, p['final_norm_w'])
    if final:
        x = (_out_proj(*args, True, rows=(0, N_PROMPT_TOK)), _out_proj(*args, True, rows=(N_PROMPT_TOK, N_TOK)))
    else:
        x = _out_proj(*args, False)
    return x, new_k, new_v


def kernel(x_prompt, x_sample, state_ret, state_rwkv, cache_diff_k, cache_diff_v, cache_na_k, cache_na_v,
           c, c_ctx, norm_w, w_mod, b_mod, final_norm_w,
           ret_w_in, ret_decay, ret_gn, ret_w_out,
           rwkv_mu, rwkv_w_in, rwkv_w0, rwkv_wA, rwkv_wB, rwkv_a0, rwkv_aA, rwkv_aB,
           rwkv_kk, rwkv_ka, rwkv_rk, rwkv_gn, rwkv_w_out,
           diff_w_in, diff_lambda, diff_gn, diff_w_out,
           na_w_in, na_bias, na_w_out):
    p = dict(locals())
    cond = jnp.zeros((N_COND, D_MODEL), F32).at[0].set(c_ctx).at[1:1 + DEC_BATCH].set(c)
    mods = _modulation(cond, w_mod, b_mod)
    x = (x_prompt.reshape(N_PROMPT_TOK, D_MODEL), x_sample.reshape(N_SAMPLE_TOK, D_MODEL))
    new = {n: [] for n in ('ret', 'rwkv', 'dk', 'dv', 'nk', 'nv')}
    for i in range(DEPTH):
        kind, j = i % N_MIXERS, i // N_MIXERS
        if kind == 0:
            x, st = _layer_ret(x, p, mods[i], j)
            new['ret'].append(st)
        elif kind == 1:
            x, st = _layer_rwkv(x, p, mods[i], j)
            new['rwkv'].append(st)
        elif kind == 2:
            x, ck, cv = _layer_diff(x, p, mods[i], j, i)
            new['dk'].append(ck)
            new['dv'].append(cv)
        else:
            x, ck, cv = _layer_na(x, p, mods[i], j, final=(i == DEPTH - 1))
            new['nk'].append(ck)
            new['nv'].append(cv)
    if DEPTH % N_MIXERS:
        raise NotImplementedError("the final norm is fused into the last neighbourhood-attention layer")
    cat = lambda xs: xs[0] if len(xs) == 1 else jnp.concatenate(xs, axis=1)
    return (x[0].reshape(BATCH, SEQ, D_MODEL), x[1].reshape(DEC_BATCH, DEC_SEQ, D_MODEL),
            cat(new['ret']), cat(new['rwkv']), cat(new['dk']), cat(new['dv']), cat(new['nk']), cat(new['nv']))
```

```python
import functools
import math

import jax
import jax.numpy as jnp
from jax import lax
from jax.experimental import pallas as pl
from jax.experimental.pallas import tpu as pltpu

F32 = jnp.float32
BF16 = jnp.bfloat16

D_MODEL = 1024
BATCH = 32
SEQ = 256
DEPTH = 4
N_MIXERS = 4
DEC_BATCH = 2
DEC_SEQ = 1024
PAST_LEN = 256
GRID_W = 64

RET_HEADS = 4
RET_DK = 256
RET_DV = 512
RET_QK = 1024
RET_V = 2048

RWKV_HD = 64
RWKV_HEADS = 16
RWKV_RANK = 64

DIFF_HEADS = 8
DIFF_HD = 64

NA_HEADS = 16
NA_HD = 64
NA_WIN_R = 8
NA_WIN_C = 16

ROPE_BASE = 10000.0
EPS = 1e-6
GN_EPS = 1e-5

N_PROMPT_TOK = BATCH * SEQ
N_SAMPLE_TOK = DEC_BATCH * DEC_SEQ
N_TOK = N_PROMPT_TOK + N_SAMPLE_TOK
N_COND = 8

LANES = 128
VMEM_LIMIT = 56 * 2 ** 20


def _params(*sem):
    return pltpu.CompilerParams(dimension_semantics=sem, vmem_limit_bytes=VMEM_LIMIT)


def _cond_of_tile(i, tm):
    npt = N_PROMPT_TOK // tm
    return jnp.where(i < npt, 0, 1 + (i - npt) // (DEC_SEQ // tm))


def _sigmoid(x):
    return 1.0 / (1.0 + jnp.exp(-x))


def _silu(x):
    return x * _sigmoid(x)


def _dot(a, b):
    return jnp.dot(a, b, preferred_element_type=F32)


def _dot_nt(a, b):
    return lax.dot_general(a, b, (((1,), (1,)), ((), ())), preferred_element_type=F32)


def _dot_tn(a, b):
    return lax.dot_general(a, b, (((0,), (0,)), ((), ())), preferred_element_type=F32)


def _softmax_rows(s):
    m = jnp.max(s, axis=-1, keepdims=True)
    e = jnp.exp(s - m)
    return e / jnp.sum(e, axis=-1, keepdims=True)


def _mod_kernel(c_ref, w_ref, b_ref, o_ref):
    s = _silu(c_ref[...])
    o_ref[0] = jnp.dot(s, w_ref[0], precision=lax.Precision.HIGHEST, preferred_element_type=F32) + b_ref[0]


def _modulation(cond, w_mod, b_mod):
    tn = D_MODEL
    out = pl.pallas_call(
        _mod_kernel,
        grid=(DEPTH, 3 * D_MODEL // tn),
        in_specs=[
            pl.BlockSpec((N_COND, D_MODEL), lambda l, j: (0, 0)),
            pl.BlockSpec((1, D_MODEL, tn), lambda l, j: (l, 0, j)),
            pl.BlockSpec((1, 1, tn), lambda l, j: (l, 0, j)),
        ],
        out_specs=pl.BlockSpec((1, N_COND, tn), lambda l, j: (l, 0, j)),
        out_shape=jax.ShapeDtypeStruct((DEPTH, N_COND, 3 * D_MODEL), F32),
        compiler_params=_params("arbitrary", "arbitrary"),
        name="modulation",
    )(cond, w_mod, b_mod.reshape(DEPTH, 1, 3 * D_MODEL))
    return out.reshape(DEPTH, N_COND, 3, 1, D_MODEL)


def _norm_mod(x, nw, mod_ref):
    ms = jnp.mean(x * x, axis=-1, keepdims=True)
    y = x * lax.rsqrt(ms + EPS) * nw
    return y * (1.0 + mod_ref[0, 1]) + mod_ref[0, 0]


IN_TM = 1024


def _x_specs(x, tm, tile_of):
    if not isinstance(x, tuple):
        return [pl.BlockSpec((tm, D_MODEL), lambda *g: (tile_of(*g), 0))], (x,)
    npt = N_PROMPT_TOK // tm
    return [pl.BlockSpec((tm, D_MODEL), lambda *g: (jnp.minimum(tile_of(*g), npt - 1), 0)),
            pl.BlockSpec((tm, D_MODEL), lambda *g: (jnp.maximum(tile_of(*g) - npt, 0), 0))], x


def _read_x(x_refs, tile, tm):
    if len(x_refs) == 1:
        return x_refs[0][...]
    return jnp.where(tile < N_PROMPT_TOK // tm, x_refs[0][...], x_refs[1][...])


def _in_proj_kernel(*refs, n_x):
    x_refs = refs[:n_x]
    nw_ref, mod_ref, w_ref, o_ref, h_ref = refs[n_x:]

    @pl.when(pl.program_id(1) == 0)
    def _():
        x = _read_x(x_refs, pl.program_id(0), IN_TM)
        h_ref[...] = _norm_mod(x, nw_ref[...], mod_ref).astype(BF16)

    o_ref[...] = _dot(h_ref[...], w_ref[...]).astype(o_ref.dtype)


def _in_proj(x, norm_w, mod, w, tn, out_dtype=F32):
    n = w.shape[1]
    w = w.astype(BF16)
    x_specs, xs = _x_specs(x, IN_TM, lambda i, j: i)
    return pl.pallas_call(
        functools.partial(_in_proj_kernel, n_x=len(xs)),
        grid=(N_TOK // IN_TM, n // tn),
        in_specs=x_specs + [
            pl.BlockSpec((1, D_MODEL), lambda i, j: (0, 0)),
            pl.BlockSpec((1, 3, 1, D_MODEL), lambda i, j: (_cond_of_tile(i, IN_TM), 0, 0, 0)),
            pl.BlockSpec((D_MODEL, tn), lambda i, j: (0, j)),
        ],
        out_specs=pl.BlockSpec((IN_TM, tn), lambda i, j: (i, j)),
        out_shape=jax.ShapeDtypeStruct((N_TOK, n), out_dtype),
        scratch_shapes=[pltpu.VMEM((IN_TM, D_MODEL), BF16)],
        compiler_params=_params("arbitrary", "arbitrary"),
        name="in_proj",
    )(*xs, norm_w.reshape(1, D_MODEL), mod, w)


def _in_proj_kv_kernel(x_ref, nw_ref, mod_ref, w_ref, o_ref, ck_ref, cv_ref, h_ref, *, heads):
    i = pl.program_id(0)
    j = pl.program_id(1)

    @pl.when(j == 0)
    def _():
        h_ref[...] = _norm_mod(x_ref[...], nw_ref[...], mod_ref).astype(BF16)

    acc = _dot(h_ref[...], w_ref[...])
    o_ref[...] = acc.astype(o_ref.dtype)
    hd = D_MODEL // heads
    for col, c_ref in ((1, ck_ref), (2, cv_ref)):
        @pl.when((j == col) & (i < N_PROMPT_TOK // IN_TM))
        def _(c_ref=c_ref):
            for s in range(IN_TM // SEQ):
                for h in range(heads):
                    c_ref[s, 0, h] = acc[s * SEQ:(s + 1) * SEQ, h * hd:(h + 1) * hd]


def _in_proj_kv(x, norm_w, mod, w, heads):
    n = w.shape[1]
    tn = D_MODEL
    spb = IN_TM // SEQ
    last = N_PROMPT_TOK // IN_TM - 1
    cache = pl.BlockSpec((spb, 1, heads, SEQ, D_MODEL // heads), lambda i, j: (jnp.minimum(i, last), 0, 0, 0, 0))
    cache_shape = jax.ShapeDtypeStruct((BATCH, 1, heads, SEQ, D_MODEL // heads), F32)
    return pl.pallas_call(
        functools.partial(_in_proj_kv_kernel, heads=heads),
        grid=(N_TOK // IN_TM, n // tn),
        in_specs=[
            pl.BlockSpec((IN_TM, D_MODEL), lambda i, j: (i, 0)),
            pl.BlockSpec((1, D_MODEL), lambda i, j: (0, 0)),
            pl.BlockSpec((1, 3, 1, D_MODEL), lambda i, j: (_cond_of_tile(i, IN_TM), 0, 0, 0)),
            pl.BlockSpec((D_MODEL, tn), lambda i, j: (0, j)),
        ],
        out_specs=[pl.BlockSpec((IN_TM, tn), lambda i, j: (i, j)), cache, cache],
        out_shape=[jax.ShapeDtypeStruct((N_TOK, n), BF16), cache_shape, cache_shape],
        scratch_shapes=[pltpu.VMEM((IN_TM, D_MODEL), BF16)],
        compiler_params=_params("arbitrary", "arbitrary"),
        name="in_proj_kv",
    )(x, norm_w.reshape(1, D_MODEL), mod, w.astype(BF16))


OUT_TM = 256


def _out_proj_kernel(*refs, n_x, t0, final):
    x_refs = refs[:n_x]
    o_ref, g_ref, w_ref, mod_ref, fw_ref, y_ref, wb_ref = refs[n_x:]

    @pl.when(pl.program_id(0) == 0)
    def _():
        wb_ref[...] = w_ref[...].astype(BF16)

    a = (o_ref[...] * _silu(g_ref[...].astype(F32))).astype(BF16)
    xn = _read_x(x_refs, t0 + pl.program_id(0), OUT_TM) + mod_ref[0, 2] * _dot(a, wb_ref[...])
    if final:
        ms = jnp.mean(xn * xn, axis=-1, keepdims=True)
        xn = xn * lax.rsqrt(ms + EPS) * fw_ref[...]
    y_ref[...] = xn


def _out_proj(o, g_arr, g_blk, w, x, mod, final_w, final, rows=(0, N_TOK), g_row0=0):
    k = w.shape[0]
    t0 = rows[0] // OUT_TM
    g0 = g_row0 // OUT_TM
    x_specs, xs = _x_specs(x, OUT_TM, lambda i: t0 + i)
    return pl.pallas_call(
        functools.partial(_out_proj_kernel, n_x=len(xs), t0=t0, final=final),
        grid=((rows[1] - rows[0]) // OUT_TM,),
        in_specs=x_specs + [
            pl.BlockSpec((OUT_TM, k), lambda i: (t0 + i, 0)),
            pl.BlockSpec((OUT_TM, k), lambda i: (g0 + t0 + i, g_blk)),
            pl.BlockSpec((k, D_MODEL), lambda i: (0, 0)),
            pl.BlockSpec((1, 3, 1, D_MODEL), lambda i: (_cond_of_tile(t0 + i, OUT_TM), 0, 0, 0)),
            pl.BlockSpec((1, D_MODEL), lambda i: (0, 0)),
        ],
        out_specs=pl.BlockSpec((OUT_TM, D_MODEL), lambda i: (i, 0)),
        out_shape=jax.ShapeDtypeStruct((rows[1] - rows[0], D_MODEL), F32),
        scratch_shapes=[pltpu.VMEM((k, D_MODEL), BF16)],
        compiler_params=_params("arbitrary"),
        name="out_proj",
    )(*xs, o, g_arr, w, mod, final_w.reshape(1, D_MODEL))


def _rope_tables(d):
    q = d // 4
    t = jnp.arange(DEC_SEQ)
    row = (t // GRID_W).astype(F32)
    col = (t % GRID_W).astype(F32)
    inv = ROPE_BASE ** (-jnp.arange(0, 2 * q, 2, dtype=F32) / (2 * q))
    ar = row[:, None] * inv[None, :]
    ac = col[:, None] * inv[None, :]
    z = jnp.zeros_like(ar)
    cos = jnp.concatenate([jnp.cos(ar), jnp.cos(ar), jnp.cos(ac), jnp.cos(ac)], axis=-1)
    sin_lo = jnp.concatenate([-jnp.sin(ar), z, -jnp.sin(ac), z], axis=-1)
    sin_hi = jnp.concatenate([z, jnp.sin(ar), z, jnp.sin(ac)], axis=-1)
    return cos, sin_lo, sin_hi


def _rope(x, cos, sin_lo, sin_hi, q):
    w = x.shape[-1]
    x_next = pltpu.roll(x, w - q, axis=1)
    x_prev = pltpu.roll(x, q, axis=1)
    return x * cos + x_next * sin_lo + x_prev * sin_hi


RET_QB = 256


def _ret_kernel(lg_ref, q_ref, k_ref, v_ref, gn_ref, *rest, seq, latent):
    if latent:
        cos_ref, slo_ref, shi_ref, s0_ref, _prev_ref, o_ref = rest
    else:
        o_ref, st_ref, dec_ref = rest
    h = pl.program_id(1 if latent else 0)
    lgf = lg_ref[0, h]
    lgb = lg_ref[1, h]
    q = q_ref[...].astype(F32)
    k = k_ref[...].astype(F32)
    if latent:
        q = _rope(q, cos_ref[...], slo_ref[...], shi_ref[...], RET_DK // 4)
        k = _rope(k, cos_ref[...], slo_ref[...], shi_ref[...], RET_DK // 4)
    k = k * (RET_DK ** -0.5)
    kb = k.astype(BF16)
    vb = v_ref[...].astype(BF16)
    gn = gn_ref[...]

    def decay(qi):
        ii = lax.broadcasted_iota(jnp.int32, (RET_QB, seq), 0) + qi * RET_QB
        jj = lax.broadcasted_iota(jnp.int32, (RET_QB, seq), 1)
        gap = (ii - jj).astype(F32)
        return (jnp.where(gap >= 0, jnp.exp(lgf * jnp.maximum(gap, 0.0)), 0.0)
                + jnp.where(gap <= 0, jnp.exp(lgb * jnp.maximum(-gap, 0.0)), 0.0))

    if not latent:
        @pl.when(pl.program_id(1) == 0)
        def _():
            dec_ref[...] = decay(0)

    for qi in range(seq // RET_QB):
        qblk = q[qi * RET_QB:(qi + 1) * RET_QB]
        s = _dot_nt(qblk.astype(BF16), kb)
        dec = decay(qi) if latent else dec_ref[...]
        o = _dot((s * dec).astype(BF16), vb)
        if latent:
            pos = (lax.broadcasted_iota(jnp.int32, (RET_QB, 1), 0) + qi * RET_QB).astype(F32)
            qf = qblk * jnp.exp(lgf * (pos + 1.0))
            qr = qblk * jnp.exp(lgb * (seq - pos))
            o = o + _dot(qf.astype(BF16), s0_ref[0, 0, 0, 0].astype(BF16))
            o = o + _dot(qr.astype(BF16), s0_ref[0, 0, 1, 0].astype(BF16))
        oc = o - jnp.mean(o, axis=-1, keepdims=True)
        o = oc * lax.rsqrt(jnp.mean(oc * oc, axis=-1, keepdims=True) + GN_EPS) * gn
        o_ref[qi * RET_QB:(qi + 1) * RET_QB, :] = o
    if not latent:
        pos = lax.broadcasted_iota(jnp.int32, (seq, 1), 0).astype(F32)
        kf = k * jnp.exp(lgf * (seq - 1.0 - pos))
        kr = k * jnp.exp(lgb * pos)
        st_ref[0, 0, 0, 0] = _dot_tn(kf.astype(BF16), vb)
        st_ref[0, 0, 1, 0] = _dot_tn(kr.astype(BF16), vb)


def _retention(p, log_g, gn_w, state_ret, j):
    smem = pl.BlockSpec(memory_space=pltpu.SMEM)
    gn = gn_w.reshape(1, RET_V)
    kq = RET_QK // RET_DK
    o_p, st = pl.pallas_call(
        functools.partial(_ret_kernel, seq=SEQ, latent=False),
        grid=(RET_HEADS, BATCH),
        in_specs=[
            smem,
            pl.BlockSpec((SEQ, RET_DK), lambda h, b: (b, h)),
            pl.BlockSpec((SEQ, RET_DK), lambda h, b: (b, kq + h)),
            pl.BlockSpec((SEQ, RET_DV), lambda h, b: (b, kq + h)),
            pl.BlockSpec((1, RET_DV), lambda h, b: (0, h)),
        ],
        out_specs=[
            pl.BlockSpec((SEQ, RET_DV), lambda h, b: (b, h)),
            pl.BlockSpec((1, 1, 2, 1, RET_DK, RET_DV), lambda h, b: (b, 0, 0, h, 0, 0)),
        ],
        out_shape=[
            jax.ShapeDtypeStruct((N_TOK, RET_V), F32),
            jax.ShapeDtypeStruct((BATCH, 1, 2, RET_HEADS, RET_DK, RET_DV), F32),
        ],
        scratch_shapes=[pltpu.VMEM((RET_QB, SEQ), F32)],
        compiler_params=_params("arbitrary", "arbitrary"),
        name="retention_prompt",
    )(log_g, p, p, p, gn)
    cos, slo, shi = _rope_tables(RET_DK)
    rb = N_PROMPT_TOK // DEC_SEQ
    full = pl.BlockSpec((DEC_SEQ, RET_DK), lambda b, h: (0, 0))
    o = pl.pallas_call(
        functools.partial(_ret_kernel, seq=DEC_SEQ, latent=True),
        grid=(DEC_BATCH, RET_HEADS),
        in_specs=[
            smem,
            pl.BlockSpec((DEC_SEQ, RET_DK), lambda b, h: (rb + b, h)),
            pl.BlockSpec((DEC_SEQ, RET_DK), lambda b, h: (rb + b, kq + h)),
            pl.BlockSpec((DEC_SEQ, RET_DV), lambda b, h: (rb + b, kq + h)),
            pl.BlockSpec((1, RET_DV), lambda b, h: (0, h)),
            full, full, full,
            pl.BlockSpec((1, 1, 2, 1, RET_DK, RET_DV), lambda b, h: (b, j, 0, h, 0, 0)),
            pl.BlockSpec(memory_space=pl.ANY),
        ],
        out_specs=pl.BlockSpec((DEC_SEQ, RET_DV), lambda b, h: (rb + b, h)),
        out_shape=jax.ShapeDtypeStruct((N_TOK, RET_V), F32),
        input_output_aliases={9: 0},
        compiler_params=_params("arbitrary", "arbitrary"),
        name="retention_latent",
    )(log_g, p, p, p, gn, cos, slo, shi, state_ret, o_p)
    return o, st


def _layer_ret(x, p, mod, j):
    i = N_MIXERS * j + 0
    proj = _in_proj(x, p['norm_w'][i], mod, p['ret_w_in'][j], 1024, out_dtype=BF16)
    log_g = jax.nn.log_sigmoid(p['ret_decay'][j].astype(F32))
    o, st = _retention(proj, log_g, p['ret_gn'][j], p['state_ret'], j)
    x = _out_proj(o, proj, (2 * RET_QK + RET_V) // RET_V, p['ret_w_out'][j], x, mod, p['final_norm_w'], False)
    return x, st


RW_TM = 512
RW_HALO = 8
RW_C = 64
RW_LOCK = 4
RW_PAIRS = 2


def _rwkv_prep_kernel(x_ref, xp_ref, xn_ref, nw_ref, mod_ref, mu_ref, wa_ref, aa_ref, wb_ref, ab_ref,
                      w0_ref, a0_ref, xm_ref, lw_ref, a_ref):
    i = pl.program_id(0)
    nw = nw_ref[...]
    h = _norm_mod(x_ref[...], nw, mod_ref)
    h_before = _norm_mod(xp_ref[RW_HALO - 1:RW_HALO, :], nw, mod_ref)
    h_after = _norm_mod(xn_ref[0:1, :], nw, mod_ref)
    seq = jnp.where(i < N_PROMPT_TOK // RW_TM, SEQ, DEC_SEQ)
    row = lax.broadcasted_iota(jnp.int32, (RW_TM, 1), 0)
    t = (row + i * RW_TM) & (seq - 1)
    prev = jnp.where(row == 0, h_before, pltpu.roll(h, 1, axis=0))
    nxt = jnp.where(row == RW_TM - 1, h_after, pltpu.roll(h, RW_TM - 1, axis=0))
    prev = jnp.where(t == 0, 0.0, prev)
    nxt = jnp.where(t == seq - 1, 0.0, nxt)
    xx = 0.5 * (prev + nxt) - h
    for n, m in enumerate((0, 2, 3, 5)):
        xm_ref[n] = (h + xx * mu_ref[m:m + 1, :]).astype(BF16)
    xw = (h + xx * mu_ref[1:2, :]).astype(BF16)
    xa = (h + xx * mu_ref[4:5, :]).astype(BF16)
    lw = jnp.tanh(_dot(xw, wa_ref[...])).astype(BF16)
    la = _dot(xa, aa_ref[...]).astype(BF16)
    for dr in range(2):
        wl = w0_ref[dr:dr + 1, :] + _dot(lw, wb_ref[dr])
        lw_ref[dr] = -math.exp(-0.5) * _sigmoid(wl)
        a_ref[dr] = _sigmoid(a0_ref[dr:dr + 1, :] + _dot(la, ab_ref[dr]))


def _rwkv_prep(x, norm_w, mod, mu, wa2, aa2, wb_pad, ab_pad, w0, a0):
    nt = N_TOK // RW_TM
    hb = RW_TM // RW_HALO
    last = N_TOK // RW_HALO - 1
    full2 = lambda shape: pl.BlockSpec(shape, lambda i: (0, 0))
    full3 = lambda shape: pl.BlockSpec(shape, lambda i: (0, 0, 0))
    return pl.pallas_call(
        _rwkv_prep_kernel,
        grid=(nt,),
        in_specs=[
            pl.BlockSpec((RW_TM, D_MODEL), lambda i: (i, 0)),
            pl.BlockSpec((RW_HALO, D_MODEL), lambda i: (jnp.maximum(i * hb - 1, 0), 0)),
            pl.BlockSpec((RW_HALO, D_MODEL), lambda i: (jnp.minimum((i + 1) * hb, last), 0)),
            full2((1, D_MODEL)),
            pl.BlockSpec((1, 3, 1, D_MODEL), lambda i: (_cond_of_tile(i, RW_TM), 0, 0, 0)),
            full2((6, D_MODEL)),
            full2((D_MODEL, 2 * RWKV_RANK)),
            full2((D_MODEL, 2 * RWKV_RANK)),
            full3((2, 2 * RWKV_RANK, D_MODEL)),
            full3((2, 2 * RWKV_RANK, D_MODEL)),
            full2((2, D_MODEL)),
            full2((2, D_MODEL)),
        ],
        out_specs=[
            pl.BlockSpec((4, RW_TM, D_MODEL), lambda i: (0, i, 0)),
            pl.BlockSpec((2, RW_TM, D_MODEL), lambda i: (0, i, 0)),
            pl.BlockSpec((2, RW_TM, D_MODEL), lambda i: (0, i, 0)),
        ],
        out_shape=[
            jax.ShapeDtypeStruct((4, N_TOK, D_MODEL), BF16),
            jax.ShapeDtypeStruct((2, N_TOK, D_MODEL), F32),
            jax.ShapeDtypeStruct((2, N_TOK, D_MODEL), F32),
        ],
        compiler_params=_params("arbitrary"),
        name="rwkv_prep",
    )(x, x, x, norm_w.reshape(1, D_MODEL), mod, mu, wa2, aa2, wb_pad, ab_pad, w0, a0)


def _bmm_kernel(a_ref, w_ref, o_ref):
    o_ref[0] = _dot(a_ref[0], w_ref[...])


def _rwkv_rkvg(xm, w):
    tm = 1024
    return pl.pallas_call(
        _bmm_kernel,
        grid=(4, N_TOK // tm),
        in_specs=[
            pl.BlockSpec((1, tm, D_MODEL), lambda n, i: (n, i, 0)),
            pl.BlockSpec((D_MODEL, D_MODEL), lambda n, i: (0, n)),
        ],
        out_specs=pl.BlockSpec((1, tm, D_MODEL), lambda n, i: (n, i, 0)),
        out_shape=jax.ShapeDtypeStruct((4, N_TOK, D_MODEL), F32),
        compiler_params=_params("arbitrary", "arbitrary"),
        name="rwkv_rkvg",
    )(xm, w)


def _head_sum(x, first):
    s0 = jnp.sum(jnp.where(first, x, 0.0), axis=-1, keepdims=True)
    s1 = jnp.sum(jnp.where(first, 0.0, x), axis=-1, keepdims=True)
    return jnp.where(first, s0, s1)


def _stack_heads(x, first):
    return jnp.concatenate([jnp.where(first, x, 0.0), jnp.where(first, 0.0, x)], axis=0)


def _cumsum_rows(tri, x):
    hi = x.astype(BF16)
    r1 = x - hi.astype(F32)
    mid = r1.astype(BF16)
    lo = (r1 - mid.astype(F32)).astype(BF16)
    return _dot(tri, hi) + _dot(tri, mid) + _dot(tri, lo)


def _rwkv_chunk_kernel(*refs, seq, zero_init):
    if zero_init:
        (rkv_ref, lw_ref, a_ref, kkp_ref, kap_ref, rkp_ref, gn_ref, o_ref, st_ref,
         kk_scr, y_scr, tar_scr, lrb_scr, b2_scr, w2_scr, yl_scr, kv_scr, pc_scr) = refs
    else:
        (rkv_ref, lw_ref, a_ref, kkp_ref, kap_ref, rkp_ref, gn_ref, s0_ref, _prev_ref, o_ref,
         kk_scr, y_scr, tar_scr, lrb_scr, b2_scr, w2_scr, yl_scr, kv_scr, pc_scr) = refs
    c_len = RW_C
    n_ch = seq // c_len
    rows2 = 2 * c_len
    first = _first_half_lanes()
    kap = kap_ref[...]

    kk = rkv_ref[1] * kkp_ref[...]
    kk_scr[...] = kk * lax.rsqrt(jnp.maximum(_head_sum(kk * kk, first), 1e-12))

    rr = lax.broadcasted_iota(jnp.int32, (rows2, rows2), 0)
    cc = lax.broadcasted_iota(jnp.int32, (rows2, rows2), 1)
    eye = (rr == cc).astype(F32)
    tr = lax.broadcasted_iota(jnp.int32, (c_len, c_len), 0)
    tc = lax.broadcasted_iota(jnp.int32, (c_len, c_len), 1)

    def same(shift):
        return (rr >> shift) == (cc >> shift)

    head = same(6)
    strict = (head & (cc < rr), head & (cc > rr))
    incl = (head & (cc <= rr), head & (cc >= rr))
    tri = ((tc <= tr).astype(BF16), (tc >= tr).astype(BF16))
    last = (c_len - 1, 0)

    def phase1(chains):
        dirs = [dr for dr, _ in chains]
        rows = [pl.ds(pl.multiple_of(c * c_len, c_len), c_len) for _, c in chains]
        lw = [lw_ref[dr, rw, :] for dr, rw in zip(dirs, rows)]
        cum = [_cumsum_rows(tri[dr], x) for dr, x in zip(dirs, lw)]
        a2, r2, b2, k2, v2, pc = [], [], [], [], [], []
        for dr, rw, lw_c, cum_c in zip(dirs, rows, lw, cum):
            a = a_ref[dr, rw, :]
            k = rkv_ref[1, rw, :]
            kk_c = kk_scr[rw, :]
            e_inc = jnp.exp(cum_c)
            e_inv = jnp.exp(-cum_c)
            a2.append(_stack_heads(-kk_c * jnp.exp(cum_c - lw_c), first).astype(BF16))
            r2.append(_stack_heads(rkv_ref[0, rw, :] * e_inc, first).astype(BF16))
            b2.append(_stack_heads(kk_c * a * e_inv, first).astype(BF16))
            k2.append(_stack_heads(k * (1.0 + (a - 1.0) * kap) * e_inv, first).astype(BF16))
            v2.append(_stack_heads(rkv_ref[2, rw, :], first).astype(BF16))
            pc.append(e_inc[last[dr]:last[dr] + 1, :])
        g = [_dot_nt(jnp.concatenate([x, y], axis=0), jnp.concatenate([z, w], axis=0))
             for x, y, z, w in zip(a2, r2, b2, k2)]
        l_ab = [jnp.where(strict[dr], x[:rows2, :rows2], 0.0) for dr, x in zip(dirs, g)]
        t = [eye + jnp.where(same(1), x, 0.0) for x in l_ab]
        side = {}
        for shift in range(1, 6):
            sib = same(shift + 1) & ~same(shift)
            tb = [x.astype(BF16) for x in t]
            mid = [_dot(jnp.where(sib, x, 0.0).astype(BF16), y) for x, y in zip(l_ab, tb)]
            if shift == 1:
                side['lv'] = [_dot(jnp.where(strict[dr], x[:rows2, rows2:], 0.0).astype(BF16), y)
                              for dr, x, y in zip(dirs, g, v2)]
            elif shift == 2:
                side['yl'] = [_dot(jnp.where(incl[dr], x[rows2:, rows2:], 0.0).astype(BF16), y)
                              for dr, x, y in zip(dirs, g, v2)]
            elif shift == 3:
                side['kv'] = [_dot_tn(x, y) for x, y in zip(v2, k2)]
            t = [x + _dot(y, z.astype(BF16)) for x, y, z in zip(t, tb, mid)]
        tb = [x.astype(BF16) for x in t]
        ta = [_dot(x, y) for x, y in zip(tb, a2)]
        w2 = [_dot(x, y.astype(BF16)) for x, y in zip(tb, side['lv'])]
        for i, (dr, c) in enumerate(chains):
            n = dr * n_ch + c
            tar_scr[n, :rows2, :] = ta[i].astype(BF16)
            tar_scr[n, rows2:, :] = r2[i]
            w2_scr[n] = w2[i]
            yl_scr[n] = side['yl'][i]
            kv_scr[n] = side['kv'][i]
            lrb_scr[n] = jnp.where(incl[dr], g[i][rows2:, :rows2], 0.0).astype(BF16)
            b2_scr[n] = b2[i]
            pc_scr[n] = pc[i]

    def body1(grp, carry):
        phase1([(dr, grp * RW_LOCK + j) for j in range(RW_LOCK) for dr in range(2)])
        return carry

    if n_ch == RW_LOCK:
        body1(0, 0)
    else:
        lax.fori_loop(0, n_ch // RW_LOCK, body1, 0)

    def body2(i, carry):
        cs = (i, n_ch - 1 - i)
        ns = [dr * n_ch + c for dr, c in enumerate(cs)]
        x = [_dot_nt(tar_scr[n], s2.astype(BF16)) for n, s2 in zip(ns, carry)]
        u2 = [(xx[:rows2] + w2_scr[n]).astype(BF16) for n, xx in zip(ns, x)]
        upd = [_dot_tn(u, b2_scr[n]) for n, u in zip(ns, u2)]
        yb = [_dot(lrb_scr[n], u) for n, u in zip(ns, u2)]
        out = []
        for dr, (c, n) in enumerate(zip(cs, ns)):
            y2 = x[dr][rows2:] + yb[dr] + yl_scr[n]
            y_scr[dr, pl.ds(pl.multiple_of(c * c_len, c_len), c_len), :] = y2[:c_len] + y2[c_len:]
            out.append((carry[dr] + upd[dr] + kv_scr[n]) * pc_scr[n])
        return tuple(out)

    if zero_init:
        init = (jnp.zeros((rows2, LANES), F32),) * 2
    else:
        init = (s0_ref[0, 0, 0], s0_ref[0, 1, 0])
    s_f, s_b = lax.fori_loop(0, n_ch, body2, init)

    y = y_scr[0] + y_scr[1]
    yc = y - _head_sum(y, first) * (1.0 / RWKV_HD)
    o = yc * lax.rsqrt(_head_sum(yc * yc, first) * (1.0 / RWKV_HD) + GN_EPS) * gn_ref[...]
    r = rkv_ref[0]
    k = rkv_ref[1]
    v = rkv_ref[2]
    for dr in range(2):
        kd = k * (1.0 + (a_ref[dr] - 1.0) * kap)
        o = o + _head_sum(r * kd * rkp_ref[...], first) * v
    o_ref[...] = o

    if zero_init:
        for dr, s2 in enumerate((s_f, s_b)):
            st_ref[0, 0, dr, 0] = s2[:RWKV_HD, :RWKV_HD]
            st_ref[0, 0, dr, 1] = s2[RWKV_HD:, RWKV_HD:]


def _rwkv_chunked(rkvg, lw, a, kkp, kap, rkp, gn, s0_pairs):
    npair = RWKV_HEADS // 2
    par = lambda *_: pl.BlockSpec((1, LANES), lambda s, p: (0, p))

    def scratch(seq):
        n = 2 * (seq // RW_C)
        r2 = 2 * RW_C
        return [
            pltpu.VMEM((seq, LANES), F32), pltpu.VMEM((2, seq, LANES), F32),
            pltpu.VMEM((n, 2 * r2, LANES), BF16), pltpu.VMEM((n, r2, r2), BF16), pltpu.VMEM((n, r2, LANES), BF16),
            pltpu.VMEM((n, r2, LANES), F32), pltpu.VMEM((n, r2, LANES), F32), pltpu.VMEM((n, r2, LANES), F32),
            pltpu.VMEM((n, 1, LANES), F32),
        ]

    def seq_specs(seq, rb):
        return [
            pl.BlockSpec((3, seq, LANES), lambda s, p: (0, rb + s, p)),
            pl.BlockSpec((2, seq, LANES), lambda s, p: (0, rb + s, p)),
            pl.BlockSpec((2, seq, LANES), lambda s, p: (0, rb + s, p)),
            par(), par(), par(), par(),
        ]

    pars = [u.reshape(1, D_MODEL) for u in (kkp, kap, rkp, gn)]
    o_p, st = pl.pallas_call(
        functools.partial(_rwkv_chunk_kernel, seq=SEQ, zero_init=True),
        grid=(BATCH, npair),
        in_specs=seq_specs(SEQ, 0),
        out_specs=[
            pl.BlockSpec((SEQ, LANES), lambda s, p: (s, p)),
            pl.BlockSpec((1, 1, 2, 2, RWKV_HD, RWKV_HD), lambda s, p: (s, 0, 0, p, 0, 0)),
        ],
        out_shape=[
            jax.ShapeDtypeStruct((N_TOK, D_MODEL), F32),
            jax.ShapeDtypeStruct((BATCH, 1, 2, RWKV_HEADS, RWKV_HD, RWKV_HD), F32),
        ],
        scratch_shapes=scratch(SEQ),
        compiler_params=_params("arbitrary", "arbitrary"),
        name="rwkv_prompt",
    )(rkvg, lw, a, *pars)
    rb = N_PROMPT_TOK // DEC_SEQ
    o = pl.pallas_call(
        functools.partial(_rwkv_chunk_kernel, seq=DEC_SEQ, zero_init=False),
        grid=(DEC_BATCH, npair),
        in_specs=seq_specs(DEC_SEQ, rb) + [
            pl.BlockSpec((1, 2, 1, LANES, LANES), lambda s, p: (s, 0, p, 0, 0)),
            pl.BlockSpec(memory_space=pl.ANY),
        ],
        out_specs=pl.BlockSpec((DEC_SEQ, LANES), lambda s, p: (rb + s, p)),
        out_shape=jax.ShapeDtypeStruct((N_TOK, D_MODEL), F32),
        input_output_aliases={8: 0},
        scratch_shapes=scratch(DEC_SEQ),
        compiler_params=_params("arbitrary", "arbitrary"),
        name="rwkv_latent",
    )(rkvg, lw, a, *pars, s0_pairs, o_p)
    return o, st


def _split3(x):
    hi = x.astype(BF16)
    r1 = x - hi.astype(F32)
    mid = r1.astype(BF16)
    return hi, mid, (r1 - mid.astype(F32)).astype(BF16)


def _rwkv_kernel(*refs, seq, zero_init, pg, np2):
    n_in = 7 if zero_init else 9
    rkv_ref, lw_ref, a_ref, kkp_ref, kap_ref, rkp_ref, gn_ref = refs[:7]
    if zero_init:
        o_ref, st_ref = refs[n_in:n_in + 2]
        scr = refs[n_in + 2:]
    else:
        s0_ref = refs[7]
        o_ref = refs[n_in]
        scr = refs[n_in + 1:]
    kk_scr, cum_scr, bon_scr, y_scr, s_scr, tar_scr, lrb_scr, b2_scr, w2_scr, yl_scr, kv_scr, pc_scr = scr
    c_len = RW_C
    n_ch = seq // c_len
    rows2 = 2 * c_len
    grp = pl.program_id(1)
    defer = np2 > pg
    base = grp * pg if defer else 0
    first = _first_half_lanes()

    rr = lax.broadcasted_iota(jnp.int32, (rows2, rows2), 0)
    cc = lax.broadcasted_iota(jnp.int32, (rows2, rows2), 1)
    eye = (rr == cc).astype(F32)

    def same(shift):
        return (rr >> shift) == (cc >> shift)

    head = same(6)
    strict = (head & (cc < rr), head & (cc > rr))
    incl = (head & (cc <= rr), head & (cc >= rr))
    last = (c_len - 1, 0)
    head_ones = head.astype(BF16)

    cs_rows = min(seq, 256)
    tr = lax.broadcasted_iota(jnp.int32, (cs_rows, cs_rows), 0)
    tc = lax.broadcasted_iota(jnp.int32, (cs_rows, cs_rows), 1)
    chunk = (tr >> 6) == (tc >> 6)
    tri = ((chunk & (tc <= tr)).astype(BF16), (chunk & (tc >= tr)).astype(BF16))
    for p in range(pg):
        ln = slice(p * LANES, (p + 1) * LANES)
        kap = kap_ref[:, ln]
        r = rkv_ref[0, :, ln]
        k = rkv_ref[1, :, ln]
        v = rkv_ref[2, :, ln]
        kk = k * kkp_ref[:, ln]
        kk_scr[p] = kk * lax.rsqrt(jnp.maximum(_head_sum(kk * kk, first), 1e-12))
        bonus = None
        for dr in range(2):
            kd = k * (1.0 + (a_ref[dr, :, ln] - 1.0) * kap)
            term = _head_sum(r * kd * rkp_ref[:, ln], first) * v
            bonus = term if bonus is None else bonus + term
        bon_scr[base + p] = bonus
    for p in range(0, pg, 2):
        for dr in range(2):
            for r0 in range(0, seq, cs_rows):
                parts = _split3(lw_ref[dr, r0:r0 + cs_rows, p * LANES:(p + 2) * LANES])
                cum = _dot(tri[dr], parts[0]) + _dot(tri[dr], parts[1]) + _dot(tri[dr], parts[2])
                cum_scr[p, dr, r0:r0 + cs_rows, :] = cum[:, :LANES]
                cum_scr[p + 1, dr, r0:r0 + cs_rows, :] = cum[:, LANES:]

    def phase1(chains):
        dirs = [dr for _, dr, _ in chains]
        a2, r2, b2, k2, v2, pc = [], [], [], [], [], []
        for p, dr, c in chains:
            ln = slice(p * LANES, (p + 1) * LANES)
            rw = pl.ds(pl.multiple_of(c * c_len, c_len), c_len)
            a = a_ref[dr, rw, ln]
            k = rkv_ref[1, rw, ln]
            kk_c = kk_scr[p, rw, :]
            cum_c = cum_scr[p, dr, rw, :]
            e_inc = jnp.exp(cum_c)
            e_inv = jnp.exp(-cum_c)
            a2.append(_stack_heads(-kk_c * jnp.exp(cum_c - lw_ref[dr, rw, ln]), first).astype(BF16))
            r2.append(_stack_heads(rkv_ref[0, rw, ln] * e_inc, first).astype(BF16))
            b2.append(_stack_heads(kk_c * a * e_inv, first).astype(BF16))
            k2.append(_stack_heads(k * (1.0 + (a - 1.0) * kap_ref[:, ln]) * e_inv, first).astype(BF16))
            v2.append(_stack_heads(rkv_ref[2, rw, ln], first).astype(BF16))
            pc.append(e_inc[last[dr]:last[dr] + 1, :])
        g = [_dot_nt(jnp.concatenate([x, y], axis=0), jnp.concatenate([z, w], axis=0))
             for x, y, z, w in zip(a2, r2, b2, k2)]
        l_ab = [jnp.where(strict[dr], x[:rows2, :rows2], 0.0) for dr, x in zip(dirs, g)]
        t = [eye + jnp.where(same(1), x, 0.0) for x in l_ab]
        side = {}
        for shift in range(1, 6):
            sib = same(shift + 1) & ~same(shift)
            tb = [x.astype(BF16) for x in t]
            mid = [_dot(jnp.where(sib, x, 0.0).astype(BF16), y) for x, y in zip(l_ab, tb)]
            if shift == 1:
                side['lv'] = [_dot(jnp.where(strict[dr], x[:rows2, rows2:], 0.0).astype(BF16), y)
                              for dr, x, y in zip(dirs, g, v2)]
            elif shift == 2:
                side['yl'] = [_dot(jnp.where(incl[dr], x[rows2:, rows2:], 0.0).astype(BF16), y)
                              for dr, x, y in zip(dirs, g, v2)]
            elif shift == 3:
                side['kv'] = [_dot_tn(x, y) for x, y in zip(v2, k2)]
            t = [x + _dot(y, z.astype(BF16)) for x, y, z in zip(t, tb, mid)]
        tb = [x.astype(BF16) for x in t]
        ta = [_dot(x, y) for x, y in zip(tb, a2)]
        w2 = [_dot(x, y.astype(BF16)) for x, y in zip(tb, side['lv'])]
        for i, (p, dr, c) in enumerate(chains):
            n = ((base + p) * 2 + dr) * n_ch + c
            tar_scr[n, :rows2, :] = ta[i].astype(BF16)
            tar_scr[n, rows2:, :] = r2[i]
            w2_scr[n] = w2[i]
            yl_scr[n] = side['yl'][i]
            kv_scr[n] = side['kv'][i]
            lrb_scr[n] = jnp.where(incl[dr], g[i][rows2:, :rows2], 0.0).astype(BF16)
            b2_scr[n] = b2[i]
            pc_scr[n] = pc[i]

    def body1(cg, carry):
        phase1([(p, dr, cg * RW_LOCK + j) for p in range(pg) for j in range(RW_LOCK) for dr in range(2)])
        return carry

    if n_ch == RW_LOCK:
        body1(0, 0)
    else:
        lax.fori_loop(0, n_ch // RW_LOCK, body1, 0)

    def finish():
        for p in range(np2):
            for dr in range(2):
                if zero_init:
                    s_scr[2 * p + dr] = jnp.zeros((rows2, LANES), F32)
                else:
                    s_scr[2 * p + dr] = s0_ref[0, dr, p]

        def body2(i, carry):
            cs = (i, n_ch - 1 - i)
            ids = [(p, dr) for p in range(np2) for dr in range(2)]
            ns = [(p * 2 + dr) * n_ch + cs[dr] for p, dr in ids]
            x = [_dot_nt(tar_scr[n], s_scr[2 * p + dr].astype(BF16)) for n, (p, dr) in zip(ns, ids)]
            u2 = [(xx[:rows2] + w2_scr[n]).astype(BF16) for n, xx in zip(ns, x)]
            upd = [_dot_tn(u, b2_scr[n]) for n, u in zip(ns, u2)]
            yb = [_dot(lrb_scr[n], u) for n, u in zip(ns, u2)]
            for j, (n, (p, dr)) in enumerate(zip(ns, ids)):
                y2 = x[j][rows2:] + yb[j] + yl_scr[n]
                y_scr[p, dr, pl.ds(pl.multiple_of(cs[dr] * c_len, c_len), c_len), :] = y2[:c_len] + y2[c_len:]
                s_scr[2 * p + dr] = (s_scr[2 * p + dr] + upd[j] + kv_scr[n]) * pc_scr[n]
            return carry

        lax.fori_loop(0, n_ch, body2, 0)

        def head_mean(xs):
            parts = [_split3(x) for x in xs]
            return [(_dot(a, head_ones) + _dot(b, head_ones) + _dot(c, head_ones)) * (1.0 / RWKV_HD)
                    for a, b, c in parts]

        ys = [y_scr[p, 0] + y_scr[p, 1] for p in range(np2)]
        yc = [y - m for y, m in zip(ys, head_mean(ys))]
        var = head_mean([x * x for x in yc])
        for p in range(np2):
            ln = slice(p * LANES, (p + 1) * LANES)
            o_ref[:, ln] = yc[p] * lax.rsqrt(var[p] + GN_EPS) * gn_ref[:, ln] + bon_scr[p]
            if zero_init:
                for dr in range(2):
                    s2 = s_scr[2 * p + dr]
                    st_ref[0, 0, dr, 2 * p] = s2[:RWKV_HD, :RWKV_HD]
                    st_ref[0, 0, dr, 2 * p + 1] = s2[RWKV_HD:, RWKV_HD:]

    if defer:
        pl.when(grp == pl.num_programs(1) - 1)(finish)
    else:
        finish()


def _rwkv_mixer(rkvg, lw, a, kkp, kap, rkp, gn, s0_pairs):
    npair = RWKV_HEADS // 2
    pg = RW_PAIRS
    wl = pg * LANES
    r2 = 2 * RW_C

    def scratch(seq, np2):
        n = 2 * np2 * (seq // RW_C)
        return [
            pltpu.VMEM((pg, seq, LANES), F32), pltpu.VMEM((pg, 2, seq, LANES), F32),
            pltpu.VMEM((np2, seq, LANES), F32), pltpu.VMEM((np2, 2, seq, LANES), F32),
            pltpu.VMEM((2 * np2, r2, LANES), F32),
            pltpu.VMEM((n, 2 * r2, LANES), BF16), pltpu.VMEM((n, r2, r2), BF16), pltpu.VMEM((n, r2, LANES), BF16),
            pltpu.VMEM((n, r2, LANES), F32), pltpu.VMEM((n, r2, LANES), F32), pltpu.VMEM((n, r2, LANES), F32),
            pltpu.VMEM((n, 1, LANES), F32),
        ]

    def seq_specs(seq, rb, gn_spec):
        par = pl.BlockSpec((1, wl), lambda s, g: (0, g))
        return [
            pl.BlockSpec((3, seq, wl), lambda s, g: (0, rb + s, g)),
            pl.BlockSpec((2, seq, wl), lambda s, g: (0, rb + s, g)),
            pl.BlockSpec((2, seq, wl), lambda s, g: (0, rb + s, g)),
            par, par, par, gn_spec,
        ]

    pars = [u.reshape(1, D_MODEL) for u in (kkp, kap, rkp, gn)]
    o_p, st = pl.pallas_call(
        functools.partial(_rwkv_kernel, seq=SEQ, zero_init=True, pg=pg, np2=npair),
        grid=(BATCH, npair // pg),
        in_specs=seq_specs(SEQ, 0, pl.BlockSpec((1, D_MODEL), lambda s, g: (0, 0))),
        out_specs=[
            pl.BlockSpec((SEQ, D_MODEL), lambda s, g: (s, 0)),
            pl.BlockSpec((1, 1, 2, RWKV_HEADS, RWKV_HD, RWKV_HD), lambda s, g: (s, 0, 0, 0, 0, 0)),
        ],
        out_shape=[
            jax.ShapeDtypeStruct((N_TOK, D_MODEL), F32),
            jax.ShapeDtypeStruct((BATCH, 1, 2, RWKV_HEADS, RWKV_HD, RWKV_HD), F32),
        ],
        scratch_shapes=scratch(SEQ, npair),
        compiler_params=_params("arbitrary", "arbitrary"),
        name="rwkv_prompt",
    )(rkvg, lw, a, *pars)
    rb = N_PROMPT_TOK // DEC_SEQ
    o = pl.pallas_call(
        functools.partial(_rwkv_kernel, seq=DEC_SEQ, zero_init=False, pg=pg, np2=pg),
        grid=(DEC_BATCH, npair // pg),
        in_specs=seq_specs(DEC_SEQ, rb, pl.BlockSpec((1, wl), lambda s, g: (0, g))) + [
            pl.BlockSpec((1, 2, pg, LANES, LANES), lambda s, g: (s, 0, g, 0, 0)),
            pl.BlockSpec(memory_space=pl.ANY),
        ],
        out_specs=pl.BlockSpec((DEC_SEQ, wl), lambda s, g: (rb + s, g)),
        out_shape=jax.ShapeDtypeStruct((N_TOK, D_MODEL), F32),
        input_output_aliases={8: 0},
        scratch_shapes=scratch(DEC_SEQ, pg),
        compiler_params=_params("arbitrary", "arbitrary"),
        name="rwkv_latent",
    )(rkvg, lw, a, *pars, s0_pairs, o_p)
    return o, st


def _state_pairs(s0):
    s = s0.reshape(DEC_BATCH, 2, RWKV_HEADS // 2, 2, RWKV_HD, RWKV_HD)
    z = jnp.zeros_like(s[:, :, :, 0])
    top = jnp.concatenate([s[:, :, :, 0], z], axis=-1)
    bot = jnp.concatenate([z, s[:, :, :, 1]], axis=-1)
    return jnp.concatenate([top, bot], axis=-2)


def _layer_rwkv(x, p, mod, j):
    i = N_MIXERS * j + 1
    wa, wb, aa, ab = p['rwkv_wA'][j], p['rwkv_wB'][j], p['rwkv_aA'][j], p['rwkv_aB'][j]
    z = jnp.zeros_like(wb[0])
    wa2 = jnp.concatenate([wa[0], wa[1]], axis=1).astype(BF16)
    aa2 = jnp.concatenate([aa[0], aa[1]], axis=1).astype(BF16)
    wb_pad = jnp.stack([jnp.concatenate([wb[0], z]), jnp.concatenate([z, wb[1]])]).astype(BF16)
    ab_pad = jnp.stack([jnp.concatenate([ab[0], z]), jnp.concatenate([z, ab[1]])]).astype(BF16)
    xm, lw, a = _rwkv_prep(x, p['norm_w'][i], mod, p['rwkv_mu'][j], wa2, aa2, wb_pad, ab_pad,
                           p['rwkv_w0'][j], p['rwkv_a0'][j])
    rkvg = _rwkv_rkvg(xm, p['rwkv_w_in'][j].astype(BF16))
    o, st = _rwkv_mixer(rkvg, lw, a, p['rwkv_kk'][j], p['rwkv_ka'][j], p['rwkv_rk'][j], p['rwkv_gn'][j],
                          _state_pairs(p['state_rwkv'][:, j]))
    x = _out_proj(o, rkvg.reshape(4 * N_TOK, D_MODEL), 0, p['rwkv_w_out'][j], x, mod, p['final_norm_w'], False,
                  g_row0=3 * N_TOK)
    return x, st


DIFF_W = 2 * DIFF_HD
ATT_QB = 256
DIFF_GROUP = 4


def _first_half_lanes():
    return lax.broadcasted_iota(jnp.int32, (1, LANES), 1) < LANES // 2


def _diff_lambda(lam_ref, lam_init):
    lp = lam_ref[...]
    return (jnp.exp(jnp.sum(lp[0:1] * lp[1:2], keepdims=True))
            - jnp.exp(jnp.sum(lp[2:3] * lp[3:4], keepdims=True)) + lam_init)


def _diff_heads(items, lam, lam_init):
    first = _first_half_lanes()
    scale = DIFF_HD ** -0.5
    sub = [(q, keys, comp) for q, keys, _ in items for comp in range(2)]
    qm = [jnp.where(first if comp == 0 else ~first, q, 0.0).astype(BF16) for q, _, comp in sub]
    s = [[_dot_nt(x, kb) * scale for kb, _ in keys] for x, (_, keys, _) in zip(qm, sub)]
    m = [functools.reduce(jnp.maximum, [jnp.max(u, axis=-1, keepdims=True) for u in ss]) for ss in s]
    e = [[jnp.exp(u - mm) for u in ss] for ss, mm in zip(s, m)]
    inv = [1.0 / functools.reduce(lambda x, y: x + y, [jnp.sum(u, axis=-1, keepdims=True) for u in ee]) for ee in e]
    outs = []
    for i, (_, keys, gn) in enumerate(items):
        o = None
        for n, (_, vb) in enumerate(keys):
            p = e[2 * i][n] * inv[2 * i] - lam * (e[2 * i + 1][n] * inv[2 * i + 1])
            part = _dot(p.astype(BF16), vb)
            o = part if o is None else o + part
        outs.append(o)
    return [o * lax.rsqrt(jnp.mean(o * o, axis=-1, keepdims=True) + EPS) * gn * (1.0 - lam_init)
            for o, (_, _, gn) in zip(outs, items)]


def _diff_prompt_kernel(lam_ref, q_ref, k_ref, v_ref, gn_ref, o_ref, *, lam_init):
    lam = _diff_lambda(lam_ref, lam_init)
    for h0 in range(0, DIFF_HEADS, DIFF_GROUP):
        items = []
        for h in range(h0, h0 + DIFF_GROUP):
            sl = slice(h * DIFF_W, (h + 1) * DIFF_W)
            items.append((q_ref[:, sl], [(k_ref[:, sl].astype(BF16), v_ref[:, sl].astype(BF16))], gn_ref[:, sl]))
        for h, o in zip(range(h0, h0 + DIFF_GROUP), _diff_heads(items, lam, lam_init)):
            o_ref[:, h * DIFF_W:(h + 1) * DIFF_W] = o


def _diff_latent_kernel(lam_ref, q_ref, k_ref, v_ref, ck_ref, cv_ref, cos_ref, slo_ref, shi_ref, gn_ref,
                        _prev_ref, o_ref, *, lam_init):
    lam = _diff_lambda(lam_ref, lam_init)
    tabs = (cos_ref[...], slo_ref[...], shi_ref[...])
    q = _rope(q_ref[...].astype(F32), *tabs, DIFF_HD // 4)
    k = _rope(k_ref[...].astype(F32), *tabs, DIFF_HD // 4)
    keys = [(k.astype(BF16), v_ref[...].astype(BF16)),
            (ck_ref[0, 0, 0].astype(BF16), cv_ref[0, 0, 0].astype(BF16))]
    gn = gn_ref[...]
    n_blk = DEC_SEQ // ATT_QB
    items = [(q[qi * ATT_QB:(qi + 1) * ATT_QB], keys, gn) for qi in range(n_blk)]
    for qi, o in enumerate(_diff_heads(items, lam, lam_init)):
        o_ref[qi * ATT_QB:(qi + 1) * ATT_QB, :] = o


def _diff_attention(proj, lam_p, gn_w, cache_k, cache_v, j, lam_init):
    gn = gn_w.reshape(1, D_MODEL)
    lam_spec = pl.BlockSpec((4, DIFF_HD), lambda *_: (0, 0))
    o_p = pl.pallas_call(
        functools.partial(_diff_prompt_kernel, lam_init=lam_init),
        grid=(BATCH,),
        in_specs=[
            lam_spec,
            pl.BlockSpec((SEQ, D_MODEL), lambda b: (b, 0)),
            pl.BlockSpec((SEQ, D_MODEL), lambda b: (b, 1)),
            pl.BlockSpec((SEQ, D_MODEL), lambda b: (b, 2)),
            pl.BlockSpec((1, D_MODEL), lambda b: (0, 0)),
        ],
        out_specs=pl.BlockSpec((SEQ, D_MODEL), lambda b: (b, 0)),
        out_shape=jax.ShapeDtypeStruct((N_TOK, D_MODEL), F32),
        compiler_params=_params("arbitrary"),
        name="diff_prompt",
    )(lam_p, proj, proj, proj, gn)
    cos, slo, shi = (jnp.concatenate([u, u], axis=-1) for u in _rope_tables(DIFF_HD))
    rb = N_PROMPT_TOK // DEC_SEQ
    nh = DIFF_HEADS
    tab = pl.BlockSpec((DEC_SEQ, DIFF_W), lambda b, h: (0, 0))
    cache = pl.BlockSpec((1, 1, 1, PAST_LEN, DIFF_W), lambda b, h: (b, j, h, 0, 0))
    o = pl.pallas_call(
        functools.partial(_diff_latent_kernel, lam_init=lam_init),
        grid=(DEC_BATCH, nh),
        in_specs=[
            lam_spec,
            pl.BlockSpec((DEC_SEQ, DIFF_W), lambda b, h: (rb + b, h)),
            pl.BlockSpec((DEC_SEQ, DIFF_W), lambda b, h: (rb + b, nh + h)),
            pl.BlockSpec((DEC_SEQ, DIFF_W), lambda b, h: (rb + b, 2 * nh + h)),
            cache, cache, tab, tab, tab,
            pl.BlockSpec((1, DIFF_W), lambda b, h: (0, h)),
            pl.BlockSpec(memory_space=pl.ANY),
        ],
        out_specs=pl.BlockSpec((DEC_SEQ, DIFF_W), lambda b, h: (rb + b, h)),
        out_shape=jax.ShapeDtypeStruct((N_TOK, D_MODEL), F32),
        input_output_aliases={10: 0},
        compiler_params=_params("arbitrary", "arbitrary"),
        name="diff_latent",
    )(lam_p, proj, proj, proj, cache_k, cache_v, cos, slo, shi, gn, o_p)
    return o


def _layer_diff(x, p, mod, j, i):
    lam_init = 0.8 - 0.6 * math.exp(-0.3 * i)
    proj, new_k, new_v = _in_proj_kv(x, p['norm_w'][i], mod, p['diff_w_in'][j], DIFF_HEADS)
    o = _diff_attention(proj, p['diff_lambda'][j], p['diff_gn'][j], p['cache_diff_k'], p['cache_diff_v'], j,
                        lam_init)
    x = _out_proj(o, proj, 3, p['diff_w_out'][j], x, mod, p['final_norm_w'], False)
    return x, new_k, new_v


NA_ROWS = DEC_SEQ // GRID_W
NA_WR = min(NA_WIN_R, NA_ROWS)
NA_LOC = NA_WR * GRID_W
NA_ROW_GROUP = 4
NA_PAIR_GROUP = 2


def _na_prompt_kernel(q_ref, k_ref, v_ref, o_ref):
    first = _first_half_lanes()
    scale = NA_HD ** -0.5
    for p0 in range(0, NA_HEADS // 2, NA_PAIR_GROUP):
        pairs = range(p0, p0 + NA_PAIR_GROUP)
        kb = [k_ref[:, pr * LANES:(pr + 1) * LANES].astype(BF16) for pr in pairs]
        vb = [v_ref[:, pr * LANES:(pr + 1) * LANES].astype(BF16) for pr in pairs]
        items = [(i, half) for i in range(NA_PAIR_GROUP) for half in range(2)]
        qm = [jnp.where(first if half == 0 else ~first, q_ref[:, (p0 + i) * LANES:(p0 + i + 1) * LANES], 0.0)
              .astype(BF16) for i, half in items]
        s = [_dot_nt(x, kb[i]) * scale for x, (i, _) in zip(qm, items)]
        e = [jnp.exp(x - jnp.max(x, axis=-1, keepdims=True)) for x in s]
        inv = [1.0 / jnp.sum(x, axis=-1, keepdims=True) for x in e]
        outs = [_dot(x.astype(BF16), vb[i]) * z for x, z, (i, _) in zip(e, inv, items)]
        for i in range(NA_PAIR_GROUP):
            o_ref[:, (p0 + i) * LANES:(p0 + i + 1) * LANES] = jnp.where(first, outs[2 * i], outs[2 * i + 1])


def _na_latent_kernel(q_ref, k_ref, v_ref, kc_ref, vc_ref, tab_ref, _prev_ref, o_ref):
    first = _first_half_lanes()
    scale = NA_HD ** -0.5
    kb = k_ref[...].astype(BF16)
    vb = v_ref[...].astype(BF16)
    kcb = kc_ref[0, 0].astype(BF16)
    vcb = vc_ref[0, 0].astype(BF16)
    qcol = lax.broadcasted_iota(jnp.int32, (GRID_W, NA_LOC), 0)
    kcol = lax.broadcasted_iota(jnp.int32, (GRID_W, NA_LOC), 1) & (GRID_W - 1)
    cstart = jnp.clip(qcol - NA_WIN_C // 2, 0, GRID_W - NA_WIN_C)
    col_ok = (kcol >= cstart) & (kcol < cstart + NA_WIN_C)
    def bias_of(r, rs, half):
        parts = []
        for w in range(0, NA_WR, 2):
            src = jnp.broadcast_to(tab_ref[half, rs + w - r + NA_WIN_R - 1], (GRID_W, LANES))
            parts.append(pltpu.roll(src, LANES - (NA_WIN_C - 1), axis=1, stride=1, stride_axis=0))
        return jnp.concatenate(parts, axis=1)

    for r0 in range(0, NA_ROWS, NA_ROW_GROUP):
        items = [(r, min(max(r - NA_WR // 2, 0), NA_ROWS - NA_WR), half)
                 for r in range(r0, r0 + NA_ROW_GROUP) for half in range(2)]
        qm = [jnp.where(first if half == 0 else ~first, q_ref[r * GRID_W:(r + 1) * GRID_W, :], 0.0).astype(BF16)
              for r, _, half in items]
        s_loc = [_dot_nt(x, kb[rs * GRID_W:(rs + NA_WR) * GRID_W]) for x, (_, rs, _) in zip(qm, items)]
        s_ctx = [_dot_nt(x, kcb) * scale for x in qm]
        s_loc = [jnp.where(col_ok, x * scale + bias_of(*it), -jnp.inf) for x, it in zip(s_loc, items)]
        m = [jnp.maximum(jnp.max(x, axis=-1, keepdims=True), jnp.max(y, axis=-1, keepdims=True))
             for x, y in zip(s_loc, s_ctx)]
        e_loc = [jnp.exp(x - mm) for x, mm in zip(s_loc, m)]
        e_ctx = [jnp.exp(x - mm) for x, mm in zip(s_ctx, m)]
        inv = [1.0 / (jnp.sum(x, axis=-1, keepdims=True) + jnp.sum(y, axis=-1, keepdims=True))
               for x, y in zip(e_loc, e_ctx)]
        pv = [_dot(x.astype(BF16), vb[rs * GRID_W:(rs + NA_WR) * GRID_W]) for x, (_, rs, _) in zip(e_loc, items)]
        pc = [_dot(x.astype(BF16), vcb) for x in e_ctx]
        outs = [(x + y) * z for x, y, z in zip(pv, pc, inv)]
        for n in range(0, len(items), 2):
            r = items[n][0]
            o_ref[r * GRID_W:(r + 1) * GRID_W, :] = jnp.where(first, outs[n], outs[n + 1])


def _na_bias_pairs(table):
    t = table.astype(F32)
    nc = 2 * NA_WIN_C - 1
    z = jnp.zeros(t[:, :-1].shape[:2] + (GRID_W - nc,), F32)
    return jnp.concatenate([t[:, :-1], z, t[:, 1:], z], axis=-1)[:, :, None, :]


def _pair_heads(cache):
    c = cache.reshape(DEC_BATCH, NA_HEADS // 2, 2, PAST_LEN, NA_HD)
    return c.transpose(0, 1, 3, 2, 4).reshape(DEC_BATCH, NA_HEADS // 2, PAST_LEN, LANES)


def _na_attention(proj, bias_table, cache_k, cache_v):
    o_p = pl.pallas_call(
        _na_prompt_kernel,
        grid=(BATCH,),
        in_specs=[
            pl.BlockSpec((SEQ, D_MODEL), lambda b: (b, 0)),
            pl.BlockSpec((SEQ, D_MODEL), lambda b: (b, 1)),
            pl.BlockSpec((SEQ, D_MODEL), lambda b: (b, 2)),
        ],
        out_specs=pl.BlockSpec((SEQ, D_MODEL), lambda b: (b, 0)),
        out_shape=jax.ShapeDtypeStruct((N_TOK, D_MODEL), F32),
        compiler_params=_params("arbitrary"),
        name="na_prompt",
    )(proj, proj, proj)
    rb = N_PROMPT_TOK // DEC_SEQ
    npair = NA_HEADS // 2
    cache = pl.BlockSpec((1, 1, PAST_LEN, LANES), lambda pr, b: (b, pr, 0, 0))
    o = pl.pallas_call(
        _na_latent_kernel,
        grid=(npair, DEC_BATCH),
        in_specs=[
            pl.BlockSpec((DEC_SEQ, LANES), lambda pr, b: (rb + b, pr)),
            pl.BlockSpec((DEC_SEQ, LANES), lambda pr, b: (rb + b, npair + pr)),
            pl.BlockSpec((DEC_SEQ, LANES), lambda pr, b: (rb + b, 2 * npair + pr)),
            cache, cache,
            pl.BlockSpec((2, 2 * NA_WIN_R - 2, 1, LANES), lambda pr, b: (pr, 0, 0, 0)),
            pl.BlockSpec(memory_space=pl.ANY),
        ],
        out_specs=pl.BlockSpec((DEC_SEQ, LANES), lambda pr, b: (rb + b, pr)),
        out_shape=jax.ShapeDtypeStruct((N_TOK, D_MODEL), F32),
        input_output_aliases={6: 0},
        compiler_params=_params("arbitrary", "arbitrary"),
        name="na_latent",
    )(proj, proj, proj, _pair_heads(cache_k), _pair_heads(cache_v), _na_bias_pairs(bias_table), o_p)
    return o


def _layer_na(x, p, mod, j, final):
    i = N_MIXERS * j + 3
    proj, new_k, new_v = _in_proj_kv(x, p['norm_w'][i], mod, p['na_w_in'][j], NA_HEADS)
    o = _na_attention(proj, p['na_bias'][j], p['cache_na_k'][:, j], p['cache_na_v'][:, j])
    args = (o, proj, 3, p['na_w_out'][j], x, mod, p['final_norm_w'])
    if final:
        x = (_out_proj(*args, True, rows=(0, N_PROMPT_TOK)), _out_proj(*args, True, rows=(N_PROMPT_TOK, N_TOK)))
    else:
        x = _out_proj(*args, False)
    return x, new_k, new_v


def kernel(x_prompt, x_sample, state_ret, state_rwkv, cache_diff_k, cache_diff_v, cache_na_k, cache_na_v,
           c, c_ctx, norm_w, w_mod, b_mod, final_norm_w,
           ret_w_in, ret_decay, ret_gn, ret_w_out,
           rwkv_mu, rwkv_w_in, rwkv_w0, rwkv_wA, rwkv_wB, rwkv_a0, rwkv_aA, rwkv_aB,
           rwkv_kk, rwkv_ka, rwkv_rk, rwkv_gn, rwkv_w_out,
           diff_w_in, diff_lambda, diff_gn, diff_w_out,
           na_w_in, na_bias, na_w_out):
    p = dict(locals())
    cond = jnp.zeros((N_COND, D_MODEL), F32).at[0].set(c_ctx).at[1:1 + DEC_BATCH].set(c)
    mods = _modulation(cond, w_mod, b_mod)
    x = (x_prompt.reshape(N_PROMPT_TOK, D_MODEL), x_sample.reshape(N_SAMPLE_TOK, D_MODEL))
    new = {n: [] for n in ('ret', 'rwkv', 'dk', 'dv', 'nk', 'nv')}
    for i in range(DEPTH):
        kind, j = i % N_MIXERS, i // N_MIXERS
        if kind == 0:
            x, st = _layer_ret(x, p, mods[i], j)
            new['ret'].append(st)
        elif kind == 1:
            x, st = _layer_rwkv(x, p, mods[i], j)
            new['rwkv'].append(st)
        elif kind == 2:
            x, ck, cv = _layer_diff(x, p, mods[i], j, i)
            new['dk'].append(ck)
            new['dv'].append(cv)
        else:
            x, ck, cv = _layer_na(x, p, mods[i], j, final=(i == DEPTH - 1))
            new['nk'].append(ck)
            new['nv'].append(cv)
    if DEPTH % N_MIXERS:
        raise NotImplementedError("the final norm is fused into the last neighbourhood-attention layer")
    cat = lambda xs: xs[0] if len(xs) == 1 else jnp.concatenate(xs, axis=1)
    return (x[0].reshape(BATCH, SEQ, D_MODEL), x[1].reshape(DEC_BATCH, DEC_SEQ, D_MODEL),
            cat(new['ret']), cat(new['rwkv']), cat(new['dk']), cat(new['dv']), cat(new['nk']), cat(new['nv']))
```

```python
import functools
import math

import jax
import jax.numpy as jnp
from jax import lax
from jax.experimental import pallas as pl
from jax.experimental.pallas import tpu as pltpu

F32 = jnp.float32
BF16 = jnp.bfloat16

D_MODEL = 1024
BATCH = 32
SEQ = 256
DEPTH = 4
N_MIXERS = 4
DEC_BATCH = 2
DEC_SEQ = 1024
PAST_LEN = 256
GRID_W = 64

RET_HEADS = 4
RET_DK = 256
RET_DV = 512
RET_QK = 1024
RET_V = 2048

RWKV_HD = 64
RWKV_HEADS = 16
RWKV_RANK = 64

DIFF_HEADS = 8
DIFF_HD = 64

NA_HEADS = 16
NA_HD = 64
NA_WIN_R = 8
NA_WIN_C = 16

ROPE_BASE = 10000.0
EPS = 1e-6
GN_EPS = 1e-5

N_PROMPT_TOK = BATCH * SEQ
N_SAMPLE_TOK = DEC_BATCH * DEC_SEQ
N_TOK = N_PROMPT_TOK + N_SAMPLE_TOK
N_COND = 8

LANES = 128
VMEM_LIMIT = 56 * 2 ** 20


def _params(*sem):
    return pltpu.CompilerParams(dimension_semantics=sem, vmem_limit_bytes=VMEM_LIMIT)


def _cond_of_tile(i, tm):
    npt = N_PROMPT_TOK // tm
    return jnp.where(i < npt, 0, 1 + (i - npt) // (DEC_SEQ // tm))


def _sigmoid(x):
    return 1.0 / (1.0 + jnp.exp(-x))


def _silu(x):
    return x * _sigmoid(x)


def _dot(a, b):
    return jnp.dot(a, b, preferred_element_type=F32)


def _dot_nt(a, b):
    return lax.dot_general(a, b, (((1,), (1,)), ((), ())), preferred_element_type=F32)


def _dot_tn(a, b):
    return lax.dot_general(a, b, (((0,), (0,)), ((), ())), preferred_element_type=F32)


def _softmax_rows(s):
    m = jnp.max(s, axis=-1, keepdims=True)
    e = jnp.exp(s - m)
    return e / jnp.sum(e, axis=-1, keepdims=True)


def _mod_kernel(c_ref, w_ref, b_ref, o_ref):
    s = _silu(c_ref[...])
    o_ref[0] = jnp.dot(s, w_ref[0], precision=lax.Precision.HIGHEST, preferred_element_type=F32) + b_ref[0]


def _modulation(cond, w_mod, b_mod):
    tn = D_MODEL
    out = pl.pallas_call(
        _mod_kernel,
        grid=(DEPTH, 3 * D_MODEL // tn),
        in_specs=[
            pl.BlockSpec((N_COND, D_MODEL), lambda l, j: (0, 0)),
            pl.BlockSpec((1, D_MODEL, tn), lambda l, j: (l, 0, j)),
            pl.BlockSpec((1, 1, tn), lambda l, j: (l, 0, j)),
        ],
        out_specs=pl.BlockSpec((1, N_COND, tn), lambda l, j: (l, 0, j)),
        out_shape=jax.ShapeDtypeStruct((DEPTH, N_COND, 3 * D_MODEL), F32),
        compiler_params=_params("arbitrary", "arbitrary"),
        name="modulation",
    )(cond, w_mod, b_mod.reshape(DEPTH, 1, 3 * D_MODEL))
    return out.reshape(DEPTH, N_COND, 3, 1, D_MODEL)


def _norm_mod(x, nw, mod_ref):
    ms = jnp.mean(x * x, axis=-1, keepdims=True)
    y = x * lax.rsqrt(ms + EPS) * nw
    return y * (1.0 + mod_ref[0, 1]) + mod_ref[0, 0]


IN_TM = 1024
IN_TN = 2048


def _x_specs(x, tm, tile_of):
    if not isinstance(x, tuple):
        return [pl.BlockSpec((tm, D_MODEL), lambda *g: (tile_of(*g), 0))], (x,)
    npt = N_PROMPT_TOK // tm
    return [pl.BlockSpec((tm, D_MODEL), lambda *g: (jnp.minimum(tile_of(*g), npt - 1), 0)),
            pl.BlockSpec((tm, D_MODEL), lambda *g: (jnp.maximum(tile_of(*g) - npt, 0), 0))], x


def _read_x(x_refs, tile, tm):
    if len(x_refs) == 1:
        return x_refs[0][...]
    return jnp.where(tile < N_PROMPT_TOK // tm, x_refs[0][...], x_refs[1][...])


def _in_proj_kernel(*refs, n_x):
    x_refs = refs[:n_x]
    nw_ref, mod_ref, w_ref, o_ref, h_ref = refs[n_x:]

    @pl.when(pl.program_id(1) == 0)
    def _():
        x = _read_x(x_refs, pl.program_id(0), IN_TM)
        h_ref[...] = _norm_mod(x, nw_ref[...], mod_ref).astype(BF16)

    o_ref[...] = _dot(h_ref[...], w_ref[...]).astype(o_ref.dtype)


def _in_proj(x, norm_w, mod, w, tn, out_dtype=F32):
    n = w.shape[1]
    w = w.astype(BF16)
    x_specs, xs = _x_specs(x, IN_TM, lambda i, j: i)
    return pl.pallas_call(
        functools.partial(_in_proj_kernel, n_x=len(xs)),
        grid=(N_TOK // IN_TM, n // tn),
        in_specs=x_specs + [
            pl.BlockSpec((1, D_MODEL), lambda i, j: (0, 0)),
            pl.BlockSpec((1, 3, 1, D_MODEL), lambda i, j: (_cond_of_tile(i, IN_TM), 0, 0, 0)),
            pl.BlockSpec((D_MODEL, tn), lambda i, j: (0, j)),
        ],
        out_specs=pl.BlockSpec((IN_TM, tn), lambda i, j: (i, j)),
        out_shape=jax.ShapeDtypeStruct((N_TOK, n), out_dtype),
        scratch_shapes=[pltpu.VMEM((IN_TM, D_MODEL), BF16)],
        compiler_params=_params("arbitrary", "arbitrary"),
        name="in_proj",
    )(*xs, norm_w.reshape(1, D_MODEL), mod, w)


def _in_proj_kv_kernel(x_ref, nw_ref, mod_ref, w_ref, o_ref, ck_ref, cv_ref, h_ref, *, heads, tn):
    i = pl.program_id(0)
    j = pl.program_id(1)

    @pl.when(j == 0)
    def _():
        h_ref[...] = _norm_mod(x_ref[...], nw_ref[...], mod_ref).astype(BF16)

    acc = _dot(h_ref[...], w_ref[...])
    o_ref[...] = acc.astype(o_ref.dtype)
    hd = D_MODEL // heads
    for col, c_ref in ((D_MODEL, ck_ref), (2 * D_MODEL, cv_ref)):
        @pl.when((j == col // tn) & (i < N_PROMPT_TOK // IN_TM))
        def _(c_ref=c_ref, c0=col % tn):
            for s in range(IN_TM // SEQ):
                for h in range(heads):
                    c_ref[s, 0, h] = acc[s * SEQ:(s + 1) * SEQ, c0 + h * hd:c0 + (h + 1) * hd]


def _in_proj_kv(x, norm_w, mod, w, heads):
    n = w.shape[1]
    tn = D_MODEL
    spb = IN_TM // SEQ
    last = N_PROMPT_TOK // IN_TM - 1
    cache = pl.BlockSpec((spb, 1, heads, SEQ, D_MODEL // heads), lambda i, j: (jnp.minimum(i, last), 0, 0, 0, 0))
    cache_shape = jax.ShapeDtypeStruct((BATCH, 1, heads, SEQ, D_MODEL // heads), F32)
    return pl.pallas_call(
        functools.partial(_in_proj_kv_kernel, heads=heads, tn=tn),
        grid=(N_TOK // IN_TM, n // tn),
        in_specs=[
            pl.BlockSpec((IN_TM, D_MODEL), lambda i, j: (i, 0)),
            pl.BlockSpec((1, D_MODEL), lambda i, j: (0, 0)),
            pl.BlockSpec((1, 3, 1, D_MODEL), lambda i, j: (_cond_of_tile(i, IN_TM), 0, 0, 0)),
            pl.BlockSpec((D_MODEL, tn), lambda i, j: (0, j)),
        ],
        out_specs=[pl.BlockSpec((IN_TM, tn), lambda i, j: (i, j)), cache, cache],
        out_shape=[jax.ShapeDtypeStruct((N_TOK, n), BF16), cache_shape, cache_shape],
        scratch_shapes=[pltpu.VMEM((IN_TM, D_MODEL), BF16)],
        compiler_params=_params("arbitrary", "arbitrary"),
        name="in_proj_kv",
    )(x, norm_w.reshape(1, D_MODEL), mod, w.astype(BF16))


OUT_TM = 256


def _out_proj_kernel(*refs, n_x, t0, final):
    x_refs = refs[:n_x]
    o_ref, g_ref, w_ref, mod_ref, fw_ref, y_ref, wb_ref = refs[n_x:]

    @pl.when(pl.program_id(0) == 0)
    def _():
        wb_ref[...] = w_ref[...].astype(BF16)

    a = (o_ref[...] * _silu(g_ref[...].astype(F32))).astype(BF16)
    xn = _read_x(x_refs, t0 + pl.program_id(0), OUT_TM) + mod_ref[0, 2] * _dot(a, wb_ref[...])
    if final:
        ms = jnp.mean(xn * xn, axis=-1, keepdims=True)
        xn = xn * lax.rsqrt(ms + EPS) * fw_ref[...]
    y_ref[...] = xn


def _out_proj(o, g_arr, g_blk, w, x, mod, final_w, final, rows=(0, N_TOK), g_row0=0):
    k = w.shape[0]
    t0 = rows[0] // OUT_TM
    g0 = g_row0 // OUT_TM
    x_specs, xs = _x_specs(x, OUT_TM, lambda i: t0 + i)
    return pl.pallas_call(
        functools.partial(_out_proj_kernel, n_x=len(xs), t0=t0, final=final),
        grid=((rows[1] - rows[0]) // OUT_TM,),
        in_specs=x_specs + [
            pl.BlockSpec((OUT_TM, k), lambda i: (t0 + i, 0)),
            pl.BlockSpec((OUT_TM, k), lambda i: (g0 + t0 + i, g_blk)),
            pl.BlockSpec((k, D_MODEL), lambda i: (0, 0)),
            pl.BlockSpec((1, 3, 1, D_MODEL), lambda i: (_cond_of_tile(t0 + i, OUT_TM), 0, 0, 0)),
            pl.BlockSpec((1, D_MODEL), lambda i: (0, 0)),
        ],
        out_specs=pl.BlockSpec((OUT_TM, D_MODEL), lambda i: (i, 0)),
        out_shape=jax.ShapeDtypeStruct((rows[1] - rows[0], D_MODEL), F32),
        scratch_shapes=[pltpu.VMEM((k, D_MODEL), BF16)],
        compiler_params=_params("arbitrary"),
        name="out_proj",
    )(*xs, o, g_arr, w, mod, final_w.reshape(1, D_MODEL))


def _rope_tables(d):
    q = d // 4
    t = jnp.arange(DEC_SEQ)
    row = (t // GRID_W).astype(F32)
    col = (t % GRID_W).astype(F32)
    inv = ROPE_BASE ** (-jnp.arange(0, 2 * q, 2, dtype=F32) / (2 * q))
    ar = row[:, None] * inv[None, :]
    ac = col[:, None] * inv[None, :]
    z = jnp.zeros_like(ar)
    cos = jnp.concatenate([jnp.cos(ar), jnp.cos(ar), jnp.cos(ac), jnp.cos(ac)], axis=-1)
    sin_lo = jnp.concatenate([-jnp.sin(ar), z, -jnp.sin(ac), z], axis=-1)
    sin_hi = jnp.concatenate([z, jnp.sin(ar), z, jnp.sin(ac)], axis=-1)
    return cos, sin_lo, sin_hi


def _rope(x, cos, sin_lo, sin_hi, q):
    w = x.shape[-1]
    x_next = pltpu.roll(x, w - q, axis=1)
    x_prev = pltpu.roll(x, q, axis=1)
    return x * cos + x_next * sin_lo + x_prev * sin_hi


RET_QB = 256


def _ret_kernel(lg_ref, q_ref, k_ref, v_ref, gn_ref, *rest, seq, latent):
    if latent:
        cos_ref, slo_ref, shi_ref, s0_ref, _prev_ref, o_ref = rest
    else:
        o_ref, st_ref, dec_ref = rest
    h = pl.program_id(1 if latent else 0)
    lgf = lg_ref[0, h]
    lgb = lg_ref[1, h]
    q = q_ref[...].astype(F32)
    k = k_ref[...].astype(F32)
    if latent:
        q = _rope(q, cos_ref[...], slo_ref[...], shi_ref[...], RET_DK // 4)
        k = _rope(k, cos_ref[...], slo_ref[...], shi_ref[...], RET_DK // 4)
    k = k * (RET_DK ** -0.5)
    kb = k.astype(BF16)
    vb = v_ref[...].astype(BF16)
    gn = gn_ref[...]

    def decay(qi):
        ii = lax.broadcasted_iota(jnp.int32, (RET_QB, seq), 0) + qi * RET_QB
        jj = lax.broadcasted_iota(jnp.int32, (RET_QB, seq), 1)
        gap = (ii - jj).astype(F32)
        return (jnp.where(gap >= 0, jnp.exp(lgf * jnp.maximum(gap, 0.0)), 0.0)
                + jnp.where(gap <= 0, jnp.exp(lgb * jnp.maximum(-gap, 0.0)), 0.0))

    if not latent:
        @pl.when(pl.program_id(1) == 0)
        def _():
            dec_ref[...] = decay(0)

    for qi in range(seq // RET_QB):
        qblk = q[qi * RET_QB:(qi + 1) * RET_QB]
        s = _dot_nt(qblk.astype(BF16), kb)
        dec = decay(qi) if latent else dec_ref[...]
        o = _dot((s * dec).astype(BF16), vb)
        if latent:
            pos = (lax.broadcasted_iota(jnp.int32, (RET_QB, 1), 0) + qi * RET_QB).astype(F32)
            qf = qblk * jnp.exp(lgf * (pos + 1.0))
            qr = qblk * jnp.exp(lgb * (seq - pos))
            o = o + _dot(qf.astype(BF16), s0_ref[0, 0, 0, 0].astype(BF16))
            o = o + _dot(qr.astype(BF16), s0_ref[0, 0, 1, 0].astype(BF16))
        oc = o - jnp.mean(o, axis=-1, keepdims=True)
        o = oc * lax.rsqrt(jnp.mean(oc * oc, axis=-1, keepdims=True) + GN_EPS) * gn
        o_ref[qi * RET_QB:(qi + 1) * RET_QB, :] = o
    if not latent:
        pos = lax.broadcasted_iota(jnp.int32, (seq, 1), 0).astype(F32)
        kf = k * jnp.exp(lgf * (seq - 1.0 - pos))
        kr = k * jnp.exp(lgb * pos)
        st_ref[0, 0, 0, 0] = _dot_tn(kf.astype(BF16), vb)
        st_ref[0, 0, 1, 0] = _dot_tn(kr.astype(BF16), vb)


def _retention(p, log_g, gn_w, state_ret, j):
    smem = pl.BlockSpec(memory_space=pltpu.SMEM)
    gn = gn_w.reshape(1, RET_V)
    kq = RET_QK // RET_DK
    o_p, st = pl.pallas_call(
        functools.partial(_ret_kernel, seq=SEQ, latent=False),
        grid=(RET_HEADS, BATCH),
        in_specs=[
            smem,
            pl.BlockSpec((SEQ, RET_DK), lambda h, b: (b, h)),
            pl.BlockSpec((SEQ, RET_DK), lambda h, b: (b, kq + h)),
            pl.BlockSpec((SEQ, RET_DV), lambda h, b: (b, kq + h)),
            pl.BlockSpec((1, RET_DV), lambda h, b: (0, h)),
        ],
        out_specs=[
            pl.BlockSpec((SEQ, RET_DV), lambda h, b: (b, h)),
            pl.BlockSpec((1, 1, 2, 1, RET_DK, RET_DV), lambda h, b: (b, 0, 0, h, 0, 0)),
        ],
        out_shape=[
            jax.ShapeDtypeStruct((N_TOK, RET_V), F32),
            jax.ShapeDtypeStruct((BATCH, 1, 2, RET_HEADS, RET_DK, RET_DV), F32),
        ],
        scratch_shapes=[pltpu.VMEM((RET_QB, SEQ), F32)],
        compiler_params=_params("arbitrary", "arbitrary"),
        name="retention_prompt",
    )(log_g, p, p, p, gn)
    cos, slo, shi = _rope_tables(RET_DK)
    rb = N_PROMPT_TOK // DEC_SEQ
    full = pl.BlockSpec((DEC_SEQ, RET_DK), lambda b, h: (0, 0))
    o = pl.pallas_call(
        functools.partial(_ret_kernel, seq=DEC_SEQ, latent=True),
        grid=(DEC_BATCH, RET_HEADS),
        in_specs=[
            smem,
            pl.BlockSpec((DEC_SEQ, RET_DK), lambda b, h: (rb + b, h)),
            pl.BlockSpec((DEC_SEQ, RET_DK), lambda b, h: (rb + b, kq + h)),
            pl.BlockSpec((DEC_SEQ, RET_DV), lambda b, h: (rb + b, kq + h)),
            pl.BlockSpec((1, RET_DV), lambda b, h: (0, h)),
            full, full, full,
            pl.BlockSpec((1, 1, 2, 1, RET_DK, RET_DV), lambda b, h: (b, j, 0, h, 0, 0)),
            pl.BlockSpec(memory_space=pl.ANY),
        ],
        out_specs=pl.BlockSpec((DEC_SEQ, RET_DV), lambda b, h: (rb + b, h)),
        out_shape=jax.ShapeDtypeStruct((N_TOK, RET_V), F32),
        input_output_aliases={9: 0},
        compiler_params=_params("arbitrary", "arbitrary"),
        name="retention_latent",
    )(log_g, p, p, p, gn, cos, slo, shi, state_ret, o_p)
    return o, st


def _layer_ret(x, p, mod, j):
    i = N_MIXERS * j + 0
    proj = _in_proj(x, p['norm_w'][i], mod, p['ret_w_in'][j], IN_TN, out_dtype=BF16)
    log_g = jax.nn.log_sigmoid(p['ret_decay'][j].astype(F32))
    o, st = _retention(proj, log_g, p['ret_gn'][j], p['state_ret'], j)
    x = _out_proj(o, proj, (2 * RET_QK + RET_V) // RET_V, p['ret_w_out'][j], x, mod, p['final_norm_w'], False)
    return x, st


RW_TM = 512
RW_HALO = 8
RW_C = 64
RW_LOCK = 4
RW_PAIRS = 2


def _rwkv_prep_kernel(x_ref, xp_ref, xn_ref, nw_ref, mod_ref, mu_ref, wa_ref, aa_ref, wb_ref, ab_ref,
                      w0_ref, a0_ref, win_ref, rkvg_ref, lw_ref, a_ref, xm_ref):
    n = pl.program_id(1)

    @pl.when(n == 0)
    def _():
        _rwkv_mix(x_ref, xp_ref, xn_ref, nw_ref, mod_ref, mu_ref, wa_ref, aa_ref, wb_ref, ab_ref,
                  w0_ref, a0_ref, xm_ref, lw_ref, a_ref)

    rkvg_ref[0] = _dot(xm_ref[n], win_ref[...])


def _rwkv_mix(x_ref, xp_ref, xn_ref, nw_ref, mod_ref, mu_ref, wa_ref, aa_ref, wb_ref, ab_ref,
              w0_ref, a0_ref, xm_ref, lw_ref, a_ref):
    i = pl.program_id(0)
    nw = nw_ref[...]
    h = _norm_mod(x_ref[...], nw, mod_ref)
    h_before = _norm_mod(xp_ref[RW_HALO - 1:RW_HALO, :], nw, mod_ref)
    h_after = _norm_mod(xn_ref[0:1, :], nw, mod_ref)
    seq = jnp.where(i < N_PROMPT_TOK // RW_TM, SEQ, DEC_SEQ)
    row = lax.broadcasted_iota(jnp.int32, (RW_TM, 1), 0)
    t = (row + i * RW_TM) & (seq - 1)
    prev = jnp.where(row == 0, h_before, pltpu.roll(h, 1, axis=0))
    nxt = jnp.where(row == RW_TM - 1, h_after, pltpu.roll(h, RW_TM - 1, axis=0))
    prev = jnp.where(t == 0, 0.0, prev)
    nxt = jnp.where(t == seq - 1, 0.0, nxt)
    xx = 0.5 * (prev + nxt) - h
    for n, m in enumerate((0, 2, 3, 5)):
        xm_ref[n] = (h + xx * mu_ref[m:m + 1, :]).astype(BF16)
    xw = (h + xx * mu_ref[1:2, :]).astype(BF16)
    xa = (h + xx * mu_ref[4:5, :]).astype(BF16)
    lw = jnp.tanh(_dot(xw, wa_ref[...])).astype(BF16)
    la = _dot(xa, aa_ref[...]).astype(BF16)
    for dr in range(2):
        wl = w0_ref[dr:dr + 1, :] + _dot(lw, wb_ref[dr])
        lw_ref[dr] = -math.exp(-0.5) * _sigmoid(wl)
        a_ref[dr] = _sigmoid(a0_ref[dr:dr + 1, :] + _dot(la, ab_ref[dr]))


def _rwkv_prep(x, norm_w, mod, mu, wa2, aa2, wb_pad, ab_pad, w0, a0, w_in):
    nt = N_TOK // RW_TM
    hb = RW_TM // RW_HALO
    last = N_TOK // RW_HALO - 1
    full2 = lambda shape: pl.BlockSpec(shape, lambda i, n: (0, 0))
    full3 = lambda shape: pl.BlockSpec(shape, lambda i, n: (0, 0, 0))
    return pl.pallas_call(
        _rwkv_prep_kernel,
        grid=(nt, 4),
        in_specs=[
            pl.BlockSpec((RW_TM, D_MODEL), lambda i, n: (i, 0)),
            pl.BlockSpec((RW_HALO, D_MODEL), lambda i, n: (jnp.maximum(i * hb - 1, 0), 0)),
            pl.BlockSpec((RW_HALO, D_MODEL), lambda i, n: (jnp.minimum((i + 1) * hb, last), 0)),
            full2((1, D_MODEL)),
            pl.BlockSpec((1, 3, 1, D_MODEL), lambda i, n: (_cond_of_tile(i, RW_TM), 0, 0, 0)),
            full2((6, D_MODEL)),
            full2((D_MODEL, 2 * RWKV_RANK)),
            full2((D_MODEL, 2 * RWKV_RANK)),
            full3((2, 2 * RWKV_RANK, D_MODEL)),
            full3((2, 2 * RWKV_RANK, D_MODEL)),
            full2((2, D_MODEL)),
            full2((2, D_MODEL)),
            pl.BlockSpec((D_MODEL, D_MODEL), lambda i, n: (0, n)),
        ],
        out_specs=[
            pl.BlockSpec((1, RW_TM, D_MODEL), lambda i, n: (n, i, 0)),
            pl.BlockSpec((2, RW_TM, D_MODEL), lambda i, n: (0, i, 0)),
            pl.BlockSpec((2, RW_TM, D_MODEL), lambda i, n: (0, i, 0)),
        ],
        out_shape=[
            jax.ShapeDtypeStruct((4, N_TOK, D_MODEL), F32),
            jax.ShapeDtypeStruct((2, N_TOK, D_MODEL), F32),
            jax.ShapeDtypeStruct((2, N_TOK, D_MODEL), F32),
        ],
        scratch_shapes=[pltpu.VMEM((4, RW_TM, D_MODEL), BF16)],
        compiler_params=_params("arbitrary", "arbitrary"),
        name="rwkv_prep",
    )(x, x, x, norm_w.reshape(1, D_MODEL), mod, mu, wa2, aa2, wb_pad, ab_pad, w0, a0, w_in.astype(BF16))


def _head_sum(x, first):
    s0 = jnp.sum(jnp.where(first, x, 0.0), axis=-1, keepdims=True)
    s1 = jnp.sum(jnp.where(first, 0.0, x), axis=-1, keepdims=True)
    return jnp.where(first, s0, s1)


def _stack_heads(x, first):
    return jnp.concatenate([jnp.where(first, x, 0.0), jnp.where(first, 0.0, x)], axis=0)


def _cumsum_rows(tri, x):
    hi = x.astype(BF16)
    r1 = x - hi.astype(F32)
    mid = r1.astype(BF16)
    lo = (r1 - mid.astype(F32)).astype(BF16)
    return _dot(tri, hi) + _dot(tri, mid) + _dot(tri, lo)


def _rwkv_chunk_kernel(*refs, seq, zero_init):
    if zero_init:
        (rkv_ref, lw_ref, a_ref, kkp_ref, kap_ref, rkp_ref, gn_ref, o_ref, st_ref,
         kk_scr, y_scr, tar_scr, lrb_scr, b2_scr, w2_scr, yl_scr, kv_scr, pc_scr) = refs
    else:
        (rkv_ref, lw_ref, a_ref, kkp_ref, kap_ref, rkp_ref, gn_ref, s0_ref, _prev_ref, o_ref,
         kk_scr, y_scr, tar_scr, lrb_scr, b2_scr, w2_scr, yl_scr, kv_scr, pc_scr) = refs
    c_len = RW_C
    n_ch = seq // c_len
    rows2 = 2 * c_len
    first = _first_half_lanes()
    kap = kap_ref[...]

    kk = rkv_ref[1] * kkp_ref[...]
    kk_scr[...] = kk * lax.rsqrt(jnp.maximum(_head_sum(kk * kk, first), 1e-12))

    rr = lax.broadcasted_iota(jnp.int32, (rows2, rows2), 0)
    cc = lax.broadcasted_iota(jnp.int32, (rows2, rows2), 1)
    eye = (rr == cc).astype(F32)
    tr = lax.broadcasted_iota(jnp.int32, (c_len, c_len), 0)
    tc = lax.broadcasted_iota(jnp.int32, (c_len, c_len), 1)

    def same(shift):
        return (rr >> shift) == (cc >> shift)

    head = same(6)
    strict = (head & (cc < rr), head & (cc > rr))
    incl = (head & (cc <= rr), head & (cc >= rr))
    tri = ((tc <= tr).astype(BF16), (tc >= tr).astype(BF16))
    last = (c_len - 1, 0)

    def phase1(chains):
        dirs = [dr for dr, _ in chains]
        rows = [pl.ds(pl.multiple_of(c * c_len, c_len), c_len) for _, c in chains]
        lw = [lw_ref[dr, rw, :] for dr, rw in zip(dirs, rows)]
        cum = [_cumsum_rows(tri[dr], x) for dr, x in zip(dirs, lw)]
        a2, r2, b2, k2, v2, pc = [], [], [], [], [], []
        for dr, rw, lw_c, cum_c in zip(dirs, rows, lw, cum):
            a = a_ref[dr, rw, :]
            k = rkv_ref[1, rw, :]
            kk_c = kk_scr[rw, :]
            e_inc = jnp.exp(cum_c)
            e_inv = jnp.exp(-cum_c)
            a2.append(_stack_heads(-kk_c * jnp.exp(cum_c - lw_c), first).astype(BF16))
            r2.append(_stack_heads(rkv_ref[0, rw, :] * e_inc, first).astype(BF16))
            b2.append(_stack_heads(kk_c * a * e_inv, first).astype(BF16))
            k2.append(_stack_heads(k * (1.0 + (a - 1.0) * kap) * e_inv, first).astype(BF16))
            v2.append(_stack_heads(rkv_ref[2, rw, :], first).astype(BF16))
            pc.append(e_inc[last[dr]:last[dr] + 1, :])
        g = [_dot_nt(jnp.concatenate([x, y], axis=0), jnp.concatenate([z, w], axis=0))
             for x, y, z, w in zip(a2, r2, b2, k2)]
        l_ab = [jnp.where(strict[dr], x[:rows2, :rows2], 0.0) for dr, x in zip(dirs, g)]
        t = [eye + jnp.where(same(1), x, 0.0) for x in l_ab]
        side = {}
        for shift in range(1, 6):
            sib = same(shift + 1) & ~same(shift)
            tb = [x.astype(BF16) for x in t]
            mid = [_dot(jnp.where(sib, x, 0.0).astype(BF16), y) for x, y in zip(l_ab, tb)]
            if shift == 1:
                side['lv'] = [_dot(jnp.where(strict[dr], x[:rows2, rows2:], 0.0).astype(BF16), y)
                              for dr, x, y in zip(dirs, g, v2)]
            elif shift == 2:
                side['yl'] = [_dot(jnp.where(incl[dr], x[rows2:, rows2:], 0.0).astype(BF16), y)
                              for dr, x, y in zip(dirs, g, v2)]
            elif shift == 3:
                side['kv'] = [_dot_tn(x, y) for x, y in zip(v2, k2)]
            t = [x + _dot(y, z.astype(BF16)) for x, y, z in zip(t, tb, mid)]
        tb = [x.astype(BF16) for x in t]
        ta = [_dot(x, y) for x, y in zip(tb, a2)]
        w2 = [_dot(x, y.astype(BF16)) for x, y in zip(tb, side['lv'])]
        for i, (dr, c) in enumerate(chains):
            n = dr * n_ch + c
            tar_scr[n, :rows2, :] = ta[i].astype(BF16)
            tar_scr[n, rows2:, :] = r2[i]
            w2_scr[n] = w2[i]
            yl_scr[n] = side['yl'][i]
            kv_scr[n] = side['kv'][i]
            lrb_scr[n] = jnp.where(incl[dr], g[i][rows2:, :rows2], 0.0).astype(BF16)
            b2_scr[n] = b2[i]
            pc_scr[n] = pc[i]

    def body1(grp, carry):
        phase1([(dr, grp * RW_LOCK + j) for j in range(RW_LOCK) for dr in range(2)])
        return carry

    if n_ch == RW_LOCK:
        body1(0, 0)
    else:
        lax.fori_loop(0, n_ch // RW_LOCK, body1, 0)

    def body2(i, carry):
        cs = (i, n_ch - 1 - i)
        ns = [dr * n_ch + c for dr, c in enumerate(cs)]
        x = [_dot_nt(tar_scr[n], s2.astype(BF16)) for n, s2 in zip(ns, carry)]
        u2 = [(xx[:rows2] + w2_scr[n]).astype(BF16) for n, xx in zip(ns, x)]
        upd = [_dot_tn(u, b2_scr[n]) for n, u in zip(ns, u2)]
        yb = [_dot(lrb_scr[n], u) for n, u in zip(ns, u2)]
        out = []
        for dr, (c, n) in enumerate(zip(cs, ns)):
            y2 = x[dr][rows2:] + yb[dr] + yl_scr[n]
            y_scr[dr, pl.ds(pl.multiple_of(c * c_len, c_len), c_len), :] = y2[:c_len] + y2[c_len:]
            out.append((carry[dr] + upd[dr] + kv_scr[n]) * pc_scr[n])
        return tuple(out)

    if zero_init:
        init = (jnp.zeros((rows2, LANES), F32),) * 2
    else:
        init = (s0_ref[0, 0, 0], s0_ref[0, 1, 0])
    s_f, s_b = lax.fori_loop(0, n_ch, body2, init)

    y = y_scr[0] + y_scr[1]
    yc = y - _head_sum(y, first) * (1.0 / RWKV_HD)
    o = yc * lax.rsqrt(_head_sum(yc * yc, first) * (1.0 / RWKV_HD) + GN_EPS) * gn_ref[...]
    r = rkv_ref[0]
    k = rkv_ref[1]
    v = rkv_ref[2]
    for dr in range(2):
        kd = k * (1.0 + (a_ref[dr] - 1.0) * kap)
        o = o + _head_sum(r * kd * rkp_ref[...], first) * v
    o_ref[...] = o

    if zero_init:
        for dr, s2 in enumerate((s_f, s_b)):
            st_ref[0, 0, dr, 0] = s2[:RWKV_HD, :RWKV_HD]
            st_ref[0, 0, dr, 1] = s2[RWKV_HD:, RWKV_HD:]


def _rwkv_chunked(rkvg, lw, a, kkp, kap, rkp, gn, s0_pairs):
    npair = RWKV_HEADS // 2
    par = lambda *_: pl.BlockSpec((1, LANES), lambda s, p: (0, p))

    def scratch(seq):
        n = 2 * (seq // RW_C)
        r2 = 2 * RW_C
        return [
            pltpu.VMEM((seq, LANES), F32), pltpu.VMEM((2, seq, LANES), F32),
            pltpu.VMEM((n, 2 * r2, LANES), BF16), pltpu.VMEM((n, r2, r2), BF16), pltpu.VMEM((n, r2, LANES), BF16),
            pltpu.VMEM((n, r2, LANES), F32), pltpu.VMEM((n, r2, LANES), F32), pltpu.VMEM((n, r2, LANES), F32),
            pltpu.VMEM((n, 1, LANES), F32),
        ]

    def seq_specs(seq, rb):
        return [
            pl.BlockSpec((3, seq, LANES), lambda s, p: (0, rb + s, p)),
            pl.BlockSpec((2, seq, LANES), lambda s, p: (0, rb + s, p)),
            pl.BlockSpec((2, seq, LANES), lambda s, p: (0, rb + s, p)),
            par(), par(), par(), par(),
        ]

    pars = [u.reshape(1, D_MODEL) for u in (kkp, kap, rkp, gn)]
    o_p, st = pl.pallas_call(
        functools.partial(_rwkv_chunk_kernel, seq=SEQ, zero_init=True),
        grid=(BATCH, npair),
        in_specs=seq_specs(SEQ, 0),
        out_specs=[
            pl.BlockSpec((SEQ, LANES), lambda s, p: (s, p)),
            pl.BlockSpec((1, 1, 2, 2, RWKV_HD, RWKV_HD), lambda s, p: (s, 0, 0, p, 0, 0)),
        ],
        out_shape=[
            jax.ShapeDtypeStruct((N_TOK, D_MODEL), F32),
            jax.ShapeDtypeStruct((BATCH, 1, 2, RWKV_HEADS, RWKV_HD, RWKV_HD), F32),
        ],
        scratch_shapes=scratch(SEQ),
        compiler_params=_params("arbitrary", "arbitrary"),
        name="rwkv_prompt",
    )(rkvg, lw, a, *pars)
    rb = N_PROMPT_TOK // DEC_SEQ
    o = pl.pallas_call(
        functools.partial(_rwkv_chunk_kernel, seq=DEC_SEQ, zero_init=False),
        grid=(DEC_BATCH, npair),
        in_specs=seq_specs(DEC_SEQ, rb) + [
            pl.BlockSpec((1, 2, 1, LANES, LANES), lambda s, p: (s, 0, p, 0, 0)),
            pl.BlockSpec(memory_space=pl.ANY),
        ],
        out_specs=pl.BlockSpec((DEC_SEQ, LANES), lambda s, p: (rb + s, p)),
        out_shape=jax.ShapeDtypeStruct((N_TOK, D_MODEL), F32),
        input_output_aliases={8: 0},
        scratch_shapes=scratch(DEC_SEQ),
        compiler_params=_params("arbitrary", "arbitrary"),
        name="rwkv_latent",
    )(rkvg, lw, a, *pars, s0_pairs, o_p)
    return o, st


def _split3(x):
    hi = x.astype(BF16)
    r1 = x - hi.astype(F32)
    mid = r1.astype(BF16)
    return hi, mid, (r1 - mid.astype(F32)).astype(BF16)


def _rwkv_kernel(*refs, seq, zero_init, pg, np2):
    n_in = 7 if zero_init else 9
    rkv_ref, lw_ref, a_ref, kkp_ref, kap_ref, rkp_ref, gn_ref = refs[:7]
    if zero_init:
        o_ref, st_ref = refs[n_in:n_in + 2]
        scr = refs[n_in + 2:]
    else:
        s0_ref = refs[7]
        o_ref = refs[n_in]
        scr = refs[n_in + 1:]
    kk_scr, cum_scr, bon_scr, y_scr, s_scr, tar_scr, lrb_scr, b2_scr, w2_scr, yl_scr, kv_scr, pc_scr = scr
    c_len = RW_C
    n_ch = seq // c_len
    rows2 = 2 * c_len
    grp = pl.program_id(1)
    defer = np2 > pg
    base = grp * pg if defer else 0
    first = _first_half_lanes()

    rr = lax.broadcasted_iota(jnp.int32, (rows2, rows2), 0)
    cc = lax.broadcasted_iota(jnp.int32, (rows2, rows2), 1)
    eye = (rr == cc).astype(F32)

    def same(shift):
        return (rr >> shift) == (cc >> shift)

    head = same(6)
    strict = (head & (cc < rr), head & (cc > rr))
    incl = (head & (cc <= rr), head & (cc >= rr))
    last = (c_len - 1, 0)
    head_ones = head.astype(BF16)

    cs_rows = min(seq, 256)
    tr = lax.broadcasted_iota(jnp.int32, (cs_rows, cs_rows), 0)
    tc = lax.broadcasted_iota(jnp.int32, (cs_rows, cs_rows), 1)
    chunk = (tr >> 6) == (tc >> 6)
    tri = ((chunk & (tc <= tr)).astype(BF16), (chunk & (tc >= tr)).astype(BF16))
    for p in range(pg):
        ln = slice(p * LANES, (p + 1) * LANES)
        kap = kap_ref[:, ln]
        r = rkv_ref[0, :, ln]
        k = rkv_ref[1, :, ln]
        v = rkv_ref[2, :, ln]
        kk = k * kkp_ref[:, ln]
        kk_scr[p] = kk * lax.rsqrt(jnp.maximum(_head_sum(kk * kk, first), 1e-12))
        bonus = None
        for dr in range(2):
            kd = k * (1.0 + (a_ref[dr, :, ln] - 1.0) * kap)
            term = _head_sum(r * kd * rkp_ref[:, ln], first) * v
            bonus = term if bonus is None else bonus + term
        bon_scr[base + p] = bonus
    for p in range(0, pg, 2):
        for dr in range(2):
            for r0 in range(0, seq, cs_rows):
                parts = _split3(lw_ref[dr, r0:r0 + cs_rows, p * LANES:(p + 2) * LANES])
                cum = _dot(tri[dr], parts[0]) + _dot(tri[dr], parts[1]) + _dot(tri[dr], parts[2])
                cum_scr[p, dr, r0:r0 + cs_rows, :] = cum[:, :LANES]
                cum_scr[p + 1, dr, r0:r0 + cs_rows, :] = cum[:, LANES:]

    def phase1(chains):
        dirs = [dr for _, dr, _ in chains]
        a2, r2, b2, k2, v2, pc = [], [], [], [], [], []
        for p, dr, c in chains:
            ln = slice(p * LANES, (p + 1) * LANES)
            rw = pl.ds(pl.multiple_of(c * c_len, c_len), c_len)
            a = a_ref[dr, rw, ln]
            k = rkv_ref[1, rw, ln]
            kk_c = kk_scr[p, rw, :]
            cum_c = cum_scr[p, dr, rw, :]
            e_inc = jnp.exp(cum_c)
            e_inv = jnp.exp(-cum_c)
            a2.append(_stack_heads(-kk_c * jnp.exp(cum_c - lw_ref[dr, rw, ln]), first).astype(BF16))
            r2.append(_stack_heads(rkv_ref[0, rw, ln] * e_inc, first).astype(BF16))
            b2.append(_stack_heads(kk_c * a * e_inv, first).astype(BF16))
            k2.append(_stack_heads(k * (1.0 + (a - 1.0) * kap_ref[:, ln]) * e_inv, first).astype(BF16))
            v2.append(_stack_heads(rkv_ref[2, rw, ln], first).astype(BF16))
            pc.append(e_inc[last[dr]:last[dr] + 1, :])
        g = [_dot_nt(jnp.concatenate([x, y], axis=0), jnp.concatenate([z, w], axis=0))
             for x, y, z, w in zip(a2, r2, b2, k2)]
        l_ab = [jnp.where(strict[dr], x[:rows2, :rows2], 0.0) for dr, x in zip(dirs, g)]
        t = [eye + jnp.where(same(1), x, 0.0) for x in l_ab]
        side = {}
        for shift in range(1, 6):
            sib = same(shift + 1) & ~same(shift)
            tb = [x.astype(BF16) for x in t]
            mid = [_dot(jnp.where(sib, x, 0.0).astype(BF16), y) for x, y in zip(l_ab, tb)]
            if shift == 1:
                side['lv'] = [_dot(jnp.where(strict[dr], x[:rows2, rows2:], 0.0).astype(BF16), y)
                              for dr, x, y in zip(dirs, g, v2)]
            elif shift == 2:
                side['yl'] = [_dot(jnp.where(incl[dr], x[rows2:, rows2:], 0.0).astype(BF16), y)
                              for dr, x, y in zip(dirs, g, v2)]
            elif shift == 3:
                side['kv'] = [_dot_tn(x, y) for x, y in zip(v2, k2)]
            t = [x + _dot(y, z.astype(BF16)) for x, y, z in zip(t, tb, mid)]
        tb = [x.astype(BF16) for x in t]
        ta = [_dot(x, y) for x, y in zip(tb, a2)]
        w2 = [_dot(x, y.astype(BF16)) for x, y in zip(tb, side['lv'])]
        for i, (p, dr, c) in enumerate(chains):
            n = ((base + p) * 2 + dr) * n_ch + c
            tar_scr[n, :rows2, :] = ta[i].astype(BF16)
            tar_scr[n, rows2:, :] = r2[i]
            w2_scr[n] = w2[i]
            yl_scr[n] = side['yl'][i]
            kv_scr[n] = side['kv'][i]
            lrb_scr[n] = jnp.where(incl[dr], g[i][rows2:, :rows2], 0.0).astype(BF16)
            b2_scr[n] = b2[i]
            pc_scr[n] = pc[i]

    def body1(cg, carry):
        phase1([(p, dr, cg * RW_LOCK + j) for p in range(pg) for j in range(RW_LOCK) for dr in range(2)])
        return carry

    if n_ch == RW_LOCK:
        body1(0, 0)
    else:
        lax.fori_loop(0, n_ch // RW_LOCK, body1, 0)

    def finish():
        for p in range(np2):
            for dr in range(2):
                if zero_init:
                    s_scr[2 * p + dr] = jnp.zeros((rows2, LANES), F32)
                else:
                    s_scr[2 * p + dr] = s0_ref[0, dr, p]

        def body2(i, carry):
            cs = (i, n_ch - 1 - i)
            ids = [(p, dr) for p in range(np2) for dr in range(2)]
            ns = [(p * 2 + dr) * n_ch + cs[dr] for p, dr in ids]
            x = [_dot_nt(tar_scr[n], s_scr[2 * p + dr].astype(BF16)) for n, (p, dr) in zip(ns, ids)]
            u2 = [(xx[:rows2] + w2_scr[n]).astype(BF16) for n, xx in zip(ns, x)]
            upd = [_dot_tn(u, b2_scr[n]) for n, u in zip(ns, u2)]
            yb = [_dot(lrb_scr[n], u) for n, u in zip(ns, u2)]
            for j, (n, (p, dr)) in enumerate(zip(ns, ids)):
                y2 = x[j][rows2:] + yb[j] + yl_scr[n]
                y_scr[p, dr, pl.ds(pl.multiple_of(cs[dr] * c_len, c_len), c_len), :] = y2[:c_len] + y2[c_len:]
                s_scr[2 * p + dr] = (s_scr[2 * p + dr] + upd[j] + kv_scr[n]) * pc_scr[n]
            return carry

        lax.fori_loop(0, n_ch, body2, 0)

        def head_mean(xs):
            parts = [_split3(x) for x in xs]
            return [(_dot(a, head_ones) + _dot(b, head_ones) + _dot(c, head_ones)) * (1.0 / RWKV_HD)
                    for a, b, c in parts]

        ys = [y_scr[p, 0] + y_scr[p, 1] for p in range(np2)]
        yc = [y - m for y, m in zip(ys, head_mean(ys))]
        var = head_mean([x * x for x in yc])
        for p in range(np2):
            ln = slice(p * LANES, (p + 1) * LANES)
            o_ref[:, ln] = yc[p] * lax.rsqrt(var[p] + GN_EPS) * gn_ref[:, ln] + bon_scr[p]
            if zero_init:
                for dr in range(2):
                    s2 = s_scr[2 * p + dr]
                    st_ref[0, 0, dr, 2 * p] = s2[:RWKV_HD, :RWKV_HD]
                    st_ref[0, 0, dr, 2 * p + 1] = s2[RWKV_HD:, RWKV_HD:]

    if defer:
        pl.when(grp == pl.num_programs(1) - 1)(finish)
    else:
        finish()


def _rwkv_mixer(rkvg, lw, a, kkp, kap, rkp, gn, s0_pairs):
    npair = RWKV_HEADS // 2
    pg = RW_PAIRS
    wl = pg * LANES
    r2 = 2 * RW_C

    def scratch(seq, np2):
        n = 2 * np2 * (seq // RW_C)
        return [
            pltpu.VMEM((pg, seq, LANES), F32), pltpu.VMEM((pg, 2, seq, LANES), F32),
            pltpu.VMEM((np2, seq, LANES), F32), pltpu.VMEM((np2, 2, seq, LANES), F32),
            pltpu.VMEM((2 * np2, r2, LANES), F32),
            pltpu.VMEM((n, 2 * r2, LANES), BF16), pltpu.VMEM((n, r2, r2), BF16), pltpu.VMEM((n, r2, LANES), BF16),
            pltpu.VMEM((n, r2, LANES), F32), pltpu.VMEM((n, r2, LANES), F32), pltpu.VMEM((n, r2, LANES), F32),
            pltpu.VMEM((n, 1, LANES), F32),
        ]

    def seq_specs(seq, rb, gn_spec):
        par = pl.BlockSpec((1, wl), lambda s, g: (0, g))
        return [
            pl.BlockSpec((3, seq, wl), lambda s, g: (0, rb + s, g)),
            pl.BlockSpec((2, seq, wl), lambda s, g: (0, rb + s, g)),
            pl.BlockSpec((2, seq, wl), lambda s, g: (0, rb + s, g)),
            par, par, par, gn_spec,
        ]

    pars = [u.reshape(1, D_MODEL) for u in (kkp, kap, rkp, gn)]
    o_p, st = pl.pallas_call(
        functools.partial(_rwkv_kernel, seq=SEQ, zero_init=True, pg=pg, np2=npair),
        grid=(BATCH, npair // pg),
        in_specs=seq_specs(SEQ, 0, pl.BlockSpec((1, D_MODEL), lambda s, g: (0, 0))),
        out_specs=[
            pl.BlockSpec((SEQ, D_MODEL), lambda s, g: (s, 0)),
            pl.BlockSpec((1, 1, 2, RWKV_HEADS, RWKV_HD, RWKV_HD), lambda s, g: (s, 0, 0, 0, 0, 0)),
        ],
        out_shape=[
            jax.ShapeDtypeStruct((N_TOK, D_MODEL), F32),
            jax.ShapeDtypeStruct((BATCH, 1, 2, RWKV_HEADS, RWKV_HD, RWKV_HD), F32),
        ],
        scratch_shapes=scratch(SEQ, npair),
        compiler_params=_params("arbitrary", "arbitrary"),
        name="rwkv_prompt",
    )(rkvg, lw, a, *pars)
    rb = N_PROMPT_TOK // DEC_SEQ
    o = pl.pallas_call(
        functools.partial(_rwkv_kernel, seq=DEC_SEQ, zero_init=False, pg=pg, np2=pg),
        grid=(DEC_BATCH, npair // pg),
        in_specs=seq_specs(DEC_SEQ, rb, pl.BlockSpec((1, wl), lambda s, g: (0, g))) + [
            pl.BlockSpec((1, 2, pg, LANES, LANES), lambda s, g: (s, 0, g, 0, 0)),
            pl.BlockSpec(memory_space=pl.ANY),
        ],
        out_specs=pl.BlockSpec((DEC_SEQ, wl), lambda s, g: (rb + s, g)),
        out_shape=jax.ShapeDtypeStruct((N_TOK, D_MODEL), F32),
        input_output_aliases={8: 0},
        scratch_shapes=scratch(DEC_SEQ, pg),
        compiler_params=_params("arbitrary", "arbitrary"),
        name="rwkv_latent",
    )(rkvg, lw, a, *pars, s0_pairs, o_p)
    return o, st


def _state_pairs(s0):
    s = s0.reshape(DEC_BATCH, 2, RWKV_HEADS // 2, 2, RWKV_HD, RWKV_HD)
    z = jnp.zeros_like(s[:, :, :, 0])
    top = jnp.concatenate([s[:, :, :, 0], z], axis=-1)
    bot = jnp.concatenate([z, s[:, :, :, 1]], axis=-1)
    return jnp.concatenate([top, bot], axis=-2)


def _layer_rwkv(x, p, mod, j):
    i = N_MIXERS * j + 1
    wa, wb, aa, ab = p['rwkv_wA'][j], p['rwkv_wB'][j], p['rwkv_aA'][j], p['rwkv_aB'][j]
    z = jnp.zeros_like(wb[0])
    wa2 = jnp.concatenate([wa[0], wa[1]], axis=1).astype(BF16)
    aa2 = jnp.concatenate([aa[0], aa[1]], axis=1).astype(BF16)
    wb_pad = jnp.stack([jnp.concatenate([wb[0], z]), jnp.concatenate([z, wb[1]])]).astype(BF16)
    ab_pad = jnp.stack([jnp.concatenate([ab[0], z]), jnp.concatenate([z, ab[1]])]).astype(BF16)
    rkvg, lw, a = _rwkv_prep(x, p['norm_w'][i], mod, p['rwkv_mu'][j], wa2, aa2, wb_pad, ab_pad,
                             p['rwkv_w0'][j], p['rwkv_a0'][j], p['rwkv_w_in'][j])
    o, st = _rwkv_mixer(rkvg, lw, a, p['rwkv_kk'][j], p['rwkv_ka'][j], p['rwkv_rk'][j], p['rwkv_gn'][j],
                          _state_pairs(p['state_rwkv'][:, j]))
    x = _out_proj(o, rkvg.reshape(4 * N_TOK, D_MODEL), 0, p['rwkv_w_out'][j], x, mod, p['final_norm_w'], False,
                  g_row0=3 * N_TOK)
    return x, st


DIFF_W = 2 * DIFF_HD
ATT_QB = 256
DIFF_GROUP = 4


def _first_half_lanes():
    return lax.broadcasted_iota(jnp.int32, (1, LANES), 1) < LANES // 2


def _diff_lambda(lam_ref, lam_init):
    lp = lam_ref[...]
    return (jnp.exp(jnp.sum(lp[0:1] * lp[1:2], keepdims=True))
            - jnp.exp(jnp.sum(lp[2:3] * lp[3:4], keepdims=True)) + lam_init)


def _diff_heads(items, lam, lam_init):
    first = _first_half_lanes()
    scale = DIFF_HD ** -0.5
    sub = [(q, keys, comp) for q, keys, _ in items for comp in range(2)]
    qm = [jnp.where(first if comp == 0 else ~first, q, 0.0).astype(BF16) for q, _, comp in sub]
    s = [[_dot_nt(x, kb) * scale for kb, _ in keys] for x, (_, keys, _) in zip(qm, sub)]
    m = [functools.reduce(jnp.maximum, [jnp.max(u, axis=-1, keepdims=True) for u in ss]) for ss in s]
    e = [[jnp.exp(u - mm) for u in ss] for ss, mm in zip(s, m)]
    inv = [1.0 / functools.reduce(lambda x, y: x + y, [jnp.sum(u, axis=-1, keepdims=True) for u in ee]) for ee in e]
    outs = []
    for i, (_, keys, gn) in enumerate(items):
        o = None
        for n, (_, vb) in enumerate(keys):
            p = e[2 * i][n] * inv[2 * i] - lam * (e[2 * i + 1][n] * inv[2 * i + 1])
            part = _dot(p.astype(BF16), vb)
            o = part if o is None else o + part
        outs.append(o)
    return [o * lax.rsqrt(jnp.mean(o * o, axis=-1, keepdims=True) + EPS) * gn * (1.0 - lam_init)
            for o, (_, _, gn) in zip(outs, items)]


def _diff_prompt_kernel(lam_ref, q_ref, k_ref, v_ref, gn_ref, o_ref, *, lam_init):
    lam = _diff_lambda(lam_ref, lam_init)
    for h0 in range(0, DIFF_HEADS, DIFF_GROUP):
        items = []
        for h in range(h0, h0 + DIFF_GROUP):
            sl = slice(h * DIFF_W, (h + 1) * DIFF_W)
            items.append((q_ref[:, sl], [(k_ref[:, sl].astype(BF16), v_ref[:, sl].astype(BF16))], gn_ref[:, sl]))
        for h, o in zip(range(h0, h0 + DIFF_GROUP), _diff_heads(items, lam, lam_init)):
            o_ref[:, h * DIFF_W:(h + 1) * DIFF_W] = o


def _diff_latent_kernel(lam_ref, q_ref, k_ref, v_ref, ck_ref, cv_ref, cos_ref, slo_ref, shi_ref, gn_ref,
                        _prev_ref, o_ref, *, lam_init):
    lam = _diff_lambda(lam_ref, lam_init)
    tabs = (cos_ref[...], slo_ref[...], shi_ref[...])
    q = _rope(q_ref[...].astype(F32), *tabs, DIFF_HD // 4)
    k = _rope(k_ref[...].astype(F32), *tabs, DIFF_HD // 4)
    keys = [(k.astype(BF16), v_ref[...].astype(BF16)),
            (ck_ref[0, 0, 0].astype(BF16), cv_ref[0, 0, 0].astype(BF16))]
    gn = gn_ref[...]
    n_blk = DEC_SEQ // ATT_QB
    items = [(q[qi * ATT_QB:(qi + 1) * ATT_QB], keys, gn) for qi in range(n_blk)]
    for qi, o in enumerate(_diff_heads(items, lam, lam_init)):
        o_ref[qi * ATT_QB:(qi + 1) * ATT_QB, :] = o


def _diff_attention(proj, lam_p, gn_w, cache_k, cache_v, j, lam_init):
    gn = gn_w.reshape(1, D_MODEL)
    lam_spec = pl.BlockSpec((4, DIFF_HD), lambda *_: (0, 0))
    o_p = pl.pallas_call(
        functools.partial(_diff_prompt_kernel, lam_init=lam_init),
        grid=(BATCH,),
        in_specs=[
            lam_spec,
            pl.BlockSpec((SEQ, D_MODEL), lambda b: (b, 0)),
            pl.BlockSpec((SEQ, D_MODEL), lambda b: (b, 1)),
            pl.BlockSpec((SEQ, D_MODEL), lambda b: (b, 2)),
            pl.BlockSpec((1, D_MODEL), lambda b: (0, 0)),
        ],
        out_specs=pl.BlockSpec((SEQ, D_MODEL), lambda b: (b, 0)),
        out_shape=jax.ShapeDtypeStruct((N_TOK, D_MODEL), F32),
        compiler_params=_params("arbitrary"),
        name="diff_prompt",
    )(lam_p, proj, proj, proj, gn)
    cos, slo, shi = (jnp.concatenate([u, u], axis=-1) for u in _rope_tables(DIFF_HD))
    rb = N_PROMPT_TOK // DEC_SEQ
    nh = DIFF_HEADS
    tab = pl.BlockSpec((DEC_SEQ, DIFF_W), lambda b, h: (0, 0))
    cache = pl.BlockSpec((1, 1, 1, PAST_LEN, DIFF_W), lambda b, h: (b, j, h, 0, 0))
    o = pl.pallas_call(
        functools.partial(_diff_latent_kernel, lam_init=lam_init),
        grid=(DEC_BATCH, nh),
        in_specs=[
            lam_spec,
            pl.BlockSpec((DEC_SEQ, DIFF_W), lambda b, h: (rb + b, h)),
            pl.BlockSpec((DEC_SEQ, DIFF_W), lambda b, h: (rb + b, nh + h)),
            pl.BlockSpec((DEC_SEQ, DIFF_W), lambda b, h: (rb + b, 2 * nh + h)),
            cache, cache, tab, tab, tab,
            pl.BlockSpec((1, DIFF_W), lambda b, h: (0, h)),
            pl.BlockSpec(memory_space=pl.ANY),
        ],
        out_specs=pl.BlockSpec((DEC_SEQ, DIFF_W), lambda b, h: (rb + b, h)),
        out_shape=jax.ShapeDtypeStruct((N_TOK, D_MODEL), F32),
        input_output_aliases={10: 0},
        compiler_params=_params("arbitrary", "arbitrary"),
        name="diff_latent",
    )(lam_p, proj, proj, proj, cache_k, cache_v, cos, slo, shi, gn, o_p)
    return o


def _layer_diff(x, p, mod, j, i):
    lam_init = 0.8 - 0.6 * math.exp(-0.3 * i)
    proj, new_k, new_v = _in_proj_kv(x, p['norm_w'][i], mod, p['diff_w_in'][j], DIFF_HEADS)
    o = _diff_attention(proj, p['diff_lambda'][j], p['diff_gn'][j], p['cache_diff_k'], p['cache_diff_v'], j,
                        lam_init)
    x = _out_proj(o, proj, 3, p['diff_w_out'][j], x, mod, p['final_norm_w'], False)
    return x, new_k, new_v


NA_ROWS = DEC_SEQ // GRID_W
NA_WR = min(NA_WIN_R, NA_ROWS)
NA_LOC = NA_WR * GRID_W
NA_ROW_GROUP = 4
NA_PAIR_GROUP = 2


def _na_prompt_kernel(q_ref, k_ref, v_ref, o_ref):
    first = _first_half_lanes()
    scale = NA_HD ** -0.5
    for p0 in range(0, NA_HEADS // 2, NA_PAIR_GROUP):
        pairs = range(p0, p0 + NA_PAIR_GROUP)
        kb = [k_ref[:, pr * LANES:(pr + 1) * LANES].astype(BF16) for pr in pairs]
        vb = [v_ref[:, pr * LANES:(pr + 1) * LANES].astype(BF16) for pr in pairs]
        items = [(i, half) for i in range(NA_PAIR_GROUP) for half in range(2)]
        qm = [jnp.where(first if half == 0 else ~first, q_ref[:, (p0 + i) * LANES:(p0 + i + 1) * LANES], 0.0)
              .astype(BF16) for i, half in items]
        s = [_dot_nt(x, kb[i]) * scale for x, (i, _) in zip(qm, items)]
        e = [jnp.exp(x - jnp.max(x, axis=-1, keepdims=True)) for x in s]
        inv = [1.0 / jnp.sum(x, axis=-1, keepdims=True) for x in e]
        outs = [_dot(x.astype(BF16), vb[i]) * z for x, z, (i, _) in zip(e, inv, items)]
        for i in range(NA_PAIR_GROUP):
            o_ref[:, (p0 + i) * LANES:(p0 + i + 1) * LANES] = jnp.where(first, outs[2 * i], outs[2 * i + 1])


def _na_latent_kernel(q_ref, k_ref, v_ref, kc_ref, vc_ref, tab_ref, _prev_ref, o_ref):
    first = _first_half_lanes()
    scale = NA_HD ** -0.5
    kb = k_ref[...].astype(BF16)
    vb = v_ref[...].astype(BF16)
    kcb = kc_ref[0, 0].astype(BF16)
    vcb = vc_ref[0, 0].astype(BF16)
    qcol = lax.broadcasted_iota(jnp.int32, (GRID_W, NA_LOC), 0)
    kcol = lax.broadcasted_iota(jnp.int32, (GRID_W, NA_LOC), 1) & (GRID_W - 1)
    cstart = jnp.clip(qcol - NA_WIN_C // 2, 0, GRID_W - NA_WIN_C)
    col_ok = (kcol >= cstart) & (kcol < cstart + NA_WIN_C)
    def bias_of(r, rs, half):
        parts = []
        for w in range(0, NA_WR, 2):
            src = jnp.broadcast_to(tab_ref[half, rs + w - r + NA_WIN_R - 1], (GRID_W, LANES))
            parts.append(pltpu.roll(src, LANES - (NA_WIN_C - 1), axis=1, stride=1, stride_axis=0))
        return jnp.concatenate(parts, axis=1)

    for r0 in range(0, NA_ROWS, NA_ROW_GROUP):
        items = [(r, min(max(r - NA_WR // 2, 0), NA_ROWS - NA_WR), half)
                 for r in range(r0, r0 + NA_ROW_GROUP) for half in range(2)]
        qm = [jnp.where(first if half == 0 else ~first, q_ref[r * GRID_W:(r + 1) * GRID_W, :], 0.0).astype(BF16)
              for r, _, half in items]
        s_loc = [_dot_nt(x, kb[rs * GRID_W:(rs + NA_WR) * GRID_W]) for x, (_, rs, _) in zip(qm, items)]
        s_ctx = [_dot_nt(x, kcb) * scale for x in qm]
        s_loc = [jnp.where(col_ok, x * scale + bias_of(*it), -jnp.inf) for x, it in zip(s_loc, items)]
        m = [jnp.maximum(jnp.max(x, axis=-1, keepdims=True), jnp.max(y, axis=-1, keepdims=True))
             for x, y in zip(s_loc, s_ctx)]
        e_loc = [jnp.exp(x - mm) for x, mm in zip(s_loc, m)]
        e_ctx = [jnp.exp(x - mm) for x, mm in zip(s_ctx, m)]
        inv = [1.0 / (jnp.sum(x, axis=-1, keepdims=True) + jnp.sum(y, axis=-1, keepdims=True))
               for x, y in zip(e_loc, e_ctx)]
        pv = [_dot(x.astype(BF16), vb[rs * GRID_W:(rs + NA_WR) * GRID_W]) for x, (_, rs, _) in zip(e_loc, items)]
        pc = [_dot(x.astype(BF16), vcb) for x in e_ctx]
        outs = [(x + y) * z for x, y, z in zip(pv, pc, inv)]
        for n in range(0, len(items), 2):
            r = items[n][0]
            o_ref[r * GRID_W:(r + 1) * GRID_W, :] = jnp.where(first, outs[n], outs[n + 1])


def _na_bias_pairs(table):
    t = table.astype(F32)
    nc = 2 * NA_WIN_C - 1
    z = jnp.zeros(t[:, :-1].shape[:2] + (GRID_W - nc,), F32)
    return jnp.concatenate([t[:, :-1], z, t[:, 1:], z], axis=-1)[:, :, None, :]


def _pair_heads(cache):
    c = cache.reshape(DEC_BATCH, NA_HEADS // 2, 2, PAST_LEN, NA_HD)
    return c.transpose(0, 1, 3, 2, 4).reshape(DEC_BATCH, NA_HEADS // 2, PAST_LEN, LANES)


def _na_attention(proj, bias_table, cache_k, cache_v):
    o_p = pl.pallas_call(
        _na_prompt_kernel,
        grid=(BATCH,),
        in_specs=[
            pl.BlockSpec((SEQ, D_MODEL), lambda b: (b, 0)),
            pl.BlockSpec((SEQ, D_MODEL), lambda b: (b, 1)),
            pl.BlockSpec((SEQ, D_MODEL), lambda b: (b, 2)),
        ],
        out_specs=pl.BlockSpec((SEQ, D_MODEL), lambda b: (b, 0)),
        out_shape=jax.ShapeDtypeStruct((N_TOK, D_MODEL), F32),
        compiler_params=_params("arbitrary"),
        name="na_prompt",
    )(proj, proj, proj)
    rb = N_PROMPT_TOK // DEC_SEQ
    npair = NA_HEADS // 2
    cache = pl.BlockSpec((1, 1, PAST_LEN, LANES), lambda pr, b: (b, pr, 0, 0))
    o = pl.pallas_call(
        _na_latent_kernel,
        grid=(npair, DEC_BATCH),
        in_specs=[
            pl.BlockSpec((DEC_SEQ, LANES), lambda pr, b: (rb + b, pr)),
            pl.BlockSpec((DEC_SEQ, LANES), lambda pr, b: (rb + b, npair + pr)),
            pl.BlockSpec((DEC_SEQ, LANES), lambda pr, b: (rb + b, 2 * npair + pr)),
            cache, cache,
            pl.BlockSpec((2, 2 * NA_WIN_R - 2, 1, LANES), lambda pr, b: (pr, 0, 0, 0)),
            pl.BlockSpec(memory_space=pl.ANY),
        ],
        out_specs=pl.BlockSpec((DEC_SEQ, LANES), lambda pr, b: (rb + b, pr)),
        out_shape=jax.ShapeDtypeStruct((N_TOK, D_MODEL), F32),
        input_output_aliases={6: 0},
        compiler_params=_params("arbitrary", "arbitrary"),
        name="na_latent",
    )(proj, proj, proj, _pair_heads(cache_k), _pair_heads(cache_v), _na_bias_pairs(bias_table), o_p)
    return o


def _layer_na(x, p, mod, j, final):
    i = N_MIXERS * j + 3
    proj, new_k, new_v = _in_proj_kv(x, p['norm_w'][i], mod, p['na_w_in'][j], NA_HEADS)
    o = _na_attention(proj, p['na_bias'][j], p['cache_na_k'][:, j], p['cache_na_v'][:, j])
    args = (o, proj, 3, p['na_w_out'][j], x, mod, p['final_norm_w'])
    if final:
        x = (_out_proj(*args, True, rows=(0, N_PROMPT_TOK)), _out_proj(*args, True, rows=(N_PROMPT_TOK, N_TOK)))
    else:
        x = _out_proj(*args, False)
    return x, new_k, new_v


def kernel(x_prompt, x_sample, state_ret, state_rwkv, cache_diff_k, cache_diff_v, cache_na_k, cache_na_v,
           c, c_ctx, norm_w, w_mod, b_mod, final_norm_w,
           ret_w_in, ret_decay, ret_gn, ret_w_out,
           rwkv_mu, rwkv_w_in, rwkv_w0, rwkv_wA, rwkv_wB, rwkv_a0, rwkv_aA, rwkv_aB,
           rwkv_kk, rwkv_ka, rwkv_rk, rwkv_gn, rwkv_w_out,
           diff_w_in, diff_lambda, diff_gn, diff_w_out,
           na_w_in, na_bias, na_w_out):
    p = dict(locals())
    cond = jnp.zeros((N_COND, D_MODEL), F32).at[0].set(c_ctx).at[1:1 + DEC_BATCH].set(c)
    mods = _modulation(cond, w_mod, b_mod)
    x = (x_prompt.reshape(N_PROMPT_TOK, D_MODEL), x_sample.reshape(N_SAMPLE_TOK, D_MODEL))
    new = {n: [] for n in ('ret', 'rwkv', 'dk', 'dv', 'nk', 'nv')}
    for i in range(DEPTH):
        kind, j = i % N_MIXERS, i // N_MIXERS
        if kind == 0:
            x, st = _layer_ret(x, p, mods[i], j)
            new['ret'].append(st)
        elif kind == 1:
            x, st = _layer_rwkv(x, p, mods[i], j)
            new['rwkv'].append(st)
        elif kind == 2:
            x, ck, cv = _layer_diff(x, p, mods[i], j, i)
            new['dk'].append(ck)
            new['dv'].append(cv)
        else:
            x, ck, cv = _layer_na(x, p, mods[i], j, final=(i == DEPTH - 1))
            new['nk'].append(ck)
            new['nv'].append(cv)
    if DEPTH % N_MIXERS:
        raise NotImplementedError("the final norm is fused into the last neighbourhood-attention layer")
    cat = lambda xs: xs[0] if len(xs) == 1 else jnp.concatenate(xs, axis=1)
    return (x[0].reshape(BATCH, SEQ, D_MODEL), x[1].reshape(DEC_BATCH, DEC_SEQ, D_MODEL),
            cat(new['ret']), cat(new['rwkv']), cat(new['dk']), cat(new['dv']), cat(new['nk']), cat(new['nv']))
```

```python
import functools
import math

import jax
import jax.numpy as jnp
from jax import lax
from jax.experimental import pallas as pl
from jax.experimental.pallas import tpu as pltpu

F32 = jnp.float32
BF16 = jnp.bfloat16

D_MODEL = 1024
BATCH = 32
SEQ = 256
DEPTH = 4
N_MIXERS = 4
DEC_BATCH = 2
DEC_SEQ = 1024
PAST_LEN = 256
GRID_W = 64

RET_HEADS = 4
RET_DK = 256
RET_DV = 512
RET_QK = 1024
RET_V = 2048

RWKV_HD = 64
RWKV_HEADS = 16
RWKV_RANK = 64

DIFF_HEADS = 8
DIFF_HD = 64

NA_HEADS = 16
NA_HD = 64
NA_WIN_R = 8
NA_WIN_C = 16

ROPE_BASE = 10000.0
EPS = 1e-6
GN_EPS = 1e-5

N_PROMPT_TOK = BATCH * SEQ
N_SAMPLE_TOK = DEC_BATCH * DEC_SEQ
N_TOK = N_PROMPT_TOK + N_SAMPLE_TOK
N_COND = 8

LANES = 128
VMEM_LIMIT = 56 * 2 ** 20


def _params(*sem):
    return pltpu.CompilerParams(dimension_semantics=sem, vmem_limit_bytes=VMEM_LIMIT)


def _cond_of_tile(i, tm):
    npt = N_PROMPT_TOK // tm
    return jnp.where(i < npt, 0, 1 + (i - npt) // (DEC_SEQ // tm))


def _sigmoid(x):
    return 1.0 / (1.0 + jnp.exp(-x))


def _silu(x):
    return x * _sigmoid(x)


def _dot(a, b):
    return jnp.dot(a, b, preferred_element_type=F32)


def _dot_nt(a, b):
    return lax.dot_general(a, b, (((1,), (1,)), ((), ())), preferred_element_type=F32)


def _dot_tn(a, b):
    return lax.dot_general(a, b, (((0,), (0,)), ((), ())), preferred_element_type=F32)


def _softmax_rows(s):
    m = jnp.max(s, axis=-1, keepdims=True)
    e = jnp.exp(s - m)
    return e / jnp.sum(e, axis=-1, keepdims=True)


def _mod_kernel(c_ref, w_ref, b_ref, o_ref):
    s = _silu(c_ref[...])
    o_ref[0] = jnp.dot(s, w_ref[0], precision=lax.Precision.HIGHEST, preferred_element_type=F32) + b_ref[0]


def _modulation(cond, w_mod, b_mod):
    tn = D_MODEL
    out = pl.pallas_call(
        _mod_kernel,
        grid=(DEPTH, 3 * D_MODEL // tn),
        in_specs=[
            pl.BlockSpec((N_COND, D_MODEL), lambda l, j: (0, 0)),
            pl.BlockSpec((1, D_MODEL, tn), lambda l, j: (l, 0, j)),
            pl.BlockSpec((1, 1, tn), lambda l, j: (l, 0, j)),
        ],
        out_specs=pl.BlockSpec((1, N_COND, tn), lambda l, j: (l, 0, j)),
        out_shape=jax.ShapeDtypeStruct((DEPTH, N_COND, 3 * D_MODEL), F32),
        compiler_params=_params("arbitrary", "arbitrary"),
        name="modulation",
    )(cond, w_mod, b_mod.reshape(DEPTH, 1, 3 * D_MODEL))
    return out.reshape(DEPTH, N_COND, 3, 1, D_MODEL)


def _norm_mod(x, nw, mod_ref):
    ms = jnp.mean(x * x, axis=-1, keepdims=True)
    y = x * lax.rsqrt(ms + EPS) * nw
    return y * (1.0 + mod_ref[0, 1]) + mod_ref[0, 0]


IN_TM = 1024
IN_TN = 2048


def _x_specs(x, tm, tile_of):
    if not isinstance(x, tuple):
        return [pl.BlockSpec((tm, D_MODEL), lambda *g: (tile_of(*g), 0))], (x,)
    npt = N_PROMPT_TOK // tm
    return [pl.BlockSpec((tm, D_MODEL), lambda *g: (jnp.minimum(tile_of(*g), npt - 1), 0)),
            pl.BlockSpec((tm, D_MODEL), lambda *g: (jnp.maximum(tile_of(*g) - npt, 0), 0))], x


def _read_x(x_refs, tile, tm):
    if len(x_refs) == 1:
        return x_refs[0][...]
    return jnp.where(tile < N_PROMPT_TOK // tm, x_refs[0][...], x_refs[1][...])


def _in_proj_kernel(*refs, n_x):
    x_refs = refs[:n_x]
    nw_ref, mod_ref, w_ref, o_ref, h_ref = refs[n_x:]

    @pl.when(pl.program_id(1) == 0)
    def _():
        x = _read_x(x_refs, pl.program_id(0), IN_TM)
        h_ref[...] = _norm_mod(x, nw_ref[...], mod_ref).astype(BF16)

    o_ref[...] = _dot(h_ref[...], w_ref[...]).astype(o_ref.dtype)


def _in_proj(x, norm_w, mod, w, tn, out_dtype=F32):
    n = w.shape[1]
    w = w.astype(BF16)
    x_specs, xs = _x_specs(x, IN_TM, lambda i, j: i)
    return pl.pallas_call(
        functools.partial(_in_proj_kernel, n_x=len(xs)),
        grid=(N_TOK // IN_TM, n // tn),
        in_specs=x_specs + [
            pl.BlockSpec((1, D_MODEL), lambda i, j: (0, 0)),
            pl.BlockSpec((1, 3, 1, D_MODEL), lambda i, j: (_cond_of_tile(i, IN_TM), 0, 0, 0)),
            pl.BlockSpec((D_MODEL, tn), lambda i, j: (0, j)),
        ],
        out_specs=pl.BlockSpec((IN_TM, tn), lambda i, j: (i, j)),
        out_shape=jax.ShapeDtypeStruct((N_TOK, n), out_dtype),
        scratch_shapes=[pltpu.VMEM((IN_TM, D_MODEL), BF16)],
        compiler_params=_params("arbitrary", "arbitrary"),
        name="in_proj",
    )(*xs, norm_w.reshape(1, D_MODEL), mod, w)


def _in_proj_kv_kernel(x_ref, nw_ref, mod_ref, w_ref, o_ref, ck_ref, cv_ref, h_ref, *, heads, tn):
    i = pl.program_id(0)
    j = pl.program_id(1)

    @pl.when(j == 0)
    def _():
        h_ref[...] = _norm_mod(x_ref[...], nw_ref[...], mod_ref).astype(BF16)

    acc = _dot(h_ref[...], w_ref[...])
    o_ref[...] = acc.astype(o_ref.dtype)
    hd = D_MODEL // heads
    for col, c_ref in ((D_MODEL, ck_ref), (2 * D_MODEL, cv_ref)):
        @pl.when((j == col // tn) & (i < N_PROMPT_TOK // IN_TM))
        def _(c_ref=c_ref, c0=col % tn):
            for s in range(IN_TM // SEQ):
                for h in range(heads):
                    c_ref[s, 0, h] = acc[s * SEQ:(s + 1) * SEQ, c0 + h * hd:c0 + (h + 1) * hd]


def _in_proj_kv(x, norm_w, mod, w, heads):
    n = w.shape[1]
    tn = D_MODEL
    spb = IN_TM // SEQ
    last = N_PROMPT_TOK // IN_TM - 1
    cache = pl.BlockSpec((spb, 1, heads, SEQ, D_MODEL // heads), lambda i, j: (jnp.minimum(i, last), 0, 0, 0, 0))
    cache_shape = jax.ShapeDtypeStruct((BATCH, 1, heads, SEQ, D_MODEL // heads), F32)
    return pl.pallas_call(
        functools.partial(_in_proj_kv_kernel, heads=heads, tn=tn),
        grid=(N_TOK // IN_TM, n // tn),
        in_specs=[
            pl.BlockSpec((IN_TM, D_MODEL), lambda i, j: (i, 0)),
            pl.BlockSpec((1, D_MODEL), lambda i, j: (0, 0)),
            pl.BlockSpec((1, 3, 1, D_MODEL), lambda i, j: (_cond_of_tile(i, IN_TM), 0, 0, 0)),
            pl.BlockSpec((D_MODEL, tn), lambda i, j: (0, j)),
        ],
        out_specs=[pl.BlockSpec((IN_TM, tn), lambda i, j: (i, j)), cache, cache],
        out_shape=[jax.ShapeDtypeStruct((N_TOK, n), BF16), cache_shape, cache_shape],
        scratch_shapes=[pltpu.VMEM((IN_TM, D_MODEL), BF16)],
        compiler_params=_params("arbitrary", "arbitrary"),
        name="in_proj_kv",
    )(x, norm_w.reshape(1, D_MODEL), mod, w.astype(BF16))


OUT_TM = 512


def _out_proj_kernel(*refs, n_x, t0, final):
    x_refs = refs[:n_x]
    o_ref, g_ref, w_ref, mod_ref, fw_ref, y_ref, wb_ref = refs[n_x:]

    @pl.when(pl.program_id(0) == 0)
    def _():
        wb_ref[...] = w_ref[...].astype(BF16)

    a = (o_ref[...] * _silu(g_ref[...].astype(F32))).astype(BF16)
    xn = _read_x(x_refs, t0 + pl.program_id(0), OUT_TM) + mod_ref[0, 2] * _dot(a, wb_ref[...])
    if final:
        ms = jnp.mean(xn * xn, axis=-1, keepdims=True)
        xn = xn * lax.rsqrt(ms + EPS) * fw_ref[...]
    y_ref[...] = xn


def _out_proj(o, g_arr, g_blk, w, x, mod, final_w, final, rows=(0, N_TOK), g_row0=0):
    k = w.shape[0]
    t0 = rows[0] // OUT_TM
    g0 = g_row0 // OUT_TM
    x_specs, xs = _x_specs(x, OUT_TM, lambda i: t0 + i)
    return pl.pallas_call(
        functools.partial(_out_proj_kernel, n_x=len(xs), t0=t0, final=final),
        grid=((rows[1] - rows[0]) // OUT_TM,),
        in_specs=x_specs + [
            pl.BlockSpec((OUT_TM, k), lambda i: (t0 + i, 0)),
            pl.BlockSpec((OUT_TM, k), lambda i: (g0 + t0 + i, g_blk)),
            pl.BlockSpec((k, D_MODEL), lambda i: (0, 0)),
            pl.BlockSpec((1, 3, 1, D_MODEL), lambda i: (_cond_of_tile(t0 + i, OUT_TM), 0, 0, 0)),
            pl.BlockSpec((1, D_MODEL), lambda i: (0, 0)),
        ],
        out_specs=pl.BlockSpec((OUT_TM, D_MODEL), lambda i: (i, 0)),
        out_shape=jax.ShapeDtypeStruct((rows[1] - rows[0], D_MODEL), F32),
        scratch_shapes=[pltpu.VMEM((k, D_MODEL), BF16)],
        compiler_params=_params("arbitrary"),
        name="out_proj",
    )(*xs, o, g_arr, w, mod, final_w.reshape(1, D_MODEL))


def _rope_tables(d):
    q = d // 4
    t = jnp.arange(DEC_SEQ)
    row = (t // GRID_W).astype(F32)
    col = (t % GRID_W).astype(F32)
    inv = ROPE_BASE ** (-jnp.arange(0, 2 * q, 2, dtype=F32) / (2 * q))
    ar = row[:, None] * inv[None, :]
    ac = col[:, None] * inv[None, :]
    z = jnp.zeros_like(ar)
    cos = jnp.concatenate([jnp.cos(ar), jnp.cos(ar), jnp.cos(ac), jnp.cos(ac)], axis=-1)
    sin_lo = jnp.concatenate([-jnp.sin(ar), z, -jnp.sin(ac), z], axis=-1)
    sin_hi = jnp.concatenate([z, jnp.sin(ar), z, jnp.sin(ac)], axis=-1)
    return cos, sin_lo, sin_hi


def _rope(x, cos, sin_lo, sin_hi, q):
    w = x.shape[-1]
    x_next = pltpu.roll(x, w - q, axis=1)
    x_prev = pltpu.roll(x, q, axis=1)
    return x * cos + x_next * sin_lo + x_prev * sin_hi


RET_QB = 256


def _ret_kernel(lg_ref, q_ref, k_ref, v_ref, gn_ref, *rest, seq, latent):
    if latent:
        cos_ref, slo_ref, shi_ref, s0_ref, _prev_ref, o_ref = rest
    else:
        o_ref, st_ref, dec_ref = rest
    h = pl.program_id(1 if latent else 0)
    lgf = lg_ref[0, h]
    lgb = lg_ref[1, h]
    q = q_ref[...].astype(F32)
    k = k_ref[...].astype(F32)
    if latent:
        q = _rope(q, cos_ref[...], slo_ref[...], shi_ref[...], RET_DK // 4)
        k = _rope(k, cos_ref[...], slo_ref[...], shi_ref[...], RET_DK // 4)
    k = k * (RET_DK ** -0.5)
    kb = k.astype(BF16)
    vb = v_ref[...].astype(BF16)
    gn = gn_ref[...]

    def decay(qi):
        ii = lax.broadcasted_iota(jnp.int32, (RET_QB, seq), 0) + qi * RET_QB
        jj = lax.broadcasted_iota(jnp.int32, (RET_QB, seq), 1)
        gap = (ii - jj).astype(F32)
        return (jnp.where(gap >= 0, jnp.exp(lgf * jnp.maximum(gap, 0.0)), 0.0)
                + jnp.where(gap <= 0, jnp.exp(lgb * jnp.maximum(-gap, 0.0)), 0.0))

    if not latent:
        @pl.when(pl.program_id(1) == 0)
        def _():
            dec_ref[...] = decay(0)

    for qi in range(seq // RET_QB):
        qblk = q[qi * RET_QB:(qi + 1) * RET_QB]
        s = _dot_nt(qblk.astype(BF16), kb)
        dec = decay(qi) if latent else dec_ref[...]
        o = _dot((s * dec).astype(BF16), vb)
        if latent:
            pos = (lax.broadcasted_iota(jnp.int32, (RET_QB, 1), 0) + qi * RET_QB).astype(F32)
            qf = qblk * jnp.exp(lgf * (pos + 1.0))
            qr = qblk * jnp.exp(lgb * (seq - pos))
            o = o + _dot(qf.astype(BF16), s0_ref[0, 0, 0, 0].astype(BF16))
            o = o + _dot(qr.astype(BF16), s0_ref[0, 0, 1, 0].astype(BF16))
        oc = o - jnp.mean(o, axis=-1, keepdims=True)
        o = oc * lax.rsqrt(jnp.mean(oc * oc, axis=-1, keepdims=True) + GN_EPS) * gn
        o_ref[qi * RET_QB:(qi + 1) * RET_QB, :] = o
    if not latent:
        pos = lax.broadcasted_iota(jnp.int32, (seq, 1), 0).astype(F32)
        kf = k * jnp.exp(lgf * (seq - 1.0 - pos))
        kr = k * jnp.exp(lgb * pos)
        st_ref[0, 0, 0, 0] = _dot_tn(kf.astype(BF16), vb)
        st_ref[0, 0, 1, 0] = _dot_tn(kr.astype(BF16), vb)


def _retention(p, log_g, gn_w, state_ret, j):
    smem = pl.BlockSpec(memory_space=pltpu.SMEM)
    gn = gn_w.reshape(1, RET_V)
    kq = RET_QK // RET_DK
    o_p, st = pl.pallas_call(
        functools.partial(_ret_kernel, seq=SEQ, latent=False),
        grid=(RET_HEADS, BATCH),
        in_specs=[
            smem,
            pl.BlockSpec((SEQ, RET_DK), lambda h, b: (b, h)),
            pl.BlockSpec((SEQ, RET_DK), lambda h, b: (b, kq + h)),
            pl.BlockSpec((SEQ, RET_DV), lambda h, b: (b, kq + h)),
            pl.BlockSpec((1, RET_DV), lambda h, b: (0, h)),
        ],
        out_specs=[
            pl.BlockSpec((SEQ, RET_DV), lambda h, b: (b, h)),
            pl.BlockSpec((1, 1, 2, 1, RET_DK, RET_DV), lambda h, b: (b, 0, 0, h, 0, 0)),
        ],
        out_shape=[
            jax.ShapeDtypeStruct((N_TOK, RET_V), F32),
            jax.ShapeDtypeStruct((BATCH, 1, 2, RET_HEADS, RET_DK, RET_DV), F32),
        ],
        scratch_shapes=[pltpu.VMEM((RET_QB, SEQ), F32)],
        compiler_params=_params("arbitrary", "arbitrary"),
        name="retention_prompt",
    )(log_g, p, p, p, gn)
    cos, slo, shi = _rope_tables(RET_DK)
    rb = N_PROMPT_TOK // DEC_SEQ
    full = pl.BlockSpec((DEC_SEQ, RET_DK), lambda b, h: (0, 0))
    o = pl.pallas_call(
        functools.partial(_ret_kernel, seq=DEC_SEQ, latent=True),
        grid=(DEC_BATCH, RET_HEADS),
        in_specs=[
            smem,
            pl.BlockSpec((DEC_SEQ, RET_DK), lambda b, h: (rb + b, h)),
            pl.BlockSpec((DEC_SEQ, RET_DK), lambda b, h: (rb + b, kq + h)),
            pl.BlockSpec((DEC_SEQ, RET_DV), lambda b, h: (rb + b, kq + h)),
            pl.BlockSpec((1, RET_DV), lambda b, h: (0, h)),
            full, full, full,
            pl.BlockSpec((1, 1, 2, 1, RET_DK, RET_DV), lambda b, h: (b, j, 0, h, 0, 0)),
            pl.BlockSpec(memory_space=pl.ANY),
        ],
        out_specs=pl.BlockSpec((DEC_SEQ, RET_DV), lambda b, h: (rb + b, h)),
        out_shape=jax.ShapeDtypeStruct((N_TOK, RET_V), F32),
        input_output_aliases={9: 0},
        compiler_params=_params("arbitrary", "arbitrary"),
        name="retention_latent",
    )(log_g, p, p, p, gn, cos, slo, shi, state_ret, o_p)
    return o, st


def _layer_ret(x, p, mod, j):
    i = N_MIXERS * j + 0
    proj = _in_proj(x, p['norm_w'][i], mod, p['ret_w_in'][j], IN_TN, out_dtype=BF16)
    log_g = jax.nn.log_sigmoid(p['ret_decay'][j].astype(F32))
    o, st = _retention(proj, log_g, p['ret_gn'][j], p['state_ret'], j)
    x = _out_proj(o, proj, (2 * RET_QK + RET_V) // RET_V, p['ret_w_out'][j], x, mod, p['final_norm_w'], False)
    return x, st


RW_TM = 512
RW_HALO = 8
RW_C = 64
RW_LOCK = 4
RW_PAIRS = 2
RW_PAIRS_PROMPT = 2


def _rwkv_prep_kernel(x_ref, xp_ref, xn_ref, nw_ref, mod_ref, mu_ref, wa_ref, aa_ref, wb_ref, ab_ref,
                      w0_ref, a0_ref, xm_ref, lw_ref, a_ref):
    i = pl.program_id(0)
    nw = nw_ref[...]
    h = _norm_mod(x_ref[...], nw, mod_ref)
    h_before = _norm_mod(xp_ref[RW_HALO - 1:RW_HALO, :], nw, mod_ref)
    h_after = _norm_mod(xn_ref[0:1, :], nw, mod_ref)
    seq = jnp.where(i < N_PROMPT_TOK // RW_TM, SEQ, DEC_SEQ)
    row = lax.broadcasted_iota(jnp.int32, (RW_TM, 1), 0)
    t = (row + i * RW_TM) & (seq - 1)
    prev = jnp.where(row == 0, h_before, pltpu.roll(h, 1, axis=0))
    nxt = jnp.where(row == RW_TM - 1, h_after, pltpu.roll(h, RW_TM - 1, axis=0))
    prev = jnp.where(t == 0, 0.0, prev)
    nxt = jnp.where(t == seq - 1, 0.0, nxt)
    xx = 0.5 * (prev + nxt) - h
    for n, m in enumerate((0, 2, 3, 5)):
        xm_ref[n] = (h + xx * mu_ref[m:m + 1, :]).astype(BF16)
    xw = (h + xx * mu_ref[1:2, :]).astype(BF16)
    xa = (h + xx * mu_ref[4:5, :]).astype(BF16)
    lw = jnp.tanh(_dot(xw, wa_ref[...])).astype(BF16)
    la = _dot(xa, aa_ref[...]).astype(BF16)
    for dr in range(2):
        wl = w0_ref[dr:dr + 1, :] + _dot(lw, wb_ref[dr])
        lw_ref[dr] = -math.exp(-0.5) * _sigmoid(wl)
        a_ref[dr] = _sigmoid(a0_ref[dr:dr + 1, :] + _dot(la, ab_ref[dr]))


def _rwkv_prep(x, norm_w, mod, mu, wa2, aa2, wb_pad, ab_pad, w0, a0):
    nt = N_TOK // RW_TM
    hb = RW_TM // RW_HALO
    last = N_TOK // RW_HALO - 1
    full2 = lambda shape: pl.BlockSpec(shape, lambda i: (0, 0))
    full3 = lambda shape: pl.BlockSpec(shape, lambda i: (0, 0, 0))
    return pl.pallas_call(
        _rwkv_prep_kernel,
        grid=(nt,),
        in_specs=[
            pl.BlockSpec((RW_TM, D_MODEL), lambda i: (i, 0)),
            pl.BlockSpec((RW_HALO, D_MODEL), lambda i: (jnp.maximum(i * hb - 1, 0), 0)),
            pl.BlockSpec((RW_HALO, D_MODEL), lambda i: (jnp.minimum((i + 1) * hb, last), 0)),
            full2((1, D_MODEL)),
            pl.BlockSpec((1, 3, 1, D_MODEL), lambda i: (_cond_of_tile(i, RW_TM), 0, 0, 0)),
            full2((6, D_MODEL)),
            full2((D_MODEL, 2 * RWKV_RANK)),
            full2((D_MODEL, 2 * RWKV_RANK)),
            full3((2, 2 * RWKV_RANK, D_MODEL)),
            full3((2, 2 * RWKV_RANK, D_MODEL)),
            full2((2, D_MODEL)),
            full2((2, D_MODEL)),
        ],
        out_specs=[
            pl.BlockSpec((4, RW_TM, D_MODEL), lambda i: (0, i, 0)),
            pl.BlockSpec((2, RW_TM, D_MODEL), lambda i: (0, i, 0)),
            pl.BlockSpec((2, RW_TM, D_MODEL), lambda i: (0, i, 0)),
        ],
        out_shape=[
            jax.ShapeDtypeStruct((4, N_TOK, D_MODEL), BF16),
            jax.ShapeDtypeStruct((2, N_TOK, D_MODEL), F32),
            jax.ShapeDtypeStruct((2, N_TOK, D_MODEL), F32),
        ],
        compiler_params=_params("arbitrary"),
        name="rwkv_prep",
    )(x, x, x, norm_w.reshape(1, D_MODEL), mod, mu, wa2, aa2, wb_pad, ab_pad, w0, a0)


def _bmm_kernel(a_ref, w_ref, o_ref):
    o_ref[0] = _dot(a_ref[0], w_ref[...])


def _rwkv_rkvg(xm, w):
    tm = 1024
    return pl.pallas_call(
        _bmm_kernel,
        grid=(4, N_TOK // tm),
        in_specs=[
            pl.BlockSpec((1, tm, D_MODEL), lambda n, i: (n, i, 0)),
            pl.BlockSpec((D_MODEL, D_MODEL), lambda n, i: (0, n)),
        ],
        out_specs=pl.BlockSpec((1, tm, D_MODEL), lambda n, i: (n, i, 0)),
        out_shape=jax.ShapeDtypeStruct((4, N_TOK, D_MODEL), F32),
        compiler_params=_params("arbitrary", "arbitrary"),
        name="rwkv_rkvg",
    )(xm, w)


def _head_sum(x, first):
    s0 = jnp.sum(jnp.where(first, x, 0.0), axis=-1, keepdims=True)
    s1 = jnp.sum(jnp.where(first, 0.0, x), axis=-1, keepdims=True)
    return jnp.where(first, s0, s1)


def _stack_heads(x, first):
    return jnp.concatenate([jnp.where(first, x, 0.0), jnp.where(first, 0.0, x)], axis=0)


def _cumsum_rows(tri, x):
    hi = x.astype(BF16)
    r1 = x - hi.astype(F32)
    mid = r1.astype(BF16)
    lo = (r1 - mid.astype(F32)).astype(BF16)
    return _dot(tri, hi) + _dot(tri, mid) + _dot(tri, lo)


def _rwkv_chunk_kernel(*refs, seq, zero_init):
    if zero_init:
        (rkv_ref, lw_ref, a_ref, kkp_ref, kap_ref, rkp_ref, gn_ref, o_ref, st_ref,
         kk_scr, y_scr, tar_scr, lrb_scr, b2_scr, w2_scr, yl_scr, kv_scr, pc_scr) = refs
    else:
        (rkv_ref, lw_ref, a_ref, kkp_ref, kap_ref, rkp_ref, gn_ref, s0_ref, _prev_ref, o_ref,
         kk_scr, y_scr, tar_scr, lrb_scr, b2_scr, w2_scr, yl_scr, kv_scr, pc_scr) = refs
    c_len = RW_C
    n_ch = seq // c_len
    rows2 = 2 * c_len
    first = _first_half_lanes()
    kap = kap_ref[...]

    kk = rkv_ref[1] * kkp_ref[...]
    kk_scr[...] = kk * lax.rsqrt(jnp.maximum(_head_sum(kk * kk, first), 1e-12))

    rr = lax.broadcasted_iota(jnp.int32, (rows2, rows2), 0)
    cc = lax.broadcasted_iota(jnp.int32, (rows2, rows2), 1)
    eye = (rr == cc).astype(F32)
    tr = lax.broadcasted_iota(jnp.int32, (c_len, c_len), 0)
    tc = lax.broadcasted_iota(jnp.int32, (c_len, c_len), 1)

    def same(shift):
        return (rr >> shift) == (cc >> shift)

    head = same(6)
    strict = (head & (cc < rr), head & (cc > rr))
    incl = (head & (cc <= rr), head & (cc >= rr))
    tri = ((tc <= tr).astype(BF16), (tc >= tr).astype(BF16))
    last = (c_len - 1, 0)

    def phase1(chains):
        dirs = [dr for dr, _ in chains]
        rows = [pl.ds(pl.multiple_of(c * c_len, c_len), c_len) for _, c in chains]
        lw = [lw_ref[dr, rw, :] for dr, rw in zip(dirs, rows)]
        cum = [_cumsum_rows(tri[dr], x) for dr, x in zip(dirs, lw)]
        a2, r2, b2, k2, v2, pc = [], [], [], [], [], []
        for dr, rw, lw_c, cum_c in zip(dirs, rows, lw, cum):
            a = a_ref[dr, rw, :]
            k = rkv_ref[1, rw, :]
            kk_c = kk_scr[rw, :]
            e_inc = jnp.exp(cum_c)
            e_inv = jnp.exp(-cum_c)
            a2.append(_stack_heads(-kk_c * jnp.exp(cum_c - lw_c), first).astype(BF16))
            r2.append(_stack_heads(rkv_ref[0, rw, :] * e_inc, first).astype(BF16))
            b2.append(_stack_heads(kk_c * a * e_inv, first).astype(BF16))
            k2.append(_stack_heads(k * (1.0 + (a - 1.0) * kap) * e_inv, first).astype(BF16))
            v2.append(_stack_heads(rkv_ref[2, rw, :], first).astype(BF16))
            pc.append(e_inc[last[dr]:last[dr] + 1, :])
        g = [_dot_nt(jnp.concatenate([x, y], axis=0), jnp.concatenate([z, w], axis=0))
             for x, y, z, w in zip(a2, r2, b2, k2)]
        l_ab = [jnp.where(strict[dr], x[:rows2, :rows2], 0.0) for dr, x in zip(dirs, g)]
        t = [eye + jnp.where(same(1), x, 0.0) for x in l_ab]
        side = {}
        for shift in range(1, 6):
            sib = same(shift + 1) & ~same(shift)
            tb = [x.astype(BF16) for x in t]
            mid = [_dot(jnp.where(sib, x, 0.0).astype(BF16), y) for x, y in zip(l_ab, tb)]
            if shift == 1:
                side['lv'] = [_dot(jnp.where(strict[dr], x[:rows2, rows2:], 0.0).astype(BF16), y)
                              for dr, x, y in zip(dirs, g, v2)]
            elif shift == 2:
                side['yl'] = [_dot(jnp.where(incl[dr], x[rows2:, rows2:], 0.0).astype(BF16), y)
                              for dr, x, y in zip(dirs, g, v2)]
            elif shift == 3:
                side['kv'] = [_dot_tn(x, y) for x, y in zip(v2, k2)]
            t = [x + _dot(y, z.astype(BF16)) for x, y, z in zip(t, tb, mid)]
        tb = [x.astype(BF16) for x in t]
        ta = [_dot(x, y) for x, y in zip(tb, a2)]
        w2 = [_dot(x, y.astype(BF16)) for x, y in zip(tb, side['lv'])]
        for i, (dr, c) in enumerate(chains):
            n = dr * n_ch + c
            tar_scr[n, :rows2, :] = ta[i].astype(BF16)
            tar_scr[n, rows2:, :] = r2[i]
            w2_scr[n] = w2[i]
            yl_scr[n] = side['yl'][i]
            kv_scr[n] = side['kv'][i]
            lrb_scr[n] = jnp.where(incl[dr], g[i][rows2:, :rows2], 0.0).astype(BF16)
            b2_scr[n] = b2[i]
            pc_scr[n] = pc[i]

    def body1(grp, carry):
        phase1([(dr, grp * RW_LOCK + j) for j in range(RW_LOCK) for dr in range(2)])
        return carry

    if n_ch == RW_LOCK:
        body1(0, 0)
    else:
        lax.fori_loop(0, n_ch // RW_LOCK, body1, 0)

    def body2(i, carry):
        cs = (i, n_ch - 1 - i)
        ns = [dr * n_ch + c for dr, c in enumerate(cs)]
        x = [_dot_nt(tar_scr[n], s2.astype(BF16)) for n, s2 in zip(ns, carry)]
        u2 = [(xx[:rows2] + w2_scr[n]).astype(BF16) for n, xx in zip(ns, x)]
        upd = [_dot_tn(u, b2_scr[n]) for n, u in zip(ns, u2)]
        yb = [_dot(lrb_scr[n], u) for n, u in zip(ns, u2)]
        out = []
        for dr, (c, n) in enumerate(zip(cs, ns)):
            y2 = x[dr][rows2:] + yb[dr] + yl_scr[n]
            y_scr[dr, pl.ds(pl.multiple_of(c * c_len, c_len), c_len), :] = y2[:c_len] + y2[c_len:]
            out.append((carry[dr] + upd[dr] + kv_scr[n]) * pc_scr[n])
        return tuple(out)

    if zero_init:
        init = (jnp.zeros((rows2, LANES), F32),) * 2
    else:
        init = (s0_ref[0, 0, 0], s0_ref[0, 1, 0])
    s_f, s_b = lax.fori_loop(0, n_ch, body2, init)

    y = y_scr[0] + y_scr[1]
    yc = y - _head_sum(y, first) * (1.0 / RWKV_HD)
    o = yc * lax.rsqrt(_head_sum(yc * yc, first) * (1.0 / RWKV_HD) + GN_EPS) * gn_ref[...]
    r = rkv_ref[0]
    k = rkv_ref[1]
    v = rkv_ref[2]
    for dr in range(2):
        kd = k * (1.0 + (a_ref[dr] - 1.0) * kap)
        o = o + _head_sum(r * kd * rkp_ref[...], first) * v
    o_ref[...] = o

    if zero_init:
        for dr, s2 in enumerate((s_f, s_b)):
            st_ref[0, 0, dr, 0] = s2[:RWKV_HD, :RWKV_HD]
            st_ref[0, 0, dr, 1] = s2[RWKV_HD:, RWKV_HD:]


def _rwkv_chunked(rkvg, lw, a, kkp, kap, rkp, gn, s0_pairs):
    npair = RWKV_HEADS // 2
    par = lambda *_: pl.BlockSpec((1, LANES), lambda s, p: (0, p))

    def scratch(seq):
        n = 2 * (seq // RW_C)
        r2 = 2 * RW_C
        return [
            pltpu.VMEM((seq, LANES), F32), pltpu.VMEM((2, seq, LANES), F32),
            pltpu.VMEM((n, 2 * r2, LANES), BF16), pltpu.VMEM((n, r2, r2), BF16), pltpu.VMEM((n, r2, LANES), BF16),
            pltpu.VMEM((n, r2, LANES), F32), pltpu.VMEM((n, r2, LANES), F32), pltpu.VMEM((n, r2, LANES), F32),
            pltpu.VMEM((n, 1, LANES), F32),
        ]

    def seq_specs(seq, rb):
        return [
            pl.BlockSpec((3, seq, LANES), lambda s, p: (0, rb + s, p)),
            pl.BlockSpec((2, seq, LANES), lambda s, p: (0, rb + s, p)),
            pl.BlockSpec((2, seq, LANES), lambda s, p: (0, rb + s, p)),
            par(), par(), par(), par(),
        ]

    pars = [u.reshape(1, D_MODEL) for u in (kkp, kap, rkp, gn)]
    o_p, st = pl.pallas_call(
        functools.partial(_rwkv_chunk_kernel, seq=SEQ, zero_init=True),
        grid=(BATCH, npair),
        in_specs=seq_specs(SEQ, 0),
        out_specs=[
            pl.BlockSpec((SEQ, LANES), lambda s, p: (s, p)),
            pl.BlockSpec((1, 1, 2, 2, RWKV_HD, RWKV_HD), lambda s, p: (s, 0, 0, p, 0, 0)),
        ],
        out_shape=[
            jax.ShapeDtypeStruct((N_TOK, D_MODEL), F32),
            jax.ShapeDtypeStruct((BATCH, 1, 2, RWKV_HEADS, RWKV_HD, RWKV_HD), F32),
        ],
        scratch_shapes=scratch(SEQ),
        compiler_params=_params("arbitrary", "arbitrary"),
        name="rwkv_prompt",
    )(rkvg, lw, a, *pars)
    rb = N_PROMPT_TOK // DEC_SEQ
    o = pl.pallas_call(
        functools.partial(_rwkv_chunk_kernel, seq=DEC_SEQ, zero_init=False),
        grid=(DEC_BATCH, npair),
        in_specs=seq_specs(DEC_SEQ, rb) + [
            pl.BlockSpec((1, 2, 1, LANES, LANES), lambda s, p: (s, 0, p, 0, 0)),
            pl.BlockSpec(memory_space=pl.ANY),
        ],
        out_specs=pl.BlockSpec((DEC_SEQ, LANES), lambda s, p: (rb + s, p)),
        out_shape=jax.ShapeDtypeStruct((N_TOK, D_MODEL), F32),
        input_output_aliases={8: 0},
        scratch_shapes=scratch(DEC_SEQ),
        compiler_params=_params("arbitrary", "arbitrary"),
        name="rwkv_latent",
    )(rkvg, lw, a, *pars, s0_pairs, o_p)
    return o, st


def _split3(x):
    hi = x.astype(BF16)
    r1 = x - hi.astype(F32)
    mid = r1.astype(BF16)
    return hi, mid, (r1 - mid.astype(F32)).astype(BF16)


def _rwkv_kernel(*refs, seq, zero_init, pg, np2):
    n_in = 7 if zero_init else 9
    rkv_ref, lw_ref, a_ref, kkp_ref, kap_ref, rkp_ref, gn_ref = refs[:7]
    if zero_init:
        o_ref, st_ref = refs[n_in:n_in + 2]
        scr = refs[n_in + 2:]
    else:
        s0_ref = refs[7]
        o_ref = refs[n_in]
        scr = refs[n_in + 1:]
    kk_scr, cum_scr, bon_scr, y_scr, s_scr, tar_scr, lrb_scr, b2_scr, w2_scr, yl_scr, kv_scr, pc_scr = scr
    c_len = RW_C
    n_ch = seq // c_len
    rows2 = 2 * c_len
    grp = pl.program_id(1)
    defer = np2 > pg
    base = grp * pg if defer else 0
    first = _first_half_lanes()

    rr = lax.broadcasted_iota(jnp.int32, (rows2, rows2), 0)
    cc = lax.broadcasted_iota(jnp.int32, (rows2, rows2), 1)
    eye = (rr == cc).astype(F32)

    def same(shift):
        return (rr >> shift) == (cc >> shift)

    head = same(6)
    strict = (head & (cc < rr), head & (cc > rr))
    incl = (head & (cc <= rr), head & (cc >= rr))
    last = (c_len - 1, 0)
    head_ones = head.astype(BF16)

    cs_rows = min(seq, 256)
    tr = lax.broadcasted_iota(jnp.int32, (cs_rows, cs_rows), 0)
    tc = lax.broadcasted_iota(jnp.int32, (cs_rows, cs_rows), 1)
    chunk = (tr >> 6) == (tc >> 6)
    tri = ((chunk & (tc <= tr)).astype(BF16), (chunk & (tc >= tr)).astype(BF16))
    for p in range(pg):
        ln = slice(p * LANES, (p + 1) * LANES)
        kap = kap_ref[:, ln]
        r = rkv_ref[0, :, ln]
        k = rkv_ref[1, :, ln]
        v = rkv_ref[2, :, ln]
        kk = k * kkp_ref[:, ln]
        kk_scr[p] = kk * lax.rsqrt(jnp.maximum(_head_sum(kk * kk, first), 1e-12))
        bonus = None
        for dr in range(2):
            kd = k * (1.0 + (a_ref[dr, :, ln] - 1.0) * kap)
            term = _head_sum(r * kd * rkp_ref[:, ln], first) * v
            bonus = term if bonus is None else bonus + term
        bon_scr[base + p] = bonus
    for p in range(0, pg, 2):
        for dr in range(2):
            for r0 in range(0, seq, cs_rows):
                parts = _split3(lw_ref[dr, r0:r0 + cs_rows, p * LANES:(p + 2) * LANES])
                cum = _dot(tri[dr], parts[0]) + _dot(tri[dr], parts[1]) + _dot(tri[dr], parts[2])
                cum_scr[p, dr, r0:r0 + cs_rows, :] = cum[:, :LANES]
                cum_scr[p + 1, dr, r0:r0 + cs_rows, :] = cum[:, LANES:]

    def phase1(chains):
        dirs = [dr for _, dr, _ in chains]
        a2, r2, b2, k2, v2, pc = [], [], [], [], [], []
        for p, dr, c in chains:
            ln = slice(p * LANES, (p + 1) * LANES)
            rw = pl.ds(pl.multiple_of(c * c_len, c_len), c_len)
            a = a_ref[dr, rw, ln]
            k = rkv_ref[1, rw, ln]
            kk_c = kk_scr[p, rw, :]
            cum_c = cum_scr[p, dr, rw, :]
            e_inc = jnp.exp(cum_c)
            e_inv = jnp.exp(-cum_c)
            a2.append(_stack_heads(-kk_c * jnp.exp(cum_c - lw_ref[dr, rw, ln]), first).astype(BF16))
            r2.append(_stack_heads(rkv_ref[0, rw, ln] * e_inc, first).astype(BF16))
            b2.append(_stack_heads(kk_c * a * e_inv, first).astype(BF16))
            k2.append(_stack_heads(k * (1.0 + (a - 1.0) * kap_ref[:, ln]) * e_inv, first).astype(BF16))
            v2.append(_stack_heads(rkv_ref[2, rw, ln], first).astype(BF16))
            pc.append(e_inc[last[dr]:last[dr] + 1, :])
        g = [_dot_nt(jnp.concatenate([x, y], axis=0), jnp.concatenate([z, w], axis=0))
             for x, y, z, w in zip(a2, r2, b2, k2)]
        l_ab = [jnp.where(strict[dr], x[:rows2, :rows2], 0.0) for dr, x in zip(dirs, g)]
        t = [eye + jnp.where(same(1), x, 0.0) for x in l_ab]
        side = {}
        for shift in range(1, 6):
            sib = same(shift + 1) & ~same(shift)
            tb = [x.astype(BF16) for x in t]
            mid = [_dot(jnp.where(sib, x, 0.0).astype(BF16), y) for x, y in zip(l_ab, tb)]
            if shift == 1:
                side['lv'] = [_dot(jnp.where(strict[dr], x[:rows2, rows2:], 0.0).astype(BF16), y)
                              for dr, x, y in zip(dirs, g, v2)]
            elif shift == 2:
                side['yl'] = [_dot(jnp.where(incl[dr], x[rows2:, rows2:], 0.0).astype(BF16), y)
                              for dr, x, y in zip(dirs, g, v2)]
            elif shift == 3:
                side['kv'] = [_dot_tn(x, y) for x, y in zip(v2, k2)]
            t = [x + _dot(y, z.astype(BF16)) for x, y, z in zip(t, tb, mid)]
        tb = [x.astype(BF16) for x in t]
        ta = [_dot(x, y) for x, y in zip(tb, a2)]
        w2 = [_dot(x, y.astype(BF16)) for x, y in zip(tb, side['lv'])]
        for i, (p, dr, c) in enumerate(chains):
            n = ((base + p) * 2 + dr) * n_ch + c
            tar_scr[n, :rows2, :] = ta[i].astype(BF16)
            tar_scr[n, rows2:, :] = r2[i]
            w2_scr[n] = w2[i]
            yl_scr[n] = side['yl'][i]
            kv_scr[n] = side['kv'][i]
            lrb_scr[n] = jnp.where(incl[dr], g[i][rows2:, :rows2], 0.0).astype(BF16)
            b2_scr[n] = b2[i]
            pc_scr[n] = pc[i]

    def body1(cg, carry):
        phase1([(p, dr, cg * RW_LOCK + j) for p in range(pg) for j in range(RW_LOCK) for dr in range(2)])
        return carry

    if n_ch == RW_LOCK:
        body1(0, 0)
    else:
        lax.fori_loop(0, n_ch // RW_LOCK, body1, 0)

    def finish():
        for p in range(np2):
            for dr in range(2):
                if zero_init:
                    s_scr[2 * p + dr] = jnp.zeros((rows2, LANES), F32)
                else:
                    s_scr[2 * p + dr] = s0_ref[0, dr, p]

        def body2(i, carry):
            cs = (i, n_ch - 1 - i)
            ids = [(p, dr) for p in range(np2) for dr in range(2)]
            ns = [(p * 2 + dr) * n_ch + cs[dr] for p, dr in ids]
            x = [_dot_nt(tar_scr[n], s_scr[2 * p + dr].astype(BF16)) for n, (p, dr) in zip(ns, ids)]
            u2 = [(xx[:rows2] + w2_scr[n]).astype(BF16) for n, xx in zip(ns, x)]
            upd = [_dot_tn(u, b2_scr[n]) for n, u in zip(ns, u2)]
            yb = [_dot(lrb_scr[n], u) for n, u in zip(ns, u2)]
            for j, (n, (p, dr)) in enumerate(zip(ns, ids)):
                y2 = x[j][rows2:] + yb[j] + yl_scr[n]
                y_scr[p, dr, pl.ds(pl.multiple_of(cs[dr] * c_len, c_len), c_len), :] = y2[:c_len] + y2[c_len:]
                s_scr[2 * p + dr] = (s_scr[2 * p + dr] + upd[j] + kv_scr[n]) * pc_scr[n]
            return carry

        lax.fori_loop(0, n_ch, body2, 0)

        def head_mean(xs):
            parts = [_split3(x) for x in xs]
            return [(_dot(a, head_ones) + _dot(b, head_ones) + _dot(c, head_ones)) * (1.0 / RWKV_HD)
                    for a, b, c in parts]

        ys = [y_scr[p, 0] + y_scr[p, 1] for p in range(np2)]
        yc = [y - m for y, m in zip(ys, head_mean(ys))]
        var = head_mean([x * x for x in yc])
        for p in range(np2):
            ln = slice(p * LANES, (p + 1) * LANES)
            o_ref[:, ln] = yc[p] * lax.rsqrt(var[p] + GN_EPS) * gn_ref[:, ln] + bon_scr[p]
            if zero_init:
                for dr in range(2):
                    s2 = s_scr[2 * p + dr]
                    st_ref[0, 0, dr, 2 * p] = s2[:RWKV_HD, :RWKV_HD]
                    st_ref[0, 0, dr, 2 * p + 1] = s2[RWKV_HD:, RWKV_HD:]

    if defer:
        pl.when(grp == pl.num_programs(1) - 1)(finish)
    else:
        finish()


def _rwkv_mixer(rkvg, lw, a, kkp, kap, rkp, gn, s0_pairs):
    npair = RWKV_HEADS // 2
    r2 = 2 * RW_C

    def scratch(seq, pg, np2):
        n = 2 * np2 * (seq // RW_C)
        return [
            pltpu.VMEM((pg, seq, LANES), F32), pltpu.VMEM((pg, 2, seq, LANES), F32),
            pltpu.VMEM((np2, seq, LANES), F32), pltpu.VMEM((np2, 2, seq, LANES), F32),
            pltpu.VMEM((2 * np2, r2, LANES), F32),
            pltpu.VMEM((n, 2 * r2, LANES), BF16), pltpu.VMEM((n, r2, r2), BF16), pltpu.VMEM((n, r2, LANES), BF16),
            pltpu.VMEM((n, r2, LANES), F32), pltpu.VMEM((n, r2, LANES), F32), pltpu.VMEM((n, r2, LANES), F32),
            pltpu.VMEM((n, 1, LANES), F32),
        ]

    def seq_specs(seq, rb, pg, gn_spec):
        wl = pg * LANES
        par =pl.BlockSpec((1, wl), lambda s, g: (0, g))
        return [
            pl.BlockSpec((3, seq, wl), lambda s, g: (0, rb + s, g)),
            pl.BlockSpec((2, seq, wl), lambda s, g: (0, rb + s, g)),
            pl.BlockSpec((2, seq, wl), lambda s, g: (0, rb + s, g)),
            par, par, par, gn_spec,
        ]

    pars = [u.reshape(1, D_MODEL) for u in (kkp, kap, rkp, gn)]
    pg = RW_PAIRS_PROMPT
    o_p, st = pl.pallas_call(
        functools.partial(_rwkv_kernel, seq=SEQ, zero_init=True, pg=pg, np2=npair),
        grid=(BATCH, npair // pg),
        in_specs=seq_specs(SEQ, 0, pg, pl.BlockSpec((1, D_MODEL), lambda s, g: (0, 0))),
        out_specs=[
            pl.BlockSpec((SEQ, D_MODEL), lambda s, g: (s, 0)),
            pl.BlockSpec((1, 1, 2, RWKV_HEADS, RWKV_HD, RWKV_HD), lambda s, g: (s, 0, 0, 0, 0, 0)),
        ],
        out_shape=[
            jax.ShapeDtypeStruct((N_TOK, D_MODEL), F32),
            jax.ShapeDtypeStruct((BATCH, 1, 2, RWKV_HEADS, RWKV_HD, RWKV_HD), F32),
        ],
        scratch_shapes=scratch(SEQ, pg, npair),
        compiler_params=_params("arbitrary", "arbitrary"),
        name="rwkv_prompt",
    )(rkvg, lw, a, *pars)
    rb = N_PROMPT_TOK // DEC_SEQ
    pg = RW_PAIRS
    wl = pg * LANES
    o = pl.pallas_call(
        functools.partial(_rwkv_kernel, seq=DEC_SEQ, zero_init=False, pg=pg, np2=pg),
        grid=(DEC_BATCH, npair // pg),
        in_specs=seq_specs(DEC_SEQ, rb, pg, pl.BlockSpec((1, wl), lambda s, g: (0, g))) + [
            pl.BlockSpec((1, 2, pg, LANES, LANES), lambda s, g: (s, 0, g, 0, 0)),
            pl.BlockSpec(memory_space=pl.ANY),
        ],
        out_specs=pl.BlockSpec((DEC_SEQ, wl), lambda s, g: (rb + s, g)),
        out_shape=jax.ShapeDtypeStruct((N_TOK, D_MODEL), F32),
        input_output_aliases={8: 0},
        scratch_shapes=scratch(DEC_SEQ, pg, pg),
        compiler_params=_params("arbitrary", "arbitrary"),
        name="rwkv_latent",
    )(rkvg, lw, a, *pars, s0_pairs, o_p)
    return o, st


def _state_pairs(s0):
    s = s0.reshape(DEC_BATCH, 2, RWKV_HEADS // 2, 2, RWKV_HD, RWKV_HD)
    z = jnp.zeros_like(s[:, :, :, 0])
    top = jnp.concatenate([s[:, :, :, 0], z], axis=-1)
    bot = jnp.concatenate([z, s[:, :, :, 1]], axis=-1)
    return jnp.concatenate([top, bot], axis=-2)


def _layer_rwkv(x, p, mod, j):
    i = N_MIXERS * j + 1
    wa, wb, aa, ab = p['rwkv_wA'][j], p['rwkv_wB'][j], p['rwkv_aA'][j], p['rwkv_aB'][j]
    z = jnp.zeros_like(wb[0])
    wa2 = jnp.concatenate([wa[0], wa[1]], axis=1).astype(BF16)
    aa2 = jnp.concatenate([aa[0], aa[1]], axis=1).astype(BF16)
    wb_pad = jnp.stack([jnp.concatenate([wb[0], z]), jnp.concatenate([z, wb[1]])]).astype(BF16)
    ab_pad = jnp.stack([jnp.concatenate([ab[0], z]), jnp.concatenate([z, ab[1]])]).astype(BF16)
    xm, lw, a = _rwkv_prep(x, p['norm_w'][i], mod, p['rwkv_mu'][j], wa2, aa2, wb_pad, ab_pad,
                           p['rwkv_w0'][j], p['rwkv_a0'][j])
    rkvg = _rwkv_rkvg(xm, p['rwkv_w_in'][j].astype(BF16))
    o, st = _rwkv_mixer(rkvg, lw, a, p['rwkv_kk'][j], p['rwkv_ka'][j], p['rwkv_rk'][j], p['rwkv_gn'][j],
                          _state_pairs(p['state_rwkv'][:, j]))
    x = _out_proj(o, rkvg.reshape(4 * N_TOK, D_MODEL), 0, p['rwkv_w_out'][j], x, mod, p['final_norm_w'], False,
                  g_row0=3 * N_TOK)
    return x, st


DIFF_W = 2 * DIFF_HD
ATT_QB = 256
DIFF_GROUP = 4


def _first_half_lanes():
    return lax.broadcasted_iota(jnp.int32, (1, LANES), 1) < LANES // 2


def _diff_lambda(lam_ref, lam_init):
    lp = lam_ref[...]
    return (jnp.exp(jnp.sum(lp[0:1] * lp[1:2], keepdims=True))
            - jnp.exp(jnp.sum(lp[2:3] * lp[3:4], keepdims=True)) + lam_init)


def _diff_heads(items, lam, lam_init):
    first = _first_half_lanes()
    scale = DIFF_HD ** -0.5
    sub = [(q, keys, comp) for q, keys, _ in items for comp in range(2)]
    qm = [jnp.where(first if comp == 0 else ~first, q, 0.0).astype(BF16) for q, _, comp in sub]
    s = [[_dot_nt(x, kb) * scale for kb, _ in keys] for x, (_, keys, _) in zip(qm, sub)]
    m = [functools.reduce(jnp.maximum, [jnp.max(u, axis=-1, keepdims=True) for u in ss]) for ss in s]
    e = [[jnp.exp(u - mm) for u in ss] for ss, mm in zip(s, m)]
    inv = [1.0 / functools.reduce(lambda x, y: x + y, [jnp.sum(u, axis=-1, keepdims=True) for u in ee]) for ee in e]
    outs = []
    for i, (_, keys, gn) in enumerate(items):
        o = None
        for n, (_, vb) in enumerate(keys):
            p = e[2 * i][n] * inv[2 * i] - lam * (e[2 * i + 1][n] * inv[2 * i + 1])
            part = _dot(p.astype(BF16), vb)
            o = part if o is None else o + part
        outs.append(o)
    return [o * lax.rsqrt(jnp.mean(o * o, axis=-1, keepdims=True) + EPS) * gn * (1.0 - lam_init)
            for o, (_, _, gn) in zip(outs, items)]


def _diff_prompt_kernel(lam_ref, q_ref, k_ref, v_ref, gn_ref, o_ref, *, lam_init):
    lam = _diff_lambda(lam_ref, lam_init)
    for h0 in range(0, DIFF_HEADS, DIFF_GROUP):
        items = []
        for h in range(h0, h0 + DIFF_GROUP):
            sl = slice(h * DIFF_W, (h + 1) * DIFF_W)
            items.append((q_ref[:, sl], [(k_ref[:, sl].astype(BF16), v_ref[:, sl].astype(BF16))], gn_ref[:, sl]))
        for h, o in zip(range(h0, h0 + DIFF_GROUP), _diff_heads(items, lam, lam_init)):
            o_ref[:, h * DIFF_W:(h + 1) * DIFF_W] = o


def _diff_latent_kernel(lam_ref, q_ref, k_ref, v_ref, ck_ref, cv_ref, cos_ref, slo_ref, shi_ref, gn_ref,
                        _prev_ref, o_ref, *, lam_init):
    lam = _diff_lambda(lam_ref, lam_init)
    tabs = (cos_ref[...], slo_ref[...], shi_ref[...])
    q = _rope(q_ref[...].astype(F32), *tabs, DIFF_HD // 4)
    k = _rope(k_ref[...].astype(F32), *tabs, DIFF_HD // 4)
    keys = [(k.astype(BF16), v_ref[...].astype(BF16)),
            (ck_ref[0, 0, 0].astype(BF16), cv_ref[0, 0, 0].astype(BF16))]
    gn = gn_ref[...]
    n_blk = DEC_SEQ // ATT_QB
    items = [(q[qi * ATT_QB:(qi + 1) * ATT_QB], keys, gn) for qi in range(n_blk)]
    for qi, o in enumerate(_diff_heads(items, lam, lam_init)):
        o_ref[qi * ATT_QB:(qi + 1) * ATT_QB, :] = o


def _diff_attention(proj, lam_p, gn_w, cache_k, cache_v, j, lam_init):
    gn = gn_w.reshape(1, D_MODEL)
    lam_spec = pl.BlockSpec((4, DIFF_HD), lambda *_: (0, 0))
    o_p = pl.pallas_call(
        functools.partial(_diff_prompt_kernel, lam_init=lam_init),
        grid=(BATCH,),
        in_specs=[
            lam_spec,
            pl.BlockSpec((SEQ, D_MODEL), lambda b: (b, 0)),
            pl.BlockSpec((SEQ, D_MODEL), lambda b: (b, 1)),
            pl.BlockSpec((SEQ, D_MODEL), lambda b: (b, 2)),
            pl.BlockSpec((1, D_MODEL), lambda b: (0, 0)),
        ],
        out_specs=pl.BlockSpec((SEQ, D_MODEL), lambda b: (b, 0)),
        out_shape=jax.ShapeDtypeStruct((N_TOK, D_MODEL), F32),
        compiler_params=_params("arbitrary"),
        name="diff_prompt",
    )(lam_p, proj, proj, proj, gn)
    cos, slo, shi = (jnp.concatenate([u, u], axis=-1) for u in _rope_tables(DIFF_HD))
    rb = N_PROMPT_TOK // DEC_SEQ
    nh = DIFF_HEADS
    tab = pl.BlockSpec((DEC_SEQ, DIFF_W), lambda b, h: (0, 0))
    cache = pl.BlockSpec((1, 1, 1, PAST_LEN, DIFF_W), lambda b, h: (b, j, h, 0, 0))
    o = pl.pallas_call(
        functools.partial(_diff_latent_kernel, lam_init=lam_init),
        grid=(DEC_BATCH, nh),
        in_specs=[
            lam_spec,
            pl.BlockSpec((DEC_SEQ, DIFF_W), lambda b, h: (rb + b, h)),
            pl.BlockSpec((DEC_SEQ, DIFF_W), lambda b, h: (rb + b, nh + h)),
            pl.BlockSpec((DEC_SEQ, DIFF_W), lambda b, h: (rb + b, 2 * nh + h)),
            cache, cache, tab, tab, tab,
            pl.BlockSpec((1, DIFF_W), lambda b, h: (0, h)),
            pl.BlockSpec(memory_space=pl.ANY),
        ],
        out_specs=pl.BlockSpec((DEC_SEQ, DIFF_W), lambda b, h: (rb + b, h)),
        out_shape=jax.ShapeDtypeStruct((N_TOK, D_MODEL), F32),
        input_output_aliases={10: 0},
        compiler_params=_params("arbitrary", "arbitrary"),
        name="diff_latent",
    )(lam_p, proj, proj, proj, cache_k, cache_v, cos, slo, shi, gn, o_p)
    return o


def _layer_diff(x, p, mod, j, i):
    lam_init = 0.8 - 0.6 * math.exp(-0.3 * i)
    proj, new_k, new_v = _in_proj_kv(x, p['norm_w'][i], mod, p['diff_w_in'][j], DIFF_HEADS)
    o = _diff_attention(proj, p['diff_lambda'][j], p['diff_gn'][j], p['cache_diff_k'], p['cache_diff_v'], j,
                        lam_init)
    x = _out_proj(o, proj, 3, p['diff_w_out'][j], x, mod, p['final_norm_w'], False)
    return x, new_k, new_v


NA_ROWS = DEC_SEQ // GRID_W
NA_WR = min(NA_WIN_R, NA_ROWS)
NA_LOC = NA_WR * GRID_W
NA_ROW_GROUP = 4
NA_PAIR_GROUP = 2


def _na_prompt_kernel(q_ref, k_ref, v_ref, o_ref):
    first = _first_half_lanes()
    scale = NA_HD ** -0.5
    for p0 in range(0, NA_HEADS // 2, NA_PAIR_GROUP):
        pairs = range(p0, p0 + NA_PAIR_GROUP)
        kb = [k_ref[:, pr * LANES:(pr + 1) * LANES].astype(BF16) for pr in pairs]
        vb = [v_ref[:, pr * LANES:(pr + 1) * LANES].astype(BF16) for pr in pairs]
        items = [(i, half) for i in range(NA_PAIR_GROUP) for half in range(2)]
        qm = [jnp.where(first if half == 0 else ~first, q_ref[:, (p0 + i) * LANES:(p0 + i + 1) * LANES], 0.0)
              .astype(BF16) for i, half in items]
        s = [_dot_nt(x, kb[i]) * scale for x, (i, _) in zip(qm, items)]
        e = [jnp.exp(x - jnp.max(x, axis=-1, keepdims=True)) for x in s]
        inv = [1.0 / jnp.sum(x, axis=-1, keepdims=True) for x in e]
        outs = [_dot(x.astype(BF16), vb[i]) * z for x, z, (i, _) in zip(e, inv, items)]
        for i in range(NA_PAIR_GROUP):
            o_ref[:, (p0 + i) * LANES:(p0 + i + 1) * LANES] = jnp.where(first, outs[2 * i], outs[2 * i + 1])


def _na_latent_kernel(q_ref, k_ref, v_ref, kc_ref, vc_ref, tab_ref, _prev_ref, o_ref):
    first = _first_half_lanes()
    scale = NA_HD ** -0.5
    kb = k_ref[...].astype(BF16)
    vb = v_ref[...].astype(BF16)
    kcb = kc_ref[0, 0].astype(BF16)
    vcb = vc_ref[0, 0].astype(BF16)
    qcol = lax.broadcasted_iota(jnp.int32, (GRID_W, NA_LOC), 0)
    kcol = lax.broadcasted_iota(jnp.int32, (GRID_W, NA_LOC), 1) & (GRID_W - 1)
    cstart = jnp.clip(qcol - NA_WIN_C // 2, 0, GRID_W - NA_WIN_C)
    col_ok = (kcol >= cstart) & (kcol < cstart + NA_WIN_C)
    def bias_of(r, rs, half):
        parts = []
        for w in range(0, NA_WR, 2):
            src = jnp.broadcast_to(tab_ref[half, rs + w - r + NA_WIN_R - 1], (GRID_W, LANES))
            parts.append(pltpu.roll(src, LANES - (NA_WIN_C - 1), axis=1, stride=1, stride_axis=0))
        return jnp.concatenate(parts, axis=1)

    for r0 in range(0, NA_ROWS, NA_ROW_GROUP):
        items = [(r, min(max(r - NA_WR // 2, 0), NA_ROWS - NA_WR), half)
                 for r in range(r0, r0 + NA_ROW_GROUP) for half in range(2)]
        qm = [jnp.where(first if half == 0 else ~first, q_ref[r * GRID_W:(r + 1) * GRID_W, :], 0.0).astype(BF16)
              for r, _, half in items]
        s_loc = [_dot_nt(x, kb[rs * GRID_W:(rs + NA_WR) * GRID_W]) for x, (_, rs, _) in zip(qm, items)]
        s_ctx = [_dot_nt(x, kcb) * scale for x in qm]
        s_loc = [jnp.where(col_ok, x * scale + bias_of(*it), -jnp.inf) for x, it in zip(s_loc, items)]
        m = [jnp.maximum(jnp.max(x, axis=-1, keepdims=True), jnp.max(y, axis=-1, keepdims=True))
             for x, y in zip(s_loc, s_ctx)]
        e_loc = [jnp.exp(x - mm) for x, mm in zip(s_loc, m)]
        e_ctx = [jnp.exp(x - mm) for x, mm in zip(s_ctx, m)]
        inv = [1.0 / (jnp.sum(x, axis=-1, keepdims=True) + jnp.sum(y, axis=-1, keepdims=True))
               for x, y in zip(e_loc, e_ctx)]
        pv = [_dot(x.astype(BF16), vb[rs * GRID_W:(rs + NA_WR) * GRID_W]) for x, (_, rs, _) in zip(e_loc, items)]
        pc = [_dot(x.astype(BF16), vcb) for x in e_ctx]
        outs = [(x + y) * z for x, y, z in zip(pv, pc, inv)]
        for n in range(0, len(items), 2):
            r = items[n][0]
            o_ref[r * GRID_W:(r + 1) * GRID_W, :] = jnp.where(first, outs[n], outs[n + 1])


def _na_bias_pairs(table):
    t = table.astype(F32)
    nc = 2 * NA_WIN_C - 1
    z = jnp.zeros(t[:, :-1].shape[:2] + (GRID_W - nc,), F32)
    return jnp.concatenate([t[:, :-1], z, t[:, 1:], z], axis=-1)[:, :, None, :]


def _pair_heads(cache):
    c = cache.reshape(DEC_BATCH, NA_HEADS // 2, 2, PAST_LEN, NA_HD)
    return c.transpose(0, 1, 3, 2, 4).reshape(DEC_BATCH, NA_HEADS // 2, PAST_LEN, LANES)


def _na_attention(proj, bias_table, cache_k, cache_v):
    o_p = pl.pallas_call(
        _na_prompt_kernel,
        grid=(BATCH,),
        in_specs=[
            pl.BlockSpec((SEQ, D_MODEL), lambda b: (b, 0)),
            pl.BlockSpec((SEQ, D_MODEL), lambda b: (b, 1)),
            pl.BlockSpec((SEQ, D_MODEL), lambda b: (b, 2)),
        ],
        out_specs=pl.BlockSpec((SEQ, D_MODEL), lambda b: (b, 0)),
        out_shape=jax.ShapeDtypeStruct((N_TOK, D_MODEL), F32),
        compiler_params=_params("arbitrary"),
        name="na_prompt",
    )(proj, proj, proj)
    rb = N_PROMPT_TOK // DEC_SEQ
    npair = NA_HEADS // 2
    cache = pl.BlockSpec((1, 1, PAST_LEN, LANES), lambda pr, b: (b, pr, 0, 0))
    o = pl.pallas_call(
        _na_latent_kernel,
        grid=(npair, DEC_BATCH),
        in_specs=[
            pl.BlockSpec((DEC_SEQ, LANES), lambda pr, b: (rb + b, pr)),
            pl.BlockSpec((DEC_SEQ, LANES), lambda pr, b: (rb + b, npair + pr)),
            pl.BlockSpec((DEC_SEQ, LANES), lambda pr, b: (rb + b, 2 * npair + pr)),
            cache, cache,
            pl.BlockSpec((2, 2 * NA_WIN_R - 2, 1, LANES), lambda pr, b: (pr, 0, 0, 0)),
            pl.BlockSpec(memory_space=pl.ANY),
        ],
        out_specs=pl.BlockSpec((DEC_SEQ, LANES), lambda pr, b: (rb + b, pr)),
        out_shape=jax.ShapeDtypeStruct((N_TOK, D_MODEL), F32),
        input_output_aliases={6: 0},
        compiler_params=_params("arbitrary", "arbitrary"),
        name="na_latent",
    )(proj, proj, proj, _pair_heads(cache_k), _pair_heads(cache_v), _na_bias_pairs(bias_table), o_p)
    return o


def _layer_na(x, p, mod, j, final):
    i = N_MIXERS * j + 3
    proj, new_k, new_v = _in_proj_kv(x, p['norm_w'][i], mod, p['na_w_in'][j], NA_HEADS)
    o = _na_attention(proj, p['na_bias'][j], p['cache_na_k'][:, j], p['cache_na_v'][:, j])
    args = (o, proj, 3, p['na_w_out'][j], x, mod, p['final_norm_w'])
    if final:
        x = (_out_proj(*args, True, rows=(0, N_PROMPT_TOK)), _out_proj(*args, True, rows=(N_PROMPT_TOK, N_TOK)))
    else:
        x = _out_proj(*args, False)
    return x, new_k, new_v


def kernel(x_prompt, x_sample, state_ret, state_rwkv, cache_diff_k, cache_diff_v, cache_na_k, cache_na_v,
           c, c_ctx, norm_w, w_mod, b_mod, final_norm_w,
           ret_w_in, ret_decay, ret_gn, ret_w_out,
           rwkv_mu, rwkv_w_in, rwkv_w0, rwkv_wA, rwkv_wB, rwkv_a0, rwkv_aA, rwkv_aB,
           rwkv_kk, rwkv_ka, rwkv_rk, rwkv_gn, rwkv_w_out,
           diff_w_in, diff_lambda, diff_gn, diff_w_out,
           na_w_in, na_bias, na_w_out):
    p = dict(locals())
    cond = jnp.zeros((N_COND, D_MODEL), F32).at[0].set(c_ctx).at[1:1 + DEC_BATCH].set(c)
    mods = _modulation(cond, w_mod, b_mod)
    x = (x_prompt.reshape(N_PROMPT_TOK, D_MODEL), x_sample.reshape(N_SAMPLE_TOK, D_MODEL))
    new = {n: [] for n in ('ret', 'rwkv', 'dk', 'dv', 'nk', 'nv')}
    for i in range(DEPTH):
        kind, j = i % N_MIXERS, i // N_MIXERS
        if kind == 0:
            x, st = _layer_ret(x, p, mods[i], j)
            new['ret'].append(st)
        elif kind == 1:
            x, st = _layer_rwkv(x, p, mods[i], j)
            new['rwkv'].append(st)
        elif kind == 2:
            x, ck, cv = _layer_diff(x, p, mods[i], j, i)
            new['dk'].append(ck)
            new['dv'].append(cv)
        else:
            x, ck, cv = _layer_na(x, p, mods[i], j, final=(i == DEPTH - 1))
            new['nk'].append(ck)
            new['nv'].append(cv)
    if DEPTH % N_MIXERS:
        raise NotImplementedError("the final norm is fused into the last neighbourhood-attention layer")
    cat = lambda xs: xs[0] if len(xs) == 1 else jnp.concatenate(xs, axis=1)
    return (x[0].reshape(BATCH, SEQ, D_MODEL), x[1].reshape(DEC_BATCH, DEC_SEQ, D_MODEL),
            cat(new['ret']), cat(new['rwkv']), cat(new['dk']), cat(new['dv']), cat(new['nk']), cat(new['nv']))
```

```python
import functools
import math

import jax
import jax.numpy as jnp
from jax import lax
from jax.experimental import pallas as pl
from jax.experimental.pallas import tpu as pltpu

F32 = jnp.float32
BF16 = jnp.bfloat16

D_MODEL = 1024
BATCH = 32
SEQ = 256
DEPTH = 4
N_MIXERS = 4
DEC_BATCH = 2
DEC_SEQ = 1024
PAST_LEN = 256
GRID_W = 64

RET_HEADS = 4
RET_DK = 256
RET_DV = 512
RET_QK = 1024
RET_V = 2048

RWKV_HD = 64
RWKV_HEADS = 16
RWKV_RANK = 64

DIFF_HEADS = 8
DIFF_HD = 64

NA_HEADS = 16
NA_HD = 64
NA_WIN_R = 8
NA_WIN_C = 16

ROPE_BASE = 10000.0
EPS = 1e-6
GN_EPS = 1e-5

N_PROMPT_TOK = BATCH * SEQ
N_SAMPLE_TOK = DEC_BATCH * DEC_SEQ
N_TOK = N_PROMPT_TOK + N_SAMPLE_TOK
N_COND = 8

LANES = 128
VMEM_LIMIT = 56 * 2 ** 20


def _params(*sem):
    return pltpu.CompilerParams(dimension_semantics=sem, vmem_limit_bytes=VMEM_LIMIT)


def _cond_of_tile(i, tm):
    npt = N_PROMPT_TOK // tm
    return jnp.where(i < npt, 0, 1 + (i - npt) // (DEC_SEQ // tm))


def _sigmoid(x):
    return 1.0 / (1.0 + jnp.exp(-x))


def _silu(x):
    return x * _sigmoid(x)


def _dot(a, b):
    return jnp.dot(a, b, preferred_element_type=F32)


def _dot_nt(a, b):
    return lax.dot_general(a, b, (((1,), (1,)), ((), ())), preferred_element_type=F32)


def _dot_tn(a, b):
    return lax.dot_general(a, b, (((0,), (0,)), ((), ())), preferred_element_type=F32)


def _softmax_rows(s):
    m = jnp.max(s, axis=-1, keepdims=True)
    e = jnp.exp(s - m)
    return e / jnp.sum(e, axis=-1, keepdims=True)


def _mod_kernel(c_ref, w_ref, b_ref, o_ref):
    s = _silu(c_ref[...])
    o_ref[0] = jnp.dot(s, w_ref[0], precision=lax.Precision.HIGHEST, preferred_element_type=F32) + b_ref[0]


def _modulation(cond, w_mod, b_mod):
    tn = D_MODEL
    out = pl.pallas_call(
        _mod_kernel,
        grid=(DEPTH, 3 * D_MODEL // tn),
        in_specs=[
            pl.BlockSpec((N_COND, D_MODEL), lambda l, j: (0, 0)),
            pl.BlockSpec((1, D_MODEL, tn), lambda l, j: (l, 0, j)),
            pl.BlockSpec((1, 1, tn), lambda l, j: (l, 0, j)),
        ],
        out_specs=pl.BlockSpec((1, N_COND, tn), lambda l, j: (l, 0, j)),
        out_shape=jax.ShapeDtypeStruct((DEPTH, N_COND, 3 * D_MODEL), F32),
        compiler_params=_params("arbitrary", "arbitrary"),
        name="modulation",
    )(cond, w_mod, b_mod.reshape(DEPTH, 1, 3 * D_MODEL))
    return out.reshape(DEPTH, N_COND, 3, 1, D_MODEL)


def _norm_mod(x, nw, mod_ref):
    ms = jnp.mean(x * x, axis=-1, keepdims=True)
    y = x * lax.rsqrt(ms + EPS) * nw
    return y * (1.0 + mod_ref[0, 1]) + mod_ref[0, 0]


IN_TM = 1024
IN_TN = 2048


def _x_specs(x, tm, tile_of):
    if not isinstance(x, tuple):
        return [pl.BlockSpec((tm, D_MODEL), lambda *g: (tile_of(*g), 0))], (x,)
    npt = N_PROMPT_TOK // tm
    return [pl.BlockSpec((tm, D_MODEL), lambda *g: (jnp.minimum(tile_of(*g), npt - 1), 0)),
            pl.BlockSpec((tm, D_MODEL), lambda *g: (jnp.maximum(tile_of(*g) - npt, 0), 0))], x


def _read_x(x_refs, tile, tm):
    if len(x_refs) == 1:
        return x_refs[0][...]
    return jnp.where(tile < N_PROMPT_TOK // tm, x_refs[0][...], x_refs[1][...])


def _in_proj_kernel(*refs, n_x):
    x_refs = refs[:n_x]
    nw_ref, mod_ref, w_ref, o_ref, h_ref = refs[n_x:]

    @pl.when(pl.program_id(1) == 0)
    def _():
        x = _read_x(x_refs, pl.program_id(0), IN_TM)
        h_ref[...] = _norm_mod(x, nw_ref[...], mod_ref).astype(BF16)

    o_ref[...] = _dot(h_ref[...], w_ref[...]).astype(o_ref.dtype)


def _in_proj(x, norm_w, mod, w, tn, out_dtype=F32):
    n = w.shape[1]
    w = w.astype(BF16)
    x_specs, xs = _x_specs(x, IN_TM, lambda i, j: i)
    return pl.pallas_call(
        functools.partial(_in_proj_kernel, n_x=len(xs)),
        grid=(N_TOK // IN_TM, n // tn),
        in_specs=x_specs + [
            pl.BlockSpec((1, D_MODEL), lambda i, j: (0, 0)),
            pl.BlockSpec((1, 3, 1, D_MODEL), lambda i, j: (_cond_of_tile(i, IN_TM), 0, 0, 0)),
            pl.BlockSpec((D_MODEL, tn), lambda i, j: (0, j)),
        ],
        out_specs=pl.BlockSpec((IN_TM, tn), lambda i, j: (i, j)),
        out_shape=jax.ShapeDtypeStruct((N_TOK, n), out_dtype),
        scratch_shapes=[pltpu.VMEM((IN_TM, D_MODEL), BF16)],
        compiler_params=_params("arbitrary", "arbitrary"),
        name="in_proj",
    )(*xs, norm_w.reshape(1, D_MODEL), mod, w)


def _in_proj_kv_kernel(x_ref, nw_ref, mod_ref, w_ref, o_ref, ck_ref, cv_ref, h_ref, *, heads, tn):
    i = pl.program_id(0)
    j = pl.program_id(1)

    @pl.when(j == 0)
    def _():
        h_ref[...] = _norm_mod(x_ref[...], nw_ref[...], mod_ref).astype(BF16)

    acc = _dot(h_ref[...], w_ref[...])
    o_ref[...] = acc.astype(o_ref.dtype)
    hd = D_MODEL // heads
    for col, c_ref in ((D_MODEL, ck_ref), (2 * D_MODEL, cv_ref)):
        @pl.when((j == col // tn) & (i < N_PROMPT_TOK // IN_TM))
        def _(c_ref=c_ref, c0=col % tn):
            for s in range(IN_TM // SEQ):
                for h in range(heads):
                    c_ref[s, 0, h] = acc[s * SEQ:(s + 1) * SEQ, c0 + h * hd:c0 + (h + 1) * hd]


def _in_proj_kv(x, norm_w, mod, w, heads):
    n = w.shape[1]
    tn = D_MODEL
    spb = IN_TM // SEQ
    last = N_PROMPT_TOK // IN_TM - 1
    cache = pl.BlockSpec((spb, 1, heads, SEQ, D_MODEL // heads), lambda i, j: (jnp.minimum(i, last), 0, 0, 0, 0))
    cache_shape = jax.ShapeDtypeStruct((BATCH, 1, heads, SEQ, D_MODEL // heads), F32)
    return pl.pallas_call(
        functools.partial(_in_proj_kv_kernel, heads=heads, tn=tn),
        grid=(N_TOK // IN_TM, n // tn),
        in_specs=[
            pl.BlockSpec((IN_TM, D_MODEL), lambda i, j: (i, 0)),
            pl.BlockSpec((1, D_MODEL), lambda i, j: (0, 0)),
            pl.BlockSpec((1, 3, 1, D_MODEL), lambda i, j: (_cond_of_tile(i, IN_TM), 0, 0, 0)),
            pl.BlockSpec((D_MODEL, tn), lambda i, j: (0, j)),
        ],
        out_specs=[pl.BlockSpec((IN_TM, tn), lambda i, j: (i, j)), cache, cache],
        out_shape=[jax.ShapeDtypeStruct((N_TOK, n), BF16), cache_shape, cache_shape],
        scratch_shapes=[pltpu.VMEM((IN_TM, D_MODEL), BF16)],
        compiler_params=_params("arbitrary", "arbitrary"),
        name="in_proj_kv",
    )(x, norm_w.reshape(1, D_MODEL), mod, w.astype(BF16))


OUT_TM_BYTES = 4 * 2 ** 20


def _out_proj_kernel(*refs, n_x, t0, tm, final):
    x_refs = refs[:n_x]
    o_ref, g_ref, w_ref, mod_ref, fw_ref, y_ref, wb_ref = refs[n_x:]

    @pl.when(pl.program_id(0) == 0)
    def _():
        wb_ref[...] = w_ref[...].astype(BF16)

    a = (o_ref[...] * _silu(g_ref[...].astype(F32))).astype(BF16)
    xn = _read_x(x_refs, t0 + pl.program_id(0), tm) + mod_ref[0, 2] * _dot(a, wb_ref[...])
    if final:
        ms = jnp.mean(xn * xn, axis=-1, keepdims=True)
        xn = xn * lax.rsqrt(ms + EPS) * fw_ref[...]
    y_ref[...] = xn


def _out_proj(o, g_arr, g_blk, w, x, mod, final_w, final, rows=(0, N_TOK), g_row0=0):
    k = w.shape[0]
    tm = OUT_TM_BYTES // (4 * k)
    t0 = rows[0] // tm
    g0 = g_row0 // tm
    x_specs, xs = _x_specs(x, tm, lambda i: t0 + i)
    return pl.pallas_call(
        functools.partial(_out_proj_kernel, n_x=len(xs), t0=t0, tm=tm, final=final),
        grid=((rows[1] - rows[0]) // tm,),
        in_specs=x_specs + [
            pl.BlockSpec((tm, k), lambda i: (t0 + i, 0)),
            pl.BlockSpec((tm, k), lambda i: (g0 + t0 + i, g_blk)),
            pl.BlockSpec((k, D_MODEL), lambda i: (0, 0)),
            pl.BlockSpec((1, 3, 1, D_MODEL), lambda i: (_cond_of_tile(t0 + i, tm), 0, 0, 0)),
            pl.BlockSpec((1, D_MODEL), lambda i: (0, 0)),
        ],
        out_specs=pl.BlockSpec((tm, D_MODEL), lambda i: (i, 0)),
        out_shape=jax.ShapeDtypeStruct((rows[1] - rows[0], D_MODEL), F32),
        scratch_shapes=[pltpu.VMEM((k, D_MODEL), BF16)],
        compiler_params=_params("arbitrary"),
        name="out_proj",
    )(*xs, o, g_arr, w, mod, final_w.reshape(1, D_MODEL))


def _rope_tables(d):
    q = d // 4
    t = jnp.arange(DEC_SEQ)
    row = (t // GRID_W).astype(F32)
    col = (t % GRID_W).astype(F32)
    inv = ROPE_BASE ** (-jnp.arange(0, 2 * q, 2, dtype=F32) / (2 * q))
    ar = row[:, None] * inv[None, :]
    ac = col[:, None] * inv[None, :]
    z = jnp.zeros_like(ar)
    cos = jnp.concatenate([jnp.cos(ar), jnp.cos(ar), jnp.cos(ac), jnp.cos(ac)], axis=-1)
    sin_lo = jnp.concatenate([-jnp.sin(ar), z, -jnp.sin(ac), z], axis=-1)
    sin_hi = jnp.concatenate([z, jnp.sin(ar), z, jnp.sin(ac)], axis=-1)
    return cos, sin_lo, sin_hi


def _rope(x, cos, sin_lo, sin_hi, q):
    w = x.shape[-1]
    x_next = pltpu.roll(x, w - q, axis=1)
    x_prev = pltpu.roll(x, q, axis=1)
    return x * cos + x_next * sin_lo + x_prev * sin_hi


RET_QB = 256


def _ret_kernel(lg_ref, q_ref, k_ref, v_ref, gn_ref, *rest, seq, latent):
    if latent:
        cos_ref, slo_ref, shi_ref, s0_ref, _prev_ref, o_ref = rest
    else:
        o_ref, st_ref, dec_ref = rest
    h = pl.program_id(1 if latent else 0)
    lgf = lg_ref[0, h]
    lgb = lg_ref[1, h]
    q = q_ref[...].astype(F32)
    k = k_ref[...].astype(F32)
    if latent:
        q = _rope(q, cos_ref[...], slo_ref[...], shi_ref[...], RET_DK // 4)
        k = _rope(k, cos_ref[...], slo_ref[...], shi_ref[...], RET_DK // 4)
    k = k * (RET_DK ** -0.5)
    kb = k.astype(BF16)
    vb = v_ref[...].astype(BF16)
    gn = gn_ref[...]

    def decay(qi):
        ii = lax.broadcasted_iota(jnp.int32, (RET_QB, seq), 0) + qi * RET_QB
        jj = lax.broadcasted_iota(jnp.int32, (RET_QB, seq), 1)
        gap = (ii - jj).astype(F32)
        return (jnp.where(gap >= 0, jnp.exp(lgf * jnp.maximum(gap, 0.0)), 0.0)
                + jnp.where(gap <= 0, jnp.exp(lgb * jnp.maximum(-gap, 0.0)), 0.0))

    if not latent:
        @pl.when(pl.program_id(1) == 0)
        def _():
            dec_ref[...] = decay(0)

    for qi in range(seq // RET_QB):
        qblk = q[qi * RET_QB:(qi + 1) * RET_QB]
        s = _dot_nt(qblk.astype(BF16), kb)
        dec = decay(qi) if latent else dec_ref[...]
        o = _dot((s * dec).astype(BF16), vb)
        if latent:
            pos = (lax.broadcasted_iota(jnp.int32, (RET_QB, 1), 0) + qi * RET_QB).astype(F32)
            qf = qblk * jnp.exp(lgf * (pos + 1.0))
            qr = qblk * jnp.exp(lgb * (seq - pos))
            o = o + _dot(qf.astype(BF16), s0_ref[0, 0, 0, 0].astype(BF16))
            o = o + _dot(qr.astype(BF16), s0_ref[0, 0, 1, 0].astype(BF16))
        oc = o - jnp.mean(o, axis=-1, keepdims=True)
        o = oc * lax.rsqrt(jnp.mean(oc * oc, axis=-1, keepdims=True) + GN_EPS) * gn
        o_ref[qi * RET_QB:(qi + 1) * RET_QB, :] = o
    if not latent:
        pos = lax.broadcasted_iota(jnp.int32, (seq, 1), 0).astype(F32)
        kf = k * jnp.exp(lgf * (seq - 1.0 - pos))
        kr = k * jnp.exp(lgb * pos)
        st_ref[0, 0, 0, 0] = _dot_tn(kf.astype(BF16), vb)
        st_ref[0, 0, 1, 0] = _dot_tn(kr.astype(BF16), vb)


def _retention(p, log_g, gn_w, state_ret, j):
    smem = pl.BlockSpec(memory_space=pltpu.SMEM)
    gn = gn_w.reshape(1, RET_V)
    kq = RET_QK // RET_DK
    o_p, st = pl.pallas_call(
        functools.partial(_ret_kernel, seq=SEQ, latent=False),
        grid=(RET_HEADS, BATCH),
        in_specs=[
            smem,
            pl.BlockSpec((SEQ, RET_DK), lambda h, b: (b, h)),
            pl.BlockSpec((SEQ, RET_DK), lambda h, b: (b, kq + h)),
            pl.BlockSpec((SEQ, RET_DV), lambda h, b: (b, kq + h)),
            pl.BlockSpec((1, RET_DV), lambda h, b: (0, h)),
        ],
        out_specs=[
            pl.BlockSpec((SEQ, RET_DV), lambda h, b: (b, h)),
            pl.BlockSpec((1, 1, 2, 1, RET_DK, RET_DV), lambda h, b: (b, 0, 0, h, 0, 0)),
        ],
        out_shape=[
            jax.ShapeDtypeStruct((N_TOK, RET_V), F32),
            jax.ShapeDtypeStruct((BATCH, 1, 2, RET_HEADS, RET_DK, RET_DV), F32),
        ],
        scratch_shapes=[pltpu.VMEM((RET_QB, SEQ), F32)],
        compiler_params=_params("arbitrary", "arbitrary"),
        name="retention_prompt",
    )(log_g, p, p, p, gn)
    cos, slo, shi = _rope_tables(RET_DK)
    rb = N_PROMPT_TOK // DEC_SEQ
    full = pl.BlockSpec((DEC_SEQ, RET_DK), lambda b, h: (0, 0))
    o = pl.pallas_call(
        functools.partial(_ret_kernel, seq=DEC_SEQ, latent=True),
        grid=(DEC_BATCH, RET_HEADS),
        in_specs=[
            smem,
            pl.BlockSpec((DEC_SEQ, RET_DK), lambda b, h: (rb + b, h)),
            pl.BlockSpec((DEC_SEQ, RET_DK), lambda b, h: (rb + b, kq + h)),
            pl.BlockSpec((DEC_SEQ, RET_DV), lambda b, h: (rb + b, kq + h)),
            pl.BlockSpec((1, RET_DV), lambda b, h: (0, h)),
            full, full, full,
            pl.BlockSpec((1, 1, 2, 1, RET_DK, RET_DV), lambda b, h: (b, j, 0, h, 0, 0)),
            pl.BlockSpec(memory_space=pl.ANY),
        ],
        out_specs=pl.BlockSpec((DEC_SEQ, RET_DV), lambda b, h: (rb + b, h)),
        out_shape=jax.ShapeDtypeStruct((N_TOK, RET_V), F32),
        input_output_aliases={9: 0},
        compiler_params=_params("arbitrary", "arbitrary"),
        name="retention_latent",
    )(log_g, p, p, p, gn, cos, slo, shi, state_ret, o_p)
    return o, st


def _layer_ret(x, p, mod, j):
    i = N_MIXERS * j + 0
    proj = _in_proj(x, p['norm_w'][i], mod, p['ret_w_in'][j], IN_TN, out_dtype=BF16)
    log_g = jax.nn.log_sigmoid(p['ret_decay'][j].astype(F32))
    o, st = _retention(proj, log_g, p['ret_gn'][j], p['state_ret'], j)
    x = _out_proj(o, proj, (2 * RET_QK + RET_V) // RET_V, p['ret_w_out'][j], x, mod, p['final_norm_w'], False)
    return x, st


RW_TM = 512
RW_HALO = 8
RW_C = 64
RW_LOCK = 4
RW_PAIRS = 2
RW_PAIRS_PROMPT = 2


def _rwkv_prep_kernel(x_ref, xp_ref, xn_ref, nw_ref, mod_ref, mu_ref, wa_ref, aa_ref, wb_ref, ab_ref,
                      w0_ref, a0_ref, xm_ref, lw_ref, a_ref):
    i = pl.program_id(0)
    nw = nw_ref[...]
    h = _norm_mod(x_ref[...], nw, mod_ref)
    h_before = _norm_mod(xp_ref[RW_HALO - 1:RW_HALO, :], nw, mod_ref)
    h_after = _norm_mod(xn_ref[0:1, :], nw, mod_ref)
    seq = jnp.where(i < N_PROMPT_TOK // RW_TM, SEQ, DEC_SEQ)
    row = lax.broadcasted_iota(jnp.int32, (RW_TM, 1), 0)
    t = (row + i * RW_TM) & (seq - 1)
    prev = jnp.where(row == 0, h_before, pltpu.roll(h, 1, axis=0))
    nxt = jnp.where(row == RW_TM - 1, h_after, pltpu.roll(h, RW_TM - 1, axis=0))
    prev = jnp.where(t == 0, 0.0, prev)
    nxt = jnp.where(t == seq - 1, 0.0, nxt)
    xx = 0.5 * (prev + nxt) - h
    for n, m in enumerate((0, 2, 3, 5)):
        xm_ref[n] = (h + xx * mu_ref[m:m + 1, :]).astype(BF16)
    xw = (h + xx * mu_ref[1:2, :]).astype(BF16)
    xa = (h + xx * mu_ref[4:5, :]).astype(BF16)
    lw = jnp.tanh(_dot(xw, wa_ref[...])).astype(BF16)
    la = _dot(xa, aa_ref[...]).astype(BF16)
    for dr in range(2):
        wl = w0_ref[dr:dr + 1, :] + _dot(lw, wb_ref[dr])
        lw_ref[dr] = -math.exp(-0.5) * _sigmoid(wl)
        a_ref[dr] = _sigmoid(a0_ref[dr:dr + 1, :] + _dot(la, ab_ref[dr]))


def _rwkv_prep(x, norm_w, mod, mu, wa2, aa2, wb_pad, ab_pad, w0, a0):
    nt = N_TOK // RW_TM
    hb = RW_TM // RW_HALO
    last = N_TOK // RW_HALO - 1
    full2 = lambda shape: pl.BlockSpec(shape, lambda i: (0, 0))
    full3 = lambda shape: pl.BlockSpec(shape, lambda i: (0, 0, 0))
    return pl.pallas_call(
        _rwkv_prep_kernel,
        grid=(nt,),
        in_specs=[
            pl.BlockSpec((RW_TM, D_MODEL), lambda i: (i, 0)),
            pl.BlockSpec((RW_HALO, D_MODEL), lambda i: (jnp.maximum(i * hb - 1, 0), 0)),
            pl.BlockSpec((RW_HALO, D_MODEL), lambda i: (jnp.minimum((i + 1) * hb, last), 0)),
            full2((1, D_MODEL)),
            pl.BlockSpec((1, 3, 1, D_MODEL), lambda i: (_cond_of_tile(i, RW_TM), 0, 0, 0)),
            full2((6, D_MODEL)),
            full2((D_MODEL, 2 * RWKV_RANK)),
            full2((D_MODEL, 2 * RWKV_RANK)),
            full3((2, 2 * RWKV_RANK, D_MODEL)),
            full3((2, 2 * RWKV_RANK, D_MODEL)),
            full2((2, D_MODEL)),
            full2((2, D_MODEL)),
        ],
        out_specs=[
            pl.BlockSpec((4, RW_TM, D_MODEL), lambda i: (0, i, 0)),
            pl.BlockSpec((2, RW_TM, D_MODEL), lambda i: (0, i, 0)),
            pl.BlockSpec((2, RW_TM, D_MODEL), lambda i: (0, i, 0)),
        ],
        out_shape=[
            jax.ShapeDtypeStruct((4, N_TOK, D_MODEL), BF16),
            jax.ShapeDtypeStruct((2, N_TOK, D_MODEL), F32),
            jax.ShapeDtypeStruct((2, N_TOK, D_MODEL), F32),
        ],
        compiler_params=_params("arbitrary"),
        name="rwkv_prep",
    )(x, x, x, norm_w.reshape(1, D_MODEL), mod, mu, wa2, aa2, wb_pad, ab_pad, w0, a0)


def _bmm_kernel(a_ref, w_ref, o_ref):
    o_ref[0] = _dot(a_ref[0], w_ref[...])


def _rwkv_rkvg(xm, w):
    tm = 1024
    return pl.pallas_call(
        _bmm_kernel,
        grid=(4, N_TOK // tm),
        in_specs=[
            pl.BlockSpec((1, tm, D_MODEL), lambda n, i: (n, i, 0)),
            pl.BlockSpec((D_MODEL, D_MODEL), lambda n, i: (0, n)),
        ],
        out_specs=pl.BlockSpec((1, tm, D_MODEL), lambda n, i: (n, i, 0)),
        out_shape=jax.ShapeDtypeStruct((4, N_TOK, D_MODEL), F32),
        compiler_params=_params("arbitrary", "arbitrary"),
        name="rwkv_rkvg",
    )(xm, w)


def _head_sum(x, first):
    s0 = jnp.sum(jnp.where(first, x, 0.0), axis=-1, keepdims=True)
    s1 = jnp.sum(jnp.where(first, 0.0, x), axis=-1, keepdims=True)
    return jnp.where(first, s0, s1)


def _stack_heads(x, first):
    return jnp.concatenate([jnp.where(first, x, 0.0), jnp.where(first, 0.0, x)], axis=0)


def _cumsum_rows(tri, x):
    hi = x.astype(BF16)
    r1 = x - hi.astype(F32)
    mid = r1.astype(BF16)
    lo = (r1 - mid.astype(F32)).astype(BF16)
    return _dot(tri, hi) + _dot(tri, mid) + _dot(tri, lo)


def _rwkv_chunk_kernel(*refs, seq, zero_init):
    if zero_init:
        (rkv_ref, lw_ref, a_ref, kkp_ref, kap_ref, rkp_ref, gn_ref, o_ref, st_ref,
         kk_scr, y_scr, tar_scr, lrb_scr, b2_scr, w2_scr, yl_scr, kv_scr, pc_scr) = refs
    else:
        (rkv_ref, lw_ref, a_ref, kkp_ref, kap_ref, rkp_ref, gn_ref, s0_ref, _prev_ref, o_ref,
         kk_scr, y_scr, tar_scr, lrb_scr, b2_scr, w2_scr, yl_scr, kv_scr, pc_scr) = refs
    c_len = RW_C
    n_ch = seq // c_len
    rows2 = 2 * c_len
    first = _first_half_lanes()
    kap = kap_ref[...]

    kk = rkv_ref[1] * kkp_ref[...]
    kk_scr[...] = kk * lax.rsqrt(jnp.maximum(_head_sum(kk * kk, first), 1e-12))

    rr = lax.broadcasted_iota(jnp.int32, (rows2, rows2), 0)
    cc = lax.broadcasted_iota(jnp.int32, (rows2, rows2), 1)
    eye = (rr == cc).astype(F32)
    tr = lax.broadcasted_iota(jnp.int32, (c_len, c_len), 0)
    tc = lax.broadcasted_iota(jnp.int32, (c_len, c_len), 1)

    def same(shift):
        return (rr >> shift) == (cc >> shift)

    head = same(6)
    strict = (head & (cc < rr), head & (cc > rr))
    incl = (head & (cc <= rr), head & (cc >= rr))
    tri = ((tc <= tr).astype(BF16), (tc >= tr).astype(BF16))
    last = (c_len - 1, 0)

    def phase1(chains):
        dirs = [dr for dr, _ in chains]
        rows = [pl.ds(pl.multiple_of(c * c_len, c_len), c_len) for _, c in chains]
        lw = [lw_ref[dr, rw, :] for dr, rw in zip(dirs, rows)]
        cum = [_cumsum_rows(tri[dr], x) for dr, x in zip(dirs, lw)]
        a2, r2, b2, k2, v2, pc = [], [], [], [], [], []
        for dr, rw, lw_c, cum_c in zip(dirs, rows, lw, cum):
            a = a_ref[dr, rw, :]
            k = rkv_ref[1, rw, :]
            kk_c = kk_scr[rw, :]
            e_inc = jnp.exp(cum_c)
            e_inv = jnp.exp(-cum_c)
            a2.append(_stack_heads(-kk_c * jnp.exp(cum_c - lw_c), first).astype(BF16))
            r2.append(_stack_heads(rkv_ref[0, rw, :] * e_inc, first).astype(BF16))
            b2.append(_stack_heads(kk_c * a * e_inv, first).astype(BF16))
            k2.append(_stack_heads(k * (1.0 + (a - 1.0) * kap) * e_inv, first).astype(BF16))
            v2.append(_stack_heads(rkv_ref[2, rw, :], first).astype(BF16))
            pc.append(e_inc[last[dr]:last[dr] + 1, :])
        g = [_dot_nt(jnp.concatenate([x, y], axis=0), jnp.concatenate([z, w], axis=0))
             for x, y, z, w in zip(a2, r2, b2, k2)]
        l_ab = [jnp.where(strict[dr], x[:rows2, :rows2], 0.0) for dr, x in zip(dirs, g)]
        t = [eye + jnp.where(same(1), x, 0.0) for x in l_ab]
        side = {}
        for shift in range(1, 6):
            sib = same(shift + 1) & ~same(shift)
            tb = [x.astype(BF16) for x in t]
            mid = [_dot(jnp.where(sib, x, 0.0).astype(BF16), y) for x, y in zip(l_ab, tb)]
            if shift == 1:
                side['lv'] = [_dot(jnp.where(strict[dr], x[:rows2, rows2:], 0.0).astype(BF16), y)
                              for dr, x, y in zip(dirs, g, v2)]
            elif shift == 2:
                side['yl'] = [_dot(jnp.where(incl[dr], x[rows2:, rows2:], 0.0).astype(BF16), y)
                              for dr, x, y in zip(dirs, g, v2)]
            elif shift == 3:
                side['kv'] = [_dot_tn(x, y) for x, y in zip(v2, k2)]
            t = [x + _dot(y, z.astype(BF16)) for x, y, z in zip(t, tb, mid)]
        tb = [x.astype(BF16) for x in t]
        ta = [_dot(x, y) for x, y in zip(tb, a2)]
        w2 = [_dot(x, y.astype(BF16)) for x, y in zip(tb, side['lv'])]
        for i, (dr, c) in enumerate(chains):
            n = dr * n_ch + c
            tar_scr[n, :rows2, :] = ta[i].astype(BF16)
            tar_scr[n, rows2:, :] = r2[i]
            w2_scr[n] = w2[i]
            yl_scr[n] = side['yl'][i]
            kv_scr[n] = side['kv'][i]
            lrb_scr[n] = jnp.where(incl[dr], g[i][rows2:, :rows2], 0.0).astype(BF16)
            b2_scr[n] = b2[i]
            pc_scr[n] = pc[i]

    def body1(grp, carry):
        phase1([(dr, grp * RW_LOCK + j) for j in range(RW_LOCK) for dr in range(2)])
        return carry

    if n_ch == RW_LOCK:
        body1(0, 0)
    else:
        lax.fori_loop(0, n_ch // RW_LOCK, body1, 0)

    def body2(i, carry):
        cs = (i, n_ch - 1 - i)
        ns = [dr * n_ch + c for dr, c in enumerate(cs)]
        x = [_dot_nt(tar_scr[n], s2.astype(BF16)) for n, s2 in zip(ns, carry)]
        u2 = [(xx[:rows2] + w2_scr[n]).astype(BF16) for n, xx in zip(ns, x)]
        upd = [_dot_tn(u, b2_scr[n]) for n, u in zip(ns, u2)]
        yb = [_dot(lrb_scr[n], u) for n, u in zip(ns, u2)]
        out = []
        for dr, (c, n) in enumerate(zip(cs, ns)):
            y2 = x[dr][rows2:] + yb[dr] + yl_scr[n]
            y_scr[dr, pl.ds(pl.multiple_of(c * c_len, c_len), c_len), :] = y2[:c_len] + y2[c_len:]
            out.append((carry[dr] + upd[dr] + kv_scr[n]) * pc_scr[n])
        return tuple(out)

    if zero_init:
        init = (jnp.zeros((rows2, LANES), F32),) * 2
    else:
        init = (s0_ref[0, 0, 0], s0_ref[0, 1, 0])
    s_f, s_b = lax.fori_loop(0, n_ch, body2, init)

    y = y_scr[0] + y_scr[1]
    yc = y - _head_sum(y, first) * (1.0 / RWKV_HD)
    o = yc * lax.rsqrt(_head_sum(yc * yc, first) * (1.0 / RWKV_HD) + GN_EPS) * gn_ref[...]
    r = rkv_ref[0]
    k = rkv_ref[1]
    v = rkv_ref[2]
    for dr in range(2):
        kd = k * (1.0 + (a_ref[dr] - 1.0) * kap)
        o = o + _head_sum(r * kd * rkp_ref[...], first) * v
    o_ref[...] = o

    if zero_init:
        for dr, s2 in enumerate((s_f, s_b)):
            st_ref[0, 0, dr, 0] = s2[:RWKV_HD, :RWKV_HD]
            st_ref[0, 0, dr, 1] = s2[RWKV_HD:, RWKV_HD:]


def _rwkv_chunked(rkvg, lw, a, kkp, kap, rkp, gn, s0_pairs):
    npair = RWKV_HEADS // 2
    par = lambda *_: pl.BlockSpec((1, LANES), lambda s, p: (0, p))

    def scratch(seq):
        n = 2 * (seq // RW_C)
        r2 = 2 * RW_C
        return [
            pltpu.VMEM((seq, LANES), F32), pltpu.VMEM((2, seq, LANES), F32),
            pltpu.VMEM((n, 2 * r2, LANES), BF16), pltpu.VMEM((n, r2, r2), BF16), pltpu.VMEM((n, r2, LANES), BF16),
            pltpu.VMEM((n, r2, LANES), F32), pltpu.VMEM((n, r2, LANES), F32), pltpu.VMEM((n, r2, LANES), F32),
            pltpu.VMEM((n, 1, LANES), F32),
        ]

    def seq_specs(seq, rb):
        return [
            pl.BlockSpec((3, seq, LANES), lambda s, p: (0, rb + s, p)),
            pl.BlockSpec((2, seq, LANES), lambda s, p: (0, rb + s, p)),
            pl.BlockSpec((2, seq, LANES), lambda s, p: (0, rb + s, p)),
            par(), par(), par(), par(),
        ]

    pars = [u.reshape(1, D_MODEL) for u in (kkp, kap, rkp, gn)]
    o_p, st = pl.pallas_call(
        functools.partial(_rwkv_chunk_kernel, seq=SEQ, zero_init=True),
        grid=(BATCH, npair),
        in_specs=seq_specs(SEQ, 0),
        out_specs=[
            pl.BlockSpec((SEQ, LANES), lambda s, p: (s, p)),
            pl.BlockSpec((1, 1, 2, 2, RWKV_HD, RWKV_HD), lambda s, p: (s, 0, 0, p, 0, 0)),
        ],
        out_shape=[
            jax.ShapeDtypeStruct((N_TOK, D_MODEL), F32),
            jax.ShapeDtypeStruct((BATCH, 1, 2, RWKV_HEADS, RWKV_HD, RWKV_HD), F32),
        ],
        scratch_shapes=scratch(SEQ),
        compiler_params=_params("arbitrary", "arbitrary"),
        name="rwkv_prompt",
    )(rkvg, lw, a, *pars)
    rb = N_PROMPT_TOK // DEC_SEQ
    o = pl.pallas_call(
        functools.partial(_rwkv_chunk_kernel, seq=DEC_SEQ, zero_init=False),
        grid=(DEC_BATCH, npair),
        in_specs=seq_specs(DEC_SEQ, rb) + [
            pl.BlockSpec((1, 2, 1, LANES, LANES), lambda s, p: (s, 0, p, 0, 0)),
            pl.BlockSpec(memory_space=pl.ANY),
        ],
        out_specs=pl.BlockSpec((DEC_SEQ, LANES), lambda s, p: (rb + s, p)),
        out_shape=jax.ShapeDtypeStruct((N_TOK, D_MODEL), F32),
        input_output_aliases={8: 0},
        scratch_shapes=scratch(DEC_SEQ),
        compiler_params=_params("arbitrary", "arbitrary"),
        name="rwkv_latent",
    )(rkvg, lw, a, *pars, s0_pairs, o_p)
    return o, st


def _split3(x):
    hi = x.astype(BF16)
    r1 = x - hi.astype(F32)
    mid = r1.astype(BF16)
    return hi, mid, (r1 - mid.astype(F32)).astype(BF16)


def _rwkv_kernel(*refs, seq, zero_init, pg, np2):
    n_in = 7 if zero_init else 9
    rkv_ref, lw_ref, a_ref, kkp_ref, kap_ref, rkp_ref, gn_ref = refs[:7]
    if zero_init:
        o_ref, st_ref = refs[n_in:n_in + 2]
        scr = refs[n_in + 2:]
    else:
        s0_ref = refs[7]
        o_ref = refs[n_in]
        scr = refs[n_in + 1:]
    kk_scr, cum_scr, bon_scr, y_scr, s_scr, tar_scr, lrb_scr, b2_scr, w2_scr, yl_scr, kv_scr, pc_scr = scr
    c_len = RW_C
    n_ch = seq // c_len
    rows2 = 2 * c_len
    grp = pl.program_id(1)
    defer = np2 > pg
    base = grp * pg if defer else 0
    first = _first_half_lanes()

    rr = lax.broadcasted_iota(jnp.int32, (rows2, rows2), 0)
    cc = lax.broadcasted_iota(jnp.int32, (rows2, rows2), 1)
    eye = (rr == cc).astype(F32)

    def same(shift):
        return (rr >> shift) == (cc >> shift)

    head = same(6)
    strict = (head & (cc < rr), head & (cc > rr))
    incl = (head & (cc <= rr), head & (cc >= rr))
    last = (c_len - 1, 0)
    head_ones = head.astype(BF16)

    cs_rows = min(seq, 256)
    tr = lax.broadcasted_iota(jnp.int32, (cs_rows, cs_rows), 0)
    tc = lax.broadcasted_iota(jnp.int32, (cs_rows, cs_rows), 1)
    chunk = (tr >> 6) == (tc >> 6)
    tri = ((chunk & (tc <= tr)).astype(BF16), (chunk & (tc >= tr)).astype(BF16))
    for p in range(pg):
        ln = slice(p * LANES, (p + 1) * LANES)
        kap = kap_ref[:, ln]
        r = rkv_ref[0, :, ln]
        k = rkv_ref[1, :, ln]
        v = rkv_ref[2, :, ln]
        kk = k * kkp_ref[:, ln]
        kk_scr[p] = kk * lax.rsqrt(jnp.maximum(_head_sum(kk * kk, first), 1e-12))
        bonus = None
        for dr in range(2):
            kd = k * (1.0 + (a_ref[dr, :, ln] - 1.0) * kap)
            term = _head_sum(r * kd * rkp_ref[:, ln], first) * v
            bonus = term if bonus is None else bonus + term
        bon_scr[base + p] = bonus
    for p in range(0, pg, 2):
        for dr in range(2):
            for r0 in range(0, seq, cs_rows):
                parts = _split3(lw_ref[dr, r0:r0 + cs_rows, p * LANES:(p + 2) * LANES])
                cum = _dot(tri[dr], parts[0]) + _dot(tri[dr], parts[1]) + _dot(tri[dr], parts[2])
                cum_scr[p, dr, r0:r0 + cs_rows, :] = cum[:, :LANES]
                cum_scr[p + 1, dr, r0:r0 + cs_rows, :] = cum[:, LANES:]

    def phase1(chains):
        dirs = [dr for _, dr, _ in chains]
        a2, r2, b2, k2, v2, pc = [], [], [], [], [], []
        for p, dr, c in chains:
            ln = slice(p * LANES, (p + 1) * LANES)
            rw = pl.ds(pl.multiple_of(c * c_len, c_len), c_len)
            a = a_ref[dr, rw, ln]
            k = rkv_ref[1, rw, ln]
            kk_c = kk_scr[p, rw, :]
            cum_c = cum_scr[p, dr, rw, :]
            e_inc = jnp.exp(cum_c)
            e_inv = jnp.exp(-cum_c)
            a2.append(_stack_heads(-kk_c * jnp.exp(cum_c - lw_ref[dr, rw, ln]), first).astype(BF16))
            r2.append(_stack_heads(rkv_ref[0, rw, ln] * e_inc, first).astype(BF16))
            b2.append(_stack_heads(kk_c * a * e_inv, first).astype(BF16))
            k2.append(_stack_heads(k * (1.0 + (a - 1.0) * kap_ref[:, ln]) * e_inv, first).astype(BF16))
            v2.append(_stack_heads(rkv_ref[2, rw, ln], first).astype(BF16))
            pc.append(e_inc[last[dr]:last[dr] + 1, :])
        g = [_dot_nt(jnp.concatenate([x, y], axis=0), jnp.concatenate([z, w], axis=0))
             for x, y, z, w in zip(a2, r2, b2, k2)]
        l_ab = [jnp.where(strict[dr], x[:rows2, :rows2], 0.0) for dr, x in zip(dirs, g)]
        t = [eye + jnp.where(same(1), x, 0.0) for x in l_ab]
        side = {}
        for shift in range(1, 6):
            sib = same(shift + 1) & ~same(shift)
            tb = [x.astype(BF16) for x in t]
            mid = [_dot(jnp.where(sib, x, 0.0).astype(BF16), y) for x, y in zip(l_ab, tb)]
            if shift == 1:
                side['lv'] = [_dot(jnp.where(strict[dr], x[:rows2, rows2:], 0.0).astype(BF16), y)
                              for dr, x, y in zip(dirs, g, v2)]
            elif shift == 2:
                side['yl'] = [_dot(jnp.where(incl[dr], x[rows2:, rows2:], 0.0).astype(BF16), y)
                              for dr, x, y in zip(dirs, g, v2)]
            elif shift == 3:
                side['kv'] = [_dot_tn(x, y) for x, y in zip(v2, k2)]
            t = [x + _dot(y, z.astype(BF16)) for x, y, z in zip(t, tb, mid)]
        tb = [x.astype(BF16) for x in t]
        ta = [_dot(x, y) for x, y in zip(tb, a2)]
        w2 = [_dot(x, y.astype(BF16)) for x, y in zip(tb, side['lv'])]
        for i, (p, dr, c) in enumerate(chains):
            n = ((base + p) * 2 + dr) * n_ch + c
            tar_scr[n, :rows2, :] = ta[i].astype(BF16)
            tar_scr[n, rows2:, :] = r2[i]
            w2_scr[n] = w2[i]
            yl_scr[n] = side['yl'][i]
            kv_scr[n] = side['kv'][i]
            lrb_scr[n] = jnp.where(incl[dr], g[i][rows2:, :rows2], 0.0).astype(BF16)
            b2_scr[n] = b2[i]
            pc_scr[n] = pc[i]

    def body1(cg, carry):
        phase1([(p, dr, cg * RW_LOCK + j) for p in range(pg) for j in range(RW_LOCK) for dr in range(2)])
        return carry

    if n_ch == RW_LOCK:
        body1(0, 0)
    else:
        lax.fori_loop(0, n_ch // RW_LOCK, body1, 0)

    def finish():
        for p in range(np2):
            for dr in range(2):
                if zero_init:
                    s_scr[2 * p + dr] = jnp.zeros((rows2, LANES), F32)
                else:
                    s_scr[2 * p + dr] = s0_ref[0, dr, p]

        def body2(i, carry):
            cs = (i, n_ch - 1 - i)
            ids = [(p, dr) for p in range(np2) for dr in range(2)]
            ns = [(p * 2 + dr) * n_ch + cs[dr] for p, dr in ids]
            x = [_dot_nt(tar_scr[n], s_scr[2 * p + dr].astype(BF16)) for n, (p, dr) in zip(ns, ids)]
            u2 = [(xx[:rows2] + w2_scr[n]).astype(BF16) for n, xx in zip(ns, x)]
            upd = [_dot_tn(u, b2_scr[n]) for n, u in zip(ns, u2)]
            yb = [_dot(lrb_scr[n], u) for n, u in zip(ns, u2)]
            for j, (n, (p, dr)) in enumerate(zip(ns, ids)):
                y2 = x[j][rows2:] + yb[j] + yl_scr[n]
                y_scr[p, dr, pl.ds(pl.multiple_of(cs[dr] * c_len, c_len), c_len), :] = y2[:c_len] + y2[c_len:]
                s_scr[2 * p + dr] = (s_scr[2 * p + dr] + upd[j] + kv_scr[n]) * pc_scr[n]
            return carry

        lax.fori_loop(0, n_ch, body2, 0)

        def head_mean(xs):
            parts = [_split3(x) for x in xs]
            return [(_dot(a, head_ones) + _dot(b, head_ones) + _dot(c, head_ones)) * (1.0 / RWKV_HD)
                    for a, b, c in parts]

        ys = [y_scr[p, 0] + y_scr[p, 1] for p in range(np2)]
        yc = [y - m for y, m in zip(ys, head_mean(ys))]
        var = head_mean([x * x for x in yc])
        for p in range(np2):
            ln = slice(p * LANES, (p + 1) * LANES)
            o_ref[:, ln] = yc[p] * lax.rsqrt(var[p] + GN_EPS) * gn_ref[:, ln] + bon_scr[p]
            if zero_init:
                for dr in range(2):
                    s2 = s_scr[2 * p + dr]
                    st_ref[0, 0, dr, 2 * p] = s2[:RWKV_HD, :RWKV_HD]
                    st_ref[0, 0, dr, 2 * p + 1] = s2[RWKV_HD:, RWKV_HD:]

    if defer:
        pl.when(grp == pl.num_programs(1) - 1)(finish)
    else:
        finish()


def _rwkv_mixer(rkvg, lw, a, kkp, kap, rkp, gn, s0_pairs):
    npair = RWKV_HEADS // 2
    r2 = 2 * RW_C

    def scratch(seq, pg, np2):
        n = 2 * np2 * (seq // RW_C)
        return [
            pltpu.VMEM((pg, seq, LANES), F32), pltpu.VMEM((pg, 2, seq, LANES), F32),
            pltpu.VMEM((np2, seq, LANES), F32), pltpu.VMEM((np2, 2, seq, LANES), F32),
            pltpu.VMEM((2 * np2, r2, LANES), F32),
            pltpu.VMEM((n, 2 * r2, LANES), BF16), pltpu.VMEM((n, r2, r2), BF16), pltpu.VMEM((n, r2, LANES), BF16),
            pltpu.VMEM((n, r2, LANES), F32), pltpu.VMEM((n, r2, LANES), F32), pltpu.VMEM((n, r2, LANES), F32),
            pltpu.VMEM((n, 1, LANES), F32),
        ]

    def seq_specs(seq, rb, pg, gn_spec):
        wl = pg * LANES
        par =pl.BlockSpec((1, wl), lambda s, g: (0, g))
        return [
            pl.BlockSpec((3, seq, wl), lambda s, g: (0, rb + s, g)),
            pl.BlockSpec((2, seq, wl), lambda s, g: (0, rb + s, g)),
            pl.BlockSpec((2, seq, wl), lambda s, g: (0, rb + s, g)),
            par, par, par, gn_spec,
        ]

    pars = [u.reshape(1, D_MODEL) for u in (kkp, kap, rkp, gn)]
    pg = RW_PAIRS_PROMPT
    o_p, st = pl.pallas_call(
        functools.partial(_rwkv_kernel, seq=SEQ, zero_init=True, pg=pg, np2=npair),
        grid=(BATCH, npair // pg),
        in_specs=seq_specs(SEQ, 0, pg, pl.BlockSpec((1, D_MODEL), lambda s, g: (0, 0))),
        out_specs=[
            pl.BlockSpec((SEQ, D_MODEL), lambda s, g: (s, 0)),
            pl.BlockSpec((1, 1, 2, RWKV_HEADS, RWKV_HD, RWKV_HD), lambda s, g: (s, 0, 0, 0, 0, 0)),
        ],
        out_shape=[
            jax.ShapeDtypeStruct((N_TOK, D_MODEL), F32),
            jax.ShapeDtypeStruct((BATCH, 1, 2, RWKV_HEADS, RWKV_HD, RWKV_HD), F32),
        ],
        scratch_shapes=scratch(SEQ, pg, npair),
        compiler_params=_params("arbitrary", "arbitrary"),
        name="rwkv_prompt",
    )(rkvg, lw, a, *pars)
    rb = N_PROMPT_TOK // DEC_SEQ
    pg = RW_PAIRS
    wl = pg * LANES
    o = pl.pallas_call(
        functools.partial(_rwkv_kernel, seq=DEC_SEQ, zero_init=False, pg=pg, np2=pg),
        grid=(DEC_BATCH, npair // pg),
        in_specs=seq_specs(DEC_SEQ, rb, pg, pl.BlockSpec((1, wl), lambda s, g: (0, g))) + [
            pl.BlockSpec((1, 2, pg, LANES, LANES), lambda s, g: (s, 0, g, 0, 0)),
            pl.BlockSpec(memory_space=pl.ANY),
        ],
        out_specs=pl.BlockSpec((DEC_SEQ, wl), lambda s, g: (rb + s, g)),
        out_shape=jax.ShapeDtypeStruct((N_TOK, D_MODEL), F32),
        input_output_aliases={8: 0},
        scratch_shapes=scratch(DEC_SEQ, pg, pg),
        compiler_params=_params("arbitrary", "arbitrary"),
        name="rwkv_latent",
    )(rkvg, lw, a, *pars, s0_pairs, o_p)
    return o, st


def _state_pairs(s0):
    s = s0.reshape(DEC_BATCH, 2, RWKV_HEADS // 2, 2, RWKV_HD, RWKV_HD)
    z = jnp.zeros_like(s[:, :, :, 0])
    top = jnp.concatenate([s[:, :, :, 0], z], axis=-1)
    bot = jnp.concatenate([z, s[:, :, :, 1]], axis=-1)
    return jnp.concatenate([top, bot], axis=-2)


def _layer_rwkv(x, p, mod, j):
    i = N_MIXERS * j + 1
    wa, wb, aa, ab = p['rwkv_wA'][j], p['rwkv_wB'][j], p['rwkv_aA'][j], p['rwkv_aB'][j]
    z = jnp.zeros_like(wb[0])
    wa2 = jnp.concatenate([wa[0], wa[1]], axis=1).astype(BF16)
    aa2 = jnp.concatenate([aa[0], aa[1]], axis=1).astype(BF16)
    wb_pad = jnp.stack([jnp.concatenate([wb[0], z]), jnp.concatenate([z, wb[1]])]).astype(BF16)
    ab_pad = jnp.stack([jnp.concatenate([ab[0], z]), jnp.concatenate([z, ab[1]])]).astype(BF16)
    xm, lw, a = _rwkv_prep(x, p['norm_w'][i], mod, p['rwkv_mu'][j], wa2, aa2, wb_pad, ab_pad,
                           p['rwkv_w0'][j], p['rwkv_a0'][j])
    rkvg = _rwkv_rkvg(xm, p['rwkv_w_in'][j].astype(BF16))
    o, st = _rwkv_mixer(rkvg, lw, a, p['rwkv_kk'][j], p['rwkv_ka'][j], p['rwkv_rk'][j], p['rwkv_gn'][j],
                          _state_pairs(p['state_rwkv'][:, j]))
    x = _out_proj(o, rkvg.reshape(4 * N_TOK, D_MODEL), 0, p['rwkv_w_out'][j], x, mod, p['final_norm_w'], False,
                  g_row0=3 * N_TOK)
    return x, st


DIFF_W = 2 * DIFF_HD
ATT_QB = 256
DIFF_GROUP = 4


def _first_half_lanes():
    return lax.broadcasted_iota(jnp.int32, (1, LANES), 1) < LANES // 2


def _diff_lambda(lam_ref, lam_init):
    lp = lam_ref[...]
    return (jnp.exp(jnp.sum(lp[0:1] * lp[1:2], keepdims=True))
            - jnp.exp(jnp.sum(lp[2:3] * lp[3:4], keepdims=True)) + lam_init)


def _diff_heads(items, lam, lam_init):
    first = _first_half_lanes()
    scale = DIFF_HD ** -0.5
    sub = [(q, keys, comp) for q, keys, _ in items for comp in range(2)]
    qm = [(jnp.where(first if comp == 0 else ~first, q, 0.0) * scale).astype(BF16) for q, _, comp in sub]
    s = [[_dot_nt(x, kb) for kb, _ in keys] for x, (_, keys, _) in zip(qm, sub)]
    m = [functools.reduce(jnp.maximum, [jnp.max(u, axis=-1, keepdims=True) for u in ss]) for ss in s]
    e = [[jnp.exp(u - mm) for u in ss] for ss, mm in zip(s, m)]
    inv = [1.0 / functools.reduce(lambda x, y: x + y, [jnp.sum(u, axis=-1, keepdims=True) for u in ee]) for ee in e]
    outs = []
    for i, (_, keys, gn) in enumerate(items):
        o = None
        lam_inv = lam * inv[2 * i + 1]
        for n, (_, vb) in enumerate(keys):
            p = e[2 * i][n] * inv[2 * i] - e[2 * i + 1][n] * lam_inv
            part = _dot(p.astype(BF16), vb)
            o = part if o is None else o + part
        outs.append(o)
    return [o * lax.rsqrt(jnp.mean(o * o, axis=-1, keepdims=True) + EPS) * gn * (1.0 - lam_init)
            for o, (_, _, gn) in zip(outs, items)]


def _diff_prompt_kernel(lam_ref, q_ref, k_ref, v_ref, gn_ref, o_ref, *, lam_init):
    lam = _diff_lambda(lam_ref, lam_init)
    for h0 in range(0, DIFF_HEADS, DIFF_GROUP):
        items = []
        for h in range(h0, h0 + DIFF_GROUP):
            sl = slice(h * DIFF_W, (h + 1) * DIFF_W)
            items.append((q_ref[:, sl], [(k_ref[:, sl].astype(BF16), v_ref[:, sl].astype(BF16))], gn_ref[:, sl]))
        for h, o in zip(range(h0, h0 + DIFF_GROUP), _diff_heads(items, lam, lam_init)):
            o_ref[:, h * DIFF_W:(h + 1) * DIFF_W] = o


def _diff_latent_kernel(lam_ref, q_ref, k_ref, v_ref, ck_ref, cv_ref, cos_ref, slo_ref, shi_ref, gn_ref,
                        _prev_ref, o_ref, *, lam_init):
    lam = _diff_lambda(lam_ref, lam_init)
    tabs = (cos_ref[...], slo_ref[...], shi_ref[...])
    q = _rope(q_ref[...].astype(F32), *tabs, DIFF_HD // 4)
    k = _rope(k_ref[...].astype(F32), *tabs, DIFF_HD // 4)
    keys = [(k.astype(BF16), v_ref[...].astype(BF16)),
            (ck_ref[0, 0, 0].astype(BF16), cv_ref[0, 0, 0].astype(BF16))]
    gn = gn_ref[...]
    n_blk = DEC_SEQ // ATT_QB
    items = [(q[qi * ATT_QB:(qi + 1) * ATT_QB], keys, gn) for qi in range(n_blk)]
    for qi, o in enumerate(_diff_heads(items, lam, lam_init)):
        o_ref[qi * ATT_QB:(qi + 1) * ATT_QB, :] = o


def _diff_attention(proj, lam_p, gn_w, cache_k, cache_v, j, lam_init):
    gn = gn_w.reshape(1, D_MODEL)
    lam_spec = pl.BlockSpec((4, DIFF_HD), lambda *_: (0, 0))
    o_p = pl.pallas_call(
        functools.partial(_diff_prompt_kernel, lam_init=lam_init),
        grid=(BATCH,),
        in_specs=[
            lam_spec,
            pl.BlockSpec((SEQ, D_MODEL), lambda b: (b, 0)),
            pl.BlockSpec((SEQ, D_MODEL), lambda b: (b, 1)),
            pl.BlockSpec((SEQ, D_MODEL), lambda b: (b, 2)),
            pl.BlockSpec((1, D_MODEL), lambda b: (0, 0)),
        ],
        out_specs=pl.BlockSpec((SEQ, D_MODEL), lambda b: (b, 0)),
        out_shape=jax.ShapeDtypeStruct((N_TOK, D_MODEL), F32),
        compiler_params=_params("arbitrary"),
        name="diff_prompt",
    )(lam_p, proj, proj, proj, gn)
    cos, slo, shi = (jnp.concatenate([u, u], axis=-1) for u in _rope_tables(DIFF_HD))
    rb = N_PROMPT_TOK // DEC_SEQ
    nh = DIFF_HEADS
    tab = pl.BlockSpec((DEC_SEQ, DIFF_W), lambda b, h: (0, 0))
    cache = pl.BlockSpec((1, 1, 1, PAST_LEN, DIFF_W), lambda b, h: (b, j, h, 0, 0))
    o = pl.pallas_call(
        functools.partial(_diff_latent_kernel, lam_init=lam_init),
        grid=(DEC_BATCH, nh),
        in_specs=[
            lam_spec,
            pl.BlockSpec((DEC_SEQ, DIFF_W), lambda b, h: (rb + b, h)),
            pl.BlockSpec((DEC_SEQ, DIFF_W), lambda b, h: (rb + b, nh + h)),
            pl.BlockSpec((DEC_SEQ, DIFF_W), lambda b, h: (rb + b, 2 * nh + h)),
            cache, cache, tab, tab, tab,
            pl.BlockSpec((1, DIFF_W), lambda b, h: (0, h)),
            pl.BlockSpec(memory_space=pl.ANY),
        ],
        out_specs=pl.BlockSpec((DEC_SEQ, DIFF_W), lambda b, h: (rb + b, h)),
        out_shape=jax.ShapeDtypeStruct((N_TOK, D_MODEL), F32),
        input_output_aliases={10: 0},
        compiler_params=_params("arbitrary", "arbitrary"),
        name="diff_latent",
    )(lam_p, proj, proj, proj, cache_k, cache_v, cos, slo, shi, gn, o_p)
    return o


def _layer_diff(x, p, mod, j, i):
    lam_init = 0.8 - 0.6 * math.exp(-0.3 * i)
    proj, new_k, new_v = _in_proj_kv(x, p['norm_w'][i], mod, p['diff_w_in'][j], DIFF_HEADS)
    o = _diff_attention(proj, p['diff_lambda'][j], p['diff_gn'][j], p['cache_diff_k'], p['cache_diff_v'], j,
                        lam_init)
    x = _out_proj(o, proj, 3, p['diff_w_out'][j], x, mod, p['final_norm_w'], False)
    return x, new_k, new_v


NA_ROWS = DEC_SEQ // GRID_W
NA_WR = min(NA_WIN_R, NA_ROWS)
NA_LOC = NA_WR * GRID_W
NA_ROW_GROUP = 4
NA_PAIR_GROUP = 2


def _na_prompt_kernel(q_ref, k_ref, v_ref, o_ref):
    first = _first_half_lanes()
    scale = NA_HD ** -0.5
    for p0 in range(0, NA_HEADS // 2, NA_PAIR_GROUP):
        pairs = range(p0, p0 + NA_PAIR_GROUP)
        kb = [k_ref[:, pr * LANES:(pr + 1) * LANES].astype(BF16) for pr in pairs]
        vb = [v_ref[:, pr * LANES:(pr + 1) * LANES].astype(BF16) for pr in pairs]
        items = [(i, half) for i in range(NA_PAIR_GROUP) for half in range(2)]
        qm = [(jnp.where(first if half == 0 else ~first, q_ref[:, (p0 + i) * LANES:(p0 + i + 1) * LANES], 0.0)
               * scale).astype(BF16) for i, half in items]
        s = [_dot_nt(x, kb[i]) for x, (i, _) in zip(qm, items)]
        e = [jnp.exp(x - jnp.max(x, axis=-1, keepdims=True)) for x in s]
        inv = [1.0 / jnp.sum(x, axis=-1, keepdims=True) for x in e]
        outs = [_dot(x.astype(BF16), vb[i]) * z for x, z, (i, _) in zip(e, inv, items)]
        for i in range(NA_PAIR_GROUP):
            o_ref[:, (p0 + i) * LANES:(p0 + i + 1) * LANES] = jnp.where(first, outs[2 * i], outs[2 * i + 1])


def _na_latent_kernel(q_ref, k_ref, v_ref, kc_ref, vc_ref, tab_ref, _prev_ref, o_ref):
    first = _first_half_lanes()
    scale = NA_HD ** -0.5
    kb = k_ref[...].astype(BF16)
    vb = v_ref[...].astype(BF16)
    kcb = kc_ref[0, 0].astype(BF16)
    vcb = vc_ref[0, 0].astype(BF16)
    qcol = lax.broadcasted_iota(jnp.int32, (GRID_W, NA_LOC), 0)
    kcol = lax.broadcasted_iota(jnp.int32, (GRID_W, NA_LOC), 1) & (GRID_W - 1)
    cstart = jnp.clip(qcol - NA_WIN_C // 2, 0, GRID_W - NA_WIN_C)
    col_ok = (kcol >= cstart) & (kcol < cstart + NA_WIN_C)
    def bias_of(r, rs, half):
        parts = []
        for w in range(0, NA_WR, 2):
            src = jnp.broadcast_to(tab_ref[half, rs + w - r + NA_WIN_R - 1], (GRID_W, LANES))
            parts.append(pltpu.roll(src, LANES - (NA_WIN_C - 1), axis=1, stride=1, stride_axis=0))
        return jnp.concatenate(parts, axis=1)

    for r0 in range(0, NA_ROWS, NA_ROW_GROUP):
        items = [(r, min(max(r - NA_WR // 2, 0), NA_ROWS - NA_WR), half)
                 for r in range(r0, r0 + NA_ROW_GROUP) for half in range(2)]
        qm = [(jnp.where(first if half == 0 else ~first, q_ref[r * GRID_W:(r + 1) * GRID_W, :], 0.0)
               * scale).astype(BF16) for r, _, half in items]
        s_loc = [_dot_nt(x, kb[rs * GRID_W:(rs + NA_WR) * GRID_W]) for x, (_, rs, _) in zip(qm, items)]
        s_ctx = [_dot_nt(x, kcb) for x in qm]
        s_loc = [jnp.where(col_ok, x + bias_of(*it), -jnp.inf) for x, it in zip(s_loc, items)]
        m = [jnp.maximum(jnp.max(x, axis=-1, keepdims=True), jnp.max(y, axis=-1, keepdims=True))
             for x, y in zip(s_loc, s_ctx)]
        e_loc = [jnp.exp(x - mm) for x, mm in zip(s_loc, m)]
        e_ctx = [jnp.exp(x - mm) for x, mm in zip(s_ctx, m)]
        inv = [1.0 / (jnp.sum(x, axis=-1, keepdims=True) + jnp.sum(y, axis=-1, keepdims=True))
               for x, y in zip(e_loc, e_ctx)]
        pv = [_dot(x.astype(BF16), vb[rs * GRID_W:(rs + NA_WR) * GRID_W]) for x, (_, rs, _) in zip(e_loc, items)]
        pc = [_dot(x.astype(BF16), vcb) for x in e_ctx]
        outs = [(x + y) * z for x, y, z in zip(pv, pc, inv)]
        for n in range(0, len(items), 2):
            r = items[n][0]
            o_ref[r * GRID_W:(r + 1) * GRID_W, :] = jnp.where(first, outs[n], outs[n + 1])


def _na_bias_pairs(table):
    t = table.astype(F32)
    nc = 2 * NA_WIN_C - 1
    z = jnp.zeros(t[:, :-1].shape[:2] + (GRID_W - nc,), F32)
    return jnp.concatenate([t[:, :-1], z, t[:, 1:], z], axis=-1)[:, :, None, :]


def _pair_heads(cache):
    c = cache.reshape(DEC_BATCH, NA_HEADS // 2, 2, PAST_LEN, NA_HD)
    return c.transpose(0, 1, 3, 2, 4).reshape(DEC_BATCH, NA_HEADS // 2, PAST_LEN, LANES)


def _na_attention(proj, bias_table, cache_k, cache_v):
    o_p = pl.pallas_call(
        _na_prompt_kernel,
        grid=(BATCH,),
        in_specs=[
            pl.BlockSpec((SEQ, D_MODEL), lambda b: (b, 0)),
            pl.BlockSpec((SEQ, D_MODEL), lambda b: (b, 1)),
            pl.BlockSpec((SEQ, D_MODEL), lambda b: (b, 2)),
        ],
        out_specs=pl.BlockSpec((SEQ, D_MODEL), lambda b: (b, 0)),
        out_shape=jax.ShapeDtypeStruct((N_TOK, D_MODEL), F32),
        compiler_params=_params("arbitrary"),
        name="na_prompt",
    )(proj, proj, proj)
    rb = N_PROMPT_TOK // DEC_SEQ
    npair = NA_HEADS // 2
    cache = pl.BlockSpec((1, 1, PAST_LEN, LANES), lambda pr, b: (b, pr, 0, 0))
    o = pl.pallas_call(
        _na_latent_kernel,
        grid=(npair, DEC_BATCH),
        in_specs=[
            pl.BlockSpec((DEC_SEQ, LANES), lambda pr, b: (rb + b, pr)),
            pl.BlockSpec((DEC_SEQ, LANES), lambda pr, b: (rb + b, npair + pr)),
            pl.BlockSpec((DEC_SEQ, LANES), lambda pr, b: (rb + b, 2 * npair + pr)),
            cache, cache,
            pl.BlockSpec((2, 2 * NA_WIN_R - 2, 1, LANES), lambda pr, b: (pr, 0, 0, 0)),
            pl.BlockSpec(memory_space=pl.ANY),
        ],
        out_specs=pl.BlockSpec((DEC_SEQ, LANES), lambda pr, b: (rb + b, pr)),
        out_shape=jax.ShapeDtypeStruct((N_TOK, D_MODEL), F32),
        input_output_aliases={6: 0},
        compiler_params=_params("arbitrary", "arbitrary"),
        name="na_latent",
    )(proj, proj, proj, _pair_heads(cache_k), _pair_heads(cache_v), _na_bias_pairs(bias_table), o_p)
    return o


def _layer_na(x, p, mod, j, final):
    i = N_MIXERS * j + 3
    proj, new_k, new_v = _in_proj_kv(x, p['norm_w'][i], mod, p['na_w_in'][j], NA_HEADS)
    o = _na_attention(proj, p['na_bias'][j], p['cache_na_k'][:, j], p['cache_na_v'][:, j])
    args = (o, proj, 3, p['na_w_out'][j], x, mod, p['final_norm_w'])
    if final:
        x = (_out_proj(*args, True, rows=(0, N_PROMPT_TOK)), _out_proj(*args, True, rows=(N_PROMPT_TOK, N_TOK)))
    else:
        x = _out_proj(*args, False)
    return x, new_k, new_v


def kernel(x_prompt, x_sample, state_ret, state_rwkv, cache_diff_k, cache_diff_v, cache_na_k, cache_na_v,
           c, c_ctx, norm_w, w_mod, b_mod, final_norm_w,
           ret_w_in, ret_decay, ret_gn, ret_w_out,
           rwkv_mu, rwkv_w_in, rwkv_w0, rwkv_wA, rwkv_wB, rwkv_a0, rwkv_aA, rwkv_aB,
           rwkv_kk, rwkv_ka, rwkv_rk, rwkv_gn, rwkv_w_out,
           diff_w_in, diff_lambda, diff_gn, diff_w_out,
           na_w_in, na_bias, na_w_out):
    p = dict(locals())
    cond = jnp.zeros((N_COND, D_MODEL), F32).at[0].set(c_ctx).at[1:1 + DEC_BATCH].set(c)
    mods = _modulation(cond, w_mod, b_mod)
    x = (x_prompt.reshape(N_PROMPT_TOK, D_MODEL), x_sample.reshape(N_SAMPLE_TOK, D_MODEL))
    new = {n: [] for n in ('ret', 'rwkv', 'dk', 'dv', 'nk', 'nv')}
    for i in range(DEPTH):
        kind, j = i % N_MIXERS, i // N_MIXERS
        if kind == 0:
            x, st = _layer_ret(x, p, mods[i], j)
            new['ret'].append(st)
        elif kind == 1:
            x, st = _layer_rwkv(x, p, mods[i], j)
            new['rwkv'].append(st)
        elif kind == 2:
            x, ck, cv = _layer_diff(x, p, mods[i], j, i)
            new['dk'].append(ck)
            new['dv'].append(cv)
        else:
            x, ck, cv = _layer_na(x, p, mods[i], j, final=(i == DEPTH - 1))
            new['nk'].append(ck)
            new['nv'].append(cv)
    if DEPTH % N_MIXERS:
        raise NotImplementedError("the final norm is fused into the last neighbourhood-attention layer")
    cat = lambda xs: xs[0] if len(xs) == 1 else jnp.concatenate(xs, axis=1)
    return (x[0].reshape(BATCH, SEQ, D_MODEL), x[1].reshape(DEC_BATCH, DEC_SEQ, D_MODEL),
            cat(new['ret']), cat(new['rwkv']), cat(new['dk']), cat(new['dv']), cat(new['nk']), cat(new['nv']))
```

```python
import functools
import math

import jax
import jax.numpy as jnp
from jax import lax
from jax.experimental import pallas as pl
from jax.experimental.pallas import tpu as pltpu

F32 = jnp.float32
BF16 = jnp.bfloat16

D_MODEL = 1024
BATCH = 32
SEQ = 256
DEPTH = 4
N_MIXERS = 4
DEC_BATCH = 2
DEC_SEQ = 1024
PAST_LEN = 256
GRID_W = 64

RET_HEADS = 4
RET_DK = 256
RET_DV = 512
RET_QK = 1024
RET_V = 2048

RWKV_HD = 64
RWKV_HEADS = 16
RWKV_RANK = 64

DIFF_HEADS = 8
DIFF_HD = 64

NA_HEADS = 16
NA_HD = 64
NA_WIN_R = 8
NA_WIN_C = 16

ROPE_BASE = 10000.0
EPS = 1e-6
GN_EPS = 1e-5

N_PROMPT_TOK = BATCH * SEQ
N_SAMPLE_TOK = DEC_BATCH * DEC_SEQ
N_TOK = N_PROMPT_TOK + N_SAMPLE_TOK
N_COND = 8

LANES = 128
VMEM_LIMIT = 56 * 2 ** 20


def _params(*sem):
    return pltpu.CompilerParams(dimension_semantics=sem, vmem_limit_bytes=VMEM_LIMIT)


def _cond_of_tile(i, tm):
    npt = N_PROMPT_TOK // tm
    return jnp.where(i < npt, 0, 1 + (i - npt) // (DEC_SEQ // tm))


def _sigmoid(x):
    return 1.0 / (1.0 + jnp.exp(-x))


def _silu(x):
    return x * _sigmoid(x)


def _dot(a, b):
    return jnp.dot(a, b, preferred_element_type=F32)


def _dot_nt(a, b):
    return lax.dot_general(a, b, (((1,), (1,)), ((), ())), preferred_element_type=F32)


def _dot_tn(a, b):
    return lax.dot_general(a, b, (((0,), (0,)), ((), ())), preferred_element_type=F32)


def _mod_kernel(c_ref, w_ref, b_ref, o_ref):
    s = _silu(c_ref[...])
    o_ref[0] = jnp.dot(s, w_ref[0], precision=lax.Precision.HIGHEST, preferred_element_type=F32) + b_ref[0]


def _modulation(cond, w_mod, b_mod):
    tn = D_MODEL
    out = pl.pallas_call(
        _mod_kernel,
        grid=(DEPTH, 3 * D_MODEL // tn),
        in_specs=[
            pl.BlockSpec((N_COND, D_MODEL), lambda l, j: (0, 0)),
            pl.BlockSpec((1, D_MODEL, tn), lambda l, j: (l, 0, j)),
            pl.BlockSpec((1, 1, tn), lambda l, j: (l, 0, j)),
        ],
        out_specs=pl.BlockSpec((1, N_COND, tn), lambda l, j: (l, 0, j)),
        out_shape=jax.ShapeDtypeStruct((DEPTH, N_COND, 3 * D_MODEL), F32),
        compiler_params=_params("arbitrary", "arbitrary"),
        name="modulation",
    )(cond, w_mod, b_mod.reshape(DEPTH, 1, 3 * D_MODEL))
    return out.reshape(DEPTH, N_COND, 3, 1, D_MODEL)


def _norm_mod(x, nw, mod_ref):
    ms = jnp.mean(x * x, axis=-1, keepdims=True)
    y = x * lax.rsqrt(ms + EPS) * nw
    return y * (1.0 + mod_ref[0, 1]) + mod_ref[0, 0]


IN_TM = 1024
IN_TN = 2048


def _x_specs(x, tm, tile_of):
    if not isinstance(x, tuple):
        return [pl.BlockSpec((tm, D_MODEL), lambda *g: (tile_of(*g), 0))], (x,)
    npt = N_PROMPT_TOK // tm
    return [pl.BlockSpec((tm, D_MODEL), lambda *g: (jnp.minimum(tile_of(*g), npt - 1), 0)),
            pl.BlockSpec((tm, D_MODEL), lambda *g: (jnp.maximum(tile_of(*g) - npt, 0), 0))], x


def _read_x(x_refs, tile, tm):
    if len(x_refs) == 1:
        return x_refs[0][...]
    return jnp.where(tile < N_PROMPT_TOK // tm, x_refs[0][...], x_refs[1][...])


def _in_proj_kernel(*refs, n_x):
    x_refs = refs[:n_x]
    nw_ref, mod_ref, w_ref, o_ref, h_ref = refs[n_x:]

    @pl.when(pl.program_id(1) == 0)
    def _():
        x = _read_x(x_refs, pl.program_id(0), IN_TM)
        h_ref[...] = _norm_mod(x, nw_ref[...], mod_ref).astype(BF16)

    o_ref[...] = _dot(h_ref[...], w_ref[...]).astype(o_ref.dtype)


def _in_proj(x, norm_w, mod, w, tn, out_dtype=F32):
    n = w.shape[1]
    w = w.astype(BF16)
    x_specs, xs = _x_specs(x, IN_TM, lambda i, j: i)
    return pl.pallas_call(
        functools.partial(_in_proj_kernel, n_x=len(xs)),
        grid=(N_TOK // IN_TM, n // tn),
        in_specs=x_specs + [
            pl.BlockSpec((1, D_MODEL), lambda i, j: (0, 0)),
            pl.BlockSpec((1, 3, 1, D_MODEL), lambda i, j: (_cond_of_tile(i, IN_TM), 0, 0, 0)),
            pl.BlockSpec((D_MODEL, tn), lambda i, j: (0, j)),
        ],
        out_specs=pl.BlockSpec((IN_TM, tn), lambda i, j: (i, j)),
        out_shape=jax.ShapeDtypeStruct((N_TOK, n), out_dtype),
        scratch_shapes=[pltpu.VMEM((IN_TM, D_MODEL), BF16)],
        compiler_params=_params("arbitrary", "arbitrary"),
        name="in_proj",
    )(*xs, norm_w.reshape(1, D_MODEL), mod, w)


def _in_proj_kv_kernel(x_ref, nw_ref, mod_ref, w_ref, o_ref, ck_ref, cv_ref, h_ref, *, heads, tn):
    i = pl.program_id(0)
    j = pl.program_id(1)

    @pl.when(j == 0)
    def _():
        h_ref[...] = _norm_mod(x_ref[...], nw_ref[...], mod_ref).astype(BF16)

    acc = _dot(h_ref[...], w_ref[...])
    o_ref[...] = acc.astype(o_ref.dtype)
    hd = D_MODEL // heads
    for col, c_ref in ((D_MODEL, ck_ref), (2 * D_MODEL, cv_ref)):
        @pl.when((j == col // tn) & (i < N_PROMPT_TOK // IN_TM))
        def _(c_ref=c_ref, c0=col % tn):
            for s in range(IN_TM // SEQ):
                for h in range(heads):
                    c_ref[s, 0, h] = acc[s * SEQ:(s + 1) * SEQ, c0 + h * hd:c0 + (h + 1) * hd]


def _in_proj_kv(x, norm_w, mod, w, heads):
    n = w.shape[1]
    tn = D_MODEL
    spb = IN_TM // SEQ
    last = N_PROMPT_TOK // IN_TM - 1
    cache = pl.BlockSpec((spb, 1, heads, SEQ, D_MODEL // heads), lambda i, j: (jnp.minimum(i, last), 0, 0, 0, 0))
    cache_shape = jax.ShapeDtypeStruct((BATCH, 1, heads, SEQ, D_MODEL // heads), F32)
    return pl.pallas_call(
        functools.partial(_in_proj_kv_kernel, heads=heads, tn=tn),
        grid=(N_TOK // IN_TM, n // tn),
        in_specs=[
            pl.BlockSpec((IN_TM, D_MODEL), lambda i, j: (i, 0)),
            pl.BlockSpec((1, D_MODEL), lambda i, j: (0, 0)),
            pl.BlockSpec((1, 3, 1, D_MODEL), lambda i, j: (_cond_of_tile(i, IN_TM), 0, 0, 0)),
            pl.BlockSpec((D_MODEL, tn), lambda i, j: (0, j)),
        ],
        out_specs=[pl.BlockSpec((IN_TM, tn), lambda i, j: (i, j)), cache, cache],
        out_shape=[jax.ShapeDtypeStruct((N_TOK, n), BF16), cache_shape, cache_shape],
        scratch_shapes=[pltpu.VMEM((IN_TM, D_MODEL), BF16)],
        compiler_params=_params("arbitrary", "arbitrary"),
        name="in_proj_kv",
    )(x, norm_w.reshape(1, D_MODEL), mod, w.astype(BF16))


OUT_TM_BYTES = 4 * 2 ** 20


def _out_proj_kernel(*refs, n_x, t0, tm, final):
    x_refs = refs[:n_x]
    o_ref, g_ref, w_ref, mod_ref, fw_ref, y_ref, wb_ref = refs[n_x:]

    @pl.when(pl.program_id(0) == 0)
    def _():
        wb_ref[...] = w_ref[...].astype(BF16)

    a = (o_ref[...] * _silu(g_ref[...].astype(F32))).astype(BF16)
    xn = _read_x(x_refs, t0 + pl.program_id(0), tm) + mod_ref[0, 2] * _dot(a, wb_ref[...])
    if final:
        ms = jnp.mean(xn * xn, axis=-1, keepdims=True)
        xn = xn * lax.rsqrt(ms + EPS) * fw_ref[...]
    y_ref[...] = xn


def _out_proj(o, g_arr, g_blk, w, x, mod, final_w, final, rows=(0, N_TOK), g_row0=0):
    k = w.shape[0]
    tm = OUT_TM_BYTES // (4 * k)
    t0 = rows[0] // tm
    g0 = g_row0 // tm
    x_specs, xs = _x_specs(x, tm, lambda i: t0 + i)
    return pl.pallas_call(
        functools.partial(_out_proj_kernel, n_x=len(xs), t0=t0, tm=tm, final=final),
        grid=((rows[1] - rows[0]) // tm,),
        in_specs=x_specs + [
            pl.BlockSpec((tm, k), lambda i: (t0 + i, 0)),
            pl.BlockSpec((tm, k), lambda i: (g0 + t0 + i, g_blk)),
            pl.BlockSpec((k, D_MODEL), lambda i: (0, 0)),
            pl.BlockSpec((1, 3, 1, D_MODEL), lambda i: (_cond_of_tile(t0 + i, tm), 0, 0, 0)),
            pl.BlockSpec((1, D_MODEL), lambda i: (0, 0)),
        ],
        out_specs=pl.BlockSpec((tm, D_MODEL), lambda i: (i, 0)),
        out_shape=jax.ShapeDtypeStruct((rows[1] - rows[0], D_MODEL), F32),
        scratch_shapes=[pltpu.VMEM((k, D_MODEL), BF16)],
        compiler_params=_params("arbitrary"),
        name="out_proj",
    )(*xs, o, g_arr, w, mod, final_w.reshape(1, D_MODEL))


def _rope_tables(d):
    q = d // 4
    t = jnp.arange(DEC_SEQ)
    row = (t // GRID_W).astype(F32)
    col = (t % GRID_W).astype(F32)
    inv = ROPE_BASE ** (-jnp.arange(0, 2 * q, 2, dtype=F32) / (2 * q))
    ar = row[:, None] * inv[None, :]
    ac = col[:, None] * inv[None, :]
    z = jnp.zeros_like(ar)
    cos = jnp.concatenate([jnp.cos(ar), jnp.cos(ar), jnp.cos(ac), jnp.cos(ac)], axis=-1)
    sin_lo = jnp.concatenate([-jnp.sin(ar), z, -jnp.sin(ac), z], axis=-1)
    sin_hi = jnp.concatenate([z, jnp.sin(ar), z, jnp.sin(ac)], axis=-1)
    return cos, sin_lo, sin_hi


def _rope(x, cos, sin_lo, sin_hi, q):
    w = x.shape[-1]
    x_next = pltpu.roll(x, w - q, axis=1)
    x_prev = pltpu.roll(x, q, axis=1)
    return x * cos + x_next * sin_lo + x_prev * sin_hi


RET_QB = 256


RET_SEQS = 4


def _ret_decay(lgf, lgb, qi, seq):
    ii = lax.broadcasted_iota(jnp.int32, (RET_QB, seq), 0) + qi * RET_QB
    jj = lax.broadcasted_iota(jnp.int32, (RET_QB, seq), 1)
    gap = (ii - jj).astype(F32)
    return (jnp.where(gap >= 0, jnp.exp(lgf * jnp.maximum(gap, 0.0)), 0.0)
            + jnp.where(gap <= 0, jnp.exp(lgb * jnp.maximum(-gap, 0.0)), 0.0))


def _head_layer_norm(o, gn):
    oc = o - jnp.mean(o, axis=-1, keepdims=True)
    return oc * lax.rsqrt(jnp.mean(oc * oc, axis=-1, keepdims=True) + GN_EPS) * gn


def _ret_prompt_kernel(lg_ref, q_ref, k_ref, v_ref, gn_ref, o_ref, st_ref, dec_ref):
    h = pl.program_id(0)
    lgf = lg_ref[0, h]
    lgb = lg_ref[1, h]

    @pl.when(pl.program_id(1) == 0)
    def _():
        dec_ref[...] = _ret_decay(lgf, lgb, 0, SEQ)

    dec = dec_ref[...]
    pos = lax.broadcasted_iota(jnp.int32, (SEQ, 1), 0).astype(F32)
    w_fwd = jnp.exp(lgf * (SEQ - 1.0 - pos))
    w_bwd = jnp.exp(lgb * pos)
    rows = [slice(s * SEQ, (s + 1) * SEQ) for s in range(RET_SEQS)]
    k = [k_ref[r, :].astype(F32) * (RET_DK ** -0.5) for r in rows]
    kb = [x.astype(BF16) for x in k]
    vb = [v_ref[r, :].astype(BF16) for r in rows]
    s = [_dot_nt(q_ref[r, :].astype(BF16), y) for r, y in zip(rows, kb)]
    o = [_dot((x * dec).astype(BF16), y) for x, y in zip(s, vb)]
    s_fwd = [_dot_tn((x * w_fwd).astype(BF16), y) for x, y in zip(k, vb)]
    s_bwd = [_dot_tn((x * w_bwd).astype(BF16), y) for x, y in zip(k, vb)]
    gn = gn_ref[...]
    for i, r in enumerate(rows):
        o_ref[r, :] = _head_layer_norm(o[i], gn)
        st_ref[i, 0, 0, 0] = s_fwd[i]
        st_ref[i, 0, 1, 0] = s_bwd[i]


def _ret_latent_kernel(lg_ref, q_ref, k_ref, v_ref, gn_ref, cos_ref, slo_ref, shi_ref, s0_ref, _prev_ref, o_ref):
    seq = DEC_SEQ
    h = pl.program_id(1)
    lgf = lg_ref[0, h]
    lgb = lg_ref[1, h]
    q = _rope(q_ref[...].astype(F32), cos_ref[...], slo_ref[...], shi_ref[...], RET_DK // 4)
    k = _rope(k_ref[...].astype(F32), cos_ref[...], slo_ref[...], shi_ref[...], RET_DK // 4)
    kb = (k * (RET_DK ** -0.5)).astype(BF16)
    vb = v_ref[...].astype(BF16)
    gn = gn_ref[...]
    for qi in range(seq // RET_QB):
        qblk = q[qi * RET_QB:(qi + 1) * RET_QB]
        s = _dot_nt(qblk.astype(BF16), kb)
        o = _dot((s * _ret_decay(lgf, lgb, qi, seq)).astype(BF16), vb)
        pos = (lax.broadcasted_iota(jnp.int32, (RET_QB, 1), 0) + qi * RET_QB).astype(F32)
        qf = qblk * jnp.exp(lgf * (pos + 1.0))
        qr = qblk * jnp.exp(lgb * (seq - pos))
        o = o + _dot(qf.astype(BF16), s0_ref[0, 0, 0, 0].astype(BF16))
        o = o + _dot(qr.astype(BF16), s0_ref[0, 0, 1, 0].astype(BF16))
        o_ref[qi * RET_QB:(qi + 1) * RET_QB, :] = _head_layer_norm(o, gn)


def _retention(p, log_g, gn_w, state_ret, j):
    smem = pl.BlockSpec(memory_space=pltpu.SMEM)
    gn = gn_w.reshape(1, RET_V)
    kq = RET_QK // RET_DK
    rows = RET_SEQS * SEQ
    o_p, st = pl.pallas_call(
        _ret_prompt_kernel,
        grid=(RET_HEADS, BATCH // RET_SEQS),
        in_specs=[
            smem,
            pl.BlockSpec((rows, RET_DK), lambda h, b: (b, h)),
            pl.BlockSpec((rows, RET_DK), lambda h, b: (b, kq + h)),
            pl.BlockSpec((rows, RET_DV), lambda h, b: (b, kq + h)),
            pl.BlockSpec((1, RET_DV), lambda h, b: (0, h)),
        ],
        out_specs=[
            pl.BlockSpec((rows, RET_DV), lambda h, b: (b, h)),
            pl.BlockSpec((RET_SEQS, 1, 2, 1, RET_DK, RET_DV), lambda h, b: (b, 0, 0, h, 0, 0)),
        ],
        out_shape=[
            jax.ShapeDtypeStruct((N_TOK, RET_V), F32),
            jax.ShapeDtypeStruct((BATCH, 1, 2, RET_HEADS, RET_DK, RET_DV), F32),
        ],
        scratch_shapes=[pltpu.VMEM((RET_QB, SEQ), F32)],
        compiler_params=_params("arbitrary", "arbitrary"),
        name="retention_prompt",
    )(log_g, p, p, p, gn)
    cos, slo, shi = _rope_tables(RET_DK)
    rb = N_PROMPT_TOK // DEC_SEQ
    full = pl.BlockSpec((DEC_SEQ, RET_DK), lambda b, h: (0, 0))
    o = pl.pallas_call(
        _ret_latent_kernel,
        grid=(DEC_BATCH, RET_HEADS),
        in_specs=[
            smem,
            pl.BlockSpec((DEC_SEQ, RET_DK), lambda b, h: (rb + b, h)),
            pl.BlockSpec((DEC_SEQ, RET_DK), lambda b, h: (rb + b, kq + h)),
            pl.BlockSpec((DEC_SEQ, RET_DV), lambda b, h: (rb + b, kq + h)),
            pl.BlockSpec((1, RET_DV), lambda b, h: (0, h)),
            full, full, full,
            pl.BlockSpec((1, 1, 2, 1, RET_DK, RET_DV), lambda b, h: (b, j, 0, h, 0, 0)),
            pl.BlockSpec(memory_space=pl.ANY),
        ],
        out_specs=pl.BlockSpec((DEC_SEQ, RET_DV), lambda b, h: (rb + b, h)),
        out_shape=jax.ShapeDtypeStruct((N_TOK, RET_V), F32),
        input_output_aliases={9: 0},
        compiler_params=_params("arbitrary", "arbitrary"),
        name="retention_latent",
    )(log_g, p, p, p, gn, cos, slo, shi, state_ret, o_p)
    return o, st


def _layer_ret(x, p, mod, j):
    i = N_MIXERS * j + 0
    proj = _in_proj(x, p['norm_w'][i], mod, p['ret_w_in'][j], IN_TN, out_dtype=BF16)
    log_g = jax.nn.log_sigmoid(p['ret_decay'][j].astype(F32))
    o, st = _retention(proj, log_g, p['ret_gn'][j], p['state_ret'], j)
    x = _out_proj(o, proj, (2 * RET_QK + RET_V) // RET_V, p['ret_w_out'][j], x, mod, p['final_norm_w'], False)
    return x, st


RW_TM = 512
RW_HALO = 8
RW_C = 64
RW_LOCK = 4
RW_PAIRS = 2
RW_PAIRS_PROMPT = 2


def _rwkv_prep_kernel(x_ref, xp_ref, xn_ref, nw_ref, mod_ref, mu_ref, wa_ref, aa_ref, wb_ref, ab_ref,
                      w0_ref, a0_ref, xm_ref, lw_ref, a_ref):
    i = pl.program_id(0)
    nw = nw_ref[...]
    h = _norm_mod(x_ref[...], nw, mod_ref)
    h_before = _norm_mod(xp_ref[RW_HALO - 1:RW_HALO, :], nw, mod_ref)
    h_after = _norm_mod(xn_ref[0:1, :], nw, mod_ref)
    seq = jnp.where(i < N_PROMPT_TOK // RW_TM, SEQ, DEC_SEQ)
    row = lax.broadcasted_iota(jnp.int32, (RW_TM, 1), 0)
    t = (row + i * RW_TM) & (seq - 1)
    prev = jnp.where(row == 0, h_before, pltpu.roll(h, 1, axis=0))
    nxt = jnp.where(row == RW_TM - 1, h_after, pltpu.roll(h, RW_TM - 1, axis=0))
    prev = jnp.where(t == 0, 0.0, prev)
    nxt = jnp.where(t == seq - 1, 0.0, nxt)
    xx = 0.5 * (prev + nxt) - h
    for n, m in enumerate((0, 2, 3, 5)):
        xm_ref[n] = (h + xx * mu_ref[m:m + 1, :]).astype(BF16)
    xw = (h + xx * mu_ref[1:2, :]).astype(BF16)
    xa = (h + xx * mu_ref[4:5, :]).astype(BF16)
    lw = jnp.tanh(_dot(xw, wa_ref[...])).astype(BF16)
    la = _dot(xa, aa_ref[...]).astype(BF16)
    for dr in range(2):
        wl = w0_ref[dr:dr + 1, :] + _dot(lw, wb_ref[dr])
        lw_ref[dr] = -math.exp(-0.5) * _sigmoid(wl)
        a_ref[dr] = _sigmoid(a0_ref[dr:dr + 1, :] + _dot(la, ab_ref[dr]))


def _rwkv_prep(x, norm_w, mod, mu, wa2, aa2, wb_pad, ab_pad, w0, a0):
    nt = N_TOK // RW_TM
    hb = RW_TM // RW_HALO
    last = N_TOK // RW_HALO - 1
    full2 = lambda shape: pl.BlockSpec(shape, lambda i: (0, 0))
    full3 = lambda shape: pl.BlockSpec(shape, lambda i: (0, 0, 0))
    return pl.pallas_call(
        _rwkv_prep_kernel,
        grid=(nt,),
        in_specs=[
            pl.BlockSpec((RW_TM, D_MODEL), lambda i: (i, 0)),
            pl.BlockSpec((RW_HALO, D_MODEL), lambda i: (jnp.maximum(i * hb - 1, 0), 0)),
            pl.BlockSpec((RW_HALO, D_MODEL), lambda i: (jnp.minimum((i + 1) * hb, last), 0)),
            full2((1, D_MODEL)),
            pl.BlockSpec((1, 3, 1, D_MODEL), lambda i: (_cond_of_tile(i, RW_TM), 0, 0, 0)),
            full2((6, D_MODEL)),
            full2((D_MODEL, 2 * RWKV_RANK)),
            full2((D_MODEL, 2 * RWKV_RANK)),
            full3((2, 2 * RWKV_RANK, D_MODEL)),
            full3((2, 2 * RWKV_RANK, D_MODEL)),
            full2((2, D_MODEL)),
            full2((2, D_MODEL)),
        ],
        out_specs=[
            pl.BlockSpec((4, RW_TM, D_MODEL), lambda i: (0, i, 0)),
            pl.BlockSpec((2, RW_TM, D_MODEL), lambda i: (0, i, 0)),
            pl.BlockSpec((2, RW_TM, D_MODEL), lambda i: (0, i, 0)),
        ],
        out_shape=[
            jax.ShapeDtypeStruct((4, N_TOK, D_MODEL), BF16),
            jax.ShapeDtypeStruct((2, N_TOK, D_MODEL), F32),
            jax.ShapeDtypeStruct((2, N_TOK, D_MODEL), F32),
        ],
        compiler_params=_params("arbitrary"),
        name="rwkv_prep",
    )(x, x, x, norm_w.reshape(1, D_MODEL), mod, mu, wa2, aa2, wb_pad, ab_pad, w0, a0)


def _bmm_kernel(a_ref, w_ref, o_ref):
    o_ref[0] = _dot(a_ref[0], w_ref[...])


def _rwkv_rkvg(xm, w):
    tm = 1024
    return pl.pallas_call(
        _bmm_kernel,
        grid=(4, N_TOK // tm),
        in_specs=[
            pl.BlockSpec((1, tm, D_MODEL), lambda n, i: (n, i, 0)),
            pl.BlockSpec((D_MODEL, D_MODEL), lambda n, i: (0, n)),
        ],
        out_specs=pl.BlockSpec((1, tm, D_MODEL), lambda n, i: (n, i, 0)),
        out_shape=jax.ShapeDtypeStruct((4, N_TOK, D_MODEL), F32),
        compiler_params=_params("arbitrary", "arbitrary"),
        name="rwkv_rkvg",
    )(xm, w)


def _head_sum(x, first):
    s0 = jnp.sum(jnp.where(first, x, 0.0), axis=-1, keepdims=True)
    s1 = jnp.sum(jnp.where(first, 0.0, x), axis=-1, keepdims=True)
    return jnp.where(first, s0, s1)


def _stack_heads(x, first):
    return jnp.concatenate([jnp.where(first, x, 0.0), jnp.where(first, 0.0, x)], axis=0)


def _split3(x):
    hi = x.astype(BF16)
    r1 = x - hi.astype(F32)
    mid = r1.astype(BF16)
    return hi, mid, (r1 - mid.astype(F32)).astype(BF16)


def _rwkv_kernel(*refs, seq, zero_init, pg, np2):
    n_in = 7 if zero_init else 9
    rkv_ref, lw_ref, a_ref, kkp_ref, kap_ref, rkp_ref, gn_ref = refs[:7]
    if zero_init:
        o_ref, st_ref = refs[n_in:n_in + 2]
        scr = refs[n_in + 2:]
    else:
        s0_ref = refs[7]
        o_ref = refs[n_in]
        scr = refs[n_in + 1:]
    kk_scr, cum_scr, bon_scr, y_scr, s_scr, tar_scr, lrb_scr, b2_scr, w2_scr, yl_scr, kv_scr, pc_scr = scr
    c_len = RW_C
    n_ch = seq // c_len
    rows2 = 2 * c_len
    grp = pl.program_id(1)
    defer = np2 > pg
    base = grp * pg if defer else 0
    first = _first_half_lanes()

    rr = lax.broadcasted_iota(jnp.int32, (rows2, rows2), 0)
    cc = lax.broadcasted_iota(jnp.int32, (rows2, rows2), 1)
    eye = (rr == cc).astype(F32)

    def same(shift):
        return (rr >> shift) == (cc >> shift)

    head = same(6)
    strict = (head & (cc < rr), head & (cc > rr))
    incl = (head & (cc <= rr), head & (cc >= rr))
    last = (c_len - 1, 0)
    head_ones = head.astype(BF16)

    cs_rows = min(seq, 256)
    tr = lax.broadcasted_iota(jnp.int32, (cs_rows, cs_rows), 0)
    tc = lax.broadcasted_iota(jnp.int32, (cs_rows, cs_rows), 1)
    chunk = (tr >> 6) == (tc >> 6)
    tri = ((chunk & (tc <= tr)).astype(BF16), (chunk & (tc >= tr)).astype(BF16))
    for p in range(pg):
        ln = slice(p * LANES, (p + 1) * LANES)
        kk = rkv_ref[1, :, ln] * kkp_ref[:, ln]
        kk_scr[p] = kk * lax.rsqrt(jnp.maximum(_head_sum(kk * kk, first), 1e-12))

    def bonus_terms():
        for p in range(pg):
            ln = slice(p * LANES, (p + 1) * LANES)
            r = rkv_ref[0, :, ln]
            k = rkv_ref[1, :, ln]
            bonus = None
            for dr in range(2):
                kd = k * (1.0 + (a_ref[dr, :, ln] - 1.0) * kap_ref[:, ln])
                term = _head_sum(r * kd * rkp_ref[:, ln], first) * rkv_ref[2, :, ln]
                bonus = term if bonus is None else bonus + term
            bon_scr[base + p] = bonus

    for p in range(0, pg, 2):
        for dr in range(2):
            for r0 in range(0, seq, cs_rows):
                parts = _split3(lw_ref[dr, r0:r0 + cs_rows, p * LANES:(p + 2) * LANES])
                cum = _dot(tri[dr], parts[0]) + _dot(tri[dr], parts[1]) + _dot(tri[dr], parts[2])
                cum_scr[p, dr, r0:r0 + cs_rows, :] = cum[:, :LANES]
                cum_scr[p + 1, dr, r0:r0 + cs_rows, :] = cum[:, LANES:]

    def phase1(chains):
        dirs = [dr for _, dr, _ in chains]
        a2, r2, b2, k2, v2, pc = [], [], [], [], [], []
        for p, dr, c in chains:
            ln = slice(p * LANES, (p + 1) * LANES)
            rw = pl.ds(pl.multiple_of(c * c_len, c_len), c_len)
            a = a_ref[dr, rw, ln]
            k = rkv_ref[1, rw, ln]
            kk_c = kk_scr[p, rw, :]
            cum_c = cum_scr[p, dr, rw, :]
            e_inc = jnp.exp(cum_c)
            e_inv = jnp.exp(-cum_c)
            a2.append(_stack_heads(-kk_c * jnp.exp(cum_c - lw_ref[dr, rw, ln]), first).astype(BF16))
            r2.append(_stack_heads(rkv_ref[0, rw, ln] * e_inc, first).astype(BF16))
            b2.append(_stack_heads(kk_c * a * e_inv, first).astype(BF16))
            k2.append(_stack_heads(k * (1.0 + (a - 1.0) * kap_ref[:, ln]) * e_inv, first).astype(BF16))
            v2.append(_stack_heads(rkv_ref[2, rw, ln], first).astype(BF16))
            pc.append(e_inc[last[dr]:last[dr] + 1, :])
        g = [_dot_nt(jnp.concatenate([x, y], axis=0), jnp.concatenate([z, w], axis=0))
             for x, y, z, w in zip(a2, r2, b2, k2)]
        l_ab = [jnp.where(strict[dr], x[:rows2, :rows2], 0.0) for dr, x in zip(dirs, g)]
        t = [eye + jnp.where(same(1), x, 0.0) for x in l_ab]
        side = {}
        for shift in range(1, 6):
            sib = same(shift + 1) & ~same(shift)
            tb = [x.astype(BF16) for x in t]
            mid = [_dot(jnp.where(sib, x, 0.0).astype(BF16), y) for x, y in zip(l_ab, tb)]
            if shift == 1:
                side['lv'] = [_dot(jnp.where(strict[dr], x[:rows2, rows2:], 0.0).astype(BF16), y)
                              for dr, x, y in zip(dirs, g, v2)]
            elif shift == 2:
                side['yl'] = [_dot(jnp.where(incl[dr], x[rows2:, rows2:], 0.0).astype(BF16), y)
                              for dr, x, y in zip(dirs, g, v2)]
            elif shift == 3:
                side['kv'] = [_dot_tn(x, y) for x, y in zip(v2, k2)]
            t = [x + _dot(y, z.astype(BF16)) for x, y, z in zip(t, tb, mid)]
        tb = [x.astype(BF16) for x in t]
        ta = [_dot(x, y) for x, y in zip(tb, a2)]
        w2 = [_dot(x, y.astype(BF16)) for x, y in zip(tb, side['lv'])]
        for i, (p, dr, c) in enumerate(chains):
            n = ((base + p) * 2 + dr) * n_ch + c
            tar_scr[n, :rows2, :] = ta[i].astype(BF16)
            tar_scr[n, rows2:, :] = r2[i]
            w2_scr[n] = w2[i]
            yl_scr[n] = side['yl'][i]
            kv_scr[n] = side['kv'][i]
            lrb_scr[n] = jnp.where(incl[dr], g[i][rows2:, :rows2], 0.0).astype(BF16)
            b2_scr[n] = b2[i]
            pc_scr[n] = pc[i]

    def body1(cg, carry):
        phase1([(p, dr, cg * RW_LOCK + j) for p in range(pg) for j in range(RW_LOCK) for dr in range(2)])
        return carry

    if n_ch == RW_LOCK:
        body1(0, 0)
    else:
        lax.fori_loop(0, n_ch // RW_LOCK, body1, 0)
    bonus_terms()

    def finish():
        for p in range(np2):
            for dr in range(2):
                if zero_init:
                    s_scr[2 * p + dr] = jnp.zeros((rows2, LANES), F32)
                else:
                    s_scr[2 * p + dr] = s0_ref[0, dr, p]

        def body2(i, carry):
            cs = (i, n_ch - 1 - i)
            ids = [(p, dr) for p in range(np2) for dr in range(2)]
            ns = [(p * 2 + dr) * n_ch + cs[dr] for p, dr in ids]
            x = [_dot_nt(tar_scr[n], s_scr[2 * p + dr].astype(BF16)) for n, (p, dr) in zip(ns, ids)]
            u2 = [(xx[:rows2] + w2_scr[n]).astype(BF16) for n, xx in zip(ns, x)]
            upd = [_dot_tn(u, b2_scr[n]) for n, u in zip(ns, u2)]
            yb = [_dot(lrb_scr[n], u) for n, u in zip(ns, u2)]
            for j, (n, (p, dr)) in enumerate(zip(ns, ids)):
                y2 = x[j][rows2:] + yb[j] + yl_scr[n]
                y_scr[p, dr, pl.ds(pl.multiple_of(cs[dr] * c_len, c_len), c_len), :] = y2[:c_len] + y2[c_len:]
                s_scr[2 * p + dr] = (s_scr[2 * p + dr] + upd[j] + kv_scr[n]) * pc_scr[n]
            return carry

        lax.fori_loop(0, n_ch, body2, 0)

        def head_mean(xs):
            parts = [_split3(x) for x in xs]
            return [(_dot(a, head_ones) + _dot(b, head_ones) + _dot(c, head_ones)) * (1.0 / RWKV_HD)
                    for a, b, c in parts]

        ys = [y_scr[p, 0] + y_scr[p, 1] for p in range(np2)]
        yc = [y - m for y, m in zip(ys, head_mean(ys))]
        var = head_mean([x * x for x in yc])
        for p in range(np2):
            ln = slice(p * LANES, (p + 1) * LANES)
            o_ref[:, ln] = yc[p] * lax.rsqrt(var[p] + GN_EPS) * gn_ref[:, ln] + bon_scr[p]
            if zero_init:
                for dr in range(2):
                    s2 = s_scr[2 * p + dr]
                    st_ref[0, 0, dr, 2 * p] = s2[:RWKV_HD, :RWKV_HD]
                    st_ref[0, 0, dr, 2 * p + 1] = s2[RWKV_HD:, RWKV_HD:]

    if defer:
        pl.when(grp == pl.num_programs(1) - 1)(finish)
    else:
        finish()


def _rwkv_mixer(rkvg, lw, a, kkp, kap, rkp, gn, s0_pairs):
    npair = RWKV_HEADS // 2
    r2 = 2 * RW_C

    def scratch(seq, pg, np2):
        n = 2 * np2 * (seq // RW_C)
        return [
            pltpu.VMEM((pg, seq, LANES), F32), pltpu.VMEM((pg, 2, seq, LANES), F32),
            pltpu.VMEM((np2, seq, LANES), F32), pltpu.VMEM((np2, 2, seq, LANES), F32),
            pltpu.VMEM((2 * np2, r2, LANES), F32),
            pltpu.VMEM((n, 2 * r2, LANES), BF16), pltpu.VMEM((n, r2, r2), BF16), pltpu.VMEM((n, r2, LANES), BF16),
            pltpu.VMEM((n, r2, LANES), F32), pltpu.VMEM((n, r2, LANES), F32), pltpu.VMEM((n, r2, LANES), F32),
            pltpu.VMEM((n, 1, LANES), F32),
        ]

    def seq_specs(seq, rb, pg, gn_spec):
        wl = pg * LANES
        par =pl.BlockSpec((1, wl), lambda s, g: (0, g))
        return [
            pl.BlockSpec((3, seq, wl), lambda s, g: (0, rb + s, g)),
            pl.BlockSpec((2, seq, wl), lambda s, g: (0, rb + s, g)),
            pl.BlockSpec((2, seq, wl), lambda s, g: (0, rb + s, g)),
            par, par, par, gn_spec,
        ]

    pars = [u.reshape(1, D_MODEL) for u in (kkp, kap, rkp, gn)]
    pg = RW_PAIRS_PROMPT
    o_p, st = pl.pallas_call(
        functools.partial(_rwkv_kernel, seq=SEQ, zero_init=True, pg=pg, np2=npair),
        grid=(BATCH, npair // pg),
        in_specs=seq_specs(SEQ, 0, pg, pl.BlockSpec((1, D_MODEL), lambda s, g: (0, 0))),
        out_specs=[
            pl.BlockSpec((SEQ, D_MODEL), lambda s, g: (s, 0)),
            pl.BlockSpec((1, 1, 2, RWKV_HEADS, RWKV_HD, RWKV_HD), lambda s, g: (s, 0, 0, 0, 0, 0)),
        ],
        out_shape=[
            jax.ShapeDtypeStruct((N_TOK, D_MODEL), F32),
            jax.ShapeDtypeStruct((BATCH, 1, 2, RWKV_HEADS, RWKV_HD, RWKV_HD), F32),
        ],
        scratch_shapes=scratch(SEQ, pg, npair),
        compiler_params=_params("arbitrary", "arbitrary"),
        name="rwkv_prompt",
    )(rkvg, lw, a, *pars)
    rb = N_PROMPT_TOK // DEC_SEQ
    pg = RW_PAIRS
    wl = pg * LANES
    o = pl.pallas_call(
        functools.partial(_rwkv_kernel, seq=DEC_SEQ, zero_init=False, pg=pg, np2=pg),
        grid=(DEC_BATCH, npair // pg),
        in_specs=seq_specs(DEC_SEQ, rb, pg, pl.BlockSpec((1, wl), lambda s, g: (0, g))) + [
            pl.BlockSpec((1, 2, pg, LANES, LANES), lambda s, g: (s, 0, g, 0, 0)),
            pl.BlockSpec(memory_space=pl.ANY),
        ],
        out_specs=pl.BlockSpec((DEC_SEQ, wl), lambda s, g: (rb + s, g)),
        out_shape=jax.ShapeDtypeStruct((N_TOK, D_MODEL), F32),
        input_output_aliases={8: 0},
        scratch_shapes=scratch(DEC_SEQ, pg, pg),
        compiler_params=_params("arbitrary", "arbitrary"),
        name="rwkv_latent",
    )(rkvg, lw, a, *pars, s0_pairs, o_p)
    return o, st


def _state_pairs(s0):
    s = s0.reshape(DEC_BATCH, 2, RWKV_HEADS // 2, 2, RWKV_HD, RWKV_HD)
    z = jnp.zeros_like(s[:, :, :, 0])
    top = jnp.concatenate([s[:, :, :, 0], z], axis=-1)
    bot = jnp.concatenate([z, s[:, :, :, 1]], axis=-1)
    return jnp.concatenate([top, bot], axis=-2)


def _layer_rwkv(x, p, mod, j):
    i = N_MIXERS * j + 1
    wa, wb, aa, ab = p['rwkv_wA'][j], p['rwkv_wB'][j], p['rwkv_aA'][j], p['rwkv_aB'][j]
    z = jnp.zeros_like(wb[0])
    wa2 = jnp.concatenate([wa[0], wa[1]], axis=1).astype(BF16)
    aa2 = jnp.concatenate([aa[0], aa[1]], axis=1).astype(BF16)
    wb_pad = jnp.stack([jnp.concatenate([wb[0], z]), jnp.concatenate([z, wb[1]])]).astype(BF16)
    ab_pad = jnp.stack([jnp.concatenate([ab[0], z]), jnp.concatenate([z, ab[1]])]).astype(BF16)
    xm, lw, a = _rwkv_prep(x, p['norm_w'][i], mod, p['rwkv_mu'][j], wa2, aa2, wb_pad, ab_pad,
                           p['rwkv_w0'][j], p['rwkv_a0'][j])
    rkvg = _rwkv_rkvg(xm, p['rwkv_w_in'][j].astype(BF16))
    o, st = _rwkv_mixer(rkvg, lw, a, p['rwkv_kk'][j], p['rwkv_ka'][j], p['rwkv_rk'][j], p['rwkv_gn'][j],
                          _state_pairs(p['state_rwkv'][:, j]))
    x = _out_proj(o, rkvg.reshape(4 * N_TOK, D_MODEL), 0, p['rwkv_w_out'][j], x, mod, p['final_norm_w'], False,
                  g_row0=3 * N_TOK)
    return x, st


DIFF_W = 2 * DIFF_HD
ATT_QB = 256
DIFF_GROUP = 4


def _first_half_lanes():
    return lax.broadcasted_iota(jnp.int32, (1, LANES), 1) < LANES // 2


def _diff_lambda(lam_ref, lam_init):
    lp = lam_ref[...]
    return (jnp.exp(jnp.sum(lp[0:1] * lp[1:2], keepdims=True))
            - jnp.exp(jnp.sum(lp[2:3] * lp[3:4], keepdims=True)) + lam_init)


def _diff_heads(items, lam, lam_init):
    first = _first_half_lanes()
    scale = DIFF_HD ** -0.5
    sub = [(q, keys, comp) for q, keys, _ in items for comp in range(2)]
    qm = [(jnp.where(first if comp == 0 else ~first, q, 0.0) * scale).astype(BF16) for q, _, comp in sub]
    s = [[_dot_nt(x, kb) for kb, _ in keys] for x, (_, keys, _) in zip(qm, sub)]
    m = [functools.reduce(jnp.maximum, [jnp.max(u, axis=-1, keepdims=True) for u in ss]) for ss in s]
    e = [[jnp.exp(u - mm) for u in ss] for ss, mm in zip(s, m)]
    inv = [1.0 / functools.reduce(lambda x, y: x + y, [jnp.sum(u, axis=-1, keepdims=True) for u in ee]) for ee in e]
    outs = []
    for i, (_, keys, gn) in enumerate(items):
        o = None
        lam_inv = lam * inv[2 * i + 1]
        for n, (_, vb) in enumerate(keys):
            p = e[2 * i][n] * inv[2 * i] - e[2 * i + 1][n] * lam_inv
            part = _dot(p.astype(BF16), vb)
            o = part if o is None else o + part
        outs.append(o)
    return [o * lax.rsqrt(jnp.mean(o * o, axis=-1, keepdims=True) + EPS) * gn * (1.0 - lam_init)
            for o, (_, _, gn) in zip(outs, items)]


def _diff_prompt_kernel(lam_ref, q_ref, k_ref, v_ref, gn_ref, o_ref, *, lam_init):
    lam = _diff_lambda(lam_ref, lam_init)
    for h0 in range(0, DIFF_HEADS, DIFF_GROUP):
        items = []
        for h in range(h0, h0 + DIFF_GROUP):
            sl = slice(h * DIFF_W, (h + 1) * DIFF_W)
            items.append((q_ref[:, sl], [(k_ref[:, sl].astype(BF16), v_ref[:, sl].astype(BF16))], gn_ref[:, sl]))
        for h, o in zip(range(h0, h0 + DIFF_GROUP), _diff_heads(items, lam, lam_init)):
            o_ref[:, h * DIFF_W:(h + 1) * DIFF_W] = o


def _diff_latent_kernel(lam_ref, q_ref, k_ref, v_ref, ck_ref, cv_ref, cos_ref, slo_ref, shi_ref, gn_ref,
                        _prev_ref, o_ref, *, lam_init):
    lam = _diff_lambda(lam_ref, lam_init)
    tabs = (cos_ref[...], slo_ref[...], shi_ref[...])
    q = _rope(q_ref[...].astype(F32), *tabs, DIFF_HD // 4)
    k = _rope(k_ref[...].astype(F32), *tabs, DIFF_HD // 4)
    keys = [(k.astype(BF16), v_ref[...].astype(BF16)),
            (ck_ref[0, 0, 0].astype(BF16), cv_ref[0, 0, 0].astype(BF16))]
    gn = gn_ref[...]
    n_blk = DEC_SEQ // ATT_QB
    items = [(q[qi * ATT_QB:(qi + 1) * ATT_QB], keys, gn) for qi in range(n_blk)]
    for qi, o in enumerate(_diff_heads(items, lam, lam_init)):
        o_ref[qi * ATT_QB:(qi + 1) * ATT_QB, :] = o


def _diff_attention(proj, lam_p, gn_w, cache_k, cache_v, j, lam_init):
    gn = gn_w.reshape(1, D_MODEL)
    lam_spec = pl.BlockSpec((4, DIFF_HD), lambda *_: (0, 0))
    o_p = pl.pallas_call(
        functools.partial(_diff_prompt_kernel, lam_init=lam_init),
        grid=(BATCH,),
        in_specs=[
            lam_spec,
            pl.BlockSpec((SEQ, D_MODEL), lambda b: (b, 0)),
            pl.BlockSpec((SEQ, D_MODEL), lambda b: (b, 1)),
            pl.BlockSpec((SEQ, D_MODEL), lambda b: (b, 2)),
            pl.BlockSpec((1, D_MODEL), lambda b: (0, 0)),
        ],
        out_specs=pl.BlockSpec((SEQ, D_MODEL), lambda b: (b, 0)),
        out_shape=jax.ShapeDtypeStruct((N_TOK, D_MODEL), F32),
        compiler_params=_params("arbitrary"),
        name="diff_prompt",
    )(lam_p, proj, proj, proj, gn)
    cos, slo, shi = (jnp.concatenate([u, u], axis=-1) for u in _rope_tables(DIFF_HD))
    rb = N_PROMPT_TOK // DEC_SEQ
    nh = DIFF_HEADS
    tab = pl.BlockSpec((DEC_SEQ, DIFF_W), lambda b, h: (0, 0))
    cache = pl.BlockSpec((1, 1, 1, PAST_LEN, DIFF_W), lambda b, h: (b, j, h, 0, 0))
    o = pl.pallas_call(
        functools.partial(_diff_latent_kernel, lam_init=lam_init),
        grid=(DEC_BATCH, nh),
        in_specs=[
            lam_spec,
            pl.BlockSpec((DEC_SEQ, DIFF_W), lambda b, h: (rb + b, h)),
            pl.BlockSpec((DEC_SEQ, DIFF_W), lambda b, h: (rb + b, nh + h)),
            pl.BlockSpec((DEC_SEQ, DIFF_W), lambda b, h: (rb + b, 2 * nh + h)),
            cache, cache, tab, tab, tab,
            pl.BlockSpec((1, DIFF_W), lambda b, h: (0, h)),
            pl.BlockSpec(memory_space=pl.ANY),
        ],
        out_specs=pl.BlockSpec((DEC_SEQ, DIFF_W), lambda b, h: (rb + b, h)),
        out_shape=jax.ShapeDtypeStruct((N_TOK, D_MODEL), F32),
        input_output_aliases={10: 0},
        compiler_params=_params("arbitrary", "arbitrary"),
        name="diff_latent",
    )(lam_p, proj, proj, proj, cache_k, cache_v, cos, slo, shi, gn, o_p)
    return o


def _layer_diff(x, p, mod, j, i):
    lam_init = 0.8 - 0.6 * math.exp(-0.3 * i)
    proj, new_k, new_v = _in_proj_kv(x, p['norm_w'][i], mod, p['diff_w_in'][j], DIFF_HEADS)
    o = _diff_attention(proj, p['diff_lambda'][j], p['diff_gn'][j], p['cache_diff_k'], p['cache_diff_v'], j,
                        lam_init)
    x = _out_proj(o, proj, 3, p['diff_w_out'][j], x, mod, p['final_norm_w'], False)
    return x, new_k, new_v


NA_ROWS = DEC_SEQ // GRID_W
NA_WR = min(NA_WIN_R, NA_ROWS)
NA_LOC = NA_WR * GRID_W
NA_ROW_GROUP = 4
NA_PAIR_GROUP = 2


def _na_prompt_kernel(q_ref, k_ref, v_ref, o_ref):
    first = _first_half_lanes()
    scale = NA_HD ** -0.5
    for p0 in range(0, NA_HEADS // 2, NA_PAIR_GROUP):
        pairs = range(p0, p0 + NA_PAIR_GROUP)
        kb = [k_ref[:, pr * LANES:(pr + 1) * LANES].astype(BF16) for pr in pairs]
        vb = [v_ref[:, pr * LANES:(pr + 1) * LANES].astype(BF16) for pr in pairs]
        items = [(i, half) for i in range(NA_PAIR_GROUP) for half in range(2)]
        qm = [(jnp.where(first if half == 0 else ~first, q_ref[:, (p0 + i) * LANES:(p0 + i + 1) * LANES], 0.0)
               * scale).astype(BF16) for i, half in items]
        s = [_dot_nt(x, kb[i]) for x, (i, _) in zip(qm, items)]
        e = [jnp.exp(x - jnp.max(x, axis=-1, keepdims=True)) for x in s]
        inv = [1.0 / jnp.sum(x, axis=-1, keepdims=True) for x in e]
        outs = [_dot(x.astype(BF16), vb[i]) * z for x, z, (i, _) in zip(e, inv, items)]
        for i in range(NA_PAIR_GROUP):
            o_ref[:, (p0 + i) * LANES:(p0 + i + 1) * LANES] = jnp.where(first, outs[2 * i], outs[2 * i + 1])


def _na_latent_kernel(q_ref, k_ref, v_ref, kc_ref, vc_ref, tab_ref, _prev_ref, o_ref):
    first = _first_half_lanes()
    scale = NA_HD ** -0.5
    kb = k_ref[...].astype(BF16)
    vb = v_ref[...].astype(BF16)
    kcb = kc_ref[0, 0].astype(BF16)
    vcb = vc_ref[0, 0].astype(BF16)
    qcol = lax.broadcasted_iota(jnp.int32, (GRID_W, NA_LOC), 0)
    kcol = lax.broadcasted_iota(jnp.int32, (GRID_W, NA_LOC), 1) & (GRID_W - 1)
    cstart = jnp.clip(qcol - NA_WIN_C // 2, 0, GRID_W - NA_WIN_C)
    col_ok = (kcol >= cstart) & (kcol < cstart + NA_WIN_C)
    def bias_of(r, rs, half):
        parts = []
        for w in range(0, NA_WR, 2):
            src = jnp.broadcast_to(tab_ref[half, rs + w - r + NA_WIN_R - 1], (GRID_W, LANES))
            parts.append(pltpu.roll(src, LANES - (NA_WIN_C - 1), axis=1, stride=1, stride_axis=0))
        return jnp.concatenate(parts, axis=1)

    for r0 in range(0, NA_ROWS, NA_ROW_GROUP):
        items = [(r, min(max(r - NA_WR // 2, 0), NA_ROWS - NA_WR), half)
                 for r in range(r0, r0 + NA_ROW_GROUP) for half in range(2)]
        qm = [(jnp.where(first if half == 0 else ~first, q_ref[r * GRID_W:(r + 1) * GRID_W, :], 0.0)
               * scale).astype(BF16) for r, _, half in items]
        s_loc = [_dot_nt(x, kb[rs * GRID_W:(rs + NA_WR) * GRID_W]) for x, (_, rs, _) in zip(qm, items)]
        s_ctx = [_dot_nt(x, kcb) for x in qm]
        s_loc = [jnp.where(col_ok, x + bias_of(*it), -jnp.inf) for x, it in zip(s_loc, items)]
        m = [jnp.maximum(jnp.max(x, axis=-1, keepdims=True), jnp.max(y, axis=-1, keepdims=True))
             for x, y in zip(s_loc, s_ctx)]
        e_loc = [jnp.exp(x - mm) for x, mm in zip(s_loc, m)]
        e_ctx = [jnp.exp(x - mm) for x, mm in zip(s_ctx, m)]
        inv = [1.0 / (jnp.sum(x, axis=-1, keepdims=True) + jnp.sum(y, axis=-1, keepdims=True))
               for x, y in zip(e_loc, e_ctx)]
        pv = [_dot(x.astype(BF16), vb[rs * GRID_W:(rs + NA_WR) * GRID_W]) for x, (_, rs, _) in zip(e_loc, items)]
        pc = [_dot(x.astype(BF16), vcb) for x in e_ctx]
        outs = [(x + y) * z for x, y, z in zip(pv, pc, inv)]
        for n in range(0, len(items), 2):
            r = items[n][0]
            o_ref[r * GRID_W:(r + 1) * GRID_W, :] = jnp.where(first, outs[n], outs[n + 1])


def _na_bias_pairs(table):
    t = table.astype(F32)
    nc = 2 * NA_WIN_C - 1
    z = jnp.zeros(t[:, :-1].shape[:2] + (GRID_W - nc,), F32)
    return jnp.concatenate([t[:, :-1], z, t[:, 1:], z], axis=-1)[:, :, None, :]


def _pair_heads(cache):
    c = cache.reshape(DEC_BATCH, NA_HEADS // 2, 2, PAST_LEN, NA_HD)
    return c.transpose(0, 1, 3, 2, 4).reshape(DEC_BATCH, NA_HEADS // 2, PAST_LEN, LANES)


def _na_attention(proj, bias_table, cache_k, cache_v):
    o_p = pl.pallas_call(
        _na_prompt_kernel,
        grid=(BATCH,),
        in_specs=[
            pl.BlockSpec((SEQ, D_MODEL), lambda b: (b, 0)),
            pl.BlockSpec((SEQ, D_MODEL), lambda b: (b, 1)),
            pl.BlockSpec((SEQ, D_MODEL), lambda b: (b, 2)),
        ],
        out_specs=pl.BlockSpec((SEQ, D_MODEL), lambda b: (b, 0)),
        out_shape=jax.ShapeDtypeStruct((N_TOK, D_MODEL), F32),
        compiler_params=_params("arbitrary"),
        name="na_prompt",
    )(proj, proj, proj)
    rb = N_PROMPT_TOK // DEC_SEQ
    npair = NA_HEADS // 2
    cache = pl.BlockSpec((1, 1, PAST_LEN, LANES), lambda pr, b: (b, pr, 0, 0))
    o = pl.pallas_call(
        _na_latent_kernel,
        grid=(npair, DEC_BATCH),
        in_specs=[
            pl.BlockSpec((DEC_SEQ, LANES), lambda pr, b: (rb + b, pr)),
            pl.BlockSpec((DEC_SEQ, LANES), lambda pr, b: (rb + b, npair + pr)),
            pl.BlockSpec((DEC_SEQ, LANES), lambda pr, b: (rb + b, 2 * npair + pr)),
            cache, cache,
            pl.BlockSpec((2, 2 * NA_WIN_R - 2, 1, LANES), lambda pr, b: (pr, 0, 0, 0)),
            pl.BlockSpec(memory_space=pl.ANY),
        ],
        out_specs=pl.BlockSpec((DEC_SEQ, LANES), lambda pr, b: (rb + b, pr)),
        out_shape=jax.ShapeDtypeStruct((N_TOK, D_MODEL), F32),
        input_output_aliases={6: 0},
        compiler_params=_params("arbitrary", "arbitrary"),
        name="na_latent",
    )(proj, proj, proj, _pair_heads(cache_k), _pair_heads(cache_v), _na_bias_pairs(bias_table), o_p)
    return o


def _layer_na(x, p, mod, j, final):
    i = N_MIXERS * j + 3
    proj, new_k, new_v = _in_proj_kv(x, p['norm_w'][i], mod, p['na_w_in'][j], NA_HEADS)
    o = _na_attention(proj, p['na_bias'][j], p['cache_na_k'][:, j], p['cache_na_v'][:, j])
    args = (o, proj, 3, p['na_w_out'][j], x, mod, p['final_norm_w'])
    if final:
        x = (_out_proj(*args, True, rows=(0, N_PROMPT_TOK)), _out_proj(*args, True, rows=(N_PROMPT_TOK, N_TOK)))
    else:
        x = _out_proj(*args, False)
    return x, new_k, new_v


def kernel(x_prompt, x_sample, state_ret, state_rwkv, cache_diff_k, cache_diff_v, cache_na_k, cache_na_v,
           c, c_ctx, norm_w, w_mod, b_mod, final_norm_w,
           ret_w_in, ret_decay, ret_gn, ret_w_out,
           rwkv_mu, rwkv_w_in, rwkv_w0, rwkv_wA, rwkv_wB, rwkv_a0, rwkv_aA, rwkv_aB,
           rwkv_kk, rwkv_ka, rwkv_rk, rwkv_gn, rwkv_w_out,
           diff_w_in, diff_lambda, diff_gn, diff_w_out,
           na_w_in, na_bias, na_w_out):
    p = dict(locals())
    cond = jnp.zeros((N_COND, D_MODEL), F32).at[0].set(c_ctx).at[1:1 + DEC_BATCH].set(c)
    mods = _modulation(cond, w_mod, b_mod)
    x = (x_prompt.reshape(N_PROMPT_TOK, D_MODEL), x_sample.reshape(N_SAMPLE_TOK, D_MODEL))
    new = {n: [] for n in ('ret', 'rwkv', 'dk', 'dv', 'nk', 'nv')}
    for i in range(DEPTH):
        kind, j = i % N_MIXERS, i // N_MIXERS
        if kind == 0:
            x, st = _layer_ret(x, p, mods[i], j)
            new['ret'].append(st)
        elif kind == 1:
            x, st = _layer_rwkv(x, p, mods[i], j)
            new['rwkv'].append(st)
        elif kind == 2:
            x, ck, cv = _layer_diff(x, p, mods[i], j, i)
            new['dk'].append(ck)
            new['dv'].append(cv)
        else:
            x, ck, cv = _layer_na(x, p, mods[i], j, final=(i == DEPTH - 1))
            new['nk'].append(ck)
            new['nv'].append(cv)
    if DEPTH % N_MIXERS:
        raise NotImplementedError("the final norm is fused into the last neighbourhood-attention layer")
    cat = lambda xs: xs[0] if len(xs) == 1 else jnp.concatenate(xs, axis=1)
    return (x[0].reshape(BATCH, SEQ, D_MODEL), x[1].reshape(DEC_BATCH, DEC_SEQ, D_MODEL),
            cat(new['ret']), cat(new['rwkv']), cat(new['dk']), cat(new['dv']), cat(new['nk']), cat(new['nv']))
```

```python
import functools
import math

import jax
import jax.numpy as jnp
from jax import lax
from jax.experimental import pallas as pl
from jax.experimental.pallas import tpu as pltpu

F32 = jnp.float32
BF16 = jnp.bfloat16

D_MODEL = 1024
BATCH = 32
SEQ = 256
DEPTH = 4
N_MIXERS = 4
DEC_BATCH = 2
DEC_SEQ = 1024
PAST_LEN = 256
GRID_W = 64

RET_HEADS = 4
RET_DK = 256
RET_DV = 512
RET_QK = 1024
RET_V = 2048

RWKV_HD = 64
RWKV_HEADS = 16
RWKV_RANK = 64

DIFF_HEADS = 8
DIFF_HD = 64

NA_HEADS = 16
NA_HD = 64
NA_WIN_R = 8
NA_WIN_C = 16

ROPE_BASE = 10000.0
EPS = 1e-6
GN_EPS = 1e-5

N_PROMPT_TOK = BATCH * SEQ
N_SAMPLE_TOK = DEC_BATCH * DEC_SEQ
N_TOK = N_PROMPT_TOK + N_SAMPLE_TOK
N_COND = 8

LANES = 128
VMEM_LIMIT = 56 * 2 ** 20


def _params(*sem):
    return pltpu.CompilerParams(dimension_semantics=sem, vmem_limit_bytes=VMEM_LIMIT)


def _cond_of_tile(i, tm):
    npt = N_PROMPT_TOK // tm
    return jnp.where(i < npt, 0, 1 + (i - npt) // (DEC_SEQ // tm))


def _sigmoid(x):
    return 1.0 / (1.0 + jnp.exp(-x))


def _silu(x):
    return x * _sigmoid(x)


def _dot(a, b):
    return jnp.dot(a, b, preferred_element_type=F32)


def _dot_nt(a, b):
    return lax.dot_general(a, b, (((1,), (1,)), ((), ())), preferred_element_type=F32)


def _dot_tn(a, b):
    return lax.dot_general(a, b, (((0,), (0,)), ((), ())), preferred_element_type=F32)


def _mod_kernel(c_ref, w_ref, b_ref, o_ref):
    s = _silu(c_ref[...])
    o_ref[0] = jnp.dot(s, w_ref[0], precision=lax.Precision.HIGHEST, preferred_element_type=F32) + b_ref[0]


def _modulation(cond, w_mod, b_mod):
    tn = D_MODEL
    out = pl.pallas_call(
        _mod_kernel,
        grid=(DEPTH, 3 * D_MODEL // tn),
        in_specs=[
            pl.BlockSpec((N_COND, D_MODEL), lambda l, j: (0, 0)),
            pl.BlockSpec((1, D_MODEL, tn), lambda l, j: (l, 0, j)),
            pl.BlockSpec((1, 1, tn), lambda l, j: (l, 0, j)),
        ],
        out_specs=pl.BlockSpec((1, N_COND, tn), lambda l, j: (l, 0, j)),
        out_shape=jax.ShapeDtypeStruct((DEPTH, N_COND, 3 * D_MODEL), F32),
        compiler_params=_params("arbitrary", "arbitrary"),
        name="modulation",
    )(cond, w_mod, b_mod.reshape(DEPTH, 1, 3 * D_MODEL))
    return out.reshape(DEPTH, N_COND, 3, 1, D_MODEL)


def _norm_mod(x, nw, mod_ref):
    ms = jnp.mean(x * x, axis=-1, keepdims=True)
    y = x * lax.rsqrt(ms + EPS) * nw
    return y * (1.0 + mod_ref[0, 1]) + mod_ref[0, 0]


IN_TM = 1024
IN_TN = 2048


def _x_specs(x, tm, tile_of):
    if not isinstance(x, tuple):
        return [pl.BlockSpec((tm, D_MODEL), lambda *g: (tile_of(*g), 0))], (x,)
    npt = N_PROMPT_TOK // tm
    return [pl.BlockSpec((tm, D_MODEL), lambda *g: (jnp.minimum(tile_of(*g), npt - 1), 0)),
            pl.BlockSpec((tm, D_MODEL), lambda *g: (jnp.maximum(tile_of(*g) - npt, 0), 0))], x


def _read_x(x_refs, tile, tm):
    if len(x_refs) == 1:
        return x_refs[0][...]
    return jnp.where(tile < N_PROMPT_TOK // tm, x_refs[0][...], x_refs[1][...])


def _in_proj_kernel(*refs, n_x):
    x_refs = refs[:n_x]
    nw_ref, mod_ref, w_ref, o_ref, h_ref = refs[n_x:]

    @pl.when(pl.program_id(1) == 0)
    def _():
        x = _read_x(x_refs, pl.program_id(0), IN_TM)
        h_ref[...] = _norm_mod(x, nw_ref[...], mod_ref).astype(BF16)

    o_ref[...] = _dot(h_ref[...], w_ref[...]).astype(o_ref.dtype)


def _in_proj(x, norm_w, mod, w, tn, out_dtype=F32):
    n = w.shape[1]
    w = w.astype(BF16)
    x_specs, xs = _x_specs(x, IN_TM, lambda i, j: i)
    return pl.pallas_call(
        functools.partial(_in_proj_kernel, n_x=len(xs)),
        grid=(N_TOK // IN_TM, n // tn),
        in_specs=x_specs + [
            pl.BlockSpec((1, D_MODEL), lambda i, j: (0, 0)),
            pl.BlockSpec((1, 3, 1, D_MODEL), lambda i, j: (_cond_of_tile(i, IN_TM), 0, 0, 0)),
            pl.BlockSpec((D_MODEL, tn), lambda i, j: (0, j)),
        ],
        out_specs=pl.BlockSpec((IN_TM, tn), lambda i, j: (i, j)),
        out_shape=jax.ShapeDtypeStruct((N_TOK, n), out_dtype),
        scratch_shapes=[pltpu.VMEM((IN_TM, D_MODEL), BF16)],
        compiler_params=_params("arbitrary", "arbitrary"),
        name="in_proj",
    )(*xs, norm_w.reshape(1, D_MODEL), mod, w)


def _in_proj_kv_kernel(x_ref, nw_ref, mod_ref, w_ref, o_ref, ck_ref, cv_ref, h_ref, *, heads, tn):
    i = pl.program_id(0)
    j = pl.program_id(1)

    @pl.when(j == 0)
    def _():
        h_ref[...] = _norm_mod(x_ref[...], nw_ref[...], mod_ref).astype(BF16)

    acc = _dot(h_ref[...], w_ref[...])
    o_ref[...] = acc.astype(o_ref.dtype)
    hd = D_MODEL // heads
    for col, c_ref in ((D_MODEL, ck_ref), (2 * D_MODEL, cv_ref)):
        @pl.when((j == col // tn) & (i < N_PROMPT_TOK // IN_TM))
        def _(c_ref=c_ref, c0=col % tn):
            for s in range(IN_TM // SEQ):
                for h in range(heads):
                    c_ref[s, 0, h] = acc[s * SEQ:(s + 1) * SEQ, c0 + h * hd:c0 + (h + 1) * hd]


def _in_proj_kv(x, norm_w, mod, w, heads):
    n = w.shape[1]
    tn = D_MODEL
    spb = IN_TM // SEQ
    last = N_PROMPT_TOK // IN_TM - 1
    cache = pl.BlockSpec((spb, 1, heads, SEQ, D_MODEL // heads), lambda i, j: (jnp.minimum(i, last), 0, 0, 0, 0))
    cache_shape = jax.ShapeDtypeStruct((BATCH, 1, heads, SEQ, D_MODEL // heads), F32)
    return pl.pallas_call(
        functools.partial(_in_proj_kv_kernel, heads=heads, tn=tn),
        grid=(N_TOK // IN_TM, n // tn),
        in_specs=[
            pl.BlockSpec((IN_TM, D_MODEL), lambda i, j: (i, 0)),
            pl.BlockSpec((1, D_MODEL), lambda i, j: (0, 0)),
            pl.BlockSpec((1, 3, 1, D_MODEL), lambda i, j: (_cond_of_tile(i, IN_TM), 0, 0, 0)),
            pl.BlockSpec((D_MODEL, tn), lambda i, j: (0, j)),
        ],
        out_specs=[pl.BlockSpec((IN_TM, tn), lambda i, j: (i, j)), cache, cache],
        out_shape=[jax.ShapeDtypeStruct((N_TOK, n), BF16), cache_shape, cache_shape],
        scratch_shapes=[pltpu.VMEM((IN_TM, D_MODEL), BF16)],
        compiler_params=_params("arbitrary", "arbitrary"),
        name="in_proj_kv",
    )(x, norm_w.reshape(1, D_MODEL), mod, w.astype(BF16))


OUT_TM_BYTES = 4 * 2 ** 20


def _out_proj_kernel(*refs, n_x, t0, tm, final):
    x_refs = refs[:n_x]
    o_ref, g_ref, w_ref, mod_ref, fw_ref, y_ref, wb_ref = refs[n_x:]

    @pl.when(pl.program_id(0) == 0)
    def _():
        wb_ref[...] = w_ref[...].astype(BF16)

    a = (o_ref[...] * _silu(g_ref[...].astype(F32))).astype(BF16)
    xn = _read_x(x_refs, t0 + pl.program_id(0), tm) + mod_ref[0, 2] * _dot(a, wb_ref[...])
    if final:
        ms = jnp.mean(xn * xn, axis=-1, keepdims=True)
        xn = xn * lax.rsqrt(ms + EPS) * fw_ref[...]
    y_ref[...] = xn


def _out_proj(o, g_arr, g_blk, w, x, mod, final_w, final, rows=(0, N_TOK), g_row0=0):
    k = w.shape[0]
    tm = OUT_TM_BYTES // (4 * k)
    t0 = rows[0] // tm
    g0 = g_row0 // tm
    x_specs, xs = _x_specs(x, tm, lambda i: t0 + i)
    return pl.pallas_call(
        functools.partial(_out_proj_kernel, n_x=len(xs), t0=t0, tm=tm, final=final),
        grid=((rows[1] - rows[0]) // tm,),
        in_specs=x_specs + [
            pl.BlockSpec((tm, k), lambda i: (t0 + i, 0)),
            pl.BlockSpec((tm, k), lambda i: (g0 + t0 + i, g_blk)),
            pl.BlockSpec((k, D_MODEL), lambda i: (0, 0)),
            pl.BlockSpec((1, 3, 1, D_MODEL), lambda i: (_cond_of_tile(t0 + i, tm), 0, 0, 0)),
            pl.BlockSpec((1, D_MODEL), lambda i: (0, 0)),
        ],
        out_specs=pl.BlockSpec((tm, D_MODEL), lambda i: (i, 0)),
        out_shape=jax.ShapeDtypeStruct((rows[1] - rows[0], D_MODEL), F32),
        scratch_shapes=[pltpu.VMEM((k, D_MODEL), BF16)],
        compiler_params=_params("arbitrary"),
        name="out_proj",
    )(*xs, o, g_arr, w, mod, final_w.reshape(1, D_MODEL))


def _rope_tables(d):
    q = d // 4
    t = jnp.arange(DEC_SEQ)
    row = (t // GRID_W).astype(F32)
    col = (t % GRID_W).astype(F32)
    inv = ROPE_BASE ** (-jnp.arange(0, 2 * q, 2, dtype=F32) / (2 * q))
    ar = row[:, None] * inv[None, :]
    ac = col[:, None] * inv[None, :]
    z = jnp.zeros_like(ar)
    cos = jnp.concatenate([jnp.cos(ar), jnp.cos(ar), jnp.cos(ac), jnp.cos(ac)], axis=-1)
    sin_lo = jnp.concatenate([-jnp.sin(ar), z, -jnp.sin(ac), z], axis=-1)
    sin_hi = jnp.concatenate([z, jnp.sin(ar), z, jnp.sin(ac)], axis=-1)
    return cos, sin_lo, sin_hi


def _rope(x, cos, sin_lo, sin_hi, q):
    w = x.shape[-1]
    x_next = pltpu.roll(x, w - q, axis=1)
    x_prev = pltpu.roll(x, q, axis=1)
    return x * cos + x_next * sin_lo + x_prev * sin_hi


RET_QB = 256


RET_SEQS = 4


def _ret_decay(lgf, lgb, qi, seq):
    ii = lax.broadcasted_iota(jnp.int32, (RET_QB, seq), 0) + qi * RET_QB
    jj = lax.broadcasted_iota(jnp.int32, (RET_QB, seq), 1)
    gap = (ii - jj).astype(F32)
    return (jnp.where(gap >= 0, jnp.exp(lgf * jnp.maximum(gap, 0.0)), 0.0)
            + jnp.where(gap <= 0, jnp.exp(lgb * jnp.maximum(-gap, 0.0)), 0.0))


def _head_layer_norm(o, gn):
    oc = o - jnp.mean(o, axis=-1, keepdims=True)
    return oc * lax.rsqrt(jnp.mean(oc * oc, axis=-1, keepdims=True) + GN_EPS) * gn


def _ret_prompt_kernel(lg_ref, q_ref, k_ref, v_ref, gn_ref, o_ref, st_ref, dec_ref):
    h = pl.program_id(0)
    lgf = lg_ref[0, h]
    lgb = lg_ref[1, h]

    @pl.when(pl.program_id(1) == 0)
    def _():
        dec_ref[...] = _ret_decay(lgf, lgb, 0, SEQ)

    dec = dec_ref[...]
    pos = lax.broadcasted_iota(jnp.int32, (SEQ, 1), 0).astype(F32)
    w_fwd = jnp.exp(lgf * (SEQ - 1.0 - pos))
    w_bwd = jnp.exp(lgb * pos)
    rows = [slice(s * SEQ, (s + 1) * SEQ) for s in range(RET_SEQS)]
    k = [k_ref[r, :].astype(F32) * (RET_DK ** -0.5) for r in rows]
    kb = [x.astype(BF16) for x in k]
    vb = [v_ref[r, :].astype(BF16) for r in rows]
    s = [_dot_nt(q_ref[r, :].astype(BF16), y) for r, y in zip(rows, kb)]
    o = [_dot((x * dec).astype(BF16), y) for x, y in zip(s, vb)]
    s_fwd = [_dot_tn((x * w_fwd).astype(BF16), y) for x, y in zip(k, vb)]
    s_bwd = [_dot_tn((x * w_bwd).astype(BF16), y) for x, y in zip(k, vb)]
    gn = gn_ref[...]
    for i, r in enumerate(rows):
        o_ref[r, :] = _head_layer_norm(o[i], gn)
        st_ref[i, 0, 0, 0] = s_fwd[i]
        st_ref[i, 0, 1, 0] = s_bwd[i]


def _ret_latent_kernel(lg_ref, q_ref, k_ref, v_ref, gn_ref, cos_ref, slo_ref, shi_ref, s0_ref, _prev_ref, o_ref,
                       dec_ref):
    seq = DEC_SEQ
    h = pl.program_id(0)
    lgf = lg_ref[0, h]
    lgb = lg_ref[1, h]

    @pl.when(pl.program_id(1) == 0)
    def _():
        for qi in range(seq // RET_QB):
            dec_ref[qi] = _ret_decay(lgf, lgb, qi, seq)

    q = _rope(q_ref[...].astype(F32), cos_ref[...], slo_ref[...], shi_ref[...], RET_DK // 4)
    k = _rope(k_ref[...].astype(F32), cos_ref[...], slo_ref[...], shi_ref[...], RET_DK // 4)
    kb = (k * (RET_DK ** -0.5)).astype(BF16)
    vb = v_ref[...].astype(BF16)
    gn = gn_ref[...]
    for qi in range(seq // RET_QB):
        qblk = q[qi * RET_QB:(qi + 1) * RET_QB]
        s = _dot_nt(qblk.astype(BF16), kb)
        o = _dot((s * dec_ref[qi]).astype(BF16), vb)
        pos = (lax.broadcasted_iota(jnp.int32, (RET_QB, 1), 0) + qi * RET_QB).astype(F32)
        qf = qblk * jnp.exp(lgf * (pos + 1.0))
        qr = qblk * jnp.exp(lgb * (seq - pos))
        o = o + _dot(qf.astype(BF16), s0_ref[0, 0, 0, 0].astype(BF16))
        o = o + _dot(qr.astype(BF16), s0_ref[0, 0, 1, 0].astype(BF16))
        o_ref[qi * RET_QB:(qi + 1) * RET_QB, :] = _head_layer_norm(o, gn)


def _retention(p, log_g, gn_w, state_ret, j):
    smem = pl.BlockSpec(memory_space=pltpu.SMEM)
    gn = gn_w.reshape(1, RET_V)
    kq = RET_QK // RET_DK
    rows = RET_SEQS * SEQ
    o_p, st = pl.pallas_call(
        _ret_prompt_kernel,
        grid=(RET_HEADS, BATCH // RET_SEQS),
        in_specs=[
            smem,
            pl.BlockSpec((rows, RET_DK), lambda h, b: (b, h)),
            pl.BlockSpec((rows, RET_DK), lambda h, b: (b, kq + h)),
            pl.BlockSpec((rows, RET_DV), lambda h, b: (b, kq + h)),
            pl.BlockSpec((1, RET_DV), lambda h, b: (0, h)),
        ],
        out_specs=[
            pl.BlockSpec((rows, RET_DV), lambda h, b: (b, h)),
            pl.BlockSpec((RET_SEQS, 1, 2, 1, RET_DK, RET_DV), lambda h, b: (b, 0, 0, h, 0, 0)),
        ],
        out_shape=[
            jax.ShapeDtypeStruct((N_TOK, RET_V), F32),
            jax.ShapeDtypeStruct((BATCH, 1, 2, RET_HEADS, RET_DK, RET_DV), F32),
        ],
        scratch_shapes=[pltpu.VMEM((RET_QB, SEQ), F32)],
        compiler_params=_params("arbitrary", "arbitrary"),
        name="retention_prompt",
    )(log_g, p, p, p, gn)
    cos, slo, shi = _rope_tables(RET_DK)
    rb = N_PROMPT_TOK // DEC_SEQ
    full = pl.BlockSpec((DEC_SEQ, RET_DK), lambda h, b: (0, 0))
    o = pl.pallas_call(
        _ret_latent_kernel,
        grid=(RET_HEADS, DEC_BATCH),
        in_specs=[
            smem,
            pl.BlockSpec((DEC_SEQ, RET_DK), lambda h, b: (rb + b, h)),
            pl.BlockSpec((DEC_SEQ, RET_DK), lambda h, b: (rb + b, kq + h)),
            pl.BlockSpec((DEC_SEQ, RET_DV), lambda h, b: (rb + b, kq + h)),
            pl.BlockSpec((1, RET_DV), lambda h, b: (0, h)),
            full, full, full,
            pl.BlockSpec((1, 1, 2, 1, RET_DK, RET_DV), lambda h, b: (b, j, 0, h, 0, 0)),
            pl.BlockSpec(memory_space=pl.ANY),
        ],
        out_specs=pl.BlockSpec((DEC_SEQ, RET_DV), lambda h, b: (rb + b, h)),
        out_shape=jax.ShapeDtypeStruct((N_TOK, RET_V), F32),
        input_output_aliases={9: 0},
        scratch_shapes=[pltpu.VMEM((DEC_SEQ // RET_QB, RET_QB, DEC_SEQ), F32)],
        compiler_params=_params("arbitrary", "arbitrary"),
        name="retention_latent",
    )(log_g, p, p, p, gn, cos, slo, shi, state_ret, o_p)
    return o, st


def _layer_ret(x, p, mod, j):
    i = N_MIXERS * j + 0
    proj = _in_proj(x, p['norm_w'][i], mod, p['ret_w_in'][j], IN_TN, out_dtype=BF16)
    log_g = jax.nn.log_sigmoid(p['ret_decay'][j].astype(F32))
    o, st = _retention(proj, log_g, p['ret_gn'][j], p['state_ret'], j)
    x = _out_proj(o, proj, (2 * RET_QK + RET_V) // RET_V, p['ret_w_out'][j], x, mod, p['final_norm_w'], False)
    return x, st


RW_TM = 512
RW_HALO = 8
RW_C = 64
RW_LOCK = 4
RW_PAIRS = 2
RW_PAIRS_PROMPT = 2


def _rwkv_prep_kernel(x_ref, xp_ref, xn_ref, nw_ref, mod_ref, mu_ref, wa_ref, aa_ref, wb_ref, ab_ref,
                      w0_ref, a0_ref, xm_ref, lw_ref, a_ref):
    i = pl.program_id(0)
    nw = nw_ref[...]
    h = _norm_mod(x_ref[...], nw, mod_ref)
    h_before = _norm_mod(xp_ref[RW_HALO - 1:RW_HALO, :], nw, mod_ref)
    h_after = _norm_mod(xn_ref[0:1, :], nw, mod_ref)
    seq = jnp.where(i < N_PROMPT_TOK // RW_TM, SEQ, DEC_SEQ)
    row = lax.broadcasted_iota(jnp.int32, (RW_TM, 1), 0)
    t = (row + i * RW_TM) & (seq - 1)
    prev = jnp.where(row == 0, h_before, pltpu.roll(h, 1, axis=0))
    nxt = jnp.where(row == RW_TM - 1, h_after, pltpu.roll(h, RW_TM - 1, axis=0))
    prev = jnp.where(t == 0, 0.0, prev)
    nxt = jnp.where(t == seq - 1, 0.0, nxt)
    xx = 0.5 * (prev + nxt) - h
    for n, m in enumerate((0, 2, 3, 5)):
        xm_ref[n] = (h + xx * mu_ref[m:m + 1, :]).astype(BF16)
    xw = (h + xx * mu_ref[1:2, :]).astype(BF16)
    xa = (h + xx * mu_ref[4:5, :]).astype(BF16)
    lw = jnp.tanh(_dot(xw, wa_ref[...])).astype(BF16)
    la = _dot(xa, aa_ref[...]).astype(BF16)
    for dr in range(2):
        wl = w0_ref[dr:dr + 1, :] + _dot(lw, wb_ref[dr])
        lw_ref[dr] = -math.exp(-0.5) * _sigmoid(wl)
        a_ref[dr] = _sigmoid(a0_ref[dr:dr + 1, :] + _dot(la, ab_ref[dr]))


def _rwkv_prep(x, norm_w, mod, mu, wa2, aa2, wb_pad, ab_pad, w0, a0):
    nt = N_TOK // RW_TM
    hb = RW_TM // RW_HALO
    last = N_TOK // RW_HALO - 1
    full2 = lambda shape: pl.BlockSpec(shape, lambda i: (0, 0))
    full3 = lambda shape: pl.BlockSpec(shape, lambda i: (0, 0, 0))
    return pl.pallas_call(
        _rwkv_prep_kernel,
        grid=(nt,),
        in_specs=[
            pl.BlockSpec((RW_TM, D_MODEL), lambda i: (i, 0)),
            pl.BlockSpec((RW_HALO, D_MODEL), lambda i: (jnp.maximum(i * hb - 1, 0), 0)),
            pl.BlockSpec((RW_HALO, D_MODEL), lambda i: (jnp.minimum((i + 1) * hb, last), 0)),
            full2((1, D_MODEL)),
            pl.BlockSpec((1, 3, 1, D_MODEL), lambda i: (_cond_of_tile(i, RW_TM), 0, 0, 0)),
            full2((6, D_MODEL)),
            full2((D_MODEL, 2 * RWKV_RANK)),
            full2((D_MODEL, 2 * RWKV_RANK)),
            full3((2, 2 * RWKV_RANK, D_MODEL)),
            full3((2, 2 * RWKV_RANK, D_MODEL)),
            full2((2, D_MODEL)),
            full2((2, D_MODEL)),
        ],
        out_specs=[
            pl.BlockSpec((4, RW_TM, D_MODEL), lambda i: (0, i, 0)),
            pl.BlockSpec((2, RW_TM, D_MODEL), lambda i: (0, i, 0)),
            pl.BlockSpec((2, RW_TM, D_MODEL), lambda i: (0, i, 0)),
        ],
        out_shape=[
            jax.ShapeDtypeStruct((4, N_TOK, D_MODEL), BF16),
            jax.ShapeDtypeStruct((2, N_TOK, D_MODEL), F32),
            jax.ShapeDtypeStruct((2, N_TOK, D_MODEL), F32),
        ],
        compiler_params=_params("arbitrary"),
        name="rwkv_prep",
    )(x, x, x, norm_w.reshape(1, D_MODEL), mod, mu, wa2, aa2, wb_pad, ab_pad, w0, a0)


def _bmm_kernel(a_ref, w_ref, o_ref):
    o_ref[0] = _dot(a_ref[0], w_ref[...])


def _rwkv_rkvg(xm, w):
    tm = 1024
    return pl.pallas_call(
        _bmm_kernel,
        grid=(4, N_TOK // tm),
        in_specs=[
            pl.BlockSpec((1, tm, D_MODEL), lambda n, i: (n, i, 0)),
            pl.BlockSpec((D_MODEL, D_MODEL), lambda n, i: (0, n)),
        ],
        out_specs=pl.BlockSpec((1, tm, D_MODEL), lambda n, i: (n, i, 0)),
        out_shape=jax.ShapeDtypeStruct((4, N_TOK, D_MODEL), F32),
        compiler_params=_params("arbitrary", "arbitrary"),
        name="rwkv_rkvg",
    )(xm, w)


def _head_sum(x, first):
    s0 = jnp.sum(jnp.where(first, x, 0.0), axis=-1, keepdims=True)
    s1 = jnp.sum(jnp.where(first, 0.0, x), axis=-1, keepdims=True)
    return jnp.where(first, s0, s1)


def _stack_heads(x, first):
    xb = x.astype(BF16)
    zero = jnp.zeros_like(xb)
    return jnp.concatenate([jnp.where(first, xb, zero), jnp.where(first, zero, xb)], axis=0)


def _split3(x):
    hi = x.astype(BF16)
    r1 = x - hi.astype(F32)
    mid = r1.astype(BF16)
    return hi, mid, (r1 - mid.astype(F32)).astype(BF16)


def _rwkv_kernel(*refs, seq, zero_init, pg, np2):
    n_in = 7 if zero_init else 9
    rkv_ref, lw_ref, a_ref, kkp_ref, kap_ref, rkp_ref, gn_ref = refs[:7]
    if zero_init:
        o_ref, st_ref = refs[n_in:n_in + 2]
        scr = refs[n_in + 2:]
    else:
        s0_ref = refs[7]
        o_ref = refs[n_in]
        scr = refs[n_in + 1:]
    kk_scr, cum_scr, bon_scr, y_scr, s_scr, tar_scr, lrb_scr, b2_scr, w2_scr, yl_scr, kv_scr, pc_scr = scr
    c_len = RW_C
    n_ch = seq // c_len
    rows2 = 2 * c_len
    grp = pl.program_id(1)
    defer = np2 > pg
    base = grp * pg if defer else 0
    first = _first_half_lanes()

    rr = lax.broadcasted_iota(jnp.int32, (rows2, rows2), 0)
    cc = lax.broadcasted_iota(jnp.int32, (rows2, rows2), 1)
    eye = (rr == cc).astype(F32)

    def same(shift):
        return (rr >> shift) == (cc >> shift)

    head = same(6)
    strict = (head & (cc < rr), head & (cc > rr))
    incl = (head & (cc <= rr), head & (cc >= rr))
    last = (c_len - 1, 0)
    head_ones = head.astype(BF16)

    cs_rows = min(seq, 256)
    tr = lax.broadcasted_iota(jnp.int32, (cs_rows, cs_rows), 0)
    tc = lax.broadcasted_iota(jnp.int32, (cs_rows, cs_rows), 1)
    chunk = (tr >> 6) == (tc >> 6)
    tri = ((chunk & (tc <= tr)).astype(BF16), (chunk & (tc >= tr)).astype(BF16))
    for p in range(pg):
        ln = slice(p * LANES, (p + 1) * LANES)
        kk = rkv_ref[1, :, ln] * kkp_ref[:, ln]
        kk_scr[p] = kk * lax.rsqrt(jnp.maximum(_head_sum(kk * kk, first), 1e-12))

    def bonus_terms():
        for p in range(pg):
            ln = slice(p * LANES, (p + 1) * LANES)
            r = rkv_ref[0, :, ln]
            k = rkv_ref[1, :, ln]
            bonus = None
            for dr in range(2):
                kd = k * (1.0 + (a_ref[dr, :, ln] - 1.0) * kap_ref[:, ln])
                term = _head_sum(r * kd * rkp_ref[:, ln], first) * rkv_ref[2, :, ln]
                bonus = term if bonus is None else bonus + term
            bon_scr[base + p] = bonus

    for p in range(0, pg, 2):
        for dr in range(2):
            for r0 in range(0, seq, cs_rows):
                parts = _split3(lw_ref[dr, r0:r0 + cs_rows, p * LANES:(p + 2) * LANES])
                cum = _dot(tri[dr], parts[0]) + _dot(tri[dr], parts[1]) + _dot(tri[dr], parts[2])
                cum_scr[p, dr, r0:r0 + cs_rows, :] = cum[:, :LANES]
                cum_scr[p + 1, dr, r0:r0 + cs_rows, :] = cum[:, LANES:]

    def phase1(chains):
        dirs = [dr for _, dr, _ in chains]
        a2, r2, b2, k2, v2, pc = [], [], [], [], [], []
        for p, dr, c in chains:
            ln = slice(p * LANES, (p + 1) * LANES)
            rw = pl.ds(pl.multiple_of(c * c_len, c_len), c_len)
            a = a_ref[dr, rw, ln]
            k = rkv_ref[1, rw, ln]
            kk_c = kk_scr[p, rw, :]
            cum_c = cum_scr[p, dr, rw, :]
            e_inc = jnp.exp(cum_c)
            e_inv = jnp.exp(-cum_c)
            a2.append(_stack_heads(-kk_c * jnp.exp(cum_c - lw_ref[dr, rw, ln]), first))
            r2.append(_stack_heads(rkv_ref[0, rw, ln] * e_inc, first))
            b2.append(_stack_heads(kk_c * a * e_inv, first))
            k2.append(_stack_heads(k * (1.0 + (a - 1.0) * kap_ref[:, ln]) * e_inv, first))
            v2.append(_stack_heads(rkv_ref[2, rw, ln], first))
            pc.append(e_inc[last[dr]:last[dr] + 1, :])
        g = [_dot_nt(jnp.concatenate([x, y], axis=0), jnp.concatenate([z, w], axis=0))
             for x, y, z, w in zip(a2, r2, b2, k2)]
        l_ab = [jnp.where(strict[dr], x[:rows2, :rows2], 0.0) for dr, x in zip(dirs, g)]
        t = [eye + jnp.where(same(1), x, 0.0) for x in l_ab]
        l_ab16 = [x.astype(BF16) for x in l_ab]
        zero16 = jnp.zeros((rows2, rows2), BF16)
        side = {}
        for shift in range(1, 6):
            sib = same(shift + 1) & ~same(shift)
            tb = [x.astype(BF16) for x in t]
            mid = [_dot(jnp.where(sib, x, zero16), y) for x, y in zip(l_ab16, tb)]
            if shift == 1:
                side['lv'] = [_dot(jnp.where(strict[dr], x[:rows2, rows2:], 0.0).astype(BF16), y)
                              for dr, x, y in zip(dirs, g, v2)]
            elif shift == 2:
                side['yl'] = [_dot(jnp.where(incl[dr], x[rows2:, rows2:], 0.0).astype(BF16), y)
                              for dr, x, y in zip(dirs, g, v2)]
            elif shift == 3:
                side['kv'] = [_dot_tn(x, y) for x, y in zip(v2, k2)]
            t = [x + _dot(y, z.astype(BF16)) for x, y, z in zip(t, tb, mid)]
        tb = [x.astype(BF16) for x in t]
        ta = [_dot(x, y) for x, y in zip(tb, a2)]
        w2 = [_dot(x, y.astype(BF16)) for x, y in zip(tb, side['lv'])]
        for i, (p, dr, c) in enumerate(chains):
            n = ((base + p) * 2 + dr) * n_ch + c
            tar_scr[n, :rows2, :] = ta[i].astype(BF16)
            tar_scr[n, rows2:, :] = r2[i]
            w2_scr[n] = w2[i]
            yl_scr[n] = side['yl'][i]
            kv_scr[n] = side['kv'][i]
            lrb_scr[n] = jnp.where(incl[dr], g[i][rows2:, :rows2], 0.0).astype(BF16)
            b2_scr[n] = b2[i]
            pc_scr[n] = pc[i]

    def body1(cg, carry):
        phase1([(p, dr, cg * RW_LOCK + j) for p in range(pg) for j in range(RW_LOCK) for dr in range(2)])
        return carry

    if n_ch == RW_LOCK:
        body1(0, 0)
    else:
        lax.fori_loop(0, n_ch // RW_LOCK, body1, 0)
    bonus_terms()

    def finish():
        for p in range(np2):
            for dr in range(2):
                if zero_init:
                    s_scr[2 * p + dr] = jnp.zeros((rows2, LANES), F32)
                else:
                    s_scr[2 * p + dr] = s0_ref[0, dr, p]

        def body2(i, carry):
            cs = (i, n_ch - 1 - i)
            ids = [(p, dr) for p in range(np2) for dr in range(2)]
            ns = [(p * 2 + dr) * n_ch + cs[dr] for p, dr in ids]
            x = [_dot_nt(tar_scr[n], s_scr[2 * p + dr].astype(BF16)) for n, (p, dr) in zip(ns, ids)]
            u2 = [(xx[:rows2] + w2_scr[n]).astype(BF16) for n, xx in zip(ns, x)]
            upd = [_dot_tn(u, b2_scr[n]) for n, u in zip(ns, u2)]
            yb = [_dot(lrb_scr[n], u) for n, u in zip(ns, u2)]
            for j, (n, (p, dr)) in enumerate(zip(ns, ids)):
                y2 = x[j][rows2:] + yb[j] + yl_scr[n]
                y_scr[p, dr, pl.ds(pl.multiple_of(cs[dr] * c_len, c_len), c_len), :] = y2[:c_len] + y2[c_len:]
                s_scr[2 * p + dr] = (s_scr[2 * p + dr] + upd[j] + kv_scr[n]) * pc_scr[n]
            return carry

        lax.fori_loop(0, n_ch, body2, 0)

        def head_mean(xs):
            parts = [_split3(x) for x in xs]
            return [(_dot(a, head_ones) + _dot(b, head_ones) + _dot(c, head_ones)) * (1.0 / RWKV_HD)
                    for a, b, c in parts]

        ys = [y_scr[p, 0] + y_scr[p, 1] for p in range(np2)]
        yc = [y - m for y, m in zip(ys, head_mean(ys))]
        var = head_mean([x * x for x in yc])
        for p in range(np2):
            ln = slice(p * LANES, (p + 1) * LANES)
            o_ref[:, ln] = yc[p] * lax.rsqrt(var[p] + GN_EPS) * gn_ref[:, ln] + bon_scr[p]
            if zero_init:
                for dr in range(2):
                    s2 = s_scr[2 * p + dr]
                    st_ref[0, 0, dr, 2 * p] = s2[:RWKV_HD, :RWKV_HD]
                    st_ref[0, 0, dr, 2 * p + 1] = s2[RWKV_HD:, RWKV_HD:]

    if defer:
        pl.when(grp == pl.num_programs(1) - 1)(finish)
    else:
        finish()


def _rwkv_mixer(rkvg, lw, a, kkp, kap, rkp, gn, s0_pairs):
    npair = RWKV_HEADS // 2
    r2 = 2 * RW_C

    def scratch(seq, pg, np2):
        n = 2 * np2 * (seq // RW_C)
        return [
            pltpu.VMEM((pg, seq, LANES), F32), pltpu.VMEM((pg, 2, seq, LANES), F32),
            pltpu.VMEM((np2, seq, LANES), F32), pltpu.VMEM((np2, 2, seq, LANES), F32),
            pltpu.VMEM((2 * np2, r2, LANES), F32),
            pltpu.VMEM((n, 2 * r2, LANES), BF16), pltpu.VMEM((n, r2, r2), BF16), pltpu.VMEM((n, r2, LANES), BF16),
            pltpu.VMEM((n, r2, LANES), F32), pltpu.VMEM((n, r2, LANES), F32), pltpu.VMEM((n, r2, LANES), F32),
            pltpu.VMEM((n, 1, LANES), F32),
        ]

    def seq_specs(seq, rb, pg, gn_spec):
        wl = pg * LANES
        par =pl.BlockSpec((1, wl), lambda s, g: (0, g))
        return [
            pl.BlockSpec((3, seq, wl), lambda s, g: (0, rb + s, g)),
            pl.BlockSpec((2, seq, wl), lambda s, g: (0, rb + s, g)),
            pl.BlockSpec((2, seq, wl), lambda s, g: (0, rb + s, g)),
            par, par, par, gn_spec,
        ]

    pars = [u.reshape(1, D_MODEL) for u in (kkp, kap, rkp, gn)]
    pg = RW_PAIRS_PROMPT
    o_p, st = pl.pallas_call(
        functools.partial(_rwkv_kernel, seq=SEQ, zero_init=True, pg=pg, np2=npair),
        grid=(BATCH, npair // pg),
        in_specs=seq_specs(SEQ, 0, pg, pl.BlockSpec((1, D_MODEL), lambda s, g: (0, 0))),
        out_specs=[
            pl.BlockSpec((SEQ, D_MODEL), lambda s, g: (s, 0)),
            pl.BlockSpec((1, 1, 2, RWKV_HEADS, RWKV_HD, RWKV_HD), lambda s, g: (s, 0, 0, 0, 0, 0)),
        ],
        out_shape=[
            jax.ShapeDtypeStruct((N_TOK, D_MODEL), F32),
            jax.ShapeDtypeStruct((BATCH, 1, 2, RWKV_HEADS, RWKV_HD, RWKV_HD), F32),
        ],
        scratch_shapes=scratch(SEQ, pg, npair),
        compiler_params=_params("arbitrary", "arbitrary"),
        name="rwkv_prompt",
    )(rkvg, lw, a, *pars)
    rb = N_PROMPT_TOK // DEC_SEQ
    pg = RW_PAIRS
    wl = pg * LANES
    o = pl.pallas_call(
        functools.partial(_rwkv_kernel, seq=DEC_SEQ, zero_init=False, pg=pg, np2=pg),
        grid=(DEC_BATCH, npair // pg),
        in_specs=seq_specs(DEC_SEQ, rb, pg, pl.BlockSpec((1, wl), lambda s, g: (0, g))) + [
            pl.BlockSpec((1, 2, pg, LANES, LANES), lambda s, g: (s, 0, g, 0, 0)),
            pl.BlockSpec(memory_space=pl.ANY),
        ],
        out_specs=pl.BlockSpec((DEC_SEQ, wl), lambda s, g: (rb + s, g)),
        out_shape=jax.ShapeDtypeStruct((N_TOK, D_MODEL), F32),
        input_output_aliases={8: 0},
        scratch_shapes=scratch(DEC_SEQ, pg, pg),
        compiler_params=_params("arbitrary", "arbitrary"),
        name="rwkv_latent",
    )(rkvg, lw, a, *pars, s0_pairs, o_p)
    return o, st


def _state_pairs(s0):
    s = s0.reshape(DEC_BATCH, 2, RWKV_HEADS // 2, 2, RWKV_HD, RWKV_HD)
    z = jnp.zeros_like(s[:, :, :, 0])
    top = jnp.concatenate([s[:, :, :, 0], z], axis=-1)
    bot = jnp.concatenate([z, s[:, :, :, 1]], axis=-1)
    return jnp.concatenate([top, bot], axis=-2)


def _layer_rwkv(x, p, mod, j):
    i = N_MIXERS * j + 1
    wa, wb, aa, ab = p['rwkv_wA'][j], p['rwkv_wB'][j], p['rwkv_aA'][j], p['rwkv_aB'][j]
    z = jnp.zeros_like(wb[0])
    wa2 = jnp.concatenate([wa[0], wa[1]], axis=1).astype(BF16)
    aa2 = jnp.concatenate([aa[0], aa[1]], axis=1).astype(BF16)
    wb_pad = jnp.stack([jnp.concatenate([wb[0], z]), jnp.concatenate([z, wb[1]])]).astype(BF16)
    ab_pad = jnp.stack([jnp.concatenate([ab[0], z]), jnp.concatenate([z, ab[1]])]).astype(BF16)
    xm, lw, a = _rwkv_prep(x, p['norm_w'][i], mod, p['rwkv_mu'][j], wa2, aa2, wb_pad, ab_pad,
                           p['rwkv_w0'][j], p['rwkv_a0'][j])
    rkvg = _rwkv_rkvg(xm, p['rwkv_w_in'][j].astype(BF16))
    o, st = _rwkv_mixer(rkvg, lw, a, p['rwkv_kk'][j], p['rwkv_ka'][j], p['rwkv_rk'][j], p['rwkv_gn'][j],
                          _state_pairs(p['state_rwkv'][:, j]))
    x = _out_proj(o, rkvg.reshape(4 * N_TOK, D_MODEL), 0, p['rwkv_w_out'][j], x, mod, p['final_norm_w'], False,
                  g_row0=3 * N_TOK)
    return x, st


DIFF_W = 2 * DIFF_HD
ATT_QB = 256
DIFF_GROUP = 4
ATT_SEQS = 2


def _first_half_lanes():
    return lax.broadcasted_iota(jnp.int32, (1, LANES), 1) < LANES // 2


def _diff_lambda(lam_ref, lam_init):
    lp = lam_ref[...]
    return (jnp.exp(jnp.sum(lp[0:1] * lp[1:2], keepdims=True))
            - jnp.exp(jnp.sum(lp[2:3] * lp[3:4], keepdims=True)) + lam_init)


def _diff_heads(items, lam, lam_init):
    first = _first_half_lanes()
    scale = DIFF_HD ** -0.5
    sub = [(q, keys, comp) for q, keys, _ in items for comp in range(2)]
    qm = [(jnp.where(first if comp == 0 else ~first, q, 0.0) * scale).astype(BF16) for q, _, comp in sub]
    s = [[_dot_nt(x, kb) for kb, _ in keys] for x, (_, keys, _) in zip(qm, sub)]
    m = [functools.reduce(jnp.maximum, [jnp.max(u, axis=-1, keepdims=True) for u in ss]) for ss in s]
    e = [[jnp.exp(u - mm) for u in ss] for ss, mm in zip(s, m)]
    inv = [1.0 / functools.reduce(lambda x, y: x + y, [jnp.sum(u, axis=-1, keepdims=True) for u in ee]) for ee in e]
    outs = []
    for i, (_, keys, gn) in enumerate(items):
        o = None
        lam_inv = lam * inv[2 * i + 1]
        for n, (_, vb) in enumerate(keys):
            p = e[2 * i][n] * inv[2 * i] - e[2 * i + 1][n] * lam_inv
            part = _dot(p.astype(BF16), vb)
            o = part if o is None else o + part
        outs.append(o)
    return [o * lax.rsqrt(jnp.mean(o * o, axis=-1, keepdims=True) + EPS) * gn * (1.0 - lam_init)
            for o, (_, _, gn) in zip(outs, items)]


def _diff_prompt_kernel(lam_ref, q_ref, k_ref, v_ref, gn_ref, o_ref, *, lam_init):
    lam = _diff_lambda(lam_ref, lam_init)
    for s in range(ATT_SEQS):
        rw = slice(s * SEQ, (s + 1) * SEQ)
        for h0 in range(0, DIFF_HEADS, DIFF_GROUP):
            items = []
            for h in range(h0, h0 + DIFF_GROUP):
                sl = slice(h * DIFF_W, (h + 1) * DIFF_W)
                keys = [(k_ref[rw, sl].astype(BF16), v_ref[rw, sl].astype(BF16))]
                items.append((q_ref[rw, sl], keys, gn_ref[:, sl]))
            for h, o in zip(range(h0, h0 + DIFF_GROUP), _diff_heads(items, lam, lam_init)):
                o_ref[rw, h * DIFF_W:(h + 1) * DIFF_W] = o


def _diff_latent_kernel(lam_ref, q_ref, k_ref, v_ref, ck_ref, cv_ref, cos_ref, slo_ref, shi_ref, gn_ref,
                        _prev_ref, o_ref, *, lam_init):
    lam = _diff_lambda(lam_ref, lam_init)
    tabs = (cos_ref[...], slo_ref[...], shi_ref[...])
    q = _rope(q_ref[...].astype(F32), *tabs, DIFF_HD // 4)
    k = _rope(k_ref[...].astype(F32), *tabs, DIFF_HD // 4)
    keys = [(k.astype(BF16), v_ref[...].astype(BF16)),
            (ck_ref[0, 0, 0].astype(BF16), cv_ref[0, 0, 0].astype(BF16))]
    gn = gn_ref[...]
    n_blk = DEC_SEQ // ATT_QB
    items = [(q[qi * ATT_QB:(qi + 1) * ATT_QB], keys, gn) for qi in range(n_blk)]
    for qi, o in enumerate(_diff_heads(items, lam, lam_init)):
        o_ref[qi * ATT_QB:(qi + 1) * ATT_QB, :] = o


def _diff_attention(proj, lam_p, gn_w, cache_k, cache_v, j, lam_init):
    gn = gn_w.reshape(1, D_MODEL)
    lam_spec = pl.BlockSpec((4, DIFF_HD), lambda *_: (0, 0))
    o_p = pl.pallas_call(
        functools.partial(_diff_prompt_kernel, lam_init=lam_init),
        grid=(BATCH // ATT_SEQS,),
        in_specs=[
            lam_spec,
            pl.BlockSpec((ATT_SEQS * SEQ, D_MODEL), lambda b: (b, 0)),
            pl.BlockSpec((ATT_SEQS * SEQ, D_MODEL), lambda b: (b, 1)),
            pl.BlockSpec((ATT_SEQS * SEQ, D_MODEL), lambda b: (b, 2)),
            pl.BlockSpec((1, D_MODEL), lambda b: (0, 0)),
        ],
        out_specs=pl.BlockSpec((ATT_SEQS * SEQ, D_MODEL), lambda b: (b, 0)),
        out_shape=jax.ShapeDtypeStruct((N_TOK, D_MODEL), F32),
        compiler_params=_params("arbitrary"),
        name="diff_prompt",
    )(lam_p, proj, proj, proj, gn)
    cos, slo, shi = (jnp.concatenate([u, u], axis=-1) for u in _rope_tables(DIFF_HD))
    rb = N_PROMPT_TOK // DEC_SEQ
    nh = DIFF_HEADS
    tab = pl.BlockSpec((DEC_SEQ, DIFF_W), lambda b, h: (0, 0))
    cache = pl.BlockSpec((1, 1, 1, PAST_LEN, DIFF_W), lambda b, h: (b, j, h, 0, 0))
    o = pl.pallas_call(
        functools.partial(_diff_latent_kernel, lam_init=lam_init),
        grid=(DEC_BATCH, nh),
        in_specs=[
            lam_spec,
            pl.BlockSpec((DEC_SEQ, DIFF_W), lambda b, h: (rb + b, h)),
            pl.BlockSpec((DEC_SEQ, DIFF_W), lambda b, h: (rb + b, nh + h)),
            pl.BlockSpec((DEC_SEQ, DIFF_W), lambda b, h: (rb + b, 2 * nh + h)),
            cache, cache, tab, tab, tab,
            pl.BlockSpec((1, DIFF_W), lambda b, h: (0, h)),
            pl.BlockSpec(memory_space=pl.ANY),
        ],
        out_specs=pl.BlockSpec((DEC_SEQ, DIFF_W), lambda b, h: (rb + b, h)),
        out_shape=jax.ShapeDtypeStruct((N_TOK, D_MODEL), F32),
        input_output_aliases={10: 0},
        compiler_params=_params("arbitrary", "arbitrary"),
        name="diff_latent",
    )(lam_p, proj, proj, proj, cache_k, cache_v, cos, slo, shi, gn, o_p)
    return o


def _layer_diff(x, p, mod, j, i):
    lam_init = 0.8 - 0.6 * math.exp(-0.3 * i)
    proj, new_k, new_v = _in_proj_kv(x, p['norm_w'][i], mod, p['diff_w_in'][j], DIFF_HEADS)
    o = _diff_attention(proj, p['diff_lambda'][j], p['diff_gn'][j], p['cache_diff_k'], p['cache_diff_v'], j,
                        lam_init)
    x = _out_proj(o, proj, 3, p['diff_w_out'][j], x, mod, p['final_norm_w'], False)
    return x, new_k, new_v


NA_ROWS = DEC_SEQ // GRID_W
NA_WR = min(NA_WIN_R, NA_ROWS)
NA_LOC = NA_WR * GRID_W
NA_ROW_GROUP = 4
NA_PAIR_GROUP = 2


def _na_prompt_kernel(q_ref, k_ref, v_ref, o_ref):
    first = _first_half_lanes()
    scale = NA_HD ** -0.5
    for rw, p0 in [(slice(n * SEQ, (n + 1) * SEQ), p0) for n in range(ATT_SEQS)
                   for p0 in range(0, NA_HEADS // 2, NA_PAIR_GROUP)]:
        pairs = range(p0, p0 + NA_PAIR_GROUP)
        kb = [k_ref[rw, pr * LANES:(pr + 1) * LANES].astype(BF16) for pr in pairs]
        vb = [v_ref[rw, pr * LANES:(pr + 1) * LANES].astype(BF16) for pr in pairs]
        items = [(i, half) for i in range(NA_PAIR_GROUP) for half in range(2)]
        qm = [(jnp.where(first if half == 0 else ~first, q_ref[rw, (p0 + i) * LANES:(p0 + i + 1) * LANES], 0.0)
               * scale).astype(BF16) for i, half in items]
        s = [_dot_nt(x, kb[i]) for x, (i, _) in zip(qm, items)]
        e = [jnp.exp(x - jnp.max(x, axis=-1, keepdims=True)) for x in s]
        inv = [1.0 / jnp.sum(x, axis=-1, keepdims=True) for x in e]
        outs = [_dot(x.astype(BF16), vb[i]) * z for x, z, (i, _) in zip(e, inv, items)]
        for i in range(NA_PAIR_GROUP):
            o_ref[rw, (p0 + i) * LANES:(p0 + i + 1) * LANES] = jnp.where(first, outs[2 * i], outs[2 * i + 1])


def _na_latent_kernel(q_ref, k_ref, v_ref, kc_ref, vc_ref, tab_ref, _prev_ref, o_ref):
    first = _first_half_lanes()
    scale = NA_HD ** -0.5
    kb = k_ref[...].astype(BF16)
    vb = v_ref[...].astype(BF16)
    kcb = kc_ref[0, 0].astype(BF16)
    vcb = vc_ref[0, 0].astype(BF16)
    qcol = lax.broadcasted_iota(jnp.int32, (GRID_W, NA_LOC), 0)
    kcol = lax.broadcasted_iota(jnp.int32, (GRID_W, NA_LOC), 1) & (GRID_W - 1)
    cstart = jnp.clip(qcol - NA_WIN_C // 2, 0, GRID_W - NA_WIN_C)
    col_ok = (kcol >= cstart) & (kcol < cstart + NA_WIN_C)
    def bias_of(r, rs, half):
        parts = []
        for w in range(0, NA_WR, 2):
            src = jnp.broadcast_to(tab_ref[half, rs + w - r + NA_WIN_R - 1], (GRID_W, LANES))
            parts.append(pltpu.roll(src, LANES - (NA_WIN_C - 1), axis=1, stride=1, stride_axis=0))
        return jnp.concatenate(parts, axis=1)

    for r0 in range(0, NA_ROWS, NA_ROW_GROUP):
        items = [(r, min(max(r - NA_WR // 2, 0), NA_ROWS - NA_WR), half)
                 for r in range(r0, r0 + NA_ROW_GROUP) for half in range(2)]
        qm = [(jnp.where(first if half == 0 else ~first, q_ref[r * GRID_W:(r + 1) * GRID_W, :], 0.0)
               * scale).astype(BF16) for r, _, half in items]
        s_loc = [_dot_nt(x, kb[rs * GRID_W:(rs + NA_WR) * GRID_W]) for x, (_, rs, _) in zip(qm, items)]
        s_ctx = [_dot_nt(x, kcb) for x in qm]
        s_loc = [jnp.where(col_ok, x + bias_of(*it), -jnp.inf) for x, it in zip(s_loc, items)]
        m = [jnp.maximum(jnp.max(x, axis=-1, keepdims=True), jnp.max(y, axis=-1, keepdims=True))
             for x, y in zip(s_loc, s_ctx)]
        e_loc = [jnp.exp(x - mm) for x, mm in zip(s_loc, m)]
        e_ctx = [jnp.exp(x - mm) for x, mm in zip(s_ctx, m)]
        inv = [1.0 / (jnp.sum(x, axis=-1, keepdims=True) + jnp.sum(y, axis=-1, keepdims=True))
               for x, y in zip(e_loc, e_ctx)]
        pv = [_dot(x.astype(BF16), vb[rs * GRID_W:(rs + NA_WR) * GRID_W]) for x, (_, rs, _) in zip(e_loc, items)]
        pc = [_dot(x.astype(BF16), vcb) for x in e_ctx]
        outs = [(x + y) * z for x, y, z in zip(pv, pc, inv)]
        for n in range(0, len(items), 2):
            r = items[n][0]
            o_ref[r * GRID_W:(r + 1) * GRID_W, :] = jnp.where(first, outs[n], outs[n + 1])


def _na_bias_pairs(table):
    t = table.astype(F32)
    nc = 2 * NA_WIN_C - 1
    z = jnp.zeros(t[:, :-1].shape[:2] + (GRID_W - nc,), F32)
    return jnp.concatenate([t[:, :-1], z, t[:, 1:], z], axis=-1)[:, :, None, :]


def _pair_heads(cache):
    c = cache.reshape(DEC_BATCH, NA_HEADS // 2, 2, PAST_LEN, NA_HD)
    return c.transpose(0, 1, 3, 2, 4).reshape(DEC_BATCH, NA_HEADS // 2, PAST_LEN, LANES)


def _na_attention(proj, bias_table, cache_k, cache_v):
    o_p = pl.pallas_call(
        _na_prompt_kernel,
        grid=(BATCH // ATT_SEQS,),
        in_specs=[
            pl.BlockSpec((ATT_SEQS * SEQ, D_MODEL), lambda b: (b, 0)),
            pl.BlockSpec((ATT_SEQS * SEQ, D_MODEL), lambda b: (b, 1)),
            pl.BlockSpec((ATT_SEQS * SEQ, D_MODEL), lambda b: (b, 2)),
        ],
        out_specs=pl.BlockSpec((ATT_SEQS * SEQ, D_MODEL), lambda b: (b, 0)),
        out_shape=jax.ShapeDtypeStruct((N_TOK, D_MODEL), F32),
        compiler_params=_params("arbitrary"),
        name="na_prompt",
    )(proj, proj, proj)
    rb = N_PROMPT_TOK // DEC_SEQ
    npair = NA_HEADS // 2
    cache = pl.BlockSpec((1, 1, PAST_LEN, LANES), lambda pr, b: (b, pr, 0, 0))
    o = pl.pallas_call(
        _na_latent_kernel,
        grid=(npair, DEC_BATCH),
        in_specs=[
            pl.BlockSpec((DEC_SEQ, LANES), lambda pr, b: (rb + b, pr)),
            pl.BlockSpec((DEC_SEQ, LANES), lambda pr, b: (rb + b, npair + pr)),
            pl.BlockSpec((DEC_SEQ, LANES), lambda pr, b: (rb + b, 2 * npair + pr)),
            cache, cache,
            pl.BlockSpec((2, 2 * NA_WIN_R - 2, 1, LANES), lambda pr, b: (pr, 0, 0, 0)),
            pl.BlockSpec(memory_space=pl.ANY),
        ],
        out_specs=pl.BlockSpec((DEC_SEQ, LANES), lambda pr, b: (rb + b, pr)),
        out_shape=jax.ShapeDtypeStruct((N_TOK, D_MODEL), F32),
        input_output_aliases={6: 0},
        compiler_params=_params("arbitrary", "arbitrary"),
        name="na_latent",
    )(proj, proj, proj, _pair_heads(cache_k), _pair_heads(cache_v), _na_bias_pairs(bias_table), o_p)
    return o


def _layer_na(x, p, mod, j, final):
    i = N_MIXERS * j + 3
    proj, new_k, new_v = _in_proj_kv(x, p['norm_w'][i], mod, p['na_w_in'][j], NA_HEADS)
    o = _na_attention(proj, p['na_bias'][j], p['cache_na_k'][:, j], p['cache_na_v'][:, j])
    args = (o, proj, 3, p['na_w_out'][j], x, mod, p['final_norm_w'])
    if final:
        x = (_out_proj(*args, True, rows=(0, N_PROMPT_TOK)), _out_proj(*args, True, rows=(N_PROMPT_TOK, N_TOK)))
    else:
        x = _out_proj(*args, False)
    return x, new_k, new_v


def kernel(x_prompt, x_sample, state_ret, state_rwkv, cache_diff_k, cache_diff_v, cache_na_k, cache_na_v,
           c, c_ctx, norm_w, w_mod, b_mod, final_norm_w,
           ret_w_in, ret_decay, ret_gn, ret_w_out,
           rwkv_mu, rwkv_w_in, rwkv_w0, rwkv_wA, rwkv_wB, rwkv_a0, rwkv_aA, rwkv_aB,
           rwkv_kk, rwkv_ka, rwkv_rk, rwkv_gn, rwkv_w_out,
           diff_w_in, diff_lambda, diff_gn, diff_w_out,
           na_w_in, na_bias, na_w_out):
    p = dict(locals())
    cond = jnp.zeros((N_COND, D_MODEL), F32).at[0].set(c_ctx).at[1:1 + DEC_BATCH].set(c)
    mods = _modulation(cond, w_mod, b_mod)
    x = (x_prompt.reshape(N_PROMPT_TOK, D_MODEL), x_sample.reshape(N_SAMPLE_TOK, D_MODEL))
    new = {n: [] for n in ('ret', 'rwkv', 'dk', 'dv', 'nk', 'nv')}
    for i in range(DEPTH):
        kind, j = i % N_MIXERS, i // N_MIXERS
        if kind == 0:
            x, st = _layer_ret(x, p, mods[i], j)
            new['ret'].append(st)
        elif kind == 1:
            x, st = _layer_rwkv(x, p, mods[i], j)
            new['rwkv'].append(st)
        elif kind == 2:
            x, ck, cv = _layer_diff(x, p, mods[i], j, i)
            new['dk'].append(ck)
            new['dv'].append(cv)
        else:
            x, ck, cv = _layer_na(x, p, mods[i], j, final=(i == DEPTH - 1))
            new['nk'].append(ck)
            new['nv'].append(cv)
    if DEPTH % N_MIXERS:
        raise NotImplementedError("the final norm is fused into the last neighbourhood-attention layer")
    cat = lambda xs: xs[0] if len(xs) == 1 else jnp.concatenate(xs, axis=1)
    return (x[0].reshape(BATCH, SEQ, D_MODEL), x[1].reshape(DEC_BATCH, DEC_SEQ, D_MODEL),
            cat(new['ret']), cat(new['rwkv']), cat(new['dk']), cat(new['dv']), cat(new['nk']), cat(new['nv']))
```

```python
import functools
import math

import jax
import jax.numpy as jnp
from jax import lax
from jax.experimental import pallas as pl
from jax.experimental.pallas import tpu as pltpu

F32 = jnp.float32
BF16 = jnp.bfloat16

D_MODEL = 1024
BATCH = 32
SEQ = 256
DEPTH = 4
N_MIXERS = 4
DEC_BATCH = 2
DEC_SEQ = 1024
PAST_LEN = 256
GRID_W = 64

RET_HEADS = 4
RET_DK = 256
RET_DV = 512
RET_QK = 1024
RET_V = 2048

RWKV_HD = 64
RWKV_HEADS = 16
RWKV_RANK = 64

DIFF_HEADS = 8
DIFF_HD = 64

NA_HEADS = 16
NA_HD = 64
NA_WIN_R = 8
NA_WIN_C = 16

ROPE_BASE = 10000.0
EPS = 1e-6
GN_EPS = 1e-5

N_PROMPT_TOK = BATCH * SEQ
N_SAMPLE_TOK = DEC_BATCH * DEC_SEQ
N_TOK = N_PROMPT_TOK + N_SAMPLE_TOK
N_COND = 8

MIX_DTYPE = BF16
LANES = 128
VMEM_LIMIT = 56 * 2 ** 20


def _params(*sem):
    return pltpu.CompilerParams(dimension_semantics=sem, vmem_limit_bytes=VMEM_LIMIT)


def _cond_of_tile(i, tm):
    npt = N_PROMPT_TOK // tm
    return jnp.where(i < npt, 0, 1 + (i - npt) // (DEC_SEQ // tm))


def _sigmoid(x):
    return 1.0 / (1.0 + jnp.exp(-x))


def _silu(x):
    return x * _sigmoid(x)


def _dot(a, b):
    return jnp.dot(a, b, preferred_element_type=F32)


def _dot_nt(a, b):
    return lax.dot_general(a, b, (((1,), (1,)), ((), ())), preferred_element_type=F32)


def _dot_tn(a, b):
    return lax.dot_general(a, b, (((0,), (0,)), ((), ())), preferred_element_type=F32)


def _mod_kernel(c_ref, w_ref, b_ref, o_ref):
    s = _silu(c_ref[...])
    o_ref[0] = jnp.dot(s, w_ref[0], precision=lax.Precision.HIGHEST, preferred_element_type=F32) + b_ref[0]


def _modulation(cond, w_mod, b_mod):
    tn = D_MODEL
    out = pl.pallas_call(
        _mod_kernel,
        grid=(DEPTH, 3 * D_MODEL // tn),
        in_specs=[
            pl.BlockSpec((N_COND, D_MODEL), lambda l, j: (0, 0)),
            pl.BlockSpec((1, D_MODEL, tn), lambda l, j: (l, 0, j)),
            pl.BlockSpec((1, 1, tn), lambda l, j: (l, 0, j)),
        ],
        out_specs=pl.BlockSpec((1, N_COND, tn), lambda l, j: (l, 0, j)),
        out_shape=jax.ShapeDtypeStruct((DEPTH, N_COND, 3 * D_MODEL), F32),
        compiler_params=_params("arbitrary", "arbitrary"),
        name="modulation",
    )(cond, w_mod, b_mod.reshape(DEPTH, 1, 3 * D_MODEL))
    return out.reshape(DEPTH, N_COND, 3, 1, D_MODEL)


def _norm_mod(x, nw, mod_ref):
    ms = jnp.mean(x * x, axis=-1, keepdims=True)
    y = x * lax.rsqrt(ms + EPS) * nw
    return y * (1.0 + mod_ref[0, 1]) + mod_ref[0, 0]


IN_TM = 1024
IN_TN = 2048


def _x_specs(x, tm, tile_of):
    if not isinstance(x, tuple):
        return [pl.BlockSpec((tm, D_MODEL), lambda *g: (tile_of(*g), 0))], (x,)
    npt = N_PROMPT_TOK // tm
    return [pl.BlockSpec((tm, D_MODEL), lambda *g: (jnp.minimum(tile_of(*g), npt - 1), 0)),
            pl.BlockSpec((tm, D_MODEL), lambda *g: (jnp.maximum(tile_of(*g) - npt, 0), 0))], x


def _read_x(x_refs, tile, tm):
    if len(x_refs) == 1:
        return x_refs[0][...]
    return jnp.where(tile < N_PROMPT_TOK // tm, x_refs[0][...], x_refs[1][...])


def _in_proj_kernel(*refs, n_x):
    x_refs = refs[:n_x]
    nw_ref, mod_ref, w_ref, o_ref, h_ref = refs[n_x:]

    @pl.when(pl.program_id(1) == 0)
    def _():
        x = _read_x(x_refs, pl.program_id(0), IN_TM)
        h_ref[...] = _norm_mod(x, nw_ref[...], mod_ref).astype(BF16)

    o_ref[...] = _dot(h_ref[...], w_ref[...]).astype(o_ref.dtype)


def _in_proj(x, norm_w, mod, w, tn, out_dtype=F32):
    n = w.shape[1]
    w = w.astype(BF16)
    x_specs, xs = _x_specs(x, IN_TM, lambda i, j: i)
    return pl.pallas_call(
        functools.partial(_in_proj_kernel, n_x=len(xs)),
        grid=(N_TOK // IN_TM, n // tn),
        in_specs=x_specs + [
            pl.BlockSpec((1, D_MODEL), lambda i, j: (0, 0)),
            pl.BlockSpec((1, 3, 1, D_MODEL), lambda i, j: (_cond_of_tile(i, IN_TM), 0, 0, 0)),
            pl.BlockSpec((D_MODEL, tn), lambda i, j: (0, j)),
        ],
        out_specs=pl.BlockSpec((IN_TM, tn), lambda i, j: (i, j)),
        out_shape=jax.ShapeDtypeStruct((N_TOK, n), out_dtype),
        scratch_shapes=[pltpu.VMEM((IN_TM, D_MODEL), BF16)],
        compiler_params=_params("arbitrary", "arbitrary"),
        name="in_proj",
    )(*xs, norm_w.reshape(1, D_MODEL), mod, w)


def _in_proj_kv_kernel(x_ref, nw_ref, mod_ref, w_ref, o_ref, ck_ref, cv_ref, h_ref, *, heads, tn):
    i = pl.program_id(0)
    j = pl.program_id(1)

    @pl.when(j == 0)
    def _():
        h_ref[...] = _norm_mod(x_ref[...], nw_ref[...], mod_ref).astype(BF16)

    acc = _dot(h_ref[...], w_ref[...])
    o_ref[...] = acc.astype(o_ref.dtype)
    hd = D_MODEL // heads
    for col, c_ref in ((D_MODEL, ck_ref), (2 * D_MODEL, cv_ref)):
        @pl.when((j == col // tn) & (i < N_PROMPT_TOK // IN_TM))
        def _(c_ref=c_ref, c0=col % tn):
            for s in range(IN_TM // SEQ):
                for h in range(heads):
                    c_ref[s, 0, h] = acc[s * SEQ:(s + 1) * SEQ, c0 + h * hd:c0 + (h + 1) * hd]


def _in_proj_kv(x, norm_w, mod, w, heads):
    n = w.shape[1]
    tn = D_MODEL
    spb = IN_TM // SEQ
    last = N_PROMPT_TOK // IN_TM - 1
    cache = pl.BlockSpec((spb, 1, heads, SEQ, D_MODEL // heads), lambda i, j: (jnp.minimum(i, last), 0, 0, 0, 0))
    cache_shape = jax.ShapeDtypeStruct((BATCH, 1, heads, SEQ, D_MODEL // heads), F32)
    return pl.pallas_call(
        functools.partial(_in_proj_kv_kernel, heads=heads, tn=tn),
        grid=(N_TOK // IN_TM, n // tn),
        in_specs=[
            pl.BlockSpec((IN_TM, D_MODEL), lambda i, j: (i, 0)),
            pl.BlockSpec((1, D_MODEL), lambda i, j: (0, 0)),
            pl.BlockSpec((1, 3, 1, D_MODEL), lambda i, j: (_cond_of_tile(i, IN_TM), 0, 0, 0)),
            pl.BlockSpec((D_MODEL, tn), lambda i, j: (0, j)),
        ],
        out_specs=[pl.BlockSpec((IN_TM, tn), lambda i, j: (i, j)), cache, cache],
        out_shape=[jax.ShapeDtypeStruct((N_TOK, n), BF16), cache_shape, cache_shape],
        scratch_shapes=[pltpu.VMEM((IN_TM, D_MODEL), BF16)],
        compiler_params=_params("arbitrary", "arbitrary"),
        name="in_proj_kv",
    )(x, norm_w.reshape(1, D_MODEL), mod, w.astype(BF16))


OUT_TM_BYTES = 4 * 2 ** 20


def _out_proj_kernel(*refs, n_x, t0, tm, final):
    x_refs = refs[:n_x]
    o_ref, g_ref, w_ref, mod_ref, fw_ref, y_ref, wb_ref = refs[n_x:]

    @pl.when(pl.program_id(0) == 0)
    def _():
        wb_ref[...] = w_ref[...].astype(BF16)

    a = (o_ref[...].astype(F32) * _silu(g_ref[...].astype(F32))).astype(BF16)
    xn = _read_x(x_refs, t0 + pl.program_id(0), tm) + mod_ref[0, 2] * _dot(a, wb_ref[...])
    if final:
        ms = jnp.mean(xn * xn, axis=-1, keepdims=True)
        xn = xn * lax.rsqrt(ms + EPS) * fw_ref[...]
    y_ref[...] = xn


def _out_proj(o, g_arr, g_blk, w, x, mod, final_w, final, rows=(0, N_TOK), g_row0=0):
    k = w.shape[0]
    tm = OUT_TM_BYTES // (4 * k)
    t0 = rows[0] // tm
    g0 = g_row0 // tm
    x_specs, xs = _x_specs(x, tm, lambda i: t0 + i)
    return pl.pallas_call(
        functools.partial(_out_proj_kernel, n_x=len(xs), t0=t0, tm=tm, final=final),
        grid=((rows[1] - rows[0]) // tm,),
        in_specs=x_specs + [
            pl.BlockSpec((tm, k), lambda i: (t0 + i, 0)),
            pl.BlockSpec((tm, k), lambda i: (g0 + t0 + i, g_blk)),
            pl.BlockSpec((k, D_MODEL), lambda i: (0, 0)),
            pl.BlockSpec((1, 3, 1, D_MODEL), lambda i: (_cond_of_tile(t0 + i, tm), 0, 0, 0)),
            pl.BlockSpec((1, D_MODEL), lambda i: (0, 0)),
        ],
        out_specs=pl.BlockSpec((tm, D_MODEL), lambda i: (i, 0)),
        out_shape=jax.ShapeDtypeStruct((rows[1] - rows[0], D_MODEL), F32),
        scratch_shapes=[pltpu.VMEM((k, D_MODEL), BF16)],
        compiler_params=_params("arbitrary"),
        name="out_proj",
    )(*xs, o, g_arr, w, mod, final_w.reshape(1, D_MODEL))


def _rope_tables(d):
    q = d // 4
    t = jnp.arange(DEC_SEQ)
    row = (t // GRID_W).astype(F32)
    col = (t % GRID_W).astype(F32)
    inv = ROPE_BASE ** (-jnp.arange(0, 2 * q, 2, dtype=F32) / (2 * q))
    ar = row[:, None] * inv[None, :]
    ac = col[:, None] * inv[None, :]
    z = jnp.zeros_like(ar)
    cos = jnp.concatenate([jnp.cos(ar), jnp.cos(ar), jnp.cos(ac), jnp.cos(ac)], axis=-1)
    sin_lo = jnp.concatenate([-jnp.sin(ar), z, -jnp.sin(ac), z], axis=-1)
    sin_hi = jnp.concatenate([z, jnp.sin(ar), z, jnp.sin(ac)], axis=-1)
    return cos, sin_lo, sin_hi


def _rope(x, cos, sin_lo, sin_hi, q):
    w = x.shape[-1]
    x_next = pltpu.roll(x, w - q, axis=1)
    x_prev = pltpu.roll(x, q, axis=1)
    return x * cos + x_next * sin_lo + x_prev * sin_hi


RET_QB = 256


RET_SEQS = 4


def _ret_decay(lgf, lgb, qi, seq):
    ii = lax.broadcasted_iota(jnp.int32, (RET_QB, seq), 0) + qi * RET_QB
    jj = lax.broadcasted_iota(jnp.int32, (RET_QB, seq), 1)
    gap = (ii - jj).astype(F32)
    return (jnp.where(gap >= 0, jnp.exp(lgf * jnp.maximum(gap, 0.0)), 0.0)
            + jnp.where(gap <= 0, jnp.exp(lgb * jnp.maximum(-gap, 0.0)), 0.0))


def _head_layer_norm(o, gn):
    oc = o - jnp.mean(o, axis=-1, keepdims=True)
    return oc * lax.rsqrt(jnp.mean(oc * oc, axis=-1, keepdims=True) + GN_EPS) * gn


def _ret_prompt_kernel(lg_ref, q_ref, k_ref, v_ref, gn_ref, o_ref, st_ref, dec_ref):
    h = pl.program_id(0)
    lgf = lg_ref[0, h]
    lgb = lg_ref[1, h]

    @pl.when(pl.program_id(1) == 0)
    def _():
        dec_ref[...] = _ret_decay(lgf, lgb, 0, SEQ)

    dec = dec_ref[...]
    pos = lax.broadcasted_iota(jnp.int32, (SEQ, 1), 0).astype(F32)
    w_fwd = jnp.exp(lgf * (SEQ - 1.0 - pos))
    w_bwd = jnp.exp(lgb * pos)
    rows = [slice(s * SEQ, (s + 1) * SEQ) for s in range(RET_SEQS)]
    k = [k_ref[r, :].astype(F32) * (RET_DK ** -0.5) for r in rows]
    kb = [x.astype(BF16) for x in k]
    vb = [v_ref[r, :].astype(BF16) for r in rows]
    s = [_dot_nt(q_ref[r, :].astype(BF16), y) for r, y in zip(rows, kb)]
    o = [_dot((x * dec).astype(BF16), y) for x, y in zip(s, vb)]
    s_fwd = [_dot_tn((x * w_fwd).astype(BF16), y) for x, y in zip(k, vb)]
    s_bwd = [_dot_tn((x * w_bwd).astype(BF16), y) for x, y in zip(k, vb)]
    gn = gn_ref[...]
    for i, r in enumerate(rows):
        o_ref[r, :] = _head_layer_norm(o[i], gn).astype(o_ref.dtype)
        st_ref[i, 0, 0, 0] = s_fwd[i]
        st_ref[i, 0, 1, 0] = s_bwd[i]


def _ret_latent_kernel(lg_ref, q_ref, k_ref, v_ref, gn_ref, cos_ref, slo_ref, shi_ref, s0_ref, _prev_ref, o_ref,
                       dec_ref):
    seq = DEC_SEQ
    h = pl.program_id(0)
    lgf = lg_ref[0, h]
    lgb = lg_ref[1, h]

    @pl.when(pl.program_id(1) == 0)
    def _():
        for qi in range(seq // RET_QB):
            dec_ref[qi] = _ret_decay(lgf, lgb, qi, seq)

    q = _rope(q_ref[...].astype(F32), cos_ref[...], slo_ref[...], shi_ref[...], RET_DK // 4)
    k = _rope(k_ref[...].astype(F32), cos_ref[...], slo_ref[...], shi_ref[...], RET_DK // 4)
    kb = (k * (RET_DK ** -0.5)).astype(BF16)
    vb = v_ref[...].astype(BF16)
    gn = gn_ref[...]
    for qi in range(seq // RET_QB):
        qblk = q[qi * RET_QB:(qi + 1) * RET_QB]
        s = _dot_nt(qblk.astype(BF16), kb)
        o = _dot((s * dec_ref[qi]).astype(BF16), vb)
        pos = (lax.broadcasted_iota(jnp.int32, (RET_QB, 1), 0) + qi * RET_QB).astype(F32)
        qf = qblk * jnp.exp(lgf * (pos + 1.0))
        qr = qblk * jnp.exp(lgb * (seq - pos))
        o = o + _dot(qf.astype(BF16), s0_ref[0, 0, 0, 0].astype(BF16))
        o = o + _dot(qr.astype(BF16), s0_ref[0, 0, 1, 0].astype(BF16))
        o_ref[qi * RET_QB:(qi + 1) * RET_QB, :] = _head_layer_norm(o, gn).astype(o_ref.dtype)


def _retention(p, log_g, gn_w, state_ret, j):
    smem = pl.BlockSpec(memory_space=pltpu.SMEM)
    gn = gn_w.reshape(1, RET_V)
    kq = RET_QK // RET_DK
    rows = RET_SEQS * SEQ
    o_p, st = pl.pallas_call(
        _ret_prompt_kernel,
        grid=(RET_HEADS, BATCH // RET_SEQS),
        in_specs=[
            smem,
            pl.BlockSpec((rows, RET_DK), lambda h, b: (b, h)),
            pl.BlockSpec((rows, RET_DK), lambda h, b: (b, kq + h)),
            pl.BlockSpec((rows, RET_DV), lambda h, b: (b, kq + h)),
            pl.BlockSpec((1, RET_DV), lambda h, b: (0, h)),
        ],
        out_specs=[
            pl.BlockSpec((rows, RET_DV), lambda h, b: (b, h)),
            pl.BlockSpec((RET_SEQS, 1, 2, 1, RET_DK, RET_DV), lambda h, b: (b, 0, 0, h, 0, 0)),
        ],
        out_shape=[
            jax.ShapeDtypeStruct((N_TOK, RET_V), MIX_DTYPE),
            jax.ShapeDtypeStruct((BATCH, 1, 2, RET_HEADS, RET_DK, RET_DV), F32),
        ],
        scratch_shapes=[pltpu.VMEM((RET_QB, SEQ), F32)],
        compiler_params=_params("arbitrary", "arbitrary"),
        name="retention_prompt",
    )(log_g, p, p, p, gn)
    cos, slo, shi = _rope_tables(RET_DK)
    rb = N_PROMPT_TOK // DEC_SEQ
    full = pl.BlockSpec((DEC_SEQ, RET_DK), lambda h, b: (0, 0))
    o = pl.pallas_call(
        _ret_latent_kernel,
        grid=(RET_HEADS, DEC_BATCH),
        in_specs=[
            smem,
            pl.BlockSpec((DEC_SEQ, RET_DK), lambda h, b: (rb + b, h)),
            pl.BlockSpec((DEC_SEQ, RET_DK), lambda h, b: (rb + b, kq + h)),
            pl.BlockSpec((DEC_SEQ, RET_DV), lambda h, b: (rb + b, kq + h)),
            pl.BlockSpec((1, RET_DV), lambda h, b: (0, h)),
            full, full, full,
            pl.BlockSpec((1, 1, 2, 1, RET_DK, RET_DV), lambda h, b: (b, j, 0, h, 0, 0)),
            pl.BlockSpec(memory_space=pl.ANY),
        ],
        out_specs=pl.BlockSpec((DEC_SEQ, RET_DV), lambda h, b: (rb + b, h)),
        out_shape=jax.ShapeDtypeStruct((N_TOK, RET_V), MIX_DTYPE),
        input_output_aliases={9: 0},
        scratch_shapes=[pltpu.VMEM((DEC_SEQ // RET_QB, RET_QB, DEC_SEQ), F32)],
        compiler_params=_params("arbitrary", "arbitrary"),
        name="retention_latent",
    )(log_g, p, p, p, gn, cos, slo, shi, state_ret, o_p)
    return o, st


def _layer_ret(x, p, mod, j):
    i = N_MIXERS * j + 0
    proj = _in_proj(x, p['norm_w'][i], mod, p['ret_w_in'][j], IN_TN, out_dtype=BF16)
    log_g = jax.nn.log_sigmoid(p['ret_decay'][j].astype(F32))
    o, st = _retention(proj, log_g, p['ret_gn'][j], p['state_ret'], j)
    x = _out_proj(o, proj, (2 * RET_QK + RET_V) // RET_V, p['ret_w_out'][j], x, mod, p['final_norm_w'], False)
    return x, st


RW_TM = 512
RW_HALO = 8
RW_C = 64
RW_LOCK = 4
RW_PAIRS = 2
RW_PAIRS_PROMPT = 2


def _rwkv_prep_kernel(x_ref, xp_ref, xn_ref, nw_ref, mod_ref, mu_ref, wa_ref, aa_ref, wb_ref, ab_ref,
                      w0_ref, a0_ref, xm_ref, lw_ref, a_ref):
    i = pl.program_id(0)
    nw = nw_ref[...]
    h = _norm_mod(x_ref[...], nw, mod_ref)
    h_before = _norm_mod(xp_ref[RW_HALO - 1:RW_HALO, :], nw, mod_ref)
    h_after = _norm_mod(xn_ref[0:1, :], nw, mod_ref)
    seq = jnp.where(i < N_PROMPT_TOK // RW_TM, SEQ, DEC_SEQ)
    row = lax.broadcasted_iota(jnp.int32, (RW_TM, 1), 0)
    t = (row + i * RW_TM) & (seq - 1)
    prev = jnp.where(row == 0, h_before, pltpu.roll(h, 1, axis=0))
    nxt = jnp.where(row == RW_TM - 1, h_after, pltpu.roll(h, RW_TM - 1, axis=0))
    prev = jnp.where(t == 0, 0.0, prev)
    nxt = jnp.where(t == seq - 1, 0.0, nxt)
    xx = 0.5 * (prev + nxt) - h
    for n, m in enumerate((0, 2, 3, 5)):
        xm_ref[n] = (h + xx * mu_ref[m:m + 1, :]).astype(BF16)
    xw = (h + xx * mu_ref[1:2, :]).astype(BF16)
    xa = (h + xx * mu_ref[4:5, :]).astype(BF16)
    lw = jnp.tanh(_dot(xw, wa_ref[...])).astype(BF16)
    la = _dot(xa, aa_ref[...]).astype(BF16)
    for dr in range(2):
        wl = w0_ref[dr:dr + 1, :] + _dot(lw, wb_ref[dr])
        lw_ref[dr] = -math.exp(-0.5) * _sigmoid(wl)
        a_ref[dr] = _sigmoid(a0_ref[dr:dr + 1, :] + _dot(la, ab_ref[dr]))


def _rwkv_prep(x, norm_w, mod, mu, wa2, aa2, wb_pad, ab_pad, w0, a0):
    nt = N_TOK // RW_TM
    hb = RW_TM // RW_HALO
    last = N_TOK // RW_HALO - 1
    full2 = lambda shape: pl.BlockSpec(shape, lambda i: (0, 0))
    full3 = lambda shape: pl.BlockSpec(shape, lambda i: (0, 0, 0))
    return pl.pallas_call(
        _rwkv_prep_kernel,
        grid=(nt,),
        in_specs=[
            pl.BlockSpec((RW_TM, D_MODEL), lambda i: (i, 0)),
            pl.BlockSpec((RW_HALO, D_MODEL), lambda i: (jnp.maximum(i * hb - 1, 0), 0)),
            pl.BlockSpec((RW_HALO, D_MODEL), lambda i: (jnp.minimum((i + 1) * hb, last), 0)),
            full2((1, D_MODEL)),
            pl.BlockSpec((1, 3, 1, D_MODEL), lambda i: (_cond_of_tile(i, RW_TM), 0, 0, 0)),
            full2((6, D_MODEL)),
            full2((D_MODEL, 2 * RWKV_RANK)),
            full2((D_MODEL, 2 * RWKV_RANK)),
            full3((2, 2 * RWKV_RANK, D_MODEL)),
            full3((2, 2 * RWKV_RANK, D_MODEL)),
            full2((2, D_MODEL)),
            full2((2, D_MODEL)),
        ],
        out_specs=[
            pl.BlockSpec((4, RW_TM, D_MODEL), lambda i: (0, i, 0)),
            pl.BlockSpec((2, RW_TM, D_MODEL), lambda i: (0, i, 0)),
            pl.BlockSpec((2, RW_TM, D_MODEL), lambda i: (0, i, 0)),
        ],
        out_shape=[
            jax.ShapeDtypeStruct((4, N_TOK, D_MODEL), BF16),
            jax.ShapeDtypeStruct((2, N_TOK, D_MODEL), F32),
            jax.ShapeDtypeStruct((2, N_TOK, D_MODEL), F32),
        ],
        compiler_params=_params("arbitrary"),
        name="rwkv_prep",
    )(x, x, x, norm_w.reshape(1, D_MODEL), mod, mu, wa2, aa2, wb_pad, ab_pad, w0, a0)


def _bmm_kernel(a_ref, w_ref, o_ref):
    o_ref[0] = _dot(a_ref[0], w_ref[...])


def _rwkv_rkvg(xm, w):
    tm = 1024
    return pl.pallas_call(
        _bmm_kernel,
        grid=(4, N_TOK // tm),
        in_specs=[
            pl.BlockSpec((1, tm, D_MODEL), lambda n, i: (n, i, 0)),
            pl.BlockSpec((D_MODEL, D_MODEL), lambda n, i: (0, n)),
        ],
        out_specs=pl.BlockSpec((1, tm, D_MODEL), lambda n, i: (n, i, 0)),
        out_shape=jax.ShapeDtypeStruct((4, N_TOK, D_MODEL), F32),
        compiler_params=_params("arbitrary", "arbitrary"),
        name="rwkv_rkvg",
    )(xm, w)


def _head_sum(x, first):
    s0 = jnp.sum(jnp.where(first, x, 0.0), axis=-1, keepdims=True)
    s1 = jnp.sum(jnp.where(first, 0.0, x), axis=-1, keepdims=True)
    return jnp.where(first, s0, s1)


def _stack_heads(x, first):
    xb = x.astype(BF16)
    zero = jnp.zeros_like(xb)
    return jnp.concatenate([jnp.where(first, xb, zero), jnp.where(first, zero, xb)], axis=0)


def _split3(x):
    hi = x.astype(BF16)
    r1 = x - hi.astype(F32)
    mid = r1.astype(BF16)
    return hi, mid, (r1 - mid.astype(F32)).astype(BF16)


def _rwkv_kernel(*refs, seq, zero_init, pg, np2):
    n_in = 7 if zero_init else 9
    rkv_ref, lw_ref, a_ref, kkp_ref, kap_ref, rkp_ref, gn_ref = refs[:7]
    if zero_init:
        o_ref, st_ref = refs[n_in:n_in + 2]
        scr = refs[n_in + 2:]
    else:
        s0_ref = refs[7]
        o_ref = refs[n_in]
        scr = refs[n_in + 1:]
    kk_scr, cum_scr, bon_scr, y_scr, s_scr, tar_scr, lrb_scr, b2_scr, w2_scr, yl_scr, kv_scr, pc_scr = scr
    c_len = RW_C
    n_ch = seq // c_len
    rows2 = 2 * c_len
    grp = pl.program_id(1)
    defer = np2 > pg
    base = grp * pg if defer else 0
    first = _first_half_lanes()

    rr = lax.broadcasted_iota(jnp.int32, (rows2, rows2), 0)
    cc = lax.broadcasted_iota(jnp.int32, (rows2, rows2), 1)
    eye = (rr == cc).astype(F32)

    def same(shift):
        return (rr >> shift) == (cc >> shift)

    head = same(6)
    strict = (head & (cc < rr), head & (cc > rr))
    incl = (head & (cc <= rr), head & (cc >= rr))
    last = (c_len - 1, 0)
    head_ones = head.astype(BF16)

    cs_rows = min(seq, 256)
    tr = lax.broadcasted_iota(jnp.int32, (cs_rows, cs_rows), 0)
    tc = lax.broadcasted_iota(jnp.int32, (cs_rows, cs_rows), 1)
    chunk = (tr >> 6) == (tc >> 6)
    tri = ((chunk & (tc <= tr)).astype(BF16), (chunk & (tc >= tr)).astype(BF16))
    for p in range(pg):
        ln = slice(p * LANES, (p + 1) * LANES)
        kk = rkv_ref[1, :, ln] * kkp_ref[:, ln]
        kk_scr[p] = kk * lax.rsqrt(jnp.maximum(_head_sum(kk * kk, first), 1e-12))

    def bonus_terms():
        for p in range(pg):
            ln = slice(p * LANES, (p + 1) * LANES)
            r = rkv_ref[0, :, ln]
            k = rkv_ref[1, :, ln]
            bonus = None
            for dr in range(2):
                kd = k * (1.0 + (a_ref[dr, :, ln] - 1.0) * kap_ref[:, ln])
                term = _head_sum(r * kd * rkp_ref[:, ln], first) * rkv_ref[2, :, ln]
                bonus = term if bonus is None else bonus + term
            bon_scr[base + p] = bonus

    for p in range(0, pg, 2):
        for dr in range(2):
            for r0 in range(0, seq, cs_rows):
                parts = _split3(lw_ref[dr, r0:r0 + cs_rows, p * LANES:(p + 2) * LANES])
                cum = _dot(tri[dr], parts[0]) + _dot(tri[dr], parts[1]) + _dot(tri[dr], parts[2])
                cum_scr[p, dr, r0:r0 + cs_rows, :] = cum[:, :LANES]
                cum_scr[p + 1, dr, r0:r0 + cs_rows, :] = cum[:, LANES:]

    def phase1(chains):
        dirs = [dr for _, dr, _ in chains]
        a2, r2, b2, k2, v2, pc = [], [], [], [], [], []
        for p, dr, c in chains:
            ln = slice(p * LANES, (p + 1) * LANES)
            rw = pl.ds(pl.multiple_of(c * c_len, c_len), c_len)
            a = a_ref[dr, rw, ln]
            k = rkv_ref[1, rw, ln]
            kk_c = kk_scr[p, rw, :]
            cum_c = cum_scr[p, dr, rw, :]
            e_inc = jnp.exp(cum_c)
            e_inv = jnp.exp(-cum_c)
            a2.append(_stack_heads(-kk_c * jnp.exp(cum_c - lw_ref[dr, rw, ln]), first))
            r2.append(_stack_heads(rkv_ref[0, rw, ln] * e_inc, first))
            b2.append(_stack_heads(kk_c * a * e_inv, first))
            k2.append(_stack_heads(k * (1.0 + (a - 1.0) * kap_ref[:, ln]) * e_inv, first))
            v2.append(_stack_heads(rkv_ref[2, rw, ln], first))
            pc.append(e_inc[last[dr]:last[dr] + 1, :])
        g = [_dot_nt(jnp.concatenate([x, y], axis=0), jnp.concatenate([z, w], axis=0))
             for x, y, z, w in zip(a2, r2, b2, k2)]
        l_ab = [jnp.where(strict[dr], x[:rows2, :rows2], 0.0) for dr, x in zip(dirs, g)]
        t = [eye + jnp.where(same(1), x, 0.0) for x in l_ab]
        l_ab16 = [x.astype(BF16) for x in l_ab]
        zero16 = jnp.zeros((rows2, rows2), BF16)
        side = {}
        for shift in range(1, 6):
            sib = same(shift + 1) & ~same(shift)
            tb = [x.astype(BF16) for x in t]
            mid = [_dot(jnp.where(sib, x, zero16), y) for x, y in zip(l_ab16, tb)]
            if shift == 1:
                side['lv'] = [_dot(jnp.where(strict[dr], x[:rows2, rows2:], 0.0).astype(BF16), y)
                              for dr, x, y in zip(dirs, g, v2)]
            elif shift == 2:
                side['yl'] = [_dot(jnp.where(incl[dr], x[rows2:, rows2:], 0.0).astype(BF16), y)
                              for dr, x, y in zip(dirs, g, v2)]
            elif shift == 3:
                side['kv'] = [_dot_tn(x, y) for x, y in zip(v2, k2)]
            t = [x + _dot(y, z.astype(BF16)) for x, y, z in zip(t, tb, mid)]
        tb = [x.astype(BF16) for x in t]
        ta = [_dot(x, y) for x, y in zip(tb, a2)]
        w2 = [_dot(x, y.astype(BF16)) for x, y in zip(tb, side['lv'])]
        for i, (p, dr, c) in enumerate(chains):
            n = ((base + p) * 2 + dr) * n_ch + c
            tar_scr[n, :rows2, :] = ta[i].astype(BF16)
            tar_scr[n, rows2:, :] = r2[i]
            w2_scr[n] = w2[i]
            yl_scr[n] = side['yl'][i]
            kv_scr[n] = side['kv'][i]
            lrb_scr[n] = jnp.where(incl[dr], g[i][rows2:, :rows2], 0.0).astype(BF16)
            b2_scr[n] = b2[i]
            pc_scr[n] = pc[i]

    def body1(cg, carry):
        phase1([(p, dr, cg * RW_LOCK + j) for p in range(pg) for j in range(RW_LOCK) for dr in range(2)])
        return carry

    if n_ch == RW_LOCK:
        body1(0, 0)
    else:
        lax.fori_loop(0, n_ch // RW_LOCK, body1, 0)
    bonus_terms()

    def finish():
        for p in range(np2):
            for dr in range(2):
                if zero_init:
                    s_scr[2 * p + dr] = jnp.zeros((rows2, LANES), F32)
                else:
                    s_scr[2 * p + dr] = s0_ref[0, dr, p]

        def body2(i, carry):
            cs = (i, n_ch - 1 - i)
            ids = [(p, dr) for p in range(np2) for dr in range(2)]
            ns = [(p * 2 + dr) * n_ch + cs[dr] for p, dr in ids]
            x = [_dot_nt(tar_scr[n], s_scr[2 * p + dr].astype(BF16)) for n, (p, dr) in zip(ns, ids)]
            u2 = [(xx[:rows2] + w2_scr[n]).astype(BF16) for n, xx in zip(ns, x)]
            upd = [_dot_tn(u, b2_scr[n]) for n, u in zip(ns, u2)]
            yb = [_dot(lrb_scr[n], u) for n, u in zip(ns, u2)]
            for j, (n, (p, dr)) in enumerate(zip(ns, ids)):
                y2 = x[j][rows2:] + yb[j] + yl_scr[n]
                y_scr[p, dr, pl.ds(pl.multiple_of(cs[dr] * c_len, c_len), c_len), :] = y2[:c_len] + y2[c_len:]
                s_scr[2 * p + dr] = (s_scr[2 * p + dr] + upd[j] + kv_scr[n]) * pc_scr[n]
            return carry

        lax.fori_loop(0, n_ch, body2, 0)

        def head_mean(xs):
            parts = [_split3(x) for x in xs]
            return [(_dot(a, head_ones) + _dot(b, head_ones) + _dot(c, head_ones)) * (1.0 / RWKV_HD)
                    for a, b, c in parts]

        ys = [y_scr[p, 0] + y_scr[p, 1] for p in range(np2)]
        yc = [y - m for y, m in zip(ys, head_mean(ys))]
        var = head_mean([x * x for x in yc])
        for p in range(np2):
            ln = slice(p * LANES, (p + 1) * LANES)
            o_ref[:, ln] = (yc[p] * lax.rsqrt(var[p] + GN_EPS) * gn_ref[:, ln] + bon_scr[p]).astype(o_ref.dtype)
            if zero_init:
                for dr in range(2):
                    s2 = s_scr[2 * p + dr]
                    st_ref[0, 0, dr, 2 * p] = s2[:RWKV_HD, :RWKV_HD]
                    st_ref[0, 0, dr, 2 * p + 1] = s2[RWKV_HD:, RWKV_HD:]

    if defer:
        pl.when(grp == pl.num_programs(1) - 1)(finish)
    else:
        finish()


def _rwkv_mixer(rkvg, lw, a, kkp, kap, rkp, gn, s0_pairs):
    npair = RWKV_HEADS // 2
    r2 = 2 * RW_C

    def scratch(seq, pg, np2):
        n = 2 * np2 * (seq // RW_C)
        return [
            pltpu.VMEM((pg, seq, LANES), F32), pltpu.VMEM((pg, 2, seq, LANES), F32),
            pltpu.VMEM((np2, seq, LANES), F32), pltpu.VMEM((np2, 2, seq, LANES), F32),
            pltpu.VMEM((2 * np2, r2, LANES), F32),
            pltpu.VMEM((n, 2 * r2, LANES), BF16), pltpu.VMEM((n, r2, r2), BF16), pltpu.VMEM((n, r2, LANES), BF16),
            pltpu.VMEM((n, r2, LANES), F32), pltpu.VMEM((n, r2, LANES), F32), pltpu.VMEM((n, r2, LANES), F32),
            pltpu.VMEM((n, 1, LANES), F32),
        ]

    def seq_specs(seq, rb, pg, gn_spec):
        wl = pg * LANES
        par =pl.BlockSpec((1, wl), lambda s, g: (0, g))
        return [
            pl.BlockSpec((3, seq, wl), lambda s, g: (0, rb + s, g)),
            pl.BlockSpec((2, seq, wl), lambda s, g: (0, rb + s, g)),
            pl.BlockSpec((2, seq, wl), lambda s, g: (0, rb + s, g)),
            par, par, par, gn_spec,
        ]

    pars = [u.reshape(1, D_MODEL) for u in (kkp, kap, rkp, gn)]
    pg = RW_PAIRS_PROMPT
    o_p, st = pl.pallas_call(
        functools.partial(_rwkv_kernel, seq=SEQ, zero_init=True, pg=pg, np2=npair),
        grid=(BATCH, npair // pg),
        in_specs=seq_specs(SEQ, 0, pg, pl.BlockSpec((1, D_MODEL), lambda s, g: (0, 0))),
        out_specs=[
            pl.BlockSpec((SEQ, D_MODEL), lambda s, g: (s, 0)),
            pl.BlockSpec((1, 1, 2, RWKV_HEADS, RWKV_HD, RWKV_HD), lambda s, g: (s, 0, 0, 0, 0, 0)),
        ],
        out_shape=[
            jax.ShapeDtypeStruct((N_TOK, D_MODEL), MIX_DTYPE),
            jax.ShapeDtypeStruct((BATCH, 1, 2, RWKV_HEADS, RWKV_HD, RWKV_HD), F32),
        ],
        scratch_shapes=scratch(SEQ, pg, npair),
        compiler_params=_params("arbitrary", "arbitrary"),
        name="rwkv_prompt",
    )(rkvg, lw, a, *pars)
    rb = N_PROMPT_TOK // DEC_SEQ
    pg = RW_PAIRS
    wl = pg * LANES
    o = pl.pallas_call(
        functools.partial(_rwkv_kernel, seq=DEC_SEQ, zero_init=False, pg=pg, np2=pg),
        grid=(DEC_BATCH, npair // pg),
        in_specs=seq_specs(DEC_SEQ, rb, pg, pl.BlockSpec((1, wl), lambda s, g: (0, g))) + [
            pl.BlockSpec((1, 2, pg, LANES, LANES), lambda s, g: (s, 0, g, 0, 0)),
            pl.BlockSpec(memory_space=pl.ANY),
        ],
        out_specs=pl.BlockSpec((DEC_SEQ, wl), lambda s, g: (rb + s, g)),
        out_shape=jax.ShapeDtypeStruct((N_TOK, D_MODEL), MIX_DTYPE),
        input_output_aliases={8: 0},
        scratch_shapes=scratch(DEC_SEQ, pg, pg),
        compiler_params=_params("arbitrary", "arbitrary"),
        name="rwkv_latent",
    )(rkvg, lw, a, *pars, s0_pairs, o_p)
    return o, st


def _state_pairs(s0):
    s = s0.reshape(DEC_BATCH, 2, RWKV_HEADS // 2, 2, RWKV_HD, RWKV_HD)
    z = jnp.zeros_like(s[:, :, :, 0])
    top = jnp.concatenate([s[:, :, :, 0], z], axis=-1)
    bot = jnp.concatenate([z, s[:, :, :, 1]], axis=-1)
    return jnp.concatenate([top, bot], axis=-2)


def _layer_rwkv(x, p, mod, j):
    i = N_MIXERS * j + 1
    wa, wb, aa, ab = p['rwkv_wA'][j], p['rwkv_wB'][j], p['rwkv_aA'][j], p['rwkv_aB'][j]
    z = jnp.zeros_like(wb[0])
    wa2 = jnp.concatenate([wa[0], wa[1]], axis=1).astype(BF16)
    aa2 = jnp.concatenate([aa[0], aa[1]], axis=1).astype(BF16)
    wb_pad = jnp.stack([jnp.concatenate([wb[0], z]), jnp.concatenate([z, wb[1]])]).astype(BF16)
    ab_pad = jnp.stack([jnp.concatenate([ab[0], z]), jnp.concatenate([z, ab[1]])]).astype(BF16)
    xm, lw, a = _rwkv_prep(x, p['norm_w'][i], mod, p['rwkv_mu'][j], wa2, aa2, wb_pad, ab_pad,
                           p['rwkv_w0'][j], p['rwkv_a0'][j])
    rkvg = _rwkv_rkvg(xm, p['rwkv_w_in'][j].astype(BF16))
    o, st = _rwkv_mixer(rkvg, lw, a, p['rwkv_kk'][j], p['rwkv_ka'][j], p['rwkv_rk'][j], p['rwkv_gn'][j],
                          _state_pairs(p['state_rwkv'][:, j]))
    x = _out_proj(o, rkvg.reshape(4 * N_TOK, D_MODEL), 0, p['rwkv_w_out'][j], x, mod, p['final_norm_w'], False,
                  g_row0=3 * N_TOK)
    return x, st


DIFF_W = 2 * DIFF_HD
ATT_QB = 256
DIFF_GROUP = 4
ATT_SEQS = 2


def _first_half_lanes():
    return lax.broadcasted_iota(jnp.int32, (1, LANES), 1) < LANES // 2


def _diff_lambda(lam_ref, lam_init):
    lp = lam_ref[...]
    return (jnp.exp(jnp.sum(lp[0:1] * lp[1:2], keepdims=True))
            - jnp.exp(jnp.sum(lp[2:3] * lp[3:4], keepdims=True)) + lam_init)


def _diff_heads(items, lam, lam_init):
    first = _first_half_lanes()
    scale = DIFF_HD ** -0.5
    sub = [(q, keys, comp) for q, keys, _ in items for comp in range(2)]
    qm = [(jnp.where(first if comp == 0 else ~first, q, 0.0) * scale).astype(BF16) for q, _, comp in sub]
    s = [[_dot_nt(x, kb) for kb, _ in keys] for x, (_, keys, _) in zip(qm, sub)]
    m = [functools.reduce(jnp.maximum, [jnp.max(u, axis=-1, keepdims=True) for u in ss]) for ss in s]
    e = [[jnp.exp(u - mm) for u in ss] for ss, mm in zip(s, m)]
    inv = [1.0 / functools.reduce(lambda x, y: x + y, [jnp.sum(u, axis=-1, keepdims=True) for u in ee]) for ee in e]
    outs = []
    for i, (_, keys, gn) in enumerate(items):
        o = None
        lam_inv = lam * inv[2 * i + 1]
        for n, (_, vb) in enumerate(keys):
            p = e[2 * i][n] * inv[2 * i] - e[2 * i + 1][n] * lam_inv
            part = _dot(p.astype(BF16), vb)
            o = part if o is None else o + part
        outs.append(o)
    return [o * lax.rsqrt(jnp.mean(o * o, axis=-1, keepdims=True) + EPS) * gn * (1.0 - lam_init)
            for o, (_, _, gn) in zip(outs, items)]


def _diff_prompt_kernel(lam_ref, q_ref, k_ref, v_ref, gn_ref, o_ref, *, lam_init):
    lam = _diff_lambda(lam_ref, lam_init)
    for s in range(ATT_SEQS):
        rw = slice(s * SEQ, (s + 1) * SEQ)
        for h0 in range(0, DIFF_HEADS, DIFF_GROUP):
            items = []
            for h in range(h0, h0 + DIFF_GROUP):
                sl = slice(h * DIFF_W, (h + 1) * DIFF_W)
                keys = [(k_ref[rw, sl].astype(BF16), v_ref[rw, sl].astype(BF16))]
                items.append((q_ref[rw, sl], keys, gn_ref[:, sl]))
            for h, o in zip(range(h0, h0 + DIFF_GROUP), _diff_heads(items, lam, lam_init)):
                o_ref[rw, h * DIFF_W:(h + 1) * DIFF_W] = o.astype(o_ref.dtype)


def _diff_latent_kernel(lam_ref, q_ref, k_ref, v_ref, ck_ref, cv_ref, cos_ref, slo_ref, shi_ref, gn_ref,
                        _prev_ref, o_ref, *, lam_init):
    lam = _diff_lambda(lam_ref, lam_init)
    tabs = (cos_ref[...], slo_ref[...], shi_ref[...])
    q = _rope(q_ref[...].astype(F32), *tabs, DIFF_HD // 4)
    k = _rope(k_ref[...].astype(F32), *tabs, DIFF_HD // 4)
    keys = [(k.astype(BF16), v_ref[...].astype(BF16)),
            (ck_ref[0, 0, 0].astype(BF16), cv_ref[0, 0, 0].astype(BF16))]
    gn = gn_ref[...]
    n_blk = DEC_SEQ // ATT_QB
    items = [(q[qi * ATT_QB:(qi + 1) * ATT_QB], keys, gn) for qi in range(n_blk)]
    for qi, o in enumerate(_diff_heads(items, lam, lam_init)):
        o_ref[qi * ATT_QB:(qi + 1) * ATT_QB, :] = o.astype(o_ref.dtype)


def _diff_attention(proj, lam_p, gn_w, cache_k, cache_v, j, lam_init):
    gn = gn_w.reshape(1, D_MODEL)
    lam_spec = pl.BlockSpec((4, DIFF_HD), lambda *_: (0, 0))
    o_p = pl.pallas_call(
        functools.partial(_diff_prompt_kernel, lam_init=lam_init),
        grid=(BATCH // ATT_SEQS,),
        in_specs=[
            lam_spec,
            pl.BlockSpec((ATT_SEQS * SEQ, D_MODEL), lambda b: (b, 0)),
            pl.BlockSpec((ATT_SEQS * SEQ, D_MODEL), lambda b: (b, 1)),
            pl.BlockSpec((ATT_SEQS * SEQ, D_MODEL), lambda b: (b, 2)),
            pl.BlockSpec((1, D_MODEL), lambda b: (0, 0)),
        ],
        out_specs=pl.BlockSpec((ATT_SEQS * SEQ, D_MODEL), lambda b: (b, 0)),
        out_shape=jax.ShapeDtypeStruct((N_TOK, D_MODEL), MIX_DTYPE),
        compiler_params=_params("arbitrary"),
        name="diff_prompt",
    )(lam_p, proj, proj, proj, gn)
    cos, slo, shi = (jnp.concatenate([u, u], axis=-1) for u in _rope_tables(DIFF_HD))
    rb = N_PROMPT_TOK // DEC_SEQ
    nh = DIFF_HEADS
    tab = pl.BlockSpec((DEC_SEQ, DIFF_W), lambda b, h: (0, 0))
    cache = pl.BlockSpec((1, 1, 1, PAST_LEN, DIFF_W), lambda b, h: (b, j, h, 0, 0))
    o = pl.pallas_call(
        functools.partial(_diff_latent_kernel, lam_init=lam_init),
        grid=(DEC_BATCH, nh),
        in_specs=[
            lam_spec,
            pl.BlockSpec((DEC_SEQ, DIFF_W), lambda b, h: (rb + b, h)),
            pl.BlockSpec((DEC_SEQ, DIFF_W), lambda b, h: (rb + b, nh + h)),
            pl.BlockSpec((DEC_SEQ, DIFF_W), lambda b, h: (rb + b, 2 * nh + h)),
            cache, cache, tab, tab, tab,
            pl.BlockSpec((1, DIFF_W), lambda b, h: (0, h)),
            pl.BlockSpec(memory_space=pl.ANY),
        ],
        out_specs=pl.BlockSpec((DEC_SEQ, DIFF_W), lambda b, h: (rb + b, h)),
        out_shape=jax.ShapeDtypeStruct((N_TOK, D_MODEL), MIX_DTYPE),
        input_output_aliases={10: 0},
        compiler_params=_params("arbitrary", "arbitrary"),
        name="diff_latent",
    )(lam_p, proj, proj, proj, cache_k, cache_v, cos, slo, shi, gn, o_p)
    return o


def _layer_diff(x, p, mod, j, i):
    lam_init = 0.8 - 0.6 * math.exp(-0.3 * i)
    proj, new_k, new_v = _in_proj_kv(x, p['norm_w'][i], mod, p['diff_w_in'][j], DIFF_HEADS)
    o = _diff_attention(proj, p['diff_lambda'][j], p['diff_gn'][j], p['cache_diff_k'], p['cache_diff_v'], j,
                        lam_init)
    x = _out_proj(o, proj, 3, p['diff_w_out'][j], x, mod, p['final_norm_w'], False)
    return x, new_k, new_v


NA_ROWS = DEC_SEQ // GRID_W
NA_WR = min(NA_WIN_R, NA_ROWS)
NA_LOC = NA_WR * GRID_W
NA_ROW_GROUP = 4
NA_PAIR_GROUP = 2


def _na_prompt_kernel(q_ref, k_ref, v_ref, o_ref):
    first = _first_half_lanes()
    scale = NA_HD ** -0.5
    for rw, p0 in [(slice(n * SEQ, (n + 1) * SEQ), p0) for n in range(ATT_SEQS)
                   for p0 in range(0, NA_HEADS // 2, NA_PAIR_GROUP)]:
        pairs = range(p0, p0 + NA_PAIR_GROUP)
        kb = [k_ref[rw, pr * LANES:(pr + 1) * LANES].astype(BF16) for pr in pairs]
        vb = [v_ref[rw, pr * LANES:(pr + 1) * LANES].astype(BF16) for pr in pairs]
        items = [(i, half) for i in range(NA_PAIR_GROUP) for half in range(2)]
        qm = [(jnp.where(first if half == 0 else ~first, q_ref[rw, (p0 + i) * LANES:(p0 + i + 1) * LANES], 0.0)
               * scale).astype(BF16) for i, half in items]
        s = [_dot_nt(x, kb[i]) for x, (i, _) in zip(qm, items)]
        e = [jnp.exp(x - jnp.max(x, axis=-1, keepdims=True)) for x in s]
        inv = [1.0 / jnp.sum(x, axis=-1, keepdims=True) for x in e]
        outs = [_dot(x.astype(BF16), vb[i]) * z for x, z, (i, _) in zip(e, inv, items)]
        for i in range(NA_PAIR_GROUP):
            o_ref[rw, (p0 + i) * LANES:(p0 + i + 1) * LANES] = (
                jnp.where(first, outs[2 * i], outs[2 * i + 1]).astype(o_ref.dtype))


def _na_latent_kernel(q_ref, k_ref, v_ref, kc_ref, vc_ref, tab_ref, _prev_ref, o_ref):
    first = _first_half_lanes()
    scale = NA_HD ** -0.5
    kb = k_ref[...].astype(BF16)
    vb = v_ref[...].astype(BF16)
    kcb = kc_ref[0, 0].astype(BF16)
    vcb = vc_ref[0, 0].astype(BF16)
    qcol = lax.broadcasted_iota(jnp.int32, (GRID_W, NA_LOC), 0)
    kcol = lax.broadcasted_iota(jnp.int32, (GRID_W, NA_LOC), 1) & (GRID_W - 1)
    cstart = jnp.clip(qcol - NA_WIN_C // 2, 0, GRID_W - NA_WIN_C)
    col_ok = (kcol >= cstart) & (kcol < cstart + NA_WIN_C)
    def bias_of(r, rs, half):
        parts = []
        for w in range(0, NA_WR, 2):
            src = jnp.broadcast_to(tab_ref[half, rs + w - r + NA_WIN_R - 1], (GRID_W, LANES))
            parts.append(pltpu.roll(src, LANES - (NA_WIN_C - 1), axis=1, stride=1, stride_axis=0))
        return jnp.concatenate(parts, axis=1)

    for r0 in range(0, NA_ROWS, NA_ROW_GROUP):
        items = [(r, min(max(r - NA_WR // 2, 0), NA_ROWS - NA_WR), half)
                 for r in range(r0, r0 + NA_ROW_GROUP) for half in range(2)]
        qm = [(jnp.where(first if half == 0 else ~first, q_ref[r * GRID_W:(r + 1) * GRID_W, :], 0.0)
               * scale).astype(BF16) for r, _, half in items]
        s_loc = [_dot_nt(x, kb[rs * GRID_W:(rs + NA_WR) * GRID_W]) for x, (_, rs, _) in zip(qm, items)]
        s_ctx = [_dot_nt(x, kcb) for x in qm]
        s_loc = [jnp.where(col_ok, x + bias_of(*it), -jnp.inf) for x, it in zip(s_loc, items)]
        m = [jnp.maximum(jnp.max(x, axis=-1, keepdims=True), jnp.max(y, axis=-1, keepdims=True))
             for x, y in zip(s_loc, s_ctx)]
        e_loc = [jnp.exp(x - mm) for x, mm in zip(s_loc, m)]
        e_ctx = [jnp.exp(x - mm) for x, mm in zip(s_ctx, m)]
        inv = [1.0 / (jnp.sum(x, axis=-1, keepdims=True) + jnp.sum(y, axis=-1, keepdims=True))
               for x, y in zip(e_loc, e_ctx)]
        pv = [_dot(x.astype(BF16), vb[rs * GRID_W:(rs + NA_WR) * GRID_W]) for x, (_, rs, _) in zip(e_loc, items)]
        pc = [_dot(x.astype(BF16), vcb) for x in e_ctx]
        outs = [(x + y) * z for x, y, z in zip(pv, pc, inv)]
        for n in range(0, len(items), 2):
            r = items[n][0]
            o_ref[r * GRID_W:(r + 1) * GRID_W, :] = jnp.where(first, outs[n], outs[n + 1]).astype(o_ref.dtype)


def _na_bias_pairs(table):
    t = table.astype(F32)
    nc = 2 * NA_WIN_C - 1
    z = jnp.zeros(t[:, :-1].shape[:2] + (GRID_W - nc,), F32)
    return jnp.concatenate([t[:, :-1], z, t[:, 1:], z], axis=-1)[:, :, None, :]


def _pair_heads(cache):
    c = cache.reshape(DEC_BATCH, NA_HEADS // 2, 2, PAST_LEN, NA_HD)
    return c.transpose(0, 1, 3, 2, 4).reshape(DEC_BATCH, NA_HEADS // 2, PAST_LEN, LANES)


def _na_attention(proj, bias_table, cache_k, cache_v):
    o_p = pl.pallas_call(
        _na_prompt_kernel,
        grid=(BATCH // ATT_SEQS,),
        in_specs=[
            pl.BlockSpec((ATT_SEQS * SEQ, D_MODEL), lambda b: (b, 0)),
            pl.BlockSpec((ATT_SEQS * SEQ, D_MODEL), lambda b: (b, 1)),
            pl.BlockSpec((ATT_SEQS * SEQ, D_MODEL), lambda b: (b, 2)),
        ],
        out_specs=pl.BlockSpec((ATT_SEQS * SEQ, D_MODEL), lambda b: (b, 0)),
        out_shape=jax.ShapeDtypeStruct((N_TOK, D_MODEL), MIX_DTYPE),
        compiler_params=_params("arbitrary"),
        name="na_prompt",
    )(proj, proj, proj)
    rb = N_PROMPT_TOK // DEC_SEQ
    npair = NA_HEADS // 2
    cache = pl.BlockSpec((1, 1, PAST_LEN, LANES), lambda pr, b: (b, pr, 0, 0))
    o = pl.pallas_call(
        _na_latent_kernel,
        grid=(npair, DEC_BATCH),
        in_specs=[
            pl.BlockSpec((DEC_SEQ, LANES), lambda pr, b: (rb + b, pr)),
            pl.BlockSpec((DEC_SEQ, LANES), lambda pr, b: (rb + b, npair + pr)),
            pl.BlockSpec((DEC_SEQ, LANES), lambda pr, b: (rb + b, 2 * npair + pr)),
            cache, cache,
            pl.BlockSpec((2, 2 * NA_WIN_R - 2, 1, LANES), lambda pr, b: (pr, 0, 0, 0)),
            pl.BlockSpec(memory_space=pl.ANY),
        ],
        out_specs=pl.BlockSpec((DEC_SEQ, LANES), lambda pr, b: (rb + b, pr)),
        out_shape=jax.ShapeDtypeStruct((N_TOK, D_MODEL), MIX_DTYPE),
        input_output_aliases={6: 0},
        compiler_params=_params("arbitrary", "arbitrary"),
        name="na_latent",
    )(proj, proj, proj, _pair_heads(cache_k), _pair_heads(cache_v), _na_bias_pairs(bias_table), o_p)
    return o


def _layer_na(x, p, mod, j, final):
    i = N_MIXERS * j + 3
    proj, new_k, new_v = _in_proj_kv(x, p['norm_w'][i], mod, p['na_w_in'][j], NA_HEADS)
    o = _na_attention(proj, p['na_bias'][j], p['cache_na_k'][:, j], p['cache_na_v'][:, j])
    args = (o, proj, 3, p['na_w_out'][j], x, mod, p['final_norm_w'])
    if final:
        x = (_out_proj(*args, True, rows=(0, N_PROMPT_TOK)), _out_proj(*args, True, rows=(N_PROMPT_TOK, N_TOK)))
    else:
        x = _out_proj(*args, False)
    return x, new_k, new_v


def kernel(x_prompt, x_sample, state_ret, state_rwkv, cache_diff_k, cache_diff_v, cache_na_k, cache_na_v,
           c, c_ctx, norm_w, w_mod, b_mod, final_norm_w,
           ret_w_in, ret_decay, ret_gn, ret_w_out,
           rwkv_mu, rwkv_w_in, rwkv_w0, rwkv_wA, rwkv_wB, rwkv_a0, rwkv_aA, rwkv_aB,
           rwkv_kk, rwkv_ka, rwkv_rk, rwkv_gn, rwkv_w_out,
           diff_w_in, diff_lambda, diff_gn, diff_w_out,
           na_w_in, na_bias, na_w_out):
    p = dict(locals())
    cond = jnp.zeros((N_COND, D_MODEL), F32).at[0].set(c_ctx).at[1:1 + DEC_BATCH].set(c)
    mods = _modulation(cond, w_mod, b_mod)
    x = (x_prompt.reshape(N_PROMPT_TOK, D_MODEL), x_sample.reshape(N_SAMPLE_TOK, D_MODEL))
    new = {n: [] for n in ('ret', 'rwkv', 'dk', 'dv', 'nk', 'nv')}
    for i in range(DEPTH):
        kind, j = i % N_MIXERS, i // N_MIXERS
        if kind == 0:
            x, st = _layer_ret(x, p, mods[i], j)
            new['ret'].append(st)
        elif kind == 1:
            x, st = _layer_rwkv(x, p, mods[i], j)
            new['rwkv'].append(st)
        elif kind == 2:
            x, ck, cv = _layer_diff(x, p, mods[i], j, i)
            new['dk'].append(ck)
            new['dv'].append(cv)
        else:
            x, ck, cv = _layer_na(x, p, mods[i], j, final=(i == DEPTH - 1))
            new['nk'].append(ck)
            new['nv'].append(cv)
    if DEPTH % N_MIXERS:
        raise NotImplementedError("the final norm is fused into the last neighbourhood-attention layer")
    cat = lambda xs: xs[0] if len(xs) == 1 else jnp.concatenate(xs, axis=1)
    return (x[0].reshape(BATCH, SEQ, D_MODEL), x[1].reshape(DEC_BATCH, DEC_SEQ, D_MODEL),
            cat(new['ret']), cat(new['rwkv']), cat(new['dk']), cat(new['dv']), cat(new['nk']), cat(new['nv']))
```

```python
import functools
import math

import jax
import jax.numpy as jnp
from jax import lax
from jax.experimental import pallas as pl
from jax.experimental.pallas import tpu as pltpu

F32 = jnp.float32
BF16 = jnp.bfloat16

D_MODEL = 1024
BATCH = 32
SEQ = 256
DEPTH = 4
N_MIXERS = 4
DEC_BATCH = 2
DEC_SEQ = 1024
PAST_LEN = 256
GRID_W = 64

RET_HEADS = 4
RET_DK = 256
RET_DV = 512
RET_QK = 1024
RET_V = 2048

RWKV_HD = 64
RWKV_HEADS = 16
RWKV_RANK = 64

DIFF_HEADS = 8
DIFF_HD = 64

NA_HEADS = 16
NA_HD = 64
NA_WIN_R = 8
NA_WIN_C = 16

ROPE_BASE = 10000.0
EPS = 1e-6
GN_EPS = 1e-5

N_PROMPT_TOK = BATCH * SEQ
N_SAMPLE_TOK = DEC_BATCH * DEC_SEQ
N_TOK = N_PROMPT_TOK + N_SAMPLE_TOK
N_COND = 8

MIX_DTYPE = BF16
LANES = 128
VMEM_LIMIT = 56 * 2 ** 20


def _params(*sem):
    return pltpu.CompilerParams(dimension_semantics=sem, vmem_limit_bytes=VMEM_LIMIT)


def _cond_of_tile(i, tm):
    npt = N_PROMPT_TOK // tm
    return jnp.where(i < npt, 0, 1 + (i - npt) // (DEC_SEQ // tm))


def _sigmoid(x):
    return 1.0 / (1.0 + jnp.exp(-x))


def _silu(x):
    return x * _sigmoid(x)


def _dot(a, b):
    return jnp.dot(a, b, preferred_element_type=F32)


def _dot_nt(a, b):
    return lax.dot_general(a, b, (((1,), (1,)), ((), ())), preferred_element_type=F32)


def _dot_tn(a, b):
    return lax.dot_general(a, b, (((0,), (0,)), ((), ())), preferred_element_type=F32)


def _mod_kernel(c_ref, w_ref, b_ref, o_ref):
    s = _silu(c_ref[...])
    o_ref[0] = jnp.dot(s, w_ref[0], precision=lax.Precision.HIGHEST, preferred_element_type=F32) + b_ref[0]


def _modulation(cond, w_mod, b_mod):
    tn = D_MODEL
    out = pl.pallas_call(
        _mod_kernel,
        grid=(DEPTH, 3 * D_MODEL // tn),
        in_specs=[
            pl.BlockSpec((N_COND, D_MODEL), lambda l, j: (0, 0)),
            pl.BlockSpec((1, D_MODEL, tn), lambda l, j: (l, 0, j)),
            pl.BlockSpec((1, 1, tn), lambda l, j: (l, 0, j)),
        ],
        out_specs=pl.BlockSpec((1, N_COND, tn), lambda l, j: (l, 0, j)),
        out_shape=jax.ShapeDtypeStruct((DEPTH, N_COND, 3 * D_MODEL), F32),
        compiler_params=_params("arbitrary", "arbitrary"),
        name="modulation",
    )(cond, w_mod, b_mod.reshape(DEPTH, 1, 3 * D_MODEL))
    return out.reshape(DEPTH, N_COND, 3, 1, D_MODEL)


def _norm_mod(x, nw, mod_ref):
    ms = jnp.mean(x * x, axis=-1, keepdims=True)
    y = x * lax.rsqrt(ms + EPS) * nw
    return y * (1.0 + mod_ref[0, 1]) + mod_ref[0, 0]


IN_TM = 1024
IN_TN = 2048


def _x_specs(x, tm, tile_of):
    if not isinstance(x, tuple):
        return [pl.BlockSpec((tm, D_MODEL), lambda *g: (tile_of(*g), 0))], (x,)
    npt = N_PROMPT_TOK // tm
    return [pl.BlockSpec((tm, D_MODEL), lambda *g: (jnp.minimum(tile_of(*g), npt - 1), 0)),
            pl.BlockSpec((tm, D_MODEL), lambda *g: (jnp.maximum(tile_of(*g) - npt, 0), 0))], x


def _read_x(x_refs, tile, tm):
    if len(x_refs) == 1:
        return x_refs[0][...]
    return jnp.where(tile < N_PROMPT_TOK // tm, x_refs[0][...], x_refs[1][...])


def _in_proj_kernel(*refs, n_x):
    x_refs = refs[:n_x]
    nw_ref, mod_ref, w_ref, o_ref, h_ref = refs[n_x:]

    @pl.when(pl.program_id(1) == 0)
    def _():
        x = _read_x(x_refs, pl.program_id(0), IN_TM)
        h_ref[...] = _norm_mod(x, nw_ref[...], mod_ref).astype(BF16)

    o_ref[...] = _dot(h_ref[...], w_ref[...]).astype(o_ref.dtype)


def _in_proj(x, norm_w, mod, w, tn, out_dtype=F32):
    n = w.shape[1]
    w = w.astype(BF16)
    x_specs, xs = _x_specs(x, IN_TM, lambda i, j: i)
    return pl.pallas_call(
        functools.partial(_in_proj_kernel, n_x=len(xs)),
        grid=(N_TOK // IN_TM, n // tn),
        in_specs=x_specs + [
            pl.BlockSpec((1, D_MODEL), lambda i, j: (0, 0)),
            pl.BlockSpec((1, 3, 1, D_MODEL), lambda i, j: (_cond_of_tile(i, IN_TM), 0, 0, 0)),
            pl.BlockSpec((D_MODEL, tn), lambda i, j: (0, j)),
        ],
        out_specs=pl.BlockSpec((IN_TM, tn), lambda i, j: (i, j)),
        out_shape=jax.ShapeDtypeStruct((N_TOK, n), out_dtype),
        scratch_shapes=[pltpu.VMEM((IN_TM, D_MODEL), BF16)],
        compiler_params=_params("arbitrary", "arbitrary"),
        name="in_proj",
    )(*xs, norm_w.reshape(1, D_MODEL), mod, w)


def _in_proj_kv_kernel(x_ref, nw_ref, mod_ref, w_ref, o_ref, ck_ref, cv_ref, h_ref, *, heads, tn):
    i = pl.program_id(0)
    j = pl.program_id(1)

    @pl.when(j == 0)
    def _():
        h_ref[...] = _norm_mod(x_ref[...], nw_ref[...], mod_ref).astype(BF16)

    acc = _dot(h_ref[...], w_ref[...])
    o_ref[...] = acc.astype(o_ref.dtype)
    hd = D_MODEL // heads
    for col, c_ref in ((D_MODEL, ck_ref), (2 * D_MODEL, cv_ref)):
        @pl.when((j == col // tn) & (i < N_PROMPT_TOK // IN_TM))
        def _(c_ref=c_ref, c0=col % tn):
            for s in range(IN_TM // SEQ):
                for h in range(heads):
                    c_ref[s, 0, h] = acc[s * SEQ:(s + 1) * SEQ, c0 + h * hd:c0 + (h + 1) * hd]


def _in_proj_kv(x, norm_w, mod, w, heads):
    n = w.shape[1]
    tn = D_MODEL
    spb = IN_TM // SEQ
    last = N_PROMPT_TOK // IN_TM - 1
    cache = pl.BlockSpec((spb, 1, heads, SEQ, D_MODEL // heads), lambda i, j: (jnp.minimum(i, last), 0, 0, 0, 0))
    cache_shape = jax.ShapeDtypeStruct((BATCH, 1, heads, SEQ, D_MODEL // heads), F32)
    return pl.pallas_call(
        functools.partial(_in_proj_kv_kernel, heads=heads, tn=tn),
        grid=(N_TOK // IN_TM, n // tn),
        in_specs=[
            pl.BlockSpec((IN_TM, D_MODEL), lambda i, j: (i, 0)),
            pl.BlockSpec((1, D_MODEL), lambda i, j: (0, 0)),
            pl.BlockSpec((1, 3, 1, D_MODEL), lambda i, j: (_cond_of_tile(i, IN_TM), 0, 0, 0)),
            pl.BlockSpec((D_MODEL, tn), lambda i, j: (0, j)),
        ],
        out_specs=[pl.BlockSpec((IN_TM, tn), lambda i, j: (i, j)), cache, cache],
        out_shape=[jax.ShapeDtypeStruct((N_TOK, n), BF16), cache_shape, cache_shape],
        scratch_shapes=[pltpu.VMEM((IN_TM, D_MODEL), BF16)],
        compiler_params=_params("arbitrary", "arbitrary"),
        name="in_proj_kv",
    )(x, norm_w.reshape(1, D_MODEL), mod, w.astype(BF16))


OUT_TM_BYTES = 4 * 2 ** 20


def _out_proj_kernel(*refs, n_x, t0, tm, final):
    x_refs = refs[:n_x]
    o_ref, g_ref, w_ref, mod_ref, fw_ref, y_ref, wb_ref = refs[n_x:]

    @pl.when(pl.program_id(0) == 0)
    def _():
        wb_ref[...] = w_ref[...].astype(BF16)

    a = (o_ref[...].astype(F32) * _silu(g_ref[...].astype(F32))).astype(BF16)
    xn = _read_x(x_refs, t0 + pl.program_id(0), tm) + mod_ref[0, 2] * _dot(a, wb_ref[...])
    if final:
        ms = jnp.mean(xn * xn, axis=-1, keepdims=True)
        xn = xn * lax.rsqrt(ms + EPS) * fw_ref[...]
    y_ref[...] = xn


def _out_proj(o, g_arr, g_blk, w, x, mod, final_w, final, rows=(0, N_TOK), g_row0=0):
    k = w.shape[0]
    tm = OUT_TM_BYTES // (4 * k)
    t0 = rows[0] // tm
    g0 = g_row0 // tm
    x_specs, xs = _x_specs(x, tm, lambda i: t0 + i)
    return pl.pallas_call(
        functools.partial(_out_proj_kernel, n_x=len(xs), t0=t0, tm=tm, final=final),
        grid=((rows[1] - rows[0]) // tm,),
        in_specs=x_specs + [
            pl.BlockSpec((tm, k), lambda i: (t0 + i, 0)),
            pl.BlockSpec((tm, k), lambda i: (g0 + t0 + i, g_blk)),
            pl.BlockSpec((k, D_MODEL), lambda i: (0, 0)),
            pl.BlockSpec((1, 3, 1, D_MODEL), lambda i: (_cond_of_tile(t0 + i, tm), 0, 0, 0)),
            pl.BlockSpec((1, D_MODEL), lambda i: (0, 0)),
        ],
        out_specs=pl.BlockSpec((tm, D_MODEL), lambda i: (i, 0)),
        out_shape=jax.ShapeDtypeStruct((rows[1] - rows[0], D_MODEL), F32),
        scratch_shapes=[pltpu.VMEM((k, D_MODEL), BF16)],
        compiler_params=_params("arbitrary"),
        name="out_proj",
    )(*xs, o, g_arr, w, mod, final_w.reshape(1, D_MODEL))


def _rope_tables(d):
    q = d // 4
    t = jnp.arange(DEC_SEQ)
    row = (t // GRID_W).astype(F32)
    col = (t % GRID_W).astype(F32)
    inv = ROPE_BASE ** (-jnp.arange(0, 2 * q, 2, dtype=F32) / (2 * q))
    ar = row[:, None] * inv[None, :]
    ac = col[:, None] * inv[None, :]
    z = jnp.zeros_like(ar)
    cos = jnp.concatenate([jnp.cos(ar), jnp.cos(ar), jnp.cos(ac), jnp.cos(ac)], axis=-1)
    sin_lo = jnp.concatenate([-jnp.sin(ar), z, -jnp.sin(ac), z], axis=-1)
    sin_hi = jnp.concatenate([z, jnp.sin(ar), z, jnp.sin(ac)], axis=-1)
    return cos, sin_lo, sin_hi


def _rope(x, cos, sin_lo, sin_hi, q):
    w = x.shape[-1]
    x_next = pltpu.roll(x, w - q, axis=1)
    x_prev = pltpu.roll(x, q, axis=1)
    return x * cos + x_next * sin_lo + x_prev * sin_hi


RET_QB = 256


RET_SEQS = 8


def _ret_decay(lgf, lgb, qi, seq):
    ii = lax.broadcasted_iota(jnp.int32, (RET_QB, seq), 0) + qi * RET_QB
    jj = lax.broadcasted_iota(jnp.int32, (RET_QB, seq), 1)
    gap = (ii - jj).astype(F32)
    return (jnp.where(gap >= 0, jnp.exp(lgf * jnp.maximum(gap, 0.0)), 0.0)
            + jnp.where(gap <= 0, jnp.exp(lgb * jnp.maximum(-gap, 0.0)), 0.0))


def _head_layer_norm(o, gn):
    oc = o - jnp.mean(o, axis=-1, keepdims=True)
    return oc * lax.rsqrt(jnp.mean(oc * oc, axis=-1, keepdims=True) + GN_EPS) * gn


def _ret_prompt_kernel(lg_ref, q_ref, k_ref, v_ref, gn_ref, o_ref, st_ref, dec_ref):
    h = pl.program_id(0)
    lgf = lg_ref[0, h]
    lgb = lg_ref[1, h]

    @pl.when(pl.program_id(1) == 0)
    def _():
        dec_ref[...] = _ret_decay(lgf, lgb, 0, SEQ)

    dec = dec_ref[...]
    pos = lax.broadcasted_iota(jnp.int32, (SEQ, 1), 0).astype(F32)
    w_fwd = jnp.exp(lgf * (SEQ - 1.0 - pos))
    w_bwd = jnp.exp(lgb * pos)
    rows = [slice(s * SEQ, (s + 1) * SEQ) for s in range(RET_SEQS)]
    k = [k_ref[r, :].astype(F32) * (RET_DK ** -0.5) for r in rows]
    kb = [x.astype(BF16) for x in k]
    vb = [v_ref[r, :].astype(BF16) for r in rows]
    s = [_dot_nt(q_ref[r, :].astype(BF16), y) for r, y in zip(rows, kb)]
    o = [_dot((x * dec).astype(BF16), y) for x, y in zip(s, vb)]
    s_fwd = [_dot_tn((x * w_fwd).astype(BF16), y) for x, y in zip(k, vb)]
    s_bwd = [_dot_tn((x * w_bwd).astype(BF16), y) for x, y in zip(k, vb)]
    gn = gn_ref[...]
    for i, r in enumerate(rows):
        o_ref[r, :] = _head_layer_norm(o[i], gn).astype(o_ref.dtype)
        st_ref[i, 0, 0, 0] = s_fwd[i]
        st_ref[i, 0, 1, 0] = s_bwd[i]


def _ret_latent_kernel(lg_ref, q_ref, k_ref, v_ref, gn_ref, cos_ref, slo_ref, shi_ref, s0_ref, _prev_ref, o_ref,
                       dec_ref):
    seq = DEC_SEQ
    h = pl.program_id(0)
    lgf = lg_ref[0, h]
    lgb = lg_ref[1, h]

    @pl.when(pl.program_id(1) == 0)
    def _():
        for qi in range(seq // RET_QB):
            dec_ref[qi] = _ret_decay(lgf, lgb, qi, seq)

    q = _rope(q_ref[...].astype(F32), cos_ref[...], slo_ref[...], shi_ref[...], RET_DK // 4)
    k = _rope(k_ref[...].astype(F32), cos_ref[...], slo_ref[...], shi_ref[...], RET_DK // 4)
    kb = (k * (RET_DK ** -0.5)).astype(BF16)
    vb = v_ref[...].astype(BF16)
    gn = gn_ref[...]
    for qi in range(seq // RET_QB):
        qblk = q[qi * RET_QB:(qi + 1) * RET_QB]
        s = _dot_nt(qblk.astype(BF16), kb)
        o = _dot((s * dec_ref[qi]).astype(BF16), vb)
        pos = (lax.broadcasted_iota(jnp.int32, (RET_QB, 1), 0) + qi * RET_QB).astype(F32)
        qf = qblk * jnp.exp(lgf * (pos + 1.0))
        qr = qblk * jnp.exp(lgb * (seq - pos))
        o = o + _dot(qf.astype(BF16), s0_ref[0, 0, 0, 0].astype(BF16))
        o = o + _dot(qr.astype(BF16), s0_ref[0, 0, 1, 0].astype(BF16))
        o_ref[qi * RET_QB:(qi + 1) * RET_QB, :] = _head_layer_norm(o, gn).astype(o_ref.dtype)


def _retention(p, log_g, gn_w, state_ret, j):
    smem = pl.BlockSpec(memory_space=pltpu.SMEM)
    gn = gn_w.reshape(1, RET_V)
    kq = RET_QK // RET_DK
    rows = RET_SEQS * SEQ
    o_p, st = pl.pallas_call(
        _ret_prompt_kernel,
        grid=(RET_HEADS, BATCH // RET_SEQS),
        in_specs=[
            smem,
            pl.BlockSpec((rows, RET_DK), lambda h, b: (b, h)),
            pl.BlockSpec((rows, RET_DK), lambda h, b: (b, kq + h)),
            pl.BlockSpec((rows, RET_DV), lambda h, b: (b, kq + h)),
            pl.BlockSpec((1, RET_DV), lambda h, b: (0, h)),
        ],
        out_specs=[
            pl.BlockSpec((rows, RET_DV), lambda h, b: (b, h)),
            pl.BlockSpec((RET_SEQS, 1, 2, 1, RET_DK, RET_DV), lambda h, b: (b, 0, 0, h, 0, 0)),
        ],
        out_shape=[
            jax.ShapeDtypeStruct((N_TOK, RET_V), MIX_DTYPE),
            jax.ShapeDtypeStruct((BATCH, 1, 2, RET_HEADS, RET_DK, RET_DV), F32),
        ],
        scratch_shapes=[pltpu.VMEM((RET_QB, SEQ), F32)],
        compiler_params=_params("arbitrary", "arbitrary"),
        name="retention_prompt",
    )(log_g, p, p, p, gn)
    cos, slo, shi = _rope_tables(RET_DK)
    rb = N_PROMPT_TOK // DEC_SEQ
    full = pl.BlockSpec((DEC_SEQ, RET_DK), lambda h, b: (0, 0))
    o = pl.pallas_call(
        _ret_latent_kernel,
        grid=(RET_HEADS, DEC_BATCH),
        in_specs=[
            smem,
            pl.BlockSpec((DEC_SEQ, RET_DK), lambda h, b: (rb + b, h)),
            pl.BlockSpec((DEC_SEQ, RET_DK), lambda h, b: (rb + b, kq + h)),
            pl.BlockSpec((DEC_SEQ, RET_DV), lambda h, b: (rb + b, kq + h)),
            pl.BlockSpec((1, RET_DV), lambda h, b: (0, h)),
            full, full, full,
            pl.BlockSpec((1, 1, 2, 1, RET_DK, RET_DV), lambda h, b: (b, j, 0, h, 0, 0)),
            pl.BlockSpec(memory_space=pl.ANY),
        ],
        out_specs=pl.BlockSpec((DEC_SEQ, RET_DV), lambda h, b: (rb + b, h)),
        out_shape=jax.ShapeDtypeStruct((N_TOK, RET_V), MIX_DTYPE),
        input_output_aliases={9: 0},
        scratch_shapes=[pltpu.VMEM((DEC_SEQ // RET_QB, RET_QB, DEC_SEQ), F32)],
        compiler_params=_params("arbitrary", "arbitrary"),
        name="retention_latent",
    )(log_g, p, p, p, gn, cos, slo, shi, state_ret, o_p)
    return o, st


def _layer_ret(x, p, mod, j):
    i = N_MIXERS * j + 0
    proj = _in_proj(x, p['norm_w'][i], mod, p['ret_w_in'][j], IN_TN, out_dtype=BF16)
    log_g = jax.nn.log_sigmoid(p['ret_decay'][j].astype(F32))
    o, st = _retention(proj, log_g, p['ret_gn'][j], p['state_ret'], j)
    x = _out_proj(o, proj, (2 * RET_QK + RET_V) // RET_V, p['ret_w_out'][j], x, mod, p['final_norm_w'], False)
    return x, st


RW_TM = 512
RW_HALO = 8
RW_C = 64
RW_LOCK = 4
RW_PAIRS = 2
RW_PAIRS_PROMPT = 2


def _rwkv_prep_kernel(x_ref, xp_ref, xn_ref, nw_ref, mod_ref, mu_ref, wa_ref, aa_ref, wb_ref, ab_ref,
                      w0_ref, a0_ref, xm_ref, lw_ref, a_ref):
    i = pl.program_id(0)
    nw = nw_ref[...]
    h = _norm_mod(x_ref[...], nw, mod_ref)
    h_before = _norm_mod(xp_ref[RW_HALO - 1:RW_HALO, :], nw, mod_ref)
    h_after = _norm_mod(xn_ref[0:1, :], nw, mod_ref)
    seq = jnp.where(i < N_PROMPT_TOK // RW_TM, SEQ, DEC_SEQ)
    row = lax.broadcasted_iota(jnp.int32, (RW_TM, 1), 0)
    t = (row + i * RW_TM) & (seq - 1)
    prev = jnp.where(row == 0, h_before, pltpu.roll(h, 1, axis=0))
    nxt = jnp.where(row == RW_TM - 1, h_after, pltpu.roll(h, RW_TM - 1, axis=0))
    prev = jnp.where(t == 0, 0.0, prev)
    nxt = jnp.where(t == seq - 1, 0.0, nxt)
    xx = 0.5 * (prev + nxt) - h
    for n, m in enumerate((0, 2, 3, 5)):
        xm_ref[n] = (h + xx * mu_ref[m:m + 1, :]).astype(BF16)
    xw = (h + xx * mu_ref[1:2, :]).astype(BF16)
    xa = (h + xx * mu_ref[4:5, :]).astype(BF16)
    lw = jnp.tanh(_dot(xw, wa_ref[...])).astype(BF16)
    la = _dot(xa, aa_ref[...]).astype(BF16)
    for dr in range(2):
        wl = w0_ref[dr:dr + 1, :] + _dot(lw, wb_ref[dr])
        lw_ref[dr] = -math.exp(-0.5) * _sigmoid(wl)
        a_ref[dr] = _sigmoid(a0_ref[dr:dr + 1, :] + _dot(la, ab_ref[dr]))


def _rwkv_prep(x, norm_w, mod, mu, wa2, aa2, wb_pad, ab_pad, w0, a0):
    nt = N_TOK // RW_TM
    hb = RW_TM // RW_HALO
    last = N_TOK // RW_HALO - 1
    full2 = lambda shape: pl.BlockSpec(shape, lambda i: (0, 0))
    full3 = lambda shape: pl.BlockSpec(shape, lambda i: (0, 0, 0))
    return pl.pallas_call(
        _rwkv_prep_kernel,
        grid=(nt,),
        in_specs=[
            pl.BlockSpec((RW_TM, D_MODEL), lambda i: (i, 0)),
            pl.BlockSpec((RW_HALO, D_MODEL), lambda i: (jnp.maximum(i * hb - 1, 0), 0)),
            pl.BlockSpec((RW_HALO, D_MODEL), lambda i: (jnp.minimum((i + 1) * hb, last), 0)),
            full2((1, D_MODEL)),
            pl.BlockSpec((1, 3, 1, D_MODEL), lambda i: (_cond_of_tile(i, RW_TM), 0, 0, 0)),
            full2((6, D_MODEL)),
            full2((D_MODEL, 2 * RWKV_RANK)),
            full2((D_MODEL, 2 * RWKV_RANK)),
            full3((2, 2 * RWKV_RANK, D_MODEL)),
            full3((2, 2 * RWKV_RANK, D_MODEL)),
            full2((2, D_MODEL)),
            full2((2, D_MODEL)),
        ],
        out_specs=[
            pl.BlockSpec((4, RW_TM, D_MODEL), lambda i: (0, i, 0)),
            pl.BlockSpec((2, RW_TM, D_MODEL), lambda i: (0, i, 0)),
            pl.BlockSpec((2, RW_TM, D_MODEL), lambda i: (0, i, 0)),
        ],
        out_shape=[
            jax.ShapeDtypeStruct((4, N_TOK, D_MODEL), BF16),
            jax.ShapeDtypeStruct((2, N_TOK, D_MODEL), F32),
            jax.ShapeDtypeStruct((2, N_TOK, D_MODEL), F32),
        ],
        compiler_params=_params("arbitrary"),
        name="rwkv_prep",
    )(x, x, x, norm_w.reshape(1, D_MODEL), mod, mu, wa2, aa2, wb_pad, ab_pad, w0, a0)


def _bmm_kernel(a_ref, w_ref, o_ref):
    o_ref[0] = _dot(a_ref[0], w_ref[...])


def _rwkv_rkvg(xm, w):
    tm = 1024
    return pl.pallas_call(
        _bmm_kernel,
        grid=(4, N_TOK // tm),
        in_specs=[
            pl.BlockSpec((1, tm, D_MODEL), lambda n, i: (n, i, 0)),
            pl.BlockSpec((D_MODEL, D_MODEL), lambda n, i: (0, n)),
        ],
        out_specs=pl.BlockSpec((1, tm, D_MODEL), lambda n, i: (n, i, 0)),
        out_shape=jax.ShapeDtypeStruct((4, N_TOK, D_MODEL), F32),
        compiler_params=_params("arbitrary", "arbitrary"),
        name="rwkv_rkvg",
    )(xm, w)


def _head_sum(x, first):
    s0 = jnp.sum(jnp.where(first, x, 0.0), axis=-1, keepdims=True)
    s1 = jnp.sum(jnp.where(first, 0.0, x), axis=-1, keepdims=True)
    return jnp.where(first, s0, s1)


def _stack_heads(x, first):
    xb = x.astype(BF16)
    zero = jnp.zeros_like(xb)
    return jnp.concatenate([jnp.where(first, xb, zero), jnp.where(first, zero, xb)], axis=0)


def _split3(x):
    hi = x.astype(BF16)
    r1 = x - hi.astype(F32)
    mid = r1.astype(BF16)
    return hi, mid, (r1 - mid.astype(F32)).astype(BF16)


def _rwkv_kernel(*refs, seq, zero_init, pg, np2):
    n_in = 7 if zero_init else 9
    rkv_ref, lw_ref, a_ref, kkp_ref, kap_ref, rkp_ref, gn_ref = refs[:7]
    if zero_init:
        o_ref, st_ref = refs[n_in:n_in + 2]
        scr = refs[n_in + 2:]
    else:
        s0_ref = refs[7]
        o_ref = refs[n_in]
        scr = refs[n_in + 1:]
    kk_scr, cum_scr, bon_scr, y_scr, s_scr, tar_scr, lrb_scr, b2_scr, w2_scr, yl_scr, kv_scr, pc_scr = scr
    c_len = RW_C
    n_ch = seq // c_len
    rows2 = 2 * c_len
    grp = pl.program_id(1)
    defer = np2 > pg
    base = grp * pg if defer else 0
    first = _first_half_lanes()

    rr = lax.broadcasted_iota(jnp.int32, (rows2, rows2), 0)
    cc = lax.broadcasted_iota(jnp.int32, (rows2, rows2), 1)
    eye = (rr == cc).astype(F32)

    def same(shift):
        return (rr >> shift) == (cc >> shift)

    head = same(6)
    strict = (head & (cc < rr), head & (cc > rr))
    incl = (head & (cc <= rr), head & (cc >= rr))
    last = (c_len - 1, 0)
    head_ones = head.astype(BF16)

    cs_rows = min(seq, 256)
    tr = lax.broadcasted_iota(jnp.int32, (cs_rows, cs_rows), 0)
    tc = lax.broadcasted_iota(jnp.int32, (cs_rows, cs_rows), 1)
    chunk = (tr >> 6) == (tc >> 6)
    tri = ((chunk & (tc <= tr)).astype(BF16), (chunk & (tc >= tr)).astype(BF16))
    for p in range(pg):
        ln = slice(p * LANES, (p + 1) * LANES)
        kk = rkv_ref[1, :, ln] * kkp_ref[:, ln]
        kk_scr[p] = kk * lax.rsqrt(jnp.maximum(_head_sum(kk * kk, first), 1e-12))

    def bonus_terms():
        for p in range(pg):
            ln = slice(p * LANES, (p + 1) * LANES)
            r = rkv_ref[0, :, ln]
            k = rkv_ref[1, :, ln]
            bonus = None
            for dr in range(2):
                kd = k * (1.0 + (a_ref[dr, :, ln] - 1.0) * kap_ref[:, ln])
                term = _head_sum(r * kd * rkp_ref[:, ln], first) * rkv_ref[2, :, ln]
                bonus = term if bonus is None else bonus + term
            bon_scr[base + p] = bonus

    for p in range(0, pg, 2):
        for dr in range(2):
            for r0 in range(0, seq, cs_rows):
                parts = _split3(lw_ref[dr, r0:r0 + cs_rows, p * LANES:(p + 2) * LANES])
                cum = _dot(tri[dr], parts[0]) + _dot(tri[dr], parts[1]) + _dot(tri[dr], parts[2])
                cum_scr[p, dr, r0:r0 + cs_rows, :] = cum[:, :LANES]
                cum_scr[p + 1, dr, r0:r0 + cs_rows, :] = cum[:, LANES:]

    def phase1(chains):
        dirs = [dr for _, dr, _ in chains]
        a2, r2, b2, k2, v2, pc = [], [], [], [], [], []
        for p, dr, c in chains:
            ln = slice(p * LANES, (p + 1) * LANES)
            rw = pl.ds(pl.multiple_of(c * c_len, c_len), c_len)
            a = a_ref[dr, rw, ln]
            k = rkv_ref[1, rw, ln]
            kk_c = kk_scr[p, rw, :]
            cum_c = cum_scr[p, dr, rw, :]
            e_inc = jnp.exp(cum_c)
            e_inv = jnp.exp(-cum_c)
            a2.append(_stack_heads(-kk_c * jnp.exp(cum_c - lw_ref[dr, rw, ln]), first))
            r2.append(_stack_heads(rkv_ref[0, rw, ln] * e_inc, first))
            b2.append(_stack_heads(kk_c * a * e_inv, first))
            k2.append(_stack_heads(k * (1.0 + (a - 1.0) * kap_ref[:, ln]) * e_inv, first))
            v2.append(_stack_heads(rkv_ref[2, rw, ln], first))
            pc.append(e_inc[last[dr]:last[dr] + 1, :])
        g = [_dot_nt(jnp.concatenate([x, y], axis=0), jnp.concatenate([z, w], axis=0))
             for x, y, z, w in zip(a2, r2, b2, k2)]
        l_ab = [jnp.where(strict[dr], x[:rows2, :rows2], 0.0) for dr, x in zip(dirs, g)]
        t = [eye + jnp.where(same(1), x, 0.0) for x in l_ab]
        l_ab16 = [x.astype(BF16) for x in l_ab]
        zero16 = jnp.zeros((rows2, rows2), BF16)
        side = {}
        for shift in range(1, 6):
            sib = same(shift + 1) & ~same(shift)
            tb = [x.astype(BF16) for x in t]
            mid = [_dot(jnp.where(sib, x, zero16), y) for x, y in zip(l_ab16, tb)]
            if shift == 1:
                side['lv'] = [_dot(jnp.where(strict[dr], x[:rows2, rows2:], 0.0).astype(BF16), y)
                              for dr, x, y in zip(dirs, g, v2)]
            elif shift == 2:
                side['yl'] = [_dot(jnp.where(incl[dr], x[rows2:, rows2:], 0.0).astype(BF16), y)
                              for dr, x, y in zip(dirs, g, v2)]
            elif shift == 3:
                side['kv'] = [_dot_tn(x, y) for x, y in zip(v2, k2)]
            t = [x + _dot(y, z.astype(BF16)) for x, y, z in zip(t, tb, mid)]
        tb = [x.astype(BF16) for x in t]
        ta = [_dot(x, y) for x, y in zip(tb, a2)]
        w2 = [_dot(x, y.astype(BF16)) for x, y in zip(tb, side['lv'])]
        for i, (p, dr, c) in enumerate(chains):
            n = ((base + p) * 2 + dr) * n_ch + c
            tar_scr[n, :rows2, :] = ta[i].astype(BF16)
            tar_scr[n, rows2:, :] = r2[i]
            w2_scr[n] = w2[i]
            yl_scr[n] = side['yl'][i]
            kv_scr[n] = side['kv'][i]
            lrb_scr[n] = jnp.where(incl[dr], g[i][rows2:, :rows2], 0.0).astype(BF16)
            b2_scr[n] = b2[i]
            pc_scr[n] = pc[i]

    def body1(cg, carry):
        phase1([(p, dr, cg * RW_LOCK + j) for p in range(pg) for j in range(RW_LOCK) for dr in range(2)])
        return carry

    if n_ch == RW_LOCK:
        body1(0, 0)
    else:
        lax.fori_loop(0, n_ch // RW_LOCK, body1, 0)
    bonus_terms()

    def finish():
        for p in range(np2):
            for dr in range(2):
                if zero_init:
                    s_scr[2 * p + dr] = jnp.zeros((rows2, LANES), F32)
                else:
                    s_scr[2 * p + dr] = s0_ref[0, dr, p]

        def body2(i, carry):
            cs = (i, n_ch - 1 - i)
            ids = [(p, dr) for p in range(np2) for dr in range(2)]
            ns = [(p * 2 + dr) * n_ch + cs[dr] for p, dr in ids]
            x = [_dot_nt(tar_scr[n], s_scr[2 * p + dr].astype(BF16)) for n, (p, dr) in zip(ns, ids)]
            u2 = [(xx[:rows2] + w2_scr[n]).astype(BF16) for n, xx in zip(ns, x)]
            upd = [_dot_tn(u, b2_scr[n]) for n, u in zip(ns, u2)]
            yb = [_dot(lrb_scr[n], u) for n, u in zip(ns, u2)]
            for j, (n, (p, dr)) in enumerate(zip(ns, ids)):
                y2 = x[j][rows2:] + yb[j] + yl_scr[n]
                y_scr[p, dr, pl.ds(pl.multiple_of(cs[dr] * c_len, c_len), c_len), :] = y2[:c_len] + y2[c_len:]
                s_scr[2 * p + dr] = (s_scr[2 * p + dr] + upd[j] + kv_scr[n]) * pc_scr[n]
            return carry

        lax.fori_loop(0, n_ch, body2, 0)

        def head_mean(xs):
            parts = [_split3(x) for x in xs]
            return [(_dot(a, head_ones) + _dot(b, head_ones) + _dot(c, head_ones)) * (1.0 / RWKV_HD)
                    for a, b, c in parts]

        ys = [y_scr[p, 0] + y_scr[p, 1] for p in range(np2)]
        yc = [y - m for y, m in zip(ys, head_mean(ys))]
        var = head_mean([x * x for x in yc])
        for p in range(np2):
            ln = slice(p * LANES, (p + 1) * LANES)
            o_ref[:, ln] = (yc[p] * lax.rsqrt(var[p] + GN_EPS) * gn_ref[:, ln] + bon_scr[p]).astype(o_ref.dtype)
            if zero_init:
                for dr in range(2):
                    s2 = s_scr[2 * p + dr]
                    st_ref[0, 0, dr, 2 * p] = s2[:RWKV_HD, :RWKV_HD]
                    st_ref[0, 0, dr, 2 * p + 1] = s2[RWKV_HD:, RWKV_HD:]

    if defer:
        pl.when(grp == pl.num_programs(1) - 1)(finish)
    else:
        finish()


def _rwkv_mixer(rkvg, lw, a, kkp, kap, rkp, gn, s0_pairs):
    npair = RWKV_HEADS // 2
    r2 = 2 * RW_C

    def scratch(seq, pg, np2):
        n = 2 * np2 * (seq // RW_C)
        return [
            pltpu.VMEM((pg, seq, LANES), F32), pltpu.VMEM((pg, 2, seq, LANES), F32),
            pltpu.VMEM((np2, seq, LANES), F32), pltpu.VMEM((np2, 2, seq, LANES), F32),
            pltpu.VMEM((2 * np2, r2, LANES), F32),
            pltpu.VMEM((n, 2 * r2, LANES), BF16), pltpu.VMEM((n, r2, r2), BF16), pltpu.VMEM((n, r2, LANES), BF16),
            pltpu.VMEM((n, r2, LANES), F32), pltpu.VMEM((n, r2, LANES), F32), pltpu.VMEM((n, r2, LANES), F32),
            pltpu.VMEM((n, 1, LANES), F32),
        ]

    def seq_specs(seq, rb, pg, gn_spec):
        wl = pg * LANES
        par =pl.BlockSpec((1, wl), lambda s, g: (0, g))
        return [
            pl.BlockSpec((3, seq, wl), lambda s, g: (0, rb + s, g)),
            pl.BlockSpec((2, seq, wl), lambda s, g: (0, rb + s, g)),
            pl.BlockSpec((2, seq, wl), lambda s, g: (0, rb + s, g)),
            par, par, par, gn_spec,
        ]

    pars = [u.reshape(1, D_MODEL) for u in (kkp, kap, rkp, gn)]
    pg = RW_PAIRS_PROMPT
    o_p, st = pl.pallas_call(
        functools.partial(_rwkv_kernel, seq=SEQ, zero_init=True, pg=pg, np2=npair),
        grid=(BATCH, npair // pg),
        in_specs=seq_specs(SEQ, 0, pg, pl.BlockSpec((1, D_MODEL), lambda s, g: (0, 0))),
        out_specs=[
            pl.BlockSpec((SEQ, D_MODEL), lambda s, g: (s, 0)),
            pl.BlockSpec((1, 1, 2, RWKV_HEADS, RWKV_HD, RWKV_HD), lambda s, g: (s, 0, 0, 0, 0, 0)),
        ],
        out_shape=[
            jax.ShapeDtypeStruct((N_TOK, D_MODEL), MIX_DTYPE),
            jax.ShapeDtypeStruct((BATCH, 1, 2, RWKV_HEADS, RWKV_HD, RWKV_HD), F32),
        ],
        scratch_shapes=scratch(SEQ, pg, npair),
        compiler_params=_params("arbitrary", "arbitrary"),
        name="rwkv_prompt",
    )(rkvg, lw, a, *pars)
    rb = N_PROMPT_TOK // DEC_SEQ
    pg = RW_PAIRS
    wl = pg * LANES
    o = pl.pallas_call(
        functools.partial(_rwkv_kernel, seq=DEC_SEQ, zero_init=False, pg=pg, np2=pg),
        grid=(DEC_BATCH, npair // pg),
        in_specs=seq_specs(DEC_SEQ, rb, pg, pl.BlockSpec((1, wl), lambda s, g: (0, g))) + [
            pl.BlockSpec((1, 2, pg, LANES, LANES), lambda s, g: (s, 0, g, 0, 0)),
            pl.BlockSpec(memory_space=pl.ANY),
        ],
        out_specs=pl.BlockSpec((DEC_SEQ, wl), lambda s, g: (rb + s, g)),
        out_shape=jax.ShapeDtypeStruct((N_TOK, D_MODEL), MIX_DTYPE),
        input_output_aliases={8: 0},
        scratch_shapes=scratch(DEC_SEQ, pg, pg),
        compiler_params=_params("arbitrary", "arbitrary"),
        name="rwkv_latent",
    )(rkvg, lw, a, *pars, s0_pairs, o_p)
    return o, st


def _state_pairs(s0):
    s = s0.reshape(DEC_BATCH, 2, RWKV_HEADS // 2, 2, RWKV_HD, RWKV_HD)
    z = jnp.zeros_like(s[:, :, :, 0])
    top = jnp.concatenate([s[:, :, :, 0], z], axis=-1)
    bot = jnp.concatenate([z, s[:, :, :, 1]], axis=-1)
    return jnp.concatenate([top, bot], axis=-2)


def _layer_rwkv(x, p, mod, j):
    i = N_MIXERS * j + 1
    wa, wb, aa, ab = p['rwkv_wA'][j], p['rwkv_wB'][j], p['rwkv_aA'][j], p['rwkv_aB'][j]
    z = jnp.zeros_like(wb[0])
    wa2 = jnp.concatenate([wa[0], wa[1]], axis=1).astype(BF16)
    aa2 = jnp.concatenate([aa[0], aa[1]], axis=1).astype(BF16)
    wb_pad = jnp.stack([jnp.concatenate([wb[0], z]), jnp.concatenate([z, wb[1]])]).astype(BF16)
    ab_pad = jnp.stack([jnp.concatenate([ab[0], z]), jnp.concatenate([z, ab[1]])]).astype(BF16)
    xm, lw, a = _rwkv_prep(x, p['norm_w'][i], mod, p['rwkv_mu'][j], wa2, aa2, wb_pad, ab_pad,
                           p['rwkv_w0'][j], p['rwkv_a0'][j])
    rkvg = _rwkv_rkvg(xm, p['rwkv_w_in'][j].astype(BF16))
    o, st = _rwkv_mixer(rkvg, lw, a, p['rwkv_kk'][j], p['rwkv_ka'][j], p['rwkv_rk'][j], p['rwkv_gn'][j],
                          _state_pairs(p['state_rwkv'][:, j]))
    x = _out_proj(o, rkvg.reshape(4 * N_TOK, D_MODEL), 0, p['rwkv_w_out'][j], x, mod, p['final_norm_w'], False,
                  g_row0=3 * N_TOK)
    return x, st


DIFF_W = 2 * DIFF_HD
ATT_QB = 256
DIFF_GROUP = 4
ATT_SEQS = 4


def _first_half_lanes():
    return lax.broadcasted_iota(jnp.int32, (1, LANES), 1) < LANES // 2


def _diff_lambda(lam_ref, lam_init):
    lp = lam_ref[...]
    return (jnp.exp(jnp.sum(lp[0:1] * lp[1:2], keepdims=True))
            - jnp.exp(jnp.sum(lp[2:3] * lp[3:4], keepdims=True)) + lam_init)


def _diff_heads(items, lam, lam_init):
    first = _first_half_lanes()
    scale = DIFF_HD ** -0.5
    sub = [(q, keys, comp) for q, keys, _ in items for comp in range(2)]
    qm = [(jnp.where(first if comp == 0 else ~first, q, 0.0) * scale).astype(BF16) for q, _, comp in sub]
    s = [[_dot_nt(x, kb) for kb, _ in keys] for x, (_, keys, _) in zip(qm, sub)]
    m = [functools.reduce(jnp.maximum, [jnp.max(u, axis=-1, keepdims=True) for u in ss]) for ss in s]
    e = [[jnp.exp(u - mm) for u in ss] for ss, mm in zip(s, m)]
    inv = [1.0 / functools.reduce(lambda x, y: x + y, [jnp.sum(u, axis=-1, keepdims=True) for u in ee]) for ee in e]
    outs = []
    for i, (_, keys, gn) in enumerate(items):
        o = None
        lam_inv = lam * inv[2 * i + 1]
        for n, (_, vb) in enumerate(keys):
            p = e[2 * i][n] * inv[2 * i] - e[2 * i + 1][n] * lam_inv
            part = _dot(p.astype(BF16), vb)
            o = part if o is None else o + part
        outs.append(o)
    return [o * lax.rsqrt(jnp.mean(o * o, axis=-1, keepdims=True) + EPS) * gn * (1.0 - lam_init)
            for o, (_, _, gn) in zip(outs, items)]


def _diff_prompt_kernel(lam_ref, q_ref, k_ref, v_ref, gn_ref, o_ref, *, lam_init):
    lam = _diff_lambda(lam_ref, lam_init)
    for s in range(ATT_SEQS):
        rw = slice(s * SEQ, (s + 1) * SEQ)
        for h0 in range(0, DIFF_HEADS, DIFF_GROUP):
            items = []
            for h in range(h0, h0 + DIFF_GROUP):
                sl = slice(h * DIFF_W, (h + 1) * DIFF_W)
                keys = [(k_ref[rw, sl].astype(BF16), v_ref[rw, sl].astype(BF16))]
                items.append((q_ref[rw, sl], keys, gn_ref[:, sl]))
            for h, o in zip(range(h0, h0 + DIFF_GROUP), _diff_heads(items, lam, lam_init)):
                o_ref[rw, h * DIFF_W:(h + 1) * DIFF_W] = o.astype(o_ref.dtype)


def _diff_latent_kernel(lam_ref, q_ref, k_ref, v_ref, ck_ref, cv_ref, cos_ref, slo_ref, shi_ref, gn_ref,
                        _prev_ref, o_ref, *, lam_init):
    lam = _diff_lambda(lam_ref, lam_init)
    tabs = (cos_ref[...], slo_ref[...], shi_ref[...])
    q = _rope(q_ref[...].astype(F32), *tabs, DIFF_HD // 4)
    k = _rope(k_ref[...].astype(F32), *tabs, DIFF_HD // 4)
    keys = [(k.astype(BF16), v_ref[...].astype(BF16)),
            (ck_ref[0, 0, 0].astype(BF16), cv_ref[0, 0, 0].astype(BF16))]
    gn = gn_ref[...]
    n_blk = DEC_SEQ // ATT_QB
    items = [(q[qi * ATT_QB:(qi + 1) * ATT_QB], keys, gn) for qi in range(n_blk)]
    for qi, o in enumerate(_diff_heads(items, lam, lam_init)):
        o_ref[qi * ATT_QB:(qi + 1) * ATT_QB, :] = o.astype(o_ref.dtype)


def _diff_attention(proj, lam_p, gn_w, cache_k, cache_v, j, lam_init):
    gn = gn_w.reshape(1, D_MODEL)
    lam_spec = pl.BlockSpec((4, DIFF_HD), lambda *_: (0, 0))
    o_p = pl.pallas_call(
        functools.partial(_diff_prompt_kernel, lam_init=lam_init),
        grid=(BATCH // ATT_SEQS,),
        in_specs=[
            lam_spec,
            pl.BlockSpec((ATT_SEQS * SEQ, D_MODEL), lambda b: (b, 0)),
            pl.BlockSpec((ATT_SEQS * SEQ, D_MODEL), lambda b: (b, 1)),
            pl.BlockSpec((ATT_SEQS * SEQ, D_MODEL), lambda b: (b, 2)),
            pl.BlockSpec((1, D_MODEL), lambda b: (0, 0)),
        ],
        out_specs=pl.BlockSpec((ATT_SEQS * SEQ, D_MODEL), lambda b: (b, 0)),
        out_shape=jax.ShapeDtypeStruct((N_TOK, D_MODEL), MIX_DTYPE),
        compiler_params=_params("arbitrary"),
        name="diff_prompt",
    )(lam_p, proj, proj, proj, gn)
    cos, slo, shi = (jnp.concatenate([u, u], axis=-1) for u in _rope_tables(DIFF_HD))
    rb = N_PROMPT_TOK // DEC_SEQ
    nh = DIFF_HEADS
    tab = pl.BlockSpec((DEC_SEQ, DIFF_W), lambda b, h: (0, 0))
    cache = pl.BlockSpec((1, 1, 1, PAST_LEN, DIFF_W), lambda b, h: (b, j, h, 0, 0))
    o = pl.pallas_call(
        functools.partial(_diff_latent_kernel, lam_init=lam_init),
        grid=(DEC_BATCH, nh),
        in_specs=[
            lam_spec,
            pl.BlockSpec((DEC_SEQ, DIFF_W), lambda b, h: (rb + b, h)),
            pl.BlockSpec((DEC_SEQ, DIFF_W), lambda b, h: (rb + b, nh + h)),
            pl.BlockSpec((DEC_SEQ, DIFF_W), lambda b, h: (rb + b, 2 * nh + h)),
            cache, cache, tab, tab, tab,
            pl.BlockSpec((1, DIFF_W), lambda b, h: (0, h)),
            pl.BlockSpec(memory_space=pl.ANY),
        ],
        out_specs=pl.BlockSpec((DEC_SEQ, DIFF_W), lambda b, h: (rb + b, h)),
        out_shape=jax.ShapeDtypeStruct((N_TOK, D_MODEL), MIX_DTYPE),
        input_output_aliases={10: 0},
        compiler_params=_params("arbitrary", "arbitrary"),
        name="diff_latent",
    )(lam_p, proj, proj, proj, cache_k, cache_v, cos, slo, shi, gn, o_p)
    return o


def _layer_diff(x, p, mod, j, i):
    lam_init = 0.8 - 0.6 * math.exp(-0.3 * i)
    proj, new_k, new_v = _in_proj_kv(x, p['norm_w'][i], mod, p['diff_w_in'][j], DIFF_HEADS)
    o = _diff_attention(proj, p['diff_lambda'][j], p['diff_gn'][j], p['cache_diff_k'], p['cache_diff_v'], j,
                        lam_init)
    x = _out_proj(o, proj, 3, p['diff_w_out'][j], x, mod, p['final_norm_w'], False)
    return x, new_k, new_v


NA_ROWS = DEC_SEQ // GRID_W
NA_WR = min(NA_WIN_R, NA_ROWS)
NA_LOC = NA_WR * GRID_W
NA_ROW_GROUP = 4
NA_PAIR_GROUP = 2


def _na_prompt_kernel(q_ref, k_ref, v_ref, o_ref):
    first = _first_half_lanes()
    scale = NA_HD ** -0.5
    for rw, p0 in [(slice(n * SEQ, (n + 1) * SEQ), p0) for n in range(ATT_SEQS)
                   for p0 in range(0, NA_HEADS // 2, NA_PAIR_GROUP)]:
        pairs = range(p0, p0 + NA_PAIR_GROUP)
        kb = [k_ref[rw, pr * LANES:(pr + 1) * LANES].astype(BF16) for pr in pairs]
        vb = [v_ref[rw, pr * LANES:(pr + 1) * LANES].astype(BF16) for pr in pairs]
        items = [(i, half) for i in range(NA_PAIR_GROUP) for half in range(2)]
        qm = [(jnp.where(first if half == 0 else ~first, q_ref[rw, (p0 + i) * LANES:(p0 + i + 1) * LANES], 0.0)
               * scale).astype(BF16) for i, half in items]
        s = [_dot_nt(x, kb[i]) for x, (i, _) in zip(qm, items)]
        e = [jnp.exp(x - jnp.max(x, axis=-1, keepdims=True)) for x in s]
        inv = [1.0 / jnp.sum(x, axis=-1, keepdims=True) for x in e]
        outs = [_dot(x.astype(BF16), vb[i]) * z for x, z, (i, _) in zip(e, inv, items)]
        for i in range(NA_PAIR_GROUP):
            o_ref[rw, (p0 + i) * LANES:(p0 + i + 1) * LANES] = (
                jnp.where(first, outs[2 * i], outs[2 * i + 1]).astype(o_ref.dtype))


def _na_latent_kernel(q_ref, k_ref, v_ref, kc_ref, vc_ref, tab_ref, _prev_ref, o_ref):
    first = _first_half_lanes()
    scale = NA_HD ** -0.5
    kb = k_ref[...].astype(BF16)
    vb = v_ref[...].astype(BF16)
    kcb = kc_ref[0, 0].astype(BF16)
    vcb = vc_ref[0, 0].astype(BF16)
    qcol = lax.broadcasted_iota(jnp.int32, (GRID_W, NA_LOC), 0)
    kcol = lax.broadcasted_iota(jnp.int32, (GRID_W, NA_LOC), 1) & (GRID_W - 1)
    cstart = jnp.clip(qcol - NA_WIN_C // 2, 0, GRID_W - NA_WIN_C)
    col_ok = (kcol >= cstart) & (kcol < cstart + NA_WIN_C)
    def bias_of(r, rs, half):
        parts = []
        for w in range(0, NA_WR, 2):
            src = jnp.broadcast_to(tab_ref[half, rs + w - r + NA_WIN_R - 1], (GRID_W, LANES))
            parts.append(pltpu.roll(src, LANES - (NA_WIN_C - 1), axis=1, stride=1, stride_axis=0))
        return jnp.concatenate(parts, axis=1)

    for r0 in range(0, NA_ROWS, NA_ROW_GROUP):
        items = [(r, min(max(r - NA_WR // 2, 0), NA_ROWS - NA_WR), half)
                 for r in range(r0, r0 + NA_ROW_GROUP) for half in range(2)]
        qm = [(jnp.where(first if half == 0 else ~first, q_ref[r * GRID_W:(r + 1) * GRID_W, :], 0.0)
               * scale).astype(BF16) for r, _, half in items]
        s_loc = [_dot_nt(x, kb[rs * GRID_W:(rs + NA_WR) * GRID_W]) for x, (_, rs, _) in zip(qm, items)]
        s_ctx = [_dot_nt(x, kcb) for x in qm]
        s_loc = [jnp.where(col_ok, x + bias_of(*it), -jnp.inf) for x, it in zip(s_loc, items)]
        m = [jnp.maximum(jnp.max(x, axis=-1, keepdims=True), jnp.max(y, axis=-1, keepdims=True))
             for x, y in zip(s_loc, s_ctx)]
        e_loc = [jnp.exp(x - mm) for x, mm in zip(s_loc, m)]
        e_ctx = [jnp.exp(x - mm) for x, mm in zip(s_ctx, m)]
        inv = [1.0 / (jnp.sum(x, axis=-1, keepdims=True) + jnp.sum(y, axis=-1, keepdims=True))
               for x, y in zip(e_loc, e_ctx)]
        pv = [_dot(x.astype(BF16), vb[rs * GRID_W:(rs + NA_WR) * GRID_W]) for x, (_, rs, _) in zip(e_loc, items)]
        pc = [_dot(x.astype(BF16), vcb) for x in e_ctx]
        outs = [(x + y) * z for x, y, z in zip(pv, pc, inv)]
        for n in range(0, len(items), 2):
            r = items[n][0]
            o_ref[r * GRID_W:(r + 1) * GRID_W, :] = jnp.where(first, outs[n], outs[n + 1]).astype(o_ref.dtype)


def _na_bias_pairs(table):
    t = table.astype(F32)
    nc = 2 * NA_WIN_C - 1
    z = jnp.zeros(t[:, :-1].shape[:2] + (GRID_W - nc,), F32)
    return jnp.concatenate([t[:, :-1], z, t[:, 1:], z], axis=-1)[:, :, None, :]


def _pair_heads(cache):
    c = cache.reshape(DEC_BATCH, NA_HEADS // 2, 2, PAST_LEN, NA_HD)
    return c.transpose(0, 1, 3, 2, 4).reshape(DEC_BATCH, NA_HEADS // 2, PAST_LEN, LANES)


def _na_attention(proj, bias_table, cache_k, cache_v):
    o_p = pl.pallas_call(
        _na_prompt_kernel,
        grid=(BATCH // ATT_SEQS,),
        in_specs=[
            pl.BlockSpec((ATT_SEQS * SEQ, D_MODEL), lambda b: (b, 0)),
            pl.BlockSpec((ATT_SEQS * SEQ, D_MODEL), lambda b: (b, 1)),
            pl.BlockSpec((ATT_SEQS * SEQ, D_MODEL), lambda b: (b, 2)),
        ],
        out_specs=pl.BlockSpec((ATT_SEQS * SEQ, D_MODEL), lambda b: (b, 0)),
        out_shape=jax.ShapeDtypeStruct((N_TOK, D_MODEL), MIX_DTYPE),
        compiler_params=_params("arbitrary"),
        name="na_prompt",
    )(proj, proj, proj)
    rb = N_PROMPT_TOK // DEC_SEQ
    npair = NA_HEADS // 2
    cache = pl.BlockSpec((1, 1, PAST_LEN, LANES), lambda pr, b: (b, pr, 0, 0))
    o = pl.pallas_call(
        _na_latent_kernel,
        grid=(npair, DEC_BATCH),
        in_specs=[
            pl.BlockSpec((DEC_SEQ, LANES), lambda pr, b: (rb + b, pr)),
            pl.BlockSpec((DEC_SEQ, LANES), lambda pr, b: (rb + b, npair + pr)),
            pl.BlockSpec((DEC_SEQ, LANES), lambda pr, b: (rb + b, 2 * npair + pr)),
            cache, cache,
            pl.BlockSpec((2, 2 * NA_WIN_R - 2, 1, LANES), lambda pr, b: (pr, 0, 0, 0)),
            pl.BlockSpec(memory_space=pl.ANY),
        ],
        out_specs=pl.BlockSpec((DEC_SEQ, LANES), lambda pr, b: (rb + b, pr)),
        out_shape=jax.ShapeDtypeStruct((N_TOK, D_MODEL), MIX_DTYPE),
        input_output_aliases={6: 0},
        compiler_params=_params("arbitrary", "arbitrary"),
        name="na_latent",
    )(proj, proj, proj, _pair_heads(cache_k), _pair_heads(cache_v), _na_bias_pairs(bias_table), o_p)
    return o


def _layer_na(x, p, mod, j, final):
    i = N_MIXERS * j + 3
    proj, new_k, new_v = _in_proj_kv(x, p['norm_w'][i], mod, p['na_w_in'][j], NA_HEADS)
    o = _na_attention(proj, p['na_bias'][j], p['cache_na_k'][:, j], p['cache_na_v'][:, j])
    args = (o, proj, 3, p['na_w_out'][j], x, mod, p['final_norm_w'])
    if final:
        x = (_out_proj(*args, True, rows=(0, N_PROMPT_TOK)), _out_proj(*args, True, rows=(N_PROMPT_TOK, N_TOK)))
    else:
        x = _out_proj(*args, False)
    return x, new_k, new_v


def kernel(x_prompt, x_sample, state_ret, state_rwkv, cache_diff_k, cache_diff_v, cache_na_k, cache_na_v,
           c, c_ctx, norm_w, w_mod, b_mod, final_norm_w,
           ret_w_in, ret_decay, ret_gn, ret_w_out,
           rwkv_mu, rwkv_w_in, rwkv_w0, rwkv_wA, rwkv_wB, rwkv_a0, rwkv_aA, rwkv_aB,
           rwkv_kk, rwkv_ka, rwkv_rk, rwkv_gn, rwkv_w_out,
           diff_w_in, diff_lambda, diff_gn, diff_w_out,
           na_w_in, na_bias, na_w_out):
    p = dict(locals())
    cond = jnp.zeros((N_COND, D_MODEL), F32).at[0].set(c_ctx).at[1:1 + DEC_BATCH].set(c)
    mods = _modulation(cond, w_mod, b_mod)
    x = (x_prompt.reshape(N_PROMPT_TOK, D_MODEL), x_sample.reshape(N_SAMPLE_TOK, D_MODEL))
    new = {n: [] for n in ('ret', 'rwkv', 'dk', 'dv', 'nk', 'nv')}
    for i in range(DEPTH):
        kind, j = i % N_MIXERS, i // N_MIXERS
        if kind == 0:
            x, st = _layer_ret(x, p, mods[i], j)
            new['ret'].append(st)
        elif kind == 1:
            x, st = _layer_rwkv(x, p, mods[i], j)
            new['rwkv'].append(st)
        elif kind == 2:
            x, ck, cv = _layer_diff(x, p, mods[i], j, i)
            new['dk'].append(ck)
            new['dv'].append(cv)
        else:
            x, ck, cv = _layer_na(x, p, mods[i], j, final=(i == DEPTH - 1))
            new['nk'].append(ck)
            new['nv'].append(cv)
    if DEPTH % N_MIXERS:
        raise NotImplementedError("the final norm is fused into the last neighbourhood-attention layer")
    cat = lambda xs: xs[0] if len(xs) == 1 else jnp.concatenate(xs, axis=1)
    return (x[0].reshape(BATCH, SEQ, D_MODEL), x[1].reshape(DEC_BATCH, DEC_SEQ, D_MODEL),
            cat(new['ret']), cat(new['rwkv']), cat(new['dk']), cat(new['dv']), cat(new['nk']), cat(new['nv']))
```

```python
import functools
import math

import jax
import jax.numpy as jnp
from jax import lax
from jax.experimental import pallas as pl
from jax.experimental.pallas import tpu as pltpu

F32 = jnp.float32
BF16 = jnp.bfloat16

D_MODEL = 1024
BATCH = 32
SEQ = 256
DEPTH = 4
N_MIXERS = 4
DEC_BATCH = 2
DEC_SEQ = 1024
PAST_LEN = 256
GRID_W = 64

RET_HEADS = 4
RET_DK = 256
RET_DV = 512
RET_QK = 1024
RET_V = 2048

RWKV_HD = 64
RWKV_HEADS = 16
RWKV_RANK = 64

DIFF_HEADS = 8
DIFF_HD = 64

NA_HEADS = 16
NA_HD = 64
NA_WIN_R = 8
NA_WIN_C = 16

ROPE_BASE = 10000.0
EPS = 1e-6
GN_EPS = 1e-5

N_PROMPT_TOK = BATCH * SEQ
N_SAMPLE_TOK = DEC_BATCH * DEC_SEQ
N_TOK = N_PROMPT_TOK + N_SAMPLE_TOK
N_COND = 8

MIX_DTYPE = BF16
LANES = 128
VMEM_LIMIT = 56 * 2 ** 20


def _params(*sem):
    return pltpu.CompilerParams(dimension_semantics=sem, vmem_limit_bytes=VMEM_LIMIT)


def _cond_of_tile(i, tm):
    npt = N_PROMPT_TOK // tm
    return jnp.where(i < npt, 0, 1 + (i - npt) // (DEC_SEQ // tm))


def _sigmoid(x):
    return 1.0 / (1.0 + jnp.exp(-x))


def _silu(x):
    return x * _sigmoid(x)


def _dot(a, b):
    return jnp.dot(a, b, preferred_element_type=F32)


def _dot_nt(a, b):
    return lax.dot_general(a, b, (((1,), (1,)), ((), ())), preferred_element_type=F32)


def _dot_tn(a, b):
    return lax.dot_general(a, b, (((0,), (0,)), ((), ())), preferred_element_type=F32)


def _mod_kernel(c_ref, w_ref, b_ref, o_ref):
    s = _silu(c_ref[...])
    o_ref[0] = jnp.dot(s, w_ref[0], precision=lax.Precision.HIGHEST, preferred_element_type=F32) + b_ref[0]


def _modulation(cond, w_mod, b_mod):
    tn = D_MODEL
    out = pl.pallas_call(
        _mod_kernel,
        grid=(DEPTH, 3 * D_MODEL // tn),
        in_specs=[
            pl.BlockSpec((N_COND, D_MODEL), lambda l, j: (0, 0)),
            pl.BlockSpec((1, D_MODEL, tn), lambda l, j: (l, 0, j)),
            pl.BlockSpec((1, 1, tn), lambda l, j: (l, 0, j)),
        ],
        out_specs=pl.BlockSpec((1, N_COND, tn), lambda l, j: (l, 0, j)),
        out_shape=jax.ShapeDtypeStruct((DEPTH, N_COND, 3 * D_MODEL), F32),
        compiler_params=_params("arbitrary", "arbitrary"),
        name="modulation",
    )(cond, w_mod, b_mod.reshape(DEPTH, 1, 3 * D_MODEL))
    return out.reshape(DEPTH, N_COND, 3, 1, D_MODEL)


def _norm_mod(x, nw, mod_ref):
    ms = jnp.mean(x * x, axis=-1, keepdims=True)
    y = x * lax.rsqrt(ms + EPS) * nw
    return y * (1.0 + mod_ref[0, 1]) + mod_ref[0, 0]


IN_TM = 1024
IN_TN = 2048


def _x_specs(x, tm, tile_of):
    if not isinstance(x, tuple):
        return [pl.BlockSpec((tm, D_MODEL), lambda *g: (tile_of(*g), 0))], (x,)
    npt = N_PROMPT_TOK // tm
    return [pl.BlockSpec((tm, D_MODEL), lambda *g: (jnp.minimum(tile_of(*g), npt - 1), 0)),
            pl.BlockSpec((tm, D_MODEL), lambda *g: (jnp.maximum(tile_of(*g) - npt, 0), 0))], x


def _read_x(x_refs, tile, tm):
    if len(x_refs) == 1:
        return x_refs[0][...]
    return jnp.where(tile < N_PROMPT_TOK // tm, x_refs[0][...], x_refs[1][...])


def _in_proj_kernel(*refs, n_x):
    x_refs = refs[:n_x]
    nw_ref, mod_ref, w_ref, o_ref, h_ref = refs[n_x:]

    @pl.when(pl.program_id(1) == 0)
    def _():
        x = _read_x(x_refs, pl.program_id(0), IN_TM)
        h_ref[...] = _norm_mod(x, nw_ref[...], mod_ref).astype(BF16)

    o_ref[...] = _dot(h_ref[...], w_ref[...]).astype(o_ref.dtype)


def _in_proj(x, norm_w, mod, w, tn, out_dtype=F32):
    n = w.shape[1]
    w = w.astype(BF16)
    x_specs, xs = _x_specs(x, IN_TM, lambda i, j: i)
    return pl.pallas_call(
        functools.partial(_in_proj_kernel, n_x=len(xs)),
        grid=(N_TOK // IN_TM, n // tn),
        in_specs=x_specs + [
            pl.BlockSpec((1, D_MODEL), lambda i, j: (0, 0)),
            pl.BlockSpec((1, 3, 1, D_MODEL), lambda i, j: (_cond_of_tile(i, IN_TM), 0, 0, 0)),
            pl.BlockSpec((D_MODEL, tn), lambda i, j: (0, j)),
        ],
        out_specs=pl.BlockSpec((IN_TM, tn), lambda i, j: (i, j)),
        out_shape=jax.ShapeDtypeStruct((N_TOK, n), out_dtype),
        scratch_shapes=[pltpu.VMEM((IN_TM, D_MODEL), BF16)],
        compiler_params=_params("arbitrary", "arbitrary"),
        name="in_proj",
    )(*xs, norm_w.reshape(1, D_MODEL), mod, w)


def _in_proj_kv_kernel(x_ref, nw_ref, mod_ref, w_ref, o_ref, ck_ref, cv_ref, h_ref, *, heads, tn):
    i = pl.program_id(0)
    j = pl.program_id(1)

    @pl.when(j == 0)
    def _():
        h_ref[...] = _norm_mod(x_ref[...], nw_ref[...], mod_ref).astype(BF16)

    acc = _dot(h_ref[...], w_ref[...])
    o_ref[...] = acc.astype(o_ref.dtype)
    hd = D_MODEL // heads
    for col, c_ref in ((D_MODEL, ck_ref), (2 * D_MODEL, cv_ref)):
        @pl.when((j == col // tn) & (i < N_PROMPT_TOK // IN_TM))
        def _(c_ref=c_ref, c0=col % tn):
            for s in range(IN_TM // SEQ):
                for h in range(heads):
                    c_ref[s, 0, h] = acc[s * SEQ:(s + 1) * SEQ, c0 + h * hd:c0 + (h + 1) * hd]


def _in_proj_kv(x, norm_w, mod, w, heads):
    n = w.shape[1]
    tn = D_MODEL
    spb = IN_TM // SEQ
    last = N_PROMPT_TOK // IN_TM - 1
    cache = pl.BlockSpec((spb, 1, heads, SEQ, D_MODEL // heads), lambda i, j: (jnp.minimum(i, last), 0, 0, 0, 0))
    cache_shape = jax.ShapeDtypeStruct((BATCH, 1, heads, SEQ, D_MODEL // heads), F32)
    return pl.pallas_call(
        functools.partial(_in_proj_kv_kernel, heads=heads, tn=tn),
        grid=(N_TOK // IN_TM, n // tn),
        in_specs=[
            pl.BlockSpec((IN_TM, D_MODEL), lambda i, j: (i, 0)),
            pl.BlockSpec((1, D_MODEL), lambda i, j: (0, 0)),
            pl.BlockSpec((1, 3, 1, D_MODEL), lambda i, j: (_cond_of_tile(i, IN_TM), 0, 0, 0)),
            pl.BlockSpec((D_MODEL, tn), lambda i, j: (0, j)),
        ],
        out_specs=[pl.BlockSpec((IN_TM, tn), lambda i, j: (i, j)), cache, cache],
        out_shape=[jax.ShapeDtypeStruct((N_TOK, n), BF16), cache_shape, cache_shape],
        scratch_shapes=[pltpu.VMEM((IN_TM, D_MODEL), BF16)],
        compiler_params=_params("arbitrary", "arbitrary"),
        name="in_proj_kv",
    )(x, norm_w.reshape(1, D_MODEL), mod, w.astype(BF16))


OUT_TM_BYTES = 4 * 2 ** 20


def _out_proj_kernel(*refs, n_x, t0, tm, final):
    x_refs = refs[:n_x]
    o_ref, g_ref, w_ref, mod_ref, fw_ref, y_ref, wb_ref = refs[n_x:]

    @pl.when(pl.program_id(0) == 0)
    def _():
        wb_ref[...] = w_ref[...].astype(BF16)

    a = (o_ref[...].astype(F32) * _silu(g_ref[...].astype(F32))).astype(BF16)
    xn = _read_x(x_refs, t0 + pl.program_id(0), tm) + mod_ref[0, 2] * _dot(a, wb_ref[...])
    if final:
        ms = jnp.mean(xn * xn, axis=-1, keepdims=True)
        xn = xn * lax.rsqrt(ms + EPS) * fw_ref[...]
    y_ref[...] = xn


def _out_proj(o, g_arr, g_blk, w, x, mod, final_w, final, rows=(0, N_TOK), g_row0=0):
    k = w.shape[0]
    tm = OUT_TM_BYTES // (4 * k)
    t0 = rows[0] // tm
    g0 = g_row0 // tm
    x_specs, xs = _x_specs(x, tm, lambda i: t0 + i)
    return pl.pallas_call(
        functools.partial(_out_proj_kernel, n_x=len(xs), t0=t0, tm=tm, final=final),
        grid=((rows[1] - rows[0]) // tm,),
        in_specs=x_specs + [
            pl.BlockSpec((tm, k), lambda i: (t0 + i, 0)),
            pl.BlockSpec((tm, k), lambda i: (g0 + t0 + i, g_blk)),
            pl.BlockSpec((k, D_MODEL), lambda i: (0, 0)),
            pl.BlockSpec((1, 3, 1, D_MODEL), lambda i: (_cond_of_tile(t0 + i, tm), 0, 0, 0)),
            pl.BlockSpec((1, D_MODEL), lambda i: (0, 0)),
        ],
        out_specs=pl.BlockSpec((tm, D_MODEL), lambda i: (i, 0)),
        out_shape=jax.ShapeDtypeStruct((rows[1] - rows[0], D_MODEL), F32),
        scratch_shapes=[pltpu.VMEM((k, D_MODEL), BF16)],
        compiler_params=_params("arbitrary"),
        name="out_proj",
    )(*xs, o, g_arr, w, mod, final_w.reshape(1, D_MODEL))


def _rope_tables(d):
    q = d // 4
    t = jnp.arange(DEC_SEQ)
    row = (t // GRID_W).astype(F32)
    col = (t % GRID_W).astype(F32)
    inv = ROPE_BASE ** (-jnp.arange(0, 2 * q, 2, dtype=F32) / (2 * q))
    ar = row[:, None] * inv[None, :]
    ac = col[:, None] * inv[None, :]
    z = jnp.zeros_like(ar)
    cos = jnp.concatenate([jnp.cos(ar), jnp.cos(ar), jnp.cos(ac), jnp.cos(ac)], axis=-1)
    sin_lo = jnp.concatenate([-jnp.sin(ar), z, -jnp.sin(ac), z], axis=-1)
    sin_hi = jnp.concatenate([z, jnp.sin(ar), z, jnp.sin(ac)], axis=-1)
    return cos, sin_lo, sin_hi


def _rope(x, cos, sin_lo, sin_hi, q):
    w = x.shape[-1]
    x_next = pltpu.roll(x, w - q, axis=1)
    x_prev = pltpu.roll(x, q, axis=1)
    return x * cos + x_next * sin_lo + x_prev * sin_hi


RET_QB = 256


RET_SEQS = 8


def _ret_decay(lgf, lgb, qi, seq):
    ii = lax.broadcasted_iota(jnp.int32, (RET_QB, seq), 0) + qi * RET_QB
    jj = lax.broadcasted_iota(jnp.int32, (RET_QB, seq), 1)
    gap = (ii - jj).astype(F32)
    return (jnp.where(gap >= 0, jnp.exp(lgf * jnp.maximum(gap, 0.0)), 0.0)
            + jnp.where(gap <= 0, jnp.exp(lgb * jnp.maximum(-gap, 0.0)), 0.0))


def _head_layer_norm(o, gn):
    oc = o - jnp.mean(o, axis=-1, keepdims=True)
    return oc * lax.rsqrt(jnp.mean(oc * oc, axis=-1, keepdims=True) + GN_EPS) * gn


def _ret_prompt_kernel(lg_ref, q_ref, k_ref, v_ref, gn_ref, o_ref, st_ref, dec_ref):
    h = pl.program_id(0)
    lgf = lg_ref[0, h]
    lgb = lg_ref[1, h]

    @pl.when(pl.program_id(1) == 0)
    def _():
        dec_ref[...] = _ret_decay(lgf, lgb, 0, SEQ)

    dec = dec_ref[...]
    pos = lax.broadcasted_iota(jnp.int32, (SEQ, 1), 0).astype(F32)
    w_fwd = jnp.exp(lgf * (SEQ - 1.0 - pos))
    w_bwd = jnp.exp(lgb * pos)
    rows = [slice(s * SEQ, (s + 1) * SEQ) for s in range(RET_SEQS)]
    k = [k_ref[r, :].astype(F32) * (RET_DK ** -0.5) for r in rows]
    kb = [x.astype(BF16) for x in k]
    vb = [v_ref[r, :].astype(BF16) for r in rows]
    s = [_dot_nt(q_ref[r, :].astype(BF16), y) for r, y in zip(rows, kb)]
    o = [_dot((x * dec).astype(BF16), y) for x, y in zip(s, vb)]
    s_fwd = [_dot_tn((x * w_fwd).astype(BF16), y) for x, y in zip(k, vb)]
    s_bwd = [_dot_tn((x * w_bwd).astype(BF16), y) for x, y in zip(k, vb)]
    gn = gn_ref[...]
    for i, r in enumerate(rows):
        o_ref[r, :] = _head_layer_norm(o[i], gn).astype(o_ref.dtype)
        st_ref[i, 0, 0, 0] = s_fwd[i]
        st_ref[i, 0, 1, 0] = s_bwd[i]


def _ret_latent_kernel(lg_ref, q_ref, k_ref, v_ref, gn_ref, cos_ref, slo_ref, shi_ref, s0_ref, _prev_ref, o_ref,
                       dec_ref):
    seq = DEC_SEQ
    h = pl.program_id(0)
    lgf = lg_ref[0, h]
    lgb = lg_ref[1, h]

    @pl.when(pl.program_id(1) == 0)
    def _():
        for qi in range(seq // RET_QB):
            dec_ref[qi] = _ret_decay(lgf, lgb, qi, seq)

    q = _rope(q_ref[...].astype(F32), cos_ref[...], slo_ref[...], shi_ref[...], RET_DK // 4)
    k = _rope(k_ref[...].astype(F32), cos_ref[...], slo_ref[...], shi_ref[...], RET_DK // 4)
    kb = (k * (RET_DK ** -0.5)).astype(BF16)
    vb = v_ref[...].astype(BF16)
    gn = gn_ref[...]
    for qi in range(seq // RET_QB):
        qblk = q[qi * RET_QB:(qi + 1) * RET_QB]
        s = _dot_nt(qblk.astype(BF16), kb)
        o = _dot((s * dec_ref[qi]).astype(BF16), vb)
        pos = (lax.broadcasted_iota(jnp.int32, (RET_QB, 1), 0) + qi * RET_QB).astype(F32)
        qf = qblk * jnp.exp(lgf * (pos + 1.0))
        qr = qblk * jnp.exp(lgb * (seq - pos))
        o = o + _dot(qf.astype(BF16), s0_ref[0, 0, 0, 0].astype(BF16))
        o = o + _dot(qr.astype(BF16), s0_ref[0, 0, 1, 0].astype(BF16))
        o_ref[qi * RET_QB:(qi + 1) * RET_QB, :] = _head_layer_norm(o, gn).astype(o_ref.dtype)


def _retention(p, log_g, gn_w, state_ret, j):
    smem = pl.BlockSpec(memory_space=pltpu.SMEM)
    gn = gn_w.reshape(1, RET_V)
    kq = RET_QK // RET_DK
    rows = RET_SEQS * SEQ
    o_p, st = pl.pallas_call(
        _ret_prompt_kernel,
        grid=(RET_HEADS, BATCH // RET_SEQS),
        in_specs=[
            smem,
            pl.BlockSpec((rows, RET_DK), lambda h, b: (b, h)),
            pl.BlockSpec((rows, RET_DK), lambda h, b: (b, kq + h)),
            pl.BlockSpec((rows, RET_DV), lambda h, b: (b, kq + h)),
            pl.BlockSpec((1, RET_DV), lambda h, b: (0, h)),
        ],
        out_specs=[
            pl.BlockSpec((rows, RET_DV), lambda h, b: (b, h)),
            pl.BlockSpec((RET_SEQS, 1, 2, 1, RET_DK, RET_DV), lambda h, b: (b, 0, 0, h, 0, 0)),
        ],
        out_shape=[
            jax.ShapeDtypeStruct((N_TOK, RET_V), MIX_DTYPE),
            jax.ShapeDtypeStruct((BATCH, 1, 2, RET_HEADS, RET_DK, RET_DV), F32),
        ],
        scratch_shapes=[pltpu.VMEM((RET_QB, SEQ), F32)],
        compiler_params=_params("arbitrary", "arbitrary"),
        name="retention_prompt",
    )(log_g, p, p, p, gn)
    cos, slo, shi = _rope_tables(RET_DK)
    rb = N_PROMPT_TOK // DEC_SEQ
    full = pl.BlockSpec((DEC_SEQ, RET_DK), lambda h, b: (0, 0))
    o = pl.pallas_call(
        _ret_latent_kernel,
        grid=(RET_HEADS, DEC_BATCH),
        in_specs=[
            smem,
            pl.BlockSpec((DEC_SEQ, RET_DK), lambda h, b: (rb + b, h)),
            pl.BlockSpec((DEC_SEQ, RET_DK), lambda h, b: (rb + b, kq + h)),
            pl.BlockSpec((DEC_SEQ, RET_DV), lambda h, b: (rb + b, kq + h)),
            pl.BlockSpec((1, RET_DV), lambda h, b: (0, h)),
            full, full, full,
            pl.BlockSpec((1, 1, 2, 1, RET_DK, RET_DV), lambda h, b: (b, j, 0, h, 0, 0)),
            pl.BlockSpec(memory_space=pl.ANY),
        ],
        out_specs=pl.BlockSpec((DEC_SEQ, RET_DV), lambda h, b: (rb + b, h)),
        out_shape=jax.ShapeDtypeStruct((N_TOK, RET_V), MIX_DTYPE),
        input_output_aliases={9: 0},
        scratch_shapes=[pltpu.VMEM((DEC_SEQ // RET_QB, RET_QB, DEC_SEQ), F32)],
        compiler_params=_params("arbitrary", "arbitrary"),
        name="retention_latent",
    )(log_g, p, p, p, gn, cos, slo, shi, state_ret, o_p)
    return o, st


def _layer_ret(x, p, mod, j):
    i = N_MIXERS * j + 0
    proj = _in_proj(x, p['norm_w'][i], mod, p['ret_w_in'][j], IN_TN, out_dtype=BF16)
    log_g = jax.nn.log_sigmoid(p['ret_decay'][j].astype(F32))
    o, st = _retention(proj, log_g, p['ret_gn'][j], p['state_ret'], j)
    x = _out_proj(o, proj, (2 * RET_QK + RET_V) // RET_V, p['ret_w_out'][j], x, mod, p['final_norm_w'], False)
    return x, st


RW_TM = 512
RW_HALO = 8
RW_C = 64
RW_LOCK = 4
RW_PAIRS = 2
RW_PAIRS_PROMPT = 4


def _rwkv_prep_kernel(x_ref, xp_ref, xn_ref, nw_ref, mod_ref, mu_ref, wa_ref, aa_ref, wb_ref, ab_ref,
                      w0_ref, a0_ref, xm_ref, lw_ref, a_ref):
    i = pl.program_id(0)
    nw = nw_ref[...]
    h = _norm_mod(x_ref[...], nw, mod_ref)
    h_before = _norm_mod(xp_ref[RW_HALO - 1:RW_HALO, :], nw, mod_ref)
    h_after = _norm_mod(xn_ref[0:1, :], nw, mod_ref)
    seq = jnp.where(i < N_PROMPT_TOK // RW_TM, SEQ, DEC_SEQ)
    row = lax.broadcasted_iota(jnp.int32, (RW_TM, 1), 0)
    t = (row + i * RW_TM) & (seq - 1)
    prev = jnp.where(row == 0, h_before, pltpu.roll(h, 1, axis=0))
    nxt = jnp.where(row == RW_TM - 1, h_after, pltpu.roll(h, RW_TM - 1, axis=0))
    prev = jnp.where(t == 0, 0.0, prev)
    nxt = jnp.where(t == seq - 1, 0.0, nxt)
    xx = 0.5 * (prev + nxt) - h
    for n, m in enumerate((0, 2, 3, 5)):
        xm_ref[n] = (h + xx * mu_ref[m:m + 1, :]).astype(BF16)
    xw = (h + xx * mu_ref[1:2, :]).astype(BF16)
    xa = (h + xx * mu_ref[4:5, :]).astype(BF16)
    lw = jnp.tanh(_dot(xw, wa_ref[...])).astype(BF16)
    la = _dot(xa, aa_ref[...]).astype(BF16)
    for dr in range(2):
        wl = w0_ref[dr:dr + 1, :] + _dot(lw, wb_ref[dr])
        lw_ref[dr] = -math.exp(-0.5) * _sigmoid(wl)
        a_ref[dr] = _sigmoid(a0_ref[dr:dr + 1, :] + _dot(la, ab_ref[dr]))


def _rwkv_prep(x, norm_w, mod, mu, wa2, aa2, wb_pad, ab_pad, w0, a0):
    nt = N_TOK // RW_TM
    hb = RW_TM // RW_HALO
    last = N_TOK // RW_HALO - 1
    full2 = lambda shape: pl.BlockSpec(shape, lambda i: (0, 0))
    full3 = lambda shape: pl.BlockSpec(shape, lambda i: (0, 0, 0))
    return pl.pallas_call(
        _rwkv_prep_kernel,
        grid=(nt,),
        in_specs=[
            pl.BlockSpec((RW_TM, D_MODEL), lambda i: (i, 0)),
            pl.BlockSpec((RW_HALO, D_MODEL), lambda i: (jnp.maximum(i * hb - 1, 0), 0)),
            pl.BlockSpec((RW_HALO, D_MODEL), lambda i: (jnp.minimum((i + 1) * hb, last), 0)),
            full2((1, D_MODEL)),
            pl.BlockSpec((1, 3, 1, D_MODEL), lambda i: (_cond_of_tile(i, RW_TM), 0, 0, 0)),
            full2((6, D_MODEL)),
            full2((D_MODEL, 2 * RWKV_RANK)),
            full2((D_MODEL, 2 * RWKV_RANK)),
            full3((2, 2 * RWKV_RANK, D_MODEL)),
            full3((2, 2 * RWKV_RANK, D_MODEL)),
            full2((2, D_MODEL)),
            full2((2, D_MODEL)),
        ],
        out_specs=[
            pl.BlockSpec((4, RW_TM, D_MODEL), lambda i: (0, i, 0)),
            pl.BlockSpec((2, RW_TM, D_MODEL), lambda i: (0, i, 0)),
            pl.BlockSpec((2, RW_TM, D_MODEL), lambda i: (0, i, 0)),
        ],
        out_shape=[
            jax.ShapeDtypeStruct((4, N_TOK, D_MODEL), BF16),
            jax.ShapeDtypeStruct((2, N_TOK, D_MODEL), F32),
            jax.ShapeDtypeStruct((2, N_TOK, D_MODEL), F32),
        ],
        compiler_params=_params("arbitrary"),
        name="rwkv_prep",
    )(x, x, x, norm_w.reshape(1, D_MODEL), mod, mu, wa2, aa2, wb_pad, ab_pad, w0, a0)


def _bmm_kernel(a_ref, w_ref, o_ref):
    o_ref[0] = _dot(a_ref[0], w_ref[...])


def _rwkv_rkvg(xm, w):
    tm = 1024
    return pl.pallas_call(
        _bmm_kernel,
        grid=(4, N_TOK // tm),
        in_specs=[
            pl.BlockSpec((1, tm, D_MODEL), lambda n, i: (n, i, 0)),
            pl.BlockSpec((D_MODEL, D_MODEL), lambda n, i: (0, n)),
        ],
        out_specs=pl.BlockSpec((1, tm, D_MODEL), lambda n, i: (n, i, 0)),
        out_shape=jax.ShapeDtypeStruct((4, N_TOK, D_MODEL), F32),
        compiler_params=_params("arbitrary", "arbitrary"),
        name="rwkv_rkvg",
    )(xm, w)


def _head_sum(x, first):
    s0 = jnp.sum(jnp.where(first, x, 0.0), axis=-1, keepdims=True)
    s1 = jnp.sum(jnp.where(first, 0.0, x), axis=-1, keepdims=True)
    return jnp.where(first, s0, s1)


def _stack_heads(x, first):
    xb = x.astype(BF16)
    zero = jnp.zeros_like(xb)
    return jnp.concatenate([jnp.where(first, xb, zero), jnp.where(first, zero, xb)], axis=0)


def _split3(x):
    hi = x.astype(BF16)
    r1 = x - hi.astype(F32)
    mid = r1.astype(BF16)
    return hi, mid, (r1 - mid.astype(F32)).astype(BF16)


def _rwkv_kernel(*refs, seq, zero_init, pg, np2):
    n_in = 7 if zero_init else 9
    rkv_ref, lw_ref, a_ref, kkp_ref, kap_ref, rkp_ref, gn_ref = refs[:7]
    if zero_init:
        o_ref, st_ref = refs[n_in:n_in + 2]
        scr = refs[n_in + 2:]
    else:
        s0_ref = refs[7]
        o_ref = refs[n_in]
        scr = refs[n_in + 1:]
    kk_scr, cum_scr, bon_scr, y_scr, s_scr, tar_scr, lrb_scr, b2_scr, w2_scr, yl_scr, kv_scr, pc_scr = scr
    c_len = RW_C
    n_ch = seq // c_len
    rows2 = 2 * c_len
    grp = pl.program_id(1)
    defer = np2 > pg
    base = grp * pg if defer else 0
    first = _first_half_lanes()

    rr = lax.broadcasted_iota(jnp.int32, (rows2, rows2), 0)
    cc = lax.broadcasted_iota(jnp.int32, (rows2, rows2), 1)
    eye = (rr == cc).astype(F32)

    def same(shift):
        return (rr >> shift) == (cc >> shift)

    head = same(6)
    strict = (head & (cc < rr), head & (cc > rr))
    incl = (head & (cc <= rr), head & (cc >= rr))
    last = (c_len - 1, 0)
    head_ones = head.astype(BF16)

    cs_rows = min(seq, 256)
    tr = lax.broadcasted_iota(jnp.int32, (cs_rows, cs_rows), 0)
    tc = lax.broadcasted_iota(jnp.int32, (cs_rows, cs_rows), 1)
    chunk = (tr >> 6) == (tc >> 6)
    tri = ((chunk & (tc <= tr)).astype(BF16), (chunk & (tc >= tr)).astype(BF16))
    for p in range(pg):
        ln = slice(p * LANES, (p + 1) * LANES)
        kk = rkv_ref[1, :, ln] * kkp_ref[:, ln]
        kk_scr[p] = kk * lax.rsqrt(jnp.maximum(_head_sum(kk * kk, first), 1e-12))

    def bonus_terms():
        for p in range(pg):
            ln = slice(p * LANES, (p + 1) * LANES)
            r = rkv_ref[0, :, ln]
            k = rkv_ref[1, :, ln]
            bonus = None
            for dr in range(2):
                kd = k * (1.0 + (a_ref[dr, :, ln] - 1.0) * kap_ref[:, ln])
                term = _head_sum(r * kd * rkp_ref[:, ln], first) * rkv_ref[2, :, ln]
                bonus = term if bonus is None else bonus + term
            bon_scr[base + p] = bonus

    for p in range(0, pg, 2):
        for dr in range(2):
            for r0 in range(0, seq, cs_rows):
                parts = _split3(lw_ref[dr, r0:r0 + cs_rows, p * LANES:(p + 2) * LANES])
                cum = _dot(tri[dr], parts[0]) + _dot(tri[dr], parts[1]) + _dot(tri[dr], parts[2])
                cum_scr[p, dr, r0:r0 + cs_rows, :] = cum[:, :LANES]
                cum_scr[p + 1, dr, r0:r0 + cs_rows, :] = cum[:, LANES:]

    def phase1(chains):
        dirs = [dr for _, dr, _ in chains]
        a2, r2, b2, k2, v2, pc = [], [], [], [], [], []
        for p, dr, c in chains:
            ln = slice(p * LANES, (p + 1) * LANES)
            rw = pl.ds(pl.multiple_of(c * c_len, c_len), c_len)
            a = a_ref[dr, rw, ln]
            k = rkv_ref[1, rw, ln]
            kk_c = kk_scr[p, rw, :]
            cum_c = cum_scr[p, dr, rw, :]
            e_inc = jnp.exp(cum_c)
            e_inv = jnp.exp(-cum_c)
            a2.append(_stack_heads(-kk_c * jnp.exp(cum_c - lw_ref[dr, rw, ln]), first))
            r2.append(_stack_heads(rkv_ref[0, rw, ln] * e_inc, first))
            b2.append(_stack_heads(kk_c * a * e_inv, first))
            k2.append(_stack_heads(k * (1.0 + (a - 1.0) * kap_ref[:, ln]) * e_inv, first))
            v2.append(_stack_heads(rkv_ref[2, rw, ln], first))
            pc.append(e_inc[last[dr]:last[dr] + 1, :])
        g = [_dot_nt(jnp.concatenate([x, y], axis=0), jnp.concatenate([z, w], axis=0))
             for x, y, z, w in zip(a2, r2, b2, k2)]
        l_ab = [jnp.where(strict[dr], x[:rows2, :rows2], 0.0) for dr, x in zip(dirs, g)]
        t = [eye + jnp.where(same(1), x, 0.0) for x in l_ab]
        l_ab16 = [x.astype(BF16) for x in l_ab]
        zero16 = jnp.zeros((rows2, rows2), BF16)
        side = {}
        for shift in range(1, 6):
            sib = same(shift + 1) & ~same(shift)
            tb = [x.astype(BF16) for x in t]
            mid = [_dot(jnp.where(sib, x, zero16), y) for x, y in zip(l_ab16, tb)]
            if shift == 1:
                side['lv'] = [_dot(jnp.where(strict[dr], x[:rows2, rows2:], 0.0).astype(BF16), y)
                              for dr, x, y in zip(dirs, g, v2)]
            elif shift == 2:
                side['yl'] = [_dot(jnp.where(incl[dr], x[rows2:, rows2:], 0.0).astype(BF16), y)
                              for dr, x, y in zip(dirs, g, v2)]
            elif shift == 3:
                side['kv'] = [_dot_tn(x, y) for x, y in zip(v2, k2)]
            t = [x + _dot(y, z.astype(BF16)) for x, y, z in zip(t, tb, mid)]
        tb = [x.astype(BF16) for x in t]
        ta = [_dot(x, y) for x, y in zip(tb, a2)]
        w2 = [_dot(x, y.astype(BF16)) for x, y in zip(tb, side['lv'])]
        for i, (p, dr, c) in enumerate(chains):
            n = ((base + p) * 2 + dr) * n_ch + c
            tar_scr[n, :rows2, :] = ta[i].astype(BF16)
            tar_scr[n, rows2:, :] = r2[i]
            w2_scr[n] = w2[i]
            yl_scr[n] = side['yl'][i]
            kv_scr[n] = side['kv'][i]
            lrb_scr[n] = jnp.where(incl[dr], g[i][rows2:, :rows2], 0.0).astype(BF16)
            b2_scr[n] = b2[i]
            pc_scr[n] = pc[i]

    def body1(cg, carry):
        phase1([(p, dr, cg * RW_LOCK + j) for p in range(pg) for j in range(RW_LOCK) for dr in range(2)])
        return carry

    if n_ch == RW_LOCK:
        body1(0, 0)
    else:
        lax.fori_loop(0, n_ch // RW_LOCK, body1, 0)
    bonus_terms()

    def finish():
        for p in range(np2):
            for dr in range(2):
                if zero_init:
                    s_scr[2 * p + dr] = jnp.zeros((rows2, LANES), F32)
                else:
                    s_scr[2 * p + dr] = s0_ref[0, dr, p]

        def body2(i, carry):
            cs = (i, n_ch - 1 - i)
            ids = [(p, dr) for p in range(np2) for dr in range(2)]
            ns = [(p * 2 + dr) * n_ch + cs[dr] for p, dr in ids]
            x = [_dot_nt(tar_scr[n], s_scr[2 * p + dr].astype(BF16)) for n, (p, dr) in zip(ns, ids)]
            u2 = [(xx[:rows2] + w2_scr[n]).astype(BF16) for n, xx in zip(ns, x)]
            upd = [_dot_tn(u, b2_scr[n]) for n, u in zip(ns, u2)]
            yb = [_dot(lrb_scr[n], u) for n, u in zip(ns, u2)]
            for j, (n, (p, dr)) in enumerate(zip(ns, ids)):
                y2 = x[j][rows2:] + yb[j] + yl_scr[n]
                y_scr[p, dr, pl.ds(pl.multiple_of(cs[dr] * c_len, c_len), c_len), :] = y2[:c_len] + y2[c_len:]
                s_scr[2 * p + dr] = (s_scr[2 * p + dr] + upd[j] + kv_scr[n]) * pc_scr[n]
            return carry

        lax.fori_loop(0, n_ch, body2, 0)

        def head_mean(xs):
            parts = [_split3(x) for x in xs]
            return [(_dot(a, head_ones) + _dot(b, head_ones) + _dot(c, head_ones)) * (1.0 / RWKV_HD)
                    for a, b, c in parts]

        ys = [y_scr[p, 0] + y_scr[p, 1] for p in range(np2)]
        yc = [y - m for y, m in zip(ys, head_mean(ys))]
        var = head_mean([x * x for x in yc])
        for p in range(np2):
            ln = slice(p * LANES, (p + 1) * LANES)
            o_ref[:, ln] = (yc[p] * lax.rsqrt(var[p] + GN_EPS) * gn_ref[:, ln] + bon_scr[p]).astype(o_ref.dtype)
            if zero_init:
                for dr in range(2):
                    s2 = s_scr[2 * p + dr]
                    st_ref[0, 0, dr, 2 * p] = s2[:RWKV_HD, :RWKV_HD]
                    st_ref[0, 0, dr, 2 * p + 1] = s2[RWKV_HD:, RWKV_HD:]

    if defer:
        pl.when(grp == pl.num_programs(1) - 1)(finish)
    else:
        finish()


def _rwkv_mixer(rkvg, lw, a, kkp, kap, rkp, gn, s0_pairs):
    npair = RWKV_HEADS // 2
    r2 = 2 * RW_C

    def scratch(seq, pg, np2):
        n = 2 * np2 * (seq // RW_C)
        return [
            pltpu.VMEM((pg, seq, LANES), F32), pltpu.VMEM((pg, 2, seq, LANES), F32),
            pltpu.VMEM((np2, seq, LANES), F32), pltpu.VMEM((np2, 2, seq, LANES), F32),
            pltpu.VMEM((2 * np2, r2, LANES), F32),
            pltpu.VMEM((n, 2 * r2, LANES), BF16), pltpu.VMEM((n, r2, r2), BF16), pltpu.VMEM((n, r2, LANES), BF16),
            pltpu.VMEM((n, r2, LANES), F32), pltpu.VMEM((n, r2, LANES), F32), pltpu.VMEM((n, r2, LANES), F32),
            pltpu.VMEM((n, 1, LANES), F32),
        ]

    def seq_specs(seq, rb, pg, gn_spec):
        wl = pg * LANES
        par =pl.BlockSpec((1, wl), lambda s, g: (0, g))
        return [
            pl.BlockSpec((3, seq, wl), lambda s, g: (0, rb + s, g)),
            pl.BlockSpec((2, seq, wl), lambda s, g: (0, rb + s, g)),
            pl.BlockSpec((2, seq, wl), lambda s, g: (0, rb + s, g)),
            par, par, par, gn_spec,
        ]

    pars = [u.reshape(1, D_MODEL) for u in (kkp, kap, rkp, gn)]
    pg = RW_PAIRS_PROMPT
    o_p, st = pl.pallas_call(
        functools.partial(_rwkv_kernel, seq=SEQ, zero_init=True, pg=pg, np2=npair),
        grid=(BATCH, npair // pg),
        in_specs=seq_specs(SEQ, 0, pg, pl.BlockSpec((1, D_MODEL), lambda s, g: (0, 0))),
        out_specs=[
            pl.BlockSpec((SEQ, D_MODEL), lambda s, g: (s, 0)),
            pl.BlockSpec((1, 1, 2, RWKV_HEADS, RWKV_HD, RWKV_HD), lambda s, g: (s, 0, 0, 0, 0, 0)),
        ],
        out_shape=[
            jax.ShapeDtypeStruct((N_TOK, D_MODEL), MIX_DTYPE),
            jax.ShapeDtypeStruct((BATCH, 1, 2, RWKV_HEADS, RWKV_HD, RWKV_HD), F32),
        ],
        scratch_shapes=scratch(SEQ, pg, npair),
        compiler_params=_params("arbitrary", "arbitrary"),
        name="rwkv_prompt",
    )(rkvg, lw, a, *pars)
    rb = N_PROMPT_TOK // DEC_SEQ
    pg = RW_PAIRS
    wl = pg * LANES
    o = pl.pallas_call(
        functools.partial(_rwkv_kernel, seq=DEC_SEQ, zero_init=False, pg=pg, np2=pg),
        grid=(DEC_BATCH, npair // pg),
        in_specs=seq_specs(DEC_SEQ, rb, pg, pl.BlockSpec((1, wl), lambda s, g: (0, g))) + [
            pl.BlockSpec((1, 2, pg, LANES, LANES), lambda s, g: (s, 0, g, 0, 0)),
            pl.BlockSpec(memory_space=pl.ANY),
        ],
        out_specs=pl.BlockSpec((DEC_SEQ, wl), lambda s, g: (rb + s, g)),
        out_shape=jax.ShapeDtypeStruct((N_TOK, D_MODEL), MIX_DTYPE),
        input_output_aliases={8: 0},
        scratch_shapes=scratch(DEC_SEQ, pg, pg),
        compiler_params=_params("arbitrary", "arbitrary"),
        name="rwkv_latent",
    )(rkvg, lw, a, *pars, s0_pairs, o_p)
    return o, st


def _state_pairs(s0):
    s = s0.reshape(DEC_BATCH, 2, RWKV_HEADS // 2, 2, RWKV_HD, RWKV_HD)
    z = jnp.zeros_like(s[:, :, :, 0])
    top = jnp.concatenate([s[:, :, :, 0], z], axis=-1)
    bot = jnp.concatenate([z, s[:, :, :, 1]], axis=-1)
    return jnp.concatenate([top, bot], axis=-2)


def _layer_rwkv(x, p, mod, j):
    i = N_MIXERS * j + 1
    wa, wb, aa, ab = p['rwkv_wA'][j], p['rwkv_wB'][j], p['rwkv_aA'][j], p['rwkv_aB'][j]
    z = jnp.zeros_like(wb[0])
    wa2 = jnp.concatenate([wa[0], wa[1]], axis=1).astype(BF16)
    aa2 = jnp.concatenate([aa[0], aa[1]], axis=1).astype(BF16)
    wb_pad = jnp.stack([jnp.concatenate([wb[0], z]), jnp.concatenate([z, wb[1]])]).astype(BF16)
    ab_pad = jnp.stack([jnp.concatenate([ab[0], z]), jnp.concatenate([z, ab[1]])]).astype(BF16)
    xm, lw, a = _rwkv_prep(x, p['norm_w'][i], mod, p['rwkv_mu'][j], wa2, aa2, wb_pad, ab_pad,
                           p['rwkv_w0'][j], p['rwkv_a0'][j])
    rkvg = _rwkv_rkvg(xm, p['rwkv_w_in'][j].astype(BF16))
    o, st = _rwkv_mixer(rkvg, lw, a, p['rwkv_kk'][j], p['rwkv_ka'][j], p['rwkv_rk'][j], p['rwkv_gn'][j],
                          _state_pairs(p['state_rwkv'][:, j]))
    x = _out_proj(o, rkvg.reshape(4 * N_TOK, D_MODEL), 0, p['rwkv_w_out'][j], x, mod, p['final_norm_w'], False,
                  g_row0=3 * N_TOK)
    return x, st


DIFF_W = 2 * DIFF_HD
ATT_QB = 256
DIFF_GROUP = 4
ATT_SEQS = 8


def _first_half_lanes():
    return lax.broadcasted_iota(jnp.int32, (1, LANES), 1) < LANES // 2


def _diff_lambda(lam_ref, lam_init):
    lp = lam_ref[...]
    return (jnp.exp(jnp.sum(lp[0:1] * lp[1:2], keepdims=True))
            - jnp.exp(jnp.sum(lp[2:3] * lp[3:4], keepdims=True)) + lam_init)


def _diff_heads(items, lam, lam_init):
    first = _first_half_lanes()
    scale = DIFF_HD ** -0.5
    sub = [(q, keys, comp) for q, keys, _ in items for comp in range(2)]
    qm = [(jnp.where(first if comp == 0 else ~first, q, 0.0) * scale).astype(BF16) for q, _, comp in sub]
    s = [[_dot_nt(x, kb) for kb, _ in keys] for x, (_, keys, _) in zip(qm, sub)]
    m = [functools.reduce(jnp.maximum, [jnp.max(u, axis=-1, keepdims=True) for u in ss]) for ss in s]
    e = [[jnp.exp(u - mm) for u in ss] for ss, mm in zip(s, m)]
    inv = [1.0 / functools.reduce(lambda x, y: x + y, [jnp.sum(u, axis=-1, keepdims=True) for u in ee]) for ee in e]
    outs = []
    for i, (_, keys, gn) in enumerate(items):
        o = None
        lam_inv = lam * inv[2 * i + 1]
        for n, (_, vb) in enumerate(keys):
            p = e[2 * i][n] * inv[2 * i] - e[2 * i + 1][n] * lam_inv
            part = _dot(p.astype(BF16), vb)
            o = part if o is None else o + part
        outs.append(o)
    return [o * lax.rsqrt(jnp.mean(o * o, axis=-1, keepdims=True) + EPS) * gn * (1.0 - lam_init)
            for o, (_, _, gn) in zip(outs, items)]


def _diff_prompt_kernel(lam_ref, q_ref, k_ref, v_ref, gn_ref, o_ref, *, lam_init):
    lam = _diff_lambda(lam_ref, lam_init)
    for s in range(ATT_SEQS):
        rw = slice(s * SEQ, (s + 1) * SEQ)
        for h0 in range(0, DIFF_HEADS, DIFF_GROUP):
            items = []
            for h in range(h0, h0 + DIFF_GROUP):
                sl = slice(h * DIFF_W, (h + 1) * DIFF_W)
                keys = [(k_ref[rw, sl].astype(BF16), v_ref[rw, sl].astype(BF16))]
                items.append((q_ref[rw, sl], keys, gn_ref[:, sl]))
            for h, o in zip(range(h0, h0 + DIFF_GROUP), _diff_heads(items, lam, lam_init)):
                o_ref[rw, h * DIFF_W:(h + 1) * DIFF_W] = o.astype(o_ref.dtype)


def _diff_latent_kernel(lam_ref, q_ref, k_ref, v_ref, ck_ref, cv_ref, cos_ref, slo_ref, shi_ref, gn_ref,
                        _prev_ref, o_ref, *, lam_init):
    lam = _diff_lambda(lam_ref, lam_init)
    tabs = (cos_ref[...], slo_ref[...], shi_ref[...])
    q = _rope(q_ref[...].astype(F32), *tabs, DIFF_HD // 4)
    k = _rope(k_ref[...].astype(F32), *tabs, DIFF_HD // 4)
    keys = [(k.astype(BF16), v_ref[...].astype(BF16)),
            (ck_ref[0, 0, 0].astype(BF16), cv_ref[0, 0, 0].astype(BF16))]
    gn = gn_ref[...]
    n_blk = DEC_SEQ // ATT_QB
    items = [(q[qi * ATT_QB:(qi + 1) * ATT_QB], keys, gn) for qi in range(n_blk)]
    for qi, o in enumerate(_diff_heads(items, lam, lam_init)):
        o_ref[qi * ATT_QB:(qi + 1) * ATT_QB, :] = o.astype(o_ref.dtype)


def _diff_attention(proj, lam_p, gn_w, cache_k, cache_v, j, lam_init):
    gn = gn_w.reshape(1, D_MODEL)
    lam_spec = pl.BlockSpec((4, DIFF_HD), lambda *_: (0, 0))
    o_p = pl.pallas_call(
        functools.partial(_diff_prompt_kernel, lam_init=lam_init),
        grid=(BATCH // ATT_SEQS,),
        in_specs=[
            lam_spec,
            pl.BlockSpec((ATT_SEQS * SEQ, D_MODEL), lambda b: (b, 0)),
            pl.BlockSpec((ATT_SEQS * SEQ, D_MODEL), lambda b: (b, 1)),
            pl.BlockSpec((ATT_SEQS * SEQ, D_MODEL), lambda b: (b, 2)),
            pl.BlockSpec((1, D_MODEL), lambda b: (0, 0)),
        ],
        out_specs=pl.BlockSpec((ATT_SEQS * SEQ, D_MODEL), lambda b: (b, 0)),
        out_shape=jax.ShapeDtypeStruct((N_TOK, D_MODEL), MIX_DTYPE),
        compiler_params=_params("arbitrary"),
        name="diff_prompt",
    )(lam_p, proj, proj, proj, gn)
    cos, slo, shi = (jnp.concatenate([u, u], axis=-1) for u in _rope_tables(DIFF_HD))
    rb = N_PROMPT_TOK // DEC_SEQ
    nh = DIFF_HEADS
    tab = pl.BlockSpec((DEC_SEQ, DIFF_W), lambda b, h: (0, 0))
    cache = pl.BlockSpec((1, 1, 1, PAST_LEN, DIFF_W), lambda b, h: (b, j, h, 0, 0))
    o = pl.pallas_call(
        functools.partial(_diff_latent_kernel, lam_init=lam_init),
        grid=(DEC_BATCH, nh),
        in_specs=[
            lam_spec,
            pl.BlockSpec((DEC_SEQ, DIFF_W), lambda b, h: (rb + b, h)),
            pl.BlockSpec((DEC_SEQ, DIFF_W), lambda b, h: (rb + b, nh + h)),
            pl.BlockSpec((DEC_SEQ, DIFF_W), lambda b, h: (rb + b, 2 * nh + h)),
            cache, cache, tab, tab, tab,
            pl.BlockSpec((1, DIFF_W), lambda b, h: (0, h)),
            pl.BlockSpec(memory_space=pl.ANY),
        ],
        out_specs=pl.BlockSpec((DEC_SEQ, DIFF_W), lambda b, h: (rb + b, h)),
        out_shape=jax.ShapeDtypeStruct((N_TOK, D_MODEL), MIX_DTYPE),
        input_output_aliases={10: 0},
        compiler_params=_params("arbitrary", "arbitrary"),
        name="diff_latent",
    )(lam_p, proj, proj, proj, cache_k, cache_v, cos, slo, shi, gn, o_p)
    return o


def _layer_diff(x, p, mod, j, i):
    lam_init = 0.8 - 0.6 * math.exp(-0.3 * i)
    proj, new_k, new_v = _in_proj_kv(x, p['norm_w'][i], mod, p['diff_w_in'][j], DIFF_HEADS)
    o = _diff_attention(proj, p['diff_lambda'][j], p['diff_gn'][j], p['cache_diff_k'], p['cache_diff_v'], j,
                        lam_init)
    x = _out_proj(o, proj, 3, p['diff_w_out'][j], x, mod, p['final_norm_w'], False)
    return x, new_k, new_v


NA_ROWS = DEC_SEQ // GRID_W
NA_WR = min(NA_WIN_R, NA_ROWS)
NA_LOC = NA_WR * GRID_W
NA_ROW_GROUP = 4
NA_PAIR_GROUP = 2


def _na_prompt_kernel(q_ref, k_ref, v_ref, o_ref):
    first = _first_half_lanes()
    scale = NA_HD ** -0.5
    for rw, p0 in [(slice(n * SEQ, (n + 1) * SEQ), p0) for n in range(ATT_SEQS)
                   for p0 in range(0, NA_HEADS // 2, NA_PAIR_GROUP)]:
        pairs = range(p0, p0 + NA_PAIR_GROUP)
        kb = [k_ref[rw, pr * LANES:(pr + 1) * LANES].astype(BF16) for pr in pairs]
        vb = [v_ref[rw, pr * LANES:(pr + 1) * LANES].astype(BF16) for pr in pairs]
        items = [(i, half) for i in range(NA_PAIR_GROUP) for half in range(2)]
        qm = [(jnp.where(first if half == 0 else ~first, q_ref[rw, (p0 + i) * LANES:(p0 + i + 1) * LANES], 0.0)
               * scale).astype(BF16) for i, half in items]
        s = [_dot_nt(x, kb[i]) for x, (i, _) in zip(qm, items)]
        e = [jnp.exp(x - jnp.max(x, axis=-1, keepdims=True)) for x in s]
        inv = [1.0 / jnp.sum(x, axis=-1, keepdims=True) for x in e]
        outs = [_dot(x.astype(BF16), vb[i]) * z for x, z, (i, _) in zip(e, inv, items)]
        for i in range(NA_PAIR_GROUP):
            o_ref[rw, (p0 + i) * LANES:(p0 + i + 1) * LANES] = (
                jnp.where(first, outs[2 * i], outs[2 * i + 1]).astype(o_ref.dtype))


def _na_latent_kernel(q_ref, k_ref, v_ref, kc_ref, vc_ref, tab_ref, _prev_ref, o_ref):
    first = _first_half_lanes()
    scale = NA_HD ** -0.5
    kb = k_ref[...].astype(BF16)
    vb = v_ref[...].astype(BF16)
    kcb = kc_ref[0, 0].astype(BF16)
    vcb = vc_ref[0, 0].astype(BF16)
    qcol = lax.broadcasted_iota(jnp.int32, (GRID_W, NA_LOC), 0)
    kcol = lax.broadcasted_iota(jnp.int32, (GRID_W, NA_LOC), 1) & (GRID_W - 1)
    cstart = jnp.clip(qcol - NA_WIN_C // 2, 0, GRID_W - NA_WIN_C)
    col_ok = (kcol >= cstart) & (kcol < cstart + NA_WIN_C)
    def bias_of(r, rs, half):
        parts = []
        for w in range(0, NA_WR, 2):
            src = jnp.broadcast_to(tab_ref[half, rs + w - r + NA_WIN_R - 1], (GRID_W, LANES))
            parts.append(pltpu.roll(src, LANES - (NA_WIN_C - 1), axis=1, stride=1, stride_axis=0))
        return jnp.concatenate(parts, axis=1)

    for r0 in range(0, NA_ROWS, NA_ROW_GROUP):
        items = [(r, min(max(r - NA_WR // 2, 0), NA_ROWS - NA_WR), half)
                 for r in range(r0, r0 + NA_ROW_GROUP) for half in range(2)]
        qm = [(jnp.where(first if half == 0 else ~first, q_ref[r * GRID_W:(r + 1) * GRID_W, :], 0.0)
               * scale).astype(BF16) for r, _, half in items]
        s_loc = [_dot_nt(x, kb[rs * GRID_W:(rs + NA_WR) * GRID_W]) for x, (_, rs, _) in zip(qm, items)]
        s_ctx = [_dot_nt(x, kcb) for x in qm]
        s_loc = [jnp.where(col_ok, x + bias_of(*it), -jnp.inf) for x, it in zip(s_loc, items)]
        m = [jnp.maximum(jnp.max(x, axis=-1, keepdims=True), jnp.max(y, axis=-1, keepdims=True))
             for x, y in zip(s_loc, s_ctx)]
        e_loc = [jnp.exp(x - mm) for x, mm in zip(s_loc, m)]
        e_ctx = [jnp.exp(x - mm) for x, mm in zip(s_ctx, m)]
        inv = [1.0 / (jnp.sum(x, axis=-1, keepdims=True) + jnp.sum(y, axis=-1, keepdims=True))
               for x, y in zip(e_loc, e_ctx)]
        pv = [_dot(x.astype(BF16), vb[rs * GRID_W:(rs + NA_WR) * GRID_W]) for x, (_, rs, _) in zip(e_loc, items)]
        pc = [_dot(x.astype(BF16), vcb) for x in e_ctx]
        outs = [(x + y) * z for x, y, z in zip(pv, pc, inv)]
        for n in range(0, len(items), 2):
            r = items[n][0]
            o_ref[r * GRID_W:(r + 1) * GRID_W, :] = jnp.where(first, outs[n], outs[n + 1]).astype(o_ref.dtype)


def _na_bias_pairs(table):
    t = table.astype(F32)
    nc = 2 * NA_WIN_C - 1
    z = jnp.zeros(t[:, :-1].shape[:2] + (GRID_W - nc,), F32)
    return jnp.concatenate([t[:, :-1], z, t[:, 1:], z], axis=-1)[:, :, None, :]


def _pair_heads(cache):
    c = cache.reshape(DEC_BATCH, NA_HEADS // 2, 2, PAST_LEN, NA_HD)
    return c.transpose(0, 1, 3, 2, 4).reshape(DEC_BATCH, NA_HEADS // 2, PAST_LEN, LANES)


def _na_attention(proj, bias_table, cache_k, cache_v):
    o_p = pl.pallas_call(
        _na_prompt_kernel,
        grid=(BATCH // ATT_SEQS,),
        in_specs=[
            pl.BlockSpec((ATT_SEQS * SEQ, D_MODEL), lambda b: (b, 0)),
            pl.BlockSpec((ATT_SEQS * SEQ, D_MODEL), lambda b: (b, 1)),
            pl.BlockSpec((ATT_SEQS * SEQ, D_MODEL), lambda b: (b, 2)),
        ],
        out_specs=pl.BlockSpec((ATT_SEQS * SEQ, D_MODEL), lambda b: (b, 0)),
        out_shape=jax.ShapeDtypeStruct((N_TOK, D_MODEL), MIX_DTYPE),
        compiler_params=_params("arbitrary"),
        name="na_prompt",
    )(proj, proj, proj)
    rb = N_PROMPT_TOK // DEC_SEQ
    npair = NA_HEADS // 2
    cache = pl.BlockSpec((1, 1, PAST_LEN, LANES), lambda pr, b: (b, pr, 0, 0))
    o = pl.pallas_call(
        _na_latent_kernel,
        grid=(npair, DEC_BATCH),
        in_specs=[
            pl.BlockSpec((DEC_SEQ, LANES), lambda pr, b: (rb + b, pr)),
            pl.BlockSpec((DEC_SEQ, LANES), lambda pr, b: (rb + b, npair + pr)),
            pl.BlockSpec((DEC_SEQ, LANES), lambda pr, b: (rb + b, 2 * npair + pr)),
            cache, cache,
            pl.BlockSpec((2, 2 * NA_WIN_R - 2, 1, LANES), lambda pr, b: (pr, 0, 0, 0)),
            pl.BlockSpec(memory_space=pl.ANY),
        ],
        out_specs=pl.BlockSpec((DEC_SEQ, LANES), lambda pr, b: (rb + b, pr)),
        out_shape=jax.ShapeDtypeStruct((N_TOK, D_MODEL), MIX_DTYPE),
        input_output_aliases={6: 0},
        compiler_params=_params("arbitrary", "arbitrary"),
        name="na_latent",
    )(proj, proj, proj, _pair_heads(cache_k), _pair_heads(cache_v), _na_bias_pairs(bias_table), o_p)
    return o


def _layer_na(x, p, mod, j, final):
    i = N_MIXERS * j + 3
    proj, new_k, new_v = _in_proj_kv(x, p['norm_w'][i], mod, p['na_w_in'][j], NA_HEADS)
    o = _na_attention(proj, p['na_bias'][j], p['cache_na_k'][:, j], p['cache_na_v'][:, j])
    args = (o, proj, 3, p['na_w_out'][j], x, mod, p['final_norm_w'])
    if final:
        x = (_out_proj(*args, True, rows=(0, N_PROMPT_TOK)), _out_proj(*args, True, rows=(N_PROMPT_TOK, N_TOK)))
    else:
        x = _out_proj(*args, False)
    return x, new_k, new_v


def kernel(x_prompt, x_sample, state_ret, state_rwkv, cache_diff_k, cache_diff_v, cache_na_k, cache_na_v,
           c, c_ctx, norm_w, w_mod, b_mod, final_norm_w,
           ret_w_in, ret_decay, ret_gn, ret_w_out,
           rwkv_mu, rwkv_w_in, rwkv_w0, rwkv_wA, rwkv_wB, rwkv_a0, rwkv_aA, rwkv_aB,
           rwkv_kk, rwkv_ka, rwkv_rk, rwkv_gn, rwkv_w_out,
           diff_w_in, diff_lambda, diff_gn, diff_w_out,
           na_w_in, na_bias, na_w_out):
    p = dict(locals())
    cond = jnp.zeros((N_COND, D_MODEL), F32).at[0].set(c_ctx).at[1:1 + DEC_BATCH].set(c)
    mods = _modulation(cond, w_mod, b_mod)
    x = (x_prompt.reshape(N_PROMPT_TOK, D_MODEL), x_sample.reshape(N_SAMPLE_TOK, D_MODEL))
    new = {n: [] for n in ('ret', 'rwkv', 'dk', 'dv', 'nk', 'nv')}
    for i in range(DEPTH):
        kind, j = i % N_MIXERS, i // N_MIXERS
        if kind == 0:
            x, st = _layer_ret(x, p, mods[i], j)
            new['ret'].append(st)
        elif kind == 1:
            x, st = _layer_rwkv(x, p, mods[i], j)
            new['rwkv'].append(st)
        elif kind == 2:
            x, ck, cv = _layer_diff(x, p, mods[i], j, i)
            new['dk'].append(ck)
            new['dv'].append(cv)
        else:
            x, ck, cv = _layer_na(x, p, mods[i], j, final=(i == DEPTH - 1))
            new['nk'].append(ck)
            new['nv'].append(cv)
    if DEPTH % N_MIXERS:
        raise NotImplementedError("the final norm is fused into the last neighbourhood-attention layer")
    cat = lambda xs: xs[0] if len(xs) == 1 else jnp.concatenate(xs, axis=1)
    return (x[0].reshape(BATCH, SEQ, D_MODEL), x[1].reshape(DEC_BATCH, DEC_SEQ, D_MODEL),
            cat(new['ret']), cat(new['rwkv']), cat(new['dk']), cat(new['dv']), cat(new['nk']), cat(new['nv']))
```
